```python
import math
import jax, jax.numpy as jnp
from jax import lax
import numpy as np

D_MODEL = 1024
BATCH = 8
SEQ = 4096
DEPTH = 4

HEAD_DIM = 128
N_MIX_HEADS = D_MODEL // HEAD_DIM
N_DN_HEADS = N_MIX_HEADS // 2
N_FOX_HEADS = N_MIX_HEADS - N_DN_HEADS
N_SB_HEADS = N_MIX_HEADS
D_DN = N_DN_HEADS * HEAD_DIM
D_FOX = N_FOX_HEADS * HEAD_DIM
CONV_WIDTH = 4
CHUNK = 64
Q_BLOCK = 128
D_FF = 2816
N_EVEN = (DEPTH + 1) // 2
N_ODD = DEPTH // 2
EPS = 1e-6
EVEN_SPLITS = (3 * D_DN, D_DN, N_DN_HEADS, N_DN_HEADS, D_FOX, D_FOX, D_FOX, D_FOX, N_FOX_HEADS)
D_IN_EVEN = 3 * D_DN + D_DN + 2 * N_DN_HEADS + 4 * D_FOX + N_FOX_HEADS
D_IN_ODD = 3 * N_SB_HEADS * HEAD_DIM

kernel_name = 'hybrid_deltanet_fox_stickbreak_macaron'


def _split(t, sizes):
    out, off = [], 0
    for s in sizes:
        out.append(t[..., off:off + s])
        off += s
    return out


def rmsnorm(x, g):
    xf = x.astype(jnp.float32)
    y = xf * lax.rsqrt(jnp.mean(xf * xf, axis=-1, keepdims=True) + EPS)
    return (y * g.astype(jnp.float32)).astype(x.dtype)


def l2norm(t):
    tf = t.astype(jnp.float32)
    return tf * lax.rsqrt(jnp.sum(tf * tf, axis=-1, keepdims=True) + EPS)


def swiglu_ffn(x, w_gu, w_down):
    g, u = jnp.split(x @ w_gu, 2, axis=-1)
    return (jax.nn.silu(g) * u) @ w_down


def split_heads(t, n):
    b, s, _ = t.shape
    return t.reshape(b, s, n, -1).transpose(0, 2, 1, 3)


def merge_heads(t):
    b, n, s, d = t.shape
    return t.transpose(0, 2, 1, 3).reshape(b, s, n * d)


def causal_conv_silu(x, w):
    s = x.shape[1]
    xp = jnp.pad(x, ((0, 0), (CONV_WIDTH - 1, 0), (0, 0)))
    y = sum(xp[:, i:i + s, :] * w[i] for i in range(CONV_WIDTH))
    return jax.nn.silu(y)


def gated_delta_rule(q, k, v, beta, g):
    b, h, s, d = q.shape
    n = s // CHUNK
    q = q * d ** -0.5
    rc = lambda t: t.reshape(b, h, n, CHUNK, *t.shape[3:])
    q, k, v, beta, g = (rc(t) for t in (q, k, v, beta, g))
    gc = jnp.cumsum(g, axis=-1)
    tri_incl = jnp.tril(jnp.ones((CHUNK, CHUNK), bool))
    tri_strict = jnp.tril(jnp.ones((CHUNK, CHUNK), bool), -1)
    decay_mat = jnp.where(tri_incl, jnp.exp(jnp.where(tri_incl, gc[..., :, None] - gc[..., None, :], 0.0)), 0.0)
    k_beta = k * beta[..., None]
    v_beta = v * beta[..., None]
    a = jnp.where(tri_strict, jnp.einsum('bhnid,bhnjd->bhnij', k_beta, k) * decay_mat, 0.0)
    lhs = jnp.eye(CHUNK, dtype=jnp.float32) + a
    rhs = jnp.concatenate([v_beta, k_beta * jnp.exp(gc)[..., None]], axis=-1)
    sol = lax.linalg.triangular_solve(lhs, rhs, left_side=True, lower=True)
    u, w = sol[..., :d], sol[..., d:]
    attn_intra = jnp.einsum('bhnid,bhnjd->bhnij', q, k) * decay_mat
    q_dec = q * jnp.exp(gc)[..., None]
    k_dec = k * jnp.exp(gc[..., -1:] - gc)[..., None]
    g_last = jnp.exp(gc[..., -1])

    def step(state, inp):
        u_c, w_c, qd_c, kd_c, at_c, gl_c = inp
        v_new = u_c - jnp.einsum('bhcd,bhde->bhce', w_c, state)
        o = jnp.einsum('bhcd,bhde->bhce', qd_c, state) + jnp.einsum('bhij,bhje->bhie', at_c, v_new)
        state = state * gl_c[..., None, None] + jnp.einsum('bhcd,bhce->bhde', kd_c, v_new)
        return state, o

    xs = tuple(jnp.moveaxis(t, 2, 0) for t in (u, w, q_dec, k_dec, attn_intra, g_last))
    _, o = lax.scan(step, jnp.zeros((b, h, d, d), jnp.float32), xs)
    return jnp.moveaxis(o, 0, 2).reshape(b, h, s, d)


def forgetting_attention(q, k, v, log_f):
    b, h, s, d = q.shape
    nb = s // Q_BLOCK
    c = jnp.cumsum(log_f, axis=-1)
    qb = q.reshape(b, h, nb, Q_BLOCK, d).transpose(2, 0, 1, 3, 4)
    cb = c.reshape(b, h, nb, Q_BLOCK).transpose(2, 0, 1, 3)
    pos_k = jnp.arange(s)

    def block(args):
        i, q_i, c_i = args
        pos_q = i * Q_BLOCK + jnp.arange(Q_BLOCK)
        logits = jnp.einsum('bhqd,bhkd->bhqk', q_i, k).astype(jnp.float32) * d ** -0.5
        logits = logits + (c_i[..., :, None] - c[..., None, :])
        logits = jnp.where(pos_k[None, :] <= pos_q[:, None], logits, -jnp.inf)
        p = jax.nn.softmax(logits, axis=-1)
        return jnp.einsum('bhqk,bhkd->bhqd', p.astype(v.dtype), v)

    o = lax.map(block, (jnp.arange(nb), qb, cb))
    return o.transpose(1, 2, 0, 3, 4).reshape(b, h, s, d)


def stick_breaking_attention(q, k, v):
    b, h, s, d = q.shape
    nb = s // Q_BLOCK
    qb = q.reshape(b, h, nb, Q_BLOCK, d).transpose(2, 0, 1, 3, 4)
    pos_k = jnp.arange(s)

    def block(args):
        i, q_i = args
        pos_q = i * Q_BLOCK + jnp.arange(Q_BLOCK)
        z = jnp.einsum('bhqd,bhkd->bhqk', q_i, k).astype(jnp.float32) * d ** -0.5
        before = pos_k[None, :] < pos_q[:, None]
        log_1m = jnp.where(before, jax.nn.log_sigmoid(-z), 0.0)
        tail = lax.cumsum(log_1m, axis=3, reverse=True) - log_1m
        a = jnp.where(before, jnp.exp(jax.nn.log_sigmoid(z) + tail), 0.0)
        return jnp.einsum('bhqk,bhkd->bhqd', a.astype(v.dtype), v)

    o = lax.map(block, (jnp.arange(nb), qb))
    return o.transpose(1, 2, 0, 3, 4).reshape(b, h, s, d)


def deltanet_fox_mixer(h, w_in, conv_w, a_log, dt_bias, dn_norm_g, q_norm_g, k_norm_g, f_bias, w_out):
    dn_qkv, dn_gate, dn_b, dn_a, fq, fk, fv, f_gate, f_pre = _split(h @ w_in, EVEN_SPLITS)
    dq, dk, dv = jnp.split(causal_conv_silu(dn_qkv, conv_w), 3, axis=-1)
    dq = l2norm(split_heads(dq, N_DN_HEADS))
    dk = l2norm(split_heads(dk, N_DN_HEADS))
    dv = split_heads(dv, N_DN_HEADS).astype(jnp.float32)
    beta = jax.nn.sigmoid(dn_b.astype(jnp.float32)).transpose(0, 2, 1)
    g = (-jnp.exp(a_log.astype(jnp.float32)) * jax.nn.softplus(dn_a.astype(jnp.float32) + dt_bias.astype(jnp.float32))).transpose(0, 2, 1)
    o_dn = gated_delta_rule(dq, dk, dv, beta, g).astype(h.dtype)
    o_dn = merge_heads(rmsnorm(o_dn, dn_norm_g)) * jax.nn.silu(dn_gate)
    fq = rmsnorm(split_heads(fq, N_FOX_HEADS), q_norm_g)
    fk = rmsnorm(split_heads(fk, N_FOX_HEADS), k_norm_g)
    fv = split_heads(fv, N_FOX_HEADS)
    log_f = jax.nn.log_sigmoid(f_pre.astype(jnp.float32) + f_bias.astype(jnp.float32)).transpose(0, 2, 1)
    o_fox = merge_heads(forgetting_attention(fq, fk, fv, log_f)) * jax.nn.sigmoid(f_gate)
    return jnp.concatenate([o_dn, o_fox], axis=-1) @ w_out


def stick_breaking_mixer(h, w_in, w_out):
    q, k, v = jnp.split(h @ w_in, 3, axis=-1)
    o = stick_breaking_attention(split_heads(q, N_SB_HEADS), split_heads(k, N_SB_HEADS), split_heads(v, N_SB_HEADS))
    return merge_heads(o) @ w_out


def _fwd_setup_inputs(seed: int = 0) -> dict:
    key = jax.random.key(seed)
    ks = jax.random.split(key, 20)
    f32 = jnp.float32

    def w(k, shape, fan_in):
        return jax.random.normal(k, shape, f32) * fan_in ** -0.5

    def gain(k, shape):
        return 1.0 + 0.02 * jax.random.normal(k, shape, f32)

    dt = jnp.exp(jax.random.uniform(ks[8], (N_EVEN, N_DN_HEADS), f32, math.log(1e-3), math.log(1e-1)))
    return {
        'x': jax.random.normal(ks[0], (BATCH, SEQ, D_MODEL), f32),
        'norm_ffn1': gain(ks[1], (DEPTH, D_MODEL)),
        'ffn1_w_gu': w(ks[2], (DEPTH, D_MODEL, 2 * D_FF), D_MODEL),
        'ffn1_w_down': w(ks[3], (DEPTH, D_FF, D_MODEL), D_FF),
        'norm_mix': gain(ks[4], (DEPTH, D_MODEL)),
        'w_in_even': w(ks[5], (N_EVEN, D_MODEL, D_IN_EVEN), D_MODEL),
        'dn_conv_w': w(ks[6], (N_EVEN, CONV_WIDTH, 3 * D_DN), CONV_WIDTH),
        'dn_a_log': jnp.log(jax.random.uniform(ks[7], (N_EVEN, N_DN_HEADS), f32, 1.0, 16.0)),
        'dn_dt_bias': dt + jnp.log(-jnp.expm1(-dt)),
        'dn_norm_g': gain(ks[9], (N_EVEN, HEAD_DIM)),
        'fox_q_norm_g': gain(ks[10], (N_EVEN, HEAD_DIM)),
        'fox_k_norm_g': gain(ks[11], (N_EVEN, HEAD_DIM)),
        'fox_f_bias': jax.random.uniform(ks[12], (N_EVEN, N_FOX_HEADS), f32, 1.0, 4.0),
        'w_out_even': w(ks[13], (N_EVEN, D_DN + D_FOX, D_MODEL), D_DN + D_FOX),
        'w_in_odd': w(ks[14], (N_ODD, D_MODEL, D_IN_ODD), D_MODEL),
        'w_out_odd': w(ks[15], (N_ODD, N_SB_HEADS * HEAD_DIM, D_MODEL), N_SB_HEADS * HEAD_DIM),
        'norm_ffn2': gain(ks[16], (DEPTH, D_MODEL)),
        'ffn2_w_gu': w(ks[17], (DEPTH, D_MODEL, 2 * D_FF), D_MODEL),
        'ffn2_w_down': w(ks[18], (DEPTH, D_FF, D_MODEL), D_FF),
    }


def _fwd_reference(x, norm_ffn1, ffn1_w_gu, ffn1_w_down, norm_mix, w_in_even, dn_conv_w, dn_a_log,
              dn_dt_bias, dn_norm_g, fox_q_norm_g, fox_k_norm_g, fox_f_bias, w_out_even,
              w_in_odd, w_out_odd, norm_ffn2, ffn2_w_gu, ffn2_w_down):
    for l in range(DEPTH):
        x = x + 0.5 * swiglu_ffn(rmsnorm(x, norm_ffn1[l]), ffn1_w_gu[l], ffn1_w_down[l])
        h = rmsnorm(x, norm_mix[l])
        j = l // 2
        if l % 2 == 0:
            x = x + deltanet_fox_mixer(h, w_in_even[j], dn_conv_w[j], dn_a_log[j], dn_dt_bias[j],
                                       dn_norm_g[j], fox_q_norm_g[j], fox_k_norm_g[j], fox_f_bias[j],
                                       w_out_even[j])
        else:
            x = x + stick_breaking_mixer(h, w_in_odd[j], w_out_odd[j])
        x = x + 0.5 * swiglu_ffn(rmsnorm(x, norm_ffn2[l]), ffn2_w_gu[l], ffn2_w_down[l])
    return x


import jax as _jax
import jax.numpy as _jnp

TWIN_FORMAT = 'train_step'
FWD_PARAMS = ['x', 'norm_ffn1', 'ffn1_w_gu', 'ffn1_w_down', 'norm_mix', 'w_in_even', 'dn_conv_w', 'dn_a_log', 'dn_dt_bias', 'dn_norm_g', 'fox_q_norm_g', 'fox_k_norm_g', 'fox_f_bias', 'w_out_even', 'w_in_odd', 'w_out_odd', 'norm_ffn2', 'ffn2_w_gu', 'ffn2_w_down']
TWIN_WEIGHTS = ['norm_ffn1', 'ffn1_w_gu', 'ffn1_w_down', 'norm_mix', 'w_in_even', 'dn_conv_w', 'dn_a_log', 'dn_dt_bias', 'dn_norm_g', 'fox_q_norm_g', 'fox_k_norm_g', 'fox_f_bias', 'w_out_even', 'w_in_odd', 'w_out_odd', 'norm_ffn2', 'ffn2_w_gu', 'ffn2_w_down']
TWIN_DIFF_INPUT = 'x'
TWIN_INPUTS = ['x', 'norm_ffn1', 'ffn1_w_gu', 'ffn1_w_down', 'norm_mix', 'w_in_even', 'dn_conv_w', 'dn_a_log', 'dn_dt_bias', 'dn_norm_g', 'fox_q_norm_g', 'fox_k_norm_g', 'fox_f_bias', 'w_out_even', 'w_in_odd', 'w_out_odd', 'norm_ffn2', 'ffn2_w_gu', 'ffn2_w_down', 'loss_target', 'm_norm_ffn1', 'm_ffn1_w_gu', 'm_ffn1_w_down', 'm_norm_mix', 'm_w_in_even', 'm_dn_conv_w', 'm_dn_a_log', 'm_dn_dt_bias', 'm_dn_norm_g', 'm_fox_q_norm_g', 'm_fox_k_norm_g', 'm_fox_f_bias', 'm_w_out_even', 'm_w_in_odd', 'm_w_out_odd', 'm_norm_ffn2', 'm_ffn2_w_gu', 'm_ffn2_w_down', 'v_norm_ffn1', 'v_ffn1_w_gu', 'v_ffn1_w_down', 'v_norm_mix', 'v_w_in_even', 'v_dn_conv_w', 'v_dn_a_log', 'v_dn_dt_bias', 'v_dn_norm_g', 'v_fox_q_norm_g', 'v_fox_k_norm_g', 'v_fox_f_bias', 'v_w_out_even', 'v_w_in_odd', 'v_w_out_odd', 'v_norm_ffn2', 'v_ffn2_w_gu', 'v_ffn2_w_down']
TWIN_OUTPUTS = ['loss', 'grad_x', 'grad_norm_ffn1', 'grad_ffn1_w_gu', 'grad_ffn1_w_down', 'grad_norm_mix', 'grad_w_in_even', 'grad_dn_conv_w', 'grad_dn_a_log', 'grad_dn_dt_bias', 'grad_dn_norm_g', 'grad_fox_q_norm_g', 'grad_fox_k_norm_g', 'grad_fox_f_bias', 'grad_w_out_even', 'grad_w_in_odd', 'grad_w_out_odd', 'grad_norm_ffn2', 'grad_ffn2_w_gu', 'grad_ffn2_w_down', 'delta_norm_ffn1', 'delta_ffn1_w_gu', 'delta_ffn1_w_down', 'delta_norm_mix', 'delta_w_in_even', 'delta_dn_conv_w', 'delta_dn_a_log', 'delta_dn_dt_bias', 'delta_dn_norm_g', 'delta_fox_q_norm_g', 'delta_fox_k_norm_g', 'delta_fox_f_bias', 'delta_w_out_even', 'delta_w_in_odd', 'delta_w_out_odd', 'delta_norm_ffn2', 'delta_ffn2_w_gu', 'delta_ffn2_w_down', 'new_m_norm_ffn1', 'new_m_ffn1_w_gu', 'new_m_ffn1_w_down', 'new_m_norm_mix', 'new_m_w_in_even', 'new_m_dn_conv_w', 'new_m_dn_a_log', 'new_m_dn_dt_bias', 'new_m_dn_norm_g', 'new_m_fox_q_norm_g', 'new_m_fox_k_norm_g', 'new_m_fox_f_bias', 'new_m_w_out_even', 'new_m_w_in_odd', 'new_m_w_out_odd', 'new_m_norm_ffn2', 'new_m_ffn2_w_gu', 'new_m_ffn2_w_down', 'new_v_norm_ffn1', 'new_v_ffn1_w_gu', 'new_v_ffn1_w_down', 'new_v_norm_mix', 'new_v_w_in_even', 'new_v_dn_conv_w', 'new_v_dn_a_log', 'new_v_dn_dt_bias', 'new_v_dn_norm_g', 'new_v_fox_q_norm_g', 'new_v_fox_k_norm_g', 'new_v_fox_f_bias', 'new_v_w_out_even', 'new_v_w_in_odd', 'new_v_w_out_odd', 'new_v_norm_ffn2', 'new_v_ffn2_w_gu', 'new_v_ffn2_w_down']
TWIN_LEAF_KINDS = {'loss': 'loss', 'grad_x': 'grad_x', 'grad_norm_ffn1': 'grad_w', 'grad_ffn1_w_gu': 'grad_w', 'grad_ffn1_w_down': 'grad_w', 'grad_norm_mix': 'grad_w', 'grad_w_in_even': 'grad_w', 'grad_dn_conv_w': 'grad_w', 'grad_dn_a_log': 'grad_w', 'grad_dn_dt_bias': 'grad_w', 'grad_dn_norm_g': 'grad_w', 'grad_fox_q_norm_g': 'grad_w', 'grad_fox_k_norm_g': 'grad_w', 'grad_fox_f_bias': 'grad_w', 'grad_w_out_even': 'grad_w', 'grad_w_in_odd': 'grad_w', 'grad_w_out_odd': 'grad_w', 'grad_norm_ffn2': 'grad_w', 'grad_ffn2_w_gu': 'grad_w', 'grad_ffn2_w_down': 'grad_w', 'delta_norm_ffn1': 'delta_w', 'delta_ffn1_w_gu': 'delta_w', 'delta_ffn1_w_down': 'delta_w', 'delta_norm_mix': 'delta_w', 'delta_w_in_even': 'delta_w', 'delta_dn_conv_w': 'delta_w', 'delta_dn_a_log': 'delta_w', 'delta_dn_dt_bias': 'delta_w', 'delta_dn_norm_g': 'delta_w', 'delta_fox_q_norm_g': 'delta_w', 'delta_fox_k_norm_g': 'delta_w', 'delta_fox_f_bias': 'delta_w', 'delta_w_out_even': 'delta_w', 'delta_w_in_odd': 'delta_w', 'delta_w_out_odd': 'delta_w', 'delta_norm_ffn2': 'delta_w', 'delta_ffn2_w_gu': 'delta_w', 'delta_ffn2_w_down': 'delta_w', 'new_m_norm_ffn1': 'new_m', 'new_m_ffn1_w_gu': 'new_m', 'new_m_ffn1_w_down': 'new_m', 'new_m_norm_mix': 'new_m', 'new_m_w_in_even': 'new_m', 'new_m_dn_conv_w': 'new_m', 'new_m_dn_a_log': 'new_m', 'new_m_dn_dt_bias': 'new_m', 'new_m_dn_norm_g': 'new_m', 'new_m_fox_q_norm_g': 'new_m', 'new_m_fox_k_norm_g': 'new_m', 'new_m_fox_f_bias': 'new_m', 'new_m_w_out_even': 'new_m', 'new_m_w_in_odd': 'new_m', 'new_m_w_out_odd': 'new_m', 'new_m_norm_ffn2': 'new_m', 'new_m_ffn2_w_gu': 'new_m', 'new_m_ffn2_w_down': 'new_m', 'new_v_norm_ffn1': 'new_v', 'new_v_ffn1_w_gu': 'new_v', 'new_v_ffn1_w_down': 'new_v', 'new_v_norm_mix': 'new_v', 'new_v_w_in_even': 'new_v', 'new_v_dn_conv_w': 'new_v', 'new_v_dn_a_log': 'new_v', 'new_v_dn_dt_bias': 'new_v', 'new_v_dn_norm_g': 'new_v', 'new_v_fox_q_norm_g': 'new_v', 'new_v_fox_k_norm_g': 'new_v', 'new_v_fox_f_bias': 'new_v', 'new_v_w_out_even': 'new_v', 'new_v_w_in_odd': 'new_v', 'new_v_w_out_odd': 'new_v', 'new_v_norm_ffn2': 'new_v', 'new_v_ffn2_w_gu': 'new_v', 'new_v_ffn2_w_down': 'new_v'}


def _forward(args):
    return _fwd_reference(*[args[k] for k in FWD_PARAMS])


def _output_shape():
    def fwd():
        inp = _fwd_setup_inputs(0)
        return _fwd_reference(*[inp[k] for k in FWD_PARAMS])
    out = _jax.eval_shape(fwd)
    return out.shape, out.dtype

N_MICROBATCH = 1
ADAM_LR = 0.001
ADAM_B1 = 0.9
ADAM_B2 = 0.999
ADAM_EPS = 1e-08
ADAM_WD = 0.01
ADAM_STEP = 10
PER_EXAMPLE_BATCH_AXIS = {'x': 0, 'loss_target': 0}
SHARED_INPUTS = []
_WEIGHT_DTYPES = {'norm_ffn1': _jnp.float32, 'ffn1_w_gu': _jnp.float32, 'ffn1_w_down': _jnp.float32, 'norm_mix': _jnp.float32, 'w_in_even': _jnp.float32, 'dn_conv_w': _jnp.float32, 'dn_a_log': _jnp.float32, 'dn_dt_bias': _jnp.float32, 'dn_norm_g': _jnp.float32, 'fox_q_norm_g': _jnp.float32, 'fox_k_norm_g': _jnp.float32, 'fox_f_bias': _jnp.float32, 'w_out_even': _jnp.float32, 'w_in_odd': _jnp.float32, 'w_out_odd': _jnp.float32, 'norm_ffn2': _jnp.float32, 'ffn2_w_gu': _jnp.float32, 'ffn2_w_down': _jnp.float32}
MOMENT_SCALE = {'norm_ffn1': 5.997927e+00, 'ffn1_w_gu': 1.741514e-01, 'ffn1_w_down': 2.978740e-01, 'norm_mix': 1.411348e+01, 'w_in_even': 4.724213e-01, 'dn_conv_w': 8.687855e-01, 'dn_a_log': 1.206497e+01, 'dn_dt_bias': 1.171186e+01, 'dn_norm_g': 5.168427e+01, 'fox_q_norm_g': 1.711127e+00, 'fox_k_norm_g': 1.710896e+00, 'fox_f_bias': 5.764934e+01, 'w_out_even': 1.364620e+00, 'w_in_odd': 6.656594e-01, 'w_out_odd': 1.026297e+00, 'norm_ffn2': 6.092400e+00, 'ffn2_w_gu': 1.476333e-01, 'ffn2_w_down': 2.557422e-01}


def _to_microbatches(a, axis):
    t = _jnp.moveaxis(a, axis, 0)
    t = t.reshape((N_MICROBATCH, t.shape[0] // N_MICROBATCH) + t.shape[1:])
    return _jnp.moveaxis(t, 1, axis + 1)


def setup_inputs(seed: int = 0) -> dict:
    inp = _fwd_setup_inputs(seed)
    key = _jax.random.fold_in(_jax.random.key(seed), 7919)
    shape, _ = _output_shape()
    out = dict(inp)
    out["loss_target"] = _jax.random.normal(_jax.random.fold_in(key, 0), shape, _jnp.float32)
    for i, name in enumerate(TWIN_WEIGHTS):
        w = inp[name].astype(_jnp.float32)
        if MOMENT_SCALE is None:
            s = _jnp.sqrt(_jnp.mean(_jnp.square(w)) + 1e-30)
        else:
            s = MOMENT_SCALE[name]
        km, kv = _jax.random.split(_jax.random.fold_in(key, i + 1))
        out[name] = w
        out["m_" + name] = s * _jax.random.normal(km, w.shape, _jnp.float32)
        out["v_" + name] = (s * s) * _jax.random.uniform(kv, w.shape, _jnp.float32, 0.5, 1.5)
    if N_MICROBATCH > 1:
        for name, axis in PER_EXAMPLE_BATCH_AXIS.items():
            out[name] = _to_microbatches(out[name], axis)
    return {'x': out['x'], 'norm_ffn1': out['norm_ffn1'], 'ffn1_w_gu': out['ffn1_w_gu'], 'ffn1_w_down': out['ffn1_w_down'], 'norm_mix': out['norm_mix'], 'w_in_even': out['w_in_even'], 'dn_conv_w': out['dn_conv_w'], 'dn_a_log': out['dn_a_log'], 'dn_dt_bias': out['dn_dt_bias'], 'dn_norm_g': out['dn_norm_g'], 'fox_q_norm_g': out['fox_q_norm_g'], 'fox_k_norm_g': out['fox_k_norm_g'], 'fox_f_bias': out['fox_f_bias'], 'w_out_even': out['w_out_even'], 'w_in_odd': out['w_in_odd'], 'w_out_odd': out['w_out_odd'], 'norm_ffn2': out['norm_ffn2'], 'ffn2_w_gu': out['ffn2_w_gu'], 'ffn2_w_down': out['ffn2_w_down'], 'loss_target': out['loss_target'], 'm_norm_ffn1': out['m_norm_ffn1'], 'm_ffn1_w_gu': out['m_ffn1_w_gu'], 'm_ffn1_w_down': out['m_ffn1_w_down'], 'm_norm_mix': out['m_norm_mix'], 'm_w_in_even': out['m_w_in_even'], 'm_dn_conv_w': out['m_dn_conv_w'], 'm_dn_a_log': out['m_dn_a_log'], 'm_dn_dt_bias': out['m_dn_dt_bias'], 'm_dn_norm_g': out['m_dn_norm_g'], 'm_fox_q_norm_g': out['m_fox_q_norm_g'], 'm_fox_k_norm_g': out['m_fox_k_norm_g'], 'm_fox_f_bias': out['m_fox_f_bias'], 'm_w_out_even': out['m_w_out_even'], 'm_w_in_odd': out['m_w_in_odd'], 'm_w_out_odd': out['m_w_out_odd'], 'm_norm_ffn2': out['m_norm_ffn2'], 'm_ffn2_w_gu': out['m_ffn2_w_gu'], 'm_ffn2_w_down': out['m_ffn2_w_down'], 'v_norm_ffn1': out['v_norm_ffn1'], 'v_ffn1_w_gu': out['v_ffn1_w_gu'], 'v_ffn1_w_down': out['v_ffn1_w_down'], 'v_norm_mix': out['v_norm_mix'], 'v_w_in_even': out['v_w_in_even'], 'v_dn_conv_w': out['v_dn_conv_w'], 'v_dn_a_log': out['v_dn_a_log'], 'v_dn_dt_bias': out['v_dn_dt_bias'], 'v_dn_norm_g': out['v_dn_norm_g'], 'v_fox_q_norm_g': out['v_fox_q_norm_g'], 'v_fox_k_norm_g': out['v_fox_k_norm_g'], 'v_fox_f_bias': out['v_fox_f_bias'], 'v_w_out_even': out['v_w_out_even'], 'v_w_in_odd': out['v_w_in_odd'], 'v_w_out_odd': out['v_w_out_odd'], 'v_norm_ffn2': out['v_norm_ffn2'], 'v_ffn2_w_gu': out['v_ffn2_w_gu'], 'v_ffn2_w_down': out['v_ffn2_w_down']}


def _loss(weights, diff, rest, loss_target):
    with _jax.named_scope("forward"):
        args = {**rest, TWIN_DIFF_INPUT: diff, **{k: w.astype(_WEIGHT_DTYPES[k]) for k, w in weights.items()}}
        y = _forward(args)
    with _jax.named_scope("loss_head"):
        err = _jnp.square(y.astype(_jnp.float32) - loss_target)
        return 0.5 * _jnp.sum(_jnp.mean(err, axis=-1)) if err.ndim else 0.5 * err


def _adamw(w, g, m, v):
    m = ADAM_B1 * m + (1.0 - ADAM_B1) * g
    v = ADAM_B2 * v + (1.0 - ADAM_B2) * _jnp.square(g)
    m_hat = m / (1.0 - ADAM_B1 ** ADAM_STEP)
    v_hat = v / (1.0 - ADAM_B2 ** ADAM_STEP)
    delta = -ADAM_LR * (m_hat / (_jnp.sqrt(v_hat) + ADAM_EPS) + ADAM_WD * w)
    return delta, m, v


def reference(x, norm_ffn1, ffn1_w_gu, ffn1_w_down, norm_mix, w_in_even, dn_conv_w, dn_a_log, dn_dt_bias, dn_norm_g, fox_q_norm_g, fox_k_norm_g, fox_f_bias, w_out_even, w_in_odd, w_out_odd, norm_ffn2, ffn2_w_gu, ffn2_w_down, loss_target, m_norm_ffn1, m_ffn1_w_gu, m_ffn1_w_down, m_norm_mix, m_w_in_even, m_dn_conv_w, m_dn_a_log, m_dn_dt_bias, m_dn_norm_g, m_fox_q_norm_g, m_fox_k_norm_g, m_fox_f_bias, m_w_out_even, m_w_in_odd, m_w_out_odd, m_norm_ffn2, m_ffn2_w_gu, m_ffn2_w_down, v_norm_ffn1, v_ffn1_w_gu, v_ffn1_w_down, v_norm_mix, v_w_in_even, v_dn_conv_w, v_dn_a_log, v_dn_dt_bias, v_dn_norm_g, v_fox_q_norm_g, v_fox_k_norm_g, v_fox_f_bias, v_w_out_even, v_w_in_odd, v_w_out_odd, v_norm_ffn2, v_ffn2_w_gu, v_ffn2_w_down):
    given = dict(x=x, norm_ffn1=norm_ffn1, ffn1_w_gu=ffn1_w_gu, ffn1_w_down=ffn1_w_down, norm_mix=norm_mix, w_in_even=w_in_even, dn_conv_w=dn_conv_w, dn_a_log=dn_a_log, dn_dt_bias=dn_dt_bias, dn_norm_g=dn_norm_g, fox_q_norm_g=fox_q_norm_g, fox_k_norm_g=fox_k_norm_g, fox_f_bias=fox_f_bias, w_out_even=w_out_even, w_in_odd=w_in_odd, w_out_odd=w_out_odd, norm_ffn2=norm_ffn2, ffn2_w_gu=ffn2_w_gu, ffn2_w_down=ffn2_w_down, loss_target=loss_target, m_norm_ffn1=m_norm_ffn1, m_ffn1_w_gu=m_ffn1_w_gu, m_ffn1_w_down=m_ffn1_w_down, m_norm_mix=m_norm_mix, m_w_in_even=m_w_in_even, m_dn_conv_w=m_dn_conv_w, m_dn_a_log=m_dn_a_log, m_dn_dt_bias=m_dn_dt_bias, m_dn_norm_g=m_dn_norm_g, m_fox_q_norm_g=m_fox_q_norm_g, m_fox_k_norm_g=m_fox_k_norm_g, m_fox_f_bias=m_fox_f_bias, m_w_out_even=m_w_out_even, m_w_in_odd=m_w_in_odd, m_w_out_odd=m_w_out_odd, m_norm_ffn2=m_norm_ffn2, m_ffn2_w_gu=m_ffn2_w_gu, m_ffn2_w_down=m_ffn2_w_down, v_norm_ffn1=v_norm_ffn1, v_ffn1_w_gu=v_ffn1_w_gu, v_ffn1_w_down=v_ffn1_w_down, v_norm_mix=v_norm_mix, v_w_in_even=v_w_in_even, v_dn_conv_w=v_dn_conv_w, v_dn_a_log=v_dn_a_log, v_dn_dt_bias=v_dn_dt_bias, v_dn_norm_g=v_dn_norm_g, v_fox_q_norm_g=v_fox_q_norm_g, v_fox_k_norm_g=v_fox_k_norm_g, v_fox_f_bias=v_fox_f_bias, v_w_out_even=v_w_out_even, v_w_in_odd=v_w_in_odd, v_w_out_odd=v_w_out_odd, v_norm_ffn2=v_norm_ffn2, v_ffn2_w_gu=v_ffn2_w_gu, v_ffn2_w_down=v_ffn2_w_down)
    weights = {n: given[n] for n in TWIN_WEIGHTS}
    shared = {n: given[n] for n in SHARED_INPUTS}
    per_example = {n: given[n] for n in ['x']}
    grad_fn = _jax.value_and_grad(_loss, argnums=(0, 1))

    def one_microbatch(ex, loss_target):
        ex = dict(ex)
        diff = ex.pop(TWIN_DIFF_INPUT)
        return grad_fn(weights, diff, {**shared, **ex}, loss_target)

    if N_MICROBATCH == 1:
        loss, (grad_w, grad_x) = one_microbatch(per_example, given["loss_target"])
    else:
        def body(carry, xs):
            loss_sum, grad_sum = carry
            l_k, (gw_k, gx_k) = one_microbatch(xs[0], xs[1])
            with _jax.named_scope("update"):
                return (loss_sum + l_k, _jax.tree.map(_jnp.add, grad_sum, gw_k)), gx_k

        init = (_jnp.zeros((), _jnp.float32), _jax.tree.map(_jnp.zeros_like, weights))
        (loss, grad_w), grad_x = _jax.lax.scan(body, init, (per_example, given["loss_target"]))
    with _jax.named_scope("update"):
        delta_w, new_m, new_v = {}, {}, {}
        for n in TWIN_WEIGHTS:
            delta_w[n], new_m[n], new_v[n] = _adamw(weights[n], grad_w[n], given["m_" + n], given["v_" + n])
    return (loss, grad_x, *[grad_w[n] for n in TWIN_WEIGHTS], *[delta_w[n] for n in TWIN_WEIGHTS],
            *[new_m[n] for n in TWIN_WEIGHTS], *[new_v[n] for n in TWIN_WEIGHTS])
```

```python
import functools
import math

import jax
import jax.numpy as jnp
from jax import lax
from jax.experimental import pallas as pl
from jax.experimental.pallas import tpu as pltpu

f32 = jnp.float32
bf16 = jnp.bfloat16
SDS = jax.ShapeDtypeStruct

HEAD_DIM = 128
LANES = 128
CONV_WIDTH = 4
DN_CHUNK = 64
EPS = 1e-6
NDEV = 8
VMEM_LIMIT_BYTES = 56 * 1024 * 1024
HI = lax.Precision.HIGHEST
ADAMW_TILE_ELEMS = 384 * 1024

ADAM_LR = 0.001
ADAM_B1 = 0.9
ADAM_B2 = 0.999
ADAM_EPS = 1e-08
ADAM_WD = 0.01
ADAM_STEP = 10

NN = (((1,), (0,)), ((), ()))
NT = (((1,), (1,)), ((), ()))
TN = (((0,), (0,)), ((), ()))


def _call(body, name, grid, in_specs, out_specs, out_shape, scratch=(), **cp):
    return pl.pallas_call(
        body, name=name, grid=grid, in_specs=in_specs, out_specs=out_specs, out_shape=out_shape,
        scratch_shapes=list(scratch),
        compiler_params=pltpu.CompilerParams(vmem_limit_bytes=VMEM_LIMIT_BYTES, **cp))


def _tile(n, want, align=8):
    if n <= want:
        return n
    for t in range(want // align * align, 0, -align):
        if n % t == 0:
            return t
    return n


def _dot(a, b, dims=NN, precision=None):
    return lax.dot_general(a, b, dims, preferred_element_type=f32, precision=precision)


def _logsig(x):
    return jnp.minimum(x, 0.0) - jnp.log(1.0 + jnp.exp(-jnp.abs(x)))


def _softplus(x):
    return jnp.maximum(x, 0.0) + jnp.log(1.0 + jnp.exp(-jnp.abs(x)))


def _rms(x, g):
    return x * lax.rsqrt(jnp.mean(x * x, axis=-1, keepdims=True) + EPS) * g


def _lane_pick(blk, lane_idx):
    lane = lax.broadcasted_iota(jnp.int32, (1, LANES), 1)
    return jnp.sum(jnp.where(lane == lane_idx, blk, 0.0), axis=1, keepdims=True)


def mm(a, b, mode, out_dtype, name, tm=512, tn=512, res=None, scale=1.0):
    if mode == "nn":
        (M, K), (_, N) = a.shape, b.shape
    elif mode == "nt":
        (M, K), (N, _) = a.shape, b.shape
    else:
        (K, M), (_, N) = a.shape, b.shape
    tm, tn = _tile(M, tm, LANES), _tile(N, tn, LANES)
    a_spec = pl.BlockSpec((K, tm), lambda j, i: (0, i)) if mode == "tn" else pl.BlockSpec((tm, K), lambda j, i: (i, 0))
    b_spec = pl.BlockSpec((tn, K), lambda j, i: (j, 0)) if mode == "nt" else pl.BlockSpec((K, tn), lambda j, i: (0, j))
    dims = {"nn": NN, "nt": NT, "tn": TN}[mode]
    o_spec = pl.BlockSpec((tm, tn), lambda j, i: (i, j))

    def body(*refs):
        acc = _dot(refs[0][...].astype(bf16), refs[1][...].astype(bf16), dims)
        if scale != 1.0:
            acc = acc * scale
        if res is not None:
            acc = acc + refs[2][...]
        refs[-1][...] = acc.astype(out_dtype)

    ins, specs = [a, b], [a_spec, b_spec]
    if res is not None:
        ins.append(res)
        specs.append(o_spec)
    return _call(body, name, (N // tn, M // tm), specs, o_spec, SDS((M, N), out_dtype))(*ins)


def _rowspec(tm, w, cb=0):
    return pl.BlockSpec((tm, w), lambda i: (i, cb))


def _bspec(w):
    return pl.BlockSpec((1, w), lambda i: (0, 0))


def rms_fwd(x, g, name):
    S, D = x.shape
    tm = _tile(S, 512)

    def body(x_ref, g_ref, h_ref):
        h_ref[...] = _rms(x_ref[...], g_ref[...]).astype(bf16)

    return _call(body, name, (S // tm,), [_rowspec(tm, D), _bspec(D)], _rowspec(tm, D), SDS((S, D), bf16))(x, g)


def rms_bwd(x, g, dh, dres, name):
    S, D = x.shape
    tm = _tile(S, 256)

    def body(x_ref, g_ref, dh_ref, dres_ref, dx_ref, dg_ref):
        _, vjp = jax.vjp(_rms, x_ref[...], g_ref[...])
        dx, dg = vjp(dh_ref[...])
        dx_ref[...] = dres_ref[...] + dx

        @pl.when(pl.program_id(0) == 0)
        def _():
            dg_ref[...] = jnp.zeros_like(dg_ref)

        dg_ref[...] += dg

    return _call(body, name, (S // tm,), [_rowspec(tm, D), _bspec(D), _rowspec(tm, D), _rowspec(tm, D)],
                 (_rowspec(tm, D), _bspec(D)), (SDS((S, D), f32), SDS((1, D), f32)))(x, g, dh, dres)


def _swiglu(g, u):
    return g * jax.nn.sigmoid(g) * u


def swiglu_fwd(gu, name):
    S, W2 = gu.shape
    W = W2 // 2
    tm = _tile(S, 256)

    def body(g_ref, u_ref, a_ref):
        a_ref[...] = _swiglu(g_ref[...].astype(f32), u_ref[...].astype(f32)).astype(bf16)

    return _call(body, name, (S // tm,), [_rowspec(tm, W, 0), _rowspec(tm, W, 1)], _rowspec(tm, W), SDS((S, W), bf16))(gu, gu)


def swiglu_bwd(gu, da, scale, name):
    S, W2 = gu.shape
    W = W2 // 2
    tm = _tile(S, 256)

    def body(g_ref, u_ref, da_ref, o_ref):
        _, vjp = jax.vjp(_swiglu, g_ref[...].astype(f32), u_ref[...].astype(f32))
        dg, du = vjp(da_ref[...].astype(f32) * scale)
        o_ref[:, :W] = dg.astype(bf16)
        o_ref[:, W:] = du.astype(bf16)

    return _call(body, name, (S // tm,), [_rowspec(tm, W, 0), _rowspec(tm, W, 1), _rowspec(tm, W)], _rowspec(tm, W2),
                 SDS((S, W2), bf16))(gu, gu, da)


def loss_head(y, t, name):
    S, D = y.shape
    tm = _tile(S, 512)

    def body(y_ref, t_ref, dy_ref, l_ref):
        e = y_ref[...] - t_ref[...]
        dy_ref[...] = e * (1.0 / D)

        @pl.when(pl.program_id(0) == 0)
        def _():
            l_ref[...] = jnp.zeros_like(l_ref)

        l_ref[...] += jnp.sum(jnp.sum(e * e, axis=1, keepdims=True), axis=0, keepdims=True) * (0.5 / D)

    return _call(body, name, (S // tm,), [_rowspec(tm, D), _rowspec(tm, D)], (_rowspec(tm, D), _bspec(LANES)),
                 (SDS((S, D), f32), SDS((1, LANES), f32)))(y, t)


def ffn_fwd(x, g, wgu, wd, tag):
    h = rms_fwd(x, g, f"{tag}_rms")
    gu = mm(h, wgu, "nn", bf16, f"{tag}_gu", tm=1024, tn=768)
    a = swiglu_fwd(gu, f"{tag}_act")
    xo = mm(a, wd, "nn", f32, f"{tag}_down", tm=512, tn=512, res=x, scale=0.5)
    return xo, (h, gu, a)


def ffn_bwd(x, g, wgu, wd, saved, dy, tag):
    h, gu, a = saved
    da = mm(dy, wd, "nt", bf16, f"{tag}_bda", tm=512, tn=768)
    dgu = swiglu_bwd(gu, da, 0.5, f"{tag}_bact")
    dwd = mm(a, dy, "tn", bf16, f"{tag}_bwd", tm=512, tn=512, scale=0.5)
    dwgu = mm(h, dgu, "tn", bf16, f"{tag}_bwgu", tm=512, tn=768)
    dh = mm(dgu, wgu, "nt", f32, f"{tag}_bdh", tm=512, tn=512)
    dx, dg = rms_bwd(x, g, dh, dy, f"{tag}_brms")
    return dx, dg, dwgu, dwd


def _split_bf16(x):
    hi = x.astype(bf16)
    lo = (x - hi.astype(f32)).astype(bf16)
    return hi, lo


def _sb_masks(tq):
    r = lax.broadcasted_iota(jnp.int32, (tq, tq), 0)
    c = lax.broadcasted_iota(jnp.int32, (tq, tq), 1)
    return c < r, (r > c).astype(bf16), (r < c).astype(bf16)


def sb_fwd(qkv, nh, name, tq=256):
    S = qkv.shape[0]
    tq = _tile(S, tq)
    scale = HEAD_DIM ** -0.5

    def body(q_ref, k_ref, v_ref, o_ref):
        i = pl.program_id(1)
        q = q_ref[...]
        strict, after, _ = _sb_masks(tq)

        def tile(kb, carry, diag):
            r, acc = carry
            off = pl.multiple_of(kb * tq, tq)
            k = k_ref[pl.ds(off, tq), :]
            v = v_ref[pl.ds(off, tq), :]
            z = _dot(q, k, NT) * scale
            ls = _logsig(z)
            lm = ls - z
            if diag:
                lm = jnp.where(strict, lm, 0.0)
            hi, lo = _split_bf16(lm)
            a = jnp.exp(ls + (_dot(hi, after) + _dot(lo, after)) + r)
            if diag:
                a = jnp.where(strict, a, 0.0)
            return r + jnp.sum(lm, axis=1, keepdims=True), acc + _dot(a.astype(bf16), v)

        carry = tile(i, (jnp.zeros((tq, 1), f32), jnp.zeros((tq, HEAD_DIM), f32)), True)
        carry = lax.fori_loop(0, i, lambda t, c: tile(i - 1 - t, c, False), carry)
        o_ref[...] = carry[1].astype(o_ref.dtype)

    blk = lambda cb0: pl.BlockSpec((tq, HEAD_DIM), lambda h, i: (i, cb0 + h))
    full = lambda cb0: pl.BlockSpec((S, HEAD_DIM), lambda h, i: (0, cb0 + h))
    return _call(body, name, (nh, S // tq), [blk(0), full(nh), full(2 * nh)], blk(0),
                 SDS((S, nh * HEAD_DIM), bf16))(qkv, qkv, qkv)


def sb_bwd(qkv, do, nh, name, tq=256):
    S = qkv.shape[0]
    tq = _tile(S, tq)
    nq = S // tq
    scale = HEAD_DIM ** -0.5

    def body(q_ref, k_ref, v_ref, do_ref, dq_ref, dk_ref, dv_ref, tc_ref):
        i = pl.program_id(1)

        @pl.when(i == 0)
        def _():
            dk_ref[...] = jnp.zeros_like(dk_ref)
            dv_ref[...] = jnp.zeros_like(dv_ref)

        q = q_ref[...]
        do_ = do_ref[...].astype(bf16)
        strict, after, before = _sb_masks(tq)

        def zparts(kb, diag):
            off = pl.multiple_of(kb * tq, tq)
            k = k_ref[pl.ds(off, tq), :]
            z = _dot(q, k, NT) * scale
            ls = _logsig(z)
            lm = ls - z
            if diag:
                lm = jnp.where(strict, lm, 0.0)
            return off, k, ls, lm

        def sweep1(kb, r, diag):
            tc_ref[kb] = jnp.broadcast_to(r, (tq, LANES))
            _, _, _, lm = zparts(kb, diag)
            return r + jnp.sum(lm, axis=1, keepdims=True)

        r = sweep1(i, jnp.zeros((tq, 1), f32), True)
        lax.fori_loop(0, i, lambda t, r: sweep1(i - 1 - t, r, False), r)

        def sweep2(kb, carry, diag):
            pe, dq = carry
            off, k, ls, lm = zparts(kb, diag)
            v = v_ref[pl.ds(off, tq), :]
            hi, lo = _split_bf16(lm)
            a = jnp.exp(ls + (_dot(hi, after) + _dot(lo, after)) + tc_ref[kb][:, :1])
            if diag:
                a = jnp.where(strict, a, 0.0)
            e = _dot(do_, v, NT) * a
            ehi, elo = _split_bf16(e)
            cum_e = pe + (_dot(ehi, before) + _dot(elo, before))
            sig = jnp.exp(ls)
            dz = e * (1.0 - sig) - cum_e * sig
            if diag:
                dz = jnp.where(strict, dz, 0.0)
            dzb = (dz * scale).astype(bf16)
            dk_ref[pl.ds(off, tq), :] += _dot(dzb, q, TN)
            dv_ref[pl.ds(off, tq), :] += _dot(a.astype(bf16), do_, TN)
            return pe + jnp.sum(e, axis=1, keepdims=True), dq + _dot(dzb, k)

        carry = (jnp.zeros((tq, 1), f32), jnp.zeros((tq, HEAD_DIM), f32))
        carry = lax.fori_loop(0, i, lambda kb, c: sweep2(kb, c, False), carry)
        carry = sweep2(i, carry, True)
        dq_ref[...] = carry[1]

    blk = lambda cb0: pl.BlockSpec((tq, HEAD_DIM), lambda h, i: (i, cb0 + h))
    full = lambda cb0: pl.BlockSpec((S, HEAD_DIM), lambda h, i: (0, cb0 + h))
    sh = SDS((S, nh * HEAD_DIM), f32)
    return _call(body, name, (nh, nq), [blk(0), full(nh), full(2 * nh), blk(0)], (blk(0), full(0), full(0)),
                 (sh, sh, sh), scratch=[pltpu.VMEM((nq, tq, LANES), f32)])(qkv, qkv, qkv, do)


def fox_fwd(fq, fk, fv, c, cT, nh, lane0, name, tq=256):
    S = fq.shape[0]
    tq = _tile(S, tq)
    scale = HEAD_DIM ** -0.5

    def body(q_ref, k_ref, v_ref, c_ref, ct_ref, o_ref, lse_ref):
        h = pl.program_id(0)
        i = pl.program_id(1)
        q = q_ref[...]
        ci = _lane_pick(c_ref[...], lane0 + h)
        r = lax.broadcasted_iota(jnp.int32, (tq, tq), 0)
        cc = lax.broadcasted_iota(jnp.int32, (tq, tq), 1)
        causal = cc <= r

        def tile(kb, carry, diag):
            m, l, acc = carry
            off = pl.multiple_of(kb * tq, tq)
            k = k_ref[pl.ds(off, tq), :]
            v = v_ref[pl.ds(off, tq), :]
            cj = ct_ref[pl.ds(lane0 % 8 + h, 1), pl.ds(off, tq)]
            s = _dot(q, k, NT) * scale + (ci - cj)
            if diag:
                s = jnp.where(causal, s, -jnp.inf)
            m2 = jnp.maximum(m, jnp.max(s, axis=1, keepdims=True))
            p = jnp.exp(s - m2)
            al = jnp.exp(m - m2)
            return m2, al * l + jnp.sum(p, axis=1, keepdims=True), al * acc + _dot(p.astype(bf16), v)

        carry = (jnp.full((tq, 1), -jnp.inf, f32), jnp.zeros((tq, 1), f32), jnp.zeros((tq, HEAD_DIM), f32))
        carry = tile(i, carry, True)
        m, l, acc = lax.fori_loop(0, i, lambda kb, cr: tile(kb, cr, False), carry)
        o_ref[...] = acc / l
        lse_ref[...] = jnp.broadcast_to(m + jnp.log(l), (tq, HEAD_DIM))

    blk = pl.BlockSpec((tq, HEAD_DIM), lambda h, i: (i, h))
    full = pl.BlockSpec((S, HEAD_DIM), lambda h, i: (0, h))
    cspec = pl.BlockSpec((tq, LANES), lambda h, i: (i, 0))
    ctspec = pl.BlockSpec((8, S), lambda h, i: (lane0 // 8, 0))
    sh = SDS((S, nh * HEAD_DIM), f32)
    return _call(body, name, (nh, S // tq), [blk, full, full, cspec, ctspec], (blk, blk), (sh, sh))(fq, fk, fv, c, cT)


def fox_bwd(fq, fk, fv, c, cT, o, lse, do, nh, lane0, name, tq=256):
    S = fq.shape[0]
    tq = _tile(S, tq)
    scale = HEAD_DIM ** -0.5

    def body(q_ref, k_ref, v_ref, c_ref, ct_ref, o_ref, lse_ref, do_ref, dq_ref, dk_ref, dv_ref, dct_ref):
        h = pl.program_id(0)
        i = pl.program_id(1)

        @pl.when(i == 0)
        def _():
            dk_ref[...] = jnp.zeros_like(dk_ref)
            dv_ref[...] = jnp.zeros_like(dv_ref)

        @pl.when((i == 0) & (h == 0))
        def _():
            dct_ref[...] = jnp.zeros_like(dct_ref)

        q = q_ref[...]
        dof = do_ref[...]
        do_ = dof.astype(bf16)
        ci = _lane_pick(c_ref[...], lane0 + h)
        lse_ = lse_ref[...][:, :1]
        dl = jnp.sum(dof * o_ref[...], axis=1, keepdims=True)
        r = lax.broadcasted_iota(jnp.int32, (tq, tq), 0)
        cc = lax.broadcasted_iota(jnp.int32, (tq, tq), 1)
        causal = cc <= r

        def tile(kb, carry, diag):
            dq, rs = carry
            off = pl.multiple_of(kb * tq, tq)
            k = k_ref[pl.ds(off, tq), :]
            v = v_ref[pl.ds(off, tq), :]
            cj = ct_ref[pl.ds(lane0 % 8 + h, 1), pl.ds(off, tq)]
            s = _dot(q, k, NT) * scale + (ci - cj)
            p = jnp.exp(s - lse_)
            if diag:
                p = jnp.where(causal, p, 0.0)
            ds = p * (_dot(do_, v, NT) - dl)
            dsb = (ds * scale).astype(bf16)
            dk_ref[pl.ds(off, tq), :] += _dot(dsb, q, TN)
            dv_ref[pl.ds(off, tq), :] += _dot(p.astype(bf16), do_, TN)
            dct_ref[pl.ds(lane0 + h, 1), pl.ds(off, tq)] -= jnp.sum(ds, axis=0, keepdims=True)
            return dq + _dot(dsb, k), rs + jnp.sum(ds, axis=1, keepdims=True)

        carry = (jnp.zeros((tq, HEAD_DIM), f32), jnp.zeros((tq, 1), f32))
        carry = lax.fori_loop(0, i, lambda kb, cr: tile(kb, cr, False), carry)
        dq, rs = tile(i, carry, True)
        dq_ref[...] = dq
        dct_ref[pl.ds(lane0 + h, 1), pl.ds(pl.multiple_of(i * tq, tq), tq)] += rs.T

    blk = pl.BlockSpec((tq, HEAD_DIM), lambda h, i: (i, h))
    full = pl.BlockSpec((S, HEAD_DIM), lambda h, i: (0, h))
    cspec = pl.BlockSpec((tq, LANES), lambda h, i: (i, 0))
    ctspec = pl.BlockSpec((8, S), lambda h, i: (lane0 // 8, 0))
    dctspec = pl.BlockSpec((LANES, S), lambda h, i: (0, 0))
    sh = SDS((S, nh * HEAD_DIM), f32)
    return _call(body, name, (nh, S // tq), [blk, full, full, cspec, ctspec, blk, blk, blk], (blk, full, full, dctspec),
                 (sh, sh, sh, SDS((LANES, S), f32)))(fq, fk, fv, c, cT, o, lse, do)


def gates_fwd(proj, small_cb, fbias_row, name, tm=256):
    S = proj.shape[0]
    tm = _tile(S, tm)

    def body(sm_ref, fb_ref, c_ref, ct_ref, carry_ref):
        @pl.when(pl.program_id(0) == 0)
        def _():
            carry_ref[...] = jnp.zeros_like(carry_ref)

        lf = _logsig(sm_ref[...] + fb_ref[...])
        r = lax.broadcasted_iota(jnp.int32, (tm, tm), 0)
        cc = lax.broadcasted_iota(jnp.int32, (tm, tm), 1)
        c = _dot((r >= cc).astype(f32), lf, NN, HI) + carry_ref[...]
        carry_ref[...] += jnp.sum(lf, axis=0, keepdims=True)
        c_ref[...] = c
        ct_ref[...] = c.T

    return _call(body, name, (S // tm,), [_rowspec(tm, LANES, small_cb), _bspec(LANES)],
                 (_rowspec(tm, LANES), pl.BlockSpec((LANES, tm), lambda i: (0, i))),
                 (SDS((S, LANES), f32), SDS((LANES, S), f32)), scratch=[pltpu.VMEM((1, LANES), f32)])(proj, fbias_row)


def gates_bwd(proj, small_cb, fbias_row, dcT, dsm_dn, lane0, nh, name, tm=256):
    S = proj.shape[0]
    tm = _tile(S, tm)
    nt = S // tm

    def body(sm_ref, fb_ref, dct_ref, dsm_ref, o_ref, dfb_ref, carry_ref):
        @pl.when(pl.program_id(0) == 0)
        def _():
            carry_ref[...] = jnp.zeros_like(carry_ref)
            dfb_ref[...] = jnp.zeros_like(dfb_ref)

        dc = dct_ref[...].T
        r = lax.broadcasted_iota(jnp.int32, (tm, tm), 0)
        cc = lax.broadcasted_iota(jnp.int32, (tm, tm), 1)
        dlf = _dot((r <= cc).astype(f32), dc, NN, HI) + carry_ref[...]
        carry_ref[...] += jnp.sum(dc, axis=0, keepdims=True)
        lane = lax.broadcasted_iota(jnp.int32, (1, LANES), 1)
        dpre = jnp.where((lane >= lane0) & (lane < lane0 + nh), dlf * jax.nn.sigmoid(-(sm_ref[...] + fb_ref[...])), 0.0)
        o_ref[...] = dsm_ref[...] + dpre
        dfb_ref[...] += jnp.sum(dpre, axis=0, keepdims=True)

    rev = lambda i: nt - 1 - i
    return _call(body, name, (nt,),
                 [pl.BlockSpec((tm, LANES), lambda i: (rev(i), small_cb)), _bspec(LANES),
                  pl.BlockSpec((LANES, tm), lambda i: (0, rev(i))), pl.BlockSpec((tm, LANES), lambda i: (rev(i), 0))],
                 (pl.BlockSpec((tm, LANES), lambda i: (rev(i), 0)), _bspec(LANES)),
                 (SDS((S, LANES), f32), SDS((1, LANES), f32)), scratch=[pltpu.VMEM((1, LANES), f32)])(proj, fbias_row, dcT, dsm_dn)


PAD_ROWS = 8


def conv_fwd(proj, w, width, name):
    S = proj.shape[0]
    cw = 256 if width % 256 == 0 else 128

    def body(x_ref, w_ref, y_ref, pad_ref):
        pad_ref[pl.ds(0, PAD_ROWS), :] = jnp.zeros((PAD_ROWS, cw), f32)
        pad_ref[pl.ds(PAD_ROWS, S), :] = x_ref[...]
        y = jnp.zeros((S, cw), f32)
        for i in range(CONV_WIDTH):
            y = y + pad_ref[pl.ds(PAD_ROWS - (CONV_WIDTH - 1) + i, S), :] * w_ref[pl.ds(i, 1), :]
        y_ref[...] = y * jax.nn.sigmoid(y)

    spec = pl.BlockSpec((S, cw), lambda j: (0, j))
    return _call(body, name, (width // cw,), [spec, pl.BlockSpec((CONV_WIDTH, cw), lambda j: (0, j))], spec,
                 SDS((S, width), f32), scratch=[pltpu.VMEM((S + PAD_ROWS, cw), f32)])(proj, w)


def conv_bwd(proj, w, dy, name):
    S, width = dy.shape
    cw = 256 if width % 256 == 0 else 128

    def body(x_ref, w_ref, dy_ref, dx_ref, dw_ref, xpad_ref, dpad_ref):
        xpad_ref[pl.ds(0, PAD_ROWS), :] = jnp.zeros((PAD_ROWS, cw), f32)
        xpad_ref[pl.ds(PAD_ROWS, S), :] = x_ref[...]
        y = jnp.zeros((S, cw), f32)
        for i in range(CONV_WIDTH):
            y = y + xpad_ref[pl.ds(PAD_ROWS - (CONV_WIDTH - 1) + i, S), :] * w_ref[pl.ds(i, 1), :]
        sg = jax.nn.sigmoid(y)
        dpre = dy_ref[...] * (sg * (1.0 + y * (1.0 - sg)))
        dpad_ref[pl.ds(S, PAD_ROWS), :] = jnp.zeros((PAD_ROWS, cw), f32)
        dpad_ref[pl.ds(0, S), :] = dpre
        dx = jnp.zeros((S, cw), f32)
        for i in range(CONV_WIDTH):
            dx = dx + dpad_ref[pl.ds(CONV_WIDTH - 1 - i, S), :] * w_ref[pl.ds(i, 1), :]
            dw_ref[pl.ds(i, 1), :] = jnp.sum(dpre * xpad_ref[pl.ds(PAD_ROWS - (CONV_WIDTH - 1) + i, S), :], axis=0, keepdims=True)
        dx_ref[...] = dx

    spec = pl.BlockSpec((S, cw), lambda j: (0, j))
    wspec = pl.BlockSpec((CONV_WIDTH, cw), lambda j: (0, j))
    return _call(body, name, (width // cw,), [spec, wspec, spec], (spec, wspec),
                 (SDS((S, width), f32), SDS((CONV_WIDTH, width), f32)),
                 scratch=[pltpu.VMEM((S + PAD_ROWS, cw), f32), pltpu.VMEM((S + PAD_ROWS, cw), f32)])(proj, w, dy)


def _dn_chunk(h, ndn, cq, ck, cv, small, alog, dtb, s0):
    C = cq.shape[0]
    b = _lane_pick(small, h)
    d = _lane_pick(small, ndn + h)
    al = _lane_pick(alog, h)
    dt = _lane_pick(dtb, h)
    q = cq * lax.rsqrt(jnp.sum(cq * cq, axis=1, keepdims=True) + EPS) * (HEAD_DIM ** -0.5)
    k = ck * lax.rsqrt(jnp.sum(ck * ck, axis=1, keepdims=True) + EPS)
    beta = jax.nn.sigmoid(b)
    g = -jnp.exp(al) * _softplus(d + dt)
    r = lax.broadcasted_iota(jnp.int32, (C, C), 0)
    c = lax.broadcasted_iota(jnp.int32, (C, C), 1)
    incl, strict = r >= c, r > c
    gc = _dot(incl.astype(f32), g, NN, HI)
    decay = jnp.where(incl, jnp.exp(jnp.where(incl, gc - gc.T, 0.0)), 0.0)
    kb = k * beta
    a = jnp.where(strict, _dot(kb, k, NT, HI) * decay, 0.0)
    t = (r == c).astype(f32) - a
    p = a
    for _ in range(int(math.log2(C)) - 1):
        p = _dot(p, p, NN, HI)
        t = t + _dot(t, p, NN, HI)
    eg = jnp.exp(gc)
    u = _dot(t, cv * beta, NN, HI)
    w = _dot(t, kb * eg, NN, HI)
    attn = _dot(q, k, NT, HI) * decay
    g_last = jnp.sum(g, axis=0, keepdims=True)
    v_new = u - _dot(w, s0, NN, HI)
    o = _dot(q * eg, s0, NN, HI) + _dot(attn, v_new, NN, HI)
    s1 = s0 * jnp.exp(g_last) + _dot(k * jnp.exp(g_last - gc), v_new, TN, HI)
    return o, s1


def dn_fwd(cqkv, proj, small_cb, alog_row, dt_row, ndn, name):
    S = cqkv.shape[0]
    C = DN_CHUNK
    nc = S // C

    def body(q_ref, k_ref, v_ref, sm_ref, al_ref, dt_ref, o_ref, ss_ref, s_ref):
        h = pl.program_id(0)

        @pl.when(pl.program_id(1) == 0)
        def _():
            s_ref[...] = jnp.zeros_like(s_ref)

        s0 = s_ref[...]
        ss_ref[0, 0] = s0
        o, s1 = _dn_chunk(h, ndn, q_ref[...], k_ref[...], v_ref[...], sm_ref[...], al_ref[...], dt_ref[...], s0)
        o_ref[...] = o
        s_ref[...] = s1

    blk = lambda cb0: pl.BlockSpec((C, HEAD_DIM), lambda h, n: (n, cb0 + h))
    row = pl.BlockSpec((1, LANES), lambda h, n: (0, 0))
    return _call(body, name, (ndn, nc),
                 [blk(0), blk(ndn), blk(2 * ndn), pl.BlockSpec((C, LANES), lambda h, n: (n, small_cb)), row, row],
                 (blk(0), pl.BlockSpec((1, 1, HEAD_DIM, HEAD_DIM), lambda h, n: (h, n, 0, 0))),
                 (SDS((S, ndn * HEAD_DIM), f32), SDS((ndn, nc, HEAD_DIM, HEAD_DIM), f32)),
                 scratch=[pltpu.VMEM((HEAD_DIM, HEAD_DIM), f32)])(cqkv, cqkv, cqkv, proj, alog_row, dt_row)


def dn_bwd(cqkv, proj, small_cb, alog_row, dt_row, ssave, do, ndn, name):
    S = cqkv.shape[0]
    C = DN_CHUNK
    nc = S // C

    def body(q_ref, k_ref, v_ref, sm_ref, al_ref, dt_ref, ss_ref, do_ref, dq_ref, dk_ref, dv_ref, dsm_ref, dal_ref, ddt_ref, ds_ref):
        h = pl.program_id(0)
        n = pl.program_id(1)

        @pl.when(n == 0)
        def _():
            ds_ref[...] = jnp.zeros_like(ds_ref)

        @pl.when((n == 0) & (h == 0))
        def _():
            dsm_ref[...] = jnp.zeros_like(dsm_ref)
            dal_ref[...] = jnp.zeros_like(dal_ref)
            ddt_ref[...] = jnp.zeros_like(ddt_ref)

        _, vjp = jax.vjp(functools.partial(_dn_chunk, h, ndn), q_ref[...], k_ref[...], v_ref[...], sm_ref[...],
                         al_ref[...], dt_ref[...], ss_ref[0, 0])
        dq, dk, dv, dsm, dal, ddt, ds0 = vjp((do_ref[...], ds_ref[...]))
        dq_ref[...] = dq
        dk_ref[...] = dk
        dv_ref[...] = dv
        row0 = pl.multiple_of((nc - 1 - n) * C, C)
        dsm_ref[pl.ds(row0, C), :] += dsm
        dal_ref[...] += dal
        ddt_ref[...] += ddt
        ds_ref[...] = ds0

    blk = lambda cb0: pl.BlockSpec((C, HEAD_DIM), lambda h, n: (nc - 1 - n, cb0 + h))
    row = pl.BlockSpec((1, LANES), lambda h, n: (0, 0))
    sh = SDS((S, ndn * HEAD_DIM), f32)
    return _call(body, name, (ndn, nc),
                 [blk(0), blk(ndn), blk(2 * ndn), pl.BlockSpec((C, LANES), lambda h, n: (nc - 1 - n, small_cb)), row, row,
                  pl.BlockSpec((1, 1, HEAD_DIM, HEAD_DIM), lambda h, n: (h, nc - 1 - n, 0, 0)), blk(0)],
                 (blk(0), blk(0), blk(0), pl.BlockSpec((S, LANES), lambda h, n: (0, 0)), row, row),
                 (sh, sh, sh, SDS((S, LANES), f32), SDS((1, LANES), f32), SDS((1, LANES), f32)),
                 scratch=[pltpu.VMEM((HEAD_DIM, HEAD_DIM), f32)])(cqkv, cqkv, cqkv, proj, alog_row, dt_row, ssave, do)


def even_pre(proj, gq, gk, dfx, cb0, name):
    S = proj.shape[0]
    tm = _tile(S, 256)
    nh = dfx // HEAD_DIM

    def body(q_ref, k_ref, v_ref, gq_ref, gk_ref, fq_ref, fk_ref, fv_ref):
        for h in range(nh):
            sl = slice(h * HEAD_DIM, (h + 1) * HEAD_DIM)
            fq_ref[:, sl] = _rms(q_ref[:, sl], gq_ref[...]).astype(bf16)
            fk_ref[:, sl] = _rms(k_ref[:, sl], gk_ref[...]).astype(bf16)
        fv_ref[...] = v_ref[...].astype(bf16)

    sh = SDS((S, dfx), bf16)
    o = _rowspec(tm, dfx)
    return _call(body, name, (S // tm,),
                 [_rowspec(tm, dfx, cb0), _rowspec(tm, dfx, cb0 + 1), _rowspec(tm, dfx, cb0 + 2), _bspec(HEAD_DIM), _bspec(HEAD_DIM)],
                 (o, o, o), (sh, sh, sh))(proj, proj, proj, gq, gk)


def even_pre_bwd(proj, gq, gk, dfq, dfk, dfx, cb0, name):
    S = proj.shape[0]
    tm = _tile(S, 256)
    nh = dfx // HEAD_DIM

    def body(q_ref, k_ref, gq_ref, gk_ref, dfq_ref, dfk_ref, dq_ref, dk_ref, dgq_ref, dgk_ref):
        @pl.when(pl.program_id(0) == 0)
        def _():
            dgq_ref[...] = jnp.zeros_like(dgq_ref)
            dgk_ref[...] = jnp.zeros_like(dgk_ref)

        for h in range(nh):
            sl = slice(h * HEAD_DIM, (h + 1) * HEAD_DIM)
            _, vq = jax.vjp(_rms, q_ref[:, sl], gq_ref[...])
            dq, dgq = vq(dfq_ref[:, sl])
            _, vk = jax.vjp(_rms, k_ref[:, sl], gk_ref[...])
            dk, dgk = vk(dfk_ref[:, sl])
            dq_ref[:, sl] = dq
            dk_ref[:, sl] = dk
            dgq_ref[...] += dgq
            dgk_ref[...] += dgk

    sh = SDS((S, dfx), f32)
    o = _rowspec(tm, dfx)
    g = SDS((1, HEAD_DIM), f32)
    return _call(body, name, (S // tm,),
                 [_rowspec(tm, dfx, cb0), _rowspec(tm, dfx, cb0 + 1), _bspec(HEAD_DIM), _bspec(HEAD_DIM), o, o],
                 (o, o, _bspec(HEAD_DIM), _bspec(HEAD_DIM)), (sh, sh, g, g))(proj, proj, gq, gk, dfq, dfk)


def _dn_out(o, gate, gn):
    return _rms(o, gn) * (gate * jax.nn.sigmoid(gate))


def _fox_out(o, gate):
    return o * jax.nn.sigmoid(gate)


def even_post(o_dn, o_fox, proj, gn, half, cb_dn_gate, cb_fox_gate, name):
    S = proj.shape[0]
    tm = _tile(S, 256)
    nh = half // HEAD_DIM

    def body(od_ref, of_ref, gd_ref, gf_ref, gn_ref, o_ref):
        for h in range(nh):
            sl = slice(h * HEAD_DIM, (h + 1) * HEAD_DIM)
            o_ref[:, sl] = _dn_out(od_ref[:, sl], gd_ref[:, sl], gn_ref[...]).astype(bf16)
        o_ref[:, half:] = _fox_out(of_ref[...], gf_ref[...]).astype(bf16)

    return _call(body, name, (S // tm,),
                 [_rowspec(tm, half), _rowspec(tm, half), _rowspec(tm, half, cb_dn_gate), _rowspec(tm, half, cb_fox_gate), _bspec(HEAD_DIM)],
                 _rowspec(tm, 2 * half), SDS((S, 2 * half), bf16))(o_dn, o_fox, proj, proj, gn)


def even_post_bwd(o_dn, o_fox, proj, gn, dom, half, cb_dn_gate, cb_fox_gate, name):
    S = proj.shape[0]
    tm = _tile(S, 256)
    nh = half // HEAD_DIM

    def body(od_ref, of_ref, gd_ref, gf_ref, gn_ref, dod_ref, dof_ref, dd_ref, df_ref, dgd_ref, dgf_ref, dgn_ref):
        @pl.when(pl.program_id(0) == 0)
        def _():
            dgn_ref[...] = jnp.zeros_like(dgn_ref)

        for h in range(nh):
            sl = slice(h * HEAD_DIM, (h + 1) * HEAD_DIM)
            _, vjp = jax.vjp(_dn_out, od_ref[:, sl], gd_ref[:, sl], gn_ref[...])
            d_o, d_gate, d_gn = vjp(dod_ref[:, sl])
            dd_ref[:, sl] = d_o
            dgd_ref[:, sl] = d_gate
            dgn_ref[...] += d_gn
        _, vjp = jax.vjp(_fox_out, of_ref[...], gf_ref[...])
        d_o, d_gate = vjp(dof_ref[...])
        df_ref[...] = d_o
        dgf_ref[...] = d_gate

    sh = SDS((S, half), f32)
    o = _rowspec(tm, half)
    return _call(body, name, (S // tm,),
                 [o, o, _rowspec(tm, half, cb_dn_gate), _rowspec(tm, half, cb_fox_gate), _bspec(HEAD_DIM),
                  _rowspec(tm, half, 0), _rowspec(tm, half, 1)],
                 (o, o, o, o, _bspec(HEAD_DIM)), (sh, sh, sh, sh, SDS((1, HEAD_DIM), f32)))(o_dn, o_fox, proj, proj, gn, dom, dom)


def _pad_row(v, lane0=0):
    return jnp.zeros((1, LANES), f32).at[0, lane0:lane0 + v.shape[0]].set(v)


def even_mixer_fwd(x, gmix, w_in, w_out, conv_w, a_log, dt_bias, gn, gq, gk, f_bias, tag):
    S, D = x.shape
    half = D // 2
    ndn = half // HEAD_DIM
    small_cb = 4 * D // LANES
    alog_row, dt_row, fb_row = _pad_row(a_log), _pad_row(dt_bias), _pad_row(f_bias, 2 * ndn)
    h = rms_fwd(x, gmix, f"{tag}_rms")
    proj = mm(h, w_in, "nn", f32, f"{tag}_in", tm=512, tn=1408)
    cqkv = conv_fwd(proj, conv_w, 3 * half, f"{tag}_conv")
    o_dn, ssave = dn_fwd(cqkv, proj, small_cb, alog_row, dt_row, ndn, f"{tag}_dn")
    c, cT = gates_fwd(proj, small_cb, fb_row, f"{tag}_gates")
    fq, fk, fv = even_pre(proj, gq, gk, half, 4, f"{tag}_pre")
    o_fox, lse = fox_fwd(fq, fk, fv, c, cT, ndn, 2 * ndn, f"{tag}_fox")
    om = even_post(o_dn, o_fox, proj, gn, half, 3, 7, f"{tag}_post")
    xo = mm(om, w_out, "nn", f32, f"{tag}_out", res=x)
    return xo, (h, proj, cqkv, o_dn, ssave, c, cT, fq, fk, fv, o_fox, lse, om, alog_row, dt_row, fb_row)


def even_mixer_bwd(x, gmix, w_in, w_out, conv_w, gn, gq, gk, saved, dy, tag):
    S, D = x.shape
    half = D // 2
    ndn = half // HEAD_DIM
    small_cb = 4 * D // LANES
    h, proj, cqkv, o_dn, ssave, c, cT, fq, fk, fv, o_fox, lse, om, alog_row, dt_row, fb_row = saved
    dom = mm(dy, w_out, "nt", f32, f"{tag}_bdom")
    dw_out = mm(om, dy, "tn", bf16, f"{tag}_bwout")
    d_odn, d_ofox, d_dgate, d_fgate, d_gn = even_post_bwd(o_dn, o_fox, proj, gn, dom, half, 3, 7, f"{tag}_bpost")
    dfq, dfk, dfv, dcT = fox_bwd(fq, fk, fv, c, cT, o_fox, lse, d_ofox, ndn, 2 * ndn, f"{tag}_bfox")
    dq_pre, dk_pre, d_gq, d_gk = even_pre_bwd(proj, gq, gk, dfq, dfk, half, 4, f"{tag}_bpre")
    dcq, dck, dcv, dsm_dn, d_alog, d_dt = dn_bwd(cqkv, proj, small_cb, alog_row, dt_row, ssave, d_odn, ndn, f"{tag}_bdn")
    d_conv_in, d_conv_w = conv_bwd(proj, conv_w, jnp.concatenate([dcq, dck, dcv], axis=1), f"{tag}_bconv")
    d_small, d_fb = gates_bwd(proj, small_cb, fb_row, dcT, dsm_dn, 2 * ndn, ndn, f"{tag}_bgates")
    dproj = jnp.concatenate([d_conv_in, d_dgate, dq_pre, dk_pre, dfv, d_fgate, d_small], axis=1)
    dw_in = mm(h, dproj, "tn", bf16, f"{tag}_bwin", tn=384)
    dh = mm(dproj, w_in, "nt", f32, f"{tag}_bdh")
    dx, d_gmix = rms_bwd(x, gmix, dh, dy, f"{tag}_brms")
    small = dict(norm_mix=d_gmix[0], dn_conv_w=d_conv_w, dn_a_log=d_alog[0, :ndn], dn_dt_bias=d_dt[0, :ndn],
                 dn_norm_g=d_gn[0], fox_q_norm_g=d_gq[0], fox_k_norm_g=d_gk[0], fox_f_bias=d_fb[0, 2 * ndn:3 * ndn])
    return dx, dw_in, dw_out, small


def odd_mixer_fwd(x, gmix, w_in, w_out, tag):
    S, D = x.shape
    nh = D // HEAD_DIM
    h = rms_fwd(x, gmix, f"{tag}_rms")
    qkv = mm(h, w_in, "nn", bf16, f"{tag}_in", tn=768)
    o = sb_fwd(qkv, nh, f"{tag}_sb")
    xo = mm(o, w_out, "nn", f32, f"{tag}_out", res=x)
    return xo, (h, qkv, o)


def odd_mixer_bwd(x, gmix, w_in, w_out, saved, dy, tag):
    S, D = x.shape
    nh = D // HEAD_DIM
    h, qkv, o = saved
    do = mm(dy, w_out, "nt", f32, f"{tag}_bdo")
    dw_out = mm(o, dy, "tn", bf16, f"{tag}_bwout")
    dq, dk, dv = sb_bwd(qkv, do, nh, f"{tag}_bsb")
    dqkv = jnp.concatenate([dq, dk, dv], axis=1)
    dw_in = mm(h, dqkv, "tn", bf16, f"{tag}_bwin", tn=768)
    dh = mm(dqkv, w_in, "nt", f32, f"{tag}_bdh")
    dx, d_gmix = rms_bwd(x, gmix, dh, dy, f"{tag}_brms")
    return dx, dw_in, dw_out, dict(norm_mix=d_gmix[0])


def _me_and_peers():
    x, y, c = lax.axis_index("x"), lax.axis_index("y"), lax.axis_index("c")
    me = 4 * x + 2 * y + c
    peers = []
    for r in range(1, NDEV):
        px = 1 - x if (r >> 2) & 1 else x
        py = 1 - y if (r >> 1) & 1 else y
        pc = 1 - c if r & 1 else c
        peers.append(((px, py, pc), 4 * px + 2 * py + pc))
    return me, peers


def _slab(ref, axis, width, k):
    idx = [slice(None)] * len(ref.shape)
    idx[axis] = pl.ds(k * width, width)
    return ref.at[tuple(idx)]


def all_gather(shards, axes, name):
    n = len(shards)
    out_shape = []
    for s, ax in zip(shards, axes):
        shp = list(s.shape)
        shp[ax] *= NDEV
        out_shape.append(SDS(tuple(shp), s.dtype))

    def body(*refs):
        ins, outs = refs[:n], refs[n:2 * n]
        send_sems, recv_sems, loc_sems = refs[2 * n:]
        me, peers = _me_and_peers()
        copies = []
        for t in range(n):
            w = ins[t].shape[axes[t]]
            loc = pltpu.make_async_copy(ins[t], _slab(outs[t], axes[t], w, me), loc_sems.at[t])
            loc.start()
            copies.append(loc)
            for r, (peer, _) in enumerate(peers):
                cp = pltpu.make_async_remote_copy(
                    src_ref=ins[t], dst_ref=_slab(outs[t], axes[t], w, me), send_sem=send_sems.at[t, r],
                    recv_sem=recv_sems.at[t, r], device_id=peer, device_id_type=pl.DeviceIdType.MESH)
                cp.start()
        for t in range(n):
            w = ins[t].shape[axes[t]]
            for r, (peer, pidx) in enumerate(peers):
                cp = pltpu.make_async_remote_copy(
                    src_ref=ins[t], dst_ref=_slab(outs[t], axes[t], w, pidx), send_sem=send_sems.at[t, r],
                    recv_sem=recv_sems.at[t, r], device_id=peer, device_id_type=pl.DeviceIdType.MESH)
                cp.wait_recv()
                cp.wait_send()
        for loc in copies:
            loc.wait()

    anyspec = pl.BlockSpec(memory_space=pl.ANY)
    return pl.pallas_call(
        body, name=name, in_specs=[anyspec] * n, out_specs=[anyspec] * n, out_shape=out_shape,
        scratch_shapes=[pltpu.SemaphoreType.DMA((n, NDEV - 1)), pltpu.SemaphoreType.DMA((n, NDEV - 1)), pltpu.SemaphoreType.DMA((n,))],
    )(*shards)


def scatter_parts(fulls, axes, name):
    n = len(fulls)
    out_shape = []
    for s, ax in zip(fulls, axes):
        shp = list(s.shape)
        shp[ax] //= NDEV
        out_shape.append(SDS((NDEV, *shp), s.dtype))

    def body(*refs):
        ins, outs = refs[:n], refs[n:2 * n]
        send_sems, recv_sems, loc_sems = refs[2 * n:]
        me, peers = _me_and_peers()
        copies = []
        for t in range(n):
            w = ins[t].shape[axes[t]] // NDEV
            loc = pltpu.make_async_copy(_slab(ins[t], axes[t], w, me), outs[t].at[me], loc_sems.at[t])
            loc.start()
            copies.append(loc)
            for r, (peer, pidx) in enumerate(peers):
                cp = pltpu.make_async_remote_copy(
                    src_ref=_slab(ins[t], axes[t], w, pidx), dst_ref=outs[t].at[me], send_sem=send_sems.at[t, r],
                    recv_sem=recv_sems.at[t, r], device_id=peer, device_id_type=pl.DeviceIdType.MESH)
                cp.start()
        for t in range(n):
            w = ins[t].shape[axes[t]] // NDEV
            for r, (peer, pidx) in enumerate(peers):
                cp = pltpu.make_async_remote_copy(
                    src_ref=_slab(ins[t], axes[t], w, pidx), dst_ref=outs[t].at[pidx], send_sem=send_sems.at[t, r],
                    recv_sem=recv_sems.at[t, r], device_id=peer, device_id_type=pl.DeviceIdType.MESH)
                cp.wait_recv()
                cp.wait_send()
        for loc in copies:
            loc.wait()

    anyspec = pl.BlockSpec(memory_space=pl.ANY)
    return pl.pallas_call(
        body, name=name, in_specs=[anyspec] * n, out_specs=[anyspec] * n, out_shape=out_shape,
        scratch_shapes=[pltpu.SemaphoreType.DMA((n, NDEV - 1)), pltpu.SemaphoreType.DMA((n, NDEV - 1)), pltpu.SemaphoreType.DMA((n,))],
    )(*fulls)


def sum_parts(parts, name):
    _, R, C = parts.shape
    tm = _tile(R, 256)

    def body(p_ref, o_ref):
        acc = p_ref[0].astype(f32)
        for k in range(1, NDEV):
            acc = acc + p_ref[k].astype(f32)
        o_ref[...] = acc

    return _call(body, name, (R // tm,), [pl.BlockSpec((NDEV, tm, C), lambda i: (0, i, 0))], _rowspec(tm, C), SDS((R, C), f32))(parts)


def adamw(w, g, m, v, name):
    R, C = w.shape
    tm = _tile(R, max(8, min(512, (ADAMW_TILE_ELEMS // C) // 8 * 8)))
    c1 = 1.0 - ADAM_B1 ** ADAM_STEP
    c2 = 1.0 - ADAM_B2 ** ADAM_STEP

    def body(w_ref, g_ref, m_ref, v_ref, d_ref, nm_ref, nv_ref):
        g_ = g_ref[...]
        nm = ADAM_B1 * m_ref[...] + (1.0 - ADAM_B1) * g_
        nv = ADAM_B2 * v_ref[...] + (1.0 - ADAM_B2) * (g_ * g_)
        d_ref[...] = -ADAM_LR * ((nm / c1) / (jnp.sqrt(nv / c2) + ADAM_EPS) + ADAM_WD * w_ref[...])
        nm_ref[...] = nm
        nv_ref[...] = nv

    spec = _rowspec(tm, C)
    sh = SDS((R, C), f32)
    return _call(body, name, (R // tm,), [spec] * 4, (spec, spec, spec), (sh, sh, sh))(w, g, m, v)


def _pad_to(n, mult):
    return -(-n // mult) * mult


def _even_perm(D):
    half = D // 2
    ndn = half // HEAD_DIM
    o_gate = 3 * half
    o_b = o_gate + half
    o_a = o_b + ndn
    o_fq = o_a + ndn
    o_fpre = o_fq + 4 * half
    return half, ndn, o_b, o_a, o_fq, o_fpre


def _even_to_mine(w):
    D = (w.shape[-1] // 4) // LANES * LANES
    half, ndn, o_b, o_a, o_fq, o_fpre = _even_perm(D)
    small = jnp.concatenate([w[..., o_b:o_b + ndn], w[..., o_a:o_a + ndn], w[..., o_fpre:o_fpre + ndn]], axis=-1)
    small = jnp.pad(small, [(0, 0)] * (w.ndim - 1) + [(0, LANES - 3 * ndn)])
    return jnp.concatenate([w[..., :o_b], w[..., o_fq:o_fpre], small], axis=-1)


def _even_from_mine(w, D):
    half, ndn, o_b, o_a, o_fq, o_fpre = _even_perm(D)
    sm = w[..., 4 * D:]
    return jnp.concatenate([w[..., :o_b], sm[..., :ndn], sm[..., ndn:2 * ndn], w[..., o_b:4 * D], sm[..., 2 * ndn:3 * ndn]], axis=-1)


def _pack_small(parts):
    flat = jnp.concatenate([p.reshape(-1).astype(f32) for p in parts])
    n = flat.shape[0]
    return jnp.pad(flat, (0, _pad_to(n, 8 * LANES) - n)).reshape(-1, LANES)


def _unpack_small(packed, like):
    flat = packed.reshape(-1)
    out, off = [], 0
    for p in like:
        out.append(flat[off:off + p.size].reshape(p.shape))
        off += p.size
    return out


SMALL_NAMES = ("norm_ffn1", "norm_mix", "dn_a_log", "dn_dt_bias", "dn_norm_g", "fox_q_norm_g", "fox_k_norm_g", "fox_f_bias", "norm_ffn2")
BIG_NAMES = ("ffn1_w_gu", "ffn1_w_down", "w_in_even", "dn_conv_w", "w_out_even", "w_in_odd", "w_out_odd", "ffn2_w_gu", "ffn2_w_down")
WEIGHT_NAMES = ("norm_ffn1", "ffn1_w_gu", "ffn1_w_down", "norm_mix", "w_in_even", "dn_conv_w", "dn_a_log", "dn_dt_bias", "dn_norm_g",
                "fox_q_norm_g", "fox_k_norm_g", "fox_f_bias", "w_out_even", "w_in_odd", "w_out_odd", "norm_ffn2", "ffn2_w_gu", "ffn2_w_down")


def kernel(x, norm_ffn1, ffn1_w_gu, ffn1_w_down, norm_mix, w_in_even, dn_conv_w, dn_a_log, dn_dt_bias, dn_norm_g, fox_q_norm_g, fox_k_norm_g, fox_f_bias, w_out_even, w_in_odd, w_out_odd, norm_ffn2, ffn2_w_gu, ffn2_w_down, loss_target, m_norm_ffn1, m_ffn1_w_gu, m_ffn1_w_down, m_norm_mix, m_w_in_even, m_dn_conv_w, m_dn_a_log, m_dn_dt_bias, m_dn_norm_g, m_fox_q_norm_g, m_fox_k_norm_g, m_fox_f_bias, m_w_out_even, m_w_in_odd, m_w_out_odd, m_norm_ffn2, m_ffn2_w_gu, m_ffn2_w_down, v_norm_ffn1, v_ffn1_w_gu, v_ffn1_w_down, v_norm_mix, v_w_in_even, v_dn_conv_w, v_dn_a_log, v_dn_dt_bias, v_dn_norm_g, v_fox_q_norm_g, v_fox_k_norm_g, v_fox_f_bias, v_w_out_even, v_w_in_odd, v_w_out_odd, v_norm_ffn2, v_ffn2_w_gu, v_ffn2_w_down):
    args = dict(locals())
    W = {n: args[n] for n in WEIGHT_NAMES}
    M = {n: args["m_" + n] for n in WEIGHT_NAMES}
    V = {n: args["v_" + n] for n in WEIGHT_NAMES}
    xs = x[0]
    tgt = loss_target[0]
    S, D = xs.shape
    L = norm_ffn1.shape[0]
    n_even = w_in_even.shape[0]
    hs = ffn1_w_down.shape[1]
    hp = _pad_to(hs, LANES)
    me = 4 * lax.axis_index("x") + 2 * lax.axis_index("y") + lax.axis_index("c")

    def prep_gu(w):
        w = w.reshape(L, D, 2, hs)
        return jnp.pad(w, ((0, 0), (0, 0), (0, 0), (0, hp - hs))).reshape(L, D, 2 * hp).astype(bf16)

    def prep_down(w):
        return jnp.pad(w, ((0, 0), (0, hp - hs), (0, 0))).astype(bf16)

    shards = [prep_gu(ffn1_w_gu), prep_down(ffn1_w_down), prep_gu(ffn2_w_gu), prep_down(ffn2_w_down),
              _even_to_mine(w_in_even).astype(bf16), w_out_even.astype(bf16), w_in_odd.astype(bf16), w_out_odd.astype(bf16),
              dn_conv_w[None]]
    axes = [2, 1, 2, 1, 1, 1, 2, 1, 0]
    wgu1, wd1, wgu2, wd2, wie, woe, wio, woo, convw = all_gather(shards, axes, "ag_weights")
    convw = jnp.moveaxis(convw, 0, 2).reshape(n_even, CONV_WIDTH, -1)

    saved = []
    cur = xs
    for l in range(L):
        j = l // 2
        x0 = cur
        x1, s1 = ffn_fwd(x0, norm_ffn1[l:l + 1], wgu1[l], wd1[l], f"l{l}f1")
        if l % 2 == 0:
            x2, sm = even_mixer_fwd(x1, norm_mix[l:l + 1], wie[j], woe[j], convw[j], dn_a_log[j], dn_dt_bias[j],
                                    dn_norm_g[j:j + 1], fox_q_norm_g[j:j + 1], fox_k_norm_g[j:j + 1], fox_f_bias[j], f"l{l}mx")
        else:
            x2, sm = odd_mixer_fwd(x1, norm_mix[l:l + 1], wio[j], woo[j], f"l{l}mx")
        x3, s2 = ffn_fwd(x2, norm_ffn2[l:l + 1], wgu2[l], wd2[l], f"l{l}f2")
        saved.append((x0, x1, x2, s1, sm, s2))
        cur = x3
    dy, loss_row = loss_head(cur, tgt, "loss")

    gsmall = {n: [None] * W[n].shape[0] for n in SMALL_NAMES}
    gconv = [None] * n_even
    parts = {n: [None] * W[n].shape[0] for n in BIG_NAMES if n != "dn_conv_w"}
    for l in reversed(range(L)):
        j = l // 2
        x0, x1, x2, s1, sm, s2 = saved[l]
        dy, dg, dwgu, dwd = ffn_bwd(x2, norm_ffn2[l:l + 1], wgu2[l], wd2[l], s2, dy, f"l{l}f2")
        gsmall["norm_ffn2"][l] = dg[0]
        fulls, faxes, fnames = [dwgu, dwd], [1, 0], ["ffn2_w_gu", "ffn2_w_down"]
        if l % 2 == 0:
            dy, dwin, dwout, sg = even_mixer_bwd(x1, norm_mix[l:l + 1], wie[j], woe[j], convw[j], dn_norm_g[j:j + 1],
                                                 fox_q_norm_g[j:j + 1], fox_k_norm_g[j:j + 1], sm, dy, f"l{l}mx")
            gconv[j] = sg.pop("dn_conv_w")
            for k_, v_ in sg.items():
                gsmall[k_][l if k_ == "norm_mix" else j] = v_
            fulls += [dwin, dwout]
            faxes += [0, 0]
            fnames += ["w_in_even", "w_out_even"]
        else:
            dy, dwin, dwout, sg = odd_mixer_bwd(x1, norm_mix[l:l + 1], wio[j], woo[j], sm, dy, f"l{l}mx")
            gsmall["norm_mix"][l] = sg["norm_mix"]
            fulls += [dwin, dwout]
            faxes += [1, 0]
            fnames += ["w_in_odd", "w_out_odd"]
        dy, dg, dwgu, dwd = ffn_bwd(x0, norm_ffn1[l:l + 1], wgu1[l], wd1[l], s1, dy, f"l{l}f1")
        gsmall["norm_ffn1"][l] = dg[0]
        fulls += [dwgu, dwd]
        faxes += [1, 0]
        fnames += ["ffn1_w_gu", "ffn1_w_down"]
        recv = scatter_parts(fulls, faxes, f"rs_l{l}")
        for nm, p in zip(fnames, recv):
            idx = l if nm.startswith("ffn") else j
            parts[nm][idx] = sum_parts(p.reshape(NDEV, -1, p.shape[-1]), f"sum_{nm}_{idx}").reshape(p.shape[1:])
    grad_x = dy[None]

    small_list = [jnp.stack(gsmall[n]) for n in SMALL_NAMES] + [jnp.stack(gconv), loss_row[0, :1]]
    packed = _pack_small(small_list)
    gathered = all_gather([packed], [0], "ag_small")[0]
    summed = sum_parts(gathered.reshape(NDEV, packed.shape[0], LANES), "sum_small")
    small_sum = _unpack_small(summed, small_list)
    G = dict(zip(SMALL_NAMES, small_sum[:len(SMALL_NAMES)]))
    conv_full = small_sum[len(SMALL_NAMES)]
    cwid = dn_conv_w.shape[2]
    G["dn_conv_w"] = lax.dynamic_slice_in_dim(conv_full, me * cwid, cwid, axis=2)
    loss = small_sum[-1][0]

    def unprep_gu(g):
        return g.reshape(L, D, 2, hp)[..., :hs].reshape(L, D, 2 * hs)

    G["ffn1_w_gu"] = unprep_gu(jnp.stack(parts["ffn1_w_gu"]))
    G["ffn2_w_gu"] = unprep_gu(jnp.stack(parts["ffn2_w_gu"]))
    G["ffn1_w_down"] = jnp.stack(parts["ffn1_w_down"])[:, :hs]
    G["ffn2_w_down"] = jnp.stack(parts["ffn2_w_down"])[:, :hs]
    G["w_in_even"] = _even_from_mine(jnp.stack(parts["w_in_even"]), D)
    G["w_out_even"] = jnp.stack(parts["w_out_even"])
    G["w_in_odd"] = jnp.stack(parts["w_in_odd"])
    G["w_out_odd"] = jnp.stack(parts["w_out_odd"])

    delta, new_m, new_v = {}, {}, {}
    for n in BIG_NAMES:
        shp = W[n].shape
        two = lambda a: a.reshape(-1, shp[-1])
        d_, m_, v_ = adamw(two(W[n]), two(G[n]), two(M[n]), two(V[n]), f"adamw_{n}")
        delta[n], new_m[n], new_v[n] = d_.reshape(shp), m_.reshape(shp), v_.reshape(shp)
    sw = [W[n] for n in SMALL_NAMES]
    d_, m_, v_ = adamw(_pack_small(sw), _pack_small([G[n] for n in SMALL_NAMES]), _pack_small([M[n] for n in SMALL_NAMES]),
                       _pack_small([V[n] for n in SMALL_NAMES]), "adamw_small")
    for n, a, b, c_ in zip(SMALL_NAMES, _unpack_small(d_, sw), _unpack_small(m_, sw), _unpack_small(v_, sw)):
        delta[n], new_m[n], new_v[n] = a, b, c_

    return (loss, grad_x, *[G[n] for n in WEIGHT_NAMES], *[delta[n] for n in WEIGHT_NAMES],
            *[new_m[n] for n in WEIGHT_NAMES], *[new_v[n] for n in WEIGHT_NAMES])
```

```python
import functools
import math

import jax
import jax.numpy as jnp
from jax import lax
from jax.experimental import pallas as pl
from jax.experimental.pallas import tpu as pltpu

f32 = jnp.float32
bf16 = jnp.bfloat16
SDS = jax.ShapeDtypeStruct

HEAD_DIM = 128
LANES = 128
CONV_WIDTH = 4
DN_CHUNK = 64
EPS = 1e-6
NDEV = 8
VMEM_LIMIT_BYTES = 56 * 1024 * 1024
HI = lax.Precision.HIGHEST
ADAMW_TILE_ELEMS = 384 * 1024

ADAM_LR = 0.001
ADAM_B1 = 0.9
ADAM_B2 = 0.999
ADAM_EPS = 1e-08
ADAM_WD = 0.01
ADAM_STEP = 10

NN = (((1,), (0,)), ((), ()))
NT = (((1,), (1,)), ((), ()))
TN = (((0,), (0,)), ((), ()))


def _me_and_peers():
    x, y, c = lax.axis_index("x"), lax.axis_index("y"), lax.axis_index("c")
    me = 4 * x + 2 * y + c
    peers = []
    for r in range(1, NDEV):
        px = 1 - x if (r >> 2) & 1 else x
        py = 1 - y if (r >> 1) & 1 else y
        pc = 1 - c if r & 1 else c
        peers.append(((px, py, pc), 4 * px + 2 * py + pc))
    return me, peers


def _slab(ref, axis, width, k):
    idx = [slice(None)] * len(ref.shape)
    idx[axis] = pl.ds(k * width, width)
    return ref.at[tuple(idx)]


class Comm:
    def __init__(self, kind, arrays, axes):
        self.kind, self.arrays, self.axes, self.n = kind, list(arrays), list(axes), len(arrays)

    def out_shape(self):
        out = []
        for a, ax in zip(self.arrays, self.axes):
            shp = list(a.shape)
            if self.kind == "ag":
                shp[ax] *= NDEV
                out.append(SDS(tuple(shp), a.dtype))
            else:
                shp[ax] //= NDEV
                out.append(SDS((NDEV, *shp), a.dtype))
        return out

    def sems(self):
        return [pltpu.SemaphoreType.DMA((self.n, NDEV - 1)), pltpu.SemaphoreType.DMA((self.n, NDEV - 1)),
                pltpu.SemaphoreType.DMA((self.n,))]

    def _src(self, ins, t, to_idx):
        if self.kind == "ag":
            return ins[t]
        return _slab(ins[t], self.axes[t], ins[t].shape[self.axes[t]] // NDEV, to_idx)

    def _dst(self, ins, outs, t, from_idx):
        if self.kind == "ag":
            return _slab(outs[t], self.axes[t], ins[t].shape[self.axes[t]], from_idx)
        return outs[t].at[from_idx]

    def _copies(self, ins, outs, sems, sending):
        send_sems, recv_sems, loc_sems = sems
        me, peers = _me_and_peers()
        local, remote = [], []
        for t in range(self.n):
            local.append(pltpu.make_async_copy(self._src(ins, t, me), self._dst(ins, outs, t, me), loc_sems.at[t]))
            for r, (peer, pidx) in enumerate(peers):
                remote.append(pltpu.make_async_remote_copy(
                    src_ref=self._src(ins, t, pidx), dst_ref=self._dst(ins, outs, t, me if sending else pidx),
                    send_sem=send_sems.at[t, r], recv_sem=recv_sems.at[t, r], device_id=peer,
                    device_id_type=pl.DeviceIdType.MESH))
        return local, remote

    def start(self, ins, outs, sems):
        local, remote = self._copies(ins, outs, sems, True)
        for cp in local + remote:
            cp.start()

    def wait(self, ins, outs, sems):
        local, remote = self._copies(ins, outs, sems, False)
        for cp in remote:
            cp.wait_recv()
            cp.wait_send()
        for cp in local:
            cp.wait()


def _call(body, name, grid, in_specs, out_specs, out_shape, scratch=(), comm=None):
    params = pltpu.CompilerParams(vmem_limit_bytes=VMEM_LIMIT_BYTES)
    if comm is None:
        return pl.pallas_call(body, name=name, grid=grid, in_specs=in_specs, out_specs=out_specs, out_shape=out_shape,
                              scratch_shapes=list(scratch), compiler_params=params)
    single = not isinstance(out_shape, (tuple, list))
    c_out_specs = [out_specs] if single else list(out_specs)
    c_out_shape = [out_shape] if single else list(out_shape)
    n_in, n_out, n_scr, n = len(in_specs), len(c_out_shape), len(scratch), comm.n
    anyspec = pl.BlockSpec(memory_space=pl.ANY)

    def wrapped(*refs):
        ins, refs = refs[:n_in], refs[n_in:]
        cins, refs = refs[:n], refs[n:]
        outs, refs = refs[:n_out], refs[n_out:]
        couts, refs = refs[:n], refs[n:]
        scr, sems = refs[:n_scr], refs[n_scr:]
        first = functools.reduce(lambda a, b: a & b, [pl.program_id(d) == 0 for d in range(len(grid))], True)
        last = functools.reduce(lambda a, b: a & b, [pl.program_id(d) == grid[d] - 1 for d in range(len(grid))], True)
        if grid:
            pl.when(first)(lambda: comm.start(cins, couts, sems))
            body(*ins, *outs, *scr)
            pl.when(last)(lambda: comm.wait(cins, couts, sems))
        else:
            comm.start(cins, couts, sems)
            if body is not None:
                body(*ins, *outs, *scr)
            comm.wait(cins, couts, sems)

    call = pl.pallas_call(
        wrapped, name=name, grid=grid, in_specs=list(in_specs) + [anyspec] * n, out_specs=c_out_specs + [anyspec] * n,
        out_shape=c_out_shape + comm.out_shape(), scratch_shapes=list(scratch) + comm.sems(), compiler_params=params)

    def run(*args):
        res = call(*args, *comm.arrays)
        mine = res[:n_out]
        return (mine[0] if single else tuple(mine)), list(res[n_out:])

    return run


def _tile(n, want, align=8):
    if n <= want:
        return n
    for t in range(want // align * align, 0, -align):
        if n % t == 0:
            return t
    return n


def _dot(a, b, dims=NN, precision=None):
    return lax.dot_general(a, b, dims, preferred_element_type=f32, precision=precision)


def _logsig(x):
    return jnp.minimum(x, 0.0) - jnp.log(1.0 + jnp.exp(-jnp.abs(x)))


def _softplus(x):
    return jnp.maximum(x, 0.0) + jnp.log(1.0 + jnp.exp(-jnp.abs(x)))


def _rms(x, g):
    return x * lax.rsqrt(jnp.mean(x * x, axis=-1, keepdims=True) + EPS) * g


def _lane_pick(blk, lane_idx):
    lane = lax.broadcasted_iota(jnp.int32, (1, LANES), 1)
    return jnp.sum(jnp.where(lane == lane_idx, blk, 0.0), axis=1, keepdims=True)


def mm(a, b, mode, out_dtype, name, tm=512, tn=512, res=None, scale=1.0):
    if mode == "nn":
        (M, K), (_, N) = a.shape, b.shape
    elif mode == "nt":
        (M, K), (N, _) = a.shape, b.shape
    else:
        (K, M), (_, N) = a.shape, b.shape
    tm, tn = _tile(M, tm, LANES), _tile(N, tn, LANES)
    a_spec = pl.BlockSpec((K, tm), lambda j, i: (0, i)) if mode == "tn" else pl.BlockSpec((tm, K), lambda j, i: (i, 0))
    b_spec = pl.BlockSpec((tn, K), lambda j, i: (j, 0)) if mode == "nt" else pl.BlockSpec((K, tn), lambda j, i: (0, j))
    dims = {"nn": NN, "nt": NT, "tn": TN}[mode]
    o_spec = pl.BlockSpec((tm, tn), lambda j, i: (i, j))

    def body(*refs):
        acc = _dot(refs[0][...].astype(bf16), refs[1][...].astype(bf16), dims)
        if scale != 1.0:
            acc = acc * scale
        if res is not None:
            acc = acc + refs[2][...]
        refs[-1][...] = acc.astype(out_dtype)

    ins, specs = [a, b], [a_spec, b_spec]
    if res is not None:
        ins.append(res)
        specs.append(o_spec)
    return _call(body, name, (N // tn, M // tm), specs, o_spec, SDS((M, N), out_dtype))(*ins)


def _rowspec(tm, w, cb=0):
    return pl.BlockSpec((tm, w), lambda i: (i, cb))


def _bspec(w):
    return pl.BlockSpec((1, w), lambda i: (0, 0))


def rms_fwd(x, g, name):
    S, D = x.shape
    tm = _tile(S, 512)

    def body(x_ref, g_ref, h_ref):
        h_ref[...] = _rms(x_ref[...], g_ref[...]).astype(bf16)

    return _call(body, name, (S // tm,), [_rowspec(tm, D), _bspec(D)], _rowspec(tm, D), SDS((S, D), bf16))(x, g)


def rms_bwd(x, g, dh, dres, name):
    S, D = x.shape
    tm = _tile(S, 256)

    def body(x_ref, g_ref, dh_ref, dres_ref, dx_ref, dg_ref):
        _, vjp = jax.vjp(_rms, x_ref[...], g_ref[...])
        dx, dg = vjp(dh_ref[...])
        dx_ref[...] = dres_ref[...] + dx

        @pl.when(pl.program_id(0) == 0)
        def _():
            dg_ref[...] = jnp.zeros_like(dg_ref)

        dg_ref[...] += dg

    return _call(body, name, (S // tm,), [_rowspec(tm, D), _bspec(D), _rowspec(tm, D), _rowspec(tm, D)],
                 (_rowspec(tm, D), _bspec(D)), (SDS((S, D), f32), SDS((1, D), f32)))(x, g, dh, dres)


def _swiglu(g, u):
    return g * jax.nn.sigmoid(g) * u


def swiglu_fwd(gu, name):
    S, W2 = gu.shape
    W = W2 // 2
    tm = _tile(S, 256)

    def body(g_ref, u_ref, a_ref):
        a_ref[...] = _swiglu(g_ref[...].astype(f32), u_ref[...].astype(f32)).astype(bf16)

    return _call(body, name, (S // tm,), [_rowspec(tm, W, 0), _rowspec(tm, W, 1)], _rowspec(tm, W), SDS((S, W), bf16))(gu, gu)


def swiglu_bwd(gu, da, scale, name):
    S, W2 = gu.shape
    W = W2 // 2
    tm = _tile(S, 256)

    def body(g_ref, u_ref, da_ref, o_ref):
        _, vjp = jax.vjp(_swiglu, g_ref[...].astype(f32), u_ref[...].astype(f32))
        dg, du = vjp(da_ref[...].astype(f32) * scale)
        o_ref[:, :W] = dg.astype(bf16)
        o_ref[:, W:] = du.astype(bf16)

    return _call(body, name, (S // tm,), [_rowspec(tm, W, 0), _rowspec(tm, W, 1), _rowspec(tm, W)], _rowspec(tm, W2),
                 SDS((S, W2), bf16))(gu, gu, da)


def loss_head(y, t, name):
    S, D = y.shape
    tm = _tile(S, 512)

    def body(y_ref, t_ref, dy_ref, l_ref):
        e = y_ref[...] - t_ref[...]
        dy_ref[...] = e * (1.0 / D)

        @pl.when(pl.program_id(0) == 0)
        def _():
            l_ref[...] = jnp.zeros_like(l_ref)

        l_ref[...] += jnp.sum(jnp.sum(e * e, axis=1, keepdims=True), axis=0, keepdims=True) * (0.5 / D)

    return _call(body, name, (S // tm,), [_rowspec(tm, D), _rowspec(tm, D)], (_rowspec(tm, D), _bspec(LANES)),
                 (SDS((S, D), f32), SDS((1, LANES), f32)))(y, t)


def ffn_fwd(x, g, wgu, wd, tag):
    h = rms_fwd(x, g, f"{tag}_rms")
    gu = mm(h, wgu, "nn", bf16, f"{tag}_gu", tm=1024, tn=768)
    a = swiglu_fwd(gu, f"{tag}_act")
    xo = mm(a, wd, "nn", f32, f"{tag}_down", tm=512, tn=512, res=x, scale=0.5)
    return xo, (h, gu, a)


def ffn_bwd(x, g, wgu, wd, saved, dy, tag):
    h, gu, a = saved
    da = mm(dy, wd, "nt", bf16, f"{tag}_bda", tm=512, tn=768)
    dgu = swiglu_bwd(gu, da, 0.5, f"{tag}_bact")
    dwd = mm(a, dy, "tn", bf16, f"{tag}_bwd", tm=512, tn=512, scale=0.5)
    dwgu = mm(h, dgu, "tn", bf16, f"{tag}_bwgu", tm=512, tn=768)
    dh = mm(dgu, wgu, "nt", f32, f"{tag}_bdh", tm=512, tn=512)
    dx, dg = rms_bwd(x, g, dh, dy, f"{tag}_brms")
    return dx, dg, dwgu, dwd


def _split_bf16(x):
    hi = x.astype(bf16)
    lo = (x - hi.astype(f32)).astype(bf16)
    return hi, lo


def _sb_masks(tq):
    r = lax.broadcasted_iota(jnp.int32, (tq, tq), 0)
    c = lax.broadcasted_iota(jnp.int32, (tq, tq), 1)
    return c < r, (r > c).astype(bf16), (r < c).astype(bf16)


def sb_fwd(qkv, nh, name, tq=256, comm=None):
    S = qkv.shape[0]
    tq = _tile(S, tq)
    scale = HEAD_DIM ** -0.5

    def body(q_ref, k_ref, v_ref, o_ref):
        i = pl.program_id(1)
        q = q_ref[...]
        strict, after, _ = _sb_masks(tq)

        def tile(kb, carry, diag):
            r, acc = carry
            off = pl.multiple_of(kb * tq, tq)
            k = k_ref[pl.ds(off, tq), :]
            v = v_ref[pl.ds(off, tq), :]
            z = _dot(q, k, NT) * scale
            ls = _logsig(z)
            lm = ls - z
            if diag:
                lm = jnp.where(strict, lm, 0.0)
            hi, lo = _split_bf16(lm)
            a = jnp.exp(ls + (_dot(hi, after) + _dot(lo, after)) + r)
            if diag:
                a = jnp.where(strict, a, 0.0)
            return r + jnp.sum(lm, axis=1, keepdims=True), acc + _dot(a.astype(bf16), v)

        carry = tile(i, (jnp.zeros((tq, 1), f32), jnp.zeros((tq, HEAD_DIM), f32)), True)
        carry = lax.fori_loop(0, i, lambda t, c: tile(i - 1 - t, c, False), carry)
        o_ref[...] = carry[1].astype(o_ref.dtype)

    blk = lambda cb0: pl.BlockSpec((tq, HEAD_DIM), lambda h, i: (i, cb0 + h))
    full = lambda cb0: pl.BlockSpec((S, HEAD_DIM), lambda h, i: (0, cb0 + h))
    return _call(body, name, (nh, S // tq), [blk(0), full(nh), full(2 * nh)], blk(0),
                 SDS((S, nh * HEAD_DIM), bf16), comm=comm)(qkv, qkv, qkv)


def sb_bwd(qkv, do, nh, name, tq=256, comm=None):
    S = qkv.shape[0]
    tq = _tile(S, tq)
    nq = S // tq
    scale = HEAD_DIM ** -0.5

    def body(q_ref, k_ref, v_ref, do_ref, dq_ref, dk_ref, dv_ref, tc_ref):
        i = pl.program_id(1)

        @pl.when(i == 0)
        def _():
            dk_ref[...] = jnp.zeros_like(dk_ref)
            dv_ref[...] = jnp.zeros_like(dv_ref)

        q = q_ref[...]
        do_ = do_ref[...].astype(bf16)
        strict, after, before = _sb_masks(tq)

        def zparts(kb, diag):
            off = pl.multiple_of(kb * tq, tq)
            k = k_ref[pl.ds(off, tq), :]
            z = _dot(q, k, NT) * scale
            ls = _logsig(z)
            lm = ls - z
            if diag:
                lm = jnp.where(strict, lm, 0.0)
            return off, k, ls, lm

        def sweep1(kb, r, diag):
            tc_ref[kb] = jnp.broadcast_to(r, (tq, LANES))
            _, _, _, lm = zparts(kb, diag)
            return r + jnp.sum(lm, axis=1, keepdims=True)

        r = sweep1(i, jnp.zeros((tq, 1), f32), True)
        lax.fori_loop(0, i, lambda t, r: sweep1(i - 1 - t, r, False), r)

        def sweep2(kb, carry, diag):
            pe, dq = carry
            off, k, ls, lm = zparts(kb, diag)
            v = v_ref[pl.ds(off, tq), :]
            hi, lo = _split_bf16(lm)
            a = jnp.exp(ls + (_dot(hi, after) + _dot(lo, after)) + tc_ref[kb][:, :1])
            if diag:
                a = jnp.where(strict, a, 0.0)
            e = _dot(do_, v, NT) * a
            ehi, elo = _split_bf16(e)
            cum_e = pe + (_dot(ehi, before) + _dot(elo, before))
            sig = jnp.exp(ls)
            dz = e * (1.0 - sig) - cum_e * sig
            if diag:
                dz = jnp.where(strict, dz, 0.0)
            dzb = (dz * scale).astype(bf16)
            dk_ref[pl.ds(off, tq), :] += _dot(dzb, q, TN)
            dv_ref[pl.ds(off, tq), :] += _dot(a.astype(bf16), do_, TN)
            return pe + jnp.sum(e, axis=1, keepdims=True), dq + _dot(dzb, k)

        carry = (jnp.zeros((tq, 1), f32), jnp.zeros((tq, HEAD_DIM), f32))
        carry = lax.fori_loop(0, i, lambda kb, c: sweep2(kb, c, False), carry)
        carry = sweep2(i, carry, True)
        dq_ref[...] = carry[1]

    blk = lambda cb0: pl.BlockSpec((tq, HEAD_DIM), lambda h, i: (i, cb0 + h))
    full = lambda cb0: pl.BlockSpec((S, HEAD_DIM), lambda h, i: (0, cb0 + h))
    sh = SDS((S, nh * HEAD_DIM), f32)
    return _call(body, name, (nh, nq), [blk(0), full(nh), full(2 * nh), blk(0)], (blk(0), full(0), full(0)),
                 (sh, sh, sh), scratch=[pltpu.VMEM((nq, tq, LANES), f32)], comm=comm)(qkv, qkv, qkv, do)


def fox_fwd(fq, fk, fv, c, cT, nh, lane0, name, tq=256, comm=None):
    S = fq.shape[0]
    tq = _tile(S, tq)
    scale = HEAD_DIM ** -0.5

    def body(q_ref, k_ref, v_ref, c_ref, ct_ref, o_ref, lse_ref):
        h = pl.program_id(0)
        i = pl.program_id(1)
        q = q_ref[...]
        ci = _lane_pick(c_ref[...], lane0 + h)
        r = lax.broadcasted_iota(jnp.int32, (tq, tq), 0)
        cc = lax.broadcasted_iota(jnp.int32, (tq, tq), 1)
        causal = cc <= r

        def tile(kb, carry, diag):
            m, l, acc = carry
            off = pl.multiple_of(kb * tq, tq)
            k = k_ref[pl.ds(off, tq), :]
            v = v_ref[pl.ds(off, tq), :]
            cj = ct_ref[pl.ds(lane0 % 8 + h, 1), pl.ds(off, tq)]
            s = _dot(q, k, NT) * scale + (ci - cj)
            if diag:
                s = jnp.where(causal, s, -jnp.inf)
            m2 = jnp.maximum(m, jnp.max(s, axis=1, keepdims=True))
            p = jnp.exp(s - m2)
            al = jnp.exp(m - m2)
            return m2, al * l + jnp.sum(p, axis=1, keepdims=True), al * acc + _dot(p.astype(bf16), v)

        carry = (jnp.full((tq, 1), -jnp.inf, f32), jnp.zeros((tq, 1), f32), jnp.zeros((tq, HEAD_DIM), f32))
        carry = tile(i, carry, True)
        m, l, acc = lax.fori_loop(0, i, lambda kb, cr: tile(kb, cr, False), carry)
        o_ref[...] = acc / l
        lse_ref[...] = jnp.broadcast_to(m + jnp.log(l), (tq, HEAD_DIM))

    blk = pl.BlockSpec((tq, HEAD_DIM), lambda h, i: (i, h))
    full = pl.BlockSpec((S, HEAD_DIM), lambda h, i: (0, h))
    cspec = pl.BlockSpec((tq, LANES), lambda h, i: (i, 0))
    ctspec = pl.BlockSpec((8, S), lambda h, i: (lane0 // 8, 0))
    sh = SDS((S, nh * HEAD_DIM), f32)
    return _call(body, name, (nh, S // tq), [blk, full, full, cspec, ctspec], (blk, blk), (sh, sh), comm=comm)(fq, fk, fv, c, cT)


def fox_bwd(fq, fk, fv, c, cT, o, lse, do, nh, lane0, name, tq=256, comm=None):
    S = fq.shape[0]
    tq = _tile(S, tq)
    scale = HEAD_DIM ** -0.5

    def body(q_ref, k_ref, v_ref, c_ref, ct_ref, o_ref, lse_ref, do_ref, dq_ref, dk_ref, dv_ref, dct_ref):
        h = pl.program_id(0)
        i = pl.program_id(1)

        @pl.when(i == 0)
        def _():
            dk_ref[...] = jnp.zeros_like(dk_ref)
            dv_ref[...] = jnp.zeros_like(dv_ref)

        @pl.when((i == 0) & (h == 0))
        def _():
            dct_ref[...] = jnp.zeros_like(dct_ref)

        q = q_ref[...]
        dof = do_ref[...]
        do_ = dof.astype(bf16)
        ci = _lane_pick(c_ref[...], lane0 + h)
        lse_ = lse_ref[...][:, :1]
        dl = jnp.sum(dof * o_ref[...], axis=1, keepdims=True)
        r = lax.broadcasted_iota(jnp.int32, (tq, tq), 0)
        cc = lax.broadcasted_iota(jnp.int32, (tq, tq), 1)
        causal = cc <= r

        def tile(kb, carry, diag):
            dq, rs = carry
            off = pl.multiple_of(kb * tq, tq)
            k = k_ref[pl.ds(off, tq), :]
            v = v_ref[pl.ds(off, tq), :]
            cj = ct_ref[pl.ds(lane0 % 8 + h, 1), pl.ds(off, tq)]
            s = _dot(q, k, NT) * scale + (ci - cj)
            p = jnp.exp(s - lse_)
            if diag:
                p = jnp.where(causal, p, 0.0)
            ds = p * (_dot(do_, v, NT) - dl)
            dsb = (ds * scale).astype(bf16)
            dk_ref[pl.ds(off, tq), :] += _dot(dsb, q, TN)
            dv_ref[pl.ds(off, tq), :] += _dot(p.astype(bf16), do_, TN)
            dct_ref[pl.ds(lane0 + h, 1), pl.ds(off, tq)] -= jnp.sum(ds, axis=0, keepdims=True)
            return dq + _dot(dsb, k), rs + jnp.sum(ds, axis=1, keepdims=True)

        carry = (jnp.zeros((tq, HEAD_DIM), f32), jnp.zeros((tq, 1), f32))
        carry = lax.fori_loop(0, i, lambda kb, cr: tile(kb, cr, False), carry)
        dq, rs = tile(i, carry, True)
        dq_ref[...] = dq
        dct_ref[pl.ds(lane0 + h, 1), pl.ds(pl.multiple_of(i * tq, tq), tq)] += rs.T

    blk = pl.BlockSpec((tq, HEAD_DIM), lambda h, i: (i, h))
    full = pl.BlockSpec((S, HEAD_DIM), lambda h, i: (0, h))
    cspec = pl.BlockSpec((tq, LANES), lambda h, i: (i, 0))
    ctspec = pl.BlockSpec((8, S), lambda h, i: (lane0 // 8, 0))
    dctspec = pl.BlockSpec((LANES, S), lambda h, i: (0, 0))
    sh = SDS((S, nh * HEAD_DIM), f32)
    return _call(body, name, (nh, S // tq), [blk, full, full, cspec, ctspec, blk, blk, blk], (blk, full, full, dctspec),
                 (sh, sh, sh, SDS((LANES, S), f32)), comm=comm)(fq, fk, fv, c, cT, o, lse, do)


def gates_fwd(proj, small_cb, fbias_row, name, tm=256):
    S = proj.shape[0]
    tm = _tile(S, tm)

    def body(sm_ref, fb_ref, c_ref, ct_ref, carry_ref):
        @pl.when(pl.program_id(0) == 0)
        def _():
            carry_ref[...] = jnp.zeros_like(carry_ref)

        lf = _logsig(sm_ref[...] + fb_ref[...])
        r = lax.broadcasted_iota(jnp.int32, (tm, tm), 0)
        cc = lax.broadcasted_iota(jnp.int32, (tm, tm), 1)
        c = _dot((r >= cc).astype(f32), lf, NN, HI) + carry_ref[...]
        carry_ref[...] += jnp.sum(lf, axis=0, keepdims=True)
        c_ref[...] = c
        ct_ref[...] = c.T

    return _call(body, name, (S // tm,), [_rowspec(tm, LANES, small_cb), _bspec(LANES)],
                 (_rowspec(tm, LANES), pl.BlockSpec((LANES, tm), lambda i: (0, i))),
                 (SDS((S, LANES), f32), SDS((LANES, S), f32)), scratch=[pltpu.VMEM((1, LANES), f32)])(proj, fbias_row)


def gates_bwd(proj, small_cb, fbias_row, dcT, dsm_dn, lane0, nh, name, tm=256):
    S = proj.shape[0]
    tm = _tile(S, tm)
    nt = S // tm

    def body(sm_ref, fb_ref, dct_ref, dsm_ref, o_ref, dfb_ref, carry_ref):
        @pl.when(pl.program_id(0) == 0)
        def _():
            carry_ref[...] = jnp.zeros_like(carry_ref)
            dfb_ref[...] = jnp.zeros_like(dfb_ref)

        dc = dct_ref[...].T
        r = lax.broadcasted_iota(jnp.int32, (tm, tm), 0)
        cc = lax.broadcasted_iota(jnp.int32, (tm, tm), 1)
        dlf = _dot((r <= cc).astype(f32), dc, NN, HI) + carry_ref[...]
        carry_ref[...] += jnp.sum(dc, axis=0, keepdims=True)
        lane = lax.broadcasted_iota(jnp.int32, (1, LANES), 1)
        dpre = jnp.where((lane >= lane0) & (lane < lane0 + nh), dlf * jax.nn.sigmoid(-(sm_ref[...] + fb_ref[...])), 0.0)
        o_ref[...] = dsm_ref[...] + dpre
        dfb_ref[...] += jnp.sum(dpre, axis=0, keepdims=True)

    rev = lambda i: nt - 1 - i
    return _call(body, name, (nt,),
                 [pl.BlockSpec((tm, LANES), lambda i: (rev(i), small_cb)), _bspec(LANES),
                  pl.BlockSpec((LANES, tm), lambda i: (0, rev(i))), pl.BlockSpec((tm, LANES), lambda i: (rev(i), 0))],
                 (pl.BlockSpec((tm, LANES), lambda i: (rev(i), 0)), _bspec(LANES)),
                 (SDS((S, LANES), f32), SDS((1, LANES), f32)), scratch=[pltpu.VMEM((1, LANES), f32)])(proj, fbias_row, dcT, dsm_dn)


PAD_ROWS = 8


def conv_fwd(proj, w, width, name):
    S = proj.shape[0]
    cw = 256 if width % 256 == 0 else 128

    def body(x_ref, w_ref, y_ref, pad_ref):
        pad_ref[pl.ds(0, PAD_ROWS), :] = jnp.zeros((PAD_ROWS, cw), f32)
        pad_ref[pl.ds(PAD_ROWS, S), :] = x_ref[...]
        y = jnp.zeros((S, cw), f32)
        for i in range(CONV_WIDTH):
            y = y + pad_ref[pl.ds(PAD_ROWS - (CONV_WIDTH - 1) + i, S), :] * w_ref[pl.ds(i, 1), :]
        y_ref[...] = y * jax.nn.sigmoid(y)

    spec = pl.BlockSpec((S, cw), lambda j: (0, j))
    return _call(body, name, (width // cw,), [spec, pl.BlockSpec((CONV_WIDTH, cw), lambda j: (0, j))], spec,
                 SDS((S, width), f32), scratch=[pltpu.VMEM((S + PAD_ROWS, cw), f32)])(proj, w)


def conv_bwd(proj, w, dy, name):
    S, width = dy.shape
    cw = 256 if width % 256 == 0 else 128

    def body(x_ref, w_ref, dy_ref, dx_ref, dw_ref, xpad_ref, dpad_ref):
        xpad_ref[pl.ds(0, PAD_ROWS), :] = jnp.zeros((PAD_ROWS, cw), f32)
        xpad_ref[pl.ds(PAD_ROWS, S), :] = x_ref[...]
        y = jnp.zeros((S, cw), f32)
        for i in range(CONV_WIDTH):
            y = y + xpad_ref[pl.ds(PAD_ROWS - (CONV_WIDTH - 1) + i, S), :] * w_ref[pl.ds(i, 1), :]
        sg = jax.nn.sigmoid(y)
        dpre = dy_ref[...] * (sg * (1.0 + y * (1.0 - sg)))
        dpad_ref[pl.ds(S, PAD_ROWS), :] = jnp.zeros((PAD_ROWS, cw), f32)
        dpad_ref[pl.ds(0, S), :] = dpre
        dx = jnp.zeros((S, cw), f32)
        for i in range(CONV_WIDTH):
            dx = dx + dpad_ref[pl.ds(CONV_WIDTH - 1 - i, S), :] * w_ref[pl.ds(i, 1), :]
            dw_ref[pl.ds(i, 1), :] = jnp.sum(dpre * xpad_ref[pl.ds(PAD_ROWS - (CONV_WIDTH - 1) + i, S), :], axis=0, keepdims=True)
        dx_ref[...] = dx

    spec = pl.BlockSpec((S, cw), lambda j: (0, j))
    wspec = pl.BlockSpec((CONV_WIDTH, cw), lambda j: (0, j))
    return _call(body, name, (width // cw,), [spec, wspec, spec], (spec, wspec),
                 (SDS((S, width), f32), SDS((CONV_WIDTH, width), f32)),
                 scratch=[pltpu.VMEM((S + PAD_ROWS, cw), f32), pltpu.VMEM((S + PAD_ROWS, cw), f32)])(proj, w, dy)


def _pdot_raw(a, b, dims, passes):
    if passes == 1:
        return _dot(a.astype(bf16), b.astype(bf16), dims)
    ah, al = _split_bf16(a)
    bh, bl = _split_bf16(b)
    return _dot(ah, bh, dims) + (_dot(ah, bl, dims) + _dot(al, bh, dims))


def _make_pdot(passes):
    @functools.partial(jax.custom_vjp, nondiff_argnums=(2,))
    def pdot(a, b, mode):
        return _pdot_raw(a, b, {"nn": NN, "nt": NT, "tn": TN}[mode], passes)

    def fwd(a, b, mode):
        return pdot(a, b, mode), (a, b)

    def bwd(mode, res, g):
        a, b = res
        if mode == "nn":
            return _pdot_raw(g, b, NT, passes), _pdot_raw(a, g, TN, passes)
        if mode == "nt":
            return _pdot_raw(g, b, NN, passes), _pdot_raw(g, a, TN, passes)
        return _pdot_raw(b, g, NT, passes), _pdot_raw(a, g, NN, passes)

    pdot.defvjp(fwd, bwd)
    return pdot


_dot1 = _make_pdot(1)
_dot3 = _make_pdot(3)


def _each(f, *lists):
    return [f(*xs) for xs in zip(*lists)]


def _dn_chunk(ndn, cq, ck, cv, small, alog, dtb, s0):
    C = cq[0].shape[0]
    hs = list(range(ndn))
    b = [_lane_pick(small, h) for h in hs]
    d = [_lane_pick(small, ndn + h) for h in hs]
    al = [_lane_pick(alog, h) for h in hs]
    dt = [_lane_pick(dtb, h) for h in hs]
    q = _each(lambda x: x * lax.rsqrt(jnp.sum(x * x, axis=1, keepdims=True) + EPS) * (HEAD_DIM ** -0.5), cq)
    k = _each(lambda x: x * lax.rsqrt(jnp.sum(x * x, axis=1, keepdims=True) + EPS), ck)
    beta = _each(jax.nn.sigmoid, b)
    g = _each(lambda al_, d_, dt_: -jnp.exp(al_) * _softplus(d_ + dt_), al, d, dt)
    r = lax.broadcasted_iota(jnp.int32, (C, C), 0)
    c = lax.broadcasted_iota(jnp.int32, (C, C), 1)
    incl, strict = r >= c, r > c
    inclf = incl.astype(f32)
    gc = _each(lambda g_: _dot3(inclf, g_, "nn"), g)
    decay = _each(lambda gc_: jnp.where(incl, jnp.exp(jnp.where(incl, gc_ - gc_.T, 0.0)), 0.0), gc)
    kb = _each(lambda k_, b_: k_ * b_, k, beta)
    a = _each(lambda kb_, k_, dec: jnp.where(strict, _dot3(kb_, k_, "nt") * dec, 0.0), kb, k, decay)
    eye = (r == c).astype(f32)
    t = _each(lambda a_: eye - a_, a)
    p = a
    for _ in range(int(math.log2(C)) - 1):
        p = _each(lambda p_: _dot3(p_, p_, "nn"), p)
        t = _each(lambda t_, p_: t_ + _dot3(t_, p_, "nn"), t, p)
    eg = _each(jnp.exp, gc)
    u = _each(lambda t_, v_, b_: _dot3(t_, v_ * b_, "nn"), t, cv, beta)
    w = _each(lambda t_, kb_, eg_: _dot3(t_, kb_ * eg_, "nn"), t, kb, eg)
    attn = _each(lambda q_, k_, dec: _dot1(q_, k_, "nt") * dec, q, k, decay)
    g_last = _each(lambda g_: jnp.sum(g_, axis=0, keepdims=True), g)
    v_new = _each(lambda u_, w_, s_: u_ - _dot1(w_, s_, "nn"), u, w, s0)
    o = _each(lambda q_, eg_, s_, at, vn: _dot1(q_ * eg_, s_, "nn") + _dot1(at, vn, "nn"), q, eg, s0, attn, v_new)
    s1 = _each(lambda s_, gl, k_, gc_, vn: s_ * jnp.exp(gl) + _dot1(k_ * jnp.exp(gl - gc_), vn, "tn"), s0, g_last, k, gc, v_new)
    return o, s1


def dn_fwd(cqkv, proj, small_cb, alog_row, dt_row, ndn, name, comm=None):
    S = cqkv.shape[0]
    C = DN_CHUNK
    nc = S // C
    half = ndn * HEAD_DIM

    def body(q_ref, k_ref, v_ref, sm_ref, al_ref, dt_ref, o_ref, ss_ref, s_ref):
        @pl.when(pl.program_id(0) == 0)
        def _():
            s_ref[...] = jnp.zeros_like(s_ref)

        sls = [slice(h * HEAD_DIM, (h + 1) * HEAD_DIM) for h in range(ndn)]
        s0 = [s_ref[h] for h in range(ndn)]
        for h in range(ndn):
            ss_ref[h, 0] = s0[h]
        o, s1 = _dn_chunk(ndn, [q_ref[:, sl] for sl in sls], [k_ref[:, sl] for sl in sls], [v_ref[:, sl] for sl in sls],
                          sm_ref[...], al_ref[...], dt_ref[...], s0)
        for h in range(ndn):
            o_ref[:, sls[h]] = o[h]
            s_ref[h] = s1[h]

    blk = lambda cb: pl.BlockSpec((C, half), lambda n: (n, cb))
    row = pl.BlockSpec((1, LANES), lambda n: (0, 0))
    return _call(body, name, (nc,),
                 [blk(0), blk(1), blk(2), pl.BlockSpec((C, LANES), lambda n: (n, small_cb)), row, row],
                 (blk(0), pl.BlockSpec((ndn, 1, HEAD_DIM, HEAD_DIM), lambda n: (0, n, 0, 0))),
                 (SDS((S, half), f32), SDS((ndn, nc, HEAD_DIM, HEAD_DIM), f32)),
                 scratch=[pltpu.VMEM((ndn, HEAD_DIM, HEAD_DIM), f32)], comm=comm)(cqkv, cqkv, cqkv, proj, alog_row, dt_row)


def dn_bwd(cqkv, proj, small_cb, alog_row, dt_row, ssave, do, ndn, name, comm=None):
    S = cqkv.shape[0]
    C = DN_CHUNK
    nc = S // C
    half = ndn * HEAD_DIM

    def body(q_ref, k_ref, v_ref, sm_ref, al_ref, dt_ref, ss_ref, do_ref, dqkv_ref, dsm_ref, dal_ref, ddt_ref, ds_ref):
        @pl.when(pl.program_id(0) == 0)
        def _():
            ds_ref[...] = jnp.zeros_like(ds_ref)
            dal_ref[...] = jnp.zeros_like(dal_ref)
            ddt_ref[...] = jnp.zeros_like(ddt_ref)

        sls = [slice(h * HEAD_DIM, (h + 1) * HEAD_DIM) for h in range(ndn)]
        _, vjp = jax.vjp(functools.partial(_dn_chunk, ndn), [q_ref[:, sl] for sl in sls], [k_ref[:, sl] for sl in sls],
                         [v_ref[:, sl] for sl in sls], sm_ref[...], al_ref[...], dt_ref[...], [ss_ref[h, 0] for h in range(ndn)])
        dq, dk, dv, dsm, dal, ddt, ds0 = vjp(([do_ref[:, sl] for sl in sls], [ds_ref[h] for h in range(ndn)]))
        for h in range(ndn):
            dqkv_ref[:, sls[h]] = dq[h]
            dqkv_ref[:, half + h * HEAD_DIM:half + (h + 1) * HEAD_DIM] = dk[h]
            dqkv_ref[:, 2 * half + h * HEAD_DIM:2 * half + (h + 1) * HEAD_DIM] = dv[h]
            ds_ref[h] = ds0[h]
        dsm_ref[...] = dsm
        dal_ref[...] += dal
        ddt_ref[...] += ddt

    rev = lambda n: nc - 1 - n
    blk = lambda cb: pl.BlockSpec((C, half), lambda n: (rev(n), cb))
    row = pl.BlockSpec((1, LANES), lambda n: (0, 0))
    return _call(body, name, (nc,),
                 [blk(0), blk(1), blk(2), pl.BlockSpec((C, LANES), lambda n: (rev(n), small_cb)), row, row,
                  pl.BlockSpec((ndn, 1, HEAD_DIM, HEAD_DIM), lambda n: (0, rev(n), 0, 0)), blk(0)],
                 (pl.BlockSpec((C, 3 * half), lambda n: (rev(n), 0)), pl.BlockSpec((C, LANES), lambda n: (rev(n), 0)), row, row),
                 (SDS((S, 3 * half), f32), SDS((S, LANES), f32), SDS((1, LANES), f32), SDS((1, LANES), f32)),
                 scratch=[pltpu.VMEM((ndn, HEAD_DIM, HEAD_DIM), f32)], comm=comm)(cqkv, cqkv, cqkv, proj, alog_row, dt_row, ssave, do)


def even_pre(proj, gq, gk, dfx, cb0, name):
    S = proj.shape[0]
    tm = _tile(S, 256)
    nh = dfx // HEAD_DIM

    def body(q_ref, k_ref, v_ref, gq_ref, gk_ref, fq_ref, fk_ref, fv_ref):
        for h in range(nh):
            sl = slice(h * HEAD_DIM, (h + 1) * HEAD_DIM)
            fq_ref[:, sl] = _rms(q_ref[:, sl], gq_ref[...]).astype(bf16)
            fk_ref[:, sl] = _rms(k_ref[:, sl], gk_ref[...]).astype(bf16)
        fv_ref[...] = v_ref[...].astype(bf16)

    sh = SDS((S, dfx), bf16)
    o = _rowspec(tm, dfx)
    return _call(body, name, (S // tm,),
                 [_rowspec(tm, dfx, cb0), _rowspec(tm, dfx, cb0 + 1), _rowspec(tm, dfx, cb0 + 2), _bspec(HEAD_DIM), _bspec(HEAD_DIM)],
                 (o, o, o), (sh, sh, sh))(proj, proj, proj, gq, gk)


def even_pre_bwd(proj, gq, gk, dfq, dfk, dfx, cb0, name):
    S = proj.shape[0]
    tm = _tile(S, 256)
    nh = dfx // HEAD_DIM

    def body(q_ref, k_ref, gq_ref, gk_ref, dfq_ref, dfk_ref, dq_ref, dk_ref, dgq_ref, dgk_ref):
        @pl.when(pl.program_id(0) == 0)
        def _():
            dgq_ref[...] = jnp.zeros_like(dgq_ref)
            dgk_ref[...] = jnp.zeros_like(dgk_ref)

        for h in range(nh):
            sl = slice(h * HEAD_DIM, (h + 1) * HEAD_DIM)
            _, vq = jax.vjp(_rms, q_ref[:, sl], gq_ref[...])
            dq, dgq = vq(dfq_ref[:, sl])
            _, vk = jax.vjp(_rms, k_ref[:, sl], gk_ref[...])
            dk, dgk = vk(dfk_ref[:, sl])
            dq_ref[:, sl] = dq
            dk_ref[:, sl] = dk
            dgq_ref[...] += dgq
            dgk_ref[...] += dgk

    sh = SDS((S, dfx), f32)
    o = _rowspec(tm, dfx)
    g = SDS((1, HEAD_DIM), f32)
    return _call(body, name, (S // tm,),
                 [_rowspec(tm, dfx, cb0), _rowspec(tm, dfx, cb0 + 1), _bspec(HEAD_DIM), _bspec(HEAD_DIM), o, o],
                 (o, o, _bspec(HEAD_DIM), _bspec(HEAD_DIM)), (sh, sh, g, g))(proj, proj, gq, gk, dfq, dfk)


def _dn_out(o, gate, gn):
    return _rms(o, gn) * (gate * jax.nn.sigmoid(gate))


def _fox_out(o, gate):
    return o * jax.nn.sigmoid(gate)


def even_post(o_dn, o_fox, proj, gn, half, cb_dn_gate, cb_fox_gate, name):
    S = proj.shape[0]
    tm = _tile(S, 256)
    nh = half // HEAD_DIM

    def body(od_ref, of_ref, gd_ref, gf_ref, gn_ref, o_ref):
        for h in range(nh):
            sl = slice(h * HEAD_DIM, (h + 1) * HEAD_DIM)
            o_ref[:, sl] = _dn_out(od_ref[:, sl], gd_ref[:, sl], gn_ref[...]).astype(bf16)
        o_ref[:, half:] = _fox_out(of_ref[...], gf_ref[...]).astype(bf16)

    return _call(body, name, (S // tm,),
                 [_rowspec(tm, half), _rowspec(tm, half), _rowspec(tm, half, cb_dn_gate), _rowspec(tm, half, cb_fox_gate), _bspec(HEAD_DIM)],
                 _rowspec(tm, 2 * half), SDS((S, 2 * half), bf16))(o_dn, o_fox, proj, proj, gn)


def even_post_bwd(o_dn, o_fox, proj, gn, dom, half, cb_dn_gate, cb_fox_gate, name):
    S = proj.shape[0]
    tm = _tile(S, 256)
    nh = half // HEAD_DIM

    def body(od_ref, of_ref, gd_ref, gf_ref, gn_ref, dod_ref, dof_ref, dd_ref, df_ref, dgd_ref, dgf_ref, dgn_ref):
        @pl.when(pl.program_id(0) == 0)
        def _():
            dgn_ref[...] = jnp.zeros_like(dgn_ref)

        for h in range(nh):
            sl = slice(h * HEAD_DIM, (h + 1) * HEAD_DIM)
            _, vjp = jax.vjp(_dn_out, od_ref[:, sl], gd_ref[:, sl], gn_ref[...])
            d_o, d_gate, d_gn = vjp(dod_ref[:, sl])
            dd_ref[:, sl] = d_o
            dgd_ref[:, sl] = d_gate
            dgn_ref[...] += d_gn
        _, vjp = jax.vjp(_fox_out, of_ref[...], gf_ref[...])
        d_o, d_gate = vjp(dof_ref[...])
        df_ref[...] = d_o
        dgf_ref[...] = d_gate

    sh = SDS((S, half), f32)
    o = _rowspec(tm, half)
    return _call(body, name, (S // tm,),
                 [o, o, _rowspec(tm, half, cb_dn_gate), _rowspec(tm, half, cb_fox_gate), _bspec(HEAD_DIM),
                  _rowspec(tm, half, 0), _rowspec(tm, half, 1)],
                 (o, o, o, o, _bspec(HEAD_DIM)), (sh, sh, sh, sh, SDS((1, HEAD_DIM), f32)))(o_dn, o_fox, proj, proj, gn, dom, dom)


def _pad_row(v, lane0=0):
    return jnp.zeros((1, LANES), f32).at[0, lane0:lane0 + v.shape[0]].set(v)


def _carried(res, comm):
    return res if comm is not None else (res, [])


def even_mixer_fwd(x, gmix, w_in, w_out, conv_w, a_log, dt_bias, gn, gq, gk, f_bias, tag, comm_fox=None, comm_dn=None):
    S, D = x.shape
    half = D // 2
    ndn = half // HEAD_DIM
    small_cb = 4 * D // LANES
    alog_row, dt_row, fb_row = _pad_row(a_log), _pad_row(dt_bias), _pad_row(f_bias, 2 * ndn)
    h = rms_fwd(x, gmix, f"{tag}_rms")
    proj = mm(h, w_in, "nn", f32, f"{tag}_in", tm=512, tn=1408)
    cqkv = conv_fwd(proj, conv_w, 3 * half, f"{tag}_conv")
    (o_dn, ssave), got_dn = _carried(dn_fwd(cqkv, proj, small_cb, alog_row, dt_row, ndn, f"{tag}_dn", comm=comm_dn), comm_dn)
    c, cT = gates_fwd(proj, small_cb, fb_row, f"{tag}_gates")
    fq, fk, fv = even_pre(proj, gq, gk, half, 4, f"{tag}_pre")
    (o_fox, lse), got_fox = _carried(fox_fwd(fq, fk, fv, c, cT, ndn, 2 * ndn, f"{tag}_fox", comm=comm_fox), comm_fox)
    om = even_post(o_dn, o_fox, proj, gn, half, 3, 7, f"{tag}_post")
    xo = mm(om, w_out, "nn", f32, f"{tag}_out", res=x)
    return xo, (h, proj, cqkv, o_dn, ssave, c, cT, fq, fk, fv, o_fox, lse, om, alog_row, dt_row, fb_row), got_fox, got_dn


def even_mixer_bwd(x, gmix, w_in, w_out, conv_w, gn, gq, gk, saved, dy, tag, comm_fox=None, comm_dn=None):
    S, D = x.shape
    half = D // 2
    ndn = half // HEAD_DIM
    small_cb = 4 * D // LANES
    h, proj, cqkv, o_dn, ssave, c, cT, fq, fk, fv, o_fox, lse, om, alog_row, dt_row, fb_row = saved
    dom = mm(dy, w_out, "nt", f32, f"{tag}_bdom")
    dw_out = mm(om, dy, "tn", bf16, f"{tag}_bwout")
    d_odn, d_ofox, d_dgate, d_fgate, d_gn = even_post_bwd(o_dn, o_fox, proj, gn, dom, half, 3, 7, f"{tag}_bpost")
    (dfq, dfk, dfv, dcT), got_fox = _carried(
        fox_bwd(fq, fk, fv, c, cT, o_fox, lse, d_ofox, ndn, 2 * ndn, f"{tag}_bfox", comm=comm_fox), comm_fox)
    dq_pre, dk_pre, d_gq, d_gk = even_pre_bwd(proj, gq, gk, dfq, dfk, half, 4, f"{tag}_bpre")
    (dcqkv, dsm_dn, d_alog, d_dt), got_dn = _carried(
        dn_bwd(cqkv, proj, small_cb, alog_row, dt_row, ssave, d_odn, ndn, f"{tag}_bdn", comm=comm_dn), comm_dn)
    d_conv_in, d_conv_w = conv_bwd(proj, conv_w, dcqkv, f"{tag}_bconv")
    d_small, d_fb = gates_bwd(proj, small_cb, fb_row, dcT, dsm_dn, 2 * ndn, ndn, f"{tag}_bgates")
    dproj = jnp.concatenate([d_conv_in, d_dgate, dq_pre, dk_pre, dfv, d_fgate, d_small], axis=1)
    dw_in = mm(h, dproj, "tn", bf16, f"{tag}_bwin", tn=384)
    dh = mm(dproj, w_in, "nt", f32, f"{tag}_bdh")
    dx, d_gmix = rms_bwd(x, gmix, dh, dy, f"{tag}_brms")
    small = dict(norm_mix=d_gmix[0], dn_conv_w=d_conv_w, dn_a_log=d_alog[0, :ndn], dn_dt_bias=d_dt[0, :ndn],
                 dn_norm_g=d_gn[0], fox_q_norm_g=d_gq[0], fox_k_norm_g=d_gk[0], fox_f_bias=d_fb[0, 2 * ndn:3 * ndn])
    return dx, dw_in, dw_out, small, got_fox, got_dn


def odd_mixer_fwd(x, gmix, w_in, w_out, tag, comm=None):
    S, D = x.shape
    nh = D // HEAD_DIM
    h = rms_fwd(x, gmix, f"{tag}_rms")
    qkv = mm(h, w_in, "nn", bf16, f"{tag}_in", tn=768)
    o, got = _carried(sb_fwd(qkv, nh, f"{tag}_sb", comm=comm), comm)
    xo = mm(o, w_out, "nn", f32, f"{tag}_out", res=x)
    return xo, (h, qkv, o), got


def odd_mixer_bwd(x, gmix, w_in, w_out, saved, dy, tag, comm=None):
    S, D = x.shape
    nh = D // HEAD_DIM
    h, qkv, o = saved
    do = mm(dy, w_out, "nt", f32, f"{tag}_bdo")
    dw_out = mm(o, dy, "tn", bf16, f"{tag}_bwout")
    (dq, dk, dv), got = _carried(sb_bwd(qkv, do, nh, f"{tag}_bsb", comm=comm), comm)
    dqkv = jnp.concatenate([dq, dk, dv], axis=1)
    dw_in = mm(h, dqkv, "tn", bf16, f"{tag}_bwin", tn=768)
    dh = mm(dqkv, w_in, "nt", f32, f"{tag}_bdh")
    dx, d_gmix = rms_bwd(x, gmix, dh, dy, f"{tag}_brms")
    return dx, dw_in, dw_out, dict(norm_mix=d_gmix[0]), got


def all_gather(shards, axes, name):
    return _call(None, name, (), [], [], [], comm=Comm("ag", shards, axes))()[1]


def scatter_parts(fulls, axes, name):
    return _call(None, name, (), [], [], [], comm=Comm("rs", fulls, axes))()[1]


def sum_parts(parts, name):
    _, R, C = parts.shape
    tm = _tile(R, 256)

    def body(p_ref, o_ref):
        acc = p_ref[0].astype(f32)
        for k in range(1, NDEV):
            acc = acc + p_ref[k].astype(f32)
        o_ref[...] = acc

    return _call(body, name, (R // tm,), [pl.BlockSpec((NDEV, tm, C), lambda i: (0, i, 0))], _rowspec(tm, C), SDS((R, C), f32))(parts)


def adamw(w, g, m, v, name):
    R, C = w.shape
    tm = _tile(R, max(8, min(512, (ADAMW_TILE_ELEMS // C) // 8 * 8)))
    c1 = 1.0 - ADAM_B1 ** ADAM_STEP
    c2 = 1.0 - ADAM_B2 ** ADAM_STEP

    def body(w_ref, g_ref, m_ref, v_ref, d_ref, nm_ref, nv_ref):
        g_ = g_ref[...]
        nm = ADAM_B1 * m_ref[...] + (1.0 - ADAM_B1) * g_
        nv = ADAM_B2 * v_ref[...] + (1.0 - ADAM_B2) * (g_ * g_)
        d_ref[...] = -ADAM_LR * ((nm / c1) / (jnp.sqrt(nv / c2) + ADAM_EPS) + ADAM_WD * w_ref[...])
        nm_ref[...] = nm
        nv_ref[...] = nv

    spec = _rowspec(tm, C)
    sh = SDS((R, C), f32)
    return _call(body, name, (R // tm,), [spec] * 4, (spec, spec, spec), (sh, sh, sh))(w, g, m, v)


def _pad_to(n, mult):
    return -(-n // mult) * mult


def _even_perm(D):
    half = D // 2
    ndn = half // HEAD_DIM
    o_gate = 3 * half
    o_b = o_gate + half
    o_a = o_b + ndn
    o_fq = o_a + ndn
    o_fpre = o_fq + 4 * half
    return half, ndn, o_b, o_a, o_fq, o_fpre


def _even_to_mine(w):
    D = (w.shape[-1] // 4) // LANES * LANES
    half, ndn, o_b, o_a, o_fq, o_fpre = _even_perm(D)
    small = jnp.concatenate([w[..., o_b:o_b + ndn], w[..., o_a:o_a + ndn], w[..., o_fpre:o_fpre + ndn]], axis=-1)
    small = jnp.pad(small, [(0, 0)] * (w.ndim - 1) + [(0, LANES - 3 * ndn)])
    return jnp.concatenate([w[..., :o_b], w[..., o_fq:o_fpre], small], axis=-1)


def _even_from_mine(w, D):
    half, ndn, o_b, o_a, o_fq, o_fpre = _even_perm(D)
    sm = w[..., 4 * D:]
    return jnp.concatenate([w[..., :o_b], sm[..., :ndn], sm[..., ndn:2 * ndn], w[..., o_b:4 * D], sm[..., 2 * ndn:3 * ndn]], axis=-1)


def _pack_small(parts):
    flat = jnp.concatenate([p.reshape(-1).astype(f32) for p in parts])
    n = flat.shape[0]
    return jnp.pad(flat, (0, _pad_to(n, 8 * LANES) - n)).reshape(-1, LANES)


def _unpack_small(packed, like):
    flat = packed.reshape(-1)
    out, off = [], 0
    for p in like:
        out.append(flat[off:off + p.size].reshape(p.shape))
        off += p.size
    return out


SMALL_NAMES = ("norm_ffn1", "norm_mix", "dn_a_log", "dn_dt_bias", "dn_norm_g", "fox_q_norm_g", "fox_k_norm_g", "fox_f_bias", "norm_ffn2")
BIG_NAMES = ("ffn1_w_gu", "ffn1_w_down", "w_in_even", "dn_conv_w", "w_out_even", "w_in_odd", "w_out_odd", "ffn2_w_gu", "ffn2_w_down")
WEIGHT_NAMES = ("norm_ffn1", "ffn1_w_gu", "ffn1_w_down", "norm_mix", "w_in_even", "dn_conv_w", "dn_a_log", "dn_dt_bias", "dn_norm_g",
                "fox_q_norm_g", "fox_k_norm_g", "fox_f_bias", "w_out_even", "w_in_odd", "w_out_odd", "norm_ffn2", "ffn2_w_gu", "ffn2_w_down")


def kernel(x, norm_ffn1, ffn1_w_gu, ffn1_w_down, norm_mix, w_in_even, dn_conv_w, dn_a_log, dn_dt_bias, dn_norm_g, fox_q_norm_g, fox_k_norm_g, fox_f_bias, w_out_even, w_in_odd, w_out_odd, norm_ffn2, ffn2_w_gu, ffn2_w_down, loss_target, m_norm_ffn1, m_ffn1_w_gu, m_ffn1_w_down, m_norm_mix, m_w_in_even, m_dn_conv_w, m_dn_a_log, m_dn_dt_bias, m_dn_norm_g, m_fox_q_norm_g, m_fox_k_norm_g, m_fox_f_bias, m_w_out_even, m_w_in_odd, m_w_out_odd, m_norm_ffn2, m_ffn2_w_gu, m_ffn2_w_down, v_norm_ffn1, v_ffn1_w_gu, v_ffn1_w_down, v_norm_mix, v_w_in_even, v_dn_conv_w, v_dn_a_log, v_dn_dt_bias, v_dn_norm_g, v_fox_q_norm_g, v_fox_k_norm_g, v_fox_f_bias, v_w_out_even, v_w_in_odd, v_w_out_odd, v_norm_ffn2, v_ffn2_w_gu, v_ffn2_w_down):
    args = dict(locals())
    W = {n: args[n] for n in WEIGHT_NAMES}
    M = {n: args["m_" + n] for n in WEIGHT_NAMES}
    V = {n: args["v_" + n] for n in WEIGHT_NAMES}
    xs = x[0]
    tgt = loss_target[0]
    S, D = xs.shape
    L = norm_ffn1.shape[0]
    n_even = w_in_even.shape[0]
    hs = ffn1_w_down.shape[1]
    hp = _pad_to(hs, LANES)
    me = 4 * lax.axis_index("x") + 2 * lax.axis_index("y") + lax.axis_index("c")

    def prep_gu(w):
        w = w.reshape(L, D, 2, hs)
        return jnp.pad(w, ((0, 0), (0, 0), (0, 0), (0, hp - hs))).reshape(L, D, 2 * hp).astype(bf16)

    def prep_down(w):
        return jnp.pad(w, ((0, 0), (0, hp - hs), (0, 0))).astype(bf16)

    sh_gu1, sh_d1, sh_gu2, sh_d2 = prep_gu(ffn1_w_gu), prep_down(ffn1_w_down), prep_gu(ffn2_w_gu), prep_down(ffn2_w_down)
    sh_ie, sh_oe = _even_to_mine(w_in_even).astype(bf16), w_out_even.astype(bf16)
    sh_io, sh_oo = w_in_odd.astype(bf16), w_out_odd.astype(bf16)

    def layer_shards(l):
        j = l // 2
        mix = [sh_ie[j], sh_oe[j]] if l % 2 == 0 else [sh_io[j], sh_oo[j]]
        return [sh_gu1[l], sh_d1[l], *mix, sh_gu2[l], sh_d2[l]]

    def layer_axes(l):
        return [1, 0, 0 if l % 2 == 0 else 1, 0, 1, 0]

    def layer_names(l):
        return ["ffn1_w_gu", "ffn1_w_down", *(["w_in_even", "w_out_even"] if l % 2 == 0 else ["w_in_odd", "w_out_odd"]),
                "ffn2_w_gu", "ffn2_w_down"]

    FOX_CARRIES, DN_CARRIES = (0, 4, 3), (1, 2, 5)

    def split_comm(kind, arrays, axes):
        return (Comm(kind, [arrays[i] for i in FOX_CARRIES], [axes[i] for i in FOX_CARRIES]),
                Comm(kind, [arrays[i] for i in DN_CARRIES], [axes[i] for i in DN_CARRIES]))

    def join_split(got_fox, got_dn):
        out = [None] * 6
        for i, a in zip(FOX_CARRIES, got_fox):
            out[i] = a
        for i, a in zip(DN_CARRIES, got_dn):
            out[i] = a
        return out

    first = all_gather(layer_shards(0) + [dn_conv_w[None]], layer_axes(0) + [0], "ag_layer0")
    weights = {0: first[:6]}
    convw = jnp.moveaxis(first[6], 0, 2).reshape(n_even, CONV_WIDTH, -1)

    saved = []
    cur = xs
    for l in range(L):
        j = l // 2
        wgu1, wd1, win, wout, wgu2, wd2 = weights[l]
        nxt = l + 1 < L
        x0 = cur
        x1, s1 = ffn_fwd(x0, norm_ffn1[l:l + 1], wgu1, wd1, f"l{l}f1")
        if l % 2 == 0:
            cf, cd = split_comm("ag", layer_shards(l + 1), layer_axes(l + 1)) if nxt else (None, None)
            x2, sm, got_f, got_d = even_mixer_fwd(
                x1, norm_mix[l:l + 1], win, wout, convw[j], dn_a_log[j], dn_dt_bias[j], dn_norm_g[j:j + 1],
                fox_q_norm_g[j:j + 1], fox_k_norm_g[j:j + 1], fox_f_bias[j], f"l{l}mx", comm_fox=cf, comm_dn=cd)
            if nxt:
                weights[l + 1] = join_split(got_f, got_d)
        else:
            cm = Comm("ag", layer_shards(l + 1), layer_axes(l + 1)) if nxt else None
            x2, sm, got = odd_mixer_fwd(x1, norm_mix[l:l + 1], win, wout, f"l{l}mx", comm=cm)
            if nxt:
                weights[l + 1] = got
        x3, s2 = ffn_fwd(x2, norm_ffn2[l:l + 1], wgu2, wd2, f"l{l}f2")
        saved.append((x0, x1, x2, s1, sm, s2))
        cur = x3
    dy, loss_row = loss_head(cur, tgt, "loss")

    gsmall = {n: [None] * W[n].shape[0] for n in SMALL_NAMES}
    gconv = [None] * n_even
    parts = {n: [None] * W[n].shape[0] for n in BIG_NAMES if n != "dn_conv_w"}

    def take(l, recv):
        for nm, p in zip(layer_names(l), recv):
            idx = l if nm.startswith("ffn") else l // 2
            parts[nm][idx] = sum_parts(p.reshape(NDEV, -1, p.shape[-1]), f"sum_{nm}_{idx}").reshape(p.shape[1:])

    pending = None
    for l in reversed(range(L)):
        j = l // 2
        wgu1, wd1, win, wout, wgu2, wd2 = weights[l]
        x0, x1, x2, s1, sm, s2 = saved[l]
        dy, dg, dwgu2, dwd2 = ffn_bwd(x2, norm_ffn2[l:l + 1], wgu2, wd2, s2, dy, f"l{l}f2")
        gsmall["norm_ffn2"][l] = dg[0]
        if l % 2 == 0:
            cf, cd = split_comm("rs", pending, layer_axes(l + 1)) if pending is not None else (None, None)
            dy, dwin, dwout, sg, got_f, got_d = even_mixer_bwd(
                x1, norm_mix[l:l + 1], win, wout, convw[j], dn_norm_g[j:j + 1], fox_q_norm_g[j:j + 1],
                fox_k_norm_g[j:j + 1], sm, dy, f"l{l}mx", comm_fox=cf, comm_dn=cd)
            if pending is not None:
                take(l + 1, join_split(got_f, got_d))
            gconv[j] = sg.pop("dn_conv_w")
            for k_, v_ in sg.items():
                gsmall[k_][l if k_ == "norm_mix" else j] = v_
        else:
            cm = Comm("rs", pending, layer_axes(l + 1)) if pending is not None else None
            dy, dwin, dwout, sg, got = odd_mixer_bwd(x1, norm_mix[l:l + 1], win, wout, sm, dy, f"l{l}mx", comm=cm)
            if pending is not None:
                take(l + 1, got)
            gsmall["norm_mix"][l] = sg["norm_mix"]
        dy, dg, dwgu1, dwd1 = ffn_bwd(x0, norm_ffn1[l:l + 1], wgu1, wd1, s1, dy, f"l{l}f1")
        gsmall["norm_ffn1"][l] = dg[0]
        pending = [dwgu1, dwd1, dwin, dwout, dwgu2, dwd2]
    take(0, scatter_parts(pending, layer_axes(0), "rs_layer0"))
    grad_x = dy[None]

    small_list = [jnp.stack(gsmall[n]) for n in SMALL_NAMES] + [jnp.stack(gconv), loss_row[0, :1]]
    packed = _pack_small(small_list)
    gathered = all_gather([packed], [0], "ag_small")[0]
    summed = sum_parts(gathered.reshape(NDEV, packed.shape[0], LANES), "sum_small")
    small_sum = _unpack_small(summed, small_list)
    G = dict(zip(SMALL_NAMES, small_sum[:len(SMALL_NAMES)]))
    conv_full = small_sum[len(SMALL_NAMES)]
    cwid = dn_conv_w.shape[2]
    G["dn_conv_w"] = lax.dynamic_slice_in_dim(conv_full, me * cwid, cwid, axis=2)
    loss = small_sum[-1][0]

    def unprep_gu(g):
        return g.reshape(L, D, 2, hp)[..., :hs].reshape(L, D, 2 * hs)

    G["ffn1_w_gu"] = unprep_gu(jnp.stack(parts["ffn1_w_gu"]))
    G["ffn2_w_gu"] = unprep_gu(jnp.stack(parts["ffn2_w_gu"]))
    G["ffn1_w_down"] = jnp.stack(parts["ffn1_w_down"])[:, :hs]
    G["ffn2_w_down"] = jnp.stack(parts["ffn2_w_down"])[:, :hs]
    G["w_in_even"] = _even_from_mine(jnp.stack(parts["w_in_even"]), D)
    G["w_out_even"] = jnp.stack(parts["w_out_even"])
    G["w_in_odd"] = jnp.stack(parts["w_in_odd"])
    G["w_out_odd"] = jnp.stack(parts["w_out_odd"])

    delta, new_m, new_v = {}, {}, {}
    for n in BIG_NAMES:
        shp = W[n].shape
        two = lambda a: a.reshape(-1, shp[-1])
        d_, m_, v_ = adamw(two(W[n]), two(G[n]), two(M[n]), two(V[n]), f"adamw_{n}")
        delta[n], new_m[n], new_v[n] = d_.reshape(shp), m_.reshape(shp), v_.reshape(shp)
    sw = [W[n] for n in SMALL_NAMES]
    d_, m_, v_ = adamw(_pack_small(sw), _pack_small([G[n] for n in SMALL_NAMES]), _pack_small([M[n] for n in SMALL_NAMES]),
                       _pack_small([V[n] for n in SMALL_NAMES]), "adamw_small")
    for n, a, b, c_ in zip(SMALL_NAMES, _unpack_small(d_, sw), _unpack_small(m_, sw), _unpack_small(v_, sw)):
        delta[n], new_m[n], new_v[n] = a, b, c_

    return (loss, grad_x, *[G[n] for n in WEIGHT_NAMES], *[delta[n] for n in WEIGHT_NAMES],
            *[new_m[n] for n in WEIGHT_NAMES], *[new_v[n] for n in WEIGHT_NAMES])
```

```python
import functools
import math

import jax
import jax.numpy as jnp
from jax import lax
from jax.experimental import pallas as pl
from jax.experimental.pallas import tpu as pltpu

f32 = jnp.float32
bf16 = jnp.bfloat16
SDS = jax.ShapeDtypeStruct

HEAD_DIM = 128
LANES = 128
CONV_WIDTH = 4
DN_CHUNK = 64
EPS = 1e-6
NDEV = 8
VMEM_LIMIT_BYTES = 56 * 1024 * 1024
HI = lax.Precision.HIGHEST
ADAMW_TILE_ELEMS = 384 * 1024

ADAM_LR = 0.001
ADAM_B1 = 0.9
ADAM_B2 = 0.999
ADAM_EPS = 1e-08
ADAM_WD = 0.01
ADAM_STEP = 10

NN = (((1,), (0,)), ((), ()))
NT = (((1,), (1,)), ((), ()))
TN = (((0,), (0,)), ((), ()))


def _me_and_peers():
    x, y, c = lax.axis_index("x"), lax.axis_index("y"), lax.axis_index("c")
    me = 4 * x + 2 * y + c
    peers = []
    for r in range(1, NDEV):
        px = 1 - x if (r >> 2) & 1 else x
        py = 1 - y if (r >> 1) & 1 else y
        pc = 1 - c if r & 1 else c
        peers.append(((px, py, pc), 4 * px + 2 * py + pc))
    return me, peers


def _slab(ref, axis, width, k):
    idx = [slice(None)] * len(ref.shape)
    idx[axis] = pl.ds(k * width, width)
    return ref.at[tuple(idx)]


class Comm:
    def __init__(self, kind, arrays, axes):
        self.kind, self.arrays, self.axes, self.n = kind, list(arrays), list(axes), len(arrays)

    def out_shape(self):
        out = []
        for a, ax in zip(self.arrays, self.axes):
            shp = list(a.shape)
            if self.kind == "ag":
                shp[ax] *= NDEV
                out.append(SDS(tuple(shp), a.dtype))
            else:
                shp[ax] //= NDEV
                out.append(SDS((NDEV, *shp), a.dtype))
        return out

    def sems(self):
        return [pltpu.SemaphoreType.DMA((self.n, NDEV - 1)), pltpu.SemaphoreType.DMA((self.n, NDEV - 1)),
                pltpu.SemaphoreType.DMA((self.n,))]

    def _src(self, ins, t, to_idx):
        if self.kind == "ag":
            return ins[t]
        return _slab(ins[t], self.axes[t], ins[t].shape[self.axes[t]] // NDEV, to_idx)

    def _dst(self, ins, outs, t, from_idx):
        if self.kind == "ag":
            return _slab(outs[t], self.axes[t], ins[t].shape[self.axes[t]], from_idx)
        return outs[t].at[from_idx]

    def _copies(self, ins, outs, sems, sending):
        send_sems, recv_sems, loc_sems = sems
        me, peers = _me_and_peers()
        local, remote = [], []
        for t in range(self.n):
            local.append(pltpu.make_async_copy(self._src(ins, t, me), self._dst(ins, outs, t, me), loc_sems.at[t]))
            for r, (peer, pidx) in enumerate(peers):
                remote.append(pltpu.make_async_remote_copy(
                    src_ref=self._src(ins, t, pidx), dst_ref=self._dst(ins, outs, t, me if sending else pidx),
                    send_sem=send_sems.at[t, r], recv_sem=recv_sems.at[t, r], device_id=peer,
                    device_id_type=pl.DeviceIdType.MESH))
        return local, remote

    def start(self, ins, outs, sems):
        local, remote = self._copies(ins, outs, sems, True)
        for cp in local + remote:
            cp.start()

    def wait(self, ins, outs, sems):
        local, remote = self._copies(ins, outs, sems, False)
        for cp in remote:
            cp.wait_recv()
            cp.wait_send()
        for cp in local:
            cp.wait()


def _call(body, name, grid, in_specs, out_specs, out_shape, scratch=(), comm=None):
    params = pltpu.CompilerParams(vmem_limit_bytes=VMEM_LIMIT_BYTES)
    if comm is None:
        return pl.pallas_call(body, name=name, grid=grid, in_specs=in_specs, out_specs=out_specs, out_shape=out_shape,
                              scratch_shapes=list(scratch), compiler_params=params)
    single = not isinstance(out_shape, (tuple, list))
    c_out_specs = [out_specs] if single else list(out_specs)
    c_out_shape = [out_shape] if single else list(out_shape)
    n_in, n_out, n_scr, n = len(in_specs), len(c_out_shape), len(scratch), comm.n
    anyspec = pl.BlockSpec(memory_space=pl.ANY)

    def wrapped(*refs):
        ins, refs = refs[:n_in], refs[n_in:]
        cins, refs = refs[:n], refs[n:]
        outs, refs = refs[:n_out], refs[n_out:]
        couts, refs = refs[:n], refs[n:]
        scr, sems = refs[:n_scr], refs[n_scr:]
        first = functools.reduce(lambda a, b: a & b, [pl.program_id(d) == 0 for d in range(len(grid))], True)
        last = functools.reduce(lambda a, b: a & b, [pl.program_id(d) == grid[d] - 1 for d in range(len(grid))], True)
        if grid:
            pl.when(first)(lambda: comm.start(cins, couts, sems))
            body(*ins, *outs, *scr)
            pl.when(last)(lambda: comm.wait(cins, couts, sems))
        else:
            comm.start(cins, couts, sems)
            if body is not None:
                body(*ins, *outs, *scr)
            comm.wait(cins, couts, sems)

    call = pl.pallas_call(
        wrapped, name=name, grid=grid, in_specs=list(in_specs) + [anyspec] * n, out_specs=c_out_specs + [anyspec] * n,
        out_shape=c_out_shape + comm.out_shape(), scratch_shapes=list(scratch) + comm.sems(), compiler_params=params)

    def run(*args):
        res = call(*args, *comm.arrays)
        mine = res[:n_out]
        return (mine[0] if single else tuple(mine)), list(res[n_out:])

    return run


def _tile(n, want, align=8):
    if n <= want:
        return n
    for t in range(want // align * align, 0, -align):
        if n % t == 0:
            return t
    return n


def _dot(a, b, dims=NN, precision=None):
    return lax.dot_general(a, b, dims, preferred_element_type=f32, precision=precision)


def _logsig(x):
    return jnp.minimum(x, 0.0) - jnp.log(1.0 + jnp.exp(-jnp.abs(x)))


def _softplus(x):
    return jnp.maximum(x, 0.0) + jnp.log(1.0 + jnp.exp(-jnp.abs(x)))


def _rms(x, g):
    return x * lax.rsqrt(jnp.mean(x * x, axis=-1, keepdims=True) + EPS) * g


def _lane_pick(blk, lane_idx):
    lane = lax.broadcasted_iota(jnp.int32, (1, LANES), 1)
    return jnp.sum(jnp.where(lane == lane_idx, blk, 0.0), axis=1, keepdims=True)


def mm(a, b, mode, out_dtype, name, tm=512, tn=512, res=None, scale=1.0):
    if mode == "nn":
        (M, K), (_, N) = a.shape, b.shape
    elif mode == "nt":
        (M, K), (N, _) = a.shape, b.shape
    else:
        (K, M), (_, N) = a.shape, b.shape
    tm, tn = _tile(M, tm, LANES), _tile(N, tn, LANES)
    a_spec = pl.BlockSpec((K, tm), lambda j, i: (0, i)) if mode == "tn" else pl.BlockSpec((tm, K), lambda j, i: (i, 0))
    b_spec = pl.BlockSpec((tn, K), lambda j, i: (j, 0)) if mode == "nt" else pl.BlockSpec((K, tn), lambda j, i: (0, j))
    dims = {"nn": NN, "nt": NT, "tn": TN}[mode]
    o_spec = pl.BlockSpec((tm, tn), lambda j, i: (i, j))

    def body(*refs):
        acc = _dot(refs[0][...].astype(bf16), refs[1][...].astype(bf16), dims)
        if scale != 1.0:
            acc = acc * scale
        if res is not None:
            acc = acc + refs[2][...]
        refs[-1][...] = acc.astype(out_dtype)

    ins, specs = [a, b], [a_spec, b_spec]
    if res is not None:
        ins.append(res)
        specs.append(o_spec)
    return _call(body, name, (N // tn, M // tm), specs, o_spec, SDS((M, N), out_dtype))(*ins)


def _rowspec(tm, w, cb=0):
    return pl.BlockSpec((tm, w), lambda i: (i, cb))


def _bspec(w):
    return pl.BlockSpec((1, w), lambda i: (0, 0))


def rms_fwd(x, g, name):
    S, D = x.shape
    tm = _tile(S, 512)

    def body(x_ref, g_ref, h_ref):
        h_ref[...] = _rms(x_ref[...], g_ref[...]).astype(bf16)

    return _call(body, name, (S // tm,), [_rowspec(tm, D), _bspec(D)], _rowspec(tm, D), SDS((S, D), bf16))(x, g)


def rms_bwd(x, g, dh, dres, name):
    S, D = x.shape
    tm = _tile(S, 256)

    def body(x_ref, g_ref, dh_ref, dres_ref, dx_ref, dg_ref):
        _, vjp = jax.vjp(_rms, x_ref[...], g_ref[...])
        dx, dg = vjp(dh_ref[...])
        dx_ref[...] = dres_ref[...] + dx

        @pl.when(pl.program_id(0) == 0)
        def _():
            dg_ref[...] = jnp.zeros_like(dg_ref)

        dg_ref[...] += dg

    return _call(body, name, (S // tm,), [_rowspec(tm, D), _bspec(D), _rowspec(tm, D), _rowspec(tm, D)],
                 (_rowspec(tm, D), _bspec(D)), (SDS((S, D), f32), SDS((1, D), f32)))(x, g, dh, dres)


def _swiglu(g, u):
    return g * jax.nn.sigmoid(g) * u


def ffn_gu_act(h, wgu, name, tm=1024, tn=768):
    S, D = h.shape
    W = wgu.shape[1] // 2
    tm, tn = _tile(S, tm, LANES), _tile(W, tn, LANES)
    nb = W // tn

    def body(h_ref, wg_ref, wu_ref, gu_ref, a_ref):
        hb = h_ref[...]
        g = _dot(hb, wg_ref[...])
        u = _dot(hb, wu_ref[...])
        gu_ref[0] = g.astype(bf16)
        gu_ref[1] = u.astype(bf16)
        a_ref[...] = _swiglu(g, u).astype(bf16)

    return _call(body, name, (nb, S // tm),
                 [pl.BlockSpec((tm, D), lambda j, i: (i, 0)), pl.BlockSpec((D, tn), lambda j, i: (0, j)),
                  pl.BlockSpec((D, tn), lambda j, i: (0, nb + j))],
                 (pl.BlockSpec((2, tm, tn), lambda j, i: (0, i, j)), pl.BlockSpec((tm, tn), lambda j, i: (i, j))),
                 (SDS((2, S, W), bf16), SDS((S, W), bf16)))(h, wgu, wgu)


def ffn_bda_act(dy, wd, gu, scale, name, tm=512, tn=768):
    S, D = dy.shape
    W = wd.shape[0]
    tm, tn = _tile(S, tm, LANES), _tile(W, tn, LANES)

    def body(dy_ref, wd_ref, gu_ref, dgu_ref):
        da = _dot(dy_ref[...].astype(bf16), wd_ref[...], NT) * scale
        _, vjp = jax.vjp(_swiglu, gu_ref[0].astype(f32), gu_ref[1].astype(f32))
        dg, du = vjp(da)
        dgu_ref[0] = dg.astype(bf16)
        dgu_ref[1] = du.astype(bf16)

    planes = pl.BlockSpec((2, tm, tn), lambda j, i: (0, i, j))
    return _call(body, name, (W // tn, S // tm),
                 [pl.BlockSpec((tm, D), lambda j, i: (i, 0)), pl.BlockSpec((tn, D), lambda j, i: (j, 0)), planes],
                 planes, SDS((2, S, W), bf16))(dy, wd, gu)


def ffn_bwgu(h, dgu, name, tm=512, tn=768):
    S, D = h.shape
    W = dgu.shape[2]
    tm, tn = _tile(D, tm, LANES), _tile(W, tn, LANES)
    nb = W // tn

    def body(h_ref, d_ref, o_ref):
        o_ref[...] = _dot(h_ref[...], d_ref[...], TN).astype(bf16)

    return _call(body, name, (2 * nb, D // tm),
                 [pl.BlockSpec((S, tm), lambda j, i: (0, i)), pl.BlockSpec((None, S, tn), lambda j, i: (j // nb, 0, j % nb))],
                 pl.BlockSpec((tm, tn), lambda j, i: (i, j)), SDS((D, 2 * W), bf16))(h, dgu)


def ffn_bdh(dgu, wgu, name, tm=512, tn=512):
    _, S, W = dgu.shape
    D = wgu.shape[0]
    tm, tn = _tile(S, tm, LANES), _tile(D, tn, LANES)

    def body(dg_ref, du_ref, wg_ref, wu_ref, o_ref):
        o_ref[...] = _dot(dg_ref[...], wg_ref[...], NT) + _dot(du_ref[...], wu_ref[...], NT)

    plane = lambda p: pl.BlockSpec((None, tm, W), lambda j, i: (p, i, 0))
    wcols = lambda p: pl.BlockSpec((tn, W), lambda j, i: (j, p))
    return _call(body, name, (D // tn, S // tm), [plane(0), plane(1), wcols(0), wcols(1)],
                 pl.BlockSpec((tm, tn), lambda j, i: (i, j)), SDS((S, D), f32))(dgu, dgu, wgu, wgu)


def loss_head(y, t, name):
    S, D = y.shape
    tm = _tile(S, 512)

    def body(y_ref, t_ref, dy_ref, l_ref):
        e = y_ref[...] - t_ref[...]
        dy_ref[...] = e * (1.0 / D)

        @pl.when(pl.program_id(0) == 0)
        def _():
            l_ref[...] = jnp.zeros_like(l_ref)

        l_ref[...] += jnp.sum(jnp.sum(e * e, axis=1, keepdims=True), axis=0, keepdims=True) * (0.5 / D)

    return _call(body, name, (S // tm,), [_rowspec(tm, D), _rowspec(tm, D)], (_rowspec(tm, D), _bspec(LANES)),
                 (SDS((S, D), f32), SDS((1, LANES), f32)))(y, t)


def ffn_fwd(x, g, wgu, wd, tag):
    h = rms_fwd(x, g, f"{tag}_rms")
    gu, a = ffn_gu_act(h, wgu, f"{tag}_gu")
    xo = mm(a, wd, "nn", f32, f"{tag}_down", tm=512, tn=512, res=x, scale=0.5)
    return xo, (h, gu, a)


def ffn_bwd(x, g, wgu, wd, saved, dy, tag):
    h, gu, a = saved
    dgu = ffn_bda_act(dy, wd, gu, 0.5, f"{tag}_bda")
    dwd = mm(a, dy, "tn", bf16, f"{tag}_bwd", tm=512, tn=512, scale=0.5)
    dwgu = ffn_bwgu(h, dgu, f"{tag}_bwgu")
    dh = ffn_bdh(dgu, wgu, f"{tag}_bdh")
    dx, dg = rms_bwd(x, g, dh, dy, f"{tag}_brms")
    return dx, dg, dwgu, dwd


def _split_bf16(x):
    hi = x.astype(bf16)
    lo = (x - hi.astype(f32)).astype(bf16)
    return hi, lo


SB_TQ, SB_TK, SB_NSUB = 512, 256, 4


def _sb_geometry(S):
    tq = min(SB_TQ, S)
    tk = min(SB_TK, tq)
    return tq, tk, tq // SB_NSUB, tq // tk


def _sb_consts(tk):
    r = lax.broadcasted_iota(jnp.int32, (tk, tk), 0)
    c = lax.broadcasted_iota(jnp.int32, (tk, tk), 1)
    return (r > c).astype(bf16), (r < c).astype(bf16)


def _sb_scores(qs, k, kb, tk, rows, masked):
    scale = HEAD_DIM ** -0.5
    z = [_dot(q, k, NT) * scale for q in qs]
    ls = [_logsig(z_) for z_ in z]
    lm = [l - z_ for l, z_ in zip(ls, z)]
    ok = None
    if masked:
        col = kb * tk + lax.broadcasted_iota(jnp.int32, z[0].shape, 1)
        ok = [col < row for row in rows]
        lm = [jnp.where(m, x, 0.0) for m, x in zip(ok, lm)]
    return ls, lm, ok


def _sb_weights(ls, lm, ok, tail, after):
    parts = [_split_bf16(x) for x in lm]
    suf = [_dot(hi, after) + _dot(lo, after) for hi, lo in parts]
    a = [jnp.exp(l + s_ + t_) for l, s_, t_ in zip(ls, suf, tail)]
    if ok is not None:
        a = [jnp.where(m, x, 0.0) for m, x in zip(ok, a)]
    return a


def sb_fwd(qkv, nh, name, comm=None):
    S = qkv.shape[0]
    tq, tk, rs, nd = _sb_geometry(S)
    nsub = tq // rs

    def body(q_ref, k_ref, v_ref, o_ref):
        i = pl.program_id(1)
        after, _ = _sb_consts(tk)
        qs = [q_ref[pl.ds(s * rs, rs), :] for s in range(nsub)]
        rows = [i * tq + s * rs + lax.broadcasted_iota(jnp.int32, (rs, tk), 0) for s in range(nsub)]

        def tile(kb, carry, masked):
            tail, acc = carry
            off = pl.multiple_of(kb * tk, tk)
            k = k_ref[pl.ds(off, tk), :]
            v = v_ref[pl.ds(off, tk), :]
            ls, lm, ok = _sb_scores(qs, k, kb, tk, rows, masked)
            a = _sb_weights(ls, lm, ok, tail, after)
            acc = [ac + _dot(a_.astype(bf16), v) for ac, a_ in zip(acc, a)]
            tail = [t_ + jnp.sum(x, axis=1, keepdims=True) for t_, x in zip(tail, lm)]
            return tail, acc

        carry = ([jnp.zeros((rs, 1), f32)] * nsub, [jnp.zeros((rs, HEAD_DIM), f32)] * nsub)
        for d in reversed(range(nd)):
            carry = tile(i * nd + d, carry, True)
        carry = lax.fori_loop(0, i * nd, lambda t, c: tile(i * nd - 1 - t, c, False), carry)
        for s in range(nsub):
            o_ref[pl.ds(s * rs, rs), :] = carry[1][s].astype(o_ref.dtype)

    blk = lambda cb0: pl.BlockSpec((tq, HEAD_DIM), lambda h, i: (i, cb0 + h))
    full = lambda cb0: pl.BlockSpec((S, HEAD_DIM), lambda h, i: (0, cb0 + h))
    return _call(body, name, (nh, S // tq), [blk(0), full(nh), full(2 * nh)], blk(0),
                 SDS((S, nh * HEAD_DIM), bf16), comm=comm)(qkv, qkv, qkv)


def sb_bwd(qkv, do, nh, name, comm=None):
    S = qkv.shape[0]
    tq, tk, rs, nd = _sb_geometry(S)
    nsub = tq // rs
    scale = HEAD_DIM ** -0.5

    def body(q_ref, k_ref, v_ref, do_ref, dq_ref, dk_ref, dv_ref, tc_ref):
        i = pl.program_id(1)

        @pl.when(i == 0)
        def _():
            dk_ref[...] = jnp.zeros_like(dk_ref)
            dv_ref[...] = jnp.zeros_like(dv_ref)

        after, before = _sb_consts(tk)
        qs = [q_ref[pl.ds(s * rs, rs), :] for s in range(nsub)]
        dos = [do_ref[pl.ds(s * rs, rs), :].astype(bf16) for s in range(nsub)]
        rows = [i * tq + s * rs + lax.broadcasted_iota(jnp.int32, (rs, tk), 0) for s in range(nsub)]

        def sweep1(kb, tail, masked):
            for s in range(nsub):
                tc_ref[kb, pl.ds(s * rs, rs), :] = jnp.broadcast_to(tail[s], (rs, LANES))
            k = k_ref[pl.ds(pl.multiple_of(kb * tk, tk), tk), :]
            _, lm, _ = _sb_scores(qs, k, kb, tk, rows, masked)
            return [t_ + jnp.sum(x, axis=1, keepdims=True) for t_, x in zip(tail, lm)]

        tail = [jnp.zeros((rs, 1), f32)] * nsub
        for d in reversed(range(nd)):
            tail = sweep1(i * nd + d, tail, True)
        lax.fori_loop(0, i * nd, lambda t, tl: sweep1(i * nd - 1 - t, tl, False), tail)

        def sweep2(kb, carry, masked):
            pe, dq = carry
            off = pl.multiple_of(kb * tk, tk)
            k = k_ref[pl.ds(off, tk), :]
            v = v_ref[pl.ds(off, tk), :]
            ls, lm, ok = _sb_scores(qs, k, kb, tk, rows, masked)
            tail = [tc_ref[kb, pl.ds(s * rs, rs), :][:, :1] for s in range(nsub)]
            a = _sb_weights(ls, lm, ok, tail, after)
            e = [_dot(d_, v, NT) * a_ for d_, a_ in zip(dos, a)]
            parts = [_split_bf16(x) for x in e]
            cum_e = [p_ + (_dot(hi, before) + _dot(lo, before)) for p_, (hi, lo) in zip(pe, parts)]
            sig = [jnp.exp(l) for l in ls]
            dz = [e_ * (1.0 - sg) - c_ * sg for e_, sg, c_ in zip(e, sig, cum_e)]
            if ok is not None:
                dz = [jnp.where(m, x, 0.0) for m, x in zip(ok, dz)]
            dzb = [(x * scale).astype(bf16) for x in dz]
            dk_ref[pl.ds(off, tk), :] += sum(_dot(x, q, TN) for x, q in zip(dzb, qs))
            dv_ref[pl.ds(off, tk), :] += sum(_dot(a_.astype(bf16), d_, TN) for a_, d_ in zip(a, dos))
            pe = [p_ + jnp.sum(e_, axis=1, keepdims=True) for p_, e_ in zip(pe, e)]
            dq = [g + _dot(x, k) for g, x in zip(dq, dzb)]
            return pe, dq

        carry = ([jnp.zeros((rs, 1), f32)] * nsub, [jnp.zeros((rs, HEAD_DIM), f32)] * nsub)
        carry = lax.fori_loop(0, i * nd, lambda kb, c: sweep2(kb, c, False), carry)
        for d in range(nd):
            carry = sweep2(i * nd + d, carry, True)
        for s in range(nsub):
            dq_ref[pl.ds(s * rs, rs), :] = carry[1][s]

    blk = lambda cb0: pl.BlockSpec((tq, HEAD_DIM), lambda h, i: (i, cb0 + h))
    full = lambda cb0: pl.BlockSpec((S, HEAD_DIM), lambda h, i: (0, cb0 + h))
    sh = SDS((S, nh * HEAD_DIM), f32)
    return _call(body, name, (nh, S // tq), [blk(0), full(nh), full(2 * nh), blk(0)], (blk(0), full(0), full(0)),
                 (sh, sh, sh), scratch=[pltpu.VMEM((S // tk, tq, LANES), f32)], comm=comm)(qkv, qkv, qkv, do)


def fox_fwd(fq, fk, fv, c, cT, nh, lane0, name, comm=None):
    S = fq.shape[0]
    tq, tk, rs, nd = _sb_geometry(S)
    nsub = tq // rs
    scale = HEAD_DIM ** -0.5

    def body(q_ref, k_ref, v_ref, c_ref, ct_ref, o_ref, lse_ref):
        h = pl.program_id(0)
        i = pl.program_id(1)
        subs = range(nsub)
        qs = [q_ref[pl.ds(s * rs, rs), :] for s in subs]
        ci = [_lane_pick(c_ref[pl.ds(s * rs, rs), :], lane0 + h) for s in subs]
        rows = [i * tq + s * rs + lax.broadcasted_iota(jnp.int32, (rs, tk), 0) for s in subs]

        def tile(kb, carry, masked):
            m, l, acc = carry
            off = pl.multiple_of(kb * tk, tk)
            k = k_ref[pl.ds(off, tk), :]
            v = v_ref[pl.ds(off, tk), :]
            cj = ct_ref[pl.ds(lane0 % 8 + h, 1), pl.ds(off, tk)]
            sc = [_dot(q, k, NT) * scale + (c_ - cj) for q, c_ in zip(qs, ci)]
            if masked:
                col = kb * tk + lax.broadcasted_iota(jnp.int32, (rs, tk), 1)
                sc = [jnp.where(col <= row, x, -jnp.inf) for x, row in zip(sc, rows)]
            m2 = [jnp.maximum(m_, jnp.max(x, axis=1, keepdims=True)) for m_, x in zip(m, sc)]
            p = [jnp.exp(x - m_) for x, m_ in zip(sc, m2)]
            al = [jnp.exp(a - b) for a, b in zip(m, m2)]
            l = [a * l_ + jnp.sum(p_, axis=1, keepdims=True) for a, l_, p_ in zip(al, l, p)]
            acc = [a * ac + _dot(p_.astype(bf16), v) for a, ac, p_ in zip(al, acc, p)]
            return m2, l, acc

        carry = ([jnp.full((rs, 1), -jnp.inf, f32)] * nsub, [jnp.zeros((rs, 1), f32)] * nsub,
                 [jnp.zeros((rs, HEAD_DIM), f32)] * nsub)
        for d in range(nd):
            carry = tile(i * nd + d, carry, True)
        m, l, acc = lax.fori_loop(0, i * nd, lambda kb, cr: tile(kb, cr, False), carry)
        for s in subs:
            o_ref[pl.ds(s * rs, rs), :] = acc[s] / l[s]
            lse_ref[pl.ds(s * rs, rs), :] = jnp.broadcast_to(m[s] + jnp.log(l[s]), (rs, HEAD_DIM))

    blk = pl.BlockSpec((tq, HEAD_DIM), lambda h, i: (i, h))
    full = pl.BlockSpec((S, HEAD_DIM), lambda h, i: (0, h))
    cspec = pl.BlockSpec((tq, LANES), lambda h, i: (i, 0))
    ctspec = pl.BlockSpec((8, S), lambda h, i: (lane0 // 8, 0))
    sh = SDS((S, nh * HEAD_DIM), f32)
    return _call(body, name, (nh, S // tq), [blk, full, full, cspec, ctspec], (blk, blk), (sh, sh), comm=comm)(fq, fk, fv, c, cT)


def fox_bwd(fq, fk, fv, c, cT, o, lse, do, nh, lane0, name, comm=None):
    S = fq.shape[0]
    tq, tk, rs, nd = _sb_geometry(S)
    nsub = tq // rs
    scale = HEAD_DIM ** -0.5

    def body(q_ref, k_ref, v_ref, c_ref, ct_ref, o_ref, lse_ref, do_ref, dq_ref, dk_ref, dv_ref, dct_ref):
        h = pl.program_id(0)
        i = pl.program_id(1)

        @pl.when(i == 0)
        def _():
            dk_ref[...] = jnp.zeros_like(dk_ref)
            dv_ref[...] = jnp.zeros_like(dv_ref)

        @pl.when((i == 0) & (h == 0))
        def _():
            dct_ref[...] = jnp.zeros_like(dct_ref)

        subs = range(nsub)
        qs = [q_ref[pl.ds(s * rs, rs), :] for s in subs]
        dof = [do_ref[pl.ds(s * rs, rs), :] for s in subs]
        dos = [x.astype(bf16) for x in dof]
        ci = [_lane_pick(c_ref[pl.ds(s * rs, rs), :], lane0 + h) for s in subs]
        lses = [lse_ref[pl.ds(s * rs, rs), :][:, :1] for s in subs]
        dl = [jnp.sum(x * o_ref[pl.ds(s * rs, rs), :], axis=1, keepdims=True) for s, x in zip(subs, dof)]
        rows = [i * tq + s * rs + lax.broadcasted_iota(jnp.int32, (rs, tk), 0) for s in subs]

        def tile(kb, carry, masked):
            dq, rsum = carry
            off = pl.multiple_of(kb * tk, tk)
            k = k_ref[pl.ds(off, tk), :]
            v = v_ref[pl.ds(off, tk), :]
            cj = ct_ref[pl.ds(lane0 % 8 + h, 1), pl.ds(off, tk)]
            p = [jnp.exp(_dot(q, k, NT) * scale + (c_ - cj) - ls) for q, c_, ls in zip(qs, ci, lses)]
            if masked:
                col = kb * tk + lax.broadcasted_iota(jnp.int32, (rs, tk), 1)
                p = [jnp.where(col <= row, x, 0.0) for x, row in zip(p, rows)]
            ds = [p_ * (_dot(d_, v, NT) - dl_) for p_, d_, dl_ in zip(p, dos, dl)]
            dsb = [(x * scale).astype(bf16) for x in ds]
            dk_ref[pl.ds(off, tk), :] += sum(_dot(x, q, TN) for x, q in zip(dsb, qs))
            dv_ref[pl.ds(off, tk), :] += sum(_dot(p_.astype(bf16), d_, TN) for p_, d_ in zip(p, dos))
            dct_ref[pl.ds(lane0 + h, 1), pl.ds(off, tk)] -= sum(jnp.sum(x, axis=0, keepdims=True) for x in ds)
            return ([g + _dot(x, k) for g, x in zip(dq, dsb)],
                    [r_ + jnp.sum(x, axis=1, keepdims=True) for r_, x in zip(rsum, ds)])

        carry = ([jnp.zeros((rs, HEAD_DIM), f32)] * nsub, [jnp.zeros((rs, 1), f32)] * nsub)
        carry = lax.fori_loop(0, i * nd, lambda kb, cr: tile(kb, cr, False), carry)
        for d in range(nd):
            carry = tile(i * nd + d, carry, True)
        dq, rsum = carry
        for s in subs:
            dq_ref[pl.ds(s * rs, rs), :] = dq[s]
        dct_ref[pl.ds(lane0 + h, 1), pl.ds(pl.multiple_of(i * tq, tq), tq)] += jnp.concatenate([r_.T for r_ in rsum], axis=1)

    blk = pl.BlockSpec((tq, HEAD_DIM), lambda h, i: (i, h))
    full = pl.BlockSpec((S, HEAD_DIM), lambda h, i: (0, h))
    cspec = pl.BlockSpec((tq, LANES), lambda h, i: (i, 0))
    ctspec = pl.BlockSpec((8, S), lambda h, i: (lane0 // 8, 0))
    dctspec = pl.BlockSpec((LANES, S), lambda h, i: (0, 0))
    sh = SDS((S, nh * HEAD_DIM), f32)
    return _call(body, name, (nh, S // tq), [blk, full, full, cspec, ctspec, blk, blk, blk], (blk, full, full, dctspec),
                 (sh, sh, sh, SDS((LANES, S), f32)), comm=comm)(fq, fk, fv, c, cT, o, lse, do)


def gates_fwd(proj, small_cb, fbias_row, name, tm=256):
    S = proj.shape[0]
    tm = _tile(S, tm)

    def body(sm_ref, fb_ref, c_ref, ct_ref, carry_ref):
        @pl.when(pl.program_id(0) == 0)
        def _():
            carry_ref[...] = jnp.zeros_like(carry_ref)

        lf = _logsig(sm_ref[...] + fb_ref[...])
        r = lax.broadcasted_iota(jnp.int32, (tm, tm), 0)
        cc = lax.broadcasted_iota(jnp.int32, (tm, tm), 1)
        c = _dot((r >= cc).astype(f32), lf, NN, HI) + carry_ref[...]
        carry_ref[...] += jnp.sum(lf, axis=0, keepdims=True)
        c_ref[...] = c
        ct_ref[...] = c.T

    return _call(body, name, (S // tm,), [_rowspec(tm, LANES, small_cb), _bspec(LANES)],
                 (_rowspec(tm, LANES), pl.BlockSpec((LANES, tm), lambda i: (0, i))),
                 (SDS((S, LANES), f32), SDS((LANES, S), f32)), scratch=[pltpu.VMEM((1, LANES), f32)])(proj, fbias_row)


def gates_bwd(proj, small_cb, fbias_row, dcT, dsm_dn, lane0, nh, name, tm=256):
    S = proj.shape[0]
    tm = _tile(S, tm)
    nt = S // tm

    def body(sm_ref, fb_ref, dct_ref, dsm_ref, o_ref, dfb_ref, carry_ref):
        @pl.when(pl.program_id(0) == 0)
        def _():
            carry_ref[...] = jnp.zeros_like(carry_ref)
            dfb_ref[...] = jnp.zeros_like(dfb_ref)

        dc = dct_ref[...].T
        r = lax.broadcasted_iota(jnp.int32, (tm, tm), 0)
        cc = lax.broadcasted_iota(jnp.int32, (tm, tm), 1)
        dlf = _dot((r <= cc).astype(f32), dc, NN, HI) + carry_ref[...]
        carry_ref[...] += jnp.sum(dc, axis=0, keepdims=True)
        lane = lax.broadcasted_iota(jnp.int32, (1, LANES), 1)
        dpre = jnp.where((lane >= lane0) & (lane < lane0 + nh), dlf * jax.nn.sigmoid(-(sm_ref[...] + fb_ref[...])), 0.0)
        o_ref[...] = dsm_ref[...] + dpre
        dfb_ref[...] += jnp.sum(dpre, axis=0, keepdims=True)

    rev = lambda i: nt - 1 - i
    return _call(body, name, (nt,),
                 [pl.BlockSpec((tm, LANES), lambda i: (rev(i), small_cb)), _bspec(LANES),
                  pl.BlockSpec((LANES, tm), lambda i: (0, rev(i))), pl.BlockSpec((tm, LANES), lambda i: (rev(i), 0))],
                 (pl.BlockSpec((tm, LANES), lambda i: (rev(i), 0)), _bspec(LANES)),
                 (SDS((S, LANES), f32), SDS((1, LANES), f32)), scratch=[pltpu.VMEM((1, LANES), f32)])(proj, fbias_row, dcT, dsm_dn)


PAD_ROWS = 8


def conv_fwd(proj, w, width, name):
    S = proj.shape[0]
    cw = 256 if width % 256 == 0 else 128

    def body(x_ref, w_ref, y_ref, pad_ref):
        pad_ref[pl.ds(0, PAD_ROWS), :] = jnp.zeros((PAD_ROWS, cw), f32)
        pad_ref[pl.ds(PAD_ROWS, S), :] = x_ref[...]
        y = jnp.zeros((S, cw), f32)
        for i in range(CONV_WIDTH):
            y = y + pad_ref[pl.ds(PAD_ROWS - (CONV_WIDTH - 1) + i, S), :] * w_ref[pl.ds(i, 1), :]
        y_ref[...] = y * jax.nn.sigmoid(y)

    spec = pl.BlockSpec((S, cw), lambda j: (0, j))
    return _call(body, name, (width // cw,), [spec, pl.BlockSpec((CONV_WIDTH, cw), lambda j: (0, j))], spec,
                 SDS((S, width), f32), scratch=[pltpu.VMEM((S + PAD_ROWS, cw), f32)])(proj, w)


def conv_bwd(proj, w, dy, name):
    S, width = dy.shape
    cw = 256 if width % 256 == 0 else 128

    def body(x_ref, w_ref, dy_ref, dx_ref, dw_ref, xpad_ref, dpad_ref):
        xpad_ref[pl.ds(0, PAD_ROWS), :] = jnp.zeros((PAD_ROWS, cw), f32)
        xpad_ref[pl.ds(PAD_ROWS, S), :] = x_ref[...]
        y = jnp.zeros((S, cw), f32)
        for i in range(CONV_WIDTH):
            y = y + xpad_ref[pl.ds(PAD_ROWS - (CONV_WIDTH - 1) + i, S), :] * w_ref[pl.ds(i, 1), :]
        sg = jax.nn.sigmoid(y)
        dpre = dy_ref[...] * (sg * (1.0 + y * (1.0 - sg)))
        dpad_ref[pl.ds(S, PAD_ROWS), :] = jnp.zeros((PAD_ROWS, cw), f32)
        dpad_ref[pl.ds(0, S), :] = dpre
        dx = jnp.zeros((S, cw), f32)
        for i in range(CONV_WIDTH):
            dx = dx + dpad_ref[pl.ds(CONV_WIDTH - 1 - i, S), :] * w_ref[pl.ds(i, 1), :]
            dw_ref[pl.ds(i, 1), :] = jnp.sum(dpre * xpad_ref[pl.ds(PAD_ROWS - (CONV_WIDTH - 1) + i, S), :], axis=0, keepdims=True)
        dx_ref[...] = dx

    spec = pl.BlockSpec((S, cw), lambda j: (0, j))
    wspec = pl.BlockSpec((CONV_WIDTH, cw), lambda j: (0, j))
    return _call(body, name, (width // cw,), [spec, wspec, spec], (spec, wspec),
                 (SDS((S, width), f32), SDS((CONV_WIDTH, width), f32)),
                 scratch=[pltpu.VMEM((S + PAD_ROWS, cw), f32), pltpu.VMEM((S + PAD_ROWS, cw), f32)])(proj, w, dy)


def _pdot_raw(a, b, dims, passes):
    if passes == 1:
        return _dot(a.astype(bf16), b.astype(bf16), dims)
    ah, al = _split_bf16(a)
    bh, bl = _split_bf16(b)
    return _dot(ah, bh, dims) + (_dot(ah, bl, dims) + _dot(al, bh, dims))


def _make_pdot(passes):
    @functools.partial(jax.custom_vjp, nondiff_argnums=(2,))
    def pdot(a, b, mode):
        return _pdot_raw(a, b, {"nn": NN, "nt": NT, "tn": TN}[mode], passes)

    def fwd(a, b, mode):
        return pdot(a, b, mode), (a, b)

    def bwd(mode, res, g):
        a, b = res
        if mode == "nn":
            return _pdot_raw(g, b, NT, passes), _pdot_raw(a, g, TN, passes)
        if mode == "nt":
            return _pdot_raw(g, b, NN, passes), _pdot_raw(g, a, TN, passes)
        return _pdot_raw(b, g, NT, passes), _pdot_raw(a, g, NN, passes)

    pdot.defvjp(fwd, bwd)
    return pdot


_dot1 = _make_pdot(1)
_dot3 = _make_pdot(3)


def _each(f, *lists):
    return [f(*xs) for xs in zip(*lists)]


def _dn_chunk(ndn, cq, ck, cv, small, alog, dtb, s0):
    C = cq[0].shape[0]
    hs = list(range(ndn))
    b = [_lane_pick(small, h) for h in hs]
    d = [_lane_pick(small, ndn + h) for h in hs]
    al = [_lane_pick(alog, h) for h in hs]
    dt = [_lane_pick(dtb, h) for h in hs]
    q = _each(lambda x: x * lax.rsqrt(jnp.sum(x * x, axis=1, keepdims=True) + EPS) * (HEAD_DIM ** -0.5), cq)
    k = _each(lambda x: x * lax.rsqrt(jnp.sum(x * x, axis=1, keepdims=True) + EPS), ck)
    beta = _each(jax.nn.sigmoid, b)
    g = _each(lambda al_, d_, dt_: -jnp.exp(al_) * _softplus(d_ + dt_), al, d, dt)
    r = lax.broadcasted_iota(jnp.int32, (C, C), 0)
    c = lax.broadcasted_iota(jnp.int32, (C, C), 1)
    incl, strict = r >= c, r > c
    inclf = incl.astype(f32)
    gc = _each(lambda g_: _dot3(inclf, g_, "nn"), g)
    decay = _each(lambda gc_: jnp.where(incl, jnp.exp(jnp.where(incl, gc_ - gc_.T, 0.0)), 0.0), gc)
    kb = _each(lambda k_, b_: k_ * b_, k, beta)
    a = _each(lambda kb_, k_, dec: jnp.where(strict, _dot3(kb_, k_, "nt") * dec, 0.0), kb, k, decay)
    eye = (r == c).astype(f32)
    t = _each(lambda a_: eye - a_, a)
    p = a
    for _ in range(int(math.log2(C)) - 1):
        p = _each(lambda p_: _dot3(p_, p_, "nn"), p)
        t = _each(lambda t_, p_: t_ + _dot3(t_, p_, "nn"), t, p)
    eg = _each(jnp.exp, gc)
    u = _each(lambda t_, v_, b_: _dot3(t_, v_ * b_, "nn"), t, cv, beta)
    w = _each(lambda t_, kb_, eg_: _dot3(t_, kb_ * eg_, "nn"), t, kb, eg)
    attn = _each(lambda q_, k_, dec: _dot1(q_, k_, "nt") * dec, q, k, decay)
    g_last = _each(lambda g_: jnp.sum(g_, axis=0, keepdims=True), g)
    v_new = _each(lambda u_, w_, s_: u_ - _dot1(w_, s_, "nn"), u, w, s0)
    o = _each(lambda q_, eg_, s_, at, vn: _dot1(q_ * eg_, s_, "nn") + _dot1(at, vn, "nn"), q, eg, s0, attn, v_new)
    s1 = _each(lambda s_, gl, k_, gc_, vn: s_ * jnp.exp(gl) + _dot1(k_ * jnp.exp(gl - gc_), vn, "tn"), s0, g_last, k, gc, v_new)
    return o, s1


def dn_fwd(cqkv, proj, small_cb, alog_row, dt_row, ndn, name, comm=None):
    S = cqkv.shape[0]
    C = DN_CHUNK
    nc = S // C
    half = ndn * HEAD_DIM

    def body(q_ref, k_ref, v_ref, sm_ref, al_ref, dt_ref, o_ref, ss_ref, s_ref):
        @pl.when(pl.program_id(0) == 0)
        def _():
            s_ref[...] = jnp.zeros_like(s_ref)

        sls = [slice(h * HEAD_DIM, (h + 1) * HEAD_DIM) for h in range(ndn)]
        s0 = [s_ref[h] for h in range(ndn)]
        for h in range(ndn):
            ss_ref[h, 0] = s0[h]
        o, s1 = _dn_chunk(ndn, [q_ref[:, sl] for sl in sls], [k_ref[:, sl] for sl in sls], [v_ref[:, sl] for sl in sls],
                          sm_ref[...], al_ref[...], dt_ref[...], s0)
        for h in range(ndn):
            o_ref[:, sls[h]] = o[h]
            s_ref[h] = s1[h]

    blk = lambda cb: pl.BlockSpec((C, half), lambda n: (n, cb))
    row = pl.BlockSpec((1, LANES), lambda n: (0, 0))
    return _call(body, name, (nc,),
                 [blk(0), blk(1), blk(2), pl.BlockSpec((C, LANES), lambda n: (n, small_cb)), row, row],
                 (blk(0), pl.BlockSpec((ndn, 1, HEAD_DIM, HEAD_DIM), lambda n: (0, n, 0, 0))),
                 (SDS((S, half), f32), SDS((ndn, nc, HEAD_DIM, HEAD_DIM), f32)),
                 scratch=[pltpu.VMEM((ndn, HEAD_DIM, HEAD_DIM), f32)], comm=comm)(cqkv, cqkv, cqkv, proj, alog_row, dt_row)


def dn_bwd(cqkv, proj, small_cb, alog_row, dt_row, ssave, do, ndn, name, comm=None):
    S = cqkv.shape[0]
    C = DN_CHUNK
    nc = S // C
    half = ndn * HEAD_DIM

    def body(q_ref, k_ref, v_ref, sm_ref, al_ref, dt_ref, ss_ref, do_ref, dqkv_ref, dsm_ref, dal_ref, ddt_ref, ds_ref):
        @pl.when(pl.program_id(0) == 0)
        def _():
            ds_ref[...] = jnp.zeros_like(ds_ref)
            dal_ref[...] = jnp.zeros_like(dal_ref)
            ddt_ref[...] = jnp.zeros_like(ddt_ref)

        sls = [slice(h * HEAD_DIM, (h + 1) * HEAD_DIM) for h in range(ndn)]
        _, vjp = jax.vjp(functools.partial(_dn_chunk, ndn), [q_ref[:, sl] for sl in sls], [k_ref[:, sl] for sl in sls],
                         [v_ref[:, sl] for sl in sls], sm_ref[...], al_ref[...], dt_ref[...], [ss_ref[h, 0] for h in range(ndn)])
        dq, dk, dv, dsm, dal, ddt, ds0 = vjp(([do_ref[:, sl] for sl in sls], [ds_ref[h] for h in range(ndn)]))
        for h in range(ndn):
            dqkv_ref[:, sls[h]] = dq[h]
            dqkv_ref[:, half + h * HEAD_DIM:half + (h + 1) * HEAD_DIM] = dk[h]
            dqkv_ref[:, 2 * half + h * HEAD_DIM:2 * half + (h + 1) * HEAD_DIM] = dv[h]
            ds_ref[h] = ds0[h]
        dsm_ref[...] = dsm
        dal_ref[...] += dal
        ddt_ref[...] += ddt

    rev = lambda n: nc - 1 - n
    blk = lambda cb: pl.BlockSpec((C, half), lambda n: (rev(n), cb))
    row = pl.BlockSpec((1, LANES), lambda n: (0, 0))
    return _call(body, name, (nc,),
                 [blk(0), blk(1), blk(2), pl.BlockSpec((C, LANES), lambda n: (rev(n), small_cb)), row, row,
                  pl.BlockSpec((ndn, 1, HEAD_DIM, HEAD_DIM), lambda n: (0, rev(n), 0, 0)), blk(0)],
                 (pl.BlockSpec((C, 3 * half), lambda n: (rev(n), 0)), pl.BlockSpec((C, LANES), lambda n: (rev(n), 0)), row, row),
                 (SDS((S, 3 * half), f32), SDS((S, LANES), f32), SDS((1, LANES), f32), SDS((1, LANES), f32)),
                 scratch=[pltpu.VMEM((ndn, HEAD_DIM, HEAD_DIM), f32)], comm=comm)(cqkv, cqkv, cqkv, proj, alog_row, dt_row, ssave, do)


def even_pre(proj, gq, gk, dfx, cb0, name):
    S = proj.shape[0]
    tm = _tile(S, 256)
    nh = dfx // HEAD_DIM

    def body(q_ref, k_ref, v_ref, gq_ref, gk_ref, fq_ref, fk_ref, fv_ref):
        for h in range(nh):
            sl = slice(h * HEAD_DIM, (h + 1) * HEAD_DIM)
            fq_ref[:, sl] = _rms(q_ref[:, sl], gq_ref[...]).astype(bf16)
            fk_ref[:, sl] = _rms(k_ref[:, sl], gk_ref[...]).astype(bf16)
        fv_ref[...] = v_ref[...].astype(bf16)

    sh = SDS((S, dfx), bf16)
    o = _rowspec(tm, dfx)
    return _call(body, name, (S // tm,),
                 [_rowspec(tm, dfx, cb0), _rowspec(tm, dfx, cb0 + 1), _rowspec(tm, dfx, cb0 + 2), _bspec(HEAD_DIM), _bspec(HEAD_DIM)],
                 (o, o, o), (sh, sh, sh))(proj, proj, proj, gq, gk)


def even_pre_bwd(proj, gq, gk, dfq, dfk, dfx, cb0, name):
    S = proj.shape[0]
    tm = _tile(S, 256)
    nh = dfx // HEAD_DIM

    def body(q_ref, k_ref, gq_ref, gk_ref, dfq_ref, dfk_ref, dq_ref, dk_ref, dgq_ref, dgk_ref):
        @pl.when(pl.program_id(0) == 0)
        def _():
            dgq_ref[...] = jnp.zeros_like(dgq_ref)
            dgk_ref[...] = jnp.zeros_like(dgk_ref)

        for h in range(nh):
            sl = slice(h * HEAD_DIM, (h + 1) * HEAD_DIM)
            _, vq = jax.vjp(_rms, q_ref[:, sl], gq_ref[...])
            dq, dgq = vq(dfq_ref[:, sl])
            _, vk = jax.vjp(_rms, k_ref[:, sl], gk_ref[...])
            dk, dgk = vk(dfk_ref[:, sl])
            dq_ref[:, sl] = dq
            dk_ref[:, sl] = dk
            dgq_ref[...] += dgq
            dgk_ref[...] += dgk

    sh = SDS((S, dfx), f32)
    o = _rowspec(tm, dfx)
    g = SDS((1, HEAD_DIM), f32)
    return _call(body, name, (S // tm,),
                 [_rowspec(tm, dfx, cb0), _rowspec(tm, dfx, cb0 + 1), _bspec(HEAD_DIM), _bspec(HEAD_DIM), o, o],
                 (o, o, _bspec(HEAD_DIM), _bspec(HEAD_DIM)), (sh, sh, g, g))(proj, proj, gq, gk, dfq, dfk)


def _dn_out(o, gate, gn):
    return _rms(o, gn) * (gate * jax.nn.sigmoid(gate))


def _fox_out(o, gate):
    return o * jax.nn.sigmoid(gate)


def even_post(o_dn, o_fox, proj, gn, half, cb_dn_gate, cb_fox_gate, name):
    S = proj.shape[0]
    tm = _tile(S, 256)
    nh = half // HEAD_DIM

    def body(od_ref, of_ref, gd_ref, gf_ref, gn_ref, o_ref):
        for h in range(nh):
            sl = slice(h * HEAD_DIM, (h + 1) * HEAD_DIM)
            o_ref[:, sl] = _dn_out(od_ref[:, sl], gd_ref[:, sl], gn_ref[...]).astype(bf16)
        o_ref[:, half:] = _fox_out(of_ref[...], gf_ref[...]).astype(bf16)

    return _call(body, name, (S // tm,),
                 [_rowspec(tm, half), _rowspec(tm, half), _rowspec(tm, half, cb_dn_gate), _rowspec(tm, half, cb_fox_gate), _bspec(HEAD_DIM)],
                 _rowspec(tm, 2 * half), SDS((S, 2 * half), bf16))(o_dn, o_fox, proj, proj, gn)


def even_post_bwd(o_dn, o_fox, proj, gn, dom, half, cb_dn_gate, cb_fox_gate, name):
    S = proj.shape[0]
    tm = _tile(S, 256)
    nh = half // HEAD_DIM

    def body(od_ref, of_ref, gd_ref, gf_ref, gn_ref, dod_ref, dof_ref, dd_ref, df_ref, dgd_ref, dgf_ref, dgn_ref):
        @pl.when(pl.program_id(0) == 0)
        def _():
            dgn_ref[...] = jnp.zeros_like(dgn_ref)

        for h in range(nh):
            sl = slice(h * HEAD_DIM, (h + 1) * HEAD_DIM)
            _, vjp = jax.vjp(_dn_out, od_ref[:, sl], gd_ref[:, sl], gn_ref[...])
            d_o, d_gate, d_gn = vjp(dod_ref[:, sl])
            dd_ref[:, sl] = d_o
            dgd_ref[:, sl] = d_gate
            dgn_ref[...] += d_gn
        _, vjp = jax.vjp(_fox_out, of_ref[...], gf_ref[...])
        d_o, d_gate = vjp(dof_ref[...])
        df_ref[...] = d_o
        dgf_ref[...] = d_gate

    sh = SDS((S, half), f32)
    o = _rowspec(tm, half)
    return _call(body, name, (S // tm,),
                 [o, o, _rowspec(tm, half, cb_dn_gate), _rowspec(tm, half, cb_fox_gate), _bspec(HEAD_DIM),
                  _rowspec(tm, half, 0), _rowspec(tm, half, 1)],
                 (o, o, o, o, _bspec(HEAD_DIM)), (sh, sh, sh, sh, SDS((1, HEAD_DIM), f32)))(o_dn, o_fox, proj, proj, gn, dom, dom)


def _pad_row(v, lane0=0):
    return jnp.pad(v, (lane0, LANES - lane0 - v.shape[0]))[None]


def _carried(res, comm):
    return res if comm is not None else (res, [])


def even_mixer_fwd(x, gmix, w_in, w_out, conv_w, a_log, dt_bias, gn, gq, gk, f_bias, tag, comm_fox=None, comm_dn=None):
    S, D = x.shape
    half = D // 2
    ndn = half // HEAD_DIM
    small_cb = 4 * D // LANES
    alog_row, dt_row, fb_row = _pad_row(a_log), _pad_row(dt_bias), _pad_row(f_bias, 2 * ndn)
    h = rms_fwd(x, gmix, f"{tag}_rms")
    proj = mm(h, w_in, "nn", f32, f"{tag}_in", tm=512, tn=1408)
    cqkv = conv_fwd(proj, conv_w, 3 * half, f"{tag}_conv")
    (o_dn, ssave), got_dn = _carried(dn_fwd(cqkv, proj, small_cb, alog_row, dt_row, ndn, f"{tag}_dn", comm=comm_dn), comm_dn)
    c, cT = gates_fwd(proj, small_cb, fb_row, f"{tag}_gates")
    fq, fk, fv = even_pre(proj, gq, gk, half, 4, f"{tag}_pre")
    (o_fox, lse), got_fox = _carried(fox_fwd(fq, fk, fv, c, cT, ndn, 2 * ndn, f"{tag}_fox", comm=comm_fox), comm_fox)
    om = even_post(o_dn, o_fox, proj, gn, half, 3, 7, f"{tag}_post")
    xo = mm(om, w_out, "nn", f32, f"{tag}_out", res=x)
    return xo, (h, proj, cqkv, o_dn, ssave, c, cT, fq, fk, fv, o_fox, lse, om, alog_row, dt_row, fb_row), got_fox, got_dn


def even_mixer_bwd(x, gmix, w_in, w_out, conv_w, gn, gq, gk, saved, dy, tag, comm_fox=None, comm_dn=None):
    S, D = x.shape
    half = D // 2
    ndn = half // HEAD_DIM
    small_cb = 4 * D // LANES
    h, proj, cqkv, o_dn, ssave, c, cT, fq, fk, fv, o_fox, lse, om, alog_row, dt_row, fb_row = saved
    dom = mm(dy, w_out, "nt", f32, f"{tag}_bdom")
    dw_out = mm(om, dy, "tn", bf16, f"{tag}_bwout")
    d_odn, d_ofox, d_dgate, d_fgate, d_gn = even_post_bwd(o_dn, o_fox, proj, gn, dom, half, 3, 7, f"{tag}_bpost")
    (dfq, dfk, dfv, dcT), got_fox = _carried(
        fox_bwd(fq, fk, fv, c, cT, o_fox, lse, d_ofox, ndn, 2 * ndn, f"{tag}_bfox", comm=comm_fox), comm_fox)
    dq_pre, dk_pre, d_gq, d_gk = even_pre_bwd(proj, gq, gk, dfq, dfk, half, 4, f"{tag}_bpre")
    (dcqkv, dsm_dn, d_alog, d_dt), got_dn = _carried(
        dn_bwd(cqkv, proj, small_cb, alog_row, dt_row, ssave, d_odn, ndn, f"{tag}_bdn", comm=comm_dn), comm_dn)
    d_conv_in, d_conv_w = conv_bwd(proj, conv_w, dcqkv, f"{tag}_bconv")
    d_small, d_fb = gates_bwd(proj, small_cb, fb_row, dcT, dsm_dn, 2 * ndn, ndn, f"{tag}_bgates")
    dproj = jnp.concatenate([d_conv_in, d_dgate, dq_pre, dk_pre, dfv, d_fgate, d_small], axis=1)
    dw_in = mm(h, dproj, "tn", bf16, f"{tag}_bwin", tn=384)
    dh = mm(dproj, w_in, "nt", f32, f"{tag}_bdh")
    dx, d_gmix = rms_bwd(x, gmix, dh, dy, f"{tag}_brms")
    small = dict(norm_mix=d_gmix[0], dn_conv_w=d_conv_w, dn_a_log=d_alog[0, :ndn], dn_dt_bias=d_dt[0, :ndn],
                 dn_norm_g=d_gn[0], fox_q_norm_g=d_gq[0], fox_k_norm_g=d_gk[0], fox_f_bias=d_fb[0, 2 * ndn:3 * ndn])
    return dx, dw_in, dw_out, small, got_fox, got_dn


def odd_mixer_fwd(x, gmix, w_in, w_out, tag, comm=None):
    S, D = x.shape
    nh = D // HEAD_DIM
    h = rms_fwd(x, gmix, f"{tag}_rms")
    qkv = mm(h, w_in, "nn", bf16, f"{tag}_in", tn=768)
    o, got = _carried(sb_fwd(qkv, nh, f"{tag}_sb", comm=comm), comm)
    xo = mm(o, w_out, "nn", f32, f"{tag}_out", res=x)
    return xo, (h, qkv, o), got


def odd_mixer_bwd(x, gmix, w_in, w_out, saved, dy, tag, comm=None):
    S, D = x.shape
    nh = D // HEAD_DIM
    h, qkv, o = saved
    do = mm(dy, w_out, "nt", f32, f"{tag}_bdo")
    dw_out = mm(o, dy, "tn", bf16, f"{tag}_bwout")
    (dq, dk, dv), got = _carried(sb_bwd(qkv, do, nh, f"{tag}_bsb", comm=comm), comm)
    dqkv = jnp.concatenate([dq, dk, dv], axis=1)
    dw_in = mm(h, dqkv, "tn", bf16, f"{tag}_bwin", tn=768)
    dh = mm(dqkv, w_in, "nt", f32, f"{tag}_bdh")
    dx, d_gmix = rms_bwd(x, gmix, dh, dy, f"{tag}_brms")
    return dx, dw_in, dw_out, dict(norm_mix=d_gmix[0]), got


def all_gather(shards, axes, name):
    return _call(None, name, (), [], [], [], comm=Comm("ag", shards, axes))()[1]


def scatter_parts(fulls, axes, name):
    return _call(None, name, (), [], [], [], comm=Comm("rs", fulls, axes))()[1]


def sum_parts(parts, name):
    _, R, C = parts.shape
    tm = _tile(R, 256)

    def body(p_ref, o_ref):
        acc = p_ref[0].astype(f32)
        for k in range(1, NDEV):
            acc = acc + p_ref[k].astype(f32)
        o_ref[...] = acc

    return _call(body, name, (R // tm,), [pl.BlockSpec((NDEV, tm, C), lambda i: (0, i, 0))], _rowspec(tm, C), SDS((R, C), f32))(parts)


def adamw(w, g, m, v, name):
    R, C = w.shape
    tm = _tile(R, max(8, min(512, (ADAMW_TILE_ELEMS // C) // 8 * 8)))
    c1 = 1.0 - ADAM_B1 ** ADAM_STEP
    c2 = 1.0 - ADAM_B2 ** ADAM_STEP

    def body(w_ref, g_ref, m_ref, v_ref, d_ref, nm_ref, nv_ref):
        g_ = g_ref[...]
        nm = ADAM_B1 * m_ref[...] + (1.0 - ADAM_B1) * g_
        nv = ADAM_B2 * v_ref[...] + (1.0 - ADAM_B2) * (g_ * g_)
        d_ref[...] = -ADAM_LR * ((nm / c1) / (jnp.sqrt(nv / c2) + ADAM_EPS) + ADAM_WD * w_ref[...])
        nm_ref[...] = nm
        nv_ref[...] = nv

    spec = _rowspec(tm, C)
    sh = SDS((R, C), f32)
    return _call(body, name, (R // tm,), [spec] * 4, (spec, spec, spec), (sh, sh, sh))(w, g, m, v)


def _pad_to(n, mult):
    return -(-n // mult) * mult


def _even_perm(D):
    half = D // 2
    ndn = half // HEAD_DIM
    o_gate = 3 * half
    o_b = o_gate + half
    o_a = o_b + ndn
    o_fq = o_a + ndn
    o_fpre = o_fq + 4 * half
    return half, ndn, o_b, o_a, o_fq, o_fpre


def _even_to_mine(w):
    D = (w.shape[-1] // 4) // LANES * LANES
    half, ndn, o_b, o_a, o_fq, o_fpre = _even_perm(D)
    small = jnp.concatenate([w[..., o_b:o_b + ndn], w[..., o_a:o_a + ndn], w[..., o_fpre:o_fpre + ndn]], axis=-1)
    small = jnp.pad(small, [(0, 0)] * (w.ndim - 1) + [(0, LANES - 3 * ndn)])
    return jnp.concatenate([w[..., :o_b], w[..., o_fq:o_fpre], small], axis=-1)


def _even_from_mine(w, D):
    half, ndn, o_b, o_a, o_fq, o_fpre = _even_perm(D)
    sm = w[..., 4 * D:]
    return jnp.concatenate([w[..., :o_b], sm[..., :ndn], sm[..., ndn:2 * ndn], w[..., o_b:4 * D], sm[..., 2 * ndn:3 * ndn]], axis=-1)


def _pack_small(parts):
    flat = jnp.concatenate([p.reshape(-1).astype(f32) for p in parts])
    n = flat.shape[0]
    return jnp.pad(flat, (0, _pad_to(n, 8 * LANES) - n)).reshape(-1, LANES)


def _unpack_small(packed, like):
    flat = packed.reshape(-1)
    out, off = [], 0
    for p in like:
        out.append(flat[off:off + p.size].reshape(p.shape))
        off += p.size
    return out


SMALL_NAMES = ("norm_ffn1", "norm_mix", "dn_a_log", "dn_dt_bias", "dn_norm_g", "fox_q_norm_g", "fox_k_norm_g", "fox_f_bias", "norm_ffn2")
BIG_NAMES = ("ffn1_w_gu", "ffn1_w_down", "w_in_even", "dn_conv_w", "w_out_even", "w_in_odd", "w_out_odd", "ffn2_w_gu", "ffn2_w_down")
WEIGHT_NAMES = ("norm_ffn1", "ffn1_w_gu", "ffn1_w_down", "norm_mix", "w_in_even", "dn_conv_w", "dn_a_log", "dn_dt_bias", "dn_norm_g",
                "fox_q_norm_g", "fox_k_norm_g", "fox_f_bias", "w_out_even", "w_in_odd", "w_out_odd", "norm_ffn2", "ffn2_w_gu", "ffn2_w_down")


def kernel(x, norm_ffn1, ffn1_w_gu, ffn1_w_down, norm_mix, w_in_even, dn_conv_w, dn_a_log, dn_dt_bias, dn_norm_g, fox_q_norm_g, fox_k_norm_g, fox_f_bias, w_out_even, w_in_odd, w_out_odd, norm_ffn2, ffn2_w_gu, ffn2_w_down, loss_target, m_norm_ffn1, m_ffn1_w_gu, m_ffn1_w_down, m_norm_mix, m_w_in_even, m_dn_conv_w, m_dn_a_log, m_dn_dt_bias, m_dn_norm_g, m_fox_q_norm_g, m_fox_k_norm_g, m_fox_f_bias, m_w_out_even, m_w_in_odd, m_w_out_odd, m_norm_ffn2, m_ffn2_w_gu, m_ffn2_w_down, v_norm_ffn1, v_ffn1_w_gu, v_ffn1_w_down, v_norm_mix, v_w_in_even, v_dn_conv_w, v_dn_a_log, v_dn_dt_bias, v_dn_norm_g, v_fox_q_norm_g, v_fox_k_norm_g, v_fox_f_bias, v_w_out_even, v_w_in_odd, v_w_out_odd, v_norm_ffn2, v_ffn2_w_gu, v_ffn2_w_down):
    args = dict(locals())
    W = {n: args[n] for n in WEIGHT_NAMES}
    M = {n: args["m_" + n] for n in WEIGHT_NAMES}
    V = {n: args["v_" + n] for n in WEIGHT_NAMES}
    xs = x[0]
    tgt = loss_target[0]
    S, D = xs.shape
    L = norm_ffn1.shape[0]
    n_even = w_in_even.shape[0]
    hs = ffn1_w_down.shape[1]
    hp = _pad_to(hs, LANES)
    me = 4 * lax.axis_index("x") + 2 * lax.axis_index("y") + lax.axis_index("c")

    def prep_gu(w):
        w = w.reshape(L, D, 2, hs)
        return jnp.pad(w, ((0, 0), (0, 0), (0, 0), (0, hp - hs))).reshape(L, D, 2 * hp).astype(bf16)

    def prep_down(w):
        return jnp.pad(w, ((0, 0), (0, hp - hs), (0, 0))).astype(bf16)

    sh_gu1, sh_d1, sh_gu2, sh_d2 = prep_gu(ffn1_w_gu), prep_down(ffn1_w_down), prep_gu(ffn2_w_gu), prep_down(ffn2_w_down)
    sh_ie, sh_oe = _even_to_mine(w_in_even).astype(bf16), w_out_even.astype(bf16)
    sh_io, sh_oo = w_in_odd.astype(bf16), w_out_odd.astype(bf16)

    def layer_shards(l):
        j = l // 2
        mix = [sh_ie[j], sh_oe[j]] if l % 2 == 0 else [sh_io[j], sh_oo[j]]
        return [sh_gu1[l], sh_d1[l], *mix, sh_gu2[l], sh_d2[l]]

    def layer_axes(l):
        return [1, 0, 0 if l % 2 == 0 else 1, 0, 1, 0]

    def layer_names(l):
        return ["ffn1_w_gu", "ffn1_w_down", *(["w_in_even", "w_out_even"] if l % 2 == 0 else ["w_in_odd", "w_out_odd"]),
                "ffn2_w_gu", "ffn2_w_down"]

    FOX_CARRIES, DN_CARRIES = (0, 4, 3), (1, 2, 5)

    def split_comm(kind, arrays, axes):
        return (Comm(kind, [arrays[i] for i in FOX_CARRIES], [axes[i] for i in FOX_CARRIES]),
                Comm(kind, [arrays[i] for i in DN_CARRIES], [axes[i] for i in DN_CARRIES]))

    def join_split(got_fox, got_dn):
        out = [None] * 6
        for i, a in zip(FOX_CARRIES, got_fox):
            out[i] = a
        for i, a in zip(DN_CARRIES, got_dn):
            out[i] = a
        return out

    first = all_gather(layer_shards(0) + [dn_conv_w[None]], layer_axes(0) + [0], "ag_layer0")
    weights = {0: first[:6]}
    convw = jnp.moveaxis(first[6], 0, 2).reshape(n_even, CONV_WIDTH, -1)

    saved = []
    cur = xs
    for l in range(L):
        j = l // 2
        wgu1, wd1, win, wout, wgu2, wd2 = weights[l]
        nxt = l + 1 < L
        x0 = cur
        x1, s1 = ffn_fwd(x0, norm_ffn1[l:l + 1], wgu1, wd1, f"l{l}f1")
        if l % 2 == 0:
            cf, cd = split_comm("ag", layer_shards(l + 1), layer_axes(l + 1)) if nxt else (None, None)
            x2, sm, got_f, got_d = even_mixer_fwd(
                x1, norm_mix[l:l + 1], win, wout, convw[j], dn_a_log[j], dn_dt_bias[j], dn_norm_g[j:j + 1],
                fox_q_norm_g[j:j + 1], fox_k_norm_g[j:j + 1], fox_f_bias[j], f"l{l}mx", comm_fox=cf, comm_dn=cd)
            if nxt:
                weights[l + 1] = join_split(got_f, got_d)
        else:
            cm = Comm("ag", layer_shards(l + 1), layer_axes(l + 1)) if nxt else None
            x2, sm, got = odd_mixer_fwd(x1, norm_mix[l:l + 1], win, wout, f"l{l}mx", comm=cm)
            if nxt:
                weights[l + 1] = got
        x3, s2 = ffn_fwd(x2, norm_ffn2[l:l + 1], wgu2, wd2, f"l{l}f2")
        saved.append((x0, x1, x2, s1, sm, s2))
        cur = x3
    dy, loss_row = loss_head(cur, tgt, "loss")

    gsmall = {n: [None] * W[n].shape[0] for n in SMALL_NAMES}
    gconv = [None] * n_even
    parts = {n: [None] * W[n].shape[0] for n in BIG_NAMES if n != "dn_conv_w"}

    def take(l, recv):
        for nm, p in zip(layer_names(l), recv):
            idx = l if nm.startswith("ffn") else l // 2
            parts[nm][idx] = sum_parts(p.reshape(NDEV, -1, p.shape[-1]), f"sum_{nm}_{idx}").reshape(p.shape[1:])

    pending = None
    for l in reversed(range(L)):
        j = l // 2
        wgu1, wd1, win, wout, wgu2, wd2 = weights[l]
        x0, x1, x2, s1, sm, s2 = saved[l]
        dy, dg, dwgu2, dwd2 = ffn_bwd(x2, norm_ffn2[l:l + 1], wgu2, wd2, s2, dy, f"l{l}f2")
        gsmall["norm_ffn2"][l] = dg[0]
        if l % 2 == 0:
            cf, cd = split_comm("rs", pending, layer_axes(l + 1)) if pending is not None else (None, None)
            dy, dwin, dwout, sg, got_f, got_d = even_mixer_bwd(
                x1, norm_mix[l:l + 1], win, wout, convw[j], dn_norm_g[j:j + 1], fox_q_norm_g[j:j + 1],
                fox_k_norm_g[j:j + 1], sm, dy, f"l{l}mx", comm_fox=cf, comm_dn=cd)
            if pending is not None:
                take(l + 1, join_split(got_f, got_d))
            gconv[j] = sg.pop("dn_conv_w")
            for k_, v_ in sg.items():
                gsmall[k_][l if k_ == "norm_mix" else j] = v_
        else:
            cm = Comm("rs", pending, layer_axes(l + 1)) if pending is not None else None
            dy, dwin, dwout, sg, got = odd_mixer_bwd(x1, norm_mix[l:l + 1], win, wout, sm, dy, f"l{l}mx", comm=cm)
            if pending is not None:
                take(l + 1, got)
            gsmall["norm_mix"][l] = sg["norm_mix"]
        dy, dg, dwgu1, dwd1 = ffn_bwd(x0, norm_ffn1[l:l + 1], wgu1, wd1, s1, dy, f"l{l}f1")
        gsmall["norm_ffn1"][l] = dg[0]
        pending = [dwgu1, dwd1, dwin, dwout, dwgu2, dwd2]
    take(0, scatter_parts(pending, layer_axes(0), "rs_layer0"))
    grad_x = dy[None]

    small_list = [jnp.stack(gsmall[n]) for n in SMALL_NAMES] + [jnp.stack(gconv), loss_row[0, :1]]
    packed = _pack_small(small_list)
    gathered = all_gather([packed], [0], "ag_small")[0]
    summed = sum_parts(gathered.reshape(NDEV, packed.shape[0], LANES), "sum_small")
    small_sum = _unpack_small(summed, small_list)
    G = dict(zip(SMALL_NAMES, small_sum[:len(SMALL_NAMES)]))
    conv_full = small_sum[len(SMALL_NAMES)]
    cwid = dn_conv_w.shape[2]
    G["dn_conv_w"] = lax.dynamic_slice_in_dim(conv_full, me * cwid, cwid, axis=2)
    loss = small_sum[-1][0]

    def unprep_gu(g):
        return g.reshape(L, D, 2, hp)[..., :hs].reshape(L, D, 2 * hs)

    G["ffn1_w_gu"] = unprep_gu(jnp.stack(parts["ffn1_w_gu"]))
    G["ffn2_w_gu"] = unprep_gu(jnp.stack(parts["ffn2_w_gu"]))
    G["ffn1_w_down"] = jnp.stack(parts["ffn1_w_down"])[:, :hs]
    G["ffn2_w_down"] = jnp.stack(parts["ffn2_w_down"])[:, :hs]
    G["w_in_even"] = _even_from_mine(jnp.stack(parts["w_in_even"]), D)
    G["w_out_even"] = jnp.stack(parts["w_out_even"])
    G["w_in_odd"] = jnp.stack(parts["w_in_odd"])
    G["w_out_odd"] = jnp.stack(parts["w_out_odd"])

    delta, new_m, new_v = {}, {}, {}
    for n in BIG_NAMES:
        shp = W[n].shape
        two = lambda a: a.reshape(-1, shp[-1])
        d_, m_, v_ = adamw(two(W[n]), two(G[n]), two(M[n]), two(V[n]), f"adamw_{n}")
        delta[n], new_m[n], new_v[n] = d_.reshape(shp), m_.reshape(shp), v_.reshape(shp)
    sw = [W[n] for n in SMALL_NAMES]
    d_, m_, v_ = adamw(_pack_small(sw), _pack_small([G[n] for n in SMALL_NAMES]), _pack_small([M[n] for n in SMALL_NAMES]),
                       _pack_small([V[n] for n in SMALL_NAMES]), "adamw_small")
    for n, a, b, c_ in zip(SMALL_NAMES, _unpack_small(d_, sw), _unpack_small(m_, sw), _unpack_small(v_, sw)):
        delta[n], new_m[n], new_v[n] = a, b, c_

    return (loss, grad_x, *[G[n] for n in WEIGHT_NAMES], *[delta[n] for n in WEIGHT_NAMES],
            *[new_m[n] for n in WEIGHT_NAMES], *[new_v[n] for n in WEIGHT_NAMES])
```

```python
import functools
import math

import jax
import jax.numpy as jnp
from jax import lax
from jax.experimental import pallas as pl
from jax.experimental.pallas import tpu as pltpu

f32 = jnp.float32
bf16 = jnp.bfloat16
SDS = jax.ShapeDtypeStruct

HEAD_DIM = 128
LANES = 128
CONV_WIDTH = 4
DN_CHUNK = 64
EPS = 1e-6
NDEV = 8
VMEM_LIMIT_BYTES = 56 * 1024 * 1024
HI = lax.Precision.HIGHEST
ADAMW_TILE_ELEMS = 384 * 1024

ADAM_LR = 0.001
ADAM_B1 = 0.9
ADAM_B2 = 0.999
ADAM_EPS = 1e-08
ADAM_WD = 0.01
ADAM_STEP = 10

NN = (((1,), (0,)), ((), ()))
NT = (((1,), (1,)), ((), ()))
TN = (((0,), (0,)), ((), ()))


def _me_and_peers():
    x, y, c = lax.axis_index("x"), lax.axis_index("y"), lax.axis_index("c")
    me = 4 * x + 2 * y + c
    peers = []
    for r in range(1, NDEV):
        px = 1 - x if (r >> 2) & 1 else x
        py = 1 - y if (r >> 1) & 1 else y
        pc = 1 - c if r & 1 else c
        peers.append(((px, py, pc), 4 * px + 2 * py + pc))
    return me, peers


def _slab(ref, axis, width, k):
    idx = [slice(None)] * len(ref.shape)
    idx[axis] = pl.ds(k * width, width)
    return ref.at[tuple(idx)]


class Comm:
    def __init__(self, kind, arrays, axes):
        self.kind, self.arrays, self.axes, self.n = kind, list(arrays), list(axes), len(arrays)

    def out_shape(self):
        out = []
        for a, ax in zip(self.arrays, self.axes):
            shp = list(a.shape)
            if self.kind == "ag":
                shp[ax] *= NDEV
                out.append(SDS(tuple(shp), a.dtype))
            else:
                shp[ax] //= NDEV
                out.append(SDS((NDEV, *shp), a.dtype))
        return out

    def sems(self):
        return [pltpu.SemaphoreType.DMA((self.n, NDEV - 1)), pltpu.SemaphoreType.DMA((self.n, NDEV - 1)),
                pltpu.SemaphoreType.DMA((self.n,))]

    def _src(self, ins, t, to_idx):
        if self.kind == "ag":
            return ins[t]
        return _slab(ins[t], self.axes[t], ins[t].shape[self.axes[t]] // NDEV, to_idx)

    def _dst(self, ins, outs, t, from_idx):
        if self.kind == "ag":
            return _slab(outs[t], self.axes[t], ins[t].shape[self.axes[t]], from_idx)
        return outs[t].at[from_idx]

    def _copies(self, ins, outs, sems, sending):
        send_sems, recv_sems, loc_sems = sems
        me, peers = _me_and_peers()
        local, remote = [], []
        for t in range(self.n):
            local.append(pltpu.make_async_copy(self._src(ins, t, me), self._dst(ins, outs, t, me), loc_sems.at[t]))
            for r, (peer, pidx) in enumerate(peers):
                remote.append(pltpu.make_async_remote_copy(
                    src_ref=self._src(ins, t, pidx), dst_ref=self._dst(ins, outs, t, me if sending else pidx),
                    send_sem=send_sems.at[t, r], recv_sem=recv_sems.at[t, r], device_id=peer,
                    device_id_type=pl.DeviceIdType.MESH))
        return local, remote

    def start(self, ins, outs, sems):
        local, remote = self._copies(ins, outs, sems, True)
        for cp in local + remote:
            cp.start()

    def wait(self, ins, outs, sems):
        local, remote = self._copies(ins, outs, sems, False)
        for cp in remote:
            cp.wait_recv()
            cp.wait_send()
        for cp in local:
            cp.wait()


def _call(body, name, grid, in_specs, out_specs, out_shape, scratch=(), comm=None):
    params = pltpu.CompilerParams(vmem_limit_bytes=VMEM_LIMIT_BYTES)
    if comm is None:
        return pl.pallas_call(body, name=name, grid=grid, in_specs=in_specs, out_specs=out_specs, out_shape=out_shape,
                              scratch_shapes=list(scratch), compiler_params=params)
    single = not isinstance(out_shape, (tuple, list))
    c_out_specs = [out_specs] if single else list(out_specs)
    c_out_shape = [out_shape] if single else list(out_shape)
    n_in, n_out, n_scr, n = len(in_specs), len(c_out_shape), len(scratch), comm.n
    anyspec = pl.BlockSpec(memory_space=pl.ANY)

    def wrapped(*refs):
        ins, refs = refs[:n_in], refs[n_in:]
        cins, refs = refs[:n], refs[n:]
        outs, refs = refs[:n_out], refs[n_out:]
        couts, refs = refs[:n], refs[n:]
        scr, sems = refs[:n_scr], refs[n_scr:]
        first = functools.reduce(lambda a, b: a & b, [pl.program_id(d) == 0 for d in range(len(grid))], True)
        last = functools.reduce(lambda a, b: a & b, [pl.program_id(d) == grid[d] - 1 for d in range(len(grid))], True)
        if grid:
            pl.when(first)(lambda: comm.start(cins, couts, sems))
            body(*ins, *outs, *scr)
            pl.when(last)(lambda: comm.wait(cins, couts, sems))
        else:
            comm.start(cins, couts, sems)
            if body is not None:
                body(*ins, *outs, *scr)
            comm.wait(cins, couts, sems)

    call = pl.pallas_call(
        wrapped, name=name, grid=grid, in_specs=list(in_specs) + [anyspec] * n, out_specs=c_out_specs + [anyspec] * n,
        out_shape=c_out_shape + comm.out_shape(), scratch_shapes=list(scratch) + comm.sems(), compiler_params=params)

    def run(*args):
        res = call(*args, *comm.arrays)
        mine = res[:n_out]
        return (mine[0] if single else tuple(mine)), list(res[n_out:])

    return run


def _tile(n, want, align=8):
    if n <= want:
        return n
    for t in range(want // align * align, 0, -align):
        if n % t == 0:
            return t
    return n


def _dot(a, b, dims=NN, precision=None):
    return lax.dot_general(a, b, dims, preferred_element_type=f32, precision=precision)


def _logsig(x):
    return jnp.minimum(x, 0.0) - jnp.log(1.0 + jnp.exp(-jnp.abs(x)))


def _softplus(x):
    return jnp.maximum(x, 0.0) + jnp.log(1.0 + jnp.exp(-jnp.abs(x)))


def _rms(x, g):
    return x * lax.rsqrt(jnp.mean(x * x, axis=-1, keepdims=True) + EPS) * g


def _lane_pick(blk, lane_idx):
    lane = lax.broadcasted_iota(jnp.int32, (1, LANES), 1)
    return jnp.sum(jnp.where(lane == lane_idx, blk, 0.0), axis=1, keepdims=True)


def mm(a, b, mode, out_dtype, name, tm=512, tn=512, res=None, scale=1.0, comm=None):
    if mode == "nn":
        (M, K), (_, N) = a.shape, b.shape
    elif mode == "nt":
        (M, K), (N, _) = a.shape, b.shape
    else:
        (K, M), (_, N) = a.shape, b.shape
    tm, tn = _tile(M, tm, LANES), _tile(N, tn, LANES)
    a_spec = pl.BlockSpec((K, tm), lambda j, i: (0, i)) if mode == "tn" else pl.BlockSpec((tm, K), lambda j, i: (i, 0))
    b_spec = pl.BlockSpec((tn, K), lambda j, i: (j, 0)) if mode == "nt" else pl.BlockSpec((K, tn), lambda j, i: (0, j))
    dims = {"nn": NN, "nt": NT, "tn": TN}[mode]
    o_spec = pl.BlockSpec((tm, tn), lambda j, i: (i, j))

    def body(*refs):
        acc = _dot(refs[0][...].astype(bf16), refs[1][...].astype(bf16), dims)
        if scale != 1.0:
            acc = acc * scale
        if res is not None:
            acc = acc + refs[2][...]
        refs[-1][...] = acc.astype(out_dtype)

    ins, specs = [a, b], [a_spec, b_spec]
    if res is not None:
        ins.append(res)
        specs.append(o_spec)
    return _call(body, name, (N // tn, M // tm), specs, o_spec, SDS((M, N), out_dtype), comm=comm)(*ins)


def _rowspec(tm, w, cb=0):
    return pl.BlockSpec((tm, w), lambda i: (i, cb))


def _bspec(w):
    return pl.BlockSpec((1, w), lambda i: (0, 0))


def rms_fwd(x, g, name):
    S, D = x.shape
    tm = _tile(S, 512)

    def body(x_ref, g_ref, h_ref):
        h_ref[...] = _rms(x_ref[...], g_ref[...]).astype(bf16)

    return _call(body, name, (S // tm,), [_rowspec(tm, D), _bspec(D)], _rowspec(tm, D), SDS((S, D), bf16))(x, g)


def rms_bwd(x, g, dh, dres, name):
    S, D = x.shape
    tm = _tile(S, 256)

    def body(x_ref, g_ref, dh_ref, dres_ref, dx_ref, dg_ref):
        _, vjp = jax.vjp(_rms, x_ref[...], g_ref[...])
        dx, dg = vjp(dh_ref[...])
        dx_ref[...] = dres_ref[...] + dx

        @pl.when(pl.program_id(0) == 0)
        def _():
            dg_ref[...] = jnp.zeros_like(dg_ref)

        dg_ref[...] += dg

    return _call(body, name, (S // tm,), [_rowspec(tm, D), _bspec(D), _rowspec(tm, D), _rowspec(tm, D)],
                 (_rowspec(tm, D), _bspec(D)), (SDS((S, D), f32), SDS((1, D), f32)))(x, g, dh, dres)


def _swiglu(g, u):
    return g * jax.nn.sigmoid(g) * u


def ffn_gu_act(h, wgu, name, tm=1024, tn=768, comm=None):
    S, D = h.shape
    W = wgu.shape[1] // 2
    tm, tn = _tile(S, tm, LANES), _tile(W, tn, LANES)
    nb = W // tn

    def body(h_ref, wg_ref, wu_ref, gu_ref, a_ref):
        hb = h_ref[...]
        g = _dot(hb, wg_ref[...])
        u = _dot(hb, wu_ref[...])
        gu_ref[0] = g.astype(bf16)
        gu_ref[1] = u.astype(bf16)
        a_ref[...] = _swiglu(g, u).astype(bf16)

    return _call(body, name, (nb, S // tm),
                 [pl.BlockSpec((tm, D), lambda j, i: (i, 0)), pl.BlockSpec((D, tn), lambda j, i: (0, j)),
                  pl.BlockSpec((D, tn), lambda j, i: (0, nb + j))],
                 (pl.BlockSpec((2, tm, tn), lambda j, i: (0, i, j)), pl.BlockSpec((tm, tn), lambda j, i: (i, j))),
                 (SDS((2, S, W), bf16), SDS((S, W), bf16)), comm=comm)(h, wgu, wgu)


def ffn_bda_act(dy, wd, gu, scale, name, tm=512, tn=768):
    S, D = dy.shape
    W = wd.shape[0]
    tm, tn = _tile(S, tm, LANES), _tile(W, tn, LANES)

    def body(dy_ref, wd_ref, gu_ref, dgu_ref):
        da = _dot(dy_ref[...].astype(bf16), wd_ref[...], NT) * scale
        _, vjp = jax.vjp(_swiglu, gu_ref[0].astype(f32), gu_ref[1].astype(f32))
        dg, du = vjp(da)
        dgu_ref[0] = dg.astype(bf16)
        dgu_ref[1] = du.astype(bf16)

    planes = pl.BlockSpec((2, tm, tn), lambda j, i: (0, i, j))
    return _call(body, name, (W // tn, S // tm),
                 [pl.BlockSpec((tm, D), lambda j, i: (i, 0)), pl.BlockSpec((tn, D), lambda j, i: (j, 0)), planes],
                 planes, SDS((2, S, W), bf16))(dy, wd, gu)


def ffn_bwgu(h, dgu, name, tm=512, tn=768):
    S, D = h.shape
    W = dgu.shape[2]
    tm, tn = _tile(D, tm, LANES), _tile(W, tn, LANES)
    nb = W // tn

    def body(h_ref, d_ref, o_ref):
        o_ref[...] = _dot(h_ref[...], d_ref[...], TN).astype(bf16)

    return _call(body, name, (2 * nb, D // tm),
                 [pl.BlockSpec((S, tm), lambda j, i: (0, i)), pl.BlockSpec((None, S, tn), lambda j, i: (j // nb, 0, j % nb))],
                 pl.BlockSpec((tm, tn), lambda j, i: (i, j)), SDS((D, 2 * W), bf16))(h, dgu)


def ffn_bdh(dgu, wgu, name, tm=512, tn=512):
    _, S, W = dgu.shape
    D = wgu.shape[0]
    tm, tn = _tile(S, tm, LANES), _tile(D, tn, LANES)

    def body(dg_ref, du_ref, wg_ref, wu_ref, o_ref):
        o_ref[...] = _dot(dg_ref[...], wg_ref[...], NT) + _dot(du_ref[...], wu_ref[...], NT)

    plane = lambda p: pl.BlockSpec((None, tm, W), lambda j, i: (p, i, 0))
    wcols = lambda p: pl.BlockSpec((tn, W), lambda j, i: (j, p))
    return _call(body, name, (D // tn, S // tm), [plane(0), plane(1), wcols(0), wcols(1)],
                 pl.BlockSpec((tm, tn), lambda j, i: (i, j)), SDS((S, D), f32))(dgu, dgu, wgu, wgu)


def loss_head(y, t, name):
    S, D = y.shape
    tm = _tile(S, 512)

    def body(y_ref, t_ref, dy_ref, l_ref):
        e = y_ref[...] - t_ref[...]
        dy_ref[...] = e * (1.0 / D)

        @pl.when(pl.program_id(0) == 0)
        def _():
            l_ref[...] = jnp.zeros_like(l_ref)

        l_ref[...] += jnp.sum(jnp.sum(e * e, axis=1, keepdims=True), axis=0, keepdims=True) * (0.5 / D)

    return _call(body, name, (S // tm,), [_rowspec(tm, D), _rowspec(tm, D)], (_rowspec(tm, D), _bspec(LANES)),
                 (SDS((S, D), f32), SDS((1, LANES), f32)))(y, t)


def ffn_fwd(x, g, wgu, wd, tag, comm_gu=None, comm_down=None):
    h = rms_fwd(x, g, f"{tag}_rms")
    (gu, a), got_gu = _carried(ffn_gu_act(h, wgu, f"{tag}_gu", comm=comm_gu), comm_gu)
    xo, got_down = _carried(mm(a, wd, "nn", f32, f"{tag}_down", tm=512, tn=512, res=x, scale=0.5, comm=comm_down), comm_down)
    return xo, (h, gu, a), got_gu, got_down


def ffn_bwd(x, g, wgu, wd, saved, dy, tag):
    h, gu, a = saved
    dgu = ffn_bda_act(dy, wd, gu, 0.5, f"{tag}_bda")
    dwd = mm(a, dy, "tn", bf16, f"{tag}_bwd", tm=512, tn=512, scale=0.5)
    dwgu = ffn_bwgu(h, dgu, f"{tag}_bwgu")
    dh = ffn_bdh(dgu, wgu, f"{tag}_bdh")
    dx, dg = rms_bwd(x, g, dh, dy, f"{tag}_brms")
    return dx, dg, dwgu, dwd


def _split_bf16(x):
    hi = x.astype(bf16)
    lo = (x - hi.astype(f32)).astype(bf16)
    return hi, lo


SB_TQ, SB_TK, SB_NSUB = 512, 256, 4


def _sb_geometry(S):
    tq = min(SB_TQ, S)
    tk = min(SB_TK, tq)
    return tq, tk, tq // SB_NSUB, tq // tk


def _sb_consts(tk):
    r = lax.broadcasted_iota(jnp.int32, (tk, tk), 0)
    c = lax.broadcasted_iota(jnp.int32, (tk, tk), 1)
    return (r > c).astype(bf16), (r < c).astype(bf16)


def _sb_scores(qs, k, kb, tk, rows, masked):
    scale = HEAD_DIM ** -0.5
    z = [_dot(q, k, NT) * scale for q in qs]
    ls = [_logsig(z_) for z_ in z]
    lm = [l - z_ for l, z_ in zip(ls, z)]
    ok = None
    if masked:
        col = kb * tk + lax.broadcasted_iota(jnp.int32, z[0].shape, 1)
        ok = [col < row for row in rows]
        lm = [jnp.where(m, x, 0.0) for m, x in zip(ok, lm)]
    return ls, lm, ok


def _sb_weights(ls, lm, ok, tail, after):
    parts = [_split_bf16(x) for x in lm]
    suf = [_dot(hi, after) + _dot(lo, after) for hi, lo in parts]
    a = [jnp.exp(l + s_ + t_) for l, s_, t_ in zip(ls, suf, tail)]
    if ok is not None:
        a = [jnp.where(m, x, 0.0) for m, x in zip(ok, a)]
    return a


def sb_fwd(qkv, nh, name, comm=None):
    S = qkv.shape[0]
    tq, tk, rs, nd = _sb_geometry(S)
    nsub = tq // rs

    def body(q_ref, k_ref, v_ref, o_ref, tails_ref):
        i = pl.program_id(1)
        after, _ = _sb_consts(tk)
        qs = [q_ref[pl.ds(s * rs, rs), :] for s in range(nsub)]
        rows = [i * tq + s * rs + lax.broadcasted_iota(jnp.int32, (rs, tk), 0) for s in range(nsub)]

        def tile(kb, carry, masked):
            tail, acc = carry
            off = pl.multiple_of(kb * tk, tk)
            k = k_ref[pl.ds(off, tk), :]
            v = v_ref[pl.ds(off, tk), :]
            ls, lm, ok = _sb_scores(qs, k, kb, tk, rows, masked)
            a = _sb_weights(ls, lm, ok, tail, after)
            tails_ref[pl.ds(kb, 1), :] += jnp.concatenate([t_.T for t_ in tail], axis=1)
            acc = [ac + _dot(a_.astype(bf16), v) for ac, a_ in zip(acc, a)]
            tail = [t_ + jnp.sum(x, axis=1, keepdims=True) for t_, x in zip(tail, lm)]
            return tail, acc

        tails_ref[...] = jnp.zeros_like(tails_ref)
        carry = ([jnp.zeros((rs, 1), f32)] * nsub, [jnp.zeros((rs, HEAD_DIM), f32)] * nsub)
        for d in reversed(range(nd)):
            carry = tile(i * nd + d, carry, True)
        carry = lax.fori_loop(0, i * nd, lambda t, c: tile(i * nd - 1 - t, c, False), carry)
        for s in range(nsub):
            o_ref[pl.ds(s * rs, rs), :] = carry[1][s].astype(o_ref.dtype)

    blk = lambda cb0: pl.BlockSpec((tq, HEAD_DIM), lambda h, i: (i, cb0 + h))
    full = lambda cb0: pl.BlockSpec((S, HEAD_DIM), lambda h, i: (0, cb0 + h))
    tails = pl.BlockSpec((S // tk, tq), lambda h, i: (h, i))
    return _call(body, name, (nh, S // tq), [blk(0), full(nh), full(2 * nh)], (blk(0), tails),
                 (SDS((S, nh * HEAD_DIM), bf16), SDS((nh * (S // tk), S), f32)), comm=comm)(qkv, qkv, qkv)


def sb_bwd(qkv, tails, do, nh, name, comm=None):
    S = qkv.shape[0]
    tq, tk, rs, nd = _sb_geometry(S)
    nsub = tq // rs
    scale = HEAD_DIM ** -0.5

    def body(q_ref, k_ref, v_ref, tails_ref, do_ref, dq_ref, dk_ref, dv_ref):
        i = pl.program_id(1)

        @pl.when(i == 0)
        def _():
            dk_ref[...] = jnp.zeros_like(dk_ref)
            dv_ref[...] = jnp.zeros_like(dv_ref)

        after, before = _sb_consts(tk)
        qs = [q_ref[pl.ds(s * rs, rs), :] for s in range(nsub)]
        dos = [do_ref[pl.ds(s * rs, rs), :].astype(bf16) for s in range(nsub)]
        rows = [i * tq + s * rs + lax.broadcasted_iota(jnp.int32, (rs, tk), 0) for s in range(nsub)]

        def sweep2(kb, carry, masked):
            pe, dq = carry
            off = pl.multiple_of(kb * tk, tk)
            k = k_ref[pl.ds(off, tk), :]
            v = v_ref[pl.ds(off, tk), :]
            ls, lm, ok = _sb_scores(qs, k, kb, tk, rows, masked)
            tail_row = tails_ref[pl.ds(kb, 1), :]
            tail = [tail_row[:, s * rs:(s + 1) * rs].T for s in range(nsub)]
            a = _sb_weights(ls, lm, ok, tail, after)
            e = [_dot(d_, v, NT) * a_ for d_, a_ in zip(dos, a)]
            parts = [_split_bf16(x) for x in e]
            cum_e = [p_ + (_dot(hi, before) + _dot(lo, before)) for p_, (hi, lo) in zip(pe, parts)]
            sig = [jnp.exp(l) for l in ls]
            dz = [e_ * (1.0 - sg) - c_ * sg for e_, sg, c_ in zip(e, sig, cum_e)]
            if ok is not None:
                dz = [jnp.where(m, x, 0.0) for m, x in zip(ok, dz)]
            dzb = [(x * scale).astype(bf16) for x in dz]
            dk_ref[pl.ds(off, tk), :] += sum(_dot(x, q, TN) for x, q in zip(dzb, qs))
            dv_ref[pl.ds(off, tk), :] += sum(_dot(a_.astype(bf16), d_, TN) for a_, d_ in zip(a, dos))
            pe = [p_ + jnp.sum(e_, axis=1, keepdims=True) for p_, e_ in zip(pe, e)]
            dq = [g + _dot(x, k) for g, x in zip(dq, dzb)]
            return pe, dq

        carry = ([jnp.zeros((rs, 1), f32)] * nsub, [jnp.zeros((rs, HEAD_DIM), f32)] * nsub)
        carry = lax.fori_loop(0, i * nd, lambda kb, c: sweep2(kb, c, False), carry)
        for d in range(nd):
            carry = sweep2(i * nd + d, carry, True)
        for s in range(nsub):
            dq_ref[pl.ds(s * rs, rs), :] = carry[1][s]

    blk = lambda cb0: pl.BlockSpec((tq, HEAD_DIM), lambda h, i: (i, cb0 + h))
    full = lambda cb0: pl.BlockSpec((S, HEAD_DIM), lambda h, i: (0, cb0 + h))
    sh = SDS((S, nh * HEAD_DIM), f32)
    tails_spec = pl.BlockSpec((S // tk, tq), lambda h, i: (h, i))
    return _call(body, name, (nh, S // tq), [blk(0), full(nh), full(2 * nh), tails_spec, blk(0)], (blk(0), full(0), full(0)),
                 (sh, sh, sh), comm=comm)(qkv, qkv, qkv, tails, do)


def fox_fwd(fq, fk, fv, c, cT, nh, lane0, name, comm=None):
    S = fq.shape[0]
    tq, tk, rs, nd = _sb_geometry(S)
    nsub = tq // rs
    scale = HEAD_DIM ** -0.5

    def body(q_ref, k_ref, v_ref, c_ref, ct_ref, o_ref, lse_ref):
        h = pl.program_id(0)
        i = pl.program_id(1)
        subs = range(nsub)
        qs = [q_ref[pl.ds(s * rs, rs), :] for s in subs]
        ci = [_lane_pick(c_ref[pl.ds(s * rs, rs), :], lane0 + h) for s in subs]
        rows = [i * tq + s * rs + lax.broadcasted_iota(jnp.int32, (rs, tk), 0) for s in subs]

        def tile(kb, carry, masked):
            m, l, acc = carry
            off = pl.multiple_of(kb * tk, tk)
            k = k_ref[pl.ds(off, tk), :]
            v = v_ref[pl.ds(off, tk), :]
            cj = ct_ref[pl.ds(lane0 % 8 + h, 1), pl.ds(off, tk)]
            sc = [_dot(q, k, NT) * scale + (c_ - cj) for q, c_ in zip(qs, ci)]
            if masked:
                col = kb * tk + lax.broadcasted_iota(jnp.int32, (rs, tk), 1)
                sc = [jnp.where(col <= row, x, -jnp.inf) for x, row in zip(sc, rows)]
            m2 = [jnp.maximum(m_, jnp.max(x, axis=1, keepdims=True)) for m_, x in zip(m, sc)]
            p = [jnp.exp(x - m_) for x, m_ in zip(sc, m2)]
            al = [jnp.exp(a - b) for a, b in zip(m, m2)]
            l = [a * l_ + jnp.sum(p_, axis=1, keepdims=True) for a, l_, p_ in zip(al, l, p)]
            acc = [a * ac + _dot(p_.astype(bf16), v) for a, ac, p_ in zip(al, acc, p)]
            return m2, l, acc

        carry = ([jnp.full((rs, 1), -jnp.inf, f32)] * nsub, [jnp.zeros((rs, 1), f32)] * nsub,
                 [jnp.zeros((rs, HEAD_DIM), f32)] * nsub)
        for d in range(nd):
            carry = tile(i * nd + d, carry, True)
        m, l, acc = lax.fori_loop(0, i * nd, lambda kb, cr: tile(kb, cr, False), carry)
        for s in subs:
            o_ref[pl.ds(s * rs, rs), :] = acc[s] / l[s]
            lse_ref[pl.ds(s * rs, rs), :] = jnp.broadcast_to(m[s] + jnp.log(l[s]), (rs, HEAD_DIM))

    blk = pl.BlockSpec((tq, HEAD_DIM), lambda h, i: (i, h))
    full = pl.BlockSpec((S, HEAD_DIM), lambda h, i: (0, h))
    cspec = pl.BlockSpec((tq, LANES), lambda h, i: (i, 0))
    ctspec = pl.BlockSpec((8, S), lambda h, i: (lane0 // 8, 0))
    sh = SDS((S, nh * HEAD_DIM), f32)
    return _call(body, name, (nh, S // tq), [blk, full, full, cspec, ctspec], (blk, blk), (sh, sh), comm=comm)(fq, fk, fv, c, cT)


def fox_bwd(fq, fk, fv, c, cT, o, lse, do, nh, lane0, name, comm=None):
    S = fq.shape[0]
    tq, tk, rs, nd = _sb_geometry(S)
    nsub = tq // rs
    scale = HEAD_DIM ** -0.5

    def body(q_ref, k_ref, v_ref, c_ref, ct_ref, o_ref, lse_ref, do_ref, dq_ref, dk_ref, dv_ref, dct_ref):
        h = pl.program_id(0)
        i = pl.program_id(1)

        @pl.when(i == 0)
        def _():
            dk_ref[...] = jnp.zeros_like(dk_ref)
            dv_ref[...] = jnp.zeros_like(dv_ref)

        @pl.when((i == 0) & (h == 0))
        def _():
            dct_ref[...] = jnp.zeros_like(dct_ref)

        subs = range(nsub)
        qs = [q_ref[pl.ds(s * rs, rs), :] for s in subs]
        dof = [do_ref[pl.ds(s * rs, rs), :] for s in subs]
        dos = [x.astype(bf16) for x in dof]
        ci = [_lane_pick(c_ref[pl.ds(s * rs, rs), :], lane0 + h) for s in subs]
        lses = [lse_ref[pl.ds(s * rs, rs), :][:, :1] for s in subs]
        dl = [jnp.sum(x * o_ref[pl.ds(s * rs, rs), :], axis=1, keepdims=True) for s, x in zip(subs, dof)]
        rows = [i * tq + s * rs + lax.broadcasted_iota(jnp.int32, (rs, tk), 0) for s in subs]

        def tile(kb, carry, masked):
            dq, rsum = carry
            off = pl.multiple_of(kb * tk, tk)
            k = k_ref[pl.ds(off, tk), :]
            v = v_ref[pl.ds(off, tk), :]
            cj = ct_ref[pl.ds(lane0 % 8 + h, 1), pl.ds(off, tk)]
            p = [jnp.exp(_dot(q, k, NT) * scale + (c_ - cj) - ls) for q, c_, ls in zip(qs, ci, lses)]
            if masked:
                col = kb * tk + lax.broadcasted_iota(jnp.int32, (rs, tk), 1)
                p = [jnp.where(col <= row, x, 0.0) for x, row in zip(p, rows)]
            ds = [p_ * (_dot(d_, v, NT) - dl_) for p_, d_, dl_ in zip(p, dos, dl)]
            dsb = [(x * scale).astype(bf16) for x in ds]
            dk_ref[pl.ds(off, tk), :] += sum(_dot(x, q, TN) for x, q in zip(dsb, qs))
            dv_ref[pl.ds(off, tk), :] += sum(_dot(p_.astype(bf16), d_, TN) for p_, d_ in zip(p, dos))
            dct_ref[pl.ds(lane0 + h, 1), pl.ds(off, tk)] -= sum(jnp.sum(x, axis=0, keepdims=True) for x in ds)
            return ([g + _dot(x, k) for g, x in zip(dq, dsb)],
                    [r_ + jnp.sum(x, axis=1, keepdims=True) for r_, x in zip(rsum, ds)])

        carry = ([jnp.zeros((rs, HEAD_DIM), f32)] * nsub, [jnp.zeros((rs, 1), f32)] * nsub)
        carry = lax.fori_loop(0, i * nd, lambda kb, cr: tile(kb, cr, False), carry)
        for d in range(nd):
            carry = tile(i * nd + d, carry, True)
        dq, rsum = carry
        for s in subs:
            dq_ref[pl.ds(s * rs, rs), :] = dq[s]
        dct_ref[pl.ds(lane0 + h, 1), pl.ds(pl.multiple_of(i * tq, tq), tq)] += jnp.concatenate([r_.T for r_ in rsum], axis=1)

    blk = pl.BlockSpec((tq, HEAD_DIM), lambda h, i: (i, h))
    full = pl.BlockSpec((S, HEAD_DIM), lambda h, i: (0, h))
    cspec = pl.BlockSpec((tq, LANES), lambda h, i: (i, 0))
    ctspec = pl.BlockSpec((8, S), lambda h, i: (lane0 // 8, 0))
    dctspec = pl.BlockSpec((LANES, S), lambda h, i: (0, 0))
    sh = SDS((S, nh * HEAD_DIM), f32)
    return _call(body, name, (nh, S // tq), [blk, full, full, cspec, ctspec, blk, blk, blk], (blk, full, full, dctspec),
                 (sh, sh, sh, SDS((LANES, S), f32)), comm=comm)(fq, fk, fv, c, cT, o, lse, do)


def gates_fwd(proj, small_cb, fbias_row, name, tm=256):
    S = proj.shape[0]
    tm = _tile(S, tm)

    def body(sm_ref, fb_ref, c_ref, ct_ref, carry_ref):
        @pl.when(pl.program_id(0) == 0)
        def _():
            carry_ref[...] = jnp.zeros_like(carry_ref)

        lf = _logsig(sm_ref[...] + fb_ref[...])
        r = lax.broadcasted_iota(jnp.int32, (tm, tm), 0)
        cc = lax.broadcasted_iota(jnp.int32, (tm, tm), 1)
        c = _dot((r >= cc).astype(f32), lf, NN, HI) + carry_ref[...]
        carry_ref[...] += jnp.sum(lf, axis=0, keepdims=True)
        c_ref[...] = c
        ct_ref[...] = c.T

    return _call(body, name, (S // tm,), [_rowspec(tm, LANES, small_cb), _bspec(LANES)],
                 (_rowspec(tm, LANES), pl.BlockSpec((LANES, tm), lambda i: (0, i))),
                 (SDS((S, LANES), f32), SDS((LANES, S), f32)), scratch=[pltpu.VMEM((1, LANES), f32)])(proj, fbias_row)


def gates_bwd(proj, small_cb, fbias_row, dcT, dsm_dn, lane0, nh, name, tm=256):
    S = proj.shape[0]
    tm = _tile(S, tm)
    nt = S // tm

    def body(sm_ref, fb_ref, dct_ref, dsm_ref, o_ref, dfb_ref, carry_ref):
        @pl.when(pl.program_id(0) == 0)
        def _():
            carry_ref[...] = jnp.zeros_like(carry_ref)
            dfb_ref[...] = jnp.zeros_like(dfb_ref)

        dc = dct_ref[...].T
        r = lax.broadcasted_iota(jnp.int32, (tm, tm), 0)
        cc = lax.broadcasted_iota(jnp.int32, (tm, tm), 1)
        dlf = _dot((r <= cc).astype(f32), dc, NN, HI) + carry_ref[...]
        carry_ref[...] += jnp.sum(dc, axis=0, keepdims=True)
        lane = lax.broadcasted_iota(jnp.int32, (1, LANES), 1)
        dpre = jnp.where((lane >= lane0) & (lane < lane0 + nh), dlf * jax.nn.sigmoid(-(sm_ref[...] + fb_ref[...])), 0.0)
        o_ref[...] = dsm_ref[...] + dpre
        dfb_ref[...] += jnp.sum(dpre, axis=0, keepdims=True)

    rev = lambda i: nt - 1 - i
    return _call(body, name, (nt,),
                 [pl.BlockSpec((tm, LANES), lambda i: (rev(i), small_cb)), _bspec(LANES),
                  pl.BlockSpec((LANES, tm), lambda i: (0, rev(i))), pl.BlockSpec((tm, LANES), lambda i: (rev(i), 0))],
                 (pl.BlockSpec((tm, LANES), lambda i: (rev(i), 0)), _bspec(LANES)),
                 (SDS((S, LANES), f32), SDS((1, LANES), f32)), scratch=[pltpu.VMEM((1, LANES), f32)])(proj, fbias_row, dcT, dsm_dn)


PAD_ROWS = 8


def conv_fwd(proj, w, width, name):
    S = proj.shape[0]
    cw = 256 if width % 256 == 0 else 128

    def body(x_ref, w_ref, y_ref, pad_ref):
        pad_ref[pl.ds(0, PAD_ROWS), :] = jnp.zeros((PAD_ROWS, cw), f32)
        pad_ref[pl.ds(PAD_ROWS, S), :] = x_ref[...]
        y = jnp.zeros((S, cw), f32)
        for i in range(CONV_WIDTH):
            y = y + pad_ref[pl.ds(PAD_ROWS - (CONV_WIDTH - 1) + i, S), :] * w_ref[pl.ds(i, 1), :]
        y_ref[...] = y * jax.nn.sigmoid(y)

    spec = pl.BlockSpec((S, cw), lambda j: (0, j))
    return _call(body, name, (width // cw,), [spec, pl.BlockSpec((CONV_WIDTH, cw), lambda j: (0, j))], spec,
                 SDS((S, width), f32), scratch=[pltpu.VMEM((S + PAD_ROWS, cw), f32)])(proj, w)


def conv_bwd(proj, w, dy, name):
    S, width = dy.shape
    cw = 256 if width % 256 == 0 else 128

    def body(x_ref, w_ref, dy_ref, dx_ref, dw_ref, xpad_ref, dpad_ref):
        xpad_ref[pl.ds(0, PAD_ROWS), :] = jnp.zeros((PAD_ROWS, cw), f32)
        xpad_ref[pl.ds(PAD_ROWS, S), :] = x_ref[...]
        y = jnp.zeros((S, cw), f32)
        for i in range(CONV_WIDTH):
            y = y + xpad_ref[pl.ds(PAD_ROWS - (CONV_WIDTH - 1) + i, S), :] * w_ref[pl.ds(i, 1), :]
        sg = jax.nn.sigmoid(y)
        dpre = dy_ref[...] * (sg * (1.0 + y * (1.0 - sg)))
        dpad_ref[pl.ds(S, PAD_ROWS), :] = jnp.zeros((PAD_ROWS, cw), f32)
        dpad_ref[pl.ds(0, S), :] = dpre
        dx = jnp.zeros((S, cw), f32)
        for i in range(CONV_WIDTH):
            dx = dx + dpad_ref[pl.ds(CONV_WIDTH - 1 - i, S), :] * w_ref[pl.ds(i, 1), :]
            dw_ref[pl.ds(i, 1), :] = jnp.sum(dpre * xpad_ref[pl.ds(PAD_ROWS - (CONV_WIDTH - 1) + i, S), :], axis=0, keepdims=True)
        dx_ref[...] = dx

    spec = pl.BlockSpec((S, cw), lambda j: (0, j))
    wspec = pl.BlockSpec((CONV_WIDTH, cw), lambda j: (0, j))
    return _call(body, name, (width // cw,), [spec, wspec, spec], (spec, wspec),
                 (SDS((S, width), f32), SDS((CONV_WIDTH, width), f32)),
                 scratch=[pltpu.VMEM((S + PAD_ROWS, cw), f32), pltpu.VMEM((S + PAD_ROWS, cw), f32)])(proj, w, dy)


def _pdot_raw(a, b, dims, passes):
    if passes == 1:
        return _dot(a.astype(bf16), b.astype(bf16), dims)
    ah, al = _split_bf16(a)
    bh, bl = _split_bf16(b)
    return _dot(ah, bh, dims) + (_dot(ah, bl, dims) + _dot(al, bh, dims))


def _make_pdot(passes):
    @functools.partial(jax.custom_vjp, nondiff_argnums=(2,))
    def pdot(a, b, mode):
        return _pdot_raw(a, b, {"nn": NN, "nt": NT, "tn": TN}[mode], passes)

    def fwd(a, b, mode):
        return pdot(a, b, mode), (a, b)

    def bwd(mode, res, g):
        a, b = res
        if mode == "nn":
            return _pdot_raw(g, b, NT, passes), _pdot_raw(a, g, TN, passes)
        if mode == "nt":
            return _pdot_raw(g, b, NN, passes), _pdot_raw(g, a, TN, passes)
        return _pdot_raw(b, g, NT, passes), _pdot_raw(a, g, NN, passes)

    pdot.defvjp(fwd, bwd)
    return pdot


_dot1 = _make_pdot(1)
_dot3 = _make_pdot(3)


def _each(f, *lists):
    return [f(*xs) for xs in zip(*lists)]


def _dn_chunk(ndn, cq, ck, cv, small, alog, dtb, s0):
    C = cq[0].shape[0]
    hs = list(range(ndn))
    b = [_lane_pick(small, h) for h in hs]
    d = [_lane_pick(small, ndn + h) for h in hs]
    al = [_lane_pick(alog, h) for h in hs]
    dt = [_lane_pick(dtb, h) for h in hs]
    q = _each(lambda x: x * lax.rsqrt(jnp.sum(x * x, axis=1, keepdims=True) + EPS) * (HEAD_DIM ** -0.5), cq)
    k = _each(lambda x: x * lax.rsqrt(jnp.sum(x * x, axis=1, keepdims=True) + EPS), ck)
    beta = _each(jax.nn.sigmoid, b)
    g = _each(lambda al_, d_, dt_: -jnp.exp(al_) * _softplus(d_ + dt_), al, d, dt)
    r = lax.broadcasted_iota(jnp.int32, (C, C), 0)
    c = lax.broadcasted_iota(jnp.int32, (C, C), 1)
    incl, strict = r >= c, r > c
    inclf = incl.astype(f32)
    gc = _each(lambda g_: _dot3(inclf, g_, "nn"), g)
    decay = _each(lambda gc_: jnp.where(incl, jnp.exp(jnp.where(incl, gc_ - gc_.T, 0.0)), 0.0), gc)
    kb = _each(lambda k_, b_: k_ * b_, k, beta)
    a = _each(lambda kb_, k_, dec: jnp.where(strict, _dot3(kb_, k_, "nt") * dec, 0.0), kb, k, decay)
    eye = (r == c).astype(f32)
    t = _each(lambda a_: eye - a_, a)
    p = a
    for _ in range(int(math.log2(C)) - 1):
        p = _each(lambda p_: _dot3(p_, p_, "nn"), p)
        t = _each(lambda t_, p_: t_ + _dot3(t_, p_, "nn"), t, p)
    eg = _each(jnp.exp, gc)
    u = _each(lambda t_, v_, b_: _dot3(t_, v_ * b_, "nn"), t, cv, beta)
    w = _each(lambda t_, kb_, eg_: _dot3(t_, kb_ * eg_, "nn"), t, kb, eg)
    attn = _each(lambda q_, k_, dec: _dot1(q_, k_, "nt") * dec, q, k, decay)
    g_last = _each(lambda g_: jnp.sum(g_, axis=0, keepdims=True), g)
    v_new = _each(lambda u_, w_, s_: u_ - _dot1(w_, s_, "nn"), u, w, s0)
    o = _each(lambda q_, eg_, s_, at, vn: _dot1(q_ * eg_, s_, "nn") + _dot1(at, vn, "nn"), q, eg, s0, attn, v_new)
    s1 = _each(lambda s_, gl, k_, gc_, vn: s_ * jnp.exp(gl) + _dot1(k_ * jnp.exp(gl - gc_), vn, "tn"), s0, g_last, k, gc, v_new)
    return o, s1


def dn_fwd(cqkv, proj, small_cb, alog_row, dt_row, ndn, name, comm=None):
    S = cqkv.shape[0]
    C = DN_CHUNK
    nc = S // C
    half = ndn * HEAD_DIM

    def body(q_ref, k_ref, v_ref, sm_ref, al_ref, dt_ref, o_ref, ss_ref, s_ref):
        @pl.when(pl.program_id(0) == 0)
        def _():
            s_ref[...] = jnp.zeros_like(s_ref)

        sls = [slice(h * HEAD_DIM, (h + 1) * HEAD_DIM) for h in range(ndn)]
        s0 = [s_ref[h] for h in range(ndn)]
        for h in range(ndn):
            ss_ref[h, 0] = s0[h]
        o, s1 = _dn_chunk(ndn, [q_ref[:, sl] for sl in sls], [k_ref[:, sl] for sl in sls], [v_ref[:, sl] for sl in sls],
                          sm_ref[...], al_ref[...], dt_ref[...], s0)
        for h in range(ndn):
            o_ref[:, sls[h]] = o[h]
            s_ref[h] = s1[h]

    blk = lambda cb: pl.BlockSpec((C, half), lambda n: (n, cb))
    row = pl.BlockSpec((1, LANES), lambda n: (0, 0))
    return _call(body, name, (nc,),
                 [blk(0), blk(1), blk(2), pl.BlockSpec((C, LANES), lambda n: (n, small_cb)), row, row],
                 (blk(0), pl.BlockSpec((ndn, 1, HEAD_DIM, HEAD_DIM), lambda n: (0, n, 0, 0))),
                 (SDS((S, half), f32), SDS((ndn, nc, HEAD_DIM, HEAD_DIM), f32)),
                 scratch=[pltpu.VMEM((ndn, HEAD_DIM, HEAD_DIM), f32)], comm=comm)(cqkv, cqkv, cqkv, proj, alog_row, dt_row)


def dn_bwd(cqkv, proj, small_cb, alog_row, dt_row, ssave, do, ndn, name, comm=None):
    S = cqkv.shape[0]
    C = DN_CHUNK
    nc = S // C
    half = ndn * HEAD_DIM

    def body(q_ref, k_ref, v_ref, sm_ref, al_ref, dt_ref, ss_ref, do_ref, dqkv_ref, dsm_ref, dal_ref, ddt_ref, ds_ref):
        @pl.when(pl.program_id(0) == 0)
        def _():
            ds_ref[...] = jnp.zeros_like(ds_ref)
            dal_ref[...] = jnp.zeros_like(dal_ref)
            ddt_ref[...] = jnp.zeros_like(ddt_ref)

        sls = [slice(h * HEAD_DIM, (h + 1) * HEAD_DIM) for h in range(ndn)]
        _, vjp = jax.vjp(functools.partial(_dn_chunk, ndn), [q_ref[:, sl] for sl in sls], [k_ref[:, sl] for sl in sls],
                         [v_ref[:, sl] for sl in sls], sm_ref[...], al_ref[...], dt_ref[...], [ss_ref[h, 0] for h in range(ndn)])
        dq, dk, dv, dsm, dal, ddt, ds0 = vjp(([do_ref[:, sl] for sl in sls], [ds_ref[h] for h in range(ndn)]))
        for h in range(ndn):
            dqkv_ref[:, sls[h]] = dq[h]
            dqkv_ref[:, half + h * HEAD_DIM:half + (h + 1) * HEAD_DIM] = dk[h]
            dqkv_ref[:, 2 * half + h * HEAD_DIM:2 * half + (h + 1) * HEAD_DIM] = dv[h]
            ds_ref[h] = ds0[h]
        dsm_ref[...] = dsm
        dal_ref[...] += dal
        ddt_ref[...] += ddt

    rev = lambda n: nc - 1 - n
    blk = lambda cb: pl.BlockSpec((C, half), lambda n: (rev(n), cb))
    row = pl.BlockSpec((1, LANES), lambda n: (0, 0))
    return _call(body, name, (nc,),
                 [blk(0), blk(1), blk(2), pl.BlockSpec((C, LANES), lambda n: (rev(n), small_cb)), row, row,
                  pl.BlockSpec((ndn, 1, HEAD_DIM, HEAD_DIM), lambda n: (0, rev(n), 0, 0)), blk(0)],
                 (pl.BlockSpec((C, 3 * half), lambda n: (rev(n), 0)), pl.BlockSpec((C, LANES), lambda n: (rev(n), 0)), row, row),
                 (SDS((S, 3 * half), f32), SDS((S, LANES), f32), SDS((1, LANES), f32), SDS((1, LANES), f32)),
                 scratch=[pltpu.VMEM((ndn, HEAD_DIM, HEAD_DIM), f32)], comm=comm)(cqkv, cqkv, cqkv, proj, alog_row, dt_row, ssave, do)


def even_pre(proj, gq, gk, dfx, cb0, name):
    S = proj.shape[0]
    tm = _tile(S, 256)
    nh = dfx // HEAD_DIM

    def body(q_ref, k_ref, v_ref, gq_ref, gk_ref, fq_ref, fk_ref, fv_ref):
        for h in range(nh):
            sl = slice(h * HEAD_DIM, (h + 1) * HEAD_DIM)
            fq_ref[:, sl] = _rms(q_ref[:, sl], gq_ref[...]).astype(bf16)
            fk_ref[:, sl] = _rms(k_ref[:, sl], gk_ref[...]).astype(bf16)
        fv_ref[...] = v_ref[...].astype(bf16)

    sh = SDS((S, dfx), bf16)
    o = _rowspec(tm, dfx)
    return _call(body, name, (S // tm,),
                 [_rowspec(tm, dfx, cb0), _rowspec(tm, dfx, cb0 + 1), _rowspec(tm, dfx, cb0 + 2), _bspec(HEAD_DIM), _bspec(HEAD_DIM)],
                 (o, o, o), (sh, sh, sh))(proj, proj, proj, gq, gk)


def even_pre_bwd(proj, gq, gk, dfq, dfk, dfx, cb0, name):
    S = proj.shape[0]
    tm = _tile(S, 256)
    nh = dfx // HEAD_DIM

    def body(q_ref, k_ref, gq_ref, gk_ref, dfq_ref, dfk_ref, dq_ref, dk_ref, dgq_ref, dgk_ref):
        @pl.when(pl.program_id(0) == 0)
        def _():
            dgq_ref[...] = jnp.zeros_like(dgq_ref)
            dgk_ref[...] = jnp.zeros_like(dgk_ref)

        for h in range(nh):
            sl = slice(h * HEAD_DIM, (h + 1) * HEAD_DIM)
            _, vq = jax.vjp(_rms, q_ref[:, sl], gq_ref[...])
            dq, dgq = vq(dfq_ref[:, sl])
            _, vk = jax.vjp(_rms, k_ref[:, sl], gk_ref[...])
            dk, dgk = vk(dfk_ref[:, sl])
            dq_ref[:, sl] = dq
            dk_ref[:, sl] = dk
            dgq_ref[...] += dgq
            dgk_ref[...] += dgk

    sh = SDS((S, dfx), f32)
    o = _rowspec(tm, dfx)
    g = SDS((1, HEAD_DIM), f32)
    return _call(body, name, (S // tm,),
                 [_rowspec(tm, dfx, cb0), _rowspec(tm, dfx, cb0 + 1), _bspec(HEAD_DIM), _bspec(HEAD_DIM), o, o],
                 (o, o, _bspec(HEAD_DIM), _bspec(HEAD_DIM)), (sh, sh, g, g))(proj, proj, gq, gk, dfq, dfk)


def _dn_out(o, gate, gn):
    return _rms(o, gn) * (gate * jax.nn.sigmoid(gate))


def _fox_out(o, gate):
    return o * jax.nn.sigmoid(gate)


def even_post(o_dn, o_fox, proj, gn, half, cb_dn_gate, cb_fox_gate, name):
    S = proj.shape[0]
    tm = _tile(S, 256)
    nh = half // HEAD_DIM

    def body(od_ref, of_ref, gd_ref, gf_ref, gn_ref, o_ref):
        for h in range(nh):
            sl = slice(h * HEAD_DIM, (h + 1) * HEAD_DIM)
            o_ref[:, sl] = _dn_out(od_ref[:, sl], gd_ref[:, sl], gn_ref[...]).astype(bf16)
        o_ref[:, half:] = _fox_out(of_ref[...], gf_ref[...]).astype(bf16)

    return _call(body, name, (S // tm,),
                 [_rowspec(tm, half), _rowspec(tm, half), _rowspec(tm, half, cb_dn_gate), _rowspec(tm, half, cb_fox_gate), _bspec(HEAD_DIM)],
                 _rowspec(tm, 2 * half), SDS((S, 2 * half), bf16))(o_dn, o_fox, proj, proj, gn)


def even_post_bwd(o_dn, o_fox, proj, gn, dom, half, cb_dn_gate, cb_fox_gate, name):
    S = proj.shape[0]
    tm = _tile(S, 256)
    nh = half // HEAD_DIM

    def body(od_ref, of_ref, gd_ref, gf_ref, gn_ref, dod_ref, dof_ref, dd_ref, df_ref, dgd_ref, dgf_ref, dgn_ref):
        @pl.when(pl.program_id(0) == 0)
        def _():
            dgn_ref[...] = jnp.zeros_like(dgn_ref)

        for h in range(nh):
            sl = slice(h * HEAD_DIM, (h + 1) * HEAD_DIM)
            _, vjp = jax.vjp(_dn_out, od_ref[:, sl], gd_ref[:, sl], gn_ref[...])
            d_o, d_gate, d_gn = vjp(dod_ref[:, sl])
            dd_ref[:, sl] = d_o
            dgd_ref[:, sl] = d_gate
            dgn_ref[...] += d_gn
        _, vjp = jax.vjp(_fox_out, of_ref[...], gf_ref[...])
        d_o, d_gate = vjp(dof_ref[...])
        df_ref[...] = d_o
        dgf_ref[...] = d_gate

    sh = SDS((S, half), f32)
    o = _rowspec(tm, half)
    return _call(body, name, (S // tm,),
                 [o, o, _rowspec(tm, half, cb_dn_gate), _rowspec(tm, half, cb_fox_gate), _bspec(HEAD_DIM),
                  _rowspec(tm, half, 0), _rowspec(tm, half, 1)],
                 (o, o, o, o, _bspec(HEAD_DIM)), (sh, sh, sh, sh, SDS((1, HEAD_DIM), f32)))(o_dn, o_fox, proj, proj, gn, dom, dom)


def _pad_row(v, lane0=0):
    return jnp.pad(v, (lane0, LANES - lane0 - v.shape[0]))[None]


def _carried(res, comm):
    return res if comm is not None else (res, [])


def even_mixer_fwd(x, gmix, w_in, w_out, conv_w, a_log, dt_bias, gn, gq, gk, f_bias, tag, comm_fox=None, comm_dn=None,
                   comm_in=None):
    S, D = x.shape
    half = D // 2
    ndn = half // HEAD_DIM
    small_cb = 4 * D // LANES
    alog_row, dt_row, fb_row = _pad_row(a_log), _pad_row(dt_bias), _pad_row(f_bias, 2 * ndn)
    h = rms_fwd(x, gmix, f"{tag}_rms")
    proj, got_in = _carried(mm(h, w_in, "nn", f32, f"{tag}_in", tm=512, tn=1408, comm=comm_in), comm_in)
    cqkv = conv_fwd(proj, conv_w, 3 * half, f"{tag}_conv")
    (o_dn, ssave), got_dn = _carried(dn_fwd(cqkv, proj, small_cb, alog_row, dt_row, ndn, f"{tag}_dn", comm=comm_dn), comm_dn)
    c, cT = gates_fwd(proj, small_cb, fb_row, f"{tag}_gates")
    fq, fk, fv = even_pre(proj, gq, gk, half, 4, f"{tag}_pre")
    (o_fox, lse), got_fox = _carried(fox_fwd(fq, fk, fv, c, cT, ndn, 2 * ndn, f"{tag}_fox", comm=comm_fox), comm_fox)
    om = even_post(o_dn, o_fox, proj, gn, half, 3, 7, f"{tag}_post")
    xo = mm(om, w_out, "nn", f32, f"{tag}_out", res=x)
    return xo, (h, proj, cqkv, o_dn, ssave, c, cT, fq, fk, fv, o_fox, lse, om, alog_row, dt_row, fb_row), got_fox, got_dn, got_in


def even_mixer_bwd(x, gmix, w_in, w_out, conv_w, gn, gq, gk, saved, dy, tag, comm_fox=None, comm_dn=None):
    S, D = x.shape
    half = D // 2
    ndn = half // HEAD_DIM
    small_cb = 4 * D // LANES
    h, proj, cqkv, o_dn, ssave, c, cT, fq, fk, fv, o_fox, lse, om, alog_row, dt_row, fb_row = saved
    dom = mm(dy, w_out, "nt", f32, f"{tag}_bdom")
    dw_out = mm(om, dy, "tn", bf16, f"{tag}_bwout")
    d_odn, d_ofox, d_dgate, d_fgate, d_gn = even_post_bwd(o_dn, o_fox, proj, gn, dom, half, 3, 7, f"{tag}_bpost")
    (dfq, dfk, dfv, dcT), got_fox = _carried(
        fox_bwd(fq, fk, fv, c, cT, o_fox, lse, d_ofox, ndn, 2 * ndn, f"{tag}_bfox", comm=comm_fox), comm_fox)
    dq_pre, dk_pre, d_gq, d_gk = even_pre_bwd(proj, gq, gk, dfq, dfk, half, 4, f"{tag}_bpre")
    (dcqkv, dsm_dn, d_alog, d_dt), got_dn = _carried(
        dn_bwd(cqkv, proj, small_cb, alog_row, dt_row, ssave, d_odn, ndn, f"{tag}_bdn", comm=comm_dn), comm_dn)
    d_conv_in, d_conv_w = conv_bwd(proj, conv_w, dcqkv, f"{tag}_bconv")
    d_small, d_fb = gates_bwd(proj, small_cb, fb_row, dcT, dsm_dn, 2 * ndn, ndn, f"{tag}_bgates")
    dproj = jnp.concatenate([d_conv_in, d_dgate, dq_pre, dk_pre, dfv, d_fgate, d_small], axis=1)
    dw_in = mm(h, dproj, "tn", bf16, f"{tag}_bwin", tn=384)
    dh = mm(dproj, w_in, "nt", f32, f"{tag}_bdh")
    dx, d_gmix = rms_bwd(x, gmix, dh, dy, f"{tag}_brms")
    small = dict(norm_mix=d_gmix[0], dn_conv_w=d_conv_w, dn_a_log=d_alog[0, :ndn], dn_dt_bias=d_dt[0, :ndn],
                 dn_norm_g=d_gn[0], fox_q_norm_g=d_gq[0], fox_k_norm_g=d_gk[0], fox_f_bias=d_fb[0, 2 * ndn:3 * ndn])
    return dx, dw_in, dw_out, small, got_fox, got_dn


def odd_mixer_fwd(x, gmix, w_in, w_out, tag, comm=None):
    S, D = x.shape
    nh = D // HEAD_DIM
    h = rms_fwd(x, gmix, f"{tag}_rms")
    qkv = mm(h, w_in, "nn", bf16, f"{tag}_in", tn=768)
    (o, tails), got = _carried(sb_fwd(qkv, nh, f"{tag}_sb", comm=comm), comm)
    xo = mm(o, w_out, "nn", f32, f"{tag}_out", res=x)
    return xo, (h, qkv, o, tails), got


def odd_mixer_bwd(x, gmix, w_in, w_out, saved, dy, tag, comm=None):
    S, D = x.shape
    nh = D // HEAD_DIM
    h, qkv, o, tails = saved
    do = mm(dy, w_out, "nt", f32, f"{tag}_bdo")
    dw_out = mm(o, dy, "tn", bf16, f"{tag}_bwout")
    (dq, dk, dv), got = _carried(sb_bwd(qkv, tails, do, nh, f"{tag}_bsb", comm=comm), comm)
    dqkv = jnp.concatenate([dq, dk, dv], axis=1)
    dw_in = mm(h, dqkv, "tn", bf16, f"{tag}_bwin", tn=768)
    dh = mm(dqkv, w_in, "nt", f32, f"{tag}_bdh")
    dx, d_gmix = rms_bwd(x, gmix, dh, dy, f"{tag}_brms")
    return dx, dw_in, dw_out, dict(norm_mix=d_gmix[0]), got


def all_gather(shards, axes, name):
    return _call(None, name, (), [], [], [], comm=Comm("ag", shards, axes))()[1]


def scatter_parts(fulls, axes, name):
    return _call(None, name, (), [], [], [], comm=Comm("rs", fulls, axes))()[1]


def sum_parts(parts, name):
    _, R, C = parts.shape
    tm = _tile(R, 256)

    def body(p_ref, o_ref):
        acc = p_ref[0].astype(f32)
        for k in range(1, NDEV):
            acc = acc + p_ref[k].astype(f32)
        o_ref[...] = acc

    return _call(body, name, (R // tm,), [pl.BlockSpec((NDEV, tm, C), lambda i: (0, i, 0))], _rowspec(tm, C), SDS((R, C), f32))(parts)


def adamw(w, g, m, v, name):
    R, C = w.shape
    tm = _tile(R, max(8, min(512, (ADAMW_TILE_ELEMS // C) // 8 * 8)))
    c1 = 1.0 - ADAM_B1 ** ADAM_STEP
    c2 = 1.0 - ADAM_B2 ** ADAM_STEP

    def body(w_ref, g_ref, m_ref, v_ref, d_ref, nm_ref, nv_ref):
        g_ = g_ref[...]
        nm = ADAM_B1 * m_ref[...] + (1.0 - ADAM_B1) * g_
        nv = ADAM_B2 * v_ref[...] + (1.0 - ADAM_B2) * (g_ * g_)
        d_ref[...] = -ADAM_LR * ((nm / c1) / (jnp.sqrt(nv / c2) + ADAM_EPS) + ADAM_WD * w_ref[...])
        nm_ref[...] = nm
        nv_ref[...] = nv

    spec = _rowspec(tm, C)
    sh = SDS((R, C), f32)
    return _call(body, name, (R // tm,), [spec] * 4, (spec, spec, spec), (sh, sh, sh))(w, g, m, v)


def _pad_to(n, mult):
    return -(-n // mult) * mult


def _even_perm(D):
    half = D // 2
    ndn = half // HEAD_DIM
    o_gate = 3 * half
    o_b = o_gate + half
    o_a = o_b + ndn
    o_fq = o_a + ndn
    o_fpre = o_fq + 4 * half
    return half, ndn, o_b, o_a, o_fq, o_fpre


def _even_to_mine(w):
    D = (w.shape[-1] // 4) // LANES * LANES
    half, ndn, o_b, o_a, o_fq, o_fpre = _even_perm(D)
    small = jnp.concatenate([w[..., o_b:o_b + ndn], w[..., o_a:o_a + ndn], w[..., o_fpre:o_fpre + ndn]], axis=-1)
    small = jnp.pad(small, [(0, 0)] * (w.ndim - 1) + [(0, LANES - 3 * ndn)])
    return jnp.concatenate([w[..., :o_b], w[..., o_fq:o_fpre], small], axis=-1)


def _even_from_mine(w, D):
    half, ndn, o_b, o_a, o_fq, o_fpre = _even_perm(D)
    sm = w[..., 4 * D:]
    return jnp.concatenate([w[..., :o_b], sm[..., :ndn], sm[..., ndn:2 * ndn], w[..., o_b:4 * D], sm[..., 2 * ndn:3 * ndn]], axis=-1)


def _pack_small(parts):
    flat = jnp.concatenate([p.reshape(-1).astype(f32) for p in parts])
    n = flat.shape[0]
    return jnp.pad(flat, (0, _pad_to(n, 8 * LANES) - n)).reshape(-1, LANES)


def _unpack_small(packed, like):
    flat = packed.reshape(-1)
    out, off = [], 0
    for p in like:
        out.append(flat[off:off + p.size].reshape(p.shape))
        off += p.size
    return out


SMALL_NAMES = ("norm_ffn1", "norm_mix", "dn_a_log", "dn_dt_bias", "dn_norm_g", "fox_q_norm_g", "fox_k_norm_g", "fox_f_bias", "norm_ffn2")
BIG_NAMES = ("ffn1_w_gu", "ffn1_w_down", "w_in_even", "dn_conv_w", "w_out_even", "w_in_odd", "w_out_odd", "ffn2_w_gu", "ffn2_w_down")
WEIGHT_NAMES = ("norm_ffn1", "ffn1_w_gu", "ffn1_w_down", "norm_mix", "w_in_even", "dn_conv_w", "dn_a_log", "dn_dt_bias", "dn_norm_g",
                "fox_q_norm_g", "fox_k_norm_g", "fox_f_bias", "w_out_even", "w_in_odd", "w_out_odd", "norm_ffn2", "ffn2_w_gu", "ffn2_w_down")


def kernel(x, norm_ffn1, ffn1_w_gu, ffn1_w_down, norm_mix, w_in_even, dn_conv_w, dn_a_log, dn_dt_bias, dn_norm_g, fox_q_norm_g, fox_k_norm_g, fox_f_bias, w_out_even, w_in_odd, w_out_odd, norm_ffn2, ffn2_w_gu, ffn2_w_down, loss_target, m_norm_ffn1, m_ffn1_w_gu, m_ffn1_w_down, m_norm_mix, m_w_in_even, m_dn_conv_w, m_dn_a_log, m_dn_dt_bias, m_dn_norm_g, m_fox_q_norm_g, m_fox_k_norm_g, m_fox_f_bias, m_w_out_even, m_w_in_odd, m_w_out_odd, m_norm_ffn2, m_ffn2_w_gu, m_ffn2_w_down, v_norm_ffn1, v_ffn1_w_gu, v_ffn1_w_down, v_norm_mix, v_w_in_even, v_dn_conv_w, v_dn_a_log, v_dn_dt_bias, v_dn_norm_g, v_fox_q_norm_g, v_fox_k_norm_g, v_fox_f_bias, v_w_out_even, v_w_in_odd, v_w_out_odd, v_norm_ffn2, v_ffn2_w_gu, v_ffn2_w_down):
    args = dict(locals())
    W = {n: args[n] for n in WEIGHT_NAMES}
    M = {n: args["m_" + n] for n in WEIGHT_NAMES}
    V = {n: args["v_" + n] for n in WEIGHT_NAMES}
    xs = x[0]
    tgt = loss_target[0]
    S, D = xs.shape
    L = norm_ffn1.shape[0]
    n_even = w_in_even.shape[0]
    hs = ffn1_w_down.shape[1]
    hp = _pad_to(hs, LANES)
    me = 4 * lax.axis_index("x") + 2 * lax.axis_index("y") + lax.axis_index("c")

    def prep_gu(w):
        w = w.reshape(L, D, 2, hs)
        return jnp.pad(w, ((0, 0), (0, 0), (0, 0), (0, hp - hs))).reshape(L, D, 2 * hp).astype(bf16)

    def prep_down(w):
        return jnp.pad(w, ((0, 0), (0, hp - hs), (0, 0))).astype(bf16)

    sh_gu1, sh_d1, sh_gu2, sh_d2 = prep_gu(ffn1_w_gu), prep_down(ffn1_w_down), prep_gu(ffn2_w_gu), prep_down(ffn2_w_down)
    sh_ie, sh_oe = _even_to_mine(w_in_even).astype(bf16), w_out_even.astype(bf16)
    sh_io, sh_oo = w_in_odd.astype(bf16), w_out_odd.astype(bf16)

    def layer_shards(l):
        j = l // 2
        mix = [sh_ie[j], sh_oe[j]] if l % 2 == 0 else [sh_io[j], sh_oo[j]]
        return [sh_gu1[l], sh_d1[l], *mix, sh_gu2[l], sh_d2[l]]

    def layer_axes(l):
        return [1, 0, 0 if l % 2 == 0 else 1, 0, 1, 0]

    def layer_names(l):
        return ["ffn1_w_gu", "ffn1_w_down", *(["w_in_even", "w_out_even"] if l % 2 == 0 else ["w_in_odd", "w_out_odd"]),
                "ffn2_w_gu", "ffn2_w_down"]

    FOX_CARRIES, DN_CARRIES = (0, 4, 3), (1, 2, 5)

    def split_comm(kind, arrays, axes):
        return (Comm(kind, [arrays[i] for i in FOX_CARRIES], [axes[i] for i in FOX_CARRIES]),
                Comm(kind, [arrays[i] for i in DN_CARRIES], [axes[i] for i in DN_CARRIES]))

    def join_split(got_fox, got_dn):
        out = [None] * 6
        for i, a in zip(FOX_CARRIES, got_fox):
            out[i] = a
        for i, a in zip(DN_CARRIES, got_dn):
            out[i] = a
        return out

    first = all_gather(layer_shards(0) + [dn_conv_w[None]], layer_axes(0) + [0], "ag_layer0")
    weights = {0: first[:6]}
    convw = jnp.moveaxis(first[6], 0, 2).reshape(n_even, CONV_WIDTH, -1)

    EVEN_FWD_CARRIES = dict(f1_gu=(), mx_in=(), dn=(1, 2, 5), fox=(0, 4, 3), f2_gu=(), f2_down=())
    saved = []
    cur = xs
    for l in range(L):
        j = l // 2
        wgu1, wd1, win, wout, wgu2, wd2 = weights[l]
        nxt = l + 1 < L
        x0 = cur
        if l % 2 == 0:
            sh_n, ax_n = (layer_shards(l + 1), layer_axes(l + 1)) if nxt else (None, None)
            cm = {k: (Comm("ag", [sh_n[i] for i in idx], [ax_n[i] for i in idx]) if nxt and idx else None)
                  for k, idx in EVEN_FWD_CARRIES.items()}
            x1, s1, got_f1gu, _ = ffn_fwd(x0, norm_ffn1[l:l + 1], wgu1, wd1, f"l{l}f1", comm_gu=cm["f1_gu"])
            x2, sm, got_f, got_d, got_in = even_mixer_fwd(
                x1, norm_mix[l:l + 1], win, wout, convw[j], dn_a_log[j], dn_dt_bias[j], dn_norm_g[j:j + 1],
                fox_q_norm_g[j:j + 1], fox_k_norm_g[j:j + 1], fox_f_bias[j], f"l{l}mx", comm_fox=cm["fox"], comm_dn=cm["dn"],
                comm_in=cm["mx_in"])
            x3, s2, got_f2gu, got_f2down = ffn_fwd(x2, norm_ffn2[l:l + 1], wgu2, wd2, f"l{l}f2", comm_gu=cm["f2_gu"],
                                                   comm_down=cm["f2_down"])
            if nxt:
                got = dict(f1_gu=got_f1gu, mx_in=got_in, dn=got_d, fox=got_f, f2_gu=got_f2gu, f2_down=got_f2down)
                weights[l + 1] = [None] * 6
                for k, idx in EVEN_FWD_CARRIES.items():
                    for i, a in zip(idx, got[k]):
                        weights[l + 1][i] = a
        else:
            x1, s1, _, _ = ffn_fwd(x0, norm_ffn1[l:l + 1], wgu1, wd1, f"l{l}f1")
            cm = Comm("ag", layer_shards(l + 1), layer_axes(l + 1)) if nxt else None
            x2, sm, got = odd_mixer_fwd(x1, norm_mix[l:l + 1], win, wout, f"l{l}mx", comm=cm)
            if nxt:
                weights[l + 1] = got
            x3, s2, _, _ = ffn_fwd(x2, norm_ffn2[l:l + 1], wgu2, wd2, f"l{l}f2")
        saved.append((x0, x1, x2, s1, sm, s2))
        cur = x3
    dy, loss_row = loss_head(cur, tgt, "loss")

    gsmall = {n: [None] * W[n].shape[0] for n in SMALL_NAMES}
    gconv = [None] * n_even
    parts = {n: [None] * W[n].shape[0] for n in BIG_NAMES if n != "dn_conv_w"}

    def take(l, recv):
        for nm, p in zip(layer_names(l), recv):
            idx = l if nm.startswith("ffn") else l // 2
            parts[nm][idx] = sum_parts(p.reshape(NDEV, -1, p.shape[-1]), f"sum_{nm}_{idx}").reshape(p.shape[1:])

    pending = None
    for l in reversed(range(L)):
        j = l // 2
        wgu1, wd1, win, wout, wgu2, wd2 = weights[l]
        x0, x1, x2, s1, sm, s2 = saved[l]
        dy, dg, dwgu2, dwd2 = ffn_bwd(x2, norm_ffn2[l:l + 1], wgu2, wd2, s2, dy, f"l{l}f2")
        gsmall["norm_ffn2"][l] = dg[0]
        if l % 2 == 0:
            cf, cd = split_comm("rs", pending, layer_axes(l + 1)) if pending is not None else (None, None)
            dy, dwin, dwout, sg, got_f, got_d = even_mixer_bwd(
                x1, norm_mix[l:l + 1], win, wout, convw[j], dn_norm_g[j:j + 1], fox_q_norm_g[j:j + 1],
                fox_k_norm_g[j:j + 1], sm, dy, f"l{l}mx", comm_fox=cf, comm_dn=cd)
            if pending is not None:
                take(l + 1, join_split(got_f, got_d))
            gconv[j] = sg.pop("dn_conv_w")
            for k_, v_ in sg.items():
                gsmall[k_][l if k_ == "norm_mix" else j] = v_
        else:
            cm = Comm("rs", pending, layer_axes(l + 1)) if pending is not None else None
            dy, dwin, dwout, sg, got = odd_mixer_bwd(x1, norm_mix[l:l + 1], win, wout, sm, dy, f"l{l}mx", comm=cm)
            if pending is not None:
                take(l + 1, got)
            gsmall["norm_mix"][l] = sg["norm_mix"]
        dy, dg, dwgu1, dwd1 = ffn_bwd(x0, norm_ffn1[l:l + 1], wgu1, wd1, s1, dy, f"l{l}f1")
        gsmall["norm_ffn1"][l] = dg[0]
        pending = [dwgu1, dwd1, dwin, dwout, dwgu2, dwd2]
    take(0, scatter_parts(pending, layer_axes(0), "rs_layer0"))
    grad_x = dy[None]

    small_list = [jnp.stack(gsmall[n]) for n in SMALL_NAMES] + [jnp.stack(gconv), loss_row[0, :1]]
    packed = _pack_small(small_list)
    gathered = all_gather([packed], [0], "ag_small")[0]
    summed = sum_parts(gathered.reshape(NDEV, packed.shape[0], LANES), "sum_small")
    small_sum = _unpack_small(summed, small_list)
    G = dict(zip(SMALL_NAMES, small_sum[:len(SMALL_NAMES)]))
    conv_full = small_sum[len(SMALL_NAMES)]
    cwid = dn_conv_w.shape[2]
    G["dn_conv_w"] = lax.dynamic_slice_in_dim(conv_full, me * cwid, cwid, axis=2)
    loss = small_sum[-1][0]

    def unprep_gu(g):
        return g.reshape(L, D, 2, hp)[..., :hs].reshape(L, D, 2 * hs)

    G["ffn1_w_gu"] = unprep_gu(jnp.stack(parts["ffn1_w_gu"]))
    G["ffn2_w_gu"] = unprep_gu(jnp.stack(parts["ffn2_w_gu"]))
    G["ffn1_w_down"] = jnp.stack(parts["ffn1_w_down"])[:, :hs]
    G["ffn2_w_down"] = jnp.stack(parts["ffn2_w_down"])[:, :hs]
    G["w_in_even"] = _even_from_mine(jnp.stack(parts["w_in_even"]), D)
    G["w_out_even"] = jnp.stack(parts["w_out_even"])
    G["w_in_odd"] = jnp.stack(parts["w_in_odd"])
    G["w_out_odd"] = jnp.stack(parts["w_out_odd"])

    delta, new_m, new_v = {}, {}, {}
    for n in BIG_NAMES:
        shp = W[n].shape
        two = lambda a: a.reshape(-1, shp[-1])
        d_, m_, v_ = adamw(two(W[n]), two(G[n]), two(M[n]), two(V[n]), f"adamw_{n}")
        delta[n], new_m[n], new_v[n] = d_.reshape(shp), m_.reshape(shp), v_.reshape(shp)
    sw = [W[n] for n in SMALL_NAMES]
    d_, m_, v_ = adamw(_pack_small(sw), _pack_small([G[n] for n in SMALL_NAMES]), _pack_small([M[n] for n in SMALL_NAMES]),
                       _pack_small([V[n] for n in SMALL_NAMES]), "adamw_small")
    for n, a, b, c_ in zip(SMALL_NAMES, _unpack_small(d_, sw), _unpack_small(m_, sw), _unpack_small(v_, sw)):
        delta[n], new_m[n], new_v[n] = a, b, c_

    return (loss, grad_x, *[G[n] for n in WEIGHT_NAMES], *[delta[n] for n in WEIGHT_NAMES],
            *[new_m[n] for n in WEIGHT_NAMES], *[new_v[n] for n in WEIGHT_NAMES])
```

```python
import functools
import math

import jax
import jax.numpy as jnp
from jax import lax
from jax.experimental import pallas as pl
from jax.experimental.pallas import tpu as pltpu

f32 = jnp.float32
bf16 = jnp.bfloat16
SDS = jax.ShapeDtypeStruct

HEAD_DIM = 128
LANES = 128
CONV_WIDTH = 4
DN_CHUNK = 64
EPS = 1e-6
NDEV = 8
VMEM_LIMIT_BYTES = 56 * 1024 * 1024
HI = lax.Precision.HIGHEST
ADAMW_TILE_ELEMS = 384 * 1024

ADAM_LR = 0.001
ADAM_B1 = 0.9
ADAM_B2 = 0.999
ADAM_EPS = 1e-08
ADAM_WD = 0.01
ADAM_STEP = 10

NN = (((1,), (0,)), ((), ()))
NT = (((1,), (1,)), ((), ()))
TN = (((0,), (0,)), ((), ()))


def _me_and_peers():
    x, y, c = lax.axis_index("x"), lax.axis_index("y"), lax.axis_index("c")
    me = 4 * x + 2 * y + c
    peers = []
    for r in range(1, NDEV):
        px = 1 - x if (r >> 2) & 1 else x
        py = 1 - y if (r >> 1) & 1 else y
        pc = 1 - c if r & 1 else c
        peers.append(((px, py, pc), 4 * px + 2 * py + pc))
    return me, peers


def _slab(ref, axis, width, k):
    idx = [slice(None)] * len(ref.shape)
    idx[axis] = pl.ds(k * width, width)
    return ref.at[tuple(idx)]


class Comm:
    def __init__(self, kind, arrays, axes):
        self.kind, self.arrays, self.axes, self.n = kind, list(arrays), list(axes), len(arrays)

    def out_shape(self):
        out = []
        for a, ax in zip(self.arrays, self.axes):
            shp = list(a.shape)
            if self.kind != "rs":
                shp[ax] *= NDEV
                out.append(SDS(tuple(shp), a.dtype))
            else:
                shp[ax] //= NDEV
                out.append(SDS((NDEV, *shp), a.dtype))
        return out

    def sems(self):
        return [pltpu.SemaphoreType.DMA((self.n, NDEV - 1)), pltpu.SemaphoreType.DMA((self.n, NDEV - 1)),
                pltpu.SemaphoreType.DMA((self.n,))]

    def _src(self, ins, t, to_idx):
        if self.kind == "ag":
            return ins[t]
        return _slab(ins[t], self.axes[t], ins[t].shape[self.axes[t]] // NDEV, to_idx)

    def _dst(self, ins, outs, t, from_idx):
        if self.kind != "rs":
            return _slab(outs[t], self.axes[t], ins[t].shape[self.axes[t]], from_idx)
        return outs[t].at[from_idx]

    def _copies(self, ins, outs, sems, sending):
        send_sems, recv_sems, loc_sems = sems
        me, peers = _me_and_peers()
        local, remote = [], []
        for t in range(self.n):
            local.append(pltpu.make_async_copy(self._src(ins, t, me), self._dst(ins, outs, t, me), loc_sems.at[t]))
            for r, (peer, pidx) in enumerate(peers):
                remote.append(pltpu.make_async_remote_copy(
                    src_ref=self._src(ins, t, pidx), dst_ref=self._dst(ins, outs, t, me if sending else pidx),
                    send_sem=send_sems.at[t, r], recv_sem=recv_sems.at[t, r], device_id=peer,
                    device_id_type=pl.DeviceIdType.MESH))
        return local, remote

    def _two_level(self, ins, outs, sems):
        send_sems, recv_sems, loc_sems = sems
        x, y, c = lax.axis_index("x"), lax.axis_index("y"), lax.axis_index("c")
        me, sibling = (x, y, c), (x, y, 1 - c)
        chips = [(1 - x, y), (x, 1 - y), (1 - x, 1 - y)]
        dev = lambda p: 4 * p[0] + 2 * p[1] + p[2]

        def copy(t, k, block, to, from_shard):
            dst = self._dst(ins, outs, t, dev(block))
            return pltpu.make_async_remote_copy(
                src_ref=ins[t] if from_shard else dst, dst_ref=dst, send_sem=send_sems.at[t, k], recv_sem=recv_sems.at[t, k],
                device_id=to, device_id_type=pl.DeviceIdType.MESH)

        ts = range(self.n)
        local = [pltpu.make_async_copy(ins[t], self._dst(ins, outs, t, dev(me)), loc_sems.at[t]) for t in ts]
        first = [copy(t, 0, me, sibling, True) for t in ts]
        first += [copy(t, 1 + j, me, (*chip, c), True) for t in ts for j, chip in enumerate(chips)]
        return local, first, copy, chips, me, sibling, c

    def start(self, ins, outs, sems):
        if self.kind == "ag2":
            local, first = self._two_level(ins, outs, sems)[:2]
            for cp in local + first:
                cp.start()
            return
        local, remote = self._copies(ins, outs, sems, True)
        for cp in local + remote:
            cp.start()

    def wait(self, ins, outs, sems):
        if self.kind == "ag2":
            local, first, copy, chips, me, sibling, c = self._two_level(ins, outs, sems)
            passed = []
            for t in range(self.n):
                for j, chip in enumerate(chips):
                    copy(t, 1 + j, (*chip, c), me, False).wait_recv()
                    fwd = copy(t, 4 + j, (*chip, c), sibling, False)
                    fwd.start()
                    passed.append(fwd)
            for t in range(self.n):
                copy(t, 0, sibling, me, False).wait_recv()
                for j, chip in enumerate(chips):
                    copy(t, 4 + j, (*chip, 1 - c), me, False).wait_recv()
            for cp in first + passed:
                cp.wait_send()
            for cp in local:
                cp.wait()
            return
        local, remote = self._copies(ins, outs, sems, False)
        for cp in remote:
            cp.wait_recv()
            cp.wait_send()
        for cp in local:
            cp.wait()


def _call(body, name, grid, in_specs, out_specs, out_shape, scratch=(), comm=None):
    params = pltpu.CompilerParams(vmem_limit_bytes=VMEM_LIMIT_BYTES)
    if comm is None:
        return pl.pallas_call(body, name=name, grid=grid, in_specs=in_specs, out_specs=out_specs, out_shape=out_shape,
                              scratch_shapes=list(scratch), compiler_params=params)
    single = not isinstance(out_shape, (tuple, list))
    c_out_specs = [out_specs] if single else list(out_specs)
    c_out_shape = [out_shape] if single else list(out_shape)
    n_in, n_out, n_scr, n = len(in_specs), len(c_out_shape), len(scratch), comm.n
    anyspec = pl.BlockSpec(memory_space=pl.ANY)

    def wrapped(*refs):
        ins, refs = refs[:n_in], refs[n_in:]
        cins, refs = refs[:n], refs[n:]
        outs, refs = refs[:n_out], refs[n_out:]
        couts, refs = refs[:n], refs[n:]
        scr, sems = refs[:n_scr], refs[n_scr:]
        first = functools.reduce(lambda a, b: a & b, [pl.program_id(d) == 0 for d in range(len(grid))], True)
        last = functools.reduce(lambda a, b: a & b, [pl.program_id(d) == grid[d] - 1 for d in range(len(grid))], True)
        if grid:
            pl.when(first)(lambda: comm.start(cins, couts, sems))
            body(*ins, *outs, *scr)
            pl.when(last)(lambda: comm.wait(cins, couts, sems))
        else:
            comm.start(cins, couts, sems)
            if body is not None:
                body(*ins, *outs, *scr)
            comm.wait(cins, couts, sems)

    call = pl.pallas_call(
        wrapped, name=name, grid=grid, in_specs=list(in_specs) + [anyspec] * n, out_specs=c_out_specs + [anyspec] * n,
        out_shape=c_out_shape + comm.out_shape(), scratch_shapes=list(scratch) + comm.sems(), compiler_params=params)

    def run(*args):
        res = call(*args, *comm.arrays)
        mine = res[:n_out]
        return (mine[0] if single else tuple(mine)), list(res[n_out:])

    return run


def _tile(n, want, align=8):
    if n <= want:
        return n
    for t in range(want // align * align, 0, -align):
        if n % t == 0:
            return t
    return n


def _dot(a, b, dims=NN, precision=None):
    return lax.dot_general(a, b, dims, preferred_element_type=f32, precision=precision)


def _logsig(x):
    return jnp.minimum(x, 0.0) - jnp.log(1.0 + jnp.exp(-jnp.abs(x)))


def _softplus(x):
    return jnp.maximum(x, 0.0) + jnp.log(1.0 + jnp.exp(-jnp.abs(x)))


def _rms(x, g):
    return x * lax.rsqrt(jnp.mean(x * x, axis=-1, keepdims=True) + EPS) * g


def _lane_pick(blk, lane_idx):
    lane = lax.broadcasted_iota(jnp.int32, (1, LANES), 1)
    return jnp.sum(jnp.where(lane == lane_idx, blk, 0.0), axis=1, keepdims=True)


def mm(a, b, mode, out_dtype, name, tm=512, tn=512, res=None, scale=1.0, comm=None):
    if mode == "nn":
        (M, K), (_, N) = a.shape, b.shape
    elif mode == "nt":
        (M, K), (N, _) = a.shape, b.shape
    else:
        (K, M), (_, N) = a.shape, b.shape
    tm, tn = _tile(M, tm, LANES), _tile(N, tn, LANES)
    a_spec = pl.BlockSpec((K, tm), lambda j, i: (0, i)) if mode == "tn" else pl.BlockSpec((tm, K), lambda j, i: (i, 0))
    b_spec = pl.BlockSpec((tn, K), lambda j, i: (j, 0)) if mode == "nt" else pl.BlockSpec((K, tn), lambda j, i: (0, j))
    dims = {"nn": NN, "nt": NT, "tn": TN}[mode]
    o_spec = pl.BlockSpec((tm, tn), lambda j, i: (i, j))

    def body(*refs):
        acc = _dot(refs[0][...].astype(bf16), refs[1][...].astype(bf16), dims)
        if scale != 1.0:
            acc = acc * scale
        if res is not None:
            acc = acc + refs[2][...]
        refs[-1][...] = acc.astype(out_dtype)

    ins, specs = [a, b], [a_spec, b_spec]
    if res is not None:
        ins.append(res)
        specs.append(o_spec)
    return _call(body, name, (N // tn, M // tm), specs, o_spec, SDS((M, N), out_dtype), comm=comm)(*ins)


def _rowspec(tm, w, cb=0):
    return pl.BlockSpec((tm, w), lambda i: (i, cb))


def _bspec(w):
    return pl.BlockSpec((1, w), lambda i: (0, 0))


def rms_fwd(x, g, name):
    S, D = x.shape
    tm = _tile(S, 512)

    def body(x_ref, g_ref, h_ref):
        h_ref[...] = _rms(x_ref[...], g_ref[...]).astype(bf16)

    return _call(body, name, (S // tm,), [_rowspec(tm, D), _bspec(D)], _rowspec(tm, D), SDS((S, D), bf16))(x, g)


def rms_bwd(x, g, dh, dres, name):
    S, D = x.shape
    tm = _tile(S, 256)

    def body(x_ref, g_ref, dh_ref, dres_ref, dx_ref, dg_ref):
        _, vjp = jax.vjp(_rms, x_ref[...], g_ref[...])
        dx, dg = vjp(dh_ref[...])
        dx_ref[...] = dres_ref[...] + dx

        @pl.when(pl.program_id(0) == 0)
        def _():
            dg_ref[...] = jnp.zeros_like(dg_ref)

        dg_ref[...] += dg

    return _call(body, name, (S // tm,), [_rowspec(tm, D), _bspec(D), _rowspec(tm, D), _rowspec(tm, D)],
                 (_rowspec(tm, D), _bspec(D)), (SDS((S, D), f32), SDS((1, D), f32)))(x, g, dh, dres)


def _swiglu(g, u):
    return g * jax.nn.sigmoid(g) * u


def ffn_gu_act(h, wgu, name, tm=1024, tn=768, comm=None):
    S, D = h.shape
    W = wgu.shape[1] // 2
    tm, tn = _tile(S, tm, LANES), _tile(W, tn, LANES)
    nb = W // tn

    def body(h_ref, wg_ref, wu_ref, gu_ref, a_ref):
        hb = h_ref[...]
        g = _dot(hb, wg_ref[...])
        u = _dot(hb, wu_ref[...])
        gu_ref[0] = g.astype(bf16)
        gu_ref[1] = u.astype(bf16)
        a_ref[...] = _swiglu(g, u).astype(bf16)

    return _call(body, name, (nb, S // tm),
                 [pl.BlockSpec((tm, D), lambda j, i: (i, 0)), pl.BlockSpec((D, tn), lambda j, i: (0, j)),
                  pl.BlockSpec((D, tn), lambda j, i: (0, nb + j))],
                 (pl.BlockSpec((2, tm, tn), lambda j, i: (0, i, j)), pl.BlockSpec((tm, tn), lambda j, i: (i, j))),
                 (SDS((2, S, W), bf16), SDS((S, W), bf16)), comm=comm)(h, wgu, wgu)


def ffn_bda_act(dy, wd, gu, scale, name, tm=512, tn=768):
    S, D = dy.shape
    W = wd.shape[0]
    tm, tn = _tile(S, tm, LANES), _tile(W, tn, LANES)

    def body(dy_ref, wd_ref, gu_ref, dgu_ref):
        da = _dot(dy_ref[...].astype(bf16), wd_ref[...], NT) * scale
        _, vjp = jax.vjp(_swiglu, gu_ref[0].astype(f32), gu_ref[1].astype(f32))
        dg, du = vjp(da)
        dgu_ref[0] = dg.astype(bf16)
        dgu_ref[1] = du.astype(bf16)

    planes = pl.BlockSpec((2, tm, tn), lambda j, i: (0, i, j))
    return _call(body, name, (W // tn, S // tm),
                 [pl.BlockSpec((tm, D), lambda j, i: (i, 0)), pl.BlockSpec((tn, D), lambda j, i: (j, 0)), planes],
                 planes, SDS((2, S, W), bf16))(dy, wd, gu)


def ffn_bwgu(h, dgu, name, tm=512, tn=768):
    S, D = h.shape
    W = dgu.shape[2]
    tm, tn = _tile(D, tm, LANES), _tile(W, tn, LANES)
    nb = W // tn

    def body(h_ref, d_ref, o_ref):
        o_ref[...] = _dot(h_ref[...], d_ref[...], TN).astype(bf16)

    return _call(body, name, (2 * nb, D // tm),
                 [pl.BlockSpec((S, tm), lambda j, i: (0, i)), pl.BlockSpec((None, S, tn), lambda j, i: (j // nb, 0, j % nb))],
                 pl.BlockSpec((tm, tn), lambda j, i: (i, j)), SDS((D, 2 * W), bf16))(h, dgu)


def ffn_bdh(dgu, wgu, name, tm=512, tn=512):
    _, S, W = dgu.shape
    D = wgu.shape[0]
    tm, tn = _tile(S, tm, LANES), _tile(D, tn, LANES)

    def body(dg_ref, du_ref, wg_ref, wu_ref, o_ref):
        o_ref[...] = _dot(dg_ref[...], wg_ref[...], NT) + _dot(du_ref[...], wu_ref[...], NT)

    plane = lambda p: pl.BlockSpec((None, tm, W), lambda j, i: (p, i, 0))
    wcols = lambda p: pl.BlockSpec((tn, W), lambda j, i: (j, p))
    return _call(body, name, (D // tn, S // tm), [plane(0), plane(1), wcols(0), wcols(1)],
                 pl.BlockSpec((tm, tn), lambda j, i: (i, j)), SDS((S, D), f32))(dgu, dgu, wgu, wgu)


def loss_head(y, t, name):
    S, D = y.shape
    tm = _tile(S, 512)

    def body(y_ref, t_ref, dy_ref, l_ref):
        e = y_ref[...] - t_ref[...]
        dy_ref[...] = e * (1.0 / D)

        @pl.when(pl.program_id(0) == 0)
        def _():
            l_ref[...] = jnp.zeros_like(l_ref)

        l_ref[...] += jnp.sum(jnp.sum(e * e, axis=1, keepdims=True), axis=0, keepdims=True) * (0.5 / D)

    return _call(body, name, (S // tm,), [_rowspec(tm, D), _rowspec(tm, D)], (_rowspec(tm, D), _bspec(LANES)),
                 (SDS((S, D), f32), SDS((1, LANES), f32)))(y, t)


def ffn_fwd(x, g, wgu, wd, tag, comm_gu=None, comm_down=None):
    h = rms_fwd(x, g, f"{tag}_rms")
    (gu, a), got_gu = _carried(ffn_gu_act(h, wgu, f"{tag}_gu", comm=comm_gu), comm_gu)
    xo, got_down = _carried(mm(a, wd, "nn", f32, f"{tag}_down", tm=512, tn=512, res=x, scale=0.5, comm=comm_down), comm_down)
    return xo, (h, gu, a), got_gu, got_down


def ffn_bwd(x, g, wgu, wd, saved, dy, tag):
    h, gu, a = saved
    dgu = ffn_bda_act(dy, wd, gu, 0.5, f"{tag}_bda")
    dwd = mm(a, dy, "tn", bf16, f"{tag}_bwd", tm=512, tn=512, scale=0.5)
    dwgu = ffn_bwgu(h, dgu, f"{tag}_bwgu")
    dh = ffn_bdh(dgu, wgu, f"{tag}_bdh")
    dx, dg = rms_bwd(x, g, dh, dy, f"{tag}_brms")
    return dx, dg, dwgu, dwd


def _split_bf16(x):
    hi = x.astype(bf16)
    lo = (x - hi.astype(f32)).astype(bf16)
    return hi, lo


SB_TQ, SB_TK, SB_NSUB = 512, 256, 4


def _sb_geometry(S):
    tq = min(SB_TQ, S)
    tk = min(SB_TK, tq)
    return tq, tk, tq // SB_NSUB, tq // tk


def _sb_consts(tk):
    r = lax.broadcasted_iota(jnp.int32, (tk, tk), 0)
    c = lax.broadcasted_iota(jnp.int32, (tk, tk), 1)
    return (r > c).astype(bf16), (r < c).astype(bf16)


def _sb_scores(qs, k, kb, tk, rows, masked):
    scale = HEAD_DIM ** -0.5
    z = [_dot(q, k, NT) * scale for q in qs]
    ls = [_logsig(z_) for z_ in z]
    lm = [l - z_ for l, z_ in zip(ls, z)]
    ok = None
    if masked:
        col = kb * tk + lax.broadcasted_iota(jnp.int32, z[0].shape, 1)
        ok = [col < row for row in rows]
        lm = [jnp.where(m, x, 0.0) for m, x in zip(ok, lm)]
    return ls, lm, ok


def _sb_weights(ls, lm, ok, tail, after):
    parts = [_split_bf16(x) for x in lm]
    suf = [_dot(hi, after) + _dot(lo, after) for hi, lo in parts]
    a = [jnp.exp(l + s_ + t_) for l, s_, t_ in zip(ls, suf, tail)]
    if ok is not None:
        a = [jnp.where(m, x, 0.0) for m, x in zip(ok, a)]
    return a


def sb_fwd(qkv, nh, name, comm=None):
    S = qkv.shape[0]
    tq, tk, rs, nd = _sb_geometry(S)
    nsub = tq // rs

    def body(q_ref, k_ref, v_ref, o_ref, tails_ref):
        i = pl.program_id(1)
        after, _ = _sb_consts(tk)
        qs = [q_ref[pl.ds(s * rs, rs), :] for s in range(nsub)]
        rows = [i * tq + s * rs + lax.broadcasted_iota(jnp.int32, (rs, tk), 0) for s in range(nsub)]

        def tile(kb, carry, masked):
            tail, acc = carry
            off = pl.multiple_of(kb * tk, tk)
            k = k_ref[pl.ds(off, tk), :]
            v = v_ref[pl.ds(off, tk), :]
            ls, lm, ok = _sb_scores(qs, k, kb, tk, rows, masked)
            a = _sb_weights(ls, lm, ok, tail, after)
            tails_ref[pl.ds(kb, 1), :] += jnp.concatenate([t_.T for t_ in tail], axis=1)
            acc = [ac + _dot(a_.astype(bf16), v) for ac, a_ in zip(acc, a)]
            tail = [t_ + jnp.sum(x, axis=1, keepdims=True) for t_, x in zip(tail, lm)]
            return tail, acc

        tails_ref[...] = jnp.zeros_like(tails_ref)
        carry = ([jnp.zeros((rs, 1), f32)] * nsub, [jnp.zeros((rs, HEAD_DIM), f32)] * nsub)
        for d in reversed(range(nd)):
            carry = tile(i * nd + d, carry, True)
        carry = lax.fori_loop(0, i * nd, lambda t, c: tile(i * nd - 1 - t, c, False), carry)
        for s in range(nsub):
            o_ref[pl.ds(s * rs, rs), :] = carry[1][s].astype(o_ref.dtype)

    blk = lambda cb0: pl.BlockSpec((tq, HEAD_DIM), lambda h, i: (i, cb0 + h))
    full = lambda cb0: pl.BlockSpec((S, HEAD_DIM), lambda h, i: (0, cb0 + h))
    tails = pl.BlockSpec((S // tk, tq), lambda h, i: (h, i))
    return _call(body, name, (nh, S // tq), [blk(0), full(nh), full(2 * nh)], (blk(0), tails),
                 (SDS((S, nh * HEAD_DIM), bf16), SDS((nh * (S // tk), S), f32)), comm=comm)(qkv, qkv, qkv)


def sb_bwd(qkv, tails, do, nh, name, comm=None):
    S = qkv.shape[0]
    tq, tk, rs, nd = _sb_geometry(S)
    nsub = tq // rs
    scale = HEAD_DIM ** -0.5

    def body(q_ref, k_ref, v_ref, tails_ref, do_ref, dq_ref, dk_ref, dv_ref):
        i = pl.program_id(1)

        @pl.when(i == 0)
        def _():
            dk_ref[...] = jnp.zeros_like(dk_ref)
            dv_ref[...] = jnp.zeros_like(dv_ref)

        after, before = _sb_consts(tk)
        qs = [q_ref[pl.ds(s * rs, rs), :] for s in range(nsub)]
        dos = [do_ref[pl.ds(s * rs, rs), :].astype(bf16) for s in range(nsub)]
        rows = [i * tq + s * rs + lax.broadcasted_iota(jnp.int32, (rs, tk), 0) for s in range(nsub)]

        def sweep2(kb, carry, masked):
            pe, dq = carry
            off = pl.multiple_of(kb * tk, tk)
            k = k_ref[pl.ds(off, tk), :]
            v = v_ref[pl.ds(off, tk), :]
            ls, lm, ok = _sb_scores(qs, k, kb, tk, rows, masked)
            tail_row = tails_ref[pl.ds(kb, 1), :]
            tail = [tail_row[:, s * rs:(s + 1) * rs].T for s in range(nsub)]
            a = _sb_weights(ls, lm, ok, tail, after)
            e = [_dot(d_, v, NT) * a_ for d_, a_ in zip(dos, a)]
            parts = [_split_bf16(x) for x in e]
            cum_e = [p_ + (_dot(hi, before) + _dot(lo, before)) for p_, (hi, lo) in zip(pe, parts)]
            sig = [jnp.exp(l) for l in ls]
            dz = [e_ * (1.0 - sg) - c_ * sg for e_, sg, c_ in zip(e, sig, cum_e)]
            if ok is not None:
                dz = [jnp.where(m, x, 0.0) for m, x in zip(ok, dz)]
            dzb = [(x * scale).astype(bf16) for x in dz]
            dk_ref[pl.ds(off, tk), :] += sum(_dot(x, q, TN) for x, q in zip(dzb, qs))
            dv_ref[pl.ds(off, tk), :] += sum(_dot(a_.astype(bf16), d_, TN) for a_, d_ in zip(a, dos))
            pe = [p_ + jnp.sum(e_, axis=1, keepdims=True) for p_, e_ in zip(pe, e)]
            dq = [g + _dot(x, k) for g, x in zip(dq, dzb)]
            return pe, dq

        carry = ([jnp.zeros((rs, 1), f32)] * nsub, [jnp.zeros((rs, HEAD_DIM), f32)] * nsub)
        carry = lax.fori_loop(0, i * nd, lambda kb, c: sweep2(kb, c, False), carry)
        for d in range(nd):
            carry = sweep2(i * nd + d, carry, True)
        for s in range(nsub):
            dq_ref[pl.ds(s * rs, rs), :] = carry[1][s]

    blk = lambda cb0: pl.BlockSpec((tq, HEAD_DIM), lambda h, i: (i, cb0 + h))
    full = lambda cb0: pl.BlockSpec((S, HEAD_DIM), lambda h, i: (0, cb0 + h))
    sh = SDS((S, nh * HEAD_DIM), f32)
    tails_spec = pl.BlockSpec((S // tk, tq), lambda h, i: (h, i))
    return _call(body, name, (nh, S // tq), [blk(0), full(nh), full(2 * nh), tails_spec, blk(0)], (blk(0), full(0), full(0)),
                 (sh, sh, sh), comm=comm)(qkv, qkv, qkv, tails, do)


def fox_fwd(fq, fk, fv, c, cT, nh, lane0, name, comm=None):
    S = fq.shape[0]
    tq, tk, rs, nd = _sb_geometry(S)
    nsub = tq // rs
    scale = HEAD_DIM ** -0.5

    def body(q_ref, k_ref, v_ref, c_ref, ct_ref, o_ref, lse_ref):
        h = pl.program_id(0)
        i = pl.program_id(1)
        subs = range(nsub)
        qs = [q_ref[pl.ds(s * rs, rs), :] for s in subs]
        ci = [_lane_pick(c_ref[pl.ds(s * rs, rs), :], lane0 + h) for s in subs]
        rows = [i * tq + s * rs + lax.broadcasted_iota(jnp.int32, (rs, tk), 0) for s in subs]

        def tile(kb, carry, masked):
            m, l, acc = carry
            off = pl.multiple_of(kb * tk, tk)
            k = k_ref[pl.ds(off, tk), :]
            v = v_ref[pl.ds(off, tk), :]
            cj = ct_ref[pl.ds(lane0 % 8 + h, 1), pl.ds(off, tk)]
            sc = [_dot(q, k, NT) * scale + (c_ - cj) for q, c_ in zip(qs, ci)]
            if masked:
                col = kb * tk + lax.broadcasted_iota(jnp.int32, (rs, tk), 1)
                sc = [jnp.where(col <= row, x, -jnp.inf) for x, row in zip(sc, rows)]
            m2 = [jnp.maximum(m_, jnp.max(x, axis=1, keepdims=True)) for m_, x in zip(m, sc)]
            p = [jnp.exp(x - m_) for x, m_ in zip(sc, m2)]
            al = [jnp.exp(a - b) for a, b in zip(m, m2)]
            l = [a * l_ + jnp.sum(p_, axis=1, keepdims=True) for a, l_, p_ in zip(al, l, p)]
            acc = [a * ac + _dot(p_.astype(bf16), v) for a, ac, p_ in zip(al, acc, p)]
            return m2, l, acc

        carry = ([jnp.full((rs, 1), -jnp.inf, f32)] * nsub, [jnp.zeros((rs, 1), f32)] * nsub,
                 [jnp.zeros((rs, HEAD_DIM), f32)] * nsub)
        for d in range(nd):
            carry = tile(i * nd + d, carry, True)
        m, l, acc = lax.fori_loop(0, i * nd, lambda kb, cr: tile(kb, cr, False), carry)
        for s in subs:
            o_ref[pl.ds(s * rs, rs), :] = acc[s] / l[s]
            lse_ref[pl.ds(s * rs, rs), :] = jnp.broadcast_to(m[s] + jnp.log(l[s]), (rs, HEAD_DIM))

    blk = pl.BlockSpec((tq, HEAD_DIM), lambda h, i: (i, h))
    full = pl.BlockSpec((S, HEAD_DIM), lambda h, i: (0, h))
    cspec = pl.BlockSpec((tq, LANES), lambda h, i: (i, 0))
    ctspec = pl.BlockSpec((8, S), lambda h, i: (lane0 // 8, 0))
    sh = SDS((S, nh * HEAD_DIM), f32)
    return _call(body, name, (nh, S // tq), [blk, full, full, cspec, ctspec], (blk, blk), (sh, sh), comm=comm)(fq, fk, fv, c, cT)


def fox_bwd(fq, fk, fv, c, cT, o, lse, do, nh, lane0, name, comm=None):
    S = fq.shape[0]
    tq, tk, rs, nd = _sb_geometry(S)
    nsub = tq // rs
    scale = HEAD_DIM ** -0.5

    def body(q_ref, k_ref, v_ref, c_ref, ct_ref, o_ref, lse_ref, do_ref, dq_ref, dk_ref, dv_ref, dct_ref):
        h = pl.program_id(0)
        i = pl.program_id(1)

        @pl.when(i == 0)
        def _():
            dk_ref[...] = jnp.zeros_like(dk_ref)
            dv_ref[...] = jnp.zeros_like(dv_ref)

        @pl.when((i == 0) & (h == 0))
        def _():
            dct_ref[...] = jnp.zeros_like(dct_ref)

        subs = range(nsub)
        qs = [q_ref[pl.ds(s * rs, rs), :] for s in subs]
        dof = [do_ref[pl.ds(s * rs, rs), :] for s in subs]
        dos = [x.astype(bf16) for x in dof]
        ci = [_lane_pick(c_ref[pl.ds(s * rs, rs), :], lane0 + h) for s in subs]
        lses = [lse_ref[pl.ds(s * rs, rs), :][:, :1] for s in subs]
        dl = [jnp.sum(x * o_ref[pl.ds(s * rs, rs), :], axis=1, keepdims=True) for s, x in zip(subs, dof)]
        rows = [i * tq + s * rs + lax.broadcasted_iota(jnp.int32, (rs, tk), 0) for s in subs]

        def tile(kb, carry, masked):
            dq, rsum = carry
            off = pl.multiple_of(kb * tk, tk)
            k = k_ref[pl.ds(off, tk), :]
            v = v_ref[pl.ds(off, tk), :]
            cj = ct_ref[pl.ds(lane0 % 8 + h, 1), pl.ds(off, tk)]
            p = [jnp.exp(_dot(q, k, NT) * scale + (c_ - cj) - ls) for q, c_, ls in zip(qs, ci, lses)]
            if masked:
                col = kb * tk + lax.broadcasted_iota(jnp.int32, (rs, tk), 1)
                p = [jnp.where(col <= row, x, 0.0) for x, row in zip(p, rows)]
            ds = [p_ * (_dot(d_, v, NT) - dl_) for p_, d_, dl_ in zip(p, dos, dl)]
            dsb = [(x * scale).astype(bf16) for x in ds]
            dk_ref[pl.ds(off, tk), :] += sum(_dot(x, q, TN) for x, q in zip(dsb, qs))
            dv_ref[pl.ds(off, tk), :] += sum(_dot(p_.astype(bf16), d_, TN) for p_, d_ in zip(p, dos))
            dct_ref[pl.ds(lane0 + h, 1), pl.ds(off, tk)] -= sum(jnp.sum(x, axis=0, keepdims=True) for x in ds)
            return ([g + _dot(x, k) for g, x in zip(dq, dsb)],
                    [r_ + jnp.sum(x, axis=1, keepdims=True) for r_, x in zip(rsum, ds)])

        carry = ([jnp.zeros((rs, HEAD_DIM), f32)] * nsub, [jnp.zeros((rs, 1), f32)] * nsub)
        carry = lax.fori_loop(0, i * nd, lambda kb, cr: tile(kb, cr, False), carry)
        for d in range(nd):
            carry = tile(i * nd + d, carry, True)
        dq, rsum = carry
        for s in subs:
            dq_ref[pl.ds(s * rs, rs), :] = dq[s]
        dct_ref[pl.ds(lane0 + h, 1), pl.ds(pl.multiple_of(i * tq, tq), tq)] += jnp.concatenate([r_.T for r_ in rsum], axis=1)

    blk = pl.BlockSpec((tq, HEAD_DIM), lambda h, i: (i, h))
    full = pl.BlockSpec((S, HEAD_DIM), lambda h, i: (0, h))
    cspec = pl.BlockSpec((tq, LANES), lambda h, i: (i, 0))
    ctspec = pl.BlockSpec((8, S), lambda h, i: (lane0 // 8, 0))
    dctspec = pl.BlockSpec((LANES, S), lambda h, i: (0, 0))
    sh = SDS((S, nh * HEAD_DIM), f32)
    return _call(body, name, (nh, S // tq), [blk, full, full, cspec, ctspec, blk, blk, blk], (blk, full, full, dctspec),
                 (sh, sh, sh, SDS((LANES, S), f32)), comm=comm)(fq, fk, fv, c, cT, o, lse, do)


def gates_fwd(proj, small_cb, fbias_row, name, tm=256):
    S = proj.shape[0]
    tm = _tile(S, tm)

    def body(sm_ref, fb_ref, c_ref, ct_ref, carry_ref):
        @pl.when(pl.program_id(0) == 0)
        def _():
            carry_ref[...] = jnp.zeros_like(carry_ref)

        lf = _logsig(sm_ref[...] + fb_ref[...])
        r = lax.broadcasted_iota(jnp.int32, (tm, tm), 0)
        cc = lax.broadcasted_iota(jnp.int32, (tm, tm), 1)
        c = _dot((r >= cc).astype(f32), lf, NN, HI) + carry_ref[...]
        carry_ref[...] += jnp.sum(lf, axis=0, keepdims=True)
        c_ref[...] = c
        ct_ref[...] = c.T

    return _call(body, name, (S // tm,), [_rowspec(tm, LANES, small_cb), _bspec(LANES)],
                 (_rowspec(tm, LANES), pl.BlockSpec((LANES, tm), lambda i: (0, i))),
                 (SDS((S, LANES), f32), SDS((LANES, S), f32)), scratch=[pltpu.VMEM((1, LANES), f32)])(proj, fbias_row)


def gates_bwd(proj, small_cb, fbias_row, dcT, dsm_dn, lane0, nh, name, tm=256):
    S = proj.shape[0]
    tm = _tile(S, tm)
    nt = S // tm

    def body(sm_ref, fb_ref, dct_ref, dsm_ref, o_ref, dfb_ref, carry_ref):
        @pl.when(pl.program_id(0) == 0)
        def _():
            carry_ref[...] = jnp.zeros_like(carry_ref)
            dfb_ref[...] = jnp.zeros_like(dfb_ref)

        dc = dct_ref[...].T
        r = lax.broadcasted_iota(jnp.int32, (tm, tm), 0)
        cc = lax.broadcasted_iota(jnp.int32, (tm, tm), 1)
        dlf = _dot((r <= cc).astype(f32), dc, NN, HI) + carry_ref[...]
        carry_ref[...] += jnp.sum(dc, axis=0, keepdims=True)
        lane = lax.broadcasted_iota(jnp.int32, (1, LANES), 1)
        dpre = jnp.where((lane >= lane0) & (lane < lane0 + nh), dlf * jax.nn.sigmoid(-(sm_ref[...] + fb_ref[...])), 0.0)
        o_ref[...] = dsm_ref[...] + dpre
        dfb_ref[...] += jnp.sum(dpre, axis=0, keepdims=True)

    rev = lambda i: nt - 1 - i
    return _call(body, name, (nt,),
                 [pl.BlockSpec((tm, LANES), lambda i: (rev(i), small_cb)), _bspec(LANES),
                  pl.BlockSpec((LANES, tm), lambda i: (0, rev(i))), pl.BlockSpec((tm, LANES), lambda i: (rev(i), 0))],
                 (pl.BlockSpec((tm, LANES), lambda i: (rev(i), 0)), _bspec(LANES)),
                 (SDS((S, LANES), f32), SDS((1, LANES), f32)), scratch=[pltpu.VMEM((1, LANES), f32)])(proj, fbias_row, dcT, dsm_dn)


PAD_ROWS = 8


def conv_fwd(proj, w, width, name):
    S = proj.shape[0]
    cw = 256 if width % 256 == 0 else 128

    def body(x_ref, w_ref, y_ref, pad_ref):
        pad_ref[pl.ds(0, PAD_ROWS), :] = jnp.zeros((PAD_ROWS, cw), f32)
        pad_ref[pl.ds(PAD_ROWS, S), :] = x_ref[...]
        y = jnp.zeros((S, cw), f32)
        for i in range(CONV_WIDTH):
            y = y + pad_ref[pl.ds(PAD_ROWS - (CONV_WIDTH - 1) + i, S), :] * w_ref[pl.ds(i, 1), :]
        y_ref[...] = y * jax.nn.sigmoid(y)

    spec = pl.BlockSpec((S, cw), lambda j: (0, j))
    return _call(body, name, (width // cw,), [spec, pl.BlockSpec((CONV_WIDTH, cw), lambda j: (0, j))], spec,
                 SDS((S, width), f32), scratch=[pltpu.VMEM((S + PAD_ROWS, cw), f32)])(proj, w)


def conv_bwd(proj, w, dy, name):
    S, width = dy.shape
    cw = 256 if width % 256 == 0 else 128

    def body(x_ref, w_ref, dy_ref, dx_ref, dw_ref, xpad_ref, dpad_ref):
        xpad_ref[pl.ds(0, PAD_ROWS), :] = jnp.zeros((PAD_ROWS, cw), f32)
        xpad_ref[pl.ds(PAD_ROWS, S), :] = x_ref[...]
        y = jnp.zeros((S, cw), f32)
        for i in range(CONV_WIDTH):
            y = y + xpad_ref[pl.ds(PAD_ROWS - (CONV_WIDTH - 1) + i, S), :] * w_ref[pl.ds(i, 1), :]
        sg = jax.nn.sigmoid(y)
        dpre = dy_ref[...] * (sg * (1.0 + y * (1.0 - sg)))
        dpad_ref[pl.ds(S, PAD_ROWS), :] = jnp.zeros((PAD_ROWS, cw), f32)
        dpad_ref[pl.ds(0, S), :] = dpre
        dx = jnp.zeros((S, cw), f32)
        for i in range(CONV_WIDTH):
            dx = dx + dpad_ref[pl.ds(CONV_WIDTH - 1 - i, S), :] * w_ref[pl.ds(i, 1), :]
            dw_ref[pl.ds(i, 1), :] = jnp.sum(dpre * xpad_ref[pl.ds(PAD_ROWS - (CONV_WIDTH - 1) + i, S), :], axis=0, keepdims=True)
        dx_ref[...] = dx

    spec = pl.BlockSpec((S, cw), lambda j: (0, j))
    wspec = pl.BlockSpec((CONV_WIDTH, cw), lambda j: (0, j))
    return _call(body, name, (width // cw,), [spec, wspec, spec], (spec, wspec),
                 (SDS((S, width), f32), SDS((CONV_WIDTH, width), f32)),
                 scratch=[pltpu.VMEM((S + PAD_ROWS, cw), f32), pltpu.VMEM((S + PAD_ROWS, cw), f32)])(proj, w, dy)


def _pdot_raw(a, b, dims, passes):
    if passes == 1:
        return _dot(a.astype(bf16), b.astype(bf16), dims)
    ah, al = _split_bf16(a)
    bh, bl = _split_bf16(b)
    return _dot(ah, bh, dims) + (_dot(ah, bl, dims) + _dot(al, bh, dims))


def _make_pdot(passes):
    @functools.partial(jax.custom_vjp, nondiff_argnums=(2,))
    def pdot(a, b, mode):
        return _pdot_raw(a, b, {"nn": NN, "nt": NT, "tn": TN}[mode], passes)

    def fwd(a, b, mode):
        return pdot(a, b, mode), (a, b)

    def bwd(mode, res, g):
        a, b = res
        if mode == "nn":
            return _pdot_raw(g, b, NT, passes), _pdot_raw(a, g, TN, passes)
        if mode == "nt":
            return _pdot_raw(g, b, NN, passes), _pdot_raw(g, a, TN, passes)
        return _pdot_raw(b, g, NT, passes), _pdot_raw(a, g, NN, passes)

    pdot.defvjp(fwd, bwd)
    return pdot


_dot1 = _make_pdot(1)
_dot3 = _make_pdot(3)


def _each(f, *lists):
    return [f(*xs) for xs in zip(*lists)]


def _dn_chunk(ndn, cq, ck, cv, small, alog, dtb, s0):
    C = cq[0].shape[0]
    hs = list(range(ndn))
    b = [_lane_pick(small, h) for h in hs]
    d = [_lane_pick(small, ndn + h) for h in hs]
    al = [_lane_pick(alog, h) for h in hs]
    dt = [_lane_pick(dtb, h) for h in hs]
    q = _each(lambda x: x * lax.rsqrt(jnp.sum(x * x, axis=1, keepdims=True) + EPS) * (HEAD_DIM ** -0.5), cq)
    k = _each(lambda x: x * lax.rsqrt(jnp.sum(x * x, axis=1, keepdims=True) + EPS), ck)
    beta = _each(jax.nn.sigmoid, b)
    g = _each(lambda al_, d_, dt_: -jnp.exp(al_) * _softplus(d_ + dt_), al, d, dt)
    r = lax.broadcasted_iota(jnp.int32, (C, C), 0)
    c = lax.broadcasted_iota(jnp.int32, (C, C), 1)
    incl, strict = r >= c, r > c
    inclf = incl.astype(f32)
    gc = _each(lambda g_: _dot3(inclf, g_, "nn"), g)
    decay = _each(lambda gc_: jnp.where(incl, jnp.exp(jnp.where(incl, gc_ - gc_.T, 0.0)), 0.0), gc)
    kb = _each(lambda k_, b_: k_ * b_, k, beta)
    a = _each(lambda kb_, k_, dec: jnp.where(strict, _dot3(kb_, k_, "nt") * dec, 0.0), kb, k, decay)
    eye = (r == c).astype(f32)
    t = _each(lambda a_: eye - a_, a)
    p = a
    for _ in range(int(math.log2(C)) - 1):
        p = _each(lambda p_: _dot3(p_, p_, "nn"), p)
        t = _each(lambda t_, p_: t_ + _dot3(t_, p_, "nn"), t, p)
    eg = _each(jnp.exp, gc)
    u = _each(lambda t_, v_, b_: _dot3(t_, v_ * b_, "nn"), t, cv, beta)
    w = _each(lambda t_, kb_, eg_: _dot3(t_, kb_ * eg_, "nn"), t, kb, eg)
    attn = _each(lambda q_, k_, dec: _dot1(q_, k_, "nt") * dec, q, k, decay)
    g_last = _each(lambda g_: jnp.sum(g_, axis=0, keepdims=True), g)
    v_new = _each(lambda u_, w_, s_: u_ - _dot1(w_, s_, "nn"), u, w, s0)
    o = _each(lambda q_, eg_, s_, at, vn: _dot1(q_ * eg_, s_, "nn") + _dot1(at, vn, "nn"), q, eg, s0, attn, v_new)
    s1 = _each(lambda s_, gl, k_, gc_, vn: s_ * jnp.exp(gl) + _dot1(k_ * jnp.exp(gl - gc_), vn, "tn"), s0, g_last, k, gc, v_new)
    return o, s1


def dn_fwd(cqkv, proj, small_cb, alog_row, dt_row, ndn, name, comm=None):
    S = cqkv.shape[0]
    C = DN_CHUNK
    nc = S // C
    half = ndn * HEAD_DIM

    def body(q_ref, k_ref, v_ref, sm_ref, al_ref, dt_ref, o_ref, ss_ref, s_ref):
        @pl.when(pl.program_id(0) == 0)
        def _():
            s_ref[...] = jnp.zeros_like(s_ref)

        sls = [slice(h * HEAD_DIM, (h + 1) * HEAD_DIM) for h in range(ndn)]
        s0 = [s_ref[h] for h in range(ndn)]
        for h in range(ndn):
            ss_ref[h, 0] = s0[h]
        o, s1 = _dn_chunk(ndn, [q_ref[:, sl] for sl in sls], [k_ref[:, sl] for sl in sls], [v_ref[:, sl] for sl in sls],
                          sm_ref[...], al_ref[...], dt_ref[...], s0)
        for h in range(ndn):
            o_ref[:, sls[h]] = o[h]
            s_ref[h] = s1[h]

    blk = lambda cb: pl.BlockSpec((C, half), lambda n: (n, cb))
    row = pl.BlockSpec((1, LANES), lambda n: (0, 0))
    return _call(body, name, (nc,),
                 [blk(0), blk(1), blk(2), pl.BlockSpec((C, LANES), lambda n: (n, small_cb)), row, row],
                 (blk(0), pl.BlockSpec((ndn, 1, HEAD_DIM, HEAD_DIM), lambda n: (0, n, 0, 0))),
                 (SDS((S, half), f32), SDS((ndn, nc, HEAD_DIM, HEAD_DIM), f32)),
                 scratch=[pltpu.VMEM((ndn, HEAD_DIM, HEAD_DIM), f32)], comm=comm)(cqkv, cqkv, cqkv, proj, alog_row, dt_row)


def dn_bwd(cqkv, proj, small_cb, alog_row, dt_row, ssave, do, ndn, name, comm=None):
    S = cqkv.shape[0]
    C = DN_CHUNK
    nc = S // C
    half = ndn * HEAD_DIM

    def body(q_ref, k_ref, v_ref, sm_ref, al_ref, dt_ref, ss_ref, do_ref, dqkv_ref, dsm_ref, dal_ref, ddt_ref, ds_ref):
        @pl.when(pl.program_id(0) == 0)
        def _():
            ds_ref[...] = jnp.zeros_like(ds_ref)
            dal_ref[...] = jnp.zeros_like(dal_ref)
            ddt_ref[...] = jnp.zeros_like(ddt_ref)

        sls = [slice(h * HEAD_DIM, (h + 1) * HEAD_DIM) for h in range(ndn)]
        _, vjp = jax.vjp(functools.partial(_dn_chunk, ndn), [q_ref[:, sl] for sl in sls], [k_ref[:, sl] for sl in sls],
                         [v_ref[:, sl] for sl in sls], sm_ref[...], al_ref[...], dt_ref[...], [ss_ref[h, 0] for h in range(ndn)])
        dq, dk, dv, dsm, dal, ddt, ds0 = vjp(([do_ref[:, sl] for sl in sls], [ds_ref[h] for h in range(ndn)]))
        for h in range(ndn):
            dqkv_ref[:, sls[h]] = dq[h]
            dqkv_ref[:, half + h * HEAD_DIM:half + (h + 1) * HEAD_DIM] = dk[h]
            dqkv_ref[:, 2 * half + h * HEAD_DIM:2 * half + (h + 1) * HEAD_DIM] = dv[h]
            ds_ref[h] = ds0[h]
        dsm_ref[...] = dsm
        dal_ref[...] += dal
        ddt_ref[...] += ddt

    rev = lambda n: nc - 1 - n
    blk = lambda cb: pl.BlockSpec((C, half), lambda n: (rev(n), cb))
    row = pl.BlockSpec((1, LANES), lambda n: (0, 0))
    return _call(body, name, (nc,),
                 [blk(0), blk(1), blk(2), pl.BlockSpec((C, LANES), lambda n: (rev(n), small_cb)), row, row,
                  pl.BlockSpec((ndn, 1, HEAD_DIM, HEAD_DIM), lambda n: (0, rev(n), 0, 0)), blk(0)],
                 (pl.BlockSpec((C, 3 * half), lambda n: (rev(n), 0)), pl.BlockSpec((C, LANES), lambda n: (rev(n), 0)), row, row),
                 (SDS((S, 3 * half), f32), SDS((S, LANES), f32), SDS((1, LANES), f32), SDS((1, LANES), f32)),
                 scratch=[pltpu.VMEM((ndn, HEAD_DIM, HEAD_DIM), f32)], comm=comm)(cqkv, cqkv, cqkv, proj, alog_row, dt_row, ssave, do)


def even_pre(proj, gq, gk, dfx, cb0, name):
    S = proj.shape[0]
    tm = _tile(S, 256)
    nh = dfx // HEAD_DIM

    def body(q_ref, k_ref, v_ref, gq_ref, gk_ref, fq_ref, fk_ref, fv_ref):
        for h in range(nh):
            sl = slice(h * HEAD_DIM, (h + 1) * HEAD_DIM)
            fq_ref[:, sl] = _rms(q_ref[:, sl], gq_ref[...]).astype(bf16)
            fk_ref[:, sl] = _rms(k_ref[:, sl], gk_ref[...]).astype(bf16)
        fv_ref[...] = v_ref[...].astype(bf16)

    sh = SDS((S, dfx), bf16)
    o = _rowspec(tm, dfx)
    return _call(body, name, (S // tm,),
                 [_rowspec(tm, dfx, cb0), _rowspec(tm, dfx, cb0 + 1), _rowspec(tm, dfx, cb0 + 2), _bspec(HEAD_DIM), _bspec(HEAD_DIM)],
                 (o, o, o), (sh, sh, sh))(proj, proj, proj, gq, gk)


def even_pre_bwd(proj, gq, gk, dfq, dfk, dfx, cb0, name):
    S = proj.shape[0]
    tm = _tile(S, 256)
    nh = dfx // HEAD_DIM

    def body(q_ref, k_ref, gq_ref, gk_ref, dfq_ref, dfk_ref, dq_ref, dk_ref, dgq_ref, dgk_ref):
        @pl.when(pl.program_id(0) == 0)
        def _():
            dgq_ref[...] = jnp.zeros_like(dgq_ref)
            dgk_ref[...] = jnp.zeros_like(dgk_ref)

        for h in range(nh):
            sl = slice(h * HEAD_DIM, (h + 1) * HEAD_DIM)
            _, vq = jax.vjp(_rms, q_ref[:, sl], gq_ref[...])
            dq, dgq = vq(dfq_ref[:, sl])
            _, vk = jax.vjp(_rms, k_ref[:, sl], gk_ref[...])
            dk, dgk = vk(dfk_ref[:, sl])
            dq_ref[:, sl] = dq
            dk_ref[:, sl] = dk
            dgq_ref[...] += dgq
            dgk_ref[...] += dgk

    sh = SDS((S, dfx), f32)
    o = _rowspec(tm, dfx)
    g = SDS((1, HEAD_DIM), f32)
    return _call(body, name, (S // tm,),
                 [_rowspec(tm, dfx, cb0), _rowspec(tm, dfx, cb0 + 1), _bspec(HEAD_DIM), _bspec(HEAD_DIM), o, o],
                 (o, o, _bspec(HEAD_DIM), _bspec(HEAD_DIM)), (sh, sh, g, g))(proj, proj, gq, gk, dfq, dfk)


def _dn_out(o, gate, gn):
    return _rms(o, gn) * (gate * jax.nn.sigmoid(gate))


def _fox_out(o, gate):
    return o * jax.nn.sigmoid(gate)


def even_post(o_dn, o_fox, proj, gn, half, cb_dn_gate, cb_fox_gate, name):
    S = proj.shape[0]
    tm = _tile(S, 256)
    nh = half // HEAD_DIM

    def body(od_ref, of_ref, gd_ref, gf_ref, gn_ref, o_ref):
        for h in range(nh):
            sl = slice(h * HEAD_DIM, (h + 1) * HEAD_DIM)
            o_ref[:, sl] = _dn_out(od_ref[:, sl], gd_ref[:, sl], gn_ref[...]).astype(bf16)
        o_ref[:, half:] = _fox_out(of_ref[...], gf_ref[...]).astype(bf16)

    return _call(body, name, (S // tm,),
                 [_rowspec(tm, half), _rowspec(tm, half), _rowspec(tm, half, cb_dn_gate), _rowspec(tm, half, cb_fox_gate), _bspec(HEAD_DIM)],
                 _rowspec(tm, 2 * half), SDS((S, 2 * half), bf16))(o_dn, o_fox, proj, proj, gn)


def even_post_bwd(o_dn, o_fox, proj, gn, dom, half, cb_dn_gate, cb_fox_gate, name):
    S = proj.shape[0]
    tm = _tile(S, 256)
    nh = half // HEAD_DIM

    def body(od_ref, of_ref, gd_ref, gf_ref, gn_ref, dod_ref, dof_ref, dd_ref, df_ref, dgd_ref, dgf_ref, dgn_ref):
        @pl.when(pl.program_id(0) == 0)
        def _():
            dgn_ref[...] = jnp.zeros_like(dgn_ref)

        for h in range(nh):
            sl = slice(h * HEAD_DIM, (h + 1) * HEAD_DIM)
            _, vjp = jax.vjp(_dn_out, od_ref[:, sl], gd_ref[:, sl], gn_ref[...])
            d_o, d_gate, d_gn = vjp(dod_ref[:, sl])
            dd_ref[:, sl] = d_o
            dgd_ref[:, sl] = d_gate
            dgn_ref[...] += d_gn
        _, vjp = jax.vjp(_fox_out, of_ref[...], gf_ref[...])
        d_o, d_gate = vjp(dof_ref[...])
        df_ref[...] = d_o
        dgf_ref[...] = d_gate

    sh = SDS((S, half), f32)
    o = _rowspec(tm, half)
    return _call(body, name, (S // tm,),
                 [o, o, _rowspec(tm, half, cb_dn_gate), _rowspec(tm, half, cb_fox_gate), _bspec(HEAD_DIM),
                  _rowspec(tm, half, 0), _rowspec(tm, half, 1)],
                 (o, o, o, o, _bspec(HEAD_DIM)), (sh, sh, sh, sh, SDS((1, HEAD_DIM), f32)))(o_dn, o_fox, proj, proj, gn, dom, dom)


def _pad_row(v, lane0=0):
    return jnp.pad(v, (lane0, LANES - lane0 - v.shape[0]))[None]


def _carried(res, comm):
    return res if comm is not None else (res, [])


def even_mixer_fwd(x, gmix, w_in, w_out, conv_w, a_log, dt_bias, gn, gq, gk, f_bias, tag, comm_fox=None, comm_dn=None,
                   comm_in=None):
    S, D = x.shape
    half = D // 2
    ndn = half // HEAD_DIM
    small_cb = 4 * D // LANES
    alog_row, dt_row, fb_row = _pad_row(a_log), _pad_row(dt_bias), _pad_row(f_bias, 2 * ndn)
    h = rms_fwd(x, gmix, f"{tag}_rms")
    proj, got_in = _carried(mm(h, w_in, "nn", f32, f"{tag}_in", tm=512, tn=1408, comm=comm_in), comm_in)
    cqkv = conv_fwd(proj, conv_w, 3 * half, f"{tag}_conv")
    (o_dn, ssave), got_dn = _carried(dn_fwd(cqkv, proj, small_cb, alog_row, dt_row, ndn, f"{tag}_dn", comm=comm_dn), comm_dn)
    c, cT = gates_fwd(proj, small_cb, fb_row, f"{tag}_gates")
    fq, fk, fv = even_pre(proj, gq, gk, half, 4, f"{tag}_pre")
    (o_fox, lse), got_fox = _carried(fox_fwd(fq, fk, fv, c, cT, ndn, 2 * ndn, f"{tag}_fox", comm=comm_fox), comm_fox)
    om = even_post(o_dn, o_fox, proj, gn, half, 3, 7, f"{tag}_post")
    xo = mm(om, w_out, "nn", f32, f"{tag}_out", res=x)
    return xo, (h, proj, cqkv, o_dn, ssave, c, cT, fq, fk, fv, o_fox, lse, om, alog_row, dt_row, fb_row), got_fox, got_dn, got_in


def even_mixer_bwd(x, gmix, w_in, w_out, conv_w, gn, gq, gk, saved, dy, tag, comm_fox=None, comm_dn=None):
    S, D = x.shape
    half = D // 2
    ndn = half // HEAD_DIM
    small_cb = 4 * D // LANES
    h, proj, cqkv, o_dn, ssave, c, cT, fq, fk, fv, o_fox, lse, om, alog_row, dt_row, fb_row = saved
    dom = mm(dy, w_out, "nt", f32, f"{tag}_bdom")
    dw_out = mm(om, dy, "tn", bf16, f"{tag}_bwout")
    d_odn, d_ofox, d_dgate, d_fgate, d_gn = even_post_bwd(o_dn, o_fox, proj, gn, dom, half, 3, 7, f"{tag}_bpost")
    (dfq, dfk, dfv, dcT), got_fox = _carried(
        fox_bwd(fq, fk, fv, c, cT, o_fox, lse, d_ofox, ndn, 2 * ndn, f"{tag}_bfox", comm=comm_fox), comm_fox)
    dq_pre, dk_pre, d_gq, d_gk = even_pre_bwd(proj, gq, gk, dfq, dfk, half, 4, f"{tag}_bpre")
    (dcqkv, dsm_dn, d_alog, d_dt), got_dn = _carried(
        dn_bwd(cqkv, proj, small_cb, alog_row, dt_row, ssave, d_odn, ndn, f"{tag}_bdn", comm=comm_dn), comm_dn)
    d_conv_in, d_conv_w = conv_bwd(proj, conv_w, dcqkv, f"{tag}_bconv")
    d_small, d_fb = gates_bwd(proj, small_cb, fb_row, dcT, dsm_dn, 2 * ndn, ndn, f"{tag}_bgates")
    dproj = jnp.concatenate([d_conv_in, d_dgate, dq_pre, dk_pre, dfv, d_fgate, d_small], axis=1)
    dw_in = mm(h, dproj, "tn", bf16, f"{tag}_bwin", tn=384)
    dh = mm(dproj, w_in, "nt", f32, f"{tag}_bdh")
    dx, d_gmix = rms_bwd(x, gmix, dh, dy, f"{tag}_brms")
    small = dict(norm_mix=d_gmix[0], dn_conv_w=d_conv_w, dn_a_log=d_alog[0, :ndn], dn_dt_bias=d_dt[0, :ndn],
                 dn_norm_g=d_gn[0], fox_q_norm_g=d_gq[0], fox_k_norm_g=d_gk[0], fox_f_bias=d_fb[0, 2 * ndn:3 * ndn])
    return dx, dw_in, dw_out, small, got_fox, got_dn


def odd_mixer_fwd(x, gmix, w_in, w_out, tag, comm=None):
    S, D = x.shape
    nh = D // HEAD_DIM
    h = rms_fwd(x, gmix, f"{tag}_rms")
    qkv = mm(h, w_in, "nn", bf16, f"{tag}_in", tn=768)
    (o, tails), got = _carried(sb_fwd(qkv, nh, f"{tag}_sb", comm=comm), comm)
    xo = mm(o, w_out, "nn", f32, f"{tag}_out", res=x)
    return xo, (h, qkv, o, tails), got


def odd_mixer_bwd(x, gmix, w_in, w_out, saved, dy, tag, comm=None):
    S, D = x.shape
    nh = D // HEAD_DIM
    h, qkv, o, tails = saved
    do = mm(dy, w_out, "nt", f32, f"{tag}_bdo")
    dw_out = mm(o, dy, "tn", bf16, f"{tag}_bwout")
    (dq, dk, dv), got = _carried(sb_bwd(qkv, tails, do, nh, f"{tag}_bsb", comm=comm), comm)
    dqkv = jnp.concatenate([dq, dk, dv], axis=1)
    dw_in = mm(h, dqkv, "tn", bf16, f"{tag}_bwin", tn=768)
    dh = mm(dqkv, w_in, "nt", f32, f"{tag}_bdh")
    dx, d_gmix = rms_bwd(x, gmix, dh, dy, f"{tag}_brms")
    return dx, dw_in, dw_out, dict(norm_mix=d_gmix[0]), got


def all_gather(shards, axes, name, kind="ag"):
    return _call(None, name, (), [], [], [], comm=Comm(kind, shards, axes))()[1]


def scatter_parts(fulls, axes, name):
    return _call(None, name, (), [], [], [], comm=Comm("rs", fulls, axes))()[1]


def sum_parts(parts, name):
    _, R, C = parts.shape
    tm = _tile(R, 256)

    def body(p_ref, o_ref):
        acc = p_ref[0].astype(f32)
        for k in range(1, NDEV):
            acc = acc + p_ref[k].astype(f32)
        o_ref[...] = acc

    return _call(body, name, (R // tm,), [pl.BlockSpec((NDEV, tm, C), lambda i: (0, i, 0))], _rowspec(tm, C), SDS((R, C), f32))(parts)


def adamw(w, g, m, v, name):
    R, C = w.shape
    tm = _tile(R, max(8, min(512, (ADAMW_TILE_ELEMS // C) // 8 * 8)))
    c1 = 1.0 - ADAM_B1 ** ADAM_STEP
    c2 = 1.0 - ADAM_B2 ** ADAM_STEP

    def body(w_ref, g_ref, m_ref, v_ref, d_ref, nm_ref, nv_ref):
        g_ = g_ref[...]
        nm = ADAM_B1 * m_ref[...] + (1.0 - ADAM_B1) * g_
        nv = ADAM_B2 * v_ref[...] + (1.0 - ADAM_B2) * (g_ * g_)
        d_ref[...] = -ADAM_LR * ((nm / c1) / (jnp.sqrt(nv / c2) + ADAM_EPS) + ADAM_WD * w_ref[...])
        nm_ref[...] = nm
        nv_ref[...] = nv

    spec = _rowspec(tm, C)
    sh = SDS((R, C), f32)
    return _call(body, name, (R // tm,), [spec] * 4, (spec, spec, spec), (sh, sh, sh))(w, g, m, v)


def _pad_to(n, mult):
    return -(-n // mult) * mult


def _even_perm(D):
    half = D // 2
    ndn = half // HEAD_DIM
    o_gate = 3 * half
    o_b = o_gate + half
    o_a = o_b + ndn
    o_fq = o_a + ndn
    o_fpre = o_fq + 4 * half
    return half, ndn, o_b, o_a, o_fq, o_fpre


def _even_to_mine(w):
    D = (w.shape[-1] // 4) // LANES * LANES
    half, ndn, o_b, o_a, o_fq, o_fpre = _even_perm(D)
    small = jnp.concatenate([w[..., o_b:o_b + ndn], w[..., o_a:o_a + ndn], w[..., o_fpre:o_fpre + ndn]], axis=-1)
    small = jnp.pad(small, [(0, 0)] * (w.ndim - 1) + [(0, LANES - 3 * ndn)])
    return jnp.concatenate([w[..., :o_b], w[..., o_fq:o_fpre], small], axis=-1)


def _even_from_mine(w, D):
    half, ndn, o_b, o_a, o_fq, o_fpre = _even_perm(D)
    sm = w[..., 4 * D:]
    return jnp.concatenate([w[..., :o_b], sm[..., :ndn], sm[..., ndn:2 * ndn], w[..., o_b:4 * D], sm[..., 2 * ndn:3 * ndn]], axis=-1)


def _pack_small(parts):
    flat = jnp.concatenate([p.reshape(-1).astype(f32) for p in parts])
    n = flat.shape[0]
    return jnp.pad(flat, (0, _pad_to(n, 8 * LANES) - n)).reshape(-1, LANES)


def _unpack_small(packed, like):
    flat = packed.reshape(-1)
    out, off = [], 0
    for p in like:
        out.append(flat[off:off + p.size].reshape(p.shape))
        off += p.size
    return out


SMALL_NAMES = ("norm_ffn1", "norm_mix", "dn_a_log", "dn_dt_bias", "dn_norm_g", "fox_q_norm_g", "fox_k_norm_g", "fox_f_bias", "norm_ffn2")
BIG_NAMES = ("ffn1_w_gu", "ffn1_w_down", "w_in_even", "dn_conv_w", "w_out_even", "w_in_odd", "w_out_odd", "ffn2_w_gu", "ffn2_w_down")
WEIGHT_NAMES = ("norm_ffn1", "ffn1_w_gu", "ffn1_w_down", "norm_mix", "w_in_even", "dn_conv_w", "dn_a_log", "dn_dt_bias", "dn_norm_g",
                "fox_q_norm_g", "fox_k_norm_g", "fox_f_bias", "w_out_even", "w_in_odd", "w_out_odd", "norm_ffn2", "ffn2_w_gu", "ffn2_w_down")


def kernel(x, norm_ffn1, ffn1_w_gu, ffn1_w_down, norm_mix, w_in_even, dn_conv_w, dn_a_log, dn_dt_bias, dn_norm_g, fox_q_norm_g, fox_k_norm_g, fox_f_bias, w_out_even, w_in_odd, w_out_odd, norm_ffn2, ffn2_w_gu, ffn2_w_down, loss_target, m_norm_ffn1, m_ffn1_w_gu, m_ffn1_w_down, m_norm_mix, m_w_in_even, m_dn_conv_w, m_dn_a_log, m_dn_dt_bias, m_dn_norm_g, m_fox_q_norm_g, m_fox_k_norm_g, m_fox_f_bias, m_w_out_even, m_w_in_odd, m_w_out_odd, m_norm_ffn2, m_ffn2_w_gu, m_ffn2_w_down, v_norm_ffn1, v_ffn1_w_gu, v_ffn1_w_down, v_norm_mix, v_w_in_even, v_dn_conv_w, v_dn_a_log, v_dn_dt_bias, v_dn_norm_g, v_fox_q_norm_g, v_fox_k_norm_g, v_fox_f_bias, v_w_out_even, v_w_in_odd, v_w_out_odd, v_norm_ffn2, v_ffn2_w_gu, v_ffn2_w_down):
    args = dict(locals())
    W = {n: args[n] for n in WEIGHT_NAMES}
    M = {n: args["m_" + n] for n in WEIGHT_NAMES}
    V = {n: args["v_" + n] for n in WEIGHT_NAMES}
    xs = x[0]
    tgt = loss_target[0]
    S, D = xs.shape
    L = norm_ffn1.shape[0]
    n_even = w_in_even.shape[0]
    hs = ffn1_w_down.shape[1]
    hp = _pad_to(hs, LANES)
    me = 4 * lax.axis_index("x") + 2 * lax.axis_index("y") + lax.axis_index("c")

    def prep_gu(w):
        w = w.reshape(L, D, 2, hs)
        return jnp.pad(w, ((0, 0), (0, 0), (0, 0), (0, hp - hs))).reshape(L, D, 2 * hp).astype(bf16)

    def prep_down(w):
        return jnp.pad(w, ((0, 0), (0, hp - hs), (0, 0))).astype(bf16)

    sh_gu1, sh_d1, sh_gu2, sh_d2 = prep_gu(ffn1_w_gu), prep_down(ffn1_w_down), prep_gu(ffn2_w_gu), prep_down(ffn2_w_down)
    sh_ie, sh_oe = _even_to_mine(w_in_even).astype(bf16), w_out_even.astype(bf16)
    sh_io, sh_oo = w_in_odd.astype(bf16), w_out_odd.astype(bf16)

    def layer_shards(l):
        j = l // 2
        mix = [sh_ie[j], sh_oe[j]] if l % 2 == 0 else [sh_io[j], sh_oo[j]]
        return [sh_gu1[l], sh_d1[l], *mix, sh_gu2[l], sh_d2[l]]

    def layer_axes(l):
        return [1, 0, 0 if l % 2 == 0 else 1, 0, 1, 0]

    def layer_names(l):
        return ["ffn1_w_gu", "ffn1_w_down", *(["w_in_even", "w_out_even"] if l % 2 == 0 else ["w_in_odd", "w_out_odd"]),
                "ffn2_w_gu", "ffn2_w_down"]

    FOX_CARRIES, DN_CARRIES = (0, 4, 3), (1, 2, 5)

    def split_comm(kind, arrays, axes):
        return (Comm(kind, [arrays[i] for i in FOX_CARRIES], [axes[i] for i in FOX_CARRIES]),
                Comm(kind, [arrays[i] for i in DN_CARRIES], [axes[i] for i in DN_CARRIES]))

    def join_split(got_fox, got_dn):
        out = [None] * 6
        for i, a in zip(FOX_CARRIES, got_fox):
            out[i] = a
        for i, a in zip(DN_CARRIES, got_dn):
            out[i] = a
        return out

    first = all_gather(layer_shards(0) + [dn_conv_w[None]], layer_axes(0) + [0], "ag_layer0", kind="ag2")
    weights = {0: first[:6]}
    convw = jnp.moveaxis(first[6], 0, 2).reshape(n_even, CONV_WIDTH, -1)

    EVEN_FWD_CARRIES = dict(f1_gu=(), mx_in=(), dn=(1, 2, 5), fox=(0, 4, 3), f2_gu=(), f2_down=())
    saved = []
    cur = xs
    for l in range(L):
        j = l // 2
        wgu1, wd1, win, wout, wgu2, wd2 = weights[l]
        nxt = l + 1 < L
        x0 = cur
        if l % 2 == 0:
            sh_n, ax_n = (layer_shards(l + 1), layer_axes(l + 1)) if nxt else (None, None)
            cm = {k: (Comm("ag2", [sh_n[i] for i in idx], [ax_n[i] for i in idx]) if nxt and idx else None)
                  for k, idx in EVEN_FWD_CARRIES.items()}
            x1, s1, got_f1gu, _ = ffn_fwd(x0, norm_ffn1[l:l + 1], wgu1, wd1, f"l{l}f1", comm_gu=cm["f1_gu"])
            x2, sm, got_f, got_d, got_in = even_mixer_fwd(
                x1, norm_mix[l:l + 1], win, wout, convw[j], dn_a_log[j], dn_dt_bias[j], dn_norm_g[j:j + 1],
                fox_q_norm_g[j:j + 1], fox_k_norm_g[j:j + 1], fox_f_bias[j], f"l{l}mx", comm_fox=cm["fox"], comm_dn=cm["dn"],
                comm_in=cm["mx_in"])
            x3, s2, got_f2gu, got_f2down = ffn_fwd(x2, norm_ffn2[l:l + 1], wgu2, wd2, f"l{l}f2", comm_gu=cm["f2_gu"],
                                                   comm_down=cm["f2_down"])
            if nxt:
                got = dict(f1_gu=got_f1gu, mx_in=got_in, dn=got_d, fox=got_f, f2_gu=got_f2gu, f2_down=got_f2down)
                weights[l + 1] = [None] * 6
                for k, idx in EVEN_FWD_CARRIES.items():
                    for i, a in zip(idx, got[k]):
                        weights[l + 1][i] = a
        else:
            x1, s1, _, _ = ffn_fwd(x0, norm_ffn1[l:l + 1], wgu1, wd1, f"l{l}f1")
            cm = Comm("ag2", layer_shards(l + 1), layer_axes(l + 1)) if nxt else None
            x2, sm, got = odd_mixer_fwd(x1, norm_mix[l:l + 1], win, wout, f"l{l}mx", comm=cm)
            if nxt:
                weights[l + 1] = got
            x3, s2, _, _ = ffn_fwd(x2, norm_ffn2[l:l + 1], wgu2, wd2, f"l{l}f2")
        saved.append((x0, x1, x2, s1, sm, s2))
        cur = x3
    dy, loss_row = loss_head(cur, tgt, "loss")

    gsmall = {n: [None] * W[n].shape[0] for n in SMALL_NAMES}
    gconv = [None] * n_even
    parts = {n: [None] * W[n].shape[0] for n in BIG_NAMES if n != "dn_conv_w"}

    def take(l, recv):
        for nm, p in zip(layer_names(l), recv):
            idx = l if nm.startswith("ffn") else l // 2
            parts[nm][idx] = sum_parts(p.reshape(NDEV, -1, p.shape[-1]), f"sum_{nm}_{idx}").reshape(p.shape[1:])

    pending = None
    for l in reversed(range(L)):
        j = l // 2
        wgu1, wd1, win, wout, wgu2, wd2 = weights[l]
        x0, x1, x2, s1, sm, s2 = saved[l]
        dy, dg, dwgu2, dwd2 = ffn_bwd(x2, norm_ffn2[l:l + 1], wgu2, wd2, s2, dy, f"l{l}f2")
        gsmall["norm_ffn2"][l] = dg[0]
        if l % 2 == 0:
            cf, cd = split_comm("rs", pending, layer_axes(l + 1)) if pending is not None else (None, None)
            dy, dwin, dwout, sg, got_f, got_d = even_mixer_bwd(
                x1, norm_mix[l:l + 1], win, wout, convw[j], dn_norm_g[j:j + 1], fox_q_norm_g[j:j + 1],
                fox_k_norm_g[j:j + 1], sm, dy, f"l{l}mx", comm_fox=cf, comm_dn=cd)
            if pending is not None:
                take(l + 1, join_split(got_f, got_d))
            gconv[j] = sg.pop("dn_conv_w")
            for k_, v_ in sg.items():
                gsmall[k_][l if k_ == "norm_mix" else j] = v_
        else:
            cm = Comm("rs", pending, layer_axes(l + 1)) if pending is not None else None
            dy, dwin, dwout, sg, got = odd_mixer_bwd(x1, norm_mix[l:l + 1], win, wout, sm, dy, f"l{l}mx", comm=cm)
            if pending is not None:
                take(l + 1, got)
            gsmall["norm_mix"][l] = sg["norm_mix"]
        dy, dg, dwgu1, dwd1 = ffn_bwd(x0, norm_ffn1[l:l + 1], wgu1, wd1, s1, dy, f"l{l}f1")
        gsmall["norm_ffn1"][l] = dg[0]
        pending = [dwgu1, dwd1, dwin, dwout, dwgu2, dwd2]
    take(0, scatter_parts(pending, layer_axes(0), "rs_layer0"))
    grad_x = dy[None]

    small_list = [jnp.stack(gsmall[n]) for n in SMALL_NAMES] + [jnp.stack(gconv), loss_row[0, :1]]
    packed = _pack_small(small_list)
    gathered = all_gather([packed], [0], "ag_small")[0]
    summed = sum_parts(gathered.reshape(NDEV, packed.shape[0], LANES), "sum_small")
    small_sum = _unpack_small(summed, small_list)
    G = dict(zip(SMALL_NAMES, small_sum[:len(SMALL_NAMES)]))
    conv_full = small_sum[len(SMALL_NAMES)]
    cwid = dn_conv_w.shape[2]
    G["dn_conv_w"] = lax.dynamic_slice_in_dim(conv_full, me * cwid, cwid, axis=2)
    loss = small_sum[-1][0]

    def unprep_gu(g):
        return g.reshape(L, D, 2, hp)[..., :hs].reshape(L, D, 2 * hs)

    G["ffn1_w_gu"] = unprep_gu(jnp.stack(parts["ffn1_w_gu"]))
    G["ffn2_w_gu"] = unprep_gu(jnp.stack(parts["ffn2_w_gu"]))
    G["ffn1_w_down"] = jnp.stack(parts["ffn1_w_down"])[:, :hs]
    G["ffn2_w_down"] = jnp.stack(parts["ffn2_w_down"])[:, :hs]
    G["w_in_even"] = _even_from_mine(jnp.stack(parts["w_in_even"]), D)
    G["w_out_even"] = jnp.stack(parts["w_out_even"])
    G["w_in_odd"] = jnp.stack(parts["w_in_odd"])
    G["w_out_odd"] = jnp.stack(parts["w_out_odd"])

    delta, new_m, new_v = {}, {}, {}
    for n in BIG_NAMES:
        shp = W[n].shape
        two = lambda a: a.reshape(-1, shp[-1])
        d_, m_, v_ = adamw(two(W[n]), two(G[n]), two(M[n]), two(V[n]), f"adamw_{n}")
        delta[n], new_m[n], new_v[n] = d_.reshape(shp), m_.reshape(shp), v_.reshape(shp)
    sw = [W[n] for n in SMALL_NAMES]
    d_, m_, v_ = adamw(_pack_small(sw), _pack_small([G[n] for n in SMALL_NAMES]), _pack_small([M[n] for n in SMALL_NAMES]),
                       _pack_small([V[n] for n in SMALL_NAMES]), "adamw_small")
    for n, a, b, c_ in zip(SMALL_NAMES, _unpack_small(d_, sw), _unpack_small(m_, sw), _unpack_small(v_, sw)):
        delta[n], new_m[n], new_v[n] = a, b, c_

    return (loss, grad_x, *[G[n] for n in WEIGHT_NAMES], *[delta[n] for n in WEIGHT_NAMES],
            *[new_m[n] for n in WEIGHT_NAMES], *[new_v[n] for n in WEIGHT_NAMES])
```

```python
import functools
import math

import jax
import jax.numpy as jnp
from jax import lax
from jax.experimental import pallas as pl
from jax.experimental.pallas import tpu as pltpu

f32 = jnp.float32
bf16 = jnp.bfloat16
SDS = jax.ShapeDtypeStruct

HEAD_DIM = 128
LANES = 128
CONV_WIDTH = 4
DN_CHUNK = 64
EPS = 1e-6
NDEV = 8
VMEM_LIMIT_BYTES = 56 * 1024 * 1024
HI = lax.Precision.HIGHEST
ADAMW_TILE_ELEMS = 384 * 1024

ADAM_LR = 0.001
ADAM_B1 = 0.9
ADAM_B2 = 0.999
ADAM_EPS = 1e-08
ADAM_WD = 0.01
ADAM_STEP = 10

NN = (((1,), (0,)), ((), ()))
NT = (((1,), (1,)), ((), ()))
TN = (((0,), (0,)), ((), ()))


def _me_and_peers():
    x, y, c = lax.axis_index("x"), lax.axis_index("y"), lax.axis_index("c")
    me = 4 * x + 2 * y + c
    peers = []
    for r in range(1, NDEV):
        px = 1 - x if (r >> 2) & 1 else x
        py = 1 - y if (r >> 1) & 1 else y
        pc = 1 - c if r & 1 else c
        peers.append(((px, py, pc), 4 * px + 2 * py + pc))
    return me, peers


def _slab(ref, axis, width, k):
    idx = [slice(None)] * len(ref.shape)
    idx[axis] = pl.ds(k * width, width)
    return ref.at[tuple(idx)]


class Comm:
    def __init__(self, kind, arrays, axes):
        self.kind, self.arrays, self.axes, self.n = kind, list(arrays), list(axes), len(arrays)

    def out_shape(self):
        out = []
        for a, ax in zip(self.arrays, self.axes):
            shp = list(a.shape)
            if self.kind != "rs":
                shp[ax] *= NDEV
                out.append(SDS(tuple(shp), a.dtype))
            else:
                shp[ax] //= NDEV
                out.append(SDS((NDEV, *shp), a.dtype))
        return out

    def sems(self):
        return [pltpu.SemaphoreType.DMA((self.n, NDEV - 1)), pltpu.SemaphoreType.DMA((self.n, NDEV - 1)),
                pltpu.SemaphoreType.DMA((self.n,))]

    def _src(self, ins, t, to_idx):
        if self.kind == "ag":
            return ins[t]
        return _slab(ins[t], self.axes[t], ins[t].shape[self.axes[t]] // NDEV, to_idx)

    def _dst(self, ins, outs, t, from_idx):
        if self.kind != "rs":
            return _slab(outs[t], self.axes[t], ins[t].shape[self.axes[t]], from_idx)
        return outs[t].at[from_idx]

    def _copies(self, ins, outs, sems, sending):
        send_sems, recv_sems, loc_sems = sems
        me, peers = _me_and_peers()
        local, remote = [], []
        for t in range(self.n):
            local.append(pltpu.make_async_copy(self._src(ins, t, me), self._dst(ins, outs, t, me), loc_sems.at[t]))
            for r, (peer, pidx) in enumerate(peers):
                remote.append(pltpu.make_async_remote_copy(
                    src_ref=self._src(ins, t, pidx), dst_ref=self._dst(ins, outs, t, me if sending else pidx),
                    send_sem=send_sems.at[t, r], recv_sem=recv_sems.at[t, r], device_id=peer,
                    device_id_type=pl.DeviceIdType.MESH))
        return local, remote

    def _two_level(self, ins, outs, sems):
        send_sems, recv_sems, loc_sems = sems
        x, y, c = lax.axis_index("x"), lax.axis_index("y"), lax.axis_index("c")
        me, sibling = (x, y, c), (x, y, 1 - c)
        chips = [(1 - x, y), (x, 1 - y), (1 - x, 1 - y)]
        dev = lambda p: 4 * p[0] + 2 * p[1] + p[2]

        def copy(t, k, block, to, from_shard):
            dst = self._dst(ins, outs, t, dev(block))
            return pltpu.make_async_remote_copy(
                src_ref=ins[t] if from_shard else dst, dst_ref=dst, send_sem=send_sems.at[t, k], recv_sem=recv_sems.at[t, k],
                device_id=to, device_id_type=pl.DeviceIdType.MESH)

        ts = range(self.n)
        local = [pltpu.make_async_copy(ins[t], self._dst(ins, outs, t, dev(me)), loc_sems.at[t]) for t in ts]
        first = [copy(t, 0, me, sibling, True) for t in ts]
        first += [copy(t, 1 + j, me, (*chip, c), True) for t in ts for j, chip in enumerate(chips)]
        return local, first, copy, chips, me, sibling, c

    def start(self, ins, outs, sems):
        if self.kind == "ag2":
            local, first = self._two_level(ins, outs, sems)[:2]
            for cp in local + first:
                cp.start()
            return
        local, remote = self._copies(ins, outs, sems, True)
        for cp in local + remote:
            cp.start()

    def wait(self, ins, outs, sems):
        if self.kind == "ag2":
            local, first, copy, chips, me, sibling, c = self._two_level(ins, outs, sems)
            passed = []
            for t in range(self.n):
                for j, chip in enumerate(chips):
                    copy(t, 1 + j, (*chip, c), me, False).wait_recv()
                    fwd = copy(t, 4 + j, (*chip, c), sibling, False)
                    fwd.start()
                    passed.append(fwd)
            for t in range(self.n):
                copy(t, 0, sibling, me, False).wait_recv()
                for j, chip in enumerate(chips):
                    copy(t, 4 + j, (*chip, 1 - c), me, False).wait_recv()
            for cp in first + passed:
                cp.wait_send()
            for cp in local:
                cp.wait()
            return
        local, remote = self._copies(ins, outs, sems, False)
        for cp in remote:
            cp.wait_recv()
            cp.wait_send()
        for cp in local:
            cp.wait()


def _call(body, name, grid, in_specs, out_specs, out_shape, scratch=(), comm=None):
    params = pltpu.CompilerParams(vmem_limit_bytes=VMEM_LIMIT_BYTES)
    if comm is None:
        return pl.pallas_call(body, name=name, grid=grid, in_specs=in_specs, out_specs=out_specs, out_shape=out_shape,
                              scratch_shapes=list(scratch), compiler_params=params)
    single = not isinstance(out_shape, (tuple, list))
    c_out_specs = [out_specs] if single else list(out_specs)
    c_out_shape = [out_shape] if single else list(out_shape)
    n_in, n_out, n_scr, n = len(in_specs), len(c_out_shape), len(scratch), comm.n
    anyspec = pl.BlockSpec(memory_space=pl.ANY)

    def wrapped(*refs):
        ins, refs = refs[:n_in], refs[n_in:]
        cins, refs = refs[:n], refs[n:]
        outs, refs = refs[:n_out], refs[n_out:]
        couts, refs = refs[:n], refs[n:]
        scr, sems = refs[:n_scr], refs[n_scr:]
        first = functools.reduce(lambda a, b: a & b, [pl.program_id(d) == 0 for d in range(len(grid))], True)
        last = functools.reduce(lambda a, b: a & b, [pl.program_id(d) == grid[d] - 1 for d in range(len(grid))], True)
        if grid:
            pl.when(first)(lambda: comm.start(cins, couts, sems))
            body(*ins, *outs, *scr)
            pl.when(last)(lambda: comm.wait(cins, couts, sems))
        else:
            comm.start(cins, couts, sems)
            if body is not None:
                body(*ins, *outs, *scr)
            comm.wait(cins, couts, sems)

    call = pl.pallas_call(
        wrapped, name=name, grid=grid, in_specs=list(in_specs) + [anyspec] * n, out_specs=c_out_specs + [anyspec] * n,
        out_shape=c_out_shape + comm.out_shape(), scratch_shapes=list(scratch) + comm.sems(), compiler_params=params)

    def run(*args):
        res = call(*args, *comm.arrays)
        mine = res[:n_out]
        return (mine[0] if single else tuple(mine)), list(res[n_out:])

    return run


def _tile(n, want, align=8):
    if n <= want:
        return n
    for t in range(want // align * align, 0, -align):
        if n % t == 0:
            return t
    return n


def _dot(a, b, dims=NN, precision=None):
    return lax.dot_general(a, b, dims, preferred_element_type=f32, precision=precision)


def _logsig(x):
    return jnp.minimum(x, 0.0) - jnp.log(1.0 + jnp.exp(-jnp.abs(x)))


def _softplus(x):
    return jnp.maximum(x, 0.0) + jnp.log(1.0 + jnp.exp(-jnp.abs(x)))


def _rms(x, g):
    return x * lax.rsqrt(jnp.mean(x * x, axis=-1, keepdims=True) + EPS) * g


def _lane_pick(blk, lane_idx):
    lane = lax.broadcasted_iota(jnp.int32, (1, LANES), 1)
    return jnp.sum(jnp.where(lane == lane_idx, blk, 0.0), axis=1, keepdims=True)


def mm(a, b, mode, out_dtype, name, tm=512, tn=512, res=None, scale=1.0, comm=None):
    if mode == "nn":
        (M, K), (_, N) = a.shape, b.shape
    elif mode == "nt":
        (M, K), (N, _) = a.shape, b.shape
    else:
        (K, M), (_, N) = a.shape, b.shape
    tm, tn = _tile(M, tm, LANES), _tile(N, tn, LANES)
    a_spec = pl.BlockSpec((K, tm), lambda j, i: (0, i)) if mode == "tn" else pl.BlockSpec((tm, K), lambda j, i: (i, 0))
    b_spec = pl.BlockSpec((tn, K), lambda j, i: (j, 0)) if mode == "nt" else pl.BlockSpec((K, tn), lambda j, i: (0, j))
    dims = {"nn": NN, "nt": NT, "tn": TN}[mode]
    o_spec = pl.BlockSpec((tm, tn), lambda j, i: (i, j))

    def body(*refs):
        acc = _dot(refs[0][...].astype(bf16), refs[1][...].astype(bf16), dims)
        if scale != 1.0:
            acc = acc * scale
        if res is not None:
            acc = acc + refs[2][...]
        refs[-1][...] = acc.astype(out_dtype)

    ins, specs = [a, b], [a_spec, b_spec]
    if res is not None:
        ins.append(res)
        specs.append(o_spec)
    return _call(body, name, (N // tn, M // tm), specs, o_spec, SDS((M, N), out_dtype), comm=comm)(*ins)


def _rowspec(tm, w, cb=0):
    return pl.BlockSpec((tm, w), lambda i: (i, cb))


def _bspec(w):
    return pl.BlockSpec((1, w), lambda i: (0, 0))


def rms_fwd(x, g, name):
    S, D = x.shape
    tm = _tile(S, 512)

    def body(x_ref, g_ref, h_ref):
        h_ref[...] = _rms(x_ref[...], g_ref[...]).astype(bf16)

    return _call(body, name, (S // tm,), [_rowspec(tm, D), _bspec(D)], _rowspec(tm, D), SDS((S, D), bf16))(x, g)


def _swiglu(g, u):
    return g * jax.nn.sigmoid(g) * u


def ffn_gu_act(h, wgu, name, tm=1024, tn=768, comm=None):
    S, D = h.shape
    W = wgu.shape[1] // 2
    tm, tn = _tile(S, tm, LANES), _tile(W, tn, LANES)
    nb = W // tn

    def body(h_ref, wg_ref, wu_ref, gu_ref, a_ref):
        hb = h_ref[...]
        g = _dot(hb, wg_ref[...])
        u = _dot(hb, wu_ref[...])
        gu_ref[0] = g.astype(bf16)
        gu_ref[1] = u.astype(bf16)
        a_ref[...] = _swiglu(g, u).astype(bf16)

    return _call(body, name, (nb, S // tm),
                 [pl.BlockSpec((tm, D), lambda j, i: (i, 0)), pl.BlockSpec((D, tn), lambda j, i: (0, j)),
                  pl.BlockSpec((D, tn), lambda j, i: (0, nb + j))],
                 (pl.BlockSpec((2, tm, tn), lambda j, i: (0, i, j)), pl.BlockSpec((tm, tn), lambda j, i: (i, j))),
                 (SDS((2, S, W), bf16), SDS((S, W), bf16)), comm=comm)(h, wgu, wgu)


def ffn_bda_act(dy, wd, gu, scale, name, tm=512, tn=768):
    S, D = dy.shape
    W = wd.shape[0]
    tm, tn = _tile(S, tm, LANES), _tile(W, tn, LANES)

    def body(dy_ref, wd_ref, gu_ref, dgu_ref):
        da = _dot(dy_ref[...].astype(bf16), wd_ref[...], NT) * scale
        _, vjp = jax.vjp(_swiglu, gu_ref[0].astype(f32), gu_ref[1].astype(f32))
        dg, du = vjp(da)
        dgu_ref[0] = dg.astype(bf16)
        dgu_ref[1] = du.astype(bf16)

    planes = pl.BlockSpec((2, tm, tn), lambda j, i: (0, i, j))
    return _call(body, name, (W // tn, S // tm),
                 [pl.BlockSpec((tm, D), lambda j, i: (i, 0)), pl.BlockSpec((tn, D), lambda j, i: (j, 0)), planes],
                 planes, SDS((2, S, W), bf16))(dy, wd, gu)


def ffn_bwgu(h, dgu, name, tm=512, tn=768):
    S, D = h.shape
    W = dgu.shape[2]
    tm, tn = _tile(D, tm, LANES), _tile(W, tn, LANES)
    nb = W // tn

    def body(h_ref, d_ref, o_ref):
        o_ref[...] = _dot(h_ref[...], d_ref[...], TN).astype(bf16)

    return _call(body, name, (2 * nb, D // tm),
                 [pl.BlockSpec((S, tm), lambda j, i: (0, i)), pl.BlockSpec((None, S, tn), lambda j, i: (j // nb, 0, j % nb))],
                 pl.BlockSpec((tm, tn), lambda j, i: (i, j)), SDS((D, 2 * W), bf16))(h, dgu)


def nt_rms_bwd(pairs, x, g, dres, name, tm=256):
    S, D = x.shape
    tm = _tile(S, tm)
    n = len(pairs)

    def body(*refs):
        x_ref, g_ref, dres_ref = refs[2 * n:2 * n + 3]
        dx_ref, dg_ref = refs[2 * n + 3:]
        dh = sum(_dot(refs[2 * k][...].astype(bf16), refs[2 * k + 1][...], NT) for k in range(n))
        _, vjp = jax.vjp(_rms, x_ref[...], g_ref[...])
        dx, dg = vjp(dh)
        dx_ref[...] = dres_ref[...] + dx

        @pl.when(pl.program_id(0) == 0)
        def _():
            dg_ref[...] = jnp.zeros_like(dg_ref)

        dg_ref[...] += dg

    arrays, specs = [], []
    for a, a_spec, b, b_spec in pairs:
        arrays += [a, b]
        specs += [a_spec, b_spec]
    return _call(body, name, (S // tm,), specs + [_rowspec(tm, D), _bspec(D), _rowspec(tm, D)],
                 (_rowspec(tm, D), _bspec(D)), (SDS((S, D), f32), SDS((1, D), f32)))(*arrays, x, g, dres)


def ffn_bdh_rms(dgu, wgu, x, g, dres, name, tm=256):
    _, S, W = dgu.shape
    D = wgu.shape[0]
    tm = _tile(S, tm)
    pairs = [(dgu, pl.BlockSpec((None, tm, W), functools.partial(lambda p, i: (p, i, 0), p)),
              wgu, pl.BlockSpec((D, W), functools.partial(lambda p, i: (0, p), p))) for p in range(2)]
    return nt_rms_bwd(pairs, x, g, dres, name, tm)


def mm_bdh_rms(d, w, x, g, dres, name, tm=256):
    S, K = d.shape
    tm = _tile(S, tm)
    return nt_rms_bwd([(d, pl.BlockSpec((tm, K), lambda i: (i, 0)), w, pl.BlockSpec(w.shape, lambda i: (0, 0)))], x, g, dres, name, tm)


def loss_head(y, t, name):
    S, D = y.shape
    tm = _tile(S, 512)

    def body(y_ref, t_ref, dy_ref, l_ref):
        e = y_ref[...] - t_ref[...]
        dy_ref[...] = e * (1.0 / D)

        @pl.when(pl.program_id(0) == 0)
        def _():
            l_ref[...] = jnp.zeros_like(l_ref)

        l_ref[...] += jnp.sum(jnp.sum(e * e, axis=1, keepdims=True), axis=0, keepdims=True) * (0.5 / D)

    return _call(body, name, (S // tm,), [_rowspec(tm, D), _rowspec(tm, D)], (_rowspec(tm, D), _bspec(LANES)),
                 (SDS((S, D), f32), SDS((1, LANES), f32)))(y, t)


def ffn_fwd(x, g, wgu, wd, tag, comm_gu=None, comm_down=None):
    h = rms_fwd(x, g, f"{tag}_rms")
    (gu, a), got_gu = _carried(ffn_gu_act(h, wgu, f"{tag}_gu", comm=comm_gu), comm_gu)
    xo, got_down = _carried(mm(a, wd, "nn", f32, f"{tag}_down", tm=512, tn=512, res=x, scale=0.5, comm=comm_down), comm_down)
    return xo, (h, gu, a), got_gu, got_down


def ffn_bwd(x, g, wgu, wd, saved, dy, tag):
    h, gu, a = saved
    dgu = ffn_bda_act(dy, wd, gu, 0.5, f"{tag}_bda")
    dwd = mm(a, dy, "tn", bf16, f"{tag}_bwd", tm=512, tn=512, scale=0.5)
    dwgu = ffn_bwgu(h, dgu, f"{tag}_bwgu")
    dx, dg = ffn_bdh_rms(dgu, wgu, x, g, dy, f"{tag}_bdh")
    return dx, dg, dwgu, dwd


def _split_bf16(x):
    hi = x.astype(bf16)
    lo = (x - hi.astype(f32)).astype(bf16)
    return hi, lo


SB_TQ, SB_TK, SB_NSUB = 512, 256, 4
FOX_TK = 256


def _sb_geometry(S, tk=SB_TK):
    tq = min(SB_TQ, S)
    tk = min(tk, tq)
    return tq, tk, tq // SB_NSUB, tq // tk


def _sb_consts(tk):
    r = lax.broadcasted_iota(jnp.int32, (tk, tk), 0)
    c = lax.broadcasted_iota(jnp.int32, (tk, tk), 1)
    return (r > c).astype(bf16), (r < c).astype(bf16)


def _sb_scores(qs, k, kb, tk, rows, masked):
    scale = HEAD_DIM ** -0.5
    z = [_dot(q, k, NT) * scale for q in qs]
    ls = [_logsig(z_) for z_ in z]
    lm = [l - z_ for l, z_ in zip(ls, z)]
    ok = None
    if masked:
        col = kb * tk + lax.broadcasted_iota(jnp.int32, z[0].shape, 1)
        ok = [col < row for row in rows]
        lm = [jnp.where(m, x, 0.0) for m, x in zip(ok, lm)]
    return ls, lm, ok


def _sb_weights(ls, lm, ok, tail, after):
    parts = [_split_bf16(x) for x in lm]
    suf = [_dot(hi, after) + _dot(lo, after) for hi, lo in parts]
    a = [jnp.exp(l + s_ + t_) for l, s_, t_ in zip(ls, suf, tail)]
    if ok is not None:
        a = [jnp.where(m, x, 0.0) for m, x in zip(ok, a)]
    return a


def sb_fwd(qkv, nh, name, comm=None):
    S = qkv.shape[0]
    tq, tk, rs, nd = _sb_geometry(S)
    nsub = tq // rs

    def body(q_ref, k_ref, v_ref, o_ref, tails_ref):
        i = pl.program_id(1)
        after, _ = _sb_consts(tk)
        qs = [q_ref[pl.ds(s * rs, rs), :] for s in range(nsub)]
        rows = [i * tq + s * rs + lax.broadcasted_iota(jnp.int32, (rs, tk), 0) for s in range(nsub)]

        def tile(kb, carry, masked):
            tail, acc = carry
            off = pl.multiple_of(kb * tk, tk)
            k = k_ref[pl.ds(off, tk), :]
            v = v_ref[pl.ds(off, tk), :]
            ls, lm, ok = _sb_scores(qs, k, kb, tk, rows, masked)
            a = _sb_weights(ls, lm, ok, tail, after)
            tails_ref[pl.ds(kb, 1), :] += jnp.concatenate([t_.T for t_ in tail], axis=1)
            acc = [ac + _dot(a_.astype(bf16), v) for ac, a_ in zip(acc, a)]
            tail = [t_ + jnp.sum(x, axis=1, keepdims=True) for t_, x in zip(tail, lm)]
            return tail, acc

        tails_ref[...] = jnp.zeros_like(tails_ref)
        carry = ([jnp.zeros((rs, 1), f32)] * nsub, [jnp.zeros((rs, HEAD_DIM), f32)] * nsub)
        for d in reversed(range(nd)):
            carry = tile(i * nd + d, carry, True)
        carry = lax.fori_loop(0, i * nd, lambda t, c: tile(i * nd - 1 - t, c, False), carry)
        for s in range(nsub):
            o_ref[pl.ds(s * rs, rs), :] = carry[1][s].astype(o_ref.dtype)

    blk = lambda cb0: pl.BlockSpec((tq, HEAD_DIM), lambda h, i: (i, cb0 + h))
    full = lambda cb0: pl.BlockSpec((S, HEAD_DIM), lambda h, i: (0, cb0 + h))
    tails = pl.BlockSpec((S // tk, tq), lambda h, i: (h, i))
    return _call(body, name, (nh, S // tq), [blk(0), full(nh), full(2 * nh)], (blk(0), tails),
                 (SDS((S, nh * HEAD_DIM), bf16), SDS((nh * (S // tk), S), f32)), comm=comm)(qkv, qkv, qkv)


def sb_bwd(qkv, tails, do, nh, name, comm=None):
    S = qkv.shape[0]
    tq, tk, rs, nd = _sb_geometry(S)
    nsub = tq // rs
    scale = HEAD_DIM ** -0.5

    def body(q_ref, k_ref, v_ref, tails_ref, do_ref, dq_ref, dk_ref, dv_ref):
        i = pl.program_id(1)

        @pl.when(i == 0)
        def _():
            dk_ref[...] = jnp.zeros_like(dk_ref)
            dv_ref[...] = jnp.zeros_like(dv_ref)

        after, before = _sb_consts(tk)
        qs = [q_ref[pl.ds(s * rs, rs), :] for s in range(nsub)]
        dos = [do_ref[pl.ds(s * rs, rs), :].astype(bf16) for s in range(nsub)]
        rows = [i * tq + s * rs + lax.broadcasted_iota(jnp.int32, (rs, tk), 0) for s in range(nsub)]

        def sweep2(kb, carry, masked):
            pe, dq = carry
            off = pl.multiple_of(kb * tk, tk)
            k = k_ref[pl.ds(off, tk), :]
            v = v_ref[pl.ds(off, tk), :]
            ls, lm, ok = _sb_scores(qs, k, kb, tk, rows, masked)
            tail_row = tails_ref[pl.ds(kb, 1), :]
            tail = [tail_row[:, s * rs:(s + 1) * rs].T for s in range(nsub)]
            a = _sb_weights(ls, lm, ok, tail, after)
            e = [_dot(d_, v, NT) * a_ for d_, a_ in zip(dos, a)]
            cum_e = [p_ + _dot(e_.astype(bf16), before) for p_, e_ in zip(pe, e)]
            sig = [jnp.exp(l) for l in ls]
            dz = [e_ * (1.0 - sg) - c_ * sg for e_, sg, c_ in zip(e, sig, cum_e)]
            if ok is not None:
                dz = [jnp.where(m, x, 0.0) for m, x in zip(ok, dz)]
            dzb = [(x * scale).astype(bf16) for x in dz]
            dk_ref[pl.ds(off, tk), :] += sum(_dot(x, q, TN) for x, q in zip(dzb, qs))
            dv_ref[pl.ds(off, tk), :] += sum(_dot(a_.astype(bf16), d_, TN) for a_, d_ in zip(a, dos))
            pe = [p_ + jnp.sum(e_, axis=1, keepdims=True) for p_, e_ in zip(pe, e)]
            dq = [g + _dot(x, k) for g, x in zip(dq, dzb)]
            return pe, dq

        carry = ([jnp.zeros((rs, 1), f32)] * nsub, [jnp.zeros((rs, HEAD_DIM), f32)] * nsub)
        carry = lax.fori_loop(0, i * nd, lambda kb, c: sweep2(kb, c, False), carry)
        for d in range(nd):
            carry = sweep2(i * nd + d, carry, True)
        for s in range(nsub):
            dq_ref[pl.ds(s * rs, rs), :] = carry[1][s]

    blk = lambda cb0: pl.BlockSpec((tq, HEAD_DIM), lambda h, i: (i, cb0 + h))
    full = lambda cb0: pl.BlockSpec((S, HEAD_DIM), lambda h, i: (0, cb0 + h))
    sh = SDS((S, nh * HEAD_DIM), f32)
    tails_spec = pl.BlockSpec((S // tk, tq), lambda h, i: (h, i))
    return _call(body, name, (nh, S // tq), [blk(0), full(nh), full(2 * nh), tails_spec, blk(0)], (blk(0), full(0), full(0)),
                 (sh, sh, sh), comm=comm)(qkv, qkv, qkv, tails, do)


def fox_fwd(fq, fk, fv, c, cT, nh, lane0, name, comm=None):
    S = fq.shape[0]
    tq, tk, rs, nd = _sb_geometry(S, FOX_TK)
    nsub = tq // rs
    scale = HEAD_DIM ** -0.5

    def body(q_ref, k_ref, v_ref, c_ref, ct_ref, o_ref, lse_ref):
        h = pl.program_id(0)
        i = pl.program_id(1)
        subs = range(nsub)
        qs = [q_ref[pl.ds(s * rs, rs), :] for s in subs]
        ci = [_lane_pick(c_ref[pl.ds(s * rs, rs), :], lane0 + h) for s in subs]
        rows = [i * tq + s * rs + lax.broadcasted_iota(jnp.int32, (rs, tk), 0) for s in subs]

        def tile(kb, carry, masked):
            m, l, acc = carry
            off = pl.multiple_of(kb * tk, tk)
            k = k_ref[pl.ds(off, tk), :]
            v = v_ref[pl.ds(off, tk), :]
            cj = ct_ref[pl.ds(lane0 % 8 + h, 1), pl.ds(off, tk)]
            sc = [_dot(q, k, NT) * scale + (c_ - cj) for q, c_ in zip(qs, ci)]
            if masked:
                col = kb * tk + lax.broadcasted_iota(jnp.int32, (rs, tk), 1)
                sc = [jnp.where(col <= row, x, -jnp.inf) for x, row in zip(sc, rows)]
            m2 = [jnp.maximum(m_, jnp.max(x, axis=1, keepdims=True)) for m_, x in zip(m, sc)]
            p = [jnp.exp(x - m_) for x, m_ in zip(sc, m2)]
            al = [jnp.exp(a - b) for a, b in zip(m, m2)]
            l = [a * l_ + jnp.sum(p_, axis=1, keepdims=True) for a, l_, p_ in zip(al, l, p)]
            acc = [a * ac + _dot(p_.astype(bf16), v) for a, ac, p_ in zip(al, acc, p)]
            return m2, l, acc

        carry = ([jnp.full((rs, 1), -jnp.inf, f32)] * nsub, [jnp.zeros((rs, 1), f32)] * nsub,
                 [jnp.zeros((rs, HEAD_DIM), f32)] * nsub)
        for d in range(nd):
            carry = tile(i * nd + d, carry, True)
        m, l, acc = lax.fori_loop(0, i * nd, lambda kb, cr: tile(kb, cr, False), carry)
        for s in subs:
            o_ref[pl.ds(s * rs, rs), :] = acc[s] / l[s]
            lse_ref[pl.ds(s * rs, rs), :] = jnp.broadcast_to(m[s] + jnp.log(l[s]), (rs, HEAD_DIM))

    blk = pl.BlockSpec((tq, HEAD_DIM), lambda h, i: (i, h))
    full = pl.BlockSpec((S, HEAD_DIM), lambda h, i: (0, h))
    cspec = pl.BlockSpec((tq, LANES), lambda h, i: (i, 0))
    ctspec = pl.BlockSpec((8, S), lambda h, i: (lane0 // 8, 0))
    sh = SDS((S, nh * HEAD_DIM), f32)
    return _call(body, name, (nh, S // tq), [blk, full, full, cspec, ctspec], (blk, blk), (sh, sh), comm=comm)(fq, fk, fv, c, cT)


def fox_bwd(fq, fk, fv, c, cT, o, lse, do, nh, lane0, name, comm=None):
    S = fq.shape[0]
    tq, tk, rs, nd = _sb_geometry(S, FOX_TK)
    nsub = tq // rs
    scale = HEAD_DIM ** -0.5

    def body(q_ref, k_ref, v_ref, c_ref, ct_ref, o_ref, lse_ref, do_ref, dq_ref, dk_ref, dv_ref, dct_ref):
        h = pl.program_id(0)
        i = pl.program_id(1)

        @pl.when(i == 0)
        def _():
            dk_ref[...] = jnp.zeros_like(dk_ref)
            dv_ref[...] = jnp.zeros_like(dv_ref)

        @pl.when((i == 0) & (h == 0))
        def _():
            dct_ref[...] = jnp.zeros_like(dct_ref)

        subs = range(nsub)
        qs = [q_ref[pl.ds(s * rs, rs), :] for s in subs]
        dof = [do_ref[pl.ds(s * rs, rs), :] for s in subs]
        dos = [x.astype(bf16) for x in dof]
        ci = [_lane_pick(c_ref[pl.ds(s * rs, rs), :], lane0 + h) for s in subs]
        lses = [lse_ref[pl.ds(s * rs, rs), :][:, :1] for s in subs]
        dl = [jnp.sum(x * o_ref[pl.ds(s * rs, rs), :], axis=1, keepdims=True) for s, x in zip(subs, dof)]
        rows = [i * tq + s * rs + lax.broadcasted_iota(jnp.int32, (rs, tk), 0) for s in subs]

        def tile(kb, carry, masked):
            dq, rsum = carry
            off = pl.multiple_of(kb * tk, tk)
            k = k_ref[pl.ds(off, tk), :]
            v = v_ref[pl.ds(off, tk), :]
            cj = ct_ref[pl.ds(lane0 % 8 + h, 1), pl.ds(off, tk)]
            p = [jnp.exp(_dot(q, k, NT) * scale + (c_ - cj) - ls) for q, c_, ls in zip(qs, ci, lses)]
            if masked:
                col = kb * tk + lax.broadcasted_iota(jnp.int32, (rs, tk), 1)
                p = [jnp.where(col <= row, x, 0.0) for x, row in zip(p, rows)]
            ds = [p_ * (_dot(d_, v, NT) - dl_) for p_, d_, dl_ in zip(p, dos, dl)]
            dsb = [(x * scale).astype(bf16) for x in ds]
            dk_ref[pl.ds(off, tk), :] += sum(_dot(x, q, TN) for x, q in zip(dsb, qs))
            dv_ref[pl.ds(off, tk), :] += sum(_dot(p_.astype(bf16), d_, TN) for p_, d_ in zip(p, dos))
            dct_ref[pl.ds(lane0 + h, 1), pl.ds(off, tk)] -= sum(jnp.sum(x, axis=0, keepdims=True) for x in ds)
            return ([g + _dot(x, k) for g, x in zip(dq, dsb)],
                    [r_ + jnp.sum(x, axis=1, keepdims=True) for r_, x in zip(rsum, ds)])

        carry = ([jnp.zeros((rs, HEAD_DIM), f32)] * nsub, [jnp.zeros((rs, 1), f32)] * nsub)
        carry = lax.fori_loop(0, i * nd, lambda kb, cr: tile(kb, cr, False), carry)
        for d in range(nd):
            carry = tile(i * nd + d, carry, True)
        dq, rsum = carry
        for s in subs:
            dq_ref[pl.ds(s * rs, rs), :] = dq[s]
        dct_ref[pl.ds(lane0 + h, 1), pl.ds(pl.multiple_of(i * tq, tq), tq)] += jnp.concatenate([r_.T for r_ in rsum], axis=1)

    blk = pl.BlockSpec((tq, HEAD_DIM), lambda h, i: (i, h))
    full = pl.BlockSpec((S, HEAD_DIM), lambda h, i: (0, h))
    cspec = pl.BlockSpec((tq, LANES), lambda h, i: (i, 0))
    ctspec = pl.BlockSpec((8, S), lambda h, i: (lane0 // 8, 0))
    dctspec = pl.BlockSpec((LANES, S), lambda h, i: (0, 0))
    sh = SDS((S, nh * HEAD_DIM), f32)
    return _call(body, name, (nh, S // tq), [blk, full, full, cspec, ctspec, blk, blk, blk], (blk, full, full, dctspec),
                 (sh, sh, sh, SDS((LANES, S), f32)), comm=comm)(fq, fk, fv, c, cT, o, lse, do)


def gates_fwd(proj, small_cb, fbias_row, name, tm=256):
    S = proj.shape[0]
    tm = _tile(S, tm)

    def body(sm_ref, fb_ref, c_ref, ct_ref, carry_ref):
        @pl.when(pl.program_id(0) == 0)
        def _():
            carry_ref[...] = jnp.zeros_like(carry_ref)

        lf = _logsig(sm_ref[...] + fb_ref[...])
        r = lax.broadcasted_iota(jnp.int32, (tm, tm), 0)
        cc = lax.broadcasted_iota(jnp.int32, (tm, tm), 1)
        c = _dot((r >= cc).astype(f32), lf, NN, HI) + carry_ref[...]
        carry_ref[...] += jnp.sum(lf, axis=0, keepdims=True)
        c_ref[...] = c
        ct_ref[...] = c.T

    return _call(body, name, (S // tm,), [_rowspec(tm, LANES, small_cb), _bspec(LANES)],
                 (_rowspec(tm, LANES), pl.BlockSpec((LANES, tm), lambda i: (0, i))),
                 (SDS((S, LANES), f32), SDS((LANES, S), f32)), scratch=[pltpu.VMEM((1, LANES), f32)])(proj, fbias_row)


def gates_bwd(proj, small_cb, fbias_row, dcT, dsm_dn, lane0, nh, name, tm=256):
    S = proj.shape[0]
    tm = _tile(S, tm)
    nt = S // tm

    def body(sm_ref, fb_ref, dct_ref, dsm_ref, o_ref, dfb_ref, carry_ref):
        @pl.when(pl.program_id(0) == 0)
        def _():
            carry_ref[...] = jnp.zeros_like(carry_ref)
            dfb_ref[...] = jnp.zeros_like(dfb_ref)

        dc = dct_ref[...].T
        r = lax.broadcasted_iota(jnp.int32, (tm, tm), 0)
        cc = lax.broadcasted_iota(jnp.int32, (tm, tm), 1)
        dlf = _dot((r <= cc).astype(f32), dc, NN, HI) + carry_ref[...]
        carry_ref[...] += jnp.sum(dc, axis=0, keepdims=True)
        lane = lax.broadcasted_iota(jnp.int32, (1, LANES), 1)
        dpre = jnp.where((lane >= lane0) & (lane < lane0 + nh), dlf * jax.nn.sigmoid(-(sm_ref[...] + fb_ref[...])), 0.0)
        o_ref[...] = dsm_ref[...] + dpre
        dfb_ref[...] += jnp.sum(dpre, axis=0, keepdims=True)

    rev = lambda i: nt - 1 - i
    return _call(body, name, (nt,),
                 [pl.BlockSpec((tm, LANES), lambda i: (rev(i), small_cb)), _bspec(LANES),
                  pl.BlockSpec((LANES, tm), lambda i: (0, rev(i))), pl.BlockSpec((tm, LANES), lambda i: (rev(i), 0))],
                 (pl.BlockSpec((tm, LANES), lambda i: (rev(i), 0)), _bspec(LANES)),
                 (SDS((S, LANES), f32), SDS((1, LANES), f32)), scratch=[pltpu.VMEM((1, LANES), f32)])(proj, fbias_row, dcT, dsm_dn)


PAD_ROWS = 8


def conv_fwd(proj, w, width, name):
    S = proj.shape[0]
    cw = 256 if width % 256 == 0 else 128

    def body(x_ref, w_ref, y_ref, pad_ref):
        pad_ref[pl.ds(0, PAD_ROWS), :] = jnp.zeros((PAD_ROWS, cw), f32)
        pad_ref[pl.ds(PAD_ROWS, S), :] = x_ref[...]
        y = jnp.zeros((S, cw), f32)
        for i in range(CONV_WIDTH):
            y = y + pad_ref[pl.ds(PAD_ROWS - (CONV_WIDTH - 1) + i, S), :] * w_ref[pl.ds(i, 1), :]
        y_ref[...] = y * jax.nn.sigmoid(y)

    spec = pl.BlockSpec((S, cw), lambda j: (0, j))
    return _call(body, name, (width // cw,), [spec, pl.BlockSpec((CONV_WIDTH, cw), lambda j: (0, j))], spec,
                 SDS((S, width), f32), scratch=[pltpu.VMEM((S + PAD_ROWS, cw), f32)])(proj, w)


def conv_bwd(proj, w, dy, name):
    S, width = dy.shape
    cw = 256 if width % 256 == 0 else 128

    def body(x_ref, w_ref, dy_ref, dx_ref, dw_ref, xpad_ref, dpad_ref):
        xpad_ref[pl.ds(0, PAD_ROWS), :] = jnp.zeros((PAD_ROWS, cw), f32)
        xpad_ref[pl.ds(PAD_ROWS, S), :] = x_ref[...]
        y = jnp.zeros((S, cw), f32)
        for i in range(CONV_WIDTH):
            y = y + xpad_ref[pl.ds(PAD_ROWS - (CONV_WIDTH - 1) + i, S), :] * w_ref[pl.ds(i, 1), :]
        sg = jax.nn.sigmoid(y)
        dpre = dy_ref[...] * (sg * (1.0 + y * (1.0 - sg)))
        dpad_ref[pl.ds(S, PAD_ROWS), :] = jnp.zeros((PAD_ROWS, cw), f32)
        dpad_ref[pl.ds(0, S), :] = dpre
        dx = jnp.zeros((S, cw), f32)
        for i in range(CONV_WIDTH):
            dx = dx + dpad_ref[pl.ds(CONV_WIDTH - 1 - i, S), :] * w_ref[pl.ds(i, 1), :]
            dw_ref[pl.ds(i, 1), :] = jnp.sum(dpre * xpad_ref[pl.ds(PAD_ROWS - (CONV_WIDTH - 1) + i, S), :], axis=0, keepdims=True)
        dx_ref[...] = dx

    spec = pl.BlockSpec((S, cw), lambda j: (0, j))
    wspec = pl.BlockSpec((CONV_WIDTH, cw), lambda j: (0, j))
    return _call(body, name, (width // cw,), [spec, wspec, spec], (spec, wspec),
                 (SDS((S, width), f32), SDS((CONV_WIDTH, width), f32)),
                 scratch=[pltpu.VMEM((S + PAD_ROWS, cw), f32), pltpu.VMEM((S + PAD_ROWS, cw), f32)])(proj, w, dy)


def _pdot_raw(a, b, dims, passes):
    if passes == 1:
        return _dot(a.astype(bf16), b.astype(bf16), dims)
    ah, al = _split_bf16(a)
    bh, bl = _split_bf16(b)
    return _dot(ah, bh, dims) + (_dot(ah, bl, dims) + _dot(al, bh, dims))


def _make_pdot(passes):
    @functools.partial(jax.custom_vjp, nondiff_argnums=(2,))
    def pdot(a, b, mode):
        return _pdot_raw(a, b, {"nn": NN, "nt": NT, "tn": TN}[mode], passes)

    def fwd(a, b, mode):
        return pdot(a, b, mode), (a, b)

    def bwd(mode, res, g):
        a, b = res
        if mode == "nn":
            return _pdot_raw(g, b, NT, passes), _pdot_raw(a, g, TN, passes)
        if mode == "nt":
            return _pdot_raw(g, b, NN, passes), _pdot_raw(g, a, TN, passes)
        return _pdot_raw(b, g, NT, passes), _pdot_raw(a, g, NN, passes)

    pdot.defvjp(fwd, bwd)
    return pdot


_dot1 = _make_pdot(1)
_dot3 = _make_pdot(3)


def _each(f, *lists):
    return [f(*xs) for xs in zip(*lists)]


def _dn_chunk(ndn, cq, ck, cv, small, alog, dtb, s0):
    C = cq[0].shape[0]
    hs = list(range(ndn))
    b = [_lane_pick(small, h) for h in hs]
    d = [_lane_pick(small, ndn + h) for h in hs]
    al = [_lane_pick(alog, h) for h in hs]
    dt = [_lane_pick(dtb, h) for h in hs]
    q = _each(lambda x: x * lax.rsqrt(jnp.sum(x * x, axis=1, keepdims=True) + EPS) * (HEAD_DIM ** -0.5), cq)
    k = _each(lambda x: x * lax.rsqrt(jnp.sum(x * x, axis=1, keepdims=True) + EPS), ck)
    beta = _each(jax.nn.sigmoid, b)
    g = _each(lambda al_, d_, dt_: -jnp.exp(al_) * _softplus(d_ + dt_), al, d, dt)
    r = lax.broadcasted_iota(jnp.int32, (C, C), 0)
    c = lax.broadcasted_iota(jnp.int32, (C, C), 1)
    incl, strict = r >= c, r > c
    inclf = incl.astype(f32)
    gc = _each(lambda g_: _dot3(inclf, g_, "nn"), g)
    decay = _each(lambda gc_: jnp.where(incl, jnp.exp(jnp.where(incl, gc_ - gc_.T, 0.0)), 0.0), gc)
    kb = _each(lambda k_, b_: k_ * b_, k, beta)
    a = _each(lambda kb_, k_, dec: jnp.where(strict, _dot3(kb_, k_, "nt") * dec, 0.0), kb, k, decay)
    eye = (r == c).astype(f32)
    t = _each(lambda a_: eye - a_, a)
    p = a
    for _ in range(int(math.log2(C)) - 1):
        p = _each(lambda p_: _dot3(p_, p_, "nn"), p)
        t = _each(lambda t_, p_: t_ + _dot3(t_, p_, "nn"), t, p)
    eg = _each(jnp.exp, gc)
    u = _each(lambda t_, v_, b_: _dot3(t_, v_ * b_, "nn"), t, cv, beta)
    w = _each(lambda t_, kb_, eg_: _dot3(t_, kb_ * eg_, "nn"), t, kb, eg)
    attn = _each(lambda q_, k_, dec: _dot1(q_, k_, "nt") * dec, q, k, decay)
    g_last = _each(lambda g_: jnp.sum(g_, axis=0, keepdims=True), g)
    v_new = _each(lambda u_, w_, s_: u_ - _dot1(w_, s_, "nn"), u, w, s0)
    o = _each(lambda q_, eg_, s_, at, vn: _dot1(q_ * eg_, s_, "nn") + _dot1(at, vn, "nn"), q, eg, s0, attn, v_new)
    s1 = _each(lambda s_, gl, k_, gc_, vn: s_ * jnp.exp(gl) + _dot1(k_ * jnp.exp(gl - gc_), vn, "tn"), s0, g_last, k, gc, v_new)
    return o, s1


def dn_fwd(cqkv, proj, small_cb, alog_row, dt_row, ndn, name, comm=None):
    S = cqkv.shape[0]
    C = DN_CHUNK
    nc = S // C
    half = ndn * HEAD_DIM

    def body(q_ref, k_ref, v_ref, sm_ref, al_ref, dt_ref, o_ref, ss_ref, s_ref):
        @pl.when(pl.program_id(0) == 0)
        def _():
            s_ref[...] = jnp.zeros_like(s_ref)

        sls = [slice(h * HEAD_DIM, (h + 1) * HEAD_DIM) for h in range(ndn)]
        s0 = [s_ref[h] for h in range(ndn)]
        for h in range(ndn):
            ss_ref[h, 0] = s0[h]
        o, s1 = _dn_chunk(ndn, [q_ref[:, sl] for sl in sls], [k_ref[:, sl] for sl in sls], [v_ref[:, sl] for sl in sls],
                          sm_ref[...], al_ref[...], dt_ref[...], s0)
        for h in range(ndn):
            o_ref[:, sls[h]] = o[h]
            s_ref[h] = s1[h]

    blk = lambda cb: pl.BlockSpec((C, half), lambda n: (n, cb))
    row = pl.BlockSpec((1, LANES), lambda n: (0, 0))
    return _call(body, name, (nc,),
                 [blk(0), blk(1), blk(2), pl.BlockSpec((C, LANES), lambda n: (n, small_cb)), row, row],
                 (blk(0), pl.BlockSpec((ndn, 1, HEAD_DIM, HEAD_DIM), lambda n: (0, n, 0, 0))),
                 (SDS((S, half), f32), SDS((ndn, nc, HEAD_DIM, HEAD_DIM), f32)),
                 scratch=[pltpu.VMEM((ndn, HEAD_DIM, HEAD_DIM), f32)], comm=comm)(cqkv, cqkv, cqkv, proj, alog_row, dt_row)


def dn_bwd(cqkv, proj, small_cb, alog_row, dt_row, ssave, do, ndn, name, comm=None):
    S = cqkv.shape[0]
    C = DN_CHUNK
    nc = S // C
    half = ndn * HEAD_DIM

    def body(q_ref, k_ref, v_ref, sm_ref, al_ref, dt_ref, ss_ref, do_ref, dqkv_ref, dsm_ref, dal_ref, ddt_ref, ds_ref):
        @pl.when(pl.program_id(0) == 0)
        def _():
            ds_ref[...] = jnp.zeros_like(ds_ref)
            dal_ref[...] = jnp.zeros_like(dal_ref)
            ddt_ref[...] = jnp.zeros_like(ddt_ref)

        sls = [slice(h * HEAD_DIM, (h + 1) * HEAD_DIM) for h in range(ndn)]
        _, vjp = jax.vjp(functools.partial(_dn_chunk, ndn), [q_ref[:, sl] for sl in sls], [k_ref[:, sl] for sl in sls],
                         [v_ref[:, sl] for sl in sls], sm_ref[...], al_ref[...], dt_ref[...], [ss_ref[h, 0] for h in range(ndn)])
        dq, dk, dv, dsm, dal, ddt, ds0 = vjp(([do_ref[:, sl] for sl in sls], [ds_ref[h] for h in range(ndn)]))
        for h in range(ndn):
            dqkv_ref[:, sls[h]] = dq[h]
            dqkv_ref[:, half + h * HEAD_DIM:half + (h + 1) * HEAD_DIM] = dk[h]
            dqkv_ref[:, 2 * half + h * HEAD_DIM:2 * half + (h + 1) * HEAD_DIM] = dv[h]
            ds_ref[h] = ds0[h]
        dsm_ref[...] = dsm
        dal_ref[...] += dal
        ddt_ref[...] += ddt

    rev = lambda n: nc - 1 - n
    blk = lambda cb: pl.BlockSpec((C, half), lambda n: (rev(n), cb))
    row = pl.BlockSpec((1, LANES), lambda n: (0, 0))
    return _call(body, name, (nc,),
                 [blk(0), blk(1), blk(2), pl.BlockSpec((C, LANES), lambda n: (rev(n), small_cb)), row, row,
                  pl.BlockSpec((ndn, 1, HEAD_DIM, HEAD_DIM), lambda n: (0, rev(n), 0, 0)), blk(0)],
                 (pl.BlockSpec((C, 3 * half), lambda n: (rev(n), 0)), pl.BlockSpec((C, LANES), lambda n: (rev(n), 0)), row, row),
                 (SDS((S, 3 * half), f32), SDS((S, LANES), f32), SDS((1, LANES), f32), SDS((1, LANES), f32)),
                 scratch=[pltpu.VMEM((ndn, HEAD_DIM, HEAD_DIM), f32)], comm=comm)(cqkv, cqkv, cqkv, proj, alog_row, dt_row, ssave, do)


def even_pre(proj, gq, gk, dfx, cb0, name):
    S = proj.shape[0]
    tm = _tile(S, 256)
    nh = dfx // HEAD_DIM

    def body(q_ref, k_ref, v_ref, gq_ref, gk_ref, fq_ref, fk_ref, fv_ref):
        for h in range(nh):
            sl = slice(h * HEAD_DIM, (h + 1) * HEAD_DIM)
            fq_ref[:, sl] = _rms(q_ref[:, sl], gq_ref[...]).astype(bf16)
            fk_ref[:, sl] = _rms(k_ref[:, sl], gk_ref[...]).astype(bf16)
        fv_ref[...] = v_ref[...].astype(bf16)

    sh = SDS((S, dfx), bf16)
    o = _rowspec(tm, dfx)
    return _call(body, name, (S // tm,),
                 [_rowspec(tm, dfx, cb0), _rowspec(tm, dfx, cb0 + 1), _rowspec(tm, dfx, cb0 + 2), _bspec(HEAD_DIM), _bspec(HEAD_DIM)],
                 (o, o, o), (sh, sh, sh))(proj, proj, proj, gq, gk)


def even_pre_bwd(proj, gq, gk, dfq, dfk, dfx, cb0, name):
    S = proj.shape[0]
    tm = _tile(S, 256)
    nh = dfx // HEAD_DIM

    def body(q_ref, k_ref, gq_ref, gk_ref, dfq_ref, dfk_ref, dq_ref, dk_ref, dgq_ref, dgk_ref):
        @pl.when(pl.program_id(0) == 0)
        def _():
            dgq_ref[...] = jnp.zeros_like(dgq_ref)
            dgk_ref[...] = jnp.zeros_like(dgk_ref)

        for h in range(nh):
            sl = slice(h * HEAD_DIM, (h + 1) * HEAD_DIM)
            _, vq = jax.vjp(_rms, q_ref[:, sl], gq_ref[...])
            dq, dgq = vq(dfq_ref[:, sl])
            _, vk = jax.vjp(_rms, k_ref[:, sl], gk_ref[...])
            dk, dgk = vk(dfk_ref[:, sl])
            dq_ref[:, sl] = dq
            dk_ref[:, sl] = dk
            dgq_ref[...] += dgq
            dgk_ref[...] += dgk

    sh = SDS((S, dfx), f32)
    o = _rowspec(tm, dfx)
    g = SDS((1, HEAD_DIM), f32)
    return _call(body, name, (S // tm,),
                 [_rowspec(tm, dfx, cb0), _rowspec(tm, dfx, cb0 + 1), _bspec(HEAD_DIM), _bspec(HEAD_DIM), o, o],
                 (o, o, _bspec(HEAD_DIM), _bspec(HEAD_DIM)), (sh, sh, g, g))(proj, proj, gq, gk, dfq, dfk)


def _dn_out(o, gate, gn):
    return _rms(o, gn) * (gate * jax.nn.sigmoid(gate))


def _fox_out(o, gate):
    return o * jax.nn.sigmoid(gate)


def even_post(o_dn, o_fox, proj, gn, half, cb_dn_gate, cb_fox_gate, name):
    S = proj.shape[0]
    tm = _tile(S, 256)
    nh = half // HEAD_DIM

    def body(od_ref, of_ref, gd_ref, gf_ref, gn_ref, o_ref):
        for h in range(nh):
            sl = slice(h * HEAD_DIM, (h + 1) * HEAD_DIM)
            o_ref[:, sl] = _dn_out(od_ref[:, sl], gd_ref[:, sl], gn_ref[...]).astype(bf16)
        o_ref[:, half:] = _fox_out(of_ref[...], gf_ref[...]).astype(bf16)

    return _call(body, name, (S // tm,),
                 [_rowspec(tm, half), _rowspec(tm, half), _rowspec(tm, half, cb_dn_gate), _rowspec(tm, half, cb_fox_gate), _bspec(HEAD_DIM)],
                 _rowspec(tm, 2 * half), SDS((S, 2 * half), bf16))(o_dn, o_fox, proj, proj, gn)


def even_post_bwd(o_dn, o_fox, proj, gn, dom, half, cb_dn_gate, cb_fox_gate, name):
    S = proj.shape[0]
    tm = _tile(S, 256)
    nh = half // HEAD_DIM

    def body(od_ref, of_ref, gd_ref, gf_ref, gn_ref, dod_ref, dof_ref, dd_ref, df_ref, dgd_ref, dgf_ref, dgn_ref):
        @pl.when(pl.program_id(0) == 0)
        def _():
            dgn_ref[...] = jnp.zeros_like(dgn_ref)

        for h in range(nh):
            sl = slice(h * HEAD_DIM, (h + 1) * HEAD_DIM)
            _, vjp = jax.vjp(_dn_out, od_ref[:, sl], gd_ref[:, sl], gn_ref[...])
            d_o, d_gate, d_gn = vjp(dod_ref[:, sl])
            dd_ref[:, sl] = d_o
            dgd_ref[:, sl] = d_gate
            dgn_ref[...] += d_gn
        _, vjp = jax.vjp(_fox_out, of_ref[...], gf_ref[...])
        d_o, d_gate = vjp(dof_ref[...])
        df_ref[...] = d_o
        dgf_ref[...] = d_gate

    sh = SDS((S, half), f32)
    o = _rowspec(tm, half)
    return _call(body, name, (S // tm,),
                 [o, o, _rowspec(tm, half, cb_dn_gate), _rowspec(tm, half, cb_fox_gate), _bspec(HEAD_DIM),
                  _rowspec(tm, half, 0), _rowspec(tm, half, 1)],
                 (o, o, o, o, _bspec(HEAD_DIM)), (sh, sh, sh, sh, SDS((1, HEAD_DIM), f32)))(o_dn, o_fox, proj, proj, gn, dom, dom)


def _pad_row(v, lane0=0):
    return jnp.pad(v, (lane0, LANES - lane0 - v.shape[0]))[None]


def _carried(res, comm):
    return res if comm is not None else (res, [])


def even_mixer_fwd(x, gmix, w_in, w_out, conv_w, a_log, dt_bias, gn, gq, gk, f_bias, tag, comm_fox=None, comm_dn=None,
                   comm_in=None):
    S, D = x.shape
    half = D // 2
    ndn = half // HEAD_DIM
    small_cb = 4 * D // LANES
    alog_row, dt_row, fb_row = _pad_row(a_log), _pad_row(dt_bias), _pad_row(f_bias, 2 * ndn)
    h = rms_fwd(x, gmix, f"{tag}_rms")
    proj, got_in = _carried(mm(h, w_in, "nn", f32, f"{tag}_in", tm=512, tn=1408, comm=comm_in), comm_in)
    cqkv = conv_fwd(proj, conv_w, 3 * half, f"{tag}_conv")
    (o_dn, ssave), got_dn = _carried(dn_fwd(cqkv, proj, small_cb, alog_row, dt_row, ndn, f"{tag}_dn", comm=comm_dn), comm_dn)
    c, cT = gates_fwd(proj, small_cb, fb_row, f"{tag}_gates")
    fq, fk, fv = even_pre(proj, gq, gk, half, 4, f"{tag}_pre")
    (o_fox, lse), got_fox = _carried(fox_fwd(fq, fk, fv, c, cT, ndn, 2 * ndn, f"{tag}_fox", comm=comm_fox), comm_fox)
    om = even_post(o_dn, o_fox, proj, gn, half, 3, 7, f"{tag}_post")
    xo = mm(om, w_out, "nn", f32, f"{tag}_out", res=x)
    return xo, (h, proj, cqkv, o_dn, ssave, c, cT, fq, fk, fv, o_fox, lse, om, alog_row, dt_row, fb_row), got_fox, got_dn, got_in


def even_mixer_bwd(x, gmix, w_in, w_out, conv_w, gn, gq, gk, saved, dy, tag, comm_fox=None, comm_dn=None):
    S, D = x.shape
    half = D // 2
    ndn = half // HEAD_DIM
    small_cb = 4 * D // LANES
    h, proj, cqkv, o_dn, ssave, c, cT, fq, fk, fv, o_fox, lse, om, alog_row, dt_row, fb_row = saved
    dom = mm(dy, w_out, "nt", f32, f"{tag}_bdom")
    dw_out = mm(om, dy, "tn", bf16, f"{tag}_bwout")
    d_odn, d_ofox, d_dgate, d_fgate, d_gn = even_post_bwd(o_dn, o_fox, proj, gn, dom, half, 3, 7, f"{tag}_bpost")
    (dfq, dfk, dfv, dcT), got_fox = _carried(
        fox_bwd(fq, fk, fv, c, cT, o_fox, lse, d_ofox, ndn, 2 * ndn, f"{tag}_bfox", comm=comm_fox), comm_fox)
    dq_pre, dk_pre, d_gq, d_gk = even_pre_bwd(proj, gq, gk, dfq, dfk, half, 4, f"{tag}_bpre")
    (dcqkv, dsm_dn, d_alog, d_dt), got_dn = _carried(
        dn_bwd(cqkv, proj, small_cb, alog_row, dt_row, ssave, d_odn, ndn, f"{tag}_bdn", comm=comm_dn), comm_dn)
    d_conv_in, d_conv_w = conv_bwd(proj, conv_w, dcqkv, f"{tag}_bconv")
    d_small, d_fb = gates_bwd(proj, small_cb, fb_row, dcT, dsm_dn, 2 * ndn, ndn, f"{tag}_bgates")
    dproj = jnp.concatenate([d_conv_in, d_dgate, dq_pre, dk_pre, dfv, d_fgate, d_small], axis=1).astype(bf16)
    dw_in = mm(h, dproj, "tn", bf16, f"{tag}_bwin", tn=1408)
    dx, d_gmix = mm_bdh_rms(dproj, w_in, x, gmix, dy, f"{tag}_bdh")
    small = dict(norm_mix=d_gmix[0], dn_conv_w=d_conv_w, dn_a_log=d_alog[0, :ndn], dn_dt_bias=d_dt[0, :ndn],
                 dn_norm_g=d_gn[0], fox_q_norm_g=d_gq[0], fox_k_norm_g=d_gk[0], fox_f_bias=d_fb[0, 2 * ndn:3 * ndn])
    return dx, dw_in, dw_out, small, got_fox, got_dn


def odd_mixer_fwd(x, gmix, w_in, w_out, tag, comm=None):
    S, D = x.shape
    nh = D // HEAD_DIM
    h = rms_fwd(x, gmix, f"{tag}_rms")
    qkv = mm(h, w_in, "nn", bf16, f"{tag}_in", tn=768)
    (o, tails), got = _carried(sb_fwd(qkv, nh, f"{tag}_sb", comm=comm), comm)
    xo = mm(o, w_out, "nn", f32, f"{tag}_out", res=x)
    return xo, (h, qkv, o, tails), got


def odd_mixer_bwd(x, gmix, w_in, w_out, saved, dy, tag, comm=None):
    S, D = x.shape
    nh = D // HEAD_DIM
    h, qkv, o, tails = saved
    do = mm(dy, w_out, "nt", f32, f"{tag}_bdo")
    dw_out = mm(o, dy, "tn", bf16, f"{tag}_bwout")
    (dq, dk, dv), got = _carried(sb_bwd(qkv, tails, do, nh, f"{tag}_bsb", comm=comm), comm)
    dqkv = jnp.concatenate([dq, dk, dv], axis=1).astype(bf16)
    dw_in = mm(h, dqkv, "tn", bf16, f"{tag}_bwin", tn=1536)
    dx, d_gmix = mm_bdh_rms(dqkv, w_in, x, gmix, dy, f"{tag}_bdh")
    return dx, dw_in, dw_out, dict(norm_mix=d_gmix[0]), got


def all_gather(shards, axes, name, kind="ag"):
    return _call(None, name, (), [], [], [], comm=Comm(kind, shards, axes))()[1]


def scatter_parts(fulls, axes, name):
    return _call(None, name, (), [], [], [], comm=Comm("rs", fulls, axes))()[1]


def sum_parts(parts, name):
    _, R, C = parts.shape
    tm = _tile(R, 256)

    def body(p_ref, o_ref):
        acc = p_ref[0].astype(f32)
        for k in range(1, NDEV):
            acc = acc + p_ref[k].astype(f32)
        o_ref[...] = acc

    return _call(body, name, (R // tm,), [pl.BlockSpec((NDEV, tm, C), lambda i: (0, i, 0))], _rowspec(tm, C), SDS((R, C), f32))(parts)


def adamw(w, g, m, v, name):
    R, C = w.shape
    tm = _tile(R, max(8, min(512, (ADAMW_TILE_ELEMS // C) // 8 * 8)))
    c1 = 1.0 - ADAM_B1 ** ADAM_STEP
    c2 = 1.0 - ADAM_B2 ** ADAM_STEP

    def body(w_ref, g_ref, m_ref, v_ref, d_ref, nm_ref, nv_ref):
        g_ = g_ref[...]
        nm = ADAM_B1 * m_ref[...] + (1.0 - ADAM_B1) * g_
        nv = ADAM_B2 * v_ref[...] + (1.0 - ADAM_B2) * (g_ * g_)
        d_ref[...] = -ADAM_LR * ((nm / c1) / (jnp.sqrt(nv / c2) + ADAM_EPS) + ADAM_WD * w_ref[...])
        nm_ref[...] = nm
        nv_ref[...] = nv

    spec = _rowspec(tm, C)
    sh = SDS((R, C), f32)
    return _call(body, name, (R // tm,), [spec] * 4, (spec, spec, spec), (sh, sh, sh))(w, g, m, v)


def _pad_to(n, mult):
    return -(-n // mult) * mult


def _even_perm(D):
    half = D // 2
    ndn = half // HEAD_DIM
    o_gate = 3 * half
    o_b = o_gate + half
    o_a = o_b + ndn
    o_fq = o_a + ndn
    o_fpre = o_fq + 4 * half
    return half, ndn, o_b, o_a, o_fq, o_fpre


def _even_to_mine(w):
    D = (w.shape[-1] // 4) // LANES * LANES
    half, ndn, o_b, o_a, o_fq, o_fpre = _even_perm(D)
    small = jnp.concatenate([w[..., o_b:o_b + ndn], w[..., o_a:o_a + ndn], w[..., o_fpre:o_fpre + ndn]], axis=-1)
    small = jnp.pad(small, [(0, 0)] * (w.ndim - 1) + [(0, LANES - 3 * ndn)])
    return jnp.concatenate([w[..., :o_b], w[..., o_fq:o_fpre], small], axis=-1)


def _even_from_mine(w, D):
    half, ndn, o_b, o_a, o_fq, o_fpre = _even_perm(D)
    sm = w[..., 4 * D:]
    return jnp.concatenate([w[..., :o_b], sm[..., :ndn], sm[..., ndn:2 * ndn], w[..., o_b:4 * D], sm[..., 2 * ndn:3 * ndn]], axis=-1)


def _pack_small(parts):
    flat = jnp.concatenate([p.reshape(-1).astype(f32) for p in parts])
    n = flat.shape[0]
    return jnp.pad(flat, (0, _pad_to(n, 8 * LANES) - n)).reshape(-1, LANES)


def _unpack_small(packed, like):
    flat = packed.reshape(-1)
    out, off = [], 0
    for p in like:
        out.append(flat[off:off + p.size].reshape(p.shape))
        off += p.size
    return out


SMALL_NAMES = ("norm_ffn1", "norm_mix", "dn_a_log", "dn_dt_bias", "dn_norm_g", "fox_q_norm_g", "fox_k_norm_g", "fox_f_bias", "norm_ffn2")
BIG_NAMES = ("ffn1_w_gu", "ffn1_w_down", "w_in_even", "dn_conv_w", "w_out_even", "w_in_odd", "w_out_odd", "ffn2_w_gu", "ffn2_w_down")
WEIGHT_NAMES = ("norm_ffn1", "ffn1_w_gu", "ffn1_w_down", "norm_mix", "w_in_even", "dn_conv_w", "dn_a_log", "dn_dt_bias", "dn_norm_g",
                "fox_q_norm_g", "fox_k_norm_g", "fox_f_bias", "w_out_even", "w_in_odd", "w_out_odd", "norm_ffn2", "ffn2_w_gu", "ffn2_w_down")


def kernel(x, norm_ffn1, ffn1_w_gu, ffn1_w_down, norm_mix, w_in_even, dn_conv_w, dn_a_log, dn_dt_bias, dn_norm_g, fox_q_norm_g, fox_k_norm_g, fox_f_bias, w_out_even, w_in_odd, w_out_odd, norm_ffn2, ffn2_w_gu, ffn2_w_down, loss_target, m_norm_ffn1, m_ffn1_w_gu, m_ffn1_w_down, m_norm_mix, m_w_in_even, m_dn_conv_w, m_dn_a_log, m_dn_dt_bias, m_dn_norm_g, m_fox_q_norm_g, m_fox_k_norm_g, m_fox_f_bias, m_w_out_even, m_w_in_odd, m_w_out_odd, m_norm_ffn2, m_ffn2_w_gu, m_ffn2_w_down, v_norm_ffn1, v_ffn1_w_gu, v_ffn1_w_down, v_norm_mix, v_w_in_even, v_dn_conv_w, v_dn_a_log, v_dn_dt_bias, v_dn_norm_g, v_fox_q_norm_g, v_fox_k_norm_g, v_fox_f_bias, v_w_out_even, v_w_in_odd, v_w_out_odd, v_norm_ffn2, v_ffn2_w_gu, v_ffn2_w_down):
    args = dict(locals())
    W = {n: args[n] for n in WEIGHT_NAMES}
    M = {n: args["m_" + n] for n in WEIGHT_NAMES}
    V = {n: args["v_" + n] for n in WEIGHT_NAMES}
    xs = x[0]
    tgt = loss_target[0]
    S, D = xs.shape
    L = norm_ffn1.shape[0]
    n_even = w_in_even.shape[0]
    hs = ffn1_w_down.shape[1]
    hp = _pad_to(hs, LANES)
    me = 4 * lax.axis_index("x") + 2 * lax.axis_index("y") + lax.axis_index("c")

    def prep_gu(w):
        w = w.reshape(L, D, 2, hs)
        return jnp.pad(w, ((0, 0), (0, 0), (0, 0), (0, hp - hs))).reshape(L, D, 2 * hp).astype(bf16)

    def prep_down(w):
        return jnp.pad(w, ((0, 0), (0, hp - hs), (0, 0))).astype(bf16)

    sh_gu1, sh_d1, sh_gu2, sh_d2 = prep_gu(ffn1_w_gu), prep_down(ffn1_w_down), prep_gu(ffn2_w_gu), prep_down(ffn2_w_down)
    sh_ie, sh_oe = _even_to_mine(w_in_even).astype(bf16), w_out_even.astype(bf16)
    sh_io, sh_oo = w_in_odd.astype(bf16), w_out_odd.astype(bf16)

    def layer_shards(l):
        j = l // 2
        mix = [sh_ie[j], sh_oe[j]] if l % 2 == 0 else [sh_io[j], sh_oo[j]]
        return [sh_gu1[l], sh_d1[l], *mix, sh_gu2[l], sh_d2[l]]

    def layer_axes(l):
        return [1, 0, 0 if l % 2 == 0 else 1, 0, 1, 0]

    def layer_names(l):
        return ["ffn1_w_gu", "ffn1_w_down", *(["w_in_even", "w_out_even"] if l % 2 == 0 else ["w_in_odd", "w_out_odd"]),
                "ffn2_w_gu", "ffn2_w_down"]

    FOX_CARRIES, DN_CARRIES = (0, 4, 3), (1, 2, 5)

    def split_comm(kind, arrays, axes):
        return (Comm(kind, [arrays[i] for i in FOX_CARRIES], [axes[i] for i in FOX_CARRIES]),
                Comm(kind, [arrays[i] for i in DN_CARRIES], [axes[i] for i in DN_CARRIES]))

    def join_split(got_fox, got_dn):
        out = [None] * 6
        for i, a in zip(FOX_CARRIES, got_fox):
            out[i] = a
        for i, a in zip(DN_CARRIES, got_dn):
            out[i] = a
        return out

    first = all_gather(layer_shards(0) + [dn_conv_w[None]], layer_axes(0) + [0], "ag_layer0", kind="ag2")
    weights = {0: first[:6]}
    convw = jnp.moveaxis(first[6], 0, 2).reshape(n_even, CONV_WIDTH, -1)

    EVEN_FWD_CARRIES = dict(f1_gu=(), mx_in=(), dn=(1, 2, 5), fox=(0, 4, 3), f2_gu=(), f2_down=())
    saved = []
    cur = xs
    for l in range(L):
        j = l // 2
        wgu1, wd1, win, wout, wgu2, wd2 = weights[l]
        nxt = l + 1 < L
        x0 = cur
        if l % 2 == 0:
            sh_n, ax_n = (layer_shards(l + 1), layer_axes(l + 1)) if nxt else (None, None)
            cm = {k: (Comm("ag2", [sh_n[i] for i in idx], [ax_n[i] for i in idx]) if nxt and idx else None)
                  for k, idx in EVEN_FWD_CARRIES.items()}
            x1, s1, got_f1gu, _ = ffn_fwd(x0, norm_ffn1[l:l + 1], wgu1, wd1, f"l{l}f1", comm_gu=cm["f1_gu"])
            x2, sm, got_f, got_d, got_in = even_mixer_fwd(
                x1, norm_mix[l:l + 1], win, wout, convw[j], dn_a_log[j], dn_dt_bias[j], dn_norm_g[j:j + 1],
                fox_q_norm_g[j:j + 1], fox_k_norm_g[j:j + 1], fox_f_bias[j], f"l{l}mx", comm_fox=cm["fox"], comm_dn=cm["dn"],
                comm_in=cm["mx_in"])
            x3, s2, got_f2gu, got_f2down = ffn_fwd(x2, norm_ffn2[l:l + 1], wgu2, wd2, f"l{l}f2", comm_gu=cm["f2_gu"],
                                                   comm_down=cm["f2_down"])
            if nxt:
                got = dict(f1_gu=got_f1gu, mx_in=got_in, dn=got_d, fox=got_f, f2_gu=got_f2gu, f2_down=got_f2down)
                weights[l + 1] = [None] * 6
                for k, idx in EVEN_FWD_CARRIES.items():
                    for i, a in zip(idx, got[k]):
                        weights[l + 1][i] = a
        else:
            x1, s1, _, _ = ffn_fwd(x0, norm_ffn1[l:l + 1], wgu1, wd1, f"l{l}f1")
            cm = Comm("ag2", layer_shards(l + 1), layer_axes(l + 1)) if nxt else None
            x2, sm, got = odd_mixer_fwd(x1, norm_mix[l:l + 1], win, wout, f"l{l}mx", comm=cm)
            if nxt:
                weights[l + 1] = got
            x3, s2, _, _ = ffn_fwd(x2, norm_ffn2[l:l + 1], wgu2, wd2, f"l{l}f2")
        saved.append((x0, x1, x2, s1, sm, s2))
        cur = x3
    dy, loss_row = loss_head(cur, tgt, "loss")

    gsmall = {n: [None] * W[n].shape[0] for n in SMALL_NAMES}
    gconv = [None] * n_even
    parts = {n: [None] * W[n].shape[0] for n in BIG_NAMES if n != "dn_conv_w"}

    def take(l, recv):
        for nm, p in zip(layer_names(l), recv):
            idx = l if nm.startswith("ffn") else l // 2
            parts[nm][idx] = sum_parts(p.reshape(NDEV, -1, p.shape[-1]), f"sum_{nm}_{idx}").reshape(p.shape[1:])

    pending = None
    for l in reversed(range(L)):
        j = l // 2
        wgu1, wd1, win, wout, wgu2, wd2 = weights[l]
        x0, x1, x2, s1, sm, s2 = saved[l]
        dy, dg, dwgu2, dwd2 = ffn_bwd(x2, norm_ffn2[l:l + 1], wgu2, wd2, s2, dy, f"l{l}f2")
        gsmall["norm_ffn2"][l] = dg[0]
        if l % 2 == 0:
            cf, cd = split_comm("rs", pending, layer_axes(l + 1)) if pending is not None else (None, None)
            dy, dwin, dwout, sg, got_f, got_d = even_mixer_bwd(
                x1, norm_mix[l:l + 1], win, wout, convw[j], dn_norm_g[j:j + 1], fox_q_norm_g[j:j + 1],
                fox_k_norm_g[j:j + 1], sm, dy, f"l{l}mx", comm_fox=cf, comm_dn=cd)
            if pending is not None:
                take(l + 1, join_split(got_f, got_d))
            gconv[j] = sg.pop("dn_conv_w")
            for k_, v_ in sg.items():
                gsmall[k_][l if k_ == "norm_mix" else j] = v_
        else:
            cm = Comm("rs", pending, layer_axes(l + 1)) if pending is not None else None
            dy, dwin, dwout, sg, got = odd_mixer_bwd(x1, norm_mix[l:l + 1], win, wout, sm, dy, f"l{l}mx", comm=cm)
            if pending is not None:
                take(l + 1, got)
            gsmall["norm_mix"][l] = sg["norm_mix"]
        dy, dg, dwgu1, dwd1 = ffn_bwd(x0, norm_ffn1[l:l + 1], wgu1, wd1, s1, dy, f"l{l}f1")
        gsmall["norm_ffn1"][l] = dg[0]
        pending = [dwgu1, dwd1, dwin, dwout, dwgu2, dwd2]
    take(0, scatter_parts(pending, layer_axes(0), "rs_layer0"))
    grad_x = dy[None]

    small_list = [jnp.stack(gsmall[n]) for n in SMALL_NAMES] + [jnp.stack(gconv), loss_row[0, :1]]
    packed = _pack_small(small_list)
    gathered = all_gather([packed], [0], "ag_small")[0]
    summed = sum_parts(gathered.reshape(NDEV, packed.shape[0], LANES), "sum_small")
    small_sum = _unpack_small(summed, small_list)
    G = dict(zip(SMALL_NAMES, small_sum[:len(SMALL_NAMES)]))
    conv_full = small_sum[len(SMALL_NAMES)]
    cwid = dn_conv_w.shape[2]
    G["dn_conv_w"] = lax.dynamic_slice_in_dim(conv_full, me * cwid, cwid, axis=2)
    loss = small_sum[-1][0]

    def unprep_gu(g):
        return g.reshape(L, D, 2, hp)[..., :hs].reshape(L, D, 2 * hs)

    G["ffn1_w_gu"] = unprep_gu(jnp.stack(parts["ffn1_w_gu"]))
    G["ffn2_w_gu"] = unprep_gu(jnp.stack(parts["ffn2_w_gu"]))
    G["ffn1_w_down"] = jnp.stack(parts["ffn1_w_down"])[:, :hs]
    G["ffn2_w_down"] = jnp.stack(parts["ffn2_w_down"])[:, :hs]
    G["w_in_even"] = _even_from_mine(jnp.stack(parts["w_in_even"]), D)
    G["w_out_even"] = jnp.stack(parts["w_out_even"])
    G["w_in_odd"] = jnp.stack(parts["w_in_odd"])
    G["w_out_odd"] = jnp.stack(parts["w_out_odd"])

    delta, new_m, new_v = {}, {}, {}
    for n in BIG_NAMES:
        shp = W[n].shape
        two = lambda a: a.reshape(-1, shp[-1])
        d_, m_, v_ = adamw(two(W[n]), two(G[n]), two(M[n]), two(V[n]), f"adamw_{n}")
        delta[n], new_m[n], new_v[n] = d_.reshape(shp), m_.reshape(shp), v_.reshape(shp)
    sw = [W[n] for n in SMALL_NAMES]
    d_, m_, v_ = adamw(_pack_small(sw), _pack_small([G[n] for n in SMALL_NAMES]), _pack_small([M[n] for n in SMALL_NAMES]),
                       _pack_small([V[n] for n in SMALL_NAMES]), "adamw_small")
    for n, a, b, c_ in zip(SMALL_NAMES, _unpack_small(d_, sw), _unpack_small(m_, sw), _unpack_small(v_, sw)):
        delta[n], new_m[n], new_v[n] = a, b, c_

    return (loss, grad_x, *[G[n] for n in WEIGHT_NAMES], *[delta[n] for n in WEIGHT_NAMES],
            *[new_m[n] for n in WEIGHT_NAMES], *[new_v[n] for n in WEIGHT_NAMES])
```

```python
import functools
import math

import jax
import jax.numpy as jnp
from jax import lax
from jax.experimental import pallas as pl
from jax.experimental.pallas import tpu as pltpu

f32 = jnp.float32
bf16 = jnp.bfloat16
SDS = jax.ShapeDtypeStruct

HEAD_DIM = 128
LANES = 128
CONV_WIDTH = 4
DN_CHUNK = 128
EPS = 1e-6
NDEV = 8
VMEM_LIMIT_BYTES = 56 * 1024 * 1024
HI = lax.Precision.HIGHEST
ADAMW_TILE_ELEMS = 384 * 1024

ADAM_LR = 0.001
ADAM_B1 = 0.9
ADAM_B2 = 0.999
ADAM_EPS = 1e-08
ADAM_WD = 0.01
ADAM_STEP = 10

NN = (((1,), (0,)), ((), ()))
NT = (((1,), (1,)), ((), ()))
TN = (((0,), (0,)), ((), ()))


def _me_and_peers():
    x, y, c = lax.axis_index("x"), lax.axis_index("y"), lax.axis_index("c")
    me = 4 * x + 2 * y + c
    peers = []
    for r in range(1, NDEV):
        px = 1 - x if (r >> 2) & 1 else x
        py = 1 - y if (r >> 1) & 1 else y
        pc = 1 - c if r & 1 else c
        peers.append(((px, py, pc), 4 * px + 2 * py + pc))
    return me, peers


def _slab(ref, axis, width, k):
    idx = [slice(None)] * len(ref.shape)
    idx[axis] = pl.ds(k * width, width)
    return ref.at[tuple(idx)]


class Comm:
    def __init__(self, kind, arrays, axes):
        self.kind, self.arrays, self.axes, self.n = kind, list(arrays), list(axes), len(arrays)

    def out_shape(self):
        out = []
        for a, ax in zip(self.arrays, self.axes):
            shp = list(a.shape)
            if self.kind != "rs":
                shp[ax] *= NDEV
                out.append(SDS(tuple(shp), a.dtype))
            else:
                shp[ax] //= NDEV
                out.append(SDS((NDEV, *shp), a.dtype))
        return out

    def sems(self):
        return [pltpu.SemaphoreType.DMA((self.n, NDEV - 1)), pltpu.SemaphoreType.DMA((self.n, NDEV - 1)),
                pltpu.SemaphoreType.DMA((self.n,))]

    def _src(self, ins, t, to_idx):
        if self.kind == "ag":
            return ins[t]
        return _slab(ins[t], self.axes[t], ins[t].shape[self.axes[t]] // NDEV, to_idx)

    def _dst(self, ins, outs, t, from_idx):
        if self.kind != "rs":
            return _slab(outs[t], self.axes[t], ins[t].shape[self.axes[t]], from_idx)
        return outs[t].at[from_idx]

    def _copies(self, ins, outs, sems, sending):
        send_sems, recv_sems, loc_sems = sems
        me, peers = _me_and_peers()
        local, remote = [], []
        for t in range(self.n):
            local.append(pltpu.make_async_copy(self._src(ins, t, me), self._dst(ins, outs, t, me), loc_sems.at[t]))
            for r, (peer, pidx) in enumerate(peers):
                remote.append(pltpu.make_async_remote_copy(
                    src_ref=self._src(ins, t, pidx), dst_ref=self._dst(ins, outs, t, me if sending else pidx),
                    send_sem=send_sems.at[t, r], recv_sem=recv_sems.at[t, r], device_id=peer,
                    device_id_type=pl.DeviceIdType.MESH))
        return local, remote

    def _two_level(self, ins, outs, sems):
        send_sems, recv_sems, loc_sems = sems
        x, y, c = lax.axis_index("x"), lax.axis_index("y"), lax.axis_index("c")
        me, sibling = (x, y, c), (x, y, 1 - c)
        chips = [(1 - x, y), (x, 1 - y), (1 - x, 1 - y)]
        dev = lambda p: 4 * p[0] + 2 * p[1] + p[2]

        def copy(t, k, block, to, from_shard):
            dst = self._dst(ins, outs, t, dev(block))
            return pltpu.make_async_remote_copy(
                src_ref=ins[t] if from_shard else dst, dst_ref=dst, send_sem=send_sems.at[t, k], recv_sem=recv_sems.at[t, k],
                device_id=to, device_id_type=pl.DeviceIdType.MESH)

        ts = range(self.n)
        local = [pltpu.make_async_copy(ins[t], self._dst(ins, outs, t, dev(me)), loc_sems.at[t]) for t in ts]
        first = [copy(t, 0, me, sibling, True) for t in ts]
        first += [copy(t, 1 + j, me, (*chip, c), True) for t in ts for j, chip in enumerate(chips)]
        return local, first, copy, chips, me, sibling, c

    def start(self, ins, outs, sems):
        if self.kind == "ag2":
            local, first = self._two_level(ins, outs, sems)[:2]
            for cp in local + first:
                cp.start()
            return
        local, remote = self._copies(ins, outs, sems, True)
        for cp in local + remote:
            cp.start()

    def wait(self, ins, outs, sems):
        if self.kind == "ag2":
            local, first, copy, chips, me, sibling, c = self._two_level(ins, outs, sems)
            passed = []
            for t in range(self.n):
                for j, chip in enumerate(chips):
                    copy(t, 1 + j, (*chip, c), me, False).wait_recv()
                    fwd = copy(t, 4 + j, (*chip, c), sibling, False)
                    fwd.start()
                    passed.append(fwd)
            for t in range(self.n):
                copy(t, 0, sibling, me, False).wait_recv()
                for j, chip in enumerate(chips):
                    copy(t, 4 + j, (*chip, 1 - c), me, False).wait_recv()
            for cp in first + passed:
                cp.wait_send()
            for cp in local:
                cp.wait()
            return
        local, remote = self._copies(ins, outs, sems, False)
        for cp in remote:
            cp.wait_recv()
            cp.wait_send()
        for cp in local:
            cp.wait()


def _call(body, name, grid, in_specs, out_specs, out_shape, scratch=(), comm=None):
    params = pltpu.CompilerParams(vmem_limit_bytes=VMEM_LIMIT_BYTES)
    if comm is None:
        return pl.pallas_call(body, name=name, grid=grid, in_specs=in_specs, out_specs=out_specs, out_shape=out_shape,
                              scratch_shapes=list(scratch), compiler_params=params)
    single = not isinstance(out_shape, (tuple, list))
    c_out_specs = [out_specs] if single else list(out_specs)
    c_out_shape = [out_shape] if single else list(out_shape)
    n_in, n_out, n_scr, n = len(in_specs), len(c_out_shape), len(scratch), comm.n
    anyspec = pl.BlockSpec(memory_space=pl.ANY)

    def wrapped(*refs):
        ins, refs = refs[:n_in], refs[n_in:]
        cins, refs = refs[:n], refs[n:]
        outs, refs = refs[:n_out], refs[n_out:]
        couts, refs = refs[:n], refs[n:]
        scr, sems = refs[:n_scr], refs[n_scr:]
        first = functools.reduce(lambda a, b: a & b, [pl.program_id(d) == 0 for d in range(len(grid))], True)
        last = functools.reduce(lambda a, b: a & b, [pl.program_id(d) == grid[d] - 1 for d in range(len(grid))], True)
        if grid:
            pl.when(first)(lambda: comm.start(cins, couts, sems))
            body(*ins, *outs, *scr)
            pl.when(last)(lambda: comm.wait(cins, couts, sems))
        else:
            comm.start(cins, couts, sems)
            if body is not None:
                body(*ins, *outs, *scr)
            comm.wait(cins, couts, sems)

    call = pl.pallas_call(
        wrapped, name=name, grid=grid, in_specs=list(in_specs) + [anyspec] * n, out_specs=c_out_specs + [anyspec] * n,
        out_shape=c_out_shape + comm.out_shape(), scratch_shapes=list(scratch) + comm.sems(), compiler_params=params)

    def run(*args):
        res = call(*args, *comm.arrays)
        mine = res[:n_out]
        return (mine[0] if single else tuple(mine)), list(res[n_out:])

    return run


def _tile(n, want, align=8):
    if n <= want:
        return n
    for t in range(want // align * align, 0, -align):
        if n % t == 0:
            return t
    return n


def _dot(a, b, dims=NN, precision=None):
    return lax.dot_general(a, b, dims, preferred_element_type=f32, precision=precision)


def _logsig(x):
    return jnp.minimum(x, 0.0) - jnp.log(1.0 + jnp.exp(-jnp.abs(x)))


def _softplus(x):
    return jnp.maximum(x, 0.0) + jnp.log(1.0 + jnp.exp(-jnp.abs(x)))


def _rms(x, g):
    return x * lax.rsqrt(jnp.mean(x * x, axis=-1, keepdims=True) + EPS) * g


def _lane_pick(blk, lane_idx):
    lane = lax.broadcasted_iota(jnp.int32, (1, LANES), 1)
    return jnp.sum(jnp.where(lane == lane_idx, blk, 0.0), axis=1, keepdims=True)


def mm(a, b, mode, out_dtype, name, tm=512, tn=512, res=None, scale=1.0, comm=None):
    if mode == "nn":
        (M, K), (_, N) = a.shape, b.shape
    elif mode == "nt":
        (M, K), (N, _) = a.shape, b.shape
    else:
        (K, M), (_, N) = a.shape, b.shape
    tm, tn = _tile(M, tm, LANES), _tile(N, tn, LANES)
    a_spec = pl.BlockSpec((K, tm), lambda j, i: (0, i)) if mode == "tn" else pl.BlockSpec((tm, K), lambda j, i: (i, 0))
    b_spec = pl.BlockSpec((tn, K), lambda j, i: (j, 0)) if mode == "nt" else pl.BlockSpec((K, tn), lambda j, i: (0, j))
    dims = {"nn": NN, "nt": NT, "tn": TN}[mode]
    o_spec = pl.BlockSpec((tm, tn), lambda j, i: (i, j))

    def body(*refs):
        acc = _dot(refs[0][...].astype(bf16), refs[1][...].astype(bf16), dims)
        if scale != 1.0:
            acc = acc * scale
        if res is not None:
            acc = acc + refs[2][...]
        refs[-1][...] = acc.astype(out_dtype)

    ins, specs = [a, b], [a_spec, b_spec]
    if res is not None:
        ins.append(res)
        specs.append(o_spec)
    return _call(body, name, (N // tn, M // tm), specs, o_spec, SDS((M, N), out_dtype), comm=comm)(*ins)


def _rowspec(tm, w, cb=0):
    return pl.BlockSpec((tm, w), lambda i: (i, cb))


def _bspec(w):
    return pl.BlockSpec((1, w), lambda i: (0, 0))


def rms_fwd(x, g, name):
    S, D = x.shape
    tm = _tile(S, 512)

    def body(x_ref, g_ref, h_ref):
        h_ref[...] = _rms(x_ref[...], g_ref[...]).astype(bf16)

    return _call(body, name, (S // tm,), [_rowspec(tm, D), _bspec(D)], _rowspec(tm, D), SDS((S, D), bf16))(x, g)


def _swiglu(g, u):
    return g * jax.nn.sigmoid(g) * u


def ffn_gu_act(h, wgu, name, tm=1024, tn=768, comm=None):
    S, D = h.shape
    W = wgu.shape[1] // 2
    tm, tn = _tile(S, tm, LANES), _tile(W, tn, LANES)
    nb = W // tn

    def body(h_ref, wg_ref, wu_ref, gu_ref, a_ref):
        hb = h_ref[...]
        g = _dot(hb, wg_ref[...])
        u = _dot(hb, wu_ref[...])
        gu_ref[0] = g.astype(bf16)
        gu_ref[1] = u.astype(bf16)
        a_ref[...] = _swiglu(g, u).astype(bf16)

    return _call(body, name, (nb, S // tm),
                 [pl.BlockSpec((tm, D), lambda j, i: (i, 0)), pl.BlockSpec((D, tn), lambda j, i: (0, j)),
                  pl.BlockSpec((D, tn), lambda j, i: (0, nb + j))],
                 (pl.BlockSpec((2, tm, tn), lambda j, i: (0, i, j)), pl.BlockSpec((tm, tn), lambda j, i: (i, j))),
                 (SDS((2, S, W), bf16), SDS((S, W), bf16)), comm=comm)(h, wgu, wgu)


def ffn_bda_act(dy, wd, gu, scale, name, tm=512, tn=768):
    S, D = dy.shape
    W = wd.shape[0]
    tm, tn = _tile(S, tm, LANES), _tile(W, tn, LANES)

    def body(dy_ref, wd_ref, gu_ref, dgu_ref):
        da = _dot(dy_ref[...].astype(bf16), wd_ref[...], NT) * scale
        _, vjp = jax.vjp(_swiglu, gu_ref[0].astype(f32), gu_ref[1].astype(f32))
        dg, du = vjp(da)
        dgu_ref[0] = dg.astype(bf16)
        dgu_ref[1] = du.astype(bf16)

    planes = pl.BlockSpec((2, tm, tn), lambda j, i: (0, i, j))
    return _call(body, name, (W // tn, S // tm),
                 [pl.BlockSpec((tm, D), lambda j, i: (i, 0)), pl.BlockSpec((tn, D), lambda j, i: (j, 0)), planes],
                 planes, SDS((2, S, W), bf16))(dy, wd, gu)


def ffn_bwgu(h, dgu, name, tm=512, tn=768):
    S, D = h.shape
    W = dgu.shape[2]
    tm, tn = _tile(D, tm, LANES), _tile(W, tn, LANES)
    nb = W // tn

    def body(h_ref, d_ref, o_ref):
        o_ref[...] = _dot(h_ref[...], d_ref[...], TN).astype(bf16)

    return _call(body, name, (2 * nb, D // tm),
                 [pl.BlockSpec((S, tm), lambda j, i: (0, i)), pl.BlockSpec((None, S, tn), lambda j, i: (j // nb, 0, j % nb))],
                 pl.BlockSpec((tm, tn), lambda j, i: (i, j)), SDS((D, 2 * W), bf16))(h, dgu)


def nt_rms_bwd(pairs, x, g, dres, name, tm=256):
    S, D = x.shape
    tm = _tile(S, tm)
    n = len(pairs)

    def body(*refs):
        x_ref, g_ref, dres_ref = refs[2 * n:2 * n + 3]
        dx_ref, dg_ref = refs[2 * n + 3:]
        dh = sum(_dot(refs[2 * k][...].astype(bf16), refs[2 * k + 1][...], NT) for k in range(n))
        _, vjp = jax.vjp(_rms, x_ref[...], g_ref[...])
        dx, dg = vjp(dh)
        dx_ref[...] = dres_ref[...] + dx

        @pl.when(pl.program_id(0) == 0)
        def _():
            dg_ref[...] = jnp.zeros_like(dg_ref)

        dg_ref[...] += dg

    arrays, specs = [], []
    for a, a_spec, b, b_spec in pairs:
        arrays += [a, b]
        specs += [a_spec, b_spec]
    return _call(body, name, (S // tm,), specs + [_rowspec(tm, D), _bspec(D), _rowspec(tm, D)],
                 (_rowspec(tm, D), _bspec(D)), (SDS((S, D), f32), SDS((1, D), f32)))(*arrays, x, g, dres)


def ffn_bdh_rms(dgu, wgu, x, g, dres, name, tm=256):
    _, S, W = dgu.shape
    D = wgu.shape[0]
    tm = _tile(S, tm)
    pairs = [(dgu, pl.BlockSpec((None, tm, W), functools.partial(lambda p, i: (p, i, 0), p)),
              wgu, pl.BlockSpec((D, W), functools.partial(lambda p, i: (0, p), p))) for p in range(2)]
    return nt_rms_bwd(pairs, x, g, dres, name, tm)


def mm_bdh_rms(d, w, x, g, dres, name, tm=256):
    S, K = d.shape
    tm = _tile(S, tm)
    return nt_rms_bwd([(d, pl.BlockSpec((tm, K), lambda i: (i, 0)), w, pl.BlockSpec(w.shape, lambda i: (0, 0)))], x, g, dres, name, tm)


def loss_head(y, t, name):
    S, D = y.shape
    tm = _tile(S, 512)

    def body(y_ref, t_ref, dy_ref, l_ref):
        e = y_ref[...] - t_ref[...]
        dy_ref[...] = e * (1.0 / D)

        @pl.when(pl.program_id(0) == 0)
        def _():
            l_ref[...] = jnp.zeros_like(l_ref)

        l_ref[...] += jnp.sum(jnp.sum(e * e, axis=1, keepdims=True), axis=0, keepdims=True) * (0.5 / D)

    return _call(body, name, (S // tm,), [_rowspec(tm, D), _rowspec(tm, D)], (_rowspec(tm, D), _bspec(LANES)),
                 (SDS((S, D), f32), SDS((1, LANES), f32)))(y, t)


def ffn_fwd(x, g, wgu, wd, tag, comm_gu=None, comm_down=None):
    h = rms_fwd(x, g, f"{tag}_rms")
    (gu, a), got_gu = _carried(ffn_gu_act(h, wgu, f"{tag}_gu", comm=comm_gu), comm_gu)
    xo, got_down = _carried(mm(a, wd, "nn", f32, f"{tag}_down", tm=512, tn=512, res=x, scale=0.5, comm=comm_down), comm_down)
    return xo, (h, gu, a), got_gu, got_down


def ffn_bwd(x, g, wgu, wd, saved, dy, tag):
    h, gu, a = saved
    dgu = ffn_bda_act(dy, wd, gu, 0.5, f"{tag}_bda")
    dwd = mm(a, dy, "tn", bf16, f"{tag}_bwd", tm=512, tn=512, scale=0.5)
    dwgu = ffn_bwgu(h, dgu, f"{tag}_bwgu")
    dx, dg = ffn_bdh_rms(dgu, wgu, x, g, dy, f"{tag}_bdh")
    return dx, dg, dwgu, dwd


def _split_bf16(x):
    hi = x.astype(bf16)
    lo = (x - hi.astype(f32)).astype(bf16)
    return hi, lo


SB_TQ, SB_TK, SB_NSUB = 512, 256, 4
FOX_TK = 256


def _sb_geometry(S, tk=SB_TK):
    tq = min(SB_TQ, S)
    tk = min(tk, tq)
    return tq, tk, tq // SB_NSUB, tq // tk


def _sb_consts(tk):
    r = lax.broadcasted_iota(jnp.int32, (tk, tk), 0)
    c = lax.broadcasted_iota(jnp.int32, (tk, tk), 1)
    return (r > c).astype(bf16), (r < c).astype(bf16)


def _sb_scores(qs, k, kb, tk, rows, masked):
    scale = HEAD_DIM ** -0.5
    z = [_dot(q, k, NT) * scale for q in qs]
    ls = [_logsig(z_) for z_ in z]
    lm = [l - z_ for l, z_ in zip(ls, z)]
    ok = None
    if masked:
        col = kb * tk + lax.broadcasted_iota(jnp.int32, z[0].shape, 1)
        ok = [col < row for row in rows]
        lm = [jnp.where(m, x, 0.0) for m, x in zip(ok, lm)]
    return ls, lm, ok


def _sb_weights(ls, lm, ok, tail, after):
    suf = [_dot(x.astype(bf16), after) for x in lm]
    a = [jnp.exp(l + s_ + t_) for l, s_, t_ in zip(ls, suf, tail)]
    if ok is not None:
        a = [jnp.where(m, x, 0.0) for m, x in zip(ok, a)]
    return a


def sb_fwd(qkv, nh, name, comm=None):
    S = qkv.shape[0]
    tq, tk, rs, nd = _sb_geometry(S)
    nsub = tq // rs

    def body(q_ref, k_ref, v_ref, o_ref, tails_ref):
        i = pl.program_id(1)
        after, _ = _sb_consts(tk)
        qs = [q_ref[pl.ds(s * rs, rs), :] for s in range(nsub)]
        rows = [i * tq + s * rs + lax.broadcasted_iota(jnp.int32, (rs, tk), 0) for s in range(nsub)]

        def tile(kb, carry, masked):
            tail, acc = carry
            off = pl.multiple_of(kb * tk, tk)
            k = k_ref[pl.ds(off, tk), :]
            v = v_ref[pl.ds(off, tk), :]
            ls, lm, ok = _sb_scores(qs, k, kb, tk, rows, masked)
            a = _sb_weights(ls, lm, ok, tail, after)
            tails_ref[pl.ds(kb, 1), :] += jnp.concatenate([t_.T for t_ in tail], axis=1)
            acc = [ac + _dot(a_.astype(bf16), v) for ac, a_ in zip(acc, a)]
            tail = [t_ + jnp.sum(x, axis=1, keepdims=True) for t_, x in zip(tail, lm)]
            return tail, acc

        tails_ref[...] = jnp.zeros_like(tails_ref)
        carry = ([jnp.zeros((rs, 1), f32)] * nsub, [jnp.zeros((rs, HEAD_DIM), f32)] * nsub)
        for d in reversed(range(nd)):
            carry = tile(i * nd + d, carry, True)
        carry = lax.fori_loop(0, i * nd, lambda t, c: tile(i * nd - 1 - t, c, False), carry)
        for s in range(nsub):
            o_ref[pl.ds(s * rs, rs), :] = carry[1][s].astype(o_ref.dtype)

    blk = lambda cb0: pl.BlockSpec((tq, HEAD_DIM), lambda h, i: (i, cb0 + h))
    full = lambda cb0: pl.BlockSpec((S, HEAD_DIM), lambda h, i: (0, cb0 + h))
    tails = pl.BlockSpec((S // tk, tq), lambda h, i: (h, i))
    return _call(body, name, (nh, S // tq), [blk(0), full(nh), full(2 * nh)], (blk(0), tails),
                 (SDS((S, nh * HEAD_DIM), bf16), SDS((nh * (S // tk), S), f32)), comm=comm)(qkv, qkv, qkv)


def sb_bwd(qkv, tails, do, nh, name, comm=None):
    S = qkv.shape[0]
    tq, tk, rs, nd = _sb_geometry(S)
    nsub = tq // rs
    scale = HEAD_DIM ** -0.5

    def body(q_ref, k_ref, v_ref, tails_ref, do_ref, dq_ref, dk_ref, dv_ref):
        i = pl.program_id(1)

        @pl.when(i == 0)
        def _():
            dk_ref[...] = jnp.zeros_like(dk_ref)
            dv_ref[...] = jnp.zeros_like(dv_ref)

        after, before = _sb_consts(tk)
        qs = [q_ref[pl.ds(s * rs, rs), :] for s in range(nsub)]
        dos = [do_ref[pl.ds(s * rs, rs), :].astype(bf16) for s in range(nsub)]
        rows = [i * tq + s * rs + lax.broadcasted_iota(jnp.int32, (rs, tk), 0) for s in range(nsub)]

        def sweep2(kb, carry, masked):
            pe, dq = carry
            off = pl.multiple_of(kb * tk, tk)
            k = k_ref[pl.ds(off, tk), :]
            v = v_ref[pl.ds(off, tk), :]
            ls, lm, ok = _sb_scores(qs, k, kb, tk, rows, masked)
            tail_row = tails_ref[pl.ds(kb, 1), :]
            tail = [tail_row[:, s * rs:(s + 1) * rs].T for s in range(nsub)]
            a = _sb_weights(ls, lm, ok, tail, after)
            e = [_dot(d_, v, NT) * a_ for d_, a_ in zip(dos, a)]
            cum_e = [p_ + _dot(e_.astype(bf16), before) for p_, e_ in zip(pe, e)]
            sig = [jnp.exp(l) for l in ls]
            dz = [e_ * (1.0 - sg) - c_ * sg for e_, sg, c_ in zip(e, sig, cum_e)]
            if ok is not None:
                dz = [jnp.where(m, x, 0.0) for m, x in zip(ok, dz)]
            dzb = [(x * scale).astype(bf16) for x in dz]
            dk_ref[pl.ds(off, tk), :] += sum(_dot(x, q, TN) for x, q in zip(dzb, qs))
            dv_ref[pl.ds(off, tk), :] += sum(_dot(a_.astype(bf16), d_, TN) for a_, d_ in zip(a, dos))
            pe = [p_ + jnp.sum(e_, axis=1, keepdims=True) for p_, e_ in zip(pe, e)]
            dq = [g + _dot(x, k) for g, x in zip(dq, dzb)]
            return pe, dq

        carry = ([jnp.zeros((rs, 1), f32)] * nsub, [jnp.zeros((rs, HEAD_DIM), f32)] * nsub)
        carry = lax.fori_loop(0, i * nd, lambda kb, c: sweep2(kb, c, False), carry)
        for d in range(nd):
            carry = sweep2(i * nd + d, carry, True)
        for s in range(nsub):
            dq_ref[pl.ds(s * rs, rs), :] = carry[1][s]

    blk = lambda cb0: pl.BlockSpec((tq, HEAD_DIM), lambda h, i: (i, cb0 + h))
    full = lambda cb0: pl.BlockSpec((S, HEAD_DIM), lambda h, i: (0, cb0 + h))
    sh = SDS((S, nh * HEAD_DIM), f32)
    tails_spec = pl.BlockSpec((S // tk, tq), lambda h, i: (h, i))
    return _call(body, name, (nh, S // tq), [blk(0), full(nh), full(2 * nh), tails_spec, blk(0)], (blk(0), full(0), full(0)),
                 (sh, sh, sh), comm=comm)(qkv, qkv, qkv, tails, do)


def fox_fwd(fq, fk, fv, c, cT, nh, lane0, name, comm=None):
    S = fq.shape[0]
    tq, tk, rs, nd = _sb_geometry(S, FOX_TK)
    nsub = tq // rs
    scale = HEAD_DIM ** -0.5

    def body(q_ref, k_ref, v_ref, c_ref, ct_ref, o_ref, lse_ref):
        h = pl.program_id(0)
        i = pl.program_id(1)
        subs = range(nsub)
        qs = [q_ref[pl.ds(s * rs, rs), :] for s in subs]
        ci = [_lane_pick(c_ref[pl.ds(s * rs, rs), :], lane0 + h) for s in subs]
        rows = [i * tq + s * rs + lax.broadcasted_iota(jnp.int32, (rs, tk), 0) for s in subs]

        def tile(kb, carry, masked):
            m, l, acc = carry
            off = pl.multiple_of(kb * tk, tk)
            k = k_ref[pl.ds(off, tk), :]
            v = v_ref[pl.ds(off, tk), :]
            cj = ct_ref[pl.ds(lane0 % 8 + h, 1), pl.ds(off, tk)]
            sc = [_dot(q, k, NT) * scale + (c_ - cj) for q, c_ in zip(qs, ci)]
            if masked:
                col = kb * tk + lax.broadcasted_iota(jnp.int32, (rs, tk), 1)
                sc = [jnp.where(col <= row, x, -jnp.inf) for x, row in zip(sc, rows)]
            m2 = [jnp.maximum(m_, jnp.max(x, axis=1, keepdims=True)) for m_, x in zip(m, sc)]
            p = [jnp.exp(x - m_) for x, m_ in zip(sc, m2)]
            al = [jnp.exp(a - b) for a, b in zip(m, m2)]
            l = [a * l_ + jnp.sum(p_, axis=1, keepdims=True) for a, l_, p_ in zip(al, l, p)]
            acc = [a * ac + _dot(p_.astype(bf16), v) for a, ac, p_ in zip(al, acc, p)]
            return m2, l, acc

        carry = ([jnp.full((rs, 1), -jnp.inf, f32)] * nsub, [jnp.zeros((rs, 1), f32)] * nsub,
                 [jnp.zeros((rs, HEAD_DIM), f32)] * nsub)
        for d in range(nd):
            carry = tile(i * nd + d, carry, True)
        m, l, acc = lax.fori_loop(0, i * nd, lambda kb, cr: tile(kb, cr, False), carry)
        for s in subs:
            o_ref[pl.ds(s * rs, rs), :] = acc[s] / l[s]
            lse_ref[pl.ds(s * rs, rs), :] = jnp.broadcast_to(m[s] + jnp.log(l[s]), (rs, HEAD_DIM))

    blk = pl.BlockSpec((tq, HEAD_DIM), lambda h, i: (i, h))
    full = pl.BlockSpec((S, HEAD_DIM), lambda h, i: (0, h))
    cspec = pl.BlockSpec((tq, LANES), lambda h, i: (i, 0))
    ctspec = pl.BlockSpec((8, S), lambda h, i: (lane0 // 8, 0))
    sh = SDS((S, nh * HEAD_DIM), f32)
    return _call(body, name, (nh, S // tq), [blk, full, full, cspec, ctspec], (blk, blk), (sh, sh), comm=comm)(fq, fk, fv, c, cT)


def fox_bwd(fq, fk, fv, c, cT, o, lse, do, nh, lane0, name, comm=None):
    S = fq.shape[0]
    tq, tk, rs, nd = _sb_geometry(S, FOX_TK)
    nsub = tq // rs
    scale = HEAD_DIM ** -0.5

    def body(q_ref, k_ref, v_ref, c_ref, ct_ref, o_ref, lse_ref, do_ref, dq_ref, dk_ref, dv_ref, dct_ref):
        h = pl.program_id(0)
        i = pl.program_id(1)

        @pl.when(i == 0)
        def _():
            dk_ref[...] = jnp.zeros_like(dk_ref)
            dv_ref[...] = jnp.zeros_like(dv_ref)

        @pl.when((i == 0) & (h == 0))
        def _():
            dct_ref[...] = jnp.zeros_like(dct_ref)

        subs = range(nsub)
        qs = [q_ref[pl.ds(s * rs, rs), :] for s in subs]
        dof = [do_ref[pl.ds(s * rs, rs), :] for s in subs]
        dos = [x.astype(bf16) for x in dof]
        ci = [_lane_pick(c_ref[pl.ds(s * rs, rs), :], lane0 + h) for s in subs]
        lses = [lse_ref[pl.ds(s * rs, rs), :][:, :1] for s in subs]
        dl = [jnp.sum(x * o_ref[pl.ds(s * rs, rs), :], axis=1, keepdims=True) for s, x in zip(subs, dof)]
        rows = [i * tq + s * rs + lax.broadcasted_iota(jnp.int32, (rs, tk), 0) for s in subs]

        def tile(kb, carry, masked):
            dq, rsum = carry
            off = pl.multiple_of(kb * tk, tk)
            k = k_ref[pl.ds(off, tk), :]
            v = v_ref[pl.ds(off, tk), :]
            cj = ct_ref[pl.ds(lane0 % 8 + h, 1), pl.ds(off, tk)]
            p = [jnp.exp(_dot(q, k, NT) * scale + (c_ - cj) - ls) for q, c_, ls in zip(qs, ci, lses)]
            if masked:
                col = kb * tk + lax.broadcasted_iota(jnp.int32, (rs, tk), 1)
                p = [jnp.where(col <= row, x, 0.0) for x, row in zip(p, rows)]
            ds = [p_ * (_dot(d_, v, NT) - dl_) for p_, d_, dl_ in zip(p, dos, dl)]
            dsb = [(x * scale).astype(bf16) for x in ds]
            dk_ref[pl.ds(off, tk), :] += sum(_dot(x, q, TN) for x, q in zip(dsb, qs))
            dv_ref[pl.ds(off, tk), :] += sum(_dot(p_.astype(bf16), d_, TN) for p_, d_ in zip(p, dos))
            dct_ref[pl.ds(lane0 + h, 1), pl.ds(off, tk)] -= sum(jnp.sum(x, axis=0, keepdims=True) for x in ds)
            return ([g + _dot(x, k) for g, x in zip(dq, dsb)],
                    [r_ + jnp.sum(x, axis=1, keepdims=True) for r_, x in zip(rsum, ds)])

        carry = ([jnp.zeros((rs, HEAD_DIM), f32)] * nsub, [jnp.zeros((rs, 1), f32)] * nsub)
        carry = lax.fori_loop(0, i * nd, lambda kb, cr: tile(kb, cr, False), carry)
        for d in range(nd):
            carry = tile(i * nd + d, carry, True)
        dq, rsum = carry
        for s in subs:
            dq_ref[pl.ds(s * rs, rs), :] = dq[s]
        dct_ref[pl.ds(lane0 + h, 1), pl.ds(pl.multiple_of(i * tq, tq), tq)] += jnp.concatenate([r_.T for r_ in rsum], axis=1)

    blk = pl.BlockSpec((tq, HEAD_DIM), lambda h, i: (i, h))
    full = pl.BlockSpec((S, HEAD_DIM), lambda h, i: (0, h))
    cspec = pl.BlockSpec((tq, LANES), lambda h, i: (i, 0))
    ctspec = pl.BlockSpec((8, S), lambda h, i: (lane0 // 8, 0))
    dctspec = pl.BlockSpec((LANES, S), lambda h, i: (0, 0))
    sh = SDS((S, nh * HEAD_DIM), f32)
    return _call(body, name, (nh, S // tq), [blk, full, full, cspec, ctspec, blk, blk, blk], (blk, full, full, dctspec),
                 (sh, sh, sh, SDS((LANES, S), f32)), comm=comm)(fq, fk, fv, c, cT, o, lse, do)


def gates_fwd(proj, small_cb, fbias_row, name, tm=256):
    S = proj.shape[0]
    tm = _tile(S, tm)

    def body(sm_ref, fb_ref, c_ref, ct_ref, carry_ref):
        @pl.when(pl.program_id(0) == 0)
        def _():
            carry_ref[...] = jnp.zeros_like(carry_ref)

        lf = _logsig(sm_ref[...] + fb_ref[...])
        r = lax.broadcasted_iota(jnp.int32, (tm, tm), 0)
        cc = lax.broadcasted_iota(jnp.int32, (tm, tm), 1)
        c = _dot((r >= cc).astype(f32), lf, NN, HI) + carry_ref[...]
        carry_ref[...] += jnp.sum(lf, axis=0, keepdims=True)
        c_ref[...] = c
        ct_ref[...] = c.T

    return _call(body, name, (S // tm,), [_rowspec(tm, LANES, small_cb), _bspec(LANES)],
                 (_rowspec(tm, LANES), pl.BlockSpec((LANES, tm), lambda i: (0, i))),
                 (SDS((S, LANES), f32), SDS((LANES, S), f32)), scratch=[pltpu.VMEM((1, LANES), f32)])(proj, fbias_row)


def gates_bwd(proj, small_cb, fbias_row, dcT, dsm_dn, lane0, nh, name, tm=256):
    S = proj.shape[0]
    tm = _tile(S, tm)
    nt = S // tm

    def body(sm_ref, fb_ref, dct_ref, dsm_ref, o_ref, dfb_ref, carry_ref):
        @pl.when(pl.program_id(0) == 0)
        def _():
            carry_ref[...] = jnp.zeros_like(carry_ref)
            dfb_ref[...] = jnp.zeros_like(dfb_ref)

        dc = dct_ref[...].T
        r = lax.broadcasted_iota(jnp.int32, (tm, tm), 0)
        cc = lax.broadcasted_iota(jnp.int32, (tm, tm), 1)
        dlf = _dot((r <= cc).astype(f32), dc, NN, HI) + carry_ref[...]
        carry_ref[...] += jnp.sum(dc, axis=0, keepdims=True)
        lane = lax.broadcasted_iota(jnp.int32, (1, LANES), 1)
        dpre = jnp.where((lane >= lane0) & (lane < lane0 + nh), dlf * jax.nn.sigmoid(-(sm_ref[...] + fb_ref[...])), 0.0)
        o_ref[...] = dsm_ref[...] + dpre
        dfb_ref[...] += jnp.sum(dpre, axis=0, keepdims=True)

    rev = lambda i: nt - 1 - i
    return _call(body, name, (nt,),
                 [pl.BlockSpec((tm, LANES), lambda i: (rev(i), small_cb)), _bspec(LANES),
                  pl.BlockSpec((LANES, tm), lambda i: (0, rev(i))), pl.BlockSpec((tm, LANES), lambda i: (rev(i), 0))],
                 (pl.BlockSpec((tm, LANES), lambda i: (rev(i), 0)), _bspec(LANES)),
                 (SDS((S, LANES), f32), SDS((1, LANES), f32)), scratch=[pltpu.VMEM((1, LANES), f32)])(proj, fbias_row, dcT, dsm_dn)


PAD_ROWS = 8


def conv_fwd(proj, w, width, name):
    S = proj.shape[0]
    cw = 256 if width % 256 == 0 else 128

    def body(x_ref, w_ref, y_ref, pad_ref):
        pad_ref[pl.ds(0, PAD_ROWS), :] = jnp.zeros((PAD_ROWS, cw), f32)
        pad_ref[pl.ds(PAD_ROWS, S), :] = x_ref[...]
        y = jnp.zeros((S, cw), f32)
        for i in range(CONV_WIDTH):
            y = y + pad_ref[pl.ds(PAD_ROWS - (CONV_WIDTH - 1) + i, S), :] * w_ref[pl.ds(i, 1), :]
        y_ref[...] = y * jax.nn.sigmoid(y)

    spec = pl.BlockSpec((S, cw), lambda j: (0, j))
    return _call(body, name, (width // cw,), [spec, pl.BlockSpec((CONV_WIDTH, cw), lambda j: (0, j))], spec,
                 SDS((S, width), f32), scratch=[pltpu.VMEM((S + PAD_ROWS, cw), f32)])(proj, w)


def conv_bwd(proj, w, dy, name):
    S, width = dy.shape
    cw = 256 if width % 256 == 0 else 128

    def body(x_ref, w_ref, dy_ref, dx_ref, dw_ref, xpad_ref, dpad_ref):
        xpad_ref[pl.ds(0, PAD_ROWS), :] = jnp.zeros((PAD_ROWS, cw), f32)
        xpad_ref[pl.ds(PAD_ROWS, S), :] = x_ref[...]
        y = jnp.zeros((S, cw), f32)
        for i in range(CONV_WIDTH):
            y = y + xpad_ref[pl.ds(PAD_ROWS - (CONV_WIDTH - 1) + i, S), :] * w_ref[pl.ds(i, 1), :]
        sg = jax.nn.sigmoid(y)
        dpre = dy_ref[...] * (sg * (1.0 + y * (1.0 - sg)))
        dpad_ref[pl.ds(S, PAD_ROWS), :] = jnp.zeros((PAD_ROWS, cw), f32)
        dpad_ref[pl.ds(0, S), :] = dpre
        dx = jnp.zeros((S, cw), f32)
        for i in range(CONV_WIDTH):
            dx = dx + dpad_ref[pl.ds(CONV_WIDTH - 1 - i, S), :] * w_ref[pl.ds(i, 1), :]
            dw_ref[pl.ds(i, 1), :] = jnp.sum(dpre * xpad_ref[pl.ds(PAD_ROWS - (CONV_WIDTH - 1) + i, S), :], axis=0, keepdims=True)
        dx_ref[...] = dx

    spec = pl.BlockSpec((S, cw), lambda j: (0, j))
    wspec = pl.BlockSpec((CONV_WIDTH, cw), lambda j: (0, j))
    return _call(body, name, (width // cw,), [spec, wspec, spec], (spec, wspec),
                 (SDS((S, width), f32), SDS((CONV_WIDTH, width), f32)),
                 scratch=[pltpu.VMEM((S + PAD_ROWS, cw), f32), pltpu.VMEM((S + PAD_ROWS, cw), f32)])(proj, w, dy)


def _pdot_raw(a, b, dims, passes):
    if passes == 1:
        return _dot(a.astype(bf16), b.astype(bf16), dims)
    ah, al = _split_bf16(a)
    bh, bl = _split_bf16(b)
    return _dot(ah, bh, dims) + (_dot(ah, bl, dims) + _dot(al, bh, dims))


def _make_pdot(passes):
    @functools.partial(jax.custom_vjp, nondiff_argnums=(2,))
    def pdot(a, b, mode):
        return _pdot_raw(a, b, {"nn": NN, "nt": NT, "tn": TN}[mode], passes)

    def fwd(a, b, mode):
        return pdot(a, b, mode), (a, b)

    def bwd(mode, res, g):
        a, b = res
        if mode == "nn":
            return _pdot_raw(g, b, NT, passes), _pdot_raw(a, g, TN, passes)
        if mode == "nt":
            return _pdot_raw(g, b, NN, passes), _pdot_raw(g, a, TN, passes)
        return _pdot_raw(b, g, NT, passes), _pdot_raw(a, g, NN, passes)

    pdot.defvjp(fwd, bwd)
    return pdot


_dot1 = _make_pdot(1)
_dot3 = _make_pdot(3)


def _each(f, *lists):
    return [f(*xs) for xs in zip(*lists)]


def _dn_chunk(ndn, cq, ck, cv, small, alog, dtb, s0):
    C = cq[0].shape[0]
    hs = list(range(ndn))
    b = [_lane_pick(small, h) for h in hs]
    d = [_lane_pick(small, ndn + h) for h in hs]
    al = [_lane_pick(alog, h) for h in hs]
    dt = [_lane_pick(dtb, h) for h in hs]
    q = _each(lambda x: x * lax.rsqrt(jnp.sum(x * x, axis=1, keepdims=True) + EPS) * (HEAD_DIM ** -0.5), cq)
    k = _each(lambda x: x * lax.rsqrt(jnp.sum(x * x, axis=1, keepdims=True) + EPS), ck)
    beta = _each(jax.nn.sigmoid, b)
    g = _each(lambda al_, d_, dt_: -jnp.exp(al_) * _softplus(d_ + dt_), al, d, dt)
    r = lax.broadcasted_iota(jnp.int32, (C, C), 0)
    c = lax.broadcasted_iota(jnp.int32, (C, C), 1)
    incl, strict = r >= c, r > c
    inclf = incl.astype(f32)
    gc = _each(lambda g_: _dot3(inclf, g_, "nn"), g)
    decay = _each(lambda gc_: jnp.where(incl, jnp.exp(jnp.where(incl, gc_ - gc_.T, 0.0)), 0.0), gc)
    kb = _each(lambda k_, b_: k_ * b_, k, beta)
    a = _each(lambda kb_, k_, dec: jnp.where(strict, _dot3(kb_, k_, "nt") * dec, 0.0), kb, k, decay)
    eye = (r == c).astype(f32)
    t = _each(lambda a_: eye - a_, a)
    p = a
    for _ in range(int(math.log2(C)) - 1):
        p = _each(lambda p_: _dot3(p_, p_, "nn"), p)
        t = _each(lambda t_, p_: t_ + _dot3(t_, p_, "nn"), t, p)
    eg = _each(jnp.exp, gc)
    u = _each(lambda t_, v_, b_: _dot3(t_, v_ * b_, "nn"), t, cv, beta)
    w = _each(lambda t_, kb_, eg_: _dot3(t_, kb_ * eg_, "nn"), t, kb, eg)
    attn = _each(lambda q_, k_, dec: _dot1(q_, k_, "nt") * dec, q, k, decay)
    g_last = _each(lambda g_: jnp.sum(g_, axis=0, keepdims=True), g)
    v_new = _each(lambda u_, w_, s_: u_ - _dot1(w_, s_, "nn"), u, w, s0)
    o = _each(lambda q_, eg_, s_, at, vn: _dot1(q_ * eg_, s_, "nn") + _dot1(at, vn, "nn"), q, eg, s0, attn, v_new)
    s1 = _each(lambda s_, gl, k_, gc_, vn: s_ * jnp.exp(gl) + _dot1(k_ * jnp.exp(gl - gc_), vn, "tn"), s0, g_last, k, gc, v_new)
    return o, s1


def dn_fwd(cqkv, proj, small_cb, alog_row, dt_row, ndn, name, comm=None):
    S = cqkv.shape[0]
    C = DN_CHUNK
    nc = S // C
    half = ndn * HEAD_DIM

    def body(q_ref, k_ref, v_ref, sm_ref, al_ref, dt_ref, o_ref, ss_ref, s_ref):
        @pl.when(pl.program_id(0) == 0)
        def _():
            s_ref[...] = jnp.zeros_like(s_ref)

        sls = [slice(h * HEAD_DIM, (h + 1) * HEAD_DIM) for h in range(ndn)]
        s0 = [s_ref[h] for h in range(ndn)]
        for h in range(ndn):
            ss_ref[h, 0] = s0[h]
        o, s1 = _dn_chunk(ndn, [q_ref[:, sl] for sl in sls], [k_ref[:, sl] for sl in sls], [v_ref[:, sl] for sl in sls],
                          sm_ref[...], al_ref[...], dt_ref[...], s0)
        for h in range(ndn):
            o_ref[:, sls[h]] = o[h]
            s_ref[h] = s1[h]

    blk = lambda cb: pl.BlockSpec((C, half), lambda n: (n, cb))
    row = pl.BlockSpec((1, LANES), lambda n: (0, 0))
    return _call(body, name, (nc,),
                 [blk(0), blk(1), blk(2), pl.BlockSpec((C, LANES), lambda n: (n, small_cb)), row, row],
                 (blk(0), pl.BlockSpec((ndn, 1, HEAD_DIM, HEAD_DIM), lambda n: (0, n, 0, 0))),
                 (SDS((S, half), f32), SDS((ndn, nc, HEAD_DIM, HEAD_DIM), f32)),
                 scratch=[pltpu.VMEM((ndn, HEAD_DIM, HEAD_DIM), f32)], comm=comm)(cqkv, cqkv, cqkv, proj, alog_row, dt_row)


def dn_bwd(cqkv, proj, small_cb, alog_row, dt_row, ssave, do, ndn, name, comm=None):
    S = cqkv.shape[0]
    C = DN_CHUNK
    nc = S // C
    half = ndn * HEAD_DIM

    def body(q_ref, k_ref, v_ref, sm_ref, al_ref, dt_ref, ss_ref, do_ref, dqkv_ref, dsm_ref, dal_ref, ddt_ref, ds_ref):
        @pl.when(pl.program_id(0) == 0)
        def _():
            ds_ref[...] = jnp.zeros_like(ds_ref)
            dal_ref[...] = jnp.zeros_like(dal_ref)
            ddt_ref[...] = jnp.zeros_like(ddt_ref)

        sls = [slice(h * HEAD_DIM, (h + 1) * HEAD_DIM) for h in range(ndn)]
        _, vjp = jax.vjp(functools.partial(_dn_chunk, ndn), [q_ref[:, sl] for sl in sls], [k_ref[:, sl] for sl in sls],
                         [v_ref[:, sl] for sl in sls], sm_ref[...], al_ref[...], dt_ref[...], [ss_ref[h, 0] for h in range(ndn)])
        dq, dk, dv, dsm, dal, ddt, ds0 = vjp(([do_ref[:, sl] for sl in sls], [ds_ref[h] for h in range(ndn)]))
        for h in range(ndn):
            dqkv_ref[:, sls[h]] = dq[h]
            dqkv_ref[:, half + h * HEAD_DIM:half + (h + 1) * HEAD_DIM] = dk[h]
            dqkv_ref[:, 2 * half + h * HEAD_DIM:2 * half + (h + 1) * HEAD_DIM] = dv[h]
            ds_ref[h] = ds0[h]
        dsm_ref[...] = dsm
        dal_ref[...] += dal
        ddt_ref[...] += ddt

    rev = lambda n: nc - 1 - n
    blk = lambda cb: pl.BlockSpec((C, half), lambda n: (rev(n), cb))
    row = pl.BlockSpec((1, LANES), lambda n: (0, 0))
    return _call(body, name, (nc,),
                 [blk(0), blk(1), blk(2), pl.BlockSpec((C, LANES), lambda n: (rev(n), small_cb)), row, row,
                  pl.BlockSpec((ndn, 1, HEAD_DIM, HEAD_DIM), lambda n: (0, rev(n), 0, 0)), blk(0)],
                 (pl.BlockSpec((C, 3 * half), lambda n: (rev(n), 0)), pl.BlockSpec((C, LANES), lambda n: (rev(n), 0)), row, row),
                 (SDS((S, 3 * half), f32), SDS((S, LANES), f32), SDS((1, LANES), f32), SDS((1, LANES), f32)),
                 scratch=[pltpu.VMEM((ndn, HEAD_DIM, HEAD_DIM), f32)], comm=comm)(cqkv, cqkv, cqkv, proj, alog_row, dt_row, ssave, do)


def even_pre(proj, gq, gk, dfx, cb0, name):
    S = proj.shape[0]
    tm = _tile(S, 256)
    nh = dfx // HEAD_DIM

    def body(q_ref, k_ref, v_ref, gq_ref, gk_ref, fq_ref, fk_ref, fv_ref):
        for h in range(nh):
            sl = slice(h * HEAD_DIM, (h + 1) * HEAD_DIM)
            fq_ref[:, sl] = _rms(q_ref[:, sl], gq_ref[...]).astype(bf16)
            fk_ref[:, sl] = _rms(k_ref[:, sl], gk_ref[...]).astype(bf16)
        fv_ref[...] = v_ref[...].astype(bf16)

    sh = SDS((S, dfx), bf16)
    o = _rowspec(tm, dfx)
    return _call(body, name, (S // tm,),
                 [_rowspec(tm, dfx, cb0), _rowspec(tm, dfx, cb0 + 1), _rowspec(tm, dfx, cb0 + 2), _bspec(HEAD_DIM), _bspec(HEAD_DIM)],
                 (o, o, o), (sh, sh, sh))(proj, proj, proj, gq, gk)


def even_pre_bwd(proj, gq, gk, dfq, dfk, dfx, cb0, name):
    S = proj.shape[0]
    tm = _tile(S, 256)
    nh = dfx // HEAD_DIM

    def body(q_ref, k_ref, gq_ref, gk_ref, dfq_ref, dfk_ref, dq_ref, dk_ref, dgq_ref, dgk_ref):
        @pl.when(pl.program_id(0) == 0)
        def _():
            dgq_ref[...] = jnp.zeros_like(dgq_ref)
            dgk_ref[...] = jnp.zeros_like(dgk_ref)

        for h in range(nh):
            sl = slice(h * HEAD_DIM, (h + 1) * HEAD_DIM)
            _, vq = jax.vjp(_rms, q_ref[:, sl], gq_ref[...])
            dq, dgq = vq(dfq_ref[:, sl])
            _, vk = jax.vjp(_rms, k_ref[:, sl], gk_ref[...])
            dk, dgk = vk(dfk_ref[:, sl])
            dq_ref[:, sl] = dq
            dk_ref[:, sl] = dk
            dgq_ref[...] += dgq
            dgk_ref[...] += dgk

    sh = SDS((S, dfx), f32)
    o = _rowspec(tm, dfx)
    g = SDS((1, HEAD_DIM), f32)
    return _call(body, name, (S // tm,),
                 [_rowspec(tm, dfx, cb0), _rowspec(tm, dfx, cb0 + 1), _bspec(HEAD_DIM), _bspec(HEAD_DIM), o, o],
                 (o, o, _bspec(HEAD_DIM), _bspec(HEAD_DIM)), (sh, sh, g, g))(proj, proj, gq, gk, dfq, dfk)


def _dn_out(o, gate, gn):
    return _rms(o, gn) * (gate * jax.nn.sigmoid(gate))


def _fox_out(o, gate):
    return o * jax.nn.sigmoid(gate)


def even_post(o_dn, o_fox, proj, gn, half, cb_dn_gate, cb_fox_gate, name):
    S = proj.shape[0]
    tm = _tile(S, 256)
    nh = half // HEAD_DIM

    def body(od_ref, of_ref, gd_ref, gf_ref, gn_ref, o_ref):
        for h in range(nh):
            sl = slice(h * HEAD_DIM, (h + 1) * HEAD_DIM)
            o_ref[:, sl] = _dn_out(od_ref[:, sl], gd_ref[:, sl], gn_ref[...]).astype(bf16)
        o_ref[:, half:] = _fox_out(of_ref[...], gf_ref[...]).astype(bf16)

    return _call(body, name, (S // tm,),
                 [_rowspec(tm, half), _rowspec(tm, half), _rowspec(tm, half, cb_dn_gate), _rowspec(tm, half, cb_fox_gate), _bspec(HEAD_DIM)],
                 _rowspec(tm, 2 * half), SDS((S, 2 * half), bf16))(o_dn, o_fox, proj, proj, gn)


def even_post_bwd(o_dn, o_fox, proj, gn, dom, half, cb_dn_gate, cb_fox_gate, name):
    S = proj.shape[0]
    tm = _tile(S, 256)
    nh = half // HEAD_DIM

    def body(od_ref, of_ref, gd_ref, gf_ref, gn_ref, dod_ref, dof_ref, dd_ref, df_ref, dgd_ref, dgf_ref, dgn_ref):
        @pl.when(pl.program_id(0) == 0)
        def _():
            dgn_ref[...] = jnp.zeros_like(dgn_ref)

        for h in range(nh):
            sl = slice(h * HEAD_DIM, (h + 1) * HEAD_DIM)
            _, vjp = jax.vjp(_dn_out, od_ref[:, sl], gd_ref[:, sl], gn_ref[...])
            d_o, d_gate, d_gn = vjp(dod_ref[:, sl])
            dd_ref[:, sl] = d_o
            dgd_ref[:, sl] = d_gate
            dgn_ref[...] += d_gn
        _, vjp = jax.vjp(_fox_out, of_ref[...], gf_ref[...])
        d_o, d_gate = vjp(dof_ref[...])
        df_ref[...] = d_o
        dgf_ref[...] = d_gate

    sh = SDS((S, half), f32)
    o = _rowspec(tm, half)
    return _call(body, name, (S // tm,),
                 [o, o, _rowspec(tm, half, cb_dn_gate), _rowspec(tm, half, cb_fox_gate), _bspec(HEAD_DIM),
                  _rowspec(tm, half, 0), _rowspec(tm, half, 1)],
                 (o, o, o, o, _bspec(HEAD_DIM)), (sh, sh, sh, sh, SDS((1, HEAD_DIM), f32)))(o_dn, o_fox, proj, proj, gn, dom, dom)


def _pad_row(v, lane0=0):
    return jnp.pad(v, (lane0, LANES - lane0 - v.shape[0]))[None]


def _carried(res, comm):
    return res if comm is not None else (res, [])


def even_mixer_fwd(x, gmix, w_in, w_out, conv_w, a_log, dt_bias, gn, gq, gk, f_bias, tag, comm_fox=None, comm_dn=None,
                   comm_in=None):
    S, D = x.shape
    half = D // 2
    ndn = half // HEAD_DIM
    small_cb = 4 * D // LANES
    alog_row, dt_row, fb_row = _pad_row(a_log), _pad_row(dt_bias), _pad_row(f_bias, 2 * ndn)
    h = rms_fwd(x, gmix, f"{tag}_rms")
    proj, got_in = _carried(mm(h, w_in, "nn", f32, f"{tag}_in", tm=512, tn=1408, comm=comm_in), comm_in)
    cqkv = conv_fwd(proj, conv_w, 3 * half, f"{tag}_conv")
    (o_dn, ssave), got_dn = _carried(dn_fwd(cqkv, proj, small_cb, alog_row, dt_row, ndn, f"{tag}_dn", comm=comm_dn), comm_dn)
    c, cT = gates_fwd(proj, small_cb, fb_row, f"{tag}_gates")
    fq, fk, fv = even_pre(proj, gq, gk, half, 4, f"{tag}_pre")
    (o_fox, lse), got_fox = _carried(fox_fwd(fq, fk, fv, c, cT, ndn, 2 * ndn, f"{tag}_fox", comm=comm_fox), comm_fox)
    om = even_post(o_dn, o_fox, proj, gn, half, 3, 7, f"{tag}_post")
    xo = mm(om, w_out, "nn", f32, f"{tag}_out", res=x)
    return xo, (h, proj, cqkv, o_dn, ssave, c, cT, fq, fk, fv, o_fox, lse, om, alog_row, dt_row, fb_row), got_fox, got_dn, got_in


def even_mixer_bwd(x, gmix, w_in, w_out, conv_w, gn, gq, gk, saved, dy, tag, comm_fox=None, comm_dn=None):
    S, D = x.shape
    half = D // 2
    ndn = half // HEAD_DIM
    small_cb = 4 * D // LANES
    h, proj, cqkv, o_dn, ssave, c, cT, fq, fk, fv, o_fox, lse, om, alog_row, dt_row, fb_row = saved
    dom = mm(dy, w_out, "nt", f32, f"{tag}_bdom")
    dw_out = mm(om, dy, "tn", bf16, f"{tag}_bwout")
    d_odn, d_ofox, d_dgate, d_fgate, d_gn = even_post_bwd(o_dn, o_fox, proj, gn, dom, half, 3, 7, f"{tag}_bpost")
    (dfq, dfk, dfv, dcT), got_fox = _carried(
        fox_bwd(fq, fk, fv, c, cT, o_fox, lse, d_ofox, ndn, 2 * ndn, f"{tag}_bfox", comm=comm_fox), comm_fox)
    dq_pre, dk_pre, d_gq, d_gk = even_pre_bwd(proj, gq, gk, dfq, dfk, half, 4, f"{tag}_bpre")
    (dcqkv, dsm_dn, d_alog, d_dt), got_dn = _carried(
        dn_bwd(cqkv, proj, small_cb, alog_row, dt_row, ssave, d_odn, ndn, f"{tag}_bdn", comm=comm_dn), comm_dn)
    d_conv_in, d_conv_w = conv_bwd(proj, conv_w, dcqkv, f"{tag}_bconv")
    d_small, d_fb = gates_bwd(proj, small_cb, fb_row, dcT, dsm_dn, 2 * ndn, ndn, f"{tag}_bgates")
    dproj = jnp.concatenate([d_conv_in, d_dgate, dq_pre, dk_pre, dfv, d_fgate, d_small], axis=1).astype(bf16)
    dw_in = mm(h, dproj, "tn", bf16, f"{tag}_bwin", tn=1408)
    dx, d_gmix = mm_bdh_rms(dproj, w_in, x, gmix, dy, f"{tag}_bdh")
    small = dict(norm_mix=d_gmix[0], dn_conv_w=d_conv_w, dn_a_log=d_alog[0, :ndn], dn_dt_bias=d_dt[0, :ndn],
                 dn_norm_g=d_gn[0], fox_q_norm_g=d_gq[0], fox_k_norm_g=d_gk[0], fox_f_bias=d_fb[0, 2 * ndn:3 * ndn])
    return dx, dw_in, dw_out, small, got_fox, got_dn


def odd_mixer_fwd(x, gmix, w_in, w_out, tag, comm=None):
    S, D = x.shape
    nh = D // HEAD_DIM
    h = rms_fwd(x, gmix, f"{tag}_rms")
    qkv = mm(h, w_in, "nn", bf16, f"{tag}_in", tn=768)
    (o, tails), got = _carried(sb_fwd(qkv, nh, f"{tag}_sb", comm=comm), comm)
    xo = mm(o, w_out, "nn", f32, f"{tag}_out", res=x)
    return xo, (h, qkv, o, tails), got


def odd_mixer_bwd(x, gmix, w_in, w_out, saved, dy, tag, comm=None):
    S, D = x.shape
    nh = D // HEAD_DIM
    h, qkv, o, tails = saved
    do = mm(dy, w_out, "nt", f32, f"{tag}_bdo")
    dw_out = mm(o, dy, "tn", bf16, f"{tag}_bwout")
    (dq, dk, dv), got = _carried(sb_bwd(qkv, tails, do, nh, f"{tag}_bsb", comm=comm), comm)
    dqkv = jnp.concatenate([dq, dk, dv], axis=1).astype(bf16)
    dw_in = mm(h, dqkv, "tn", bf16, f"{tag}_bwin", tn=1536)
    dx, d_gmix = mm_bdh_rms(dqkv, w_in, x, gmix, dy, f"{tag}_bdh")
    return dx, dw_in, dw_out, dict(norm_mix=d_gmix[0]), got


def all_gather(shards, axes, name, kind="ag"):
    return _call(None, name, (), [], [], [], comm=Comm(kind, shards, axes))()[1]


def scatter_parts(fulls, axes, name):
    return _call(None, name, (), [], [], [], comm=Comm("rs", fulls, axes))()[1]


def sum_parts(parts, name):
    _, R, C = parts.shape
    tm = _tile(R, 256)

    def body(p_ref, o_ref):
        acc = p_ref[0].astype(f32)
        for k in range(1, NDEV):
            acc = acc + p_ref[k].astype(f32)
        o_ref[...] = acc

    return _call(body, name, (R // tm,), [pl.BlockSpec((NDEV, tm, C), lambda i: (0, i, 0))], _rowspec(tm, C), SDS((R, C), f32))(parts)


def adamw(w, g, m, v, name):
    L, R, C = w.shape
    tm = _tile(R, max(8, min(512, (ADAMW_TILE_ELEMS // C) // 8 * 8)))
    c1 = 1.0 - ADAM_B1 ** ADAM_STEP
    c2 = 1.0 - ADAM_B2 ** ADAM_STEP

    def body(w_ref, g_ref, m_ref, v_ref, d_ref, nm_ref, nv_ref):
        g_ = g_ref[...]
        nm = ADAM_B1 * m_ref[...] + (1.0 - ADAM_B1) * g_
        nv = ADAM_B2 * v_ref[...] + (1.0 - ADAM_B2) * (g_ * g_)
        d_ref[...] = -ADAM_LR * ((nm / c1) / (jnp.sqrt(nv / c2) + ADAM_EPS) + ADAM_WD * w_ref[...])
        nm_ref[...] = nm
        nv_ref[...] = nv

    spec = pl.BlockSpec((None, tm, C), lambda l, i: (l, i, 0))
    sh = SDS((L, R, C), f32)
    return _call(body, name, (L, R // tm), [spec] * 4, (spec, spec, spec), (sh, sh, sh))(w, g, m, v)


def _pad_to(n, mult):
    return -(-n // mult) * mult


def _even_perm(D):
    half = D // 2
    ndn = half // HEAD_DIM
    o_gate = 3 * half
    o_b = o_gate + half
    o_a = o_b + ndn
    o_fq = o_a + ndn
    o_fpre = o_fq + 4 * half
    return half, ndn, o_b, o_a, o_fq, o_fpre


def _even_to_mine(w):
    D = (w.shape[-1] // 4) // LANES * LANES
    half, ndn, o_b, o_a, o_fq, o_fpre = _even_perm(D)
    small = jnp.concatenate([w[..., o_b:o_b + ndn], w[..., o_a:o_a + ndn], w[..., o_fpre:o_fpre + ndn]], axis=-1)
    small = jnp.pad(small, [(0, 0)] * (w.ndim - 1) + [(0, LANES - 3 * ndn)])
    return jnp.concatenate([w[..., :o_b], w[..., o_fq:o_fpre], small], axis=-1)


def _even_from_mine(w, D):
    half, ndn, o_b, o_a, o_fq, o_fpre = _even_perm(D)
    sm = w[..., 4 * D:]
    return jnp.concatenate([w[..., :o_b], sm[..., :ndn], sm[..., ndn:2 * ndn], w[..., o_b:4 * D], sm[..., 2 * ndn:3 * ndn]], axis=-1)


def _pack_small(parts):
    flat = jnp.concatenate([p.reshape(-1).astype(f32) for p in parts])
    n = flat.shape[0]
    return jnp.pad(flat, (0, _pad_to(n, 8 * LANES) - n)).reshape(-1, LANES)


def _unpack_small(packed, like):
    flat = packed.reshape(-1)
    out, off = [], 0
    for p in like:
        out.append(flat[off:off + p.size].reshape(p.shape))
        off += p.size
    return out


SMALL_NAMES = ("norm_ffn1", "norm_mix", "dn_a_log", "dn_dt_bias", "dn_norm_g", "fox_q_norm_g", "fox_k_norm_g", "fox_f_bias", "norm_ffn2")
BIG_NAMES = ("ffn1_w_gu", "ffn1_w_down", "w_in_even", "dn_conv_w", "w_out_even", "w_in_odd", "w_out_odd", "ffn2_w_gu", "ffn2_w_down")
WEIGHT_NAMES = ("norm_ffn1", "ffn1_w_gu", "ffn1_w_down", "norm_mix", "w_in_even", "dn_conv_w", "dn_a_log", "dn_dt_bias", "dn_norm_g",
                "fox_q_norm_g", "fox_k_norm_g", "fox_f_bias", "w_out_even", "w_in_odd", "w_out_odd", "norm_ffn2", "ffn2_w_gu", "ffn2_w_down")


def kernel(x, norm_ffn1, ffn1_w_gu, ffn1_w_down, norm_mix, w_in_even, dn_conv_w, dn_a_log, dn_dt_bias, dn_norm_g, fox_q_norm_g, fox_k_norm_g, fox_f_bias, w_out_even, w_in_odd, w_out_odd, norm_ffn2, ffn2_w_gu, ffn2_w_down, loss_target, m_norm_ffn1, m_ffn1_w_gu, m_ffn1_w_down, m_norm_mix, m_w_in_even, m_dn_conv_w, m_dn_a_log, m_dn_dt_bias, m_dn_norm_g, m_fox_q_norm_g, m_fox_k_norm_g, m_fox_f_bias, m_w_out_even, m_w_in_odd, m_w_out_odd, m_norm_ffn2, m_ffn2_w_gu, m_ffn2_w_down, v_norm_ffn1, v_ffn1_w_gu, v_ffn1_w_down, v_norm_mix, v_w_in_even, v_dn_conv_w, v_dn_a_log, v_dn_dt_bias, v_dn_norm_g, v_fox_q_norm_g, v_fox_k_norm_g, v_fox_f_bias, v_w_out_even, v_w_in_odd, v_w_out_odd, v_norm_ffn2, v_ffn2_w_gu, v_ffn2_w_down):
    args = dict(locals())
    W = {n: args[n] for n in WEIGHT_NAMES}
    M = {n: args["m_" + n] for n in WEIGHT_NAMES}
    V = {n: args["v_" + n] for n in WEIGHT_NAMES}
    xs = x[0]
    tgt = loss_target[0]
    S, D = xs.shape
    L = norm_ffn1.shape[0]
    n_even = w_in_even.shape[0]
    hs = ffn1_w_down.shape[1]
    hp = _pad_to(hs, LANES)
    me = 4 * lax.axis_index("x") + 2 * lax.axis_index("y") + lax.axis_index("c")

    def prep_gu(w):
        w = w.reshape(L, D, 2, hs)
        return jnp.pad(w, ((0, 0), (0, 0), (0, 0), (0, hp - hs))).reshape(L, D, 2 * hp).astype(bf16)

    def prep_down(w):
        return jnp.pad(w, ((0, 0), (0, hp - hs), (0, 0))).astype(bf16)

    sh_gu1, sh_d1, sh_gu2, sh_d2 = prep_gu(ffn1_w_gu), prep_down(ffn1_w_down), prep_gu(ffn2_w_gu), prep_down(ffn2_w_down)
    sh_ie, sh_oe = _even_to_mine(w_in_even).astype(bf16), w_out_even.astype(bf16)
    sh_io, sh_oo = w_in_odd.astype(bf16), w_out_odd.astype(bf16)

    def layer_shards(l):
        j = l // 2
        mix = [sh_ie[j], sh_oe[j]] if l % 2 == 0 else [sh_io[j], sh_oo[j]]
        return [sh_gu1[l], sh_d1[l], *mix, sh_gu2[l], sh_d2[l]]

    def layer_axes(l):
        return [1, 0, 0 if l % 2 == 0 else 1, 0, 1, 0]

    def layer_names(l):
        return ["ffn1_w_gu", "ffn1_w_down", *(["w_in_even", "w_out_even"] if l % 2 == 0 else ["w_in_odd", "w_out_odd"]),
                "ffn2_w_gu", "ffn2_w_down"]

    FOX_CARRIES, DN_CARRIES = (0, 4, 3), (1, 2, 5)

    def split_comm(kind, arrays, axes):
        return (Comm(kind, [arrays[i] for i in FOX_CARRIES], [axes[i] for i in FOX_CARRIES]),
                Comm(kind, [arrays[i] for i in DN_CARRIES], [axes[i] for i in DN_CARRIES]))

    def join_split(got_fox, got_dn):
        out = [None] * 6
        for i, a in zip(FOX_CARRIES, got_fox):
            out[i] = a
        for i, a in zip(DN_CARRIES, got_dn):
            out[i] = a
        return out

    first = all_gather(layer_shards(0) + [dn_conv_w[None]], layer_axes(0) + [0], "ag_layer0", kind="ag2")
    weights = {0: first[:6]}
    convw = jnp.moveaxis(first[6], 0, 2).reshape(n_even, CONV_WIDTH, -1)

    EVEN_FWD_CARRIES = dict(f1_gu=(), mx_in=(), dn=(1, 2, 5), fox=(0, 4, 3), f2_gu=(), f2_down=())
    saved = []
    cur = xs
    for l in range(L):
        j = l // 2
        wgu1, wd1, win, wout, wgu2, wd2 = weights[l]
        nxt = l + 1 < L
        x0 = cur
        if l % 2 == 0:
            sh_n, ax_n = (layer_shards(l + 1), layer_axes(l + 1)) if nxt else (None, None)
            cm = {k: (Comm("ag2", [sh_n[i] for i in idx], [ax_n[i] for i in idx]) if nxt and idx else None)
                  for k, idx in EVEN_FWD_CARRIES.items()}
            x1, s1, got_f1gu, _ = ffn_fwd(x0, norm_ffn1[l:l + 1], wgu1, wd1, f"l{l}f1", comm_gu=cm["f1_gu"])
            x2, sm, got_f, got_d, got_in = even_mixer_fwd(
                x1, norm_mix[l:l + 1], win, wout, convw[j], dn_a_log[j], dn_dt_bias[j], dn_norm_g[j:j + 1],
                fox_q_norm_g[j:j + 1], fox_k_norm_g[j:j + 1], fox_f_bias[j], f"l{l}mx", comm_fox=cm["fox"], comm_dn=cm["dn"],
                comm_in=cm["mx_in"])
            x3, s2, got_f2gu, got_f2down = ffn_fwd(x2, norm_ffn2[l:l + 1], wgu2, wd2, f"l{l}f2", comm_gu=cm["f2_gu"],
                                                   comm_down=cm["f2_down"])
            if nxt:
                got = dict(f1_gu=got_f1gu, mx_in=got_in, dn=got_d, fox=got_f, f2_gu=got_f2gu, f2_down=got_f2down)
                weights[l + 1] = [None] * 6
                for k, idx in EVEN_FWD_CARRIES.items():
                    for i, a in zip(idx, got[k]):
                        weights[l + 1][i] = a
        else:
            x1, s1, _, _ = ffn_fwd(x0, norm_ffn1[l:l + 1], wgu1, wd1, f"l{l}f1")
            cm = Comm("ag2", layer_shards(l + 1), layer_axes(l + 1)) if nxt else None
            x2, sm, got = odd_mixer_fwd(x1, norm_mix[l:l + 1], win, wout, f"l{l}mx", comm=cm)
            if nxt:
                weights[l + 1] = got
            x3, s2, _, _ = ffn_fwd(x2, norm_ffn2[l:l + 1], wgu2, wd2, f"l{l}f2")
        saved.append((x0, x1, x2, s1, sm, s2))
        cur = x3
    dy, loss_row = loss_head(cur, tgt, "loss")

    gsmall = {n: [None] * W[n].shape[0] for n in SMALL_NAMES}
    gconv = [None] * n_even
    parts = {n: [None] * W[n].shape[0] for n in BIG_NAMES if n != "dn_conv_w"}

    def take(l, recv):
        for nm, p in zip(layer_names(l), recv):
            idx = l if nm.startswith("ffn") else l // 2
            parts[nm][idx] = sum_parts(p.reshape(NDEV, -1, p.shape[-1]), f"sum_{nm}_{idx}").reshape(p.shape[1:])

    pending = None
    for l in reversed(range(L)):
        j = l // 2
        wgu1, wd1, win, wout, wgu2, wd2 = weights[l]
        x0, x1, x2, s1, sm, s2 = saved[l]
        dy, dg, dwgu2, dwd2 = ffn_bwd(x2, norm_ffn2[l:l + 1], wgu2, wd2, s2, dy, f"l{l}f2")
        gsmall["norm_ffn2"][l] = dg[0]
        if l % 2 == 0:
            cf, cd = split_comm("rs", pending, layer_axes(l + 1)) if pending is not None else (None, None)
            dy, dwin, dwout, sg, got_f, got_d = even_mixer_bwd(
                x1, norm_mix[l:l + 1], win, wout, convw[j], dn_norm_g[j:j + 1], fox_q_norm_g[j:j + 1],
                fox_k_norm_g[j:j + 1], sm, dy, f"l{l}mx", comm_fox=cf, comm_dn=cd)
            if pending is not None:
                take(l + 1, join_split(got_f, got_d))
            gconv[j] = sg.pop("dn_conv_w")
            for k_, v_ in sg.items():
                gsmall[k_][l if k_ == "norm_mix" else j] = v_
        else:
            cm = Comm("rs", pending, layer_axes(l + 1)) if pending is not None else None
            dy, dwin, dwout, sg, got = odd_mixer_bwd(x1, norm_mix[l:l + 1], win, wout, sm, dy, f"l{l}mx", comm=cm)
            if pending is not None:
                take(l + 1, got)
            gsmall["norm_mix"][l] = sg["norm_mix"]
        dy, dg, dwgu1, dwd1 = ffn_bwd(x0, norm_ffn1[l:l + 1], wgu1, wd1, s1, dy, f"l{l}f1")
        gsmall["norm_ffn1"][l] = dg[0]
        pending = [dwgu1, dwd1, dwin, dwout, dwgu2, dwd2]
    take(0, scatter_parts(pending, layer_axes(0), "rs_layer0"))
    grad_x = dy[None]

    small_list = [jnp.stack(gsmall[n]) for n in SMALL_NAMES] + [jnp.stack(gconv), loss_row[0, :1]]
    packed = _pack_small(small_list)
    gathered = all_gather([packed], [0], "ag_small")[0]
    summed = sum_parts(gathered.reshape(NDEV, packed.shape[0], LANES), "sum_small")
    small_sum = _unpack_small(summed, small_list)
    G = dict(zip(SMALL_NAMES, small_sum[:len(SMALL_NAMES)]))
    conv_full = small_sum[len(SMALL_NAMES)]
    cwid = dn_conv_w.shape[2]
    G["dn_conv_w"] = lax.dynamic_slice_in_dim(conv_full, me * cwid, cwid, axis=2)
    loss = small_sum[-1][0]

    def unprep_gu(g):
        return g.reshape(L, D, 2, hp)[..., :hs].reshape(L, D, 2 * hs)

    G["ffn1_w_gu"] = unprep_gu(jnp.stack(parts["ffn1_w_gu"]))
    G["ffn2_w_gu"] = unprep_gu(jnp.stack(parts["ffn2_w_gu"]))
    G["ffn1_w_down"] = jnp.stack(parts["ffn1_w_down"])[:, :hs]
    G["ffn2_w_down"] = jnp.stack(parts["ffn2_w_down"])[:, :hs]
    G["w_in_even"] = _even_from_mine(jnp.stack(parts["w_in_even"]), D)
    G["w_out_even"] = jnp.stack(parts["w_out_even"])
    G["w_in_odd"] = jnp.stack(parts["w_in_odd"])
    G["w_out_odd"] = jnp.stack(parts["w_out_odd"])

    delta, new_m, new_v = {}, {}, {}
    for n in BIG_NAMES:
        delta[n], new_m[n], new_v[n] = adamw(W[n], G[n], M[n], V[n], f"adamw_{n}")
    sw = [W[n] for n in SMALL_NAMES]
    d_, m_, v_ = adamw(_pack_small(sw)[None], _pack_small([G[n] for n in SMALL_NAMES])[None],
                       _pack_small([M[n] for n in SMALL_NAMES])[None], _pack_small([V[n] for n in SMALL_NAMES])[None], "adamw_small")
    for n, a, b, c_ in zip(SMALL_NAMES, _unpack_small(d_, sw), _unpack_small(m_, sw), _unpack_small(v_, sw)):
        delta[n], new_m[n], new_v[n] = a, b, c_

    return (loss, grad_x, *[G[n] for n in WEIGHT_NAMES], *[delta[n] for n in WEIGHT_NAMES],
            *[new_m[n] for n in WEIGHT_NAMES], *[new_v[n] for n in WEIGHT_NAMES])
```

```python
import functools
import math

import jax
import jax.numpy as jnp
from jax import lax
from jax.experimental import pallas as pl
from jax.experimental.pallas import tpu as pltpu

f32 = jnp.float32
bf16 = jnp.bfloat16
SDS = jax.ShapeDtypeStruct

HEAD_DIM = 128
LANES = 128
CONV_WIDTH = 4
DN_CHUNK = 128
EPS = 1e-6
NDEV = 8
VMEM_LIMIT_BYTES = 56 * 1024 * 1024
HI = lax.Precision.HIGHEST
ADAMW_TILE_ELEMS = 384 * 1024

ADAM_LR = 0.001
ADAM_B1 = 0.9
ADAM_B2 = 0.999
ADAM_EPS = 1e-08
ADAM_WD = 0.01
ADAM_STEP = 10

NN = (((1,), (0,)), ((), ()))
NT = (((1,), (1,)), ((), ()))
TN = (((0,), (0,)), ((), ()))


def _me_and_peers():
    x, y, c = lax.axis_index("x"), lax.axis_index("y"), lax.axis_index("c")
    me = 4 * x + 2 * y + c
    peers = []
    for r in range(1, NDEV):
        px = 1 - x if (r >> 2) & 1 else x
        py = 1 - y if (r >> 1) & 1 else y
        pc = 1 - c if r & 1 else c
        peers.append(((px, py, pc), 4 * px + 2 * py + pc))
    return me, peers


def _slab(ref, axis, width, k):
    idx = [slice(None)] * len(ref.shape)
    idx[axis] = pl.ds(k * width, width)
    return ref.at[tuple(idx)]


class Comm:
    def __init__(self, kind, arrays, axes):
        self.kind, self.arrays, self.axes, self.n = kind, list(arrays), list(axes), len(arrays)

    def out_shape(self):
        out = []
        for a, ax in zip(self.arrays, self.axes):
            shp = list(a.shape)
            if self.kind != "rs":
                shp[ax] *= NDEV
                out.append(SDS(tuple(shp), a.dtype))
            else:
                shp[ax] //= NDEV
                out.append(SDS((NDEV, *shp), a.dtype))
        return out

    def sems(self):
        return [pltpu.SemaphoreType.DMA((self.n, NDEV - 1)), pltpu.SemaphoreType.DMA((self.n, NDEV - 1)),
                pltpu.SemaphoreType.DMA((self.n,))]

    def _src(self, ins, t, to_idx):
        if self.kind == "ag":
            return ins[t]
        return _slab(ins[t], self.axes[t], ins[t].shape[self.axes[t]] // NDEV, to_idx)

    def _dst(self, ins, outs, t, from_idx):
        if self.kind != "rs":
            return _slab(outs[t], self.axes[t], ins[t].shape[self.axes[t]], from_idx)
        return outs[t].at[from_idx]

    def _copies(self, ins, outs, sems, sending):
        send_sems, recv_sems, loc_sems = sems
        me, peers = _me_and_peers()
        local, remote = [], []
        for t in range(self.n):
            local.append(pltpu.make_async_copy(self._src(ins, t, me), self._dst(ins, outs, t, me), loc_sems.at[t]))
            for r, (peer, pidx) in enumerate(peers):
                remote.append(pltpu.make_async_remote_copy(
                    src_ref=self._src(ins, t, pidx), dst_ref=self._dst(ins, outs, t, me if sending else pidx),
                    send_sem=send_sems.at[t, r], recv_sem=recv_sems.at[t, r], device_id=peer,
                    device_id_type=pl.DeviceIdType.MESH))
        return local, remote

    def _two_level(self, ins, outs, sems):
        send_sems, recv_sems, loc_sems = sems
        x, y, c = lax.axis_index("x"), lax.axis_index("y"), lax.axis_index("c")
        me, sibling = (x, y, c), (x, y, 1 - c)
        chips = [(1 - x, y), (x, 1 - y), (1 - x, 1 - y)]
        dev = lambda p: 4 * p[0] + 2 * p[1] + p[2]

        def copy(t, k, block, to, from_shard):
            dst = self._dst(ins, outs, t, dev(block))
            return pltpu.make_async_remote_copy(
                src_ref=ins[t] if from_shard else dst, dst_ref=dst, send_sem=send_sems.at[t, k], recv_sem=recv_sems.at[t, k],
                device_id=to, device_id_type=pl.DeviceIdType.MESH)

        ts = range(self.n)
        local = [pltpu.make_async_copy(ins[t], self._dst(ins, outs, t, dev(me)), loc_sems.at[t]) for t in ts]
        first = [copy(t, 0, me, sibling, True) for t in ts]
        first += [copy(t, 1 + j, me, (*chip, c), True) for t in ts for j, chip in enumerate(chips)]
        return local, first, copy, chips, me, sibling, c

    def start(self, ins, outs, sems):
        if self.kind == "ag2":
            local, first = self._two_level(ins, outs, sems)[:2]
            for cp in local + first:
                cp.start()
            return
        local, remote = self._copies(ins, outs, sems, True)
        for cp in local + remote:
            cp.start()

    def wait(self, ins, outs, sems):
        if self.kind == "ag2":
            local, first, copy, chips, me, sibling, c = self._two_level(ins, outs, sems)
            passed = []
            for t in range(self.n):
                for j, chip in enumerate(chips):
                    copy(t, 1 + j, (*chip, c), me, False).wait_recv()
                    fwd = copy(t, 4 + j, (*chip, c), sibling, False)
                    fwd.start()
                    passed.append(fwd)
            for t in range(self.n):
                copy(t, 0, sibling, me, False).wait_recv()
                for j, chip in enumerate(chips):
                    copy(t, 4 + j, (*chip, 1 - c), me, False).wait_recv()
            for cp in first + passed:
                cp.wait_send()
            for cp in local:
                cp.wait()
            return
        local, remote = self._copies(ins, outs, sems, False)
        for cp in remote:
            cp.wait_recv()
            cp.wait_send()
        for cp in local:
            cp.wait()


def _call(body, name, grid, in_specs, out_specs, out_shape, scratch=(), comm=None):
    params = pltpu.CompilerParams(vmem_limit_bytes=VMEM_LIMIT_BYTES)
    if comm is None:
        return pl.pallas_call(body, name=name, grid=grid, in_specs=in_specs, out_specs=out_specs, out_shape=out_shape,
                              scratch_shapes=list(scratch), compiler_params=params)
    single = not isinstance(out_shape, (tuple, list))
    c_out_specs = [out_specs] if single else list(out_specs)
    c_out_shape = [out_shape] if single else list(out_shape)
    n_in, n_out, n_scr, n = len(in_specs), len(c_out_shape), len(scratch), comm.n
    anyspec = pl.BlockSpec(memory_space=pl.ANY)

    def wrapped(*refs):
        ins, refs = refs[:n_in], refs[n_in:]
        cins, refs = refs[:n], refs[n:]
        outs, refs = refs[:n_out], refs[n_out:]
        couts, refs = refs[:n], refs[n:]
        scr, sems = refs[:n_scr], refs[n_scr:]
        first = functools.reduce(lambda a, b: a & b, [pl.program_id(d) == 0 for d in range(len(grid))], True)
        last = functools.reduce(lambda a, b: a & b, [pl.program_id(d) == grid[d] - 1 for d in range(len(grid))], True)
        if grid:
            pl.when(first)(lambda: comm.start(cins, couts, sems))
            body(*ins, *outs, *scr)
            pl.when(last)(lambda: comm.wait(cins, couts, sems))
        else:
            comm.start(cins, couts, sems)
            if body is not None:
                body(*ins, *outs, *scr)
            comm.wait(cins, couts, sems)

    call = pl.pallas_call(
        wrapped, name=name, grid=grid, in_specs=list(in_specs) + [anyspec] * n, out_specs=c_out_specs + [anyspec] * n,
        out_shape=c_out_shape + comm.out_shape(), scratch_shapes=list(scratch) + comm.sems(), compiler_params=params)

    def run(*args):
        res = call(*args, *comm.arrays)
        mine = res[:n_out]
        return (mine[0] if single else tuple(mine)), list(res[n_out:])

    return run


def _tile(n, want, align=8):
    if n <= want:
        return n
    for t in range(want // align * align, 0, -align):
        if n % t == 0:
            return t
    return n


def _dot(a, b, dims=NN, precision=None):
    return lax.dot_general(a, b, dims, preferred_element_type=f32, precision=precision)


def _logsig(x):
    return jnp.minimum(x, 0.0) - jnp.log(1.0 + jnp.exp(-jnp.abs(x)))


def _softplus(x):
    return jnp.maximum(x, 0.0) + jnp.log(1.0 + jnp.exp(-jnp.abs(x)))


def _rms(x, g):
    return x * lax.rsqrt(jnp.mean(x * x, axis=-1, keepdims=True) + EPS) * g


def _lane_pick(blk, lane_idx):
    lane = lax.broadcasted_iota(jnp.int32, (1, LANES), 1)
    return jnp.sum(jnp.where(lane == lane_idx, blk, 0.0), axis=1, keepdims=True)


def mm(a, b, mode, out_dtype, name, tm=512, tn=512, res=None, scale=1.0, comm=None):
    if mode == "nn":
        (M, K), (_, N) = a.shape, b.shape
    elif mode == "nt":
        (M, K), (N, _) = a.shape, b.shape
    else:
        (K, M), (_, N) = a.shape, b.shape
    tm, tn = _tile(M, tm, LANES), _tile(N, tn, LANES)
    a_spec = pl.BlockSpec((K, tm), lambda j, i: (0, i)) if mode == "tn" else pl.BlockSpec((tm, K), lambda j, i: (i, 0))
    b_spec = pl.BlockSpec((tn, K), lambda j, i: (j, 0)) if mode == "nt" else pl.BlockSpec((K, tn), lambda j, i: (0, j))
    dims = {"nn": NN, "nt": NT, "tn": TN}[mode]
    o_spec = pl.BlockSpec((tm, tn), lambda j, i: (i, j))

    def body(*refs):
        acc = _dot(refs[0][...].astype(bf16), refs[1][...].astype(bf16), dims)
        if scale != 1.0:
            acc = acc * scale
        if res is not None:
            acc = acc + refs[2][...]
        refs[-1][...] = acc.astype(out_dtype)

    ins, specs = [a, b], [a_spec, b_spec]
    if res is not None:
        ins.append(res)
        specs.append(o_spec)
    return _call(body, name, (N // tn, M // tm), specs, o_spec, SDS((M, N), out_dtype), comm=comm)(*ins)


def _rowspec(tm, w, cb=0):
    return pl.BlockSpec((tm, w), lambda i: (i, cb))


def _bspec(w):
    return pl.BlockSpec((1, w), lambda i: (0, 0))


def rms_fwd(x, g, name):
    S, D = x.shape
    tm = _tile(S, 512)

    def body(x_ref, g_ref, h_ref):
        h_ref[...] = _rms(x_ref[...], g_ref[...]).astype(bf16)

    return _call(body, name, (S // tm,), [_rowspec(tm, D), _bspec(D)], _rowspec(tm, D), SDS((S, D), bf16))(x, g)


def _swiglu(g, u):
    return g * jax.nn.sigmoid(g) * u


def ffn_gu_act(h, wgu, name, tm=1024, tn=768, comm=None):
    S, D = h.shape
    W = wgu.shape[1] // 2
    tm, tn = _tile(S, tm, LANES), _tile(W, tn, LANES)
    nb = W // tn

    def body(h_ref, wg_ref, wu_ref, gu_ref, a_ref):
        hb = h_ref[...]
        g = _dot(hb, wg_ref[...])
        u = _dot(hb, wu_ref[...])
        gu_ref[0] = g.astype(bf16)
        gu_ref[1] = u.astype(bf16)
        a_ref[...] = _swiglu(g, u).astype(bf16)

    return _call(body, name, (nb, S // tm),
                 [pl.BlockSpec((tm, D), lambda j, i: (i, 0)), pl.BlockSpec((D, tn), lambda j, i: (0, j)),
                  pl.BlockSpec((D, tn), lambda j, i: (0, nb + j))],
                 (pl.BlockSpec((2, tm, tn), lambda j, i: (0, i, j)), pl.BlockSpec((tm, tn), lambda j, i: (i, j))),
                 (SDS((2, S, W), bf16), SDS((S, W), bf16)), comm=comm)(h, wgu, wgu)


def ffn_bda_act(dy, wd, gu, scale, name, tm=512, tn=768):
    S, D = dy.shape
    W = wd.shape[0]
    tm, tn = _tile(S, tm, LANES), _tile(W, tn, LANES)

    def body(dy_ref, wd_ref, gu_ref, dgu_ref):
        da = _dot(dy_ref[...].astype(bf16), wd_ref[...], NT) * scale
        _, vjp = jax.vjp(_swiglu, gu_ref[0].astype(f32), gu_ref[1].astype(f32))
        dg, du = vjp(da)
        dgu_ref[0] = dg.astype(bf16)
        dgu_ref[1] = du.astype(bf16)

    planes = pl.BlockSpec((2, tm, tn), lambda j, i: (0, i, j))
    return _call(body, name, (W // tn, S // tm),
                 [pl.BlockSpec((tm, D), lambda j, i: (i, 0)), pl.BlockSpec((tn, D), lambda j, i: (j, 0)), planes],
                 planes, SDS((2, S, W), bf16))(dy, wd, gu)


def ffn_bwgu(h, dgu, name, tm=512, tn=768):
    S, D = h.shape
    W = dgu.shape[2]
    tm, tn = _tile(D, tm, LANES), _tile(W, tn, LANES)
    nb = W // tn

    def body(h_ref, d_ref, o_ref):
        o_ref[...] = _dot(h_ref[...], d_ref[...], TN).astype(bf16)

    return _call(body, name, (2 * nb, D // tm),
                 [pl.BlockSpec((S, tm), lambda j, i: (0, i)), pl.BlockSpec((None, S, tn), lambda j, i: (j // nb, 0, j % nb))],
                 pl.BlockSpec((tm, tn), lambda j, i: (i, j)), SDS((D, 2 * W), bf16))(h, dgu)


def nt_rms_bwd(pairs, x, g, dres, name, tm=256):
    S, D = x.shape
    tm = _tile(S, tm)
    n = len(pairs)

    def body(*refs):
        x_ref, g_ref, dres_ref = refs[2 * n:2 * n + 3]
        dx_ref, dxb_ref, dg_ref = refs[2 * n + 3:]
        dh = sum(_dot(refs[2 * k][...].astype(bf16), refs[2 * k + 1][...], NT) for k in range(n))
        _, vjp = jax.vjp(_rms, x_ref[...], g_ref[...])
        dx, dg = vjp(dh)
        dx = dres_ref[...] + dx
        dx_ref[...] = dx
        dxb_ref[...] = dx.astype(bf16)

        @pl.when(pl.program_id(0) == 0)
        def _():
            dg_ref[...] = jnp.zeros_like(dg_ref)

        dg_ref[...] += dg

    arrays, specs = [], []
    for a, a_spec, b, b_spec in pairs:
        arrays += [a, b]
        specs += [a_spec, b_spec]
    dx, dxb, dg = _call(body, name, (S // tm,), specs + [_rowspec(tm, D), _bspec(D), _rowspec(tm, D)],
                        (_rowspec(tm, D), _rowspec(tm, D), _bspec(D)),
                        (SDS((S, D), f32), SDS((S, D), bf16), SDS((1, D), f32)))(*arrays, x, g, dres)
    return (dx, dxb), dg


def ffn_bdh_rms(dgu, wgu, x, g, dres, name, tm=256):
    _, S, W = dgu.shape
    D = wgu.shape[0]
    tm = _tile(S, tm)
    pairs = [(dgu, pl.BlockSpec((None, tm, W), functools.partial(lambda p, i: (p, i, 0), p)),
              wgu, pl.BlockSpec((D, W), functools.partial(lambda p, i: (0, p), p))) for p in range(2)]
    return nt_rms_bwd(pairs, x, g, dres, name, tm)


def mm_bdh_rms(d, w, x, g, dres, name, tm=256):
    S, K = d.shape
    tm = _tile(S, tm)
    return nt_rms_bwd([(d, pl.BlockSpec((tm, K), lambda i: (i, 0)), w, pl.BlockSpec(w.shape, lambda i: (0, 0)))], x, g, dres, name, tm)


def loss_head(y, t, name):
    S, D = y.shape
    tm = _tile(S, 512)

    def body(y_ref, t_ref, dy_ref, dyb_ref, l_ref):
        e = y_ref[...] - t_ref[...]
        dy_ref[...] = e * (1.0 / D)
        dyb_ref[...] = (e * (1.0 / D)).astype(bf16)

        @pl.when(pl.program_id(0) == 0)
        def _():
            l_ref[...] = jnp.zeros_like(l_ref)

        l_ref[...] += jnp.sum(jnp.sum(e * e, axis=1, keepdims=True), axis=0, keepdims=True) * (0.5 / D)

    dy, dyb, loss = _call(body, name, (S // tm,), [_rowspec(tm, D), _rowspec(tm, D)],
                          (_rowspec(tm, D), _rowspec(tm, D), _bspec(LANES)),
                          (SDS((S, D), f32), SDS((S, D), bf16), SDS((1, LANES), f32)))(y, t)
    return (dy, dyb), loss


def ffn_fwd(x, g, wgu, wd, tag, comm_gu=None, comm_down=None):
    h = rms_fwd(x, g, f"{tag}_rms")
    (gu, a), got_gu = _carried(ffn_gu_act(h, wgu, f"{tag}_gu", comm=comm_gu), comm_gu)
    xo, got_down = _carried(mm(a, wd, "nn", f32, f"{tag}_down", tm=512, tn=512, res=x, scale=0.5, comm=comm_down), comm_down)
    return xo, (h, gu, a), got_gu, got_down


def ffn_bwd(x, g, wgu, wd, saved, dy, tag):
    h, gu, a = saved
    dy32, dy16 = dy
    dgu = ffn_bda_act(dy16, wd, gu, 0.5, f"{tag}_bda")
    dwd = mm(a, dy16, "tn", bf16, f"{tag}_bwd", tm=512, tn=512, scale=0.5)
    dwgu = ffn_bwgu(h, dgu, f"{tag}_bwgu")
    dx, dg = ffn_bdh_rms(dgu, wgu, x, g, dy32, f"{tag}_bdh")
    return dx, dg, dwgu, dwd


def _split_bf16(x):
    hi = x.astype(bf16)
    lo = (x - hi.astype(f32)).astype(bf16)
    return hi, lo


SB_TQ, SB_TK, SB_NSUB = 512, 256, 4
FOX_TK = 256


def _sb_geometry(S, tk=SB_TK):
    tq = min(SB_TQ, S)
    tk = min(tk, tq)
    return tq, tk, tq // SB_NSUB, tq // tk


def _sb_consts(tk):
    r = lax.broadcasted_iota(jnp.int32, (tk, tk), 0)
    c = lax.broadcasted_iota(jnp.int32, (tk, tk), 1)
    return (r > c).astype(bf16), (r < c).astype(bf16)


def _sb_scores(qs, k, kb, tk, rows, masked):
    scale = HEAD_DIM ** -0.5
    z = [_dot(q, k, NT) * scale for q in qs]
    ls = [_logsig(z_) for z_ in z]
    lm = [l - z_ for l, z_ in zip(ls, z)]
    ok = None
    if masked:
        col = kb * tk + lax.broadcasted_iota(jnp.int32, z[0].shape, 1)
        ok = [col < row for row in rows]
        lm = [jnp.where(m, x, 0.0) for m, x in zip(ok, lm)]
    return ls, lm, ok


def _sb_weights(ls, lm, ok, tail, after):
    suf = [_dot(x.astype(bf16), after) for x in lm]
    a = [jnp.exp(l + s_ + t_) for l, s_, t_ in zip(ls, suf, tail)]
    if ok is not None:
        a = [jnp.where(m, x, 0.0) for m, x in zip(ok, a)]
    return a


def sb_fwd(qkv, nh, name, comm=None):
    S = qkv.shape[0]
    tq, tk, rs, nd = _sb_geometry(S)
    nsub = tq // rs

    def body(q_ref, k_ref, v_ref, o_ref, tails_ref):
        i = pl.program_id(1)
        after, _ = _sb_consts(tk)
        qs = [q_ref[pl.ds(s * rs, rs), :] for s in range(nsub)]
        rows = [i * tq + s * rs + lax.broadcasted_iota(jnp.int32, (rs, tk), 0) for s in range(nsub)]

        def tile(kb, carry, masked):
            tail, acc = carry
            off = pl.multiple_of(kb * tk, tk)
            k = k_ref[pl.ds(off, tk), :]
            v = v_ref[pl.ds(off, tk), :]
            ls, lm, ok = _sb_scores(qs, k, kb, tk, rows, masked)
            a = _sb_weights(ls, lm, ok, tail, after)
            tails_ref[pl.ds(kb, 1), :] += jnp.concatenate([t_.T for t_ in tail], axis=1)
            acc = [ac + _dot(a_.astype(bf16), v) for ac, a_ in zip(acc, a)]
            tail = [t_ + jnp.sum(x, axis=1, keepdims=True) for t_, x in zip(tail, lm)]
            return tail, acc

        tails_ref[...] = jnp.zeros_like(tails_ref)
        carry = ([jnp.zeros((rs, 1), f32)] * nsub, [jnp.zeros((rs, HEAD_DIM), f32)] * nsub)
        for d in reversed(range(nd)):
            carry = tile(i * nd + d, carry, True)
        carry = lax.fori_loop(0, i * nd, lambda t, c: tile(i * nd - 1 - t, c, False), carry)
        for s in range(nsub):
            o_ref[pl.ds(s * rs, rs), :] = carry[1][s].astype(o_ref.dtype)

    blk = lambda cb0: pl.BlockSpec((tq, HEAD_DIM), lambda h, i: (i, cb0 + h))
    full = lambda cb0: pl.BlockSpec((S, HEAD_DIM), lambda h, i: (0, cb0 + h))
    tails = pl.BlockSpec((S // tk, tq), lambda h, i: (h, i))
    return _call(body, name, (nh, S // tq), [blk(0), full(nh), full(2 * nh)], (blk(0), tails),
                 (SDS((S, nh * HEAD_DIM), bf16), SDS((nh * (S // tk), S), f32)), comm=comm)(qkv, qkv, qkv)


def sb_bwd(qkv, tails, do, nh, name, comm=None):
    S = qkv.shape[0]
    tq, tk, rs, nd = _sb_geometry(S)
    nsub = tq // rs
    scale = HEAD_DIM ** -0.5

    def body(q_ref, k_ref, v_ref, tails_ref, do_ref, dq_ref, dk_ref, dv_ref):
        i = pl.program_id(1)

        @pl.when(i == 0)
        def _():
            dk_ref[...] = jnp.zeros_like(dk_ref)
            dv_ref[...] = jnp.zeros_like(dv_ref)

        after, before = _sb_consts(tk)
        qs = [q_ref[pl.ds(s * rs, rs), :] for s in range(nsub)]
        do_all = do_ref[...].astype(bf16)
        dos = [do_all[s * rs:(s + 1) * rs] for s in range(nsub)]
        rows = [i * tq + s * rs + lax.broadcasted_iota(jnp.int32, (rs, tk), 0) for s in range(nsub)]

        def sweep2(kb, carry, masked):
            pe, dq = carry
            off = pl.multiple_of(kb * tk, tk)
            k = k_ref[pl.ds(off, tk), :]
            v = v_ref[pl.ds(off, tk), :]
            ls, lm, ok = _sb_scores(qs, k, kb, tk, rows, masked)
            tail_row = tails_ref[pl.ds(kb, 1), :]
            tail = [tail_row[:, s * rs:(s + 1) * rs].T for s in range(nsub)]
            a = _sb_weights(ls, lm, ok, tail, after)
            e = [_dot(d_, v, NT) * a_ for d_, a_ in zip(dos, a)]
            cum_e = [p_ + _dot(e_.astype(bf16), before) for p_, e_ in zip(pe, e)]
            sig = [jnp.exp(l) for l in ls]
            dz = [e_ * (1.0 - sg) - c_ * sg for e_, sg, c_ in zip(e, sig, cum_e)]
            if ok is not None:
                dz = [jnp.where(m, x, 0.0) for m, x in zip(ok, dz)]
            dzb = [(x * scale).astype(bf16) for x in dz]
            dk_ref[pl.ds(off, tk), :] += _dot(jnp.concatenate(dzb, axis=0), q_ref[...], TN)
            dv_ref[pl.ds(off, tk), :] += _dot(jnp.concatenate([a_.astype(bf16) for a_ in a], axis=0), do_all, TN)
            pe = [p_ + jnp.sum(e_, axis=1, keepdims=True) for p_, e_ in zip(pe, e)]
            dq = [g + _dot(x, k) for g, x in zip(dq, dzb)]
            return pe, dq

        carry = ([jnp.zeros((rs, 1), f32)] * nsub, [jnp.zeros((rs, HEAD_DIM), f32)] * nsub)
        carry = lax.fori_loop(0, i * nd, lambda kb, c: sweep2(kb, c, False), carry)
        for d in range(nd):
            carry = sweep2(i * nd + d, carry, True)
        for s in range(nsub):
            dq_ref[pl.ds(s * rs, rs), :] = carry[1][s]

    blk = lambda cb0: pl.BlockSpec((tq, HEAD_DIM), lambda h, i: (i, cb0 + h))
    full = lambda cb0: pl.BlockSpec((S, HEAD_DIM), lambda h, i: (0, cb0 + h))
    sh = SDS((S, nh * HEAD_DIM), f32)
    tails_spec = pl.BlockSpec((S // tk, tq), lambda h, i: (h, i))
    return _call(body, name, (nh, S // tq), [blk(0), full(nh), full(2 * nh), tails_spec, blk(0)], (blk(0), full(0), full(0)),
                 (sh, sh, sh), comm=comm)(qkv, qkv, qkv, tails, do)


def fox_fwd(fq, fk, fv, c, cT, nh, lane0, name, comm=None):
    S = fq.shape[0]
    tq, tk, rs, nd = _sb_geometry(S, FOX_TK)
    nsub = tq // rs
    scale = HEAD_DIM ** -0.5

    def body(q_ref, k_ref, v_ref, c_ref, ct_ref, o_ref, lse_ref):
        h = pl.program_id(0)
        i = pl.program_id(1)
        subs = range(nsub)
        qs = [q_ref[pl.ds(s * rs, rs), :] for s in subs]
        ci = [_lane_pick(c_ref[pl.ds(s * rs, rs), :], lane0 + h) for s in subs]
        rows = [i * tq + s * rs + lax.broadcasted_iota(jnp.int32, (rs, tk), 0) for s in subs]

        def tile(kb, carry, masked):
            m, l, acc = carry
            off = pl.multiple_of(kb * tk, tk)
            k = k_ref[pl.ds(off, tk), :]
            v = v_ref[pl.ds(off, tk), :]
            cj = ct_ref[pl.ds(lane0 % 8 + h, 1), pl.ds(off, tk)]
            sc = [_dot(q, k, NT) * scale + (c_ - cj) for q, c_ in zip(qs, ci)]
            if masked:
                col = kb * tk + lax.broadcasted_iota(jnp.int32, (rs, tk), 1)
                sc = [jnp.where(col <= row, x, -jnp.inf) for x, row in zip(sc, rows)]
            m2 = [jnp.maximum(m_, jnp.max(x, axis=1, keepdims=True)) for m_, x in zip(m, sc)]
            p = [jnp.exp(x - m_) for x, m_ in zip(sc, m2)]
            al = [jnp.exp(a - b) for a, b in zip(m, m2)]
            l = [a * l_ + jnp.sum(p_, axis=1, keepdims=True) for a, l_, p_ in zip(al, l, p)]
            acc = [a * ac + _dot(p_.astype(bf16), v) for a, ac, p_ in zip(al, acc, p)]
            return m2, l, acc

        carry = ([jnp.full((rs, 1), -jnp.inf, f32)] * nsub, [jnp.zeros((rs, 1), f32)] * nsub,
                 [jnp.zeros((rs, HEAD_DIM), f32)] * nsub)
        for d in range(nd):
            carry = tile(i * nd + d, carry, True)
        m, l, acc = lax.fori_loop(0, i * nd, lambda kb, cr: tile(kb, cr, False), carry)
        for s in subs:
            o_ref[pl.ds(s * rs, rs), :] = acc[s] / l[s]
            lse_ref[pl.ds(s * rs, rs), :] = jnp.broadcast_to(m[s] + jnp.log(l[s]), (rs, HEAD_DIM))

    blk = pl.BlockSpec((tq, HEAD_DIM), lambda h, i: (i, h))
    full = pl.BlockSpec((S, HEAD_DIM), lambda h, i: (0, h))
    cspec = pl.BlockSpec((tq, LANES), lambda h, i: (i, 0))
    ctspec = pl.BlockSpec((8, S), lambda h, i: (lane0 // 8, 0))
    sh = SDS((S, nh * HEAD_DIM), f32)
    return _call(body, name, (nh, S // tq), [blk, full, full, cspec, ctspec], (blk, blk), (sh, sh), comm=comm)(fq, fk, fv, c, cT)


def fox_bwd(fq, fk, fv, c, cT, o, lse, do, nh, lane0, name, comm=None):
    S = fq.shape[0]
    tq, tk, rs, nd = _sb_geometry(S, FOX_TK)
    nsub = tq // rs
    scale = HEAD_DIM ** -0.5

    def body(q_ref, k_ref, v_ref, c_ref, ct_ref, o_ref, lse_ref, do_ref, dq_ref, dk_ref, dv_ref, dct_ref):
        h = pl.program_id(0)
        i = pl.program_id(1)

        @pl.when(i == 0)
        def _():
            dk_ref[...] = jnp.zeros_like(dk_ref)
            dv_ref[...] = jnp.zeros_like(dv_ref)

        @pl.when((i == 0) & (h == 0))
        def _():
            dct_ref[...] = jnp.zeros_like(dct_ref)

        subs = range(nsub)
        qs = [q_ref[pl.ds(s * rs, rs), :] for s in subs]
        dof = [do_ref[pl.ds(s * rs, rs), :] for s in subs]
        dos = [x.astype(bf16) for x in dof]
        ci = [_lane_pick(c_ref[pl.ds(s * rs, rs), :], lane0 + h) for s in subs]
        lses = [lse_ref[pl.ds(s * rs, rs), :][:, :1] for s in subs]
        dl = [jnp.sum(x * o_ref[pl.ds(s * rs, rs), :], axis=1, keepdims=True) for s, x in zip(subs, dof)]
        rows = [i * tq + s * rs + lax.broadcasted_iota(jnp.int32, (rs, tk), 0) for s in subs]

        def tile(kb, carry, masked):
            dq, rsum = carry
            off = pl.multiple_of(kb * tk, tk)
            k = k_ref[pl.ds(off, tk), :]
            v = v_ref[pl.ds(off, tk), :]
            cj = ct_ref[pl.ds(lane0 % 8 + h, 1), pl.ds(off, tk)]
            p = [jnp.exp(_dot(q, k, NT) * scale + (c_ - cj) - ls) for q, c_, ls in zip(qs, ci, lses)]
            if masked:
                col = kb * tk + lax.broadcasted_iota(jnp.int32, (rs, tk), 1)
                p = [jnp.where(col <= row, x, 0.0) for x, row in zip(p, rows)]
            ds = [p_ * (_dot(d_, v, NT) - dl_) for p_, d_, dl_ in zip(p, dos, dl)]
            dsb = [(x * scale).astype(bf16) for x in ds]
            dk_ref[pl.ds(off, tk), :] += _dot(jnp.concatenate(dsb, axis=0), q_ref[...], TN)
            dv_ref[pl.ds(off, tk), :] += _dot(jnp.concatenate([p_.astype(bf16) for p_ in p], axis=0),
                                              jnp.concatenate(dos, axis=0), TN)
            dct_ref[pl.ds(lane0 + h, 1), pl.ds(off, tk)] -= sum(jnp.sum(x, axis=0, keepdims=True) for x in ds)
            return ([g + _dot(x, k) for g, x in zip(dq, dsb)],
                    [r_ + jnp.sum(x, axis=1, keepdims=True) for r_, x in zip(rsum, ds)])

        carry = ([jnp.zeros((rs, HEAD_DIM), f32)] * nsub, [jnp.zeros((rs, 1), f32)] * nsub)
        carry = lax.fori_loop(0, i * nd, lambda kb, cr: tile(kb, cr, False), carry)
        for d in range(nd):
            carry = tile(i * nd + d, carry, True)
        dq, rsum = carry
        for s in subs:
            dq_ref[pl.ds(s * rs, rs), :] = dq[s]
        dct_ref[pl.ds(lane0 + h, 1), pl.ds(pl.multiple_of(i * tq, tq), tq)] += jnp.concatenate([r_.T for r_ in rsum], axis=1)

    blk = pl.BlockSpec((tq, HEAD_DIM), lambda h, i: (i, h))
    full = pl.BlockSpec((S, HEAD_DIM), lambda h, i: (0, h))
    cspec = pl.BlockSpec((tq, LANES), lambda h, i: (i, 0))
    ctspec = pl.BlockSpec((8, S), lambda h, i: (lane0 // 8, 0))
    dctspec = pl.BlockSpec((LANES, S), lambda h, i: (0, 0))
    sh = SDS((S, nh * HEAD_DIM), f32)
    return _call(body, name, (nh, S // tq), [blk, full, full, cspec, ctspec, blk, blk, blk], (blk, full, full, dctspec),
                 (sh, sh, sh, SDS((LANES, S), f32)), comm=comm)(fq, fk, fv, c, cT, o, lse, do)


def gates_fwd(proj, small_cb, fbias_row, name, tm=256):
    S = proj.shape[0]
    tm = _tile(S, tm)

    def body(sm_ref, fb_ref, c_ref, ct_ref, carry_ref):
        @pl.when(pl.program_id(0) == 0)
        def _():
            carry_ref[...] = jnp.zeros_like(carry_ref)

        lf = _logsig(sm_ref[...] + fb_ref[...])
        r = lax.broadcasted_iota(jnp.int32, (tm, tm), 0)
        cc = lax.broadcasted_iota(jnp.int32, (tm, tm), 1)
        c = _dot((r >= cc).astype(f32), lf, NN, HI) + carry_ref[...]
        carry_ref[...] += jnp.sum(lf, axis=0, keepdims=True)
        c_ref[...] = c
        ct_ref[...] = c.T

    return _call(body, name, (S // tm,), [_rowspec(tm, LANES, small_cb), _bspec(LANES)],
                 (_rowspec(tm, LANES), pl.BlockSpec((LANES, tm), lambda i: (0, i))),
                 (SDS((S, LANES), f32), SDS((LANES, S), f32)), scratch=[pltpu.VMEM((1, LANES), f32)])(proj, fbias_row)


def gates_bwd(proj, small_cb, fbias_row, dcT, dsm_dn, lane0, nh, name, tm=256):
    S = proj.shape[0]
    tm = _tile(S, tm)
    nt = S // tm

    def body(sm_ref, fb_ref, dct_ref, dsm_ref, o_ref, dfb_ref, carry_ref):
        @pl.when(pl.program_id(0) == 0)
        def _():
            carry_ref[...] = jnp.zeros_like(carry_ref)
            dfb_ref[...] = jnp.zeros_like(dfb_ref)

        dc = dct_ref[...].T
        r = lax.broadcasted_iota(jnp.int32, (tm, tm), 0)
        cc = lax.broadcasted_iota(jnp.int32, (tm, tm), 1)
        dlf = _dot((r <= cc).astype(f32), dc, NN, HI) + carry_ref[...]
        carry_ref[...] += jnp.sum(dc, axis=0, keepdims=True)
        lane = lax.broadcasted_iota(jnp.int32, (1, LANES), 1)
        dpre = jnp.where((lane >= lane0) & (lane < lane0 + nh), dlf * jax.nn.sigmoid(-(sm_ref[...] + fb_ref[...])), 0.0)
        o_ref[...] = dsm_ref[...] + dpre
        dfb_ref[...] += jnp.sum(dpre, axis=0, keepdims=True)

    rev = lambda i: nt - 1 - i
    return _call(body, name, (nt,),
                 [pl.BlockSpec((tm, LANES), lambda i: (rev(i), small_cb)), _bspec(LANES),
                  pl.BlockSpec((LANES, tm), lambda i: (0, rev(i))), pl.BlockSpec((tm, LANES), lambda i: (rev(i), 0))],
                 (pl.BlockSpec((tm, LANES), lambda i: (rev(i), 0)), _bspec(LANES)),
                 (SDS((S, LANES), f32), SDS((1, LANES), f32)), scratch=[pltpu.VMEM((1, LANES), f32)])(proj, fbias_row, dcT, dsm_dn)


PAD_ROWS = 8


def conv_fwd(proj, w, width, name):
    S = proj.shape[0]
    cw = 256 if width % 256 == 0 else 128

    def body(x_ref, w_ref, y_ref, pad_ref):
        pad_ref[pl.ds(0, PAD_ROWS), :] = jnp.zeros((PAD_ROWS, cw), f32)
        pad_ref[pl.ds(PAD_ROWS, S), :] = x_ref[...]
        y = jnp.zeros((S, cw), f32)
        for i in range(CONV_WIDTH):
            y = y + pad_ref[pl.ds(PAD_ROWS - (CONV_WIDTH - 1) + i, S), :] * w_ref[pl.ds(i, 1), :]
        y_ref[...] = y * jax.nn.sigmoid(y)

    spec = pl.BlockSpec((S, cw), lambda j: (0, j))
    return _call(body, name, (width // cw,), [spec, pl.BlockSpec((CONV_WIDTH, cw), lambda j: (0, j))], spec,
                 SDS((S, width), f32), scratch=[pltpu.VMEM((S + PAD_ROWS, cw), f32)])(proj, w)


def conv_bwd(proj, w, dy, name):
    S, width = dy.shape
    cw = 256 if width % 256 == 0 else 128

    def body(x_ref, w_ref, dy_ref, dx_ref, dw_ref, xpad_ref, dpad_ref):
        xpad_ref[pl.ds(0, PAD_ROWS), :] = jnp.zeros((PAD_ROWS, cw), f32)
        xpad_ref[pl.ds(PAD_ROWS, S), :] = x_ref[...]
        y = jnp.zeros((S, cw), f32)
        for i in range(CONV_WIDTH):
            y = y + xpad_ref[pl.ds(PAD_ROWS - (CONV_WIDTH - 1) + i, S), :] * w_ref[pl.ds(i, 1), :]
        sg = jax.nn.sigmoid(y)
        dpre = dy_ref[...] * (sg * (1.0 + y * (1.0 - sg)))
        dpad_ref[pl.ds(S, PAD_ROWS), :] = jnp.zeros((PAD_ROWS, cw), f32)
        dpad_ref[pl.ds(0, S), :] = dpre
        dx = jnp.zeros((S, cw), f32)
        for i in range(CONV_WIDTH):
            dx = dx + dpad_ref[pl.ds(CONV_WIDTH - 1 - i, S), :] * w_ref[pl.ds(i, 1), :]
            dw_ref[pl.ds(i, 1), :] = jnp.sum(dpre * xpad_ref[pl.ds(PAD_ROWS - (CONV_WIDTH - 1) + i, S), :], axis=0, keepdims=True)
        dx_ref[...] = dx

    spec = pl.BlockSpec((S, cw), lambda j: (0, j))
    wspec = pl.BlockSpec((CONV_WIDTH, cw), lambda j: (0, j))
    return _call(body, name, (width // cw,), [spec, wspec, spec], (spec, wspec),
                 (SDS((S, width), f32), SDS((CONV_WIDTH, width), f32)),
                 scratch=[pltpu.VMEM((S + PAD_ROWS, cw), f32), pltpu.VMEM((S + PAD_ROWS, cw), f32)])(proj, w, dy)


def _pdot_raw(a, b, dims, passes):
    if passes == 1:
        return _dot(a.astype(bf16), b.astype(bf16), dims)
    ah, al = _split_bf16(a)
    bh, bl = _split_bf16(b)
    return _dot(ah, bh, dims) + (_dot(ah, bl, dims) + _dot(al, bh, dims))


def _make_pdot(passes):
    @functools.partial(jax.custom_vjp, nondiff_argnums=(2,))
    def pdot(a, b, mode):
        return _pdot_raw(a, b, {"nn": NN, "nt": NT, "tn": TN}[mode], passes)

    def fwd(a, b, mode):
        return pdot(a, b, mode), (a, b)

    def bwd(mode, res, g):
        a, b = res
        if mode == "nn":
            return _pdot_raw(g, b, NT, passes), _pdot_raw(a, g, TN, passes)
        if mode == "nt":
            return _pdot_raw(g, b, NN, passes), _pdot_raw(g, a, TN, passes)
        return _pdot_raw(b, g, NT, passes), _pdot_raw(a, g, NN, passes)

    pdot.defvjp(fwd, bwd)
    return pdot


_dot1 = _make_pdot(1)
_dot3 = _make_pdot(3)


def _each(f, *lists):
    return [f(*xs) for xs in zip(*lists)]


def _dn_chunk(ndn, cq, ck, cv, small, alog, dtb, s0):
    C = cq[0].shape[0]
    hs = list(range(ndn))
    b = [_lane_pick(small, h) for h in hs]
    d = [_lane_pick(small, ndn + h) for h in hs]
    al = [_lane_pick(alog, h) for h in hs]
    dt = [_lane_pick(dtb, h) for h in hs]
    q = _each(lambda x: x * lax.rsqrt(jnp.sum(x * x, axis=1, keepdims=True) + EPS) * (HEAD_DIM ** -0.5), cq)
    k = _each(lambda x: x * lax.rsqrt(jnp.sum(x * x, axis=1, keepdims=True) + EPS), ck)
    beta = _each(jax.nn.sigmoid, b)
    g = _each(lambda al_, d_, dt_: -jnp.exp(al_) * _softplus(d_ + dt_), al, d, dt)
    r = lax.broadcasted_iota(jnp.int32, (C, C), 0)
    c = lax.broadcasted_iota(jnp.int32, (C, C), 1)
    incl, strict = r >= c, r > c
    inclf = incl.astype(f32)
    gc = _each(lambda g_: _dot3(inclf, g_, "nn"), g)
    decay = _each(lambda gc_: jnp.where(incl, jnp.exp(jnp.where(incl, gc_ - gc_.T, 0.0)), 0.0), gc)
    kb = _each(lambda k_, b_: k_ * b_, k, beta)
    a = _each(lambda kb_, k_, dec: jnp.where(strict, _dot3(kb_, k_, "nt") * dec, 0.0), kb, k, decay)
    eye = (r == c).astype(f32)
    t = _each(lambda a_: eye - a_, a)
    p = a
    for _ in range(int(math.log2(C)) - 1):
        p = _each(lambda p_: _dot3(p_, p_, "nn"), p)
        t = _each(lambda t_, p_: t_ + _dot3(t_, p_, "nn"), t, p)
    eg = _each(jnp.exp, gc)
    u = _each(lambda t_, v_, b_: _dot3(t_, v_ * b_, "nn"), t, cv, beta)
    w = _each(lambda t_, kb_, eg_: _dot3(t_, kb_ * eg_, "nn"), t, kb, eg)
    attn = _each(lambda q_, k_, dec: _dot1(q_, k_, "nt") * dec, q, k, decay)
    g_last = _each(lambda g_: jnp.sum(g_, axis=0, keepdims=True), g)
    v_new = _each(lambda u_, w_, s_: u_ - _dot1(w_, s_, "nn"), u, w, s0)
    o = _each(lambda q_, eg_, s_, at, vn: _dot1(q_ * eg_, s_, "nn") + _dot1(at, vn, "nn"), q, eg, s0, attn, v_new)
    s1 = _each(lambda s_, gl, k_, gc_, vn: s_ * jnp.exp(gl) + _dot1(k_ * jnp.exp(gl - gc_), vn, "tn"), s0, g_last, k, gc, v_new)
    return o, s1


def dn_fwd(cqkv, proj, small_cb, alog_row, dt_row, ndn, name, comm=None):
    S = cqkv.shape[0]
    C = DN_CHUNK
    nc = S // C
    half = ndn * HEAD_DIM

    def body(q_ref, k_ref, v_ref, sm_ref, al_ref, dt_ref, o_ref, ss_ref, s_ref):
        @pl.when(pl.program_id(0) == 0)
        def _():
            s_ref[...] = jnp.zeros_like(s_ref)

        sls = [slice(h * HEAD_DIM, (h + 1) * HEAD_DIM) for h in range(ndn)]
        s0 = [s_ref[h] for h in range(ndn)]
        for h in range(ndn):
            ss_ref[h, 0] = s0[h]
        o, s1 = _dn_chunk(ndn, [q_ref[:, sl] for sl in sls], [k_ref[:, sl] for sl in sls], [v_ref[:, sl] for sl in sls],
                          sm_ref[...], al_ref[...], dt_ref[...], s0)
        for h in range(ndn):
            o_ref[:, sls[h]] = o[h]
            s_ref[h] = s1[h]

    blk = lambda cb: pl.BlockSpec((C, half), lambda n: (n, cb))
    row = pl.BlockSpec((1, LANES), lambda n: (0, 0))
    return _call(body, name, (nc,),
                 [blk(0), blk(1), blk(2), pl.BlockSpec((C, LANES), lambda n: (n, small_cb)), row, row],
                 (blk(0), pl.BlockSpec((ndn, 1, HEAD_DIM, HEAD_DIM), lambda n: (0, n, 0, 0))),
                 (SDS((S, half), f32), SDS((ndn, nc, HEAD_DIM, HEAD_DIM), f32)),
                 scratch=[pltpu.VMEM((ndn, HEAD_DIM, HEAD_DIM), f32)], comm=comm)(cqkv, cqkv, cqkv, proj, alog_row, dt_row)


def dn_bwd(cqkv, proj, small_cb, alog_row, dt_row, ssave, do, ndn, name, comm=None):
    S = cqkv.shape[0]
    C = DN_CHUNK
    nc = S // C
    half = ndn * HEAD_DIM

    def body(q_ref, k_ref, v_ref, sm_ref, al_ref, dt_ref, ss_ref, do_ref, dqkv_ref, dsm_ref, dal_ref, ddt_ref, ds_ref):
        @pl.when(pl.program_id(0) == 0)
        def _():
            ds_ref[...] = jnp.zeros_like(ds_ref)
            dal_ref[...] = jnp.zeros_like(dal_ref)
            ddt_ref[...] = jnp.zeros_like(ddt_ref)

        sls = [slice(h * HEAD_DIM, (h + 1) * HEAD_DIM) for h in range(ndn)]
        _, vjp = jax.vjp(functools.partial(_dn_chunk, ndn), [q_ref[:, sl] for sl in sls], [k_ref[:, sl] for sl in sls],
                         [v_ref[:, sl] for sl in sls], sm_ref[...], al_ref[...], dt_ref[...], [ss_ref[h, 0] for h in range(ndn)])
        dq, dk, dv, dsm, dal, ddt, ds0 = vjp(([do_ref[:, sl] for sl in sls], [ds_ref[h] for h in range(ndn)]))
        for h in range(ndn):
            dqkv_ref[:, sls[h]] = dq[h]
            dqkv_ref[:, half + h * HEAD_DIM:half + (h + 1) * HEAD_DIM] = dk[h]
            dqkv_ref[:, 2 * half + h * HEAD_DIM:2 * half + (h + 1) * HEAD_DIM] = dv[h]
            ds_ref[h] = ds0[h]
        dsm_ref[...] = dsm
        dal_ref[...] += dal
        ddt_ref[...] += ddt

    rev = lambda n: nc - 1 - n
    blk = lambda cb: pl.BlockSpec((C, half), lambda n: (rev(n), cb))
    row = pl.BlockSpec((1, LANES), lambda n: (0, 0))
    return _call(body, name, (nc,),
                 [blk(0), blk(1), blk(2), pl.BlockSpec((C, LANES), lambda n: (rev(n), small_cb)), row, row,
                  pl.BlockSpec((ndn, 1, HEAD_DIM, HEAD_DIM), lambda n: (0, rev(n), 0, 0)), blk(0)],
                 (pl.BlockSpec((C, 3 * half), lambda n: (rev(n), 0)), pl.BlockSpec((C, LANES), lambda n: (rev(n), 0)), row, row),
                 (SDS((S, 3 * half), f32), SDS((S, LANES), f32), SDS((1, LANES), f32), SDS((1, LANES), f32)),
                 scratch=[pltpu.VMEM((ndn, HEAD_DIM, HEAD_DIM), f32)], comm=comm)(cqkv, cqkv, cqkv, proj, alog_row, dt_row, ssave, do)


def even_pre(proj, gq, gk, dfx, cb0, name):
    S = proj.shape[0]
    tm = _tile(S, 256)
    nh = dfx // HEAD_DIM

    def body(q_ref, k_ref, v_ref, gq_ref, gk_ref, fq_ref, fk_ref, fv_ref):
        for h in range(nh):
            sl = slice(h * HEAD_DIM, (h + 1) * HEAD_DIM)
            fq_ref[:, sl] = _rms(q_ref[:, sl], gq_ref[...]).astype(bf16)
            fk_ref[:, sl] = _rms(k_ref[:, sl], gk_ref[...]).astype(bf16)
        fv_ref[...] = v_ref[...].astype(bf16)

    sh = SDS((S, dfx), bf16)
    o = _rowspec(tm, dfx)
    return _call(body, name, (S // tm,),
                 [_rowspec(tm, dfx, cb0), _rowspec(tm, dfx, cb0 + 1), _rowspec(tm, dfx, cb0 + 2), _bspec(HEAD_DIM), _bspec(HEAD_DIM)],
                 (o, o, o), (sh, sh, sh))(proj, proj, proj, gq, gk)


def even_pre_bwd(proj, gq, gk, dfq, dfk, dfx, cb0, name):
    S = proj.shape[0]
    tm = _tile(S, 256)
    nh = dfx // HEAD_DIM

    def body(q_ref, k_ref, gq_ref, gk_ref, dfq_ref, dfk_ref, dq_ref, dk_ref, dgq_ref, dgk_ref):
        @pl.when(pl.program_id(0) == 0)
        def _():
            dgq_ref[...] = jnp.zeros_like(dgq_ref)
            dgk_ref[...] = jnp.zeros_like(dgk_ref)

        for h in range(nh):
            sl = slice(h * HEAD_DIM, (h + 1) * HEAD_DIM)
            _, vq = jax.vjp(_rms, q_ref[:, sl], gq_ref[...])
            dq, dgq = vq(dfq_ref[:, sl])
            _, vk = jax.vjp(_rms, k_ref[:, sl], gk_ref[...])
            dk, dgk = vk(dfk_ref[:, sl])
            dq_ref[:, sl] = dq
            dk_ref[:, sl] = dk
            dgq_ref[...] += dgq
            dgk_ref[...] += dgk

    sh = SDS((S, dfx), f32)
    o = _rowspec(tm, dfx)
    g = SDS((1, HEAD_DIM), f32)
    return _call(body, name, (S // tm,),
                 [_rowspec(tm, dfx, cb0), _rowspec(tm, dfx, cb0 + 1), _bspec(HEAD_DIM), _bspec(HEAD_DIM), o, o],
                 (o, o, _bspec(HEAD_DIM), _bspec(HEAD_DIM)), (sh, sh, g, g))(proj, proj, gq, gk, dfq, dfk)


def _dn_out(o, gate, gn):
    return _rms(o, gn) * (gate * jax.nn.sigmoid(gate))


def _fox_out(o, gate):
    return o * jax.nn.sigmoid(gate)


def even_post(o_dn, o_fox, proj, gn, half, cb_dn_gate, cb_fox_gate, name):
    S = proj.shape[0]
    tm = _tile(S, 256)
    nh = half // HEAD_DIM

    def body(od_ref, of_ref, gd_ref, gf_ref, gn_ref, o_ref):
        for h in range(nh):
            sl = slice(h * HEAD_DIM, (h + 1) * HEAD_DIM)
            o_ref[:, sl] = _dn_out(od_ref[:, sl], gd_ref[:, sl], gn_ref[...]).astype(bf16)
        o_ref[:, half:] = _fox_out(of_ref[...], gf_ref[...]).astype(bf16)

    return _call(body, name, (S // tm,),
                 [_rowspec(tm, half), _rowspec(tm, half), _rowspec(tm, half, cb_dn_gate), _rowspec(tm, half, cb_fox_gate), _bspec(HEAD_DIM)],
                 _rowspec(tm, 2 * half), SDS((S, 2 * half), bf16))(o_dn, o_fox, proj, proj, gn)


def even_post_bwd(o_dn, o_fox, proj, gn, dom, half, cb_dn_gate, cb_fox_gate, name):
    S = proj.shape[0]
    tm = _tile(S, 256)
    nh = half // HEAD_DIM

    def body(od_ref, of_ref, gd_ref, gf_ref, gn_ref, dod_ref, dof_ref, dd_ref, df_ref, dgd_ref, dgf_ref, dgn_ref):
        @pl.when(pl.program_id(0) == 0)
        def _():
            dgn_ref[...] = jnp.zeros_like(dgn_ref)

        for h in range(nh):
            sl = slice(h * HEAD_DIM, (h + 1) * HEAD_DIM)
            _, vjp = jax.vjp(_dn_out, od_ref[:, sl], gd_ref[:, sl], gn_ref[...])
            d_o, d_gate, d_gn = vjp(dod_ref[:, sl])
            dd_ref[:, sl] = d_o
            dgd_ref[:, sl] = d_gate
            dgn_ref[...] += d_gn
        _, vjp = jax.vjp(_fox_out, of_ref[...], gf_ref[...])
        d_o, d_gate = vjp(dof_ref[...])
        df_ref[...] = d_o
        dgf_ref[...] = d_gate

    sh = SDS((S, half), f32)
    o = _rowspec(tm, half)
    return _call(body, name, (S // tm,),
                 [o, o, _rowspec(tm, half, cb_dn_gate), _rowspec(tm, half, cb_fox_gate), _bspec(HEAD_DIM),
                  _rowspec(tm, half, 0), _rowspec(tm, half, 1)],
                 (o, o, o, o, _bspec(HEAD_DIM)), (sh, sh, sh, sh, SDS((1, HEAD_DIM), f32)))(o_dn, o_fox, proj, proj, gn, dom, dom)


def _pad_row(v, lane0=0):
    return jnp.pad(v, (lane0, LANES - lane0 - v.shape[0]))[None]


def _carried(res, comm):
    return res if comm is not None else (res, [])


def even_mixer_fwd(x, gmix, w_in, w_out, conv_w, a_log, dt_bias, gn, gq, gk, f_bias, tag, comm_fox=None, comm_dn=None,
                   comm_in=None):
    S, D = x.shape
    half = D // 2
    ndn = half // HEAD_DIM
    small_cb = 4 * D // LANES
    alog_row, dt_row, fb_row = _pad_row(a_log), _pad_row(dt_bias), _pad_row(f_bias, 2 * ndn)
    h = rms_fwd(x, gmix, f"{tag}_rms")
    proj, got_in = _carried(mm(h, w_in, "nn", f32, f"{tag}_in", tm=512, tn=1408, comm=comm_in), comm_in)
    cqkv = conv_fwd(proj, conv_w, 3 * half, f"{tag}_conv")
    (o_dn, ssave), got_dn = _carried(dn_fwd(cqkv, proj, small_cb, alog_row, dt_row, ndn, f"{tag}_dn", comm=comm_dn), comm_dn)
    c, cT = gates_fwd(proj, small_cb, fb_row, f"{tag}_gates")
    fq, fk, fv = even_pre(proj, gq, gk, half, 4, f"{tag}_pre")
    (o_fox, lse), got_fox = _carried(fox_fwd(fq, fk, fv, c, cT, ndn, 2 * ndn, f"{tag}_fox", comm=comm_fox), comm_fox)
    om = even_post(o_dn, o_fox, proj, gn, half, 3, 7, f"{tag}_post")
    xo = mm(om, w_out, "nn", f32, f"{tag}_out", res=x)
    return xo, (h, proj, cqkv, o_dn, ssave, c, cT, fq, fk, fv, o_fox, lse, om, alog_row, dt_row, fb_row), got_fox, got_dn, got_in


def even_mixer_bwd(x, gmix, w_in, w_out, conv_w, gn, gq, gk, saved, dy, tag, comm_fox=None, comm_dn=None):
    S, D = x.shape
    half = D // 2
    ndn = half // HEAD_DIM
    small_cb = 4 * D // LANES
    h, proj, cqkv, o_dn, ssave, c, cT, fq, fk, fv, o_fox, lse, om, alog_row, dt_row, fb_row = saved
    dy32, dy16 = dy
    dom = mm(dy16, w_out, "nt", f32, f"{tag}_bdom")
    dw_out = mm(om, dy16, "tn", bf16, f"{tag}_bwout")
    d_odn, d_ofox, d_dgate, d_fgate, d_gn = even_post_bwd(o_dn, o_fox, proj, gn, dom, half, 3, 7, f"{tag}_bpost")
    (dfq, dfk, dfv, dcT), got_fox = _carried(
        fox_bwd(fq, fk, fv, c, cT, o_fox, lse, d_ofox, ndn, 2 * ndn, f"{tag}_bfox", comm=comm_fox), comm_fox)
    dq_pre, dk_pre, d_gq, d_gk = even_pre_bwd(proj, gq, gk, dfq, dfk, half, 4, f"{tag}_bpre")
    (dcqkv, dsm_dn, d_alog, d_dt), got_dn = _carried(
        dn_bwd(cqkv, proj, small_cb, alog_row, dt_row, ssave, d_odn, ndn, f"{tag}_bdn", comm=comm_dn), comm_dn)
    d_conv_in, d_conv_w = conv_bwd(proj, conv_w, dcqkv, f"{tag}_bconv")
    d_small, d_fb = gates_bwd(proj, small_cb, fb_row, dcT, dsm_dn, 2 * ndn, ndn, f"{tag}_bgates")
    dproj = jnp.concatenate([d_conv_in, d_dgate, dq_pre, dk_pre, dfv, d_fgate, d_small], axis=1).astype(bf16)
    dw_in = mm(h, dproj, "tn", bf16, f"{tag}_bwin", tn=1408)
    dx, d_gmix = mm_bdh_rms(dproj, w_in, x, gmix, dy32, f"{tag}_bdh")
    small = dict(norm_mix=d_gmix[0], dn_conv_w=d_conv_w, dn_a_log=d_alog[0, :ndn], dn_dt_bias=d_dt[0, :ndn],
                 dn_norm_g=d_gn[0], fox_q_norm_g=d_gq[0], fox_k_norm_g=d_gk[0], fox_f_bias=d_fb[0, 2 * ndn:3 * ndn])
    return dx, dw_in, dw_out, small, got_fox, got_dn


def odd_mixer_fwd(x, gmix, w_in, w_out, tag, comm=None):
    S, D = x.shape
    nh = D // HEAD_DIM
    h = rms_fwd(x, gmix, f"{tag}_rms")
    qkv = mm(h, w_in, "nn", bf16, f"{tag}_in", tn=768)
    (o, tails), got = _carried(sb_fwd(qkv, nh, f"{tag}_sb", comm=comm), comm)
    xo = mm(o, w_out, "nn", f32, f"{tag}_out", res=x)
    return xo, (h, qkv, o, tails), got


def odd_mixer_bwd(x, gmix, w_in, w_out, saved, dy, tag, comm=None):
    S, D = x.shape
    nh = D // HEAD_DIM
    h, qkv, o, tails = saved
    dy32, dy16 = dy
    do = mm(dy16, w_out, "nt", f32, f"{tag}_bdo")
    dw_out = mm(o, dy16, "tn", bf16, f"{tag}_bwout")
    (dq, dk, dv), got = _carried(sb_bwd(qkv, tails, do, nh, f"{tag}_bsb", comm=comm), comm)
    dqkv = jnp.concatenate([dq, dk, dv], axis=1).astype(bf16)
    dw_in = mm(h, dqkv, "tn", bf16, f"{tag}_bwin", tn=1536)
    dx, d_gmix = mm_bdh_rms(dqkv, w_in, x, gmix, dy32, f"{tag}_bdh")
    return dx, dw_in, dw_out, dict(norm_mix=d_gmix[0]), got


def all_gather(shards, axes, name, kind="ag"):
    return _call(None, name, (), [], [], [], comm=Comm(kind, shards, axes))()[1]


def scatter_parts(fulls, axes, name):
    return _call(None, name, (), [], [], [], comm=Comm("rs", fulls, axes))()[1]


def sum_parts(parts, name):
    _, R, C = parts.shape
    tm = _tile(R, 256)

    def body(p_ref, o_ref):
        acc = p_ref[0].astype(f32)
        for k in range(1, NDEV):
            acc = acc + p_ref[k].astype(f32)
        o_ref[...] = acc

    return _call(body, name, (R // tm,), [pl.BlockSpec((NDEV, tm, C), lambda i: (0, i, 0))], _rowspec(tm, C), SDS((R, C), f32))(parts)


def adamw(w, g, m, v, name):
    L, R, C = w.shape
    tm = _tile(R, max(8, min(512, (ADAMW_TILE_ELEMS // C) // 8 * 8)))
    c1 = 1.0 - ADAM_B1 ** ADAM_STEP
    c2 = 1.0 - ADAM_B2 ** ADAM_STEP

    def body(w_ref, g_ref, m_ref, v_ref, d_ref, nm_ref, nv_ref):
        g_ = g_ref[...]
        nm = ADAM_B1 * m_ref[...] + (1.0 - ADAM_B1) * g_
        nv = ADAM_B2 * v_ref[...] + (1.0 - ADAM_B2) * (g_ * g_)
        d_ref[...] = -ADAM_LR * ((nm / c1) / (jnp.sqrt(nv / c2) + ADAM_EPS) + ADAM_WD * w_ref[...])
        nm_ref[...] = nm
        nv_ref[...] = nv

    spec = pl.BlockSpec((None, tm, C), lambda l, i: (l, i, 0))
    sh = SDS((L, R, C), f32)
    return _call(body, name, (L, R // tm), [spec] * 4, (spec, spec, spec), (sh, sh, sh))(w, g, m, v)


def _pad_to(n, mult):
    return -(-n // mult) * mult


def _even_perm(D):
    half = D // 2
    ndn = half // HEAD_DIM
    o_gate = 3 * half
    o_b = o_gate + half
    o_a = o_b + ndn
    o_fq = o_a + ndn
    o_fpre = o_fq + 4 * half
    return half, ndn, o_b, o_a, o_fq, o_fpre


def _even_to_mine(w):
    D = (w.shape[-1] // 4) // LANES * LANES
    half, ndn, o_b, o_a, o_fq, o_fpre = _even_perm(D)
    small = jnp.concatenate([w[..., o_b:o_b + ndn], w[..., o_a:o_a + ndn], w[..., o_fpre:o_fpre + ndn]], axis=-1)
    small = jnp.pad(small, [(0, 0)] * (w.ndim - 1) + [(0, LANES - 3 * ndn)])
    return jnp.concatenate([w[..., :o_b], w[..., o_fq:o_fpre], small], axis=-1)


def _even_from_mine(w, D):
    half, ndn, o_b, o_a, o_fq, o_fpre = _even_perm(D)
    sm = w[..., 4 * D:]
    return jnp.concatenate([w[..., :o_b], sm[..., :ndn], sm[..., ndn:2 * ndn], w[..., o_b:4 * D], sm[..., 2 * ndn:3 * ndn]], axis=-1)


def _pack_small(parts):
    flat = jnp.concatenate([p.reshape(-1).astype(f32) for p in parts])
    n = flat.shape[0]
    return jnp.pad(flat, (0, _pad_to(n, 8 * LANES) - n)).reshape(-1, LANES)


def _unpack_small(packed, like):
    flat = packed.reshape(-1)
    out, off = [], 0
    for p in like:
        out.append(flat[off:off + p.size].reshape(p.shape))
        off += p.size
    return out


SMALL_NAMES = ("norm_ffn1", "norm_mix", "dn_a_log", "dn_dt_bias", "dn_norm_g", "fox_q_norm_g", "fox_k_norm_g", "fox_f_bias", "norm_ffn2")
BIG_NAMES = ("ffn1_w_gu", "ffn1_w_down", "w_in_even", "dn_conv_w", "w_out_even", "w_in_odd", "w_out_odd", "ffn2_w_gu", "ffn2_w_down")
WEIGHT_NAMES = ("norm_ffn1", "ffn1_w_gu", "ffn1_w_down", "norm_mix", "w_in_even", "dn_conv_w", "dn_a_log", "dn_dt_bias", "dn_norm_g",
                "fox_q_norm_g", "fox_k_norm_g", "fox_f_bias", "w_out_even", "w_in_odd", "w_out_odd", "norm_ffn2", "ffn2_w_gu", "ffn2_w_down")


def kernel(x, norm_ffn1, ffn1_w_gu, ffn1_w_down, norm_mix, w_in_even, dn_conv_w, dn_a_log, dn_dt_bias, dn_norm_g, fox_q_norm_g, fox_k_norm_g, fox_f_bias, w_out_even, w_in_odd, w_out_odd, norm_ffn2, ffn2_w_gu, ffn2_w_down, loss_target, m_norm_ffn1, m_ffn1_w_gu, m_ffn1_w_down, m_norm_mix, m_w_in_even, m_dn_conv_w, m_dn_a_log, m_dn_dt_bias, m_dn_norm_g, m_fox_q_norm_g, m_fox_k_norm_g, m_fox_f_bias, m_w_out_even, m_w_in_odd, m_w_out_odd, m_norm_ffn2, m_ffn2_w_gu, m_ffn2_w_down, v_norm_ffn1, v_ffn1_w_gu, v_ffn1_w_down, v_norm_mix, v_w_in_even, v_dn_conv_w, v_dn_a_log, v_dn_dt_bias, v_dn_norm_g, v_fox_q_norm_g, v_fox_k_norm_g, v_fox_f_bias, v_w_out_even, v_w_in_odd, v_w_out_odd, v_norm_ffn2, v_ffn2_w_gu, v_ffn2_w_down):
    args = dict(locals())
    W = {n: args[n] for n in WEIGHT_NAMES}
    M = {n: args["m_" + n] for n in WEIGHT_NAMES}
    V = {n: args["v_" + n] for n in WEIGHT_NAMES}
    xs = x[0]
    tgt = loss_target[0]
    S, D = xs.shape
    L = norm_ffn1.shape[0]
    n_even = w_in_even.shape[0]
    hs = ffn1_w_down.shape[1]
    hp = _pad_to(hs, LANES)
    me = 4 * lax.axis_index("x") + 2 * lax.axis_index("y") + lax.axis_index("c")

    def prep_gu(w):
        w = w.reshape(L, D, 2, hs)
        return jnp.pad(w, ((0, 0), (0, 0), (0, 0), (0, hp - hs))).reshape(L, D, 2 * hp).astype(bf16)

    def prep_down(w):
        return jnp.pad(w, ((0, 0), (0, hp - hs), (0, 0))).astype(bf16)

    sh_gu1, sh_d1, sh_gu2, sh_d2 = prep_gu(ffn1_w_gu), prep_down(ffn1_w_down), prep_gu(ffn2_w_gu), prep_down(ffn2_w_down)
    sh_ie, sh_oe = _even_to_mine(w_in_even).astype(bf16), w_out_even.astype(bf16)
    sh_io, sh_oo = w_in_odd.astype(bf16), w_out_odd.astype(bf16)

    def layer_shards(l):
        j = l // 2
        mix = [sh_ie[j], sh_oe[j]] if l % 2 == 0 else [sh_io[j], sh_oo[j]]
        return [sh_gu1[l], sh_d1[l], *mix, sh_gu2[l], sh_d2[l]]

    def layer_axes(l):
        return [1, 0, 0 if l % 2 == 0 else 1, 0, 1, 0]

    def layer_names(l):
        return ["ffn1_w_gu", "ffn1_w_down", *(["w_in_even", "w_out_even"] if l % 2 == 0 else ["w_in_odd", "w_out_odd"]),
                "ffn2_w_gu", "ffn2_w_down"]

    FOX_CARRIES, DN_CARRIES = (0, 4, 3), (1, 2, 5)

    def split_comm(kind, arrays, axes):
        return (Comm(kind, [arrays[i] for i in FOX_CARRIES], [axes[i] for i in FOX_CARRIES]),
                Comm(kind, [arrays[i] for i in DN_CARRIES], [axes[i] for i in DN_CARRIES]))

    def join_split(got_fox, got_dn):
        out = [None] * 6
        for i, a in zip(FOX_CARRIES, got_fox):
            out[i] = a
        for i, a in zip(DN_CARRIES, got_dn):
            out[i] = a
        return out

    first = all_gather(layer_shards(0) + [dn_conv_w[None]], layer_axes(0) + [0], "ag_layer0", kind="ag2")
    weights = {0: first[:6]}
    convw = jnp.moveaxis(first[6], 0, 2).reshape(n_even, CONV_WIDTH, -1)

    EVEN_FWD_CARRIES = dict(f1_gu=(), mx_in=(), dn=(1, 2, 5), fox=(0, 4, 3), f2_gu=(), f2_down=())
    saved = []
    cur = xs
    for l in range(L):
        j = l // 2
        wgu1, wd1, win, wout, wgu2, wd2 = weights[l]
        nxt = l + 1 < L
        x0 = cur
        if l % 2 == 0:
            sh_n, ax_n = (layer_shards(l + 1), layer_axes(l + 1)) if nxt else (None, None)
            cm = {k: (Comm("ag2", [sh_n[i] for i in idx], [ax_n[i] for i in idx]) if nxt and idx else None)
                  for k, idx in EVEN_FWD_CARRIES.items()}
            x1, s1, got_f1gu, _ = ffn_fwd(x0, norm_ffn1[l:l + 1], wgu1, wd1, f"l{l}f1", comm_gu=cm["f1_gu"])
            x2, sm, got_f, got_d, got_in = even_mixer_fwd(
                x1, norm_mix[l:l + 1], win, wout, convw[j], dn_a_log[j], dn_dt_bias[j], dn_norm_g[j:j + 1],
                fox_q_norm_g[j:j + 1], fox_k_norm_g[j:j + 1], fox_f_bias[j], f"l{l}mx", comm_fox=cm["fox"], comm_dn=cm["dn"],
                comm_in=cm["mx_in"])
            x3, s2, got_f2gu, got_f2down = ffn_fwd(x2, norm_ffn2[l:l + 1], wgu2, wd2, f"l{l}f2", comm_gu=cm["f2_gu"],
                                                   comm_down=cm["f2_down"])
            if nxt:
                got = dict(f1_gu=got_f1gu, mx_in=got_in, dn=got_d, fox=got_f, f2_gu=got_f2gu, f2_down=got_f2down)
                weights[l + 1] = [None] * 6
                for k, idx in EVEN_FWD_CARRIES.items():
                    for i, a in zip(idx, got[k]):
                        weights[l + 1][i] = a
        else:
            x1, s1, _, _ = ffn_fwd(x0, norm_ffn1[l:l + 1], wgu1, wd1, f"l{l}f1")
            cm = Comm("ag2", layer_shards(l + 1), layer_axes(l + 1)) if nxt else None
            x2, sm, got = odd_mixer_fwd(x1, norm_mix[l:l + 1], win, wout, f"l{l}mx", comm=cm)
            if nxt:
                weights[l + 1] = got
            x3, s2, _, _ = ffn_fwd(x2, norm_ffn2[l:l + 1], wgu2, wd2, f"l{l}f2")
        saved.append((x0, x1, x2, s1, sm, s2))
        cur = x3
    dy, loss_row = loss_head(cur, tgt, "loss")

    gsmall = {n: [None] * W[n].shape[0] for n in SMALL_NAMES}
    gconv = [None] * n_even
    parts = {n: [None] * W[n].shape[0] for n in BIG_NAMES if n != "dn_conv_w"}

    def take(l, recv):
        for nm, p in zip(layer_names(l), recv):
            idx = l if nm.startswith("ffn") else l // 2
            parts[nm][idx] = sum_parts(p.reshape(NDEV, -1, p.shape[-1]), f"sum_{nm}_{idx}").reshape(p.shape[1:])

    pending = None
    for l in reversed(range(L)):
        j = l // 2
        wgu1, wd1, win, wout, wgu2, wd2 = weights[l]
        x0, x1, x2, s1, sm, s2 = saved[l]
        dy, dg, dwgu2, dwd2 = ffn_bwd(x2, norm_ffn2[l:l + 1], wgu2, wd2, s2, dy, f"l{l}f2")
        gsmall["norm_ffn2"][l] = dg[0]
        if l % 2 == 0:
            cf, cd = split_comm("rs", pending, layer_axes(l + 1)) if pending is not None else (None, None)
            dy, dwin, dwout, sg, got_f, got_d = even_mixer_bwd(
                x1, norm_mix[l:l + 1], win, wout, convw[j], dn_norm_g[j:j + 1], fox_q_norm_g[j:j + 1],
                fox_k_norm_g[j:j + 1], sm, dy, f"l{l}mx", comm_fox=cf, comm_dn=cd)
            if pending is not None:
                take(l + 1, join_split(got_f, got_d))
            gconv[j] = sg.pop("dn_conv_w")
            for k_, v_ in sg.items():
                gsmall[k_][l if k_ == "norm_mix" else j] = v_
        else:
            cm = Comm("rs", pending, layer_axes(l + 1)) if pending is not None else None
            dy, dwin, dwout, sg, got = odd_mixer_bwd(x1, norm_mix[l:l + 1], win, wout, sm, dy, f"l{l}mx", comm=cm)
            if pending is not None:
                take(l + 1, got)
            gsmall["norm_mix"][l] = sg["norm_mix"]
        dy, dg, dwgu1, dwd1 = ffn_bwd(x0, norm_ffn1[l:l + 1], wgu1, wd1, s1, dy, f"l{l}f1")
        gsmall["norm_ffn1"][l] = dg[0]
        pending = [dwgu1, dwd1, dwin, dwout, dwgu2, dwd2]
    take(0, scatter_parts(pending, layer_axes(0), "rs_layer0"))
    grad_x = dy[0][None]

    small_list = [jnp.stack(gsmall[n]) for n in SMALL_NAMES] + [jnp.stack(gconv), loss_row[0, :1]]
    packed = _pack_small(small_list)
    gathered = all_gather([packed], [0], "ag_small")[0]
    summed = sum_parts(gathered.reshape(NDEV, packed.shape[0], LANES), "sum_small")
    small_sum = _unpack_small(summed, small_list)
    G = dict(zip(SMALL_NAMES, small_sum[:len(SMALL_NAMES)]))
    conv_full = small_sum[len(SMALL_NAMES)]
    cwid = dn_conv_w.shape[2]
    G["dn_conv_w"] = lax.dynamic_slice_in_dim(conv_full, me * cwid, cwid, axis=2)
    loss = small_sum[-1][0]

    def unprep_gu(g):
        return g.reshape(L, D, 2, hp)[..., :hs].reshape(L, D, 2 * hs)

    G["ffn1_w_gu"] = unprep_gu(jnp.stack(parts["ffn1_w_gu"]))
    G["ffn2_w_gu"] = unprep_gu(jnp.stack(parts["ffn2_w_gu"]))
    G["ffn1_w_down"] = jnp.stack(parts["ffn1_w_down"])[:, :hs]
    G["ffn2_w_down"] = jnp.stack(parts["ffn2_w_down"])[:, :hs]
    G["w_in_even"] = _even_from_mine(jnp.stack(parts["w_in_even"]), D)
    G["w_out_even"] = jnp.stack(parts["w_out_even"])
    G["w_in_odd"] = jnp.stack(parts["w_in_odd"])
    G["w_out_odd"] = jnp.stack(parts["w_out_odd"])

    delta, new_m, new_v = {}, {}, {}
    for n in BIG_NAMES:
        delta[n], new_m[n], new_v[n] = adamw(W[n], G[n], M[n], V[n], f"adamw_{n}")
    sw = [W[n] for n in SMALL_NAMES]
    d_, m_, v_ = adamw(_pack_small(sw)[None], _pack_small([G[n] for n in SMALL_NAMES])[None],
                       _pack_small([M[n] for n in SMALL_NAMES])[None], _pack_small([V[n] for n in SMALL_NAMES])[None], "adamw_small")
    for n, a, b, c_ in zip(SMALL_NAMES, _unpack_small(d_, sw), _unpack_small(m_, sw), _unpack_small(v_, sw)):
        delta[n], new_m[n], new_v[n] = a, b, c_

    return (loss, grad_x, *[G[n] for n in WEIGHT_NAMES], *[delta[n] for n in WEIGHT_NAMES],
            *[new_m[n] for n in WEIGHT_NAMES], *[new_v[n] for n in WEIGHT_NAMES])
```

```python
import functools
import math

import jax
import jax.numpy as jnp
from jax import lax
from jax.experimental import pallas as pl
from jax.experimental.pallas import tpu as pltpu

f32 = jnp.float32
bf16 = jnp.bfloat16
SDS = jax.ShapeDtypeStruct

HEAD_DIM = 128
LANES = 128
CONV_WIDTH = 4
DN_CHUNK = 128
EPS = 1e-6
NDEV = 8
VMEM_LIMIT_BYTES = 56 * 1024 * 1024
HI = lax.Precision.HIGHEST
ADAMW_TILE_ELEMS = 384 * 1024

ADAM_LR = 0.001
ADAM_B1 = 0.9
ADAM_B2 = 0.999
ADAM_EPS = 1e-08
ADAM_WD = 0.01
ADAM_STEP = 10

NN = (((1,), (0,)), ((), ()))
NT = (((1,), (1,)), ((), ()))
TN = (((0,), (0,)), ((), ()))


def _me_and_peers():
    x, y, c = lax.axis_index("x"), lax.axis_index("y"), lax.axis_index("c")
    me = 4 * x + 2 * y + c
    peers = []
    for r in range(1, NDEV):
        px = 1 - x if (r >> 2) & 1 else x
        py = 1 - y if (r >> 1) & 1 else y
        pc = 1 - c if r & 1 else c
        peers.append(((px, py, pc), 4 * px + 2 * py + pc))
    return me, peers


def _slab(ref, axis, width, k):
    idx = [slice(None)] * len(ref.shape)
    idx[axis] = pl.ds(k * width, width)
    return ref.at[tuple(idx)]


class Comm:
    def __init__(self, kind, arrays, axes):
        self.kind, self.arrays, self.axes, self.n = kind, list(arrays), list(axes), len(arrays)

    def out_shape(self):
        out = []
        for a, ax in zip(self.arrays, self.axes):
            shp = list(a.shape)
            if self.kind != "rs":
                shp[ax] *= NDEV
                out.append(SDS(tuple(shp), a.dtype))
            else:
                shp[ax] //= NDEV
                out.append(SDS((NDEV, *shp), a.dtype))
        return out

    def sems(self):
        return [pltpu.SemaphoreType.DMA((self.n, NDEV - 1)), pltpu.SemaphoreType.DMA((self.n, NDEV - 1)),
                pltpu.SemaphoreType.DMA((self.n,))]

    def _src(self, ins, t, to_idx):
        if self.kind == "ag":
            return ins[t]
        return _slab(ins[t], self.axes[t], ins[t].shape[self.axes[t]] // NDEV, to_idx)

    def _dst(self, ins, outs, t, from_idx):
        if self.kind != "rs":
            return _slab(outs[t], self.axes[t], ins[t].shape[self.axes[t]], from_idx)
        return outs[t].at[from_idx]

    def _copies(self, ins, outs, sems, sending):
        send_sems, recv_sems, loc_sems = sems
        me, peers = _me_and_peers()
        local, remote = [], []
        for t in range(self.n):
            local.append(pltpu.make_async_copy(self._src(ins, t, me), self._dst(ins, outs, t, me), loc_sems.at[t]))
            for r, (peer, pidx) in enumerate(peers):
                remote.append(pltpu.make_async_remote_copy(
                    src_ref=self._src(ins, t, pidx), dst_ref=self._dst(ins, outs, t, me if sending else pidx),
                    send_sem=send_sems.at[t, r], recv_sem=recv_sems.at[t, r], device_id=peer,
                    device_id_type=pl.DeviceIdType.MESH))
        return local, remote

    def _two_level(self, ins, outs, sems):
        send_sems, recv_sems, loc_sems = sems
        x, y, c = lax.axis_index("x"), lax.axis_index("y"), lax.axis_index("c")
        me, sibling = (x, y, c), (x, y, 1 - c)
        chips = [(1 - x, y), (x, 1 - y), (1 - x, 1 - y)]
        dev = lambda p: 4 * p[0] + 2 * p[1] + p[2]

        def copy(t, k, block, to, from_shard):
            dst = self._dst(ins, outs, t, dev(block))
            return pltpu.make_async_remote_copy(
                src_ref=ins[t] if from_shard else dst, dst_ref=dst, send_sem=send_sems.at[t, k], recv_sem=recv_sems.at[t, k],
                device_id=to, device_id_type=pl.DeviceIdType.MESH)

        ts = range(self.n)
        local = [pltpu.make_async_copy(ins[t], self._dst(ins, outs, t, dev(me)), loc_sems.at[t]) for t in ts]
        first = [copy(t, 0, me, sibling, True) for t in ts]
        first += [copy(t, 1 + j, me, (*chip, c), True) for t in ts for j, chip in enumerate(chips)]
        return local, first, copy, chips, me, sibling, c

    def start(self, ins, outs, sems):
        if self.kind == "ag2":
            local, first = self._two_level(ins, outs, sems)[:2]
            for cp in local + first:
                cp.start()
            return
        local, remote = self._copies(ins, outs, sems, True)
        for cp in local + remote:
            cp.start()

    def wait(self, ins, outs, sems):
        if self.kind == "ag2":
            local, first, copy, chips, me, sibling, c = self._two_level(ins, outs, sems)
            passed = []
            for t in range(self.n):
                for j, chip in enumerate(chips):
                    copy(t, 1 + j, (*chip, c), me, False).wait_recv()
                    fwd = copy(t, 4 + j, (*chip, c), sibling, False)
                    fwd.start()
                    passed.append(fwd)
            for t in range(self.n):
                copy(t, 0, sibling, me, False).wait_recv()
                for j, chip in enumerate(chips):
                    copy(t, 4 + j, (*chip, 1 - c), me, False).wait_recv()
            for cp in first + passed:
                cp.wait_send()
            for cp in local:
                cp.wait()
            return
        local, remote = self._copies(ins, outs, sems, False)
        for cp in remote:
            cp.wait_recv()
            cp.wait_send()
        for cp in local:
            cp.wait()


def _call(body, name, grid, in_specs, out_specs, out_shape, scratch=(), comm=None):
    params = pltpu.CompilerParams(vmem_limit_bytes=VMEM_LIMIT_BYTES)
    if comm is None:
        return pl.pallas_call(body, name=name, grid=grid, in_specs=in_specs, out_specs=out_specs, out_shape=out_shape,
                              scratch_shapes=list(scratch), compiler_params=params)
    single = not isinstance(out_shape, (tuple, list))
    c_out_specs = [out_specs] if single else list(out_specs)
    c_out_shape = [out_shape] if single else list(out_shape)
    n_in, n_out, n_scr, n = len(in_specs), len(c_out_shape), len(scratch), comm.n
    anyspec = pl.BlockSpec(memory_space=pl.ANY)

    def wrapped(*refs):
        ins, refs = refs[:n_in], refs[n_in:]
        cins, refs = refs[:n], refs[n:]
        outs, refs = refs[:n_out], refs[n_out:]
        couts, refs = refs[:n], refs[n:]
        scr, sems = refs[:n_scr], refs[n_scr:]
        first = functools.reduce(lambda a, b: a & b, [pl.program_id(d) == 0 for d in range(len(grid))], True)
        last = functools.reduce(lambda a, b: a & b, [pl.program_id(d) == grid[d] - 1 for d in range(len(grid))], True)
        if grid:
            pl.when(first)(lambda: comm.start(cins, couts, sems))
            body(*ins, *outs, *scr)
            pl.when(last)(lambda: comm.wait(cins, couts, sems))
        else:
            comm.start(cins, couts, sems)
            if body is not None:
                body(*ins, *outs, *scr)
            comm.wait(cins, couts, sems)

    call = pl.pallas_call(
        wrapped, name=name, grid=grid, in_specs=list(in_specs) + [anyspec] * n, out_specs=c_out_specs + [anyspec] * n,
        out_shape=c_out_shape + comm.out_shape(), scratch_shapes=list(scratch) + comm.sems(), compiler_params=params)

    def run(*args):
        res = call(*args, *comm.arrays)
        mine = res[:n_out]
        return (mine[0] if single else tuple(mine)), list(res[n_out:])

    return run


def _tile(n, want, align=8):
    if n <= want:
        return n
    for t in range(want // align * align, 0, -align):
        if n % t == 0:
            return t
    return n


def _dot(a, b, dims=NN, precision=None):
    return lax.dot_general(a, b, dims, preferred_element_type=f32, precision=precision)


def _logsig(x):
    return jnp.minimum(x, 0.0) - jnp.log(1.0 + jnp.exp(-jnp.abs(x)))


def _softplus(x):
    return jnp.maximum(x, 0.0) + jnp.log(1.0 + jnp.exp(-jnp.abs(x)))


def _rms(x, g):
    return x * lax.rsqrt(jnp.mean(x * x, axis=-1, keepdims=True) + EPS) * g


def _lane_pick(blk, lane_idx):
    lane = lax.broadcasted_iota(jnp.int32, (1, LANES), 1)
    return jnp.sum(jnp.where(lane == lane_idx, blk, 0.0), axis=1, keepdims=True)


def mm(a, b, mode, out_dtype, name, tm=512, tn=512, res=None, scale=1.0, comm=None):
    if mode == "nn":
        (M, K), (_, N) = a.shape, b.shape
    elif mode == "nt":
        (M, K), (N, _) = a.shape, b.shape
    else:
        (K, M), (_, N) = a.shape, b.shape
    tm, tn = _tile(M, tm, LANES), _tile(N, tn, LANES)
    a_spec = pl.BlockSpec((K, tm), lambda j, i: (0, i)) if mode == "tn" else pl.BlockSpec((tm, K), lambda j, i: (i, 0))
    b_spec = pl.BlockSpec((tn, K), lambda j, i: (j, 0)) if mode == "nt" else pl.BlockSpec((K, tn), lambda j, i: (0, j))
    dims = {"nn": NN, "nt": NT, "tn": TN}[mode]
    o_spec = pl.BlockSpec((tm, tn), lambda j, i: (i, j))

    def body(*refs):
        acc = _dot(refs[0][...].astype(bf16), refs[1][...].astype(bf16), dims)
        if scale != 1.0:
            acc = acc * scale
        if res is not None:
            acc = acc + refs[2][...]
        refs[-1][...] = acc.astype(out_dtype)

    ins, specs = [a, b], [a_spec, b_spec]
    if res is not None:
        ins.append(res)
        specs.append(o_spec)
    return _call(body, name, (N // tn, M // tm), specs, o_spec, SDS((M, N), out_dtype), comm=comm)(*ins)


def _rowspec(tm, w, cb=0):
    return pl.BlockSpec((tm, w), lambda i: (i, cb))


def _bspec(w):
    return pl.BlockSpec((1, w), lambda i: (0, 0))


def rms_fwd(x, g, name):
    S, D = x.shape
    tm = _tile(S, 512)

    def body(x_ref, g_ref, h_ref):
        h_ref[...] = _rms(x_ref[...], g_ref[...]).astype(bf16)

    return _call(body, name, (S // tm,), [_rowspec(tm, D), _bspec(D)], _rowspec(tm, D), SDS((S, D), bf16))(x, g)


def _swiglu(g, u):
    return g * jax.nn.sigmoid(g) * u


def ffn_gu_act(h, wgu, name, tm=1024, tn=768, comm=None):
    S, D = h.shape
    W = wgu.shape[1] // 2
    tm, tn = _tile(S, tm, LANES), _tile(W, tn, LANES)
    nb = W // tn

    def body(h_ref, wg_ref, wu_ref, gu_ref, a_ref):
        hb = h_ref[...]
        g = _dot(hb, wg_ref[...])
        u = _dot(hb, wu_ref[...])
        gu_ref[0] = g.astype(bf16)
        gu_ref[1] = u.astype(bf16)
        a_ref[...] = _swiglu(g, u).astype(bf16)

    return _call(body, name, (nb, S // tm),
                 [pl.BlockSpec((tm, D), lambda j, i: (i, 0)), pl.BlockSpec((D, tn), lambda j, i: (0, j)),
                  pl.BlockSpec((D, tn), lambda j, i: (0, nb + j))],
                 (pl.BlockSpec((2, tm, tn), lambda j, i: (0, i, j)), pl.BlockSpec((tm, tn), lambda j, i: (i, j))),
                 (SDS((2, S, W), bf16), SDS((S, W), bf16)), comm=comm)(h, wgu, wgu)


def ffn_bda_act(dy, wd, gu, scale, name, tm=512, tn=768, comm=None):
    S, D = dy.shape
    W = wd.shape[0]
    tm, tn = _tile(S, tm, LANES), _tile(W, tn, LANES)

    def body(dy_ref, wd_ref, gu_ref, dgu_ref):
        da = _dot(dy_ref[...].astype(bf16), wd_ref[...], NT) * scale
        _, vjp = jax.vjp(_swiglu, gu_ref[0].astype(f32), gu_ref[1].astype(f32))
        dg, du = vjp(da)
        dgu_ref[0] = dg.astype(bf16)
        dgu_ref[1] = du.astype(bf16)

    planes = pl.BlockSpec((2, tm, tn), lambda j, i: (0, i, j))
    return _call(body, name, (W // tn, S // tm),
                 [pl.BlockSpec((tm, D), lambda j, i: (i, 0)), pl.BlockSpec((tn, D), lambda j, i: (j, 0)), planes],
                 planes, SDS((2, S, W), bf16), comm=comm)(dy, wd, gu)


def ffn_bwgu(h, dgu, name, tm=512, tn=768, comm=None):
    S, D = h.shape
    W = dgu.shape[2]
    tm, tn = _tile(D, tm, LANES), _tile(W, tn, LANES)
    nb = W // tn

    def body(h_ref, d_ref, o_ref):
        o_ref[...] = _dot(h_ref[...], d_ref[...], TN).astype(bf16)

    return _call(body, name, (2 * nb, D // tm),
                 [pl.BlockSpec((S, tm), lambda j, i: (0, i)), pl.BlockSpec((None, S, tn), lambda j, i: (j // nb, 0, j % nb))],
                 pl.BlockSpec((tm, tn), lambda j, i: (i, j)), SDS((D, 2 * W), bf16), comm=comm)(h, dgu)


def nt_rms_bwd(pairs, x, g, dres, name, tm=256, comm=None):
    S, D = x.shape
    tm = _tile(S, tm)
    n = len(pairs)

    def body(*refs):
        x_ref, g_ref, dres_ref = refs[2 * n:2 * n + 3]
        dx_ref, dxb_ref, dg_ref = refs[2 * n + 3:]
        dh = sum(_dot(refs[2 * k][...].astype(bf16), refs[2 * k + 1][...], NT) for k in range(n))
        _, vjp = jax.vjp(_rms, x_ref[...], g_ref[...])
        dx, dg = vjp(dh)
        dx = dres_ref[...] + dx
        dx_ref[...] = dx
        dxb_ref[...] = dx.astype(bf16)

        @pl.when(pl.program_id(0) == 0)
        def _():
            dg_ref[...] = jnp.zeros_like(dg_ref)

        dg_ref[...] += dg

    arrays, specs = [], []
    for a, a_spec, b, b_spec in pairs:
        arrays += [a, b]
        specs += [a_spec, b_spec]
    (dx, dxb, dg), got = _carried(_call(body, name, (S // tm,), specs + [_rowspec(tm, D), _bspec(D), _rowspec(tm, D)],
                                        (_rowspec(tm, D), _rowspec(tm, D), _bspec(D)),
                                        (SDS((S, D), f32), SDS((S, D), bf16), SDS((1, D), f32)), comm=comm)(*arrays, x, g, dres),
                                  comm)
    return ((dx, dxb), dg) if comm is None else ((dx, dxb), dg, got)


def ffn_bdh_rms(dgu, wgu, x, g, dres, name, tm=256, comm=None):
    _, S, W = dgu.shape
    D = wgu.shape[0]
    tm = _tile(S, tm)
    pairs = [(dgu, pl.BlockSpec((None, tm, W), functools.partial(lambda p, i: (p, i, 0), p)),
              wgu, pl.BlockSpec((D, W), functools.partial(lambda p, i: (0, p), p))) for p in range(2)]
    return nt_rms_bwd(pairs, x, g, dres, name, tm, comm=comm)


def mm_bdh_rms(d, w, x, g, dres, name, tm=256):
    S, K = d.shape
    tm = _tile(S, tm)
    return nt_rms_bwd([(d, pl.BlockSpec((tm, K), lambda i: (i, 0)), w, pl.BlockSpec(w.shape, lambda i: (0, 0)))], x, g, dres, name, tm)


def loss_head(y, t, name):
    S, D = y.shape
    tm = _tile(S, 512)

    def body(y_ref, t_ref, dy_ref, dyb_ref, l_ref):
        e = y_ref[...] - t_ref[...]
        dy_ref[...] = e * (1.0 / D)
        dyb_ref[...] = (e * (1.0 / D)).astype(bf16)

        @pl.when(pl.program_id(0) == 0)
        def _():
            l_ref[...] = jnp.zeros_like(l_ref)

        l_ref[...] += jnp.sum(jnp.sum(e * e, axis=1, keepdims=True), axis=0, keepdims=True) * (0.5 / D)

    dy, dyb, loss = _call(body, name, (S // tm,), [_rowspec(tm, D), _rowspec(tm, D)],
                          (_rowspec(tm, D), _rowspec(tm, D), _bspec(LANES)),
                          (SDS((S, D), f32), SDS((S, D), bf16), SDS((1, LANES), f32)))(y, t)
    return (dy, dyb), loss


def ffn_fwd(x, g, wgu, wd, tag, comm_gu=None, comm_down=None):
    h = rms_fwd(x, g, f"{tag}_rms")
    (gu, a), got_gu = _carried(ffn_gu_act(h, wgu, f"{tag}_gu", comm=comm_gu), comm_gu)
    xo, got_down = _carried(mm(a, wd, "nn", f32, f"{tag}_down", tm=512, tn=512, res=x, scale=0.5, comm=comm_down), comm_down)
    return xo, (h, gu, a), got_gu, got_down


def ffn_bwd(x, g, wgu, wd, saved, dy, tag, comms=None):
    h, gu, a = saved
    dy32, dy16 = dy
    cm = comms or {}
    dgu, got_bda = _carried(ffn_bda_act(dy16, wd, gu, 0.5, f"{tag}_bda", comm=cm.get("bda")), cm.get("bda"))
    dwgu, got_bwgu = _carried(ffn_bwgu(h, dgu, f"{tag}_bwgu", comm=cm.get("bwgu")), cm.get("bwgu"))
    dwd, got_bwd = _carried(mm(a, dy16, "tn", bf16, f"{tag}_bwd", tm=512, tn=512, scale=0.5, comm=cm.get("bwd")), cm.get("bwd"))
    res = ffn_bdh_rms(dgu, wgu, x, g, dy32, f"{tag}_bdh", comm=cm.get("bdh"))
    dx, dg = res[:2]
    if comms is None:
        return dx, dg, dwgu, dwd
    return dx, dg, dwgu, dwd, dict(bda=got_bda, bwgu=got_bwgu, bwd=got_bwd, bdh=res[2] if len(res) > 2 else [])


def _split_bf16(x):
    hi = x.astype(bf16)
    lo = (x - hi.astype(f32)).astype(bf16)
    return hi, lo


SB_TQ, SB_TK, SB_NSUB = 512, 256, 4
FOX_TK = 256


def _sb_geometry(S, tk=SB_TK):
    tq = min(SB_TQ, S)
    tk = min(tk, tq)
    return tq, tk, tq // SB_NSUB, tq // tk


def _sb_consts(tk):
    r = lax.broadcasted_iota(jnp.int32, (tk, tk), 0)
    c = lax.broadcasted_iota(jnp.int32, (tk, tk), 1)
    return (r > c).astype(bf16), (r < c).astype(bf16)


def _sb_scores(qs, k, kb, tk, rows, masked):
    scale = HEAD_DIM ** -0.5
    z = [_dot(q, k, NT) * scale for q in qs]
    ls = [_logsig(z_) for z_ in z]
    lm = [l - z_ for l, z_ in zip(ls, z)]
    ok = None
    if masked:
        col = kb * tk + lax.broadcasted_iota(jnp.int32, z[0].shape, 1)
        ok = [col < row for row in rows]
        lm = [jnp.where(m, x, 0.0) for m, x in zip(ok, lm)]
    return ls, lm, ok


def _sb_weights(ls, lm, ok, tail, after):
    suf = [_dot(x.astype(bf16), after) for x in lm]
    a = [jnp.exp(l + s_ + t_) for l, s_, t_ in zip(ls, suf, tail)]
    if ok is not None:
        a = [jnp.where(m, x, 0.0) for m, x in zip(ok, a)]
    return a


def sb_fwd(qkv, nh, name, comm=None):
    S = qkv.shape[0]
    tq, tk, rs, nd = _sb_geometry(S)
    nsub = tq // rs

    def body(q_ref, k_ref, v_ref, o_ref, tails_ref):
        i = pl.program_id(1)
        after, _ = _sb_consts(tk)
        qs = [q_ref[pl.ds(s * rs, rs), :] for s in range(nsub)]
        rows = [i * tq + s * rs + lax.broadcasted_iota(jnp.int32, (rs, tk), 0) for s in range(nsub)]

        def tile(kb, carry, masked):
            tail, acc = carry
            off = pl.multiple_of(kb * tk, tk)
            k = k_ref[pl.ds(off, tk), :]
            v = v_ref[pl.ds(off, tk), :]
            ls, lm, ok = _sb_scores(qs, k, kb, tk, rows, masked)
            a = _sb_weights(ls, lm, ok, tail, after)
            tails_ref[pl.ds(kb, 1), :] += jnp.concatenate([t_.T for t_ in tail], axis=1)
            acc = [ac + _dot(a_.astype(bf16), v) for ac, a_ in zip(acc, a)]
            tail = [t_ + jnp.sum(x, axis=1, keepdims=True) for t_, x in zip(tail, lm)]
            return tail, acc

        tails_ref[...] = jnp.zeros_like(tails_ref)
        carry = ([jnp.zeros((rs, 1), f32)] * nsub, [jnp.zeros((rs, HEAD_DIM), f32)] * nsub)
        for d in reversed(range(nd)):
            carry = tile(i * nd + d, carry, True)
        carry = lax.fori_loop(0, i * nd, lambda t, c: tile(i * nd - 1 - t, c, False), carry)
        for s in range(nsub):
            o_ref[pl.ds(s * rs, rs), :] = carry[1][s].astype(o_ref.dtype)

    blk = lambda cb0: pl.BlockSpec((tq, HEAD_DIM), lambda h, i: (i, cb0 + h))
    full = lambda cb0: pl.BlockSpec((S, HEAD_DIM), lambda h, i: (0, cb0 + h))
    tails = pl.BlockSpec((S // tk, tq), lambda h, i: (h, i))
    return _call(body, name, (nh, S // tq), [blk(0), full(nh), full(2 * nh)], (blk(0), tails),
                 (SDS((S, nh * HEAD_DIM), bf16), SDS((nh * (S // tk), S), f32)), comm=comm)(qkv, qkv, qkv)


def sb_bwd(qkv, tails, do, nh, name, comm=None):
    S = qkv.shape[0]
    tq, tk, rs, nd = _sb_geometry(S)
    nsub = tq // rs
    scale = HEAD_DIM ** -0.5

    def body(q_ref, k_ref, v_ref, tails_ref, do_ref, dq_ref, dk_ref, dv_ref):
        i = pl.program_id(1)

        @pl.when(i == 0)
        def _():
            dk_ref[...] = jnp.zeros_like(dk_ref)
            dv_ref[...] = jnp.zeros_like(dv_ref)

        after, before = _sb_consts(tk)
        qs = [q_ref[pl.ds(s * rs, rs), :] for s in range(nsub)]
        dos = [do_ref[pl.ds(s * rs, rs), :].astype(bf16) for s in range(nsub)]
        rows = [i * tq + s * rs + lax.broadcasted_iota(jnp.int32, (rs, tk), 0) for s in range(nsub)]

        def sweep2(kb, carry, masked):
            pe, dq = carry
            off = pl.multiple_of(kb * tk, tk)
            k = k_ref[pl.ds(off, tk), :]
            v = v_ref[pl.ds(off, tk), :]
            ls, lm, ok = _sb_scores(qs, k, kb, tk, rows, masked)
            tail_row = tails_ref[pl.ds(kb, 1), :]
            tail = [tail_row[:, s * rs:(s + 1) * rs].T for s in range(nsub)]
            a = _sb_weights(ls, lm, ok, tail, after)
            e = [_dot(d_, v, NT) * a_ for d_, a_ in zip(dos, a)]
            cum_e = [p_ + _dot(e_.astype(bf16), before) for p_, e_ in zip(pe, e)]
            sig = [jnp.exp(l) for l in ls]
            dz = [e_ * (1.0 - sg) - c_ * sg for e_, sg, c_ in zip(e, sig, cum_e)]
            if ok is not None:
                dz = [jnp.where(m, x, 0.0) for m, x in zip(ok, dz)]
            dzb = [(x * scale).astype(bf16) for x in dz]
            dk_ref[pl.ds(off, tk), :] += sum(_dot(x, q, TN) for x, q in zip(dzb, qs))
            dv_ref[pl.ds(off, tk), :] += sum(_dot(a_.astype(bf16), d_, TN) for a_, d_ in zip(a, dos))
            pe = [p_ + jnp.sum(e_, axis=1, keepdims=True) for p_, e_ in zip(pe, e)]
            dq = [g + _dot(x, k) for g, x in zip(dq, dzb)]
            return pe, dq

        carry = ([jnp.zeros((rs, 1), f32)] * nsub, [jnp.zeros((rs, HEAD_DIM), f32)] * nsub)
        carry = lax.fori_loop(0, i * nd, lambda kb, c: sweep2(kb, c, False), carry)
        for d in range(nd):
            carry = sweep2(i * nd + d, carry, True)
        for s in range(nsub):
            dq_ref[pl.ds(s * rs, rs), :] = carry[1][s]

    blk = lambda cb0: pl.BlockSpec((tq, HEAD_DIM), lambda h, i: (i, cb0 + h))
    full = lambda cb0: pl.BlockSpec((S, HEAD_DIM), lambda h, i: (0, cb0 + h))
    sh = SDS((S, nh * HEAD_DIM), f32)
    tails_spec = pl.BlockSpec((S // tk, tq), lambda h, i: (h, i))
    return _call(body, name, (nh, S // tq), [blk(0), full(nh), full(2 * nh), tails_spec, blk(0)], (blk(0), full(0), full(0)),
                 (sh, sh, sh), comm=comm)(qkv, qkv, qkv, tails, do)


def fox_fwd(fq, fk, fv, c, cT, nh, lane0, name, comm=None):
    S = fq.shape[0]
    tq, tk, rs, nd = _sb_geometry(S, FOX_TK)
    nsub = tq // rs
    scale = HEAD_DIM ** -0.5

    def body(q_ref, k_ref, v_ref, c_ref, ct_ref, o_ref, lse_ref):
        h = pl.program_id(0)
        i = pl.program_id(1)
        subs = range(nsub)
        qs = [q_ref[pl.ds(s * rs, rs), :] for s in subs]
        ci = [_lane_pick(c_ref[pl.ds(s * rs, rs), :], lane0 + h) for s in subs]
        rows = [i * tq + s * rs + lax.broadcasted_iota(jnp.int32, (rs, tk), 0) for s in subs]

        def tile(kb, carry, masked):
            m, l, acc = carry
            off = pl.multiple_of(kb * tk, tk)
            k = k_ref[pl.ds(off, tk), :]
            v = v_ref[pl.ds(off, tk), :]
            cj = ct_ref[pl.ds(lane0 % 8 + h, 1), pl.ds(off, tk)]
            sc = [_dot(q, k, NT) * scale + (c_ - cj) for q, c_ in zip(qs, ci)]
            if masked:
                col = kb * tk + lax.broadcasted_iota(jnp.int32, (rs, tk), 1)
                sc = [jnp.where(col <= row, x, -jnp.inf) for x, row in zip(sc, rows)]
            m2 = [jnp.maximum(m_, jnp.max(x, axis=1, keepdims=True)) for m_, x in zip(m, sc)]
            p = [jnp.exp(x - m_) for x, m_ in zip(sc, m2)]
            al = [jnp.exp(a - b) for a, b in zip(m, m2)]
            l = [a * l_ + jnp.sum(p_, axis=1, keepdims=True) for a, l_, p_ in zip(al, l, p)]
            acc = [a * ac + _dot(p_.astype(bf16), v) for a, ac, p_ in zip(al, acc, p)]
            return m2, l, acc

        carry = ([jnp.full((rs, 1), -jnp.inf, f32)] * nsub, [jnp.zeros((rs, 1), f32)] * nsub,
                 [jnp.zeros((rs, HEAD_DIM), f32)] * nsub)
        for d in range(nd):
            carry = tile(i * nd + d, carry, True)
        m, l, acc = lax.fori_loop(0, i * nd, lambda kb, cr: tile(kb, cr, False), carry)
        for s in subs:
            o_ref[pl.ds(s * rs, rs), :] = acc[s] / l[s]
            lse_ref[pl.ds(s * rs, rs), :] = jnp.broadcast_to(m[s] + jnp.log(l[s]), (rs, HEAD_DIM))

    blk = pl.BlockSpec((tq, HEAD_DIM), lambda h, i: (i, h))
    full = pl.BlockSpec((S, HEAD_DIM), lambda h, i: (0, h))
    cspec = pl.BlockSpec((tq, LANES), lambda h, i: (i, 0))
    ctspec = pl.BlockSpec((8, S), lambda h, i: (lane0 // 8, 0))
    sh = SDS((S, nh * HEAD_DIM), f32)
    return _call(body, name, (nh, S // tq), [blk, full, full, cspec, ctspec], (blk, blk), (sh, sh), comm=comm)(fq, fk, fv, c, cT)


def fox_bwd(fq, fk, fv, c, cT, o, lse, do, nh, lane0, name, comm=None):
    S = fq.shape[0]
    tq, tk, rs, nd = _sb_geometry(S, FOX_TK)
    nsub = tq // rs
    scale = HEAD_DIM ** -0.5

    def body(q_ref, k_ref, v_ref, c_ref, ct_ref, o_ref, lse_ref, do_ref, dq_ref, dk_ref, dv_ref, dct_ref):
        h = pl.program_id(0)
        i = pl.program_id(1)

        @pl.when(i == 0)
        def _():
            dk_ref[...] = jnp.zeros_like(dk_ref)
            dv_ref[...] = jnp.zeros_like(dv_ref)

        @pl.when((i == 0) & (h == 0))
        def _():
            dct_ref[...] = jnp.zeros_like(dct_ref)

        subs = range(nsub)
        qs = [q_ref[pl.ds(s * rs, rs), :] for s in subs]
        dof = [do_ref[pl.ds(s * rs, rs), :] for s in subs]
        dos = [x.astype(bf16) for x in dof]
        ci = [_lane_pick(c_ref[pl.ds(s * rs, rs), :], lane0 + h) for s in subs]
        lses = [lse_ref[pl.ds(s * rs, rs), :][:, :1] for s in subs]
        dl = [jnp.sum(x * o_ref[pl.ds(s * rs, rs), :], axis=1, keepdims=True) for s, x in zip(subs, dof)]
        rows = [i * tq + s * rs + lax.broadcasted_iota(jnp.int32, (rs, tk), 0) for s in subs]

        def tile(kb, carry, masked):
            dq, rsum = carry
            off = pl.multiple_of(kb * tk, tk)
            k = k_ref[pl.ds(off, tk), :]
            v = v_ref[pl.ds(off, tk), :]
            cj = ct_ref[pl.ds(lane0 % 8 + h, 1), pl.ds(off, tk)]
            p = [jnp.exp(_dot(q, k, NT) * scale + (c_ - cj) - ls) for q, c_, ls in zip(qs, ci, lses)]
            if masked:
                col = kb * tk + lax.broadcasted_iota(jnp.int32, (rs, tk), 1)
                p = [jnp.where(col <= row, x, 0.0) for x, row in zip(p, rows)]
            ds = [p_ * (_dot(d_, v, NT) - dl_) for p_, d_, dl_ in zip(p, dos, dl)]
            dsb = [(x * scale).astype(bf16) for x in ds]
            dk_ref[pl.ds(off, tk), :] += sum(_dot(x, q, TN) for x, q in zip(dsb, qs))
            dv_ref[pl.ds(off, tk), :] += sum(_dot(p_.astype(bf16), d_, TN) for p_, d_ in zip(p, dos))
            dct_ref[pl.ds(lane0 + h, 1), pl.ds(off, tk)] -= sum(jnp.sum(x, axis=0, keepdims=True) for x in ds)
            return ([g + _dot(x, k) for g, x in zip(dq, dsb)],
                    [r_ + jnp.sum(x, axis=1, keepdims=True) for r_, x in zip(rsum, ds)])

        carry = ([jnp.zeros((rs, HEAD_DIM), f32)] * nsub, [jnp.zeros((rs, 1), f32)] * nsub)
        carry = lax.fori_loop(0, i * nd, lambda kb, cr: tile(kb, cr, False), carry)
        for d in range(nd):
            carry = tile(i * nd + d, carry, True)
        dq, rsum = carry
        for s in subs:
            dq_ref[pl.ds(s * rs, rs), :] = dq[s]
        dct_ref[pl.ds(lane0 + h, 1), pl.ds(pl.multiple_of(i * tq, tq), tq)] += jnp.concatenate([r_.T for r_ in rsum], axis=1)

    blk = pl.BlockSpec((tq, HEAD_DIM), lambda h, i: (i, h))
    full = pl.BlockSpec((S, HEAD_DIM), lambda h, i: (0, h))
    cspec = pl.BlockSpec((tq, LANES), lambda h, i: (i, 0))
    ctspec = pl.BlockSpec((8, S), lambda h, i: (lane0 // 8, 0))
    dctspec = pl.BlockSpec((LANES, S), lambda h, i: (0, 0))
    sh = SDS((S, nh * HEAD_DIM), f32)
    return _call(body, name, (nh, S // tq), [blk, full, full, cspec, ctspec, blk, blk, blk], (blk, full, full, dctspec),
                 (sh, sh, sh, SDS((LANES, S), f32)), comm=comm)(fq, fk, fv, c, cT, o, lse, do)


def gates_fwd(proj, small_cb, fbias_row, name, tm=256):
    S = proj.shape[0]
    tm = _tile(S, tm)

    def body(sm_ref, fb_ref, c_ref, ct_ref, carry_ref):
        @pl.when(pl.program_id(0) == 0)
        def _():
            carry_ref[...] = jnp.zeros_like(carry_ref)

        lf = _logsig(sm_ref[...] + fb_ref[...])
        r = lax.broadcasted_iota(jnp.int32, (tm, tm), 0)
        cc = lax.broadcasted_iota(jnp.int32, (tm, tm), 1)
        c = _dot((r >= cc).astype(f32), lf, NN, HI) + carry_ref[...]
        carry_ref[...] += jnp.sum(lf, axis=0, keepdims=True)
        c_ref[...] = c
        ct_ref[...] = c.T

    return _call(body, name, (S // tm,), [_rowspec(tm, LANES, small_cb), _bspec(LANES)],
                 (_rowspec(tm, LANES), pl.BlockSpec((LANES, tm), lambda i: (0, i))),
                 (SDS((S, LANES), f32), SDS((LANES, S), f32)), scratch=[pltpu.VMEM((1, LANES), f32)])(proj, fbias_row)


def gates_bwd(proj, small_cb, fbias_row, dcT, dsm_dn, lane0, nh, name, tm=256):
    S = proj.shape[0]
    tm = _tile(S, tm)
    nt = S // tm

    def body(sm_ref, fb_ref, dct_ref, dsm_ref, o_ref, dfb_ref, carry_ref):
        @pl.when(pl.program_id(0) == 0)
        def _():
            carry_ref[...] = jnp.zeros_like(carry_ref)
            dfb_ref[...] = jnp.zeros_like(dfb_ref)

        dc = dct_ref[...].T
        r = lax.broadcasted_iota(jnp.int32, (tm, tm), 0)
        cc = lax.broadcasted_iota(jnp.int32, (tm, tm), 1)
        dlf = _dot((r <= cc).astype(f32), dc, NN, HI) + carry_ref[...]
        carry_ref[...] += jnp.sum(dc, axis=0, keepdims=True)
        lane = lax.broadcasted_iota(jnp.int32, (1, LANES), 1)
        dpre = jnp.where((lane >= lane0) & (lane < lane0 + nh), dlf * jax.nn.sigmoid(-(sm_ref[...] + fb_ref[...])), 0.0)
        o_ref[...] = dsm_ref[...] + dpre
        dfb_ref[...] += jnp.sum(dpre, axis=0, keepdims=True)

    rev = lambda i: nt - 1 - i
    return _call(body, name, (nt,),
                 [pl.BlockSpec((tm, LANES), lambda i: (rev(i), small_cb)), _bspec(LANES),
                  pl.BlockSpec((LANES, tm), lambda i: (0, rev(i))), pl.BlockSpec((tm, LANES), lambda i: (rev(i), 0))],
                 (pl.BlockSpec((tm, LANES), lambda i: (rev(i), 0)), _bspec(LANES)),
                 (SDS((S, LANES), f32), SDS((1, LANES), f32)), scratch=[pltpu.VMEM((1, LANES), f32)])(proj, fbias_row, dcT, dsm_dn)


PAD_ROWS = 8


def conv_fwd(proj, w, width, name):
    S = proj.shape[0]
    cw = 256 if width % 256 == 0 else 128

    def body(x_ref, w_ref, y_ref, pad_ref):
        pad_ref[pl.ds(0, PAD_ROWS), :] = jnp.zeros((PAD_ROWS, cw), f32)
        pad_ref[pl.ds(PAD_ROWS, S), :] = x_ref[...]
        y = jnp.zeros((S, cw), f32)
        for i in range(CONV_WIDTH):
            y = y + pad_ref[pl.ds(PAD_ROWS - (CONV_WIDTH - 1) + i, S), :] * w_ref[pl.ds(i, 1), :]
        y_ref[...] = y * jax.nn.sigmoid(y)

    spec = pl.BlockSpec((S, cw), lambda j: (0, j))
    return _call(body, name, (width // cw,), [spec, pl.BlockSpec((CONV_WIDTH, cw), lambda j: (0, j))], spec,
                 SDS((S, width), f32), scratch=[pltpu.VMEM((S + PAD_ROWS, cw), f32)])(proj, w)


def conv_bwd(proj, w, dy, name):
    S, width = dy.shape
    cw = 256 if width % 256 == 0 else 128

    def body(x_ref, w_ref, dy_ref, dx_ref, dw_ref, xpad_ref, dpad_ref):
        xpad_ref[pl.ds(0, PAD_ROWS), :] = jnp.zeros((PAD_ROWS, cw), f32)
        xpad_ref[pl.ds(PAD_ROWS, S), :] = x_ref[...]
        y = jnp.zeros((S, cw), f32)
        for i in range(CONV_WIDTH):
            y = y + xpad_ref[pl.ds(PAD_ROWS - (CONV_WIDTH - 1) + i, S), :] * w_ref[pl.ds(i, 1), :]
        sg = jax.nn.sigmoid(y)
        dpre = dy_ref[...] * (sg * (1.0 + y * (1.0 - sg)))
        dpad_ref[pl.ds(S, PAD_ROWS), :] = jnp.zeros((PAD_ROWS, cw), f32)
        dpad_ref[pl.ds(0, S), :] = dpre
        dx = jnp.zeros((S, cw), f32)
        for i in range(CONV_WIDTH):
            dx = dx + dpad_ref[pl.ds(CONV_WIDTH - 1 - i, S), :] * w_ref[pl.ds(i, 1), :]
            dw_ref[pl.ds(i, 1), :] = jnp.sum(dpre * xpad_ref[pl.ds(PAD_ROWS - (CONV_WIDTH - 1) + i, S), :], axis=0, keepdims=True)
        dx_ref[...] = dx

    spec = pl.BlockSpec((S, cw), lambda j: (0, j))
    wspec = pl.BlockSpec((CONV_WIDTH, cw), lambda j: (0, j))
    return _call(body, name, (width // cw,), [spec, wspec, spec], (spec, wspec),
                 (SDS((S, width), f32), SDS((CONV_WIDTH, width), f32)),
                 scratch=[pltpu.VMEM((S + PAD_ROWS, cw), f32), pltpu.VMEM((S + PAD_ROWS, cw), f32)])(proj, w, dy)


def _pdot_raw(a, b, dims, passes):
    if passes == 1:
        return _dot(a.astype(bf16), b.astype(bf16), dims)
    ah, al = _split_bf16(a)
    bh, bl = _split_bf16(b)
    return _dot(ah, bh, dims) + (_dot(ah, bl, dims) + _dot(al, bh, dims))


def _make_pdot(passes):
    @functools.partial(jax.custom_vjp, nondiff_argnums=(2,))
    def pdot(a, b, mode):
        return _pdot_raw(a, b, {"nn": NN, "nt": NT, "tn": TN}[mode], passes)

    def fwd(a, b, mode):
        return pdot(a, b, mode), (a, b)

    def bwd(mode, res, g):
        a, b = res
        if mode == "nn":
            return _pdot_raw(g, b, NT, passes), _pdot_raw(a, g, TN, passes)
        if mode == "nt":
            return _pdot_raw(g, b, NN, passes), _pdot_raw(g, a, TN, passes)
        return _pdot_raw(b, g, NT, passes), _pdot_raw(a, g, NN, passes)

    pdot.defvjp(fwd, bwd)
    return pdot


_dot1 = _make_pdot(1)
_dot3 = _make_pdot(3)


def _each(f, *lists):
    return [f(*xs) for xs in zip(*lists)]


def _dn_chunk(ndn, cq, ck, cv, small, alog, dtb, s0):
    C = cq[0].shape[0]
    hs = list(range(ndn))
    b = [_lane_pick(small, h) for h in hs]
    d = [_lane_pick(small, ndn + h) for h in hs]
    al = [_lane_pick(alog, h) for h in hs]
    dt = [_lane_pick(dtb, h) for h in hs]
    q = _each(lambda x: x * lax.rsqrt(jnp.sum(x * x, axis=1, keepdims=True) + EPS) * (HEAD_DIM ** -0.5), cq)
    k = _each(lambda x: x * lax.rsqrt(jnp.sum(x * x, axis=1, keepdims=True) + EPS), ck)
    beta = _each(jax.nn.sigmoid, b)
    g = _each(lambda al_, d_, dt_: -jnp.exp(al_) * _softplus(d_ + dt_), al, d, dt)
    r = lax.broadcasted_iota(jnp.int32, (C, C), 0)
    c = lax.broadcasted_iota(jnp.int32, (C, C), 1)
    incl, strict = r >= c, r > c
    inclf = incl.astype(f32)
    gc = _each(lambda g_: _dot3(inclf, g_, "nn"), g)
    decay = _each(lambda gc_: jnp.where(incl, jnp.exp(jnp.where(incl, gc_ - gc_.T, 0.0)), 0.0), gc)
    kb = _each(lambda k_, b_: k_ * b_, k, beta)
    a = _each(lambda kb_, k_, dec: jnp.where(strict, _dot3(kb_, k_, "nt") * dec, 0.0), kb, k, decay)
    eye = (r == c).astype(f32)
    t = _each(lambda a_: eye - a_, a)
    p = a
    for _ in range(int(math.log2(C)) - 1):
        p = _each(lambda p_: _dot3(p_, p_, "nn"), p)
        t = _each(lambda t_, p_: t_ + _dot3(t_, p_, "nn"), t, p)
    eg = _each(jnp.exp, gc)
    u = _each(lambda t_, v_, b_: _dot3(t_, v_ * b_, "nn"), t, cv, beta)
    w = _each(lambda t_, kb_, eg_: _dot3(t_, kb_ * eg_, "nn"), t, kb, eg)
    attn = _each(lambda q_, k_, dec: _dot1(q_, k_, "nt") * dec, q, k, decay)
    g_last = _each(lambda g_: jnp.sum(g_, axis=0, keepdims=True), g)
    v_new = _each(lambda u_, w_, s_: u_ - _dot1(w_, s_, "nn"), u, w, s0)
    o = _each(lambda q_, eg_, s_, at, vn: _dot1(q_ * eg_, s_, "nn") + _dot1(at, vn, "nn"), q, eg, s0, attn, v_new)
    s1 = _each(lambda s_, gl, k_, gc_, vn: s_ * jnp.exp(gl) + _dot1(k_ * jnp.exp(gl - gc_), vn, "tn"), s0, g_last, k, gc, v_new)
    return o, s1


def dn_fwd(cqkv, proj, small_cb, alog_row, dt_row, ndn, name, comm=None):
    S = cqkv.shape[0]
    C = DN_CHUNK
    nc = S // C
    half = ndn * HEAD_DIM

    def body(q_ref, k_ref, v_ref, sm_ref, al_ref, dt_ref, o_ref, ss_ref, s_ref):
        @pl.when(pl.program_id(0) == 0)
        def _():
            s_ref[...] = jnp.zeros_like(s_ref)

        sls = [slice(h * HEAD_DIM, (h + 1) * HEAD_DIM) for h in range(ndn)]
        s0 = [s_ref[h] for h in range(ndn)]
        for h in range(ndn):
            ss_ref[h, 0] = s0[h]
        o, s1 = _dn_chunk(ndn, [q_ref[:, sl] for sl in sls], [k_ref[:, sl] for sl in sls], [v_ref[:, sl] for sl in sls],
                          sm_ref[...], al_ref[...], dt_ref[...], s0)
        for h in range(ndn):
            o_ref[:, sls[h]] = o[h]
            s_ref[h] = s1[h]

    blk = lambda cb: pl.BlockSpec((C, half), lambda n: (n, cb))
    row = pl.BlockSpec((1, LANES), lambda n: (0, 0))
    return _call(body, name, (nc,),
                 [blk(0), blk(1), blk(2), pl.BlockSpec((C, LANES), lambda n: (n, small_cb)), row, row],
                 (blk(0), pl.BlockSpec((ndn, 1, HEAD_DIM, HEAD_DIM), lambda n: (0, n, 0, 0))),
                 (SDS((S, half), f32), SDS((ndn, nc, HEAD_DIM, HEAD_DIM), f32)),
                 scratch=[pltpu.VMEM((ndn, HEAD_DIM, HEAD_DIM), f32)], comm=comm)(cqkv, cqkv, cqkv, proj, alog_row, dt_row)


def dn_bwd(cqkv, proj, small_cb, alog_row, dt_row, ssave, do, ndn, name, comm=None):
    S = cqkv.shape[0]
    C = DN_CHUNK
    nc = S // C
    half = ndn * HEAD_DIM

    def body(q_ref, k_ref, v_ref, sm_ref, al_ref, dt_ref, ss_ref, do_ref, dqkv_ref, dsm_ref, dal_ref, ddt_ref, ds_ref):
        @pl.when(pl.program_id(0) == 0)
        def _():
            ds_ref[...] = jnp.zeros_like(ds_ref)
            dal_ref[...] = jnp.zeros_like(dal_ref)
            ddt_ref[...] = jnp.zeros_like(ddt_ref)

        sls = [slice(h * HEAD_DIM, (h + 1) * HEAD_DIM) for h in range(ndn)]
        _, vjp = jax.vjp(functools.partial(_dn_chunk, ndn), [q_ref[:, sl] for sl in sls], [k_ref[:, sl] for sl in sls],
                         [v_ref[:, sl] for sl in sls], sm_ref[...], al_ref[...], dt_ref[...], [ss_ref[h, 0] for h in range(ndn)])
        dq, dk, dv, dsm, dal, ddt, ds0 = vjp(([do_ref[:, sl] for sl in sls], [ds_ref[h] for h in range(ndn)]))
        for h in range(ndn):
            dqkv_ref[:, sls[h]] = dq[h]
            dqkv_ref[:, half + h * HEAD_DIM:half + (h + 1) * HEAD_DIM] = dk[h]
            dqkv_ref[:, 2 * half + h * HEAD_DIM:2 * half + (h + 1) * HEAD_DIM] = dv[h]
            ds_ref[h] = ds0[h]
        dsm_ref[...] = dsm
        dal_ref[...] += dal
        ddt_ref[...] += ddt

    rev = lambda n: nc - 1 - n
    blk = lambda cb: pl.BlockSpec((C, half), lambda n: (rev(n), cb))
    row = pl.BlockSpec((1, LANES), lambda n: (0, 0))
    return _call(body, name, (nc,),
                 [blk(0), blk(1), blk(2), pl.BlockSpec((C, LANES), lambda n: (rev(n), small_cb)), row, row,
                  pl.BlockSpec((ndn, 1, HEAD_DIM, HEAD_DIM), lambda n: (0, rev(n), 0, 0)), blk(0)],
                 (pl.BlockSpec((C, 3 * half), lambda n: (rev(n), 0)), pl.BlockSpec((C, LANES), lambda n: (rev(n), 0)), row, row),
                 (SDS((S, 3 * half), f32), SDS((S, LANES), f32), SDS((1, LANES), f32), SDS((1, LANES), f32)),
                 scratch=[pltpu.VMEM((ndn, HEAD_DIM, HEAD_DIM), f32)], comm=comm)(cqkv, cqkv, cqkv, proj, alog_row, dt_row, ssave, do)


def even_pre(proj, gq, gk, dfx, cb0, name):
    S = proj.shape[0]
    tm = _tile(S, 256)
    nh = dfx // HEAD_DIM

    def body(q_ref, k_ref, v_ref, gq_ref, gk_ref, fq_ref, fk_ref, fv_ref):
        for h in range(nh):
            sl = slice(h * HEAD_DIM, (h + 1) * HEAD_DIM)
            fq_ref[:, sl] = _rms(q_ref[:, sl], gq_ref[...]).astype(bf16)
            fk_ref[:, sl] = _rms(k_ref[:, sl], gk_ref[...]).astype(bf16)
        fv_ref[...] = v_ref[...].astype(bf16)

    sh = SDS((S, dfx), bf16)
    o = _rowspec(tm, dfx)
    return _call(body, name, (S // tm,),
                 [_rowspec(tm, dfx, cb0), _rowspec(tm, dfx, cb0 + 1), _rowspec(tm, dfx, cb0 + 2), _bspec(HEAD_DIM), _bspec(HEAD_DIM)],
                 (o, o, o), (sh, sh, sh))(proj, proj, proj, gq, gk)


def even_pre_bwd(proj, gq, gk, dfq, dfk, dfx, cb0, name):
    S = proj.shape[0]
    tm = _tile(S, 256)
    nh = dfx // HEAD_DIM

    def body(q_ref, k_ref, gq_ref, gk_ref, dfq_ref, dfk_ref, dq_ref, dk_ref, dgq_ref, dgk_ref):
        @pl.when(pl.program_id(0) == 0)
        def _():
            dgq_ref[...] = jnp.zeros_like(dgq_ref)
            dgk_ref[...] = jnp.zeros_like(dgk_ref)

        for h in range(nh):
            sl = slice(h * HEAD_DIM, (h + 1) * HEAD_DIM)
            _, vq = jax.vjp(_rms, q_ref[:, sl], gq_ref[...])
            dq, dgq = vq(dfq_ref[:, sl])
            _, vk = jax.vjp(_rms, k_ref[:, sl], gk_ref[...])
            dk, dgk = vk(dfk_ref[:, sl])
            dq_ref[:, sl] = dq
            dk_ref[:, sl] = dk
            dgq_ref[...] += dgq
            dgk_ref[...] += dgk

    sh = SDS((S, dfx), f32)
    o = _rowspec(tm, dfx)
    g = SDS((1, HEAD_DIM), f32)
    return _call(body, name, (S // tm,),
                 [_rowspec(tm, dfx, cb0), _rowspec(tm, dfx, cb0 + 1), _bspec(HEAD_DIM), _bspec(HEAD_DIM), o, o],
                 (o, o, _bspec(HEAD_DIM), _bspec(HEAD_DIM)), (sh, sh, g, g))(proj, proj, gq, gk, dfq, dfk)


def _dn_out(o, gate, gn):
    return _rms(o, gn) * (gate * jax.nn.sigmoid(gate))


def _fox_out(o, gate):
    return o * jax.nn.sigmoid(gate)


def even_post(o_dn, o_fox, proj, gn, half, cb_dn_gate, cb_fox_gate, name):
    S = proj.shape[0]
    tm = _tile(S, 256)
    nh = half // HEAD_DIM

    def body(od_ref, of_ref, gd_ref, gf_ref, gn_ref, o_ref):
        for h in range(nh):
            sl = slice(h * HEAD_DIM, (h + 1) * HEAD_DIM)
            o_ref[:, sl] = _dn_out(od_ref[:, sl], gd_ref[:, sl], gn_ref[...]).astype(bf16)
        o_ref[:, half:] = _fox_out(of_ref[...], gf_ref[...]).astype(bf16)

    return _call(body, name, (S // tm,),
                 [_rowspec(tm, half), _rowspec(tm, half), _rowspec(tm, half, cb_dn_gate), _rowspec(tm, half, cb_fox_gate), _bspec(HEAD_DIM)],
                 _rowspec(tm, 2 * half), SDS((S, 2 * half), bf16))(o_dn, o_fox, proj, proj, gn)


def even_post_bwd(o_dn, o_fox, proj, gn, dom, half, cb_dn_gate, cb_fox_gate, name):
    S = proj.shape[0]
    tm = _tile(S, 256)
    nh = half // HEAD_DIM

    def body(od_ref, of_ref, gd_ref, gf_ref, gn_ref, dod_ref, dof_ref, dd_ref, df_ref, dgd_ref, dgf_ref, dgn_ref):
        @pl.when(pl.program_id(0) == 0)
        def _():
            dgn_ref[...] = jnp.zeros_like(dgn_ref)

        for h in range(nh):
            sl = slice(h * HEAD_DIM, (h + 1) * HEAD_DIM)
            _, vjp = jax.vjp(_dn_out, od_ref[:, sl], gd_ref[:, sl], gn_ref[...])
            d_o, d_gate, d_gn = vjp(dod_ref[:, sl])
            dd_ref[:, sl] = d_o
            dgd_ref[:, sl] = d_gate
            dgn_ref[...] += d_gn
        _, vjp = jax.vjp(_fox_out, of_ref[...], gf_ref[...])
        d_o, d_gate = vjp(dof_ref[...])
        df_ref[...] = d_o
        dgf_ref[...] = d_gate

    sh = SDS((S, half), f32)
    o = _rowspec(tm, half)
    return _call(body, name, (S // tm,),
                 [o, o, _rowspec(tm, half, cb_dn_gate), _rowspec(tm, half, cb_fox_gate), _bspec(HEAD_DIM),
                  _rowspec(tm, half, 0), _rowspec(tm, half, 1)],
                 (o, o, o, o, _bspec(HEAD_DIM)), (sh, sh, sh, sh, SDS((1, HEAD_DIM), f32)))(o_dn, o_fox, proj, proj, gn, dom, dom)


def _pad_row(v, lane0=0):
    return jnp.pad(v, (lane0, LANES - lane0 - v.shape[0]))[None]


def _carried(res, comm):
    return res if comm is not None else (res, [])


def even_mixer_fwd(x, gmix, w_in, w_out, conv_w, a_log, dt_bias, gn, gq, gk, f_bias, tag, comm_fox=None, comm_dn=None,
                   comm_in=None):
    S, D = x.shape
    half = D // 2
    ndn = half // HEAD_DIM
    small_cb = 4 * D // LANES
    alog_row, dt_row, fb_row = _pad_row(a_log), _pad_row(dt_bias), _pad_row(f_bias, 2 * ndn)
    h = rms_fwd(x, gmix, f"{tag}_rms")
    proj, got_in = _carried(mm(h, w_in, "nn", f32, f"{tag}_in", tm=512, tn=1408, comm=comm_in), comm_in)
    cqkv = conv_fwd(proj, conv_w, 3 * half, f"{tag}_conv")
    (o_dn, ssave), got_dn = _carried(dn_fwd(cqkv, proj, small_cb, alog_row, dt_row, ndn, f"{tag}_dn", comm=comm_dn), comm_dn)
    c, cT = gates_fwd(proj, small_cb, fb_row, f"{tag}_gates")
    fq, fk, fv = even_pre(proj, gq, gk, half, 4, f"{tag}_pre")
    (o_fox, lse), got_fox = _carried(fox_fwd(fq, fk, fv, c, cT, ndn, 2 * ndn, f"{tag}_fox", comm=comm_fox), comm_fox)
    om = even_post(o_dn, o_fox, proj, gn, half, 3, 7, f"{tag}_post")
    xo = mm(om, w_out, "nn", f32, f"{tag}_out", res=x)
    return xo, (h, proj, cqkv, o_dn, ssave, c, cT, fq, fk, fv, o_fox, lse, om, alog_row, dt_row, fb_row), got_fox, got_dn, got_in


def even_mixer_bwd(x, gmix, w_in, w_out, conv_w, gn, gq, gk, saved, dy, tag, comm_fox=None, comm_dn=None):
    S, D = x.shape
    half = D // 2
    ndn = half // HEAD_DIM
    small_cb = 4 * D // LANES
    h, proj, cqkv, o_dn, ssave, c, cT, fq, fk, fv, o_fox, lse, om, alog_row, dt_row, fb_row = saved
    dy32, dy16 = dy
    dom = mm(dy16, w_out, "nt", f32, f"{tag}_bdom")
    dw_out = mm(om, dy16, "tn", bf16, f"{tag}_bwout")
    d_odn, d_ofox, d_dgate, d_fgate, d_gn = even_post_bwd(o_dn, o_fox, proj, gn, dom, half, 3, 7, f"{tag}_bpost")
    (dfq, dfk, dfv, dcT), got_fox = _carried(
        fox_bwd(fq, fk, fv, c, cT, o_fox, lse, d_ofox, ndn, 2 * ndn, f"{tag}_bfox", comm=comm_fox), comm_fox)
    dq_pre, dk_pre, d_gq, d_gk = even_pre_bwd(proj, gq, gk, dfq, dfk, half, 4, f"{tag}_bpre")
    (dcqkv, dsm_dn, d_alog, d_dt), got_dn = _carried(
        dn_bwd(cqkv, proj, small_cb, alog_row, dt_row, ssave, d_odn, ndn, f"{tag}_bdn", comm=comm_dn), comm_dn)
    d_conv_in, d_conv_w = conv_bwd(proj, conv_w, dcqkv, f"{tag}_bconv")
    d_small, d_fb = gates_bwd(proj, small_cb, fb_row, dcT, dsm_dn, 2 * ndn, ndn, f"{tag}_bgates")
    dproj = jnp.concatenate([d_conv_in, d_dgate, dq_pre, dk_pre, dfv, d_fgate, d_small], axis=1).astype(bf16)
    dw_in = mm(h, dproj, "tn", bf16, f"{tag}_bwin", tn=1408)
    dx, d_gmix = mm_bdh_rms(dproj, w_in, x, gmix, dy32, f"{tag}_bdh")
    small = dict(norm_mix=d_gmix[0], dn_conv_w=d_conv_w, dn_a_log=d_alog[0, :ndn], dn_dt_bias=d_dt[0, :ndn],
                 dn_norm_g=d_gn[0], fox_q_norm_g=d_gq[0], fox_k_norm_g=d_gk[0], fox_f_bias=d_fb[0, 2 * ndn:3 * ndn])
    return dx, dw_in, dw_out, small, got_fox, got_dn


def odd_mixer_fwd(x, gmix, w_in, w_out, tag, comm=None):
    S, D = x.shape
    nh = D // HEAD_DIM
    h = rms_fwd(x, gmix, f"{tag}_rms")
    qkv = mm(h, w_in, "nn", bf16, f"{tag}_in", tn=768)
    (o, tails), got = _carried(sb_fwd(qkv, nh, f"{tag}_sb", comm=comm), comm)
    xo = mm(o, w_out, "nn", f32, f"{tag}_out", res=x)
    return xo, (h, qkv, o, tails), got


def odd_mixer_bwd(x, gmix, w_in, w_out, saved, dy, tag, comm=None):
    S, D = x.shape
    nh = D // HEAD_DIM
    h, qkv, o, tails = saved
    dy32, dy16 = dy
    do = mm(dy16, w_out, "nt", f32, f"{tag}_bdo")
    dw_out = mm(o, dy16, "tn", bf16, f"{tag}_bwout")
    (dq, dk, dv), got = _carried(sb_bwd(qkv, tails, do, nh, f"{tag}_bsb", comm=comm), comm)
    dqkv = jnp.concatenate([dq, dk, dv], axis=1).astype(bf16)
    dw_in = mm(h, dqkv, "tn", bf16, f"{tag}_bwin", tn=1536)
    dx, d_gmix = mm_bdh_rms(dqkv, w_in, x, gmix, dy32, f"{tag}_bdh")
    return dx, dw_in, dw_out, dict(norm_mix=d_gmix[0]), got


def all_gather(shards, axes, name, kind="ag"):
    return _call(None, name, (), [], [], [], comm=Comm(kind, shards, axes))()[1]


def scatter_parts(fulls, axes, name):
    return _call(None, name, (), [], [], [], comm=Comm("rs", fulls, axes))()[1]


def sum_parts(parts, name):
    _, R, C = parts.shape
    tm = _tile(R, 256)

    def body(p_ref, o_ref):
        acc = p_ref[0].astype(f32)
        for k in range(1, NDEV):
            acc = acc + p_ref[k].astype(f32)
        o_ref[...] = acc

    return _call(body, name, (R // tm,), [pl.BlockSpec((NDEV, tm, C), lambda i: (0, i, 0))], _rowspec(tm, C), SDS((R, C), f32))(parts)


def adamw(w, g, m, v, name):
    L, R, C = w.shape
    tm = _tile(R, max(8, min(512, (ADAMW_TILE_ELEMS // C) // 8 * 8)))
    c1 = 1.0 - ADAM_B1 ** ADAM_STEP
    c2 = 1.0 - ADAM_B2 ** ADAM_STEP

    def body(w_ref, g_ref, m_ref, v_ref, d_ref, nm_ref, nv_ref):
        g_ = g_ref[...]
        nm = ADAM_B1 * m_ref[...] + (1.0 - ADAM_B1) * g_
        nv = ADAM_B2 * v_ref[...] + (1.0 - ADAM_B2) * (g_ * g_)
        d_ref[...] = -ADAM_LR * ((nm / c1) / (jnp.sqrt(nv / c2) + ADAM_EPS) + ADAM_WD * w_ref[...])
        nm_ref[...] = nm
        nv_ref[...] = nv

    spec = pl.BlockSpec((None, tm, C), lambda l, i: (l, i, 0))
    sh = SDS((L, R, C), f32)
    return _call(body, name, (L, R // tm), [spec] * 4, (spec, spec, spec), (sh, sh, sh))(w, g, m, v)


def _pad_to(n, mult):
    return -(-n // mult) * mult


def _even_perm(D):
    half = D // 2
    ndn = half // HEAD_DIM
    o_gate = 3 * half
    o_b = o_gate + half
    o_a = o_b + ndn
    o_fq = o_a + ndn
    o_fpre = o_fq + 4 * half
    return half, ndn, o_b, o_a, o_fq, o_fpre


def _even_to_mine(w):
    D = (w.shape[-1] // 4) // LANES * LANES
    half, ndn, o_b, o_a, o_fq, o_fpre = _even_perm(D)
    small = jnp.concatenate([w[..., o_b:o_b + ndn], w[..., o_a:o_a + ndn], w[..., o_fpre:o_fpre + ndn]], axis=-1)
    small = jnp.pad(small, [(0, 0)] * (w.ndim - 1) + [(0, LANES - 3 * ndn)])
    return jnp.concatenate([w[..., :o_b], w[..., o_fq:o_fpre], small], axis=-1)


def _even_from_mine(w, D):
    half, ndn, o_b, o_a, o_fq, o_fpre = _even_perm(D)
    sm = w[..., 4 * D:]
    return jnp.concatenate([w[..., :o_b], sm[..., :ndn], sm[..., ndn:2 * ndn], w[..., o_b:4 * D], sm[..., 2 * ndn:3 * ndn]], axis=-1)


def _pack_small(parts):
    flat = jnp.concatenate([p.reshape(-1).astype(f32) for p in parts])
    n = flat.shape[0]
    return jnp.pad(flat, (0, _pad_to(n, 8 * LANES) - n)).reshape(-1, LANES)


def _unpack_small(packed, like):
    flat = packed.reshape(-1)
    out, off = [], 0
    for p in like:
        out.append(flat[off:off + p.size].reshape(p.shape))
        off += p.size
    return out


SMALL_NAMES = ("norm_ffn1", "norm_mix", "dn_a_log", "dn_dt_bias", "dn_norm_g", "fox_q_norm_g", "fox_k_norm_g", "fox_f_bias", "norm_ffn2")
BIG_NAMES = ("ffn1_w_gu", "ffn1_w_down", "w_in_even", "dn_conv_w", "w_out_even", "w_in_odd", "w_out_odd", "ffn2_w_gu", "ffn2_w_down")
WEIGHT_NAMES = ("norm_ffn1", "ffn1_w_gu", "ffn1_w_down", "norm_mix", "w_in_even", "dn_conv_w", "dn_a_log", "dn_dt_bias", "dn_norm_g",
                "fox_q_norm_g", "fox_k_norm_g", "fox_f_bias", "w_out_even", "w_in_odd", "w_out_odd", "norm_ffn2", "ffn2_w_gu", "ffn2_w_down")


def kernel(x, norm_ffn1, ffn1_w_gu, ffn1_w_down, norm_mix, w_in_even, dn_conv_w, dn_a_log, dn_dt_bias, dn_norm_g, fox_q_norm_g, fox_k_norm_g, fox_f_bias, w_out_even, w_in_odd, w_out_odd, norm_ffn2, ffn2_w_gu, ffn2_w_down, loss_target, m_norm_ffn1, m_ffn1_w_gu, m_ffn1_w_down, m_norm_mix, m_w_in_even, m_dn_conv_w, m_dn_a_log, m_dn_dt_bias, m_dn_norm_g, m_fox_q_norm_g, m_fox_k_norm_g, m_fox_f_bias, m_w_out_even, m_w_in_odd, m_w_out_odd, m_norm_ffn2, m_ffn2_w_gu, m_ffn2_w_down, v_norm_ffn1, v_ffn1_w_gu, v_ffn1_w_down, v_norm_mix, v_w_in_even, v_dn_conv_w, v_dn_a_log, v_dn_dt_bias, v_dn_norm_g, v_fox_q_norm_g, v_fox_k_norm_g, v_fox_f_bias, v_w_out_even, v_w_in_odd, v_w_out_odd, v_norm_ffn2, v_ffn2_w_gu, v_ffn2_w_down):
    args = dict(locals())
    W = {n: args[n] for n in WEIGHT_NAMES}
    M = {n: args["m_" + n] for n in WEIGHT_NAMES}
    V = {n: args["v_" + n] for n in WEIGHT_NAMES}
    xs = x[0]
    tgt = loss_target[0]
    S, D = xs.shape
    L = norm_ffn1.shape[0]
    n_even = w_in_even.shape[0]
    hs = ffn1_w_down.shape[1]
    hp = _pad_to(hs, LANES)
    me = 4 * lax.axis_index("x") + 2 * lax.axis_index("y") + lax.axis_index("c")

    def prep_gu(w):
        w = w.reshape(L, D, 2, hs)
        return jnp.pad(w, ((0, 0), (0, 0), (0, 0), (0, hp - hs))).reshape(L, D, 2 * hp).astype(bf16)

    def prep_down(w):
        return jnp.pad(w, ((0, 0), (0, hp - hs), (0, 0))).astype(bf16)

    sh_gu1, sh_d1, sh_gu2, sh_d2 = prep_gu(ffn1_w_gu), prep_down(ffn1_w_down), prep_gu(ffn2_w_gu), prep_down(ffn2_w_down)
    sh_ie, sh_oe = _even_to_mine(w_in_even).astype(bf16), w_out_even.astype(bf16)
    sh_io, sh_oo = w_in_odd.astype(bf16), w_out_odd.astype(bf16)

    def layer_shards(l):
        j = l // 2
        mix = [sh_ie[j], sh_oe[j]] if l % 2 == 0 else [sh_io[j], sh_oo[j]]
        return [sh_gu1[l], sh_d1[l], *mix, sh_gu2[l], sh_d2[l]]

    def layer_axes(l):
        return [1, 0, 0 if l % 2 == 0 else 1, 0, 1, 0]

    def layer_names(l):
        return ["ffn1_w_gu", "ffn1_w_down", *(["w_in_even", "w_out_even"] if l % 2 == 0 else ["w_in_odd", "w_out_odd"]),
                "ffn2_w_gu", "ffn2_w_down"]

    FOX_CARRIES, DN_CARRIES = (0, 4, 3), (1, 2, 5)

    def split_comm(kind, arrays, axes):
        return (Comm(kind, [arrays[i] for i in FOX_CARRIES], [axes[i] for i in FOX_CARRIES]),
                Comm(kind, [arrays[i] for i in DN_CARRIES], [axes[i] for i in DN_CARRIES]))

    def join_split(got_fox, got_dn):
        out = [None] * 6
        for i, a in zip(FOX_CARRIES, got_fox):
            out[i] = a
        for i, a in zip(DN_CARRIES, got_dn):
            out[i] = a
        return out

    sh0, ax0 = layer_shards(0), layer_axes(0)
    first = all_gather(sh0[:2] + [dn_conv_w[None]], ax0[:2] + [0], "ag_layer0", kind="ag2")
    weights = {0: first[:2] + [None] * 4}
    convw = jnp.moveaxis(first[2], 0, 2).reshape(n_even, CONV_WIDTH, -1)
    LAYER0_CARRIES = dict(f1_gu=(2,), f1_down=(3, 5), mx_in=(4,))

    EVEN_FWD_CARRIES = dict(f1_gu=(), mx_in=(), dn=(1, 2, 5), fox=(0, 4, 3), f2_gu=(), f2_down=())
    saved = []
    cur = xs
    for l in range(L):
        j = l // 2
        wgu1, wd1, win, wout, wgu2, wd2 = weights[l]
        nxt = l + 1 < L
        x0 = cur
        if l % 2 == 0:
            sh_n, ax_n = (layer_shards(l + 1), layer_axes(l + 1)) if nxt else (None, None)
            cm = {k: (Comm("ag2", [sh_n[i] for i in idx], [ax_n[i] for i in idx]) if nxt and idx else None)
                  for k, idx in EVEN_FWD_CARRIES.items()}
            if l == 0:
                cm.update({k: Comm("ag2", [sh0[i] for i in idx], [ax0[i] for i in idx]) for k, idx in LAYER0_CARRIES.items()})
            x1, s1, got_f1gu, got_f1down = ffn_fwd(x0, norm_ffn1[l:l + 1], wgu1, wd1, f"l{l}f1", comm_gu=cm["f1_gu"],
                                                   comm_down=cm["f1_down"] if l == 0 else None)
            if l == 0:
                win, wout, wd2 = got_f1gu[0], got_f1down[0], got_f1down[1]
            x2, sm, got_f, got_d, got_in = even_mixer_fwd(
                x1, norm_mix[l:l + 1], win, wout, convw[j], dn_a_log[j], dn_dt_bias[j], dn_norm_g[j:j + 1],
                fox_q_norm_g[j:j + 1], fox_k_norm_g[j:j + 1], fox_f_bias[j], f"l{l}mx", comm_fox=cm["fox"], comm_dn=cm["dn"],
                comm_in=cm["mx_in"])
            if l == 0:
                wgu2 = got_in[0]
                weights[0] = [wgu1, wd1, win, wout, wgu2, wd2]
                got_f1gu, got_in = [], []
            x3, s2, got_f2gu, got_f2down = ffn_fwd(x2, norm_ffn2[l:l + 1], wgu2, wd2, f"l{l}f2", comm_gu=cm["f2_gu"],
                                                   comm_down=cm["f2_down"])
            if nxt:
                got = dict(f1_gu=got_f1gu, mx_in=got_in, dn=got_d, fox=got_f, f2_gu=got_f2gu, f2_down=got_f2down)
                weights[l + 1] = [None] * 6
                for k, idx in EVEN_FWD_CARRIES.items():
                    for i, a in zip(idx, got[k]):
                        weights[l + 1][i] = a
        else:
            x1, s1, _, _ = ffn_fwd(x0, norm_ffn1[l:l + 1], wgu1, wd1, f"l{l}f1")
            cm = Comm("ag2", layer_shards(l + 1), layer_axes(l + 1)) if nxt else None
            x2, sm, got = odd_mixer_fwd(x1, norm_mix[l:l + 1], win, wout, f"l{l}mx", comm=cm)
            if nxt:
                weights[l + 1] = got
            x3, s2, _, _ = ffn_fwd(x2, norm_ffn2[l:l + 1], wgu2, wd2, f"l{l}f2")
        saved.append((x0, x1, x2, s1, sm, s2))
        cur = x3
    dy, loss_row = loss_head(cur, tgt, "loss")

    gsmall = {n: [None] * W[n].shape[0] for n in SMALL_NAMES}
    gconv = [None] * n_even
    parts = {n: [None] * W[n].shape[0] for n in BIG_NAMES if n != "dn_conv_w"}

    def take(l, recv):
        for nm, p in zip(layer_names(l), recv):
            idx = l if nm.startswith("ffn") else l // 2
            parts[nm][idx] = sum_parts(p.reshape(NDEV, -1, p.shape[-1]), f"sum_{nm}_{idx}").reshape(p.shape[1:])

    pending = None
    for l in reversed(range(L)):
        j = l // 2
        wgu1, wd1, win, wout, wgu2, wd2 = weights[l]
        x0, x1, x2, s1, sm, s2 = saved[l]
        dy, dg, dwgu2, dwd2 = ffn_bwd(x2, norm_ffn2[l:l + 1], wgu2, wd2, s2, dy, f"l{l}f2")
        gsmall["norm_ffn2"][l] = dg[0]
        if l % 2 == 0:
            cf, cd = split_comm("rs", pending, layer_axes(l + 1)) if pending is not None else (None, None)
            dy, dwin, dwout, sg, got_f, got_d = even_mixer_bwd(
                x1, norm_mix[l:l + 1], win, wout, convw[j], dn_norm_g[j:j + 1], fox_q_norm_g[j:j + 1],
                fox_k_norm_g[j:j + 1], sm, dy, f"l{l}mx", comm_fox=cf, comm_dn=cd)
            if pending is not None:
                take(l + 1, join_split(got_f, got_d))
            gconv[j] = sg.pop("dn_conv_w")
            for k_, v_ in sg.items():
                gsmall[k_][l if k_ == "norm_mix" else j] = v_
        else:
            cm = Comm("rs", pending, layer_axes(l + 1)) if pending is not None else None
            dy, dwin, dwout, sg, got = odd_mixer_bwd(x1, norm_mix[l:l + 1], win, wout, sm, dy, f"l{l}mx", comm=cm)
            if pending is not None:
                take(l + 1, got)
            gsmall["norm_mix"][l] = sg["norm_mix"]
        if l > 0:
            dy, dg, dwgu1, dwd1 = ffn_bwd(x0, norm_ffn1[l:l + 1], wgu1, wd1, s1, dy, f"l{l}f1")
        else:
            rs = lambda a, ax: Comm("rs", [a], [ax])
            dy, dg, dwgu1, dwd1, got0 = ffn_bwd(x0, norm_ffn1[l:l + 1], wgu1, wd1, s1, dy, f"l{l}f1",
                                                comms=dict(bda=rs(dwd2, ax0[5]), bwgu=rs(dwin, ax0[2]), bwd=rs(dwout, ax0[3]),
                                                           bdh=rs(dwgu2, ax0[4])))
        gsmall["norm_ffn1"][l] = dg[0]
        pending = [dwgu1, dwd1, dwin, dwout, dwgu2, dwd2]
    last = scatter_parts(pending[:2], ax0[:2], "rs_layer0")
    take(0, last + [got0["bwgu"][0], got0["bwd"][0], got0["bdh"][0], got0["bda"][0]])
    grad_x = dy[0][None]

    small_list = [jnp.stack(gsmall[n]) for n in SMALL_NAMES] + [jnp.stack(gconv), loss_row[0, :1]]
    packed = _pack_small(small_list)
    gathered = all_gather([packed], [0], "ag_small")[0]
    summed = sum_parts(gathered.reshape(NDEV, packed.shape[0], LANES), "sum_small")
    small_sum = _unpack_small(summed, small_list)
    G = dict(zip(SMALL_NAMES, small_sum[:len(SMALL_NAMES)]))
    conv_full = small_sum[len(SMALL_NAMES)]
    cwid = dn_conv_w.shape[2]
    G["dn_conv_w"] = lax.dynamic_slice_in_dim(conv_full, me * cwid, cwid, axis=2)
    loss = small_sum[-1][0]

    def unprep_gu(g):
        return g.reshape(L, D, 2, hp)[..., :hs].reshape(L, D, 2 * hs)

    G["ffn1_w_gu"] = unprep_gu(jnp.stack(parts["ffn1_w_gu"]))
    G["ffn2_w_gu"] = unprep_gu(jnp.stack(parts["ffn2_w_gu"]))
    G["ffn1_w_down"] = jnp.stack(parts["ffn1_w_down"])[:, :hs]
    G["ffn2_w_down"] = jnp.stack(parts["ffn2_w_down"])[:, :hs]
    G["w_in_even"] = _even_from_mine(jnp.stack(parts["w_in_even"]), D)
    G["w_out_even"] = jnp.stack(parts["w_out_even"])
    G["w_in_odd"] = jnp.stack(parts["w_in_odd"])
    G["w_out_odd"] = jnp.stack(parts["w_out_odd"])

    delta, new_m, new_v = {}, {}, {}
    for n in BIG_NAMES:
        delta[n], new_m[n], new_v[n] = adamw(W[n], G[n], M[n], V[n], f"adamw_{n}")
    sw = [W[n] for n in SMALL_NAMES]
    d_, m_, v_ = adamw(_pack_small(sw)[None], _pack_small([G[n] for n in SMALL_NAMES])[None],
                       _pack_small([M[n] for n in SMALL_NAMES])[None], _pack_small([V[n] for n in SMALL_NAMES])[None], "adamw_small")
    for n, a, b, c_ in zip(SMALL_NAMES, _unpack_small(d_, sw), _unpack_small(m_, sw), _unpack_small(v_, sw)):
        delta[n], new_m[n], new_v[n] = a, b, c_

    return (loss, grad_x, *[G[n] for n in WEIGHT_NAMES], *[delta[n] for n in WEIGHT_NAMES],
            *[new_m[n] for n in WEIGHT_NAMES], *[new_v[n] for n in WEIGHT_NAMES])
```

```python
import functools
import math

import jax
import jax.numpy as jnp
from jax import lax
from jax.experimental import pallas as pl
from jax.experimental.pallas import tpu as pltpu

f32 = jnp.float32
bf16 = jnp.bfloat16
SDS = jax.ShapeDtypeStruct

HEAD_DIM = 128
LANES = 128
CONV_WIDTH = 4
DN_CHUNK = 128
EPS = 1e-6
NDEV = 8
VMEM_LIMIT_BYTES = 56 * 1024 * 1024
HI = lax.Precision.HIGHEST
ADAMW_TILE_ELEMS = 384 * 1024

ADAM_LR = 0.001
ADAM_B1 = 0.9
ADAM_B2 = 0.999
ADAM_EPS = 1e-08
ADAM_WD = 0.01
ADAM_STEP = 10

NN = (((1,), (0,)), ((), ()))
NT = (((1,), (1,)), ((), ()))
TN = (((0,), (0,)), ((), ()))


def _me_and_peers():
    x, y, c = lax.axis_index("x"), lax.axis_index("y"), lax.axis_index("c")
    me = 4 * x + 2 * y + c
    peers = []
    for r in range(1, NDEV):
        px = 1 - x if (r >> 2) & 1 else x
        py = 1 - y if (r >> 1) & 1 else y
        pc = 1 - c if r & 1 else c
        peers.append(((px, py, pc), 4 * px + 2 * py + pc))
    return me, peers


def _slab(ref, axis, width, k):
    idx = [slice(None)] * len(ref.shape)
    idx[axis] = pl.ds(k * width, width)
    return ref.at[tuple(idx)]


class Comm:
    def __init__(self, kind, arrays, axes):
        self.kind, self.arrays, self.axes, self.n = kind, list(arrays), list(axes), len(arrays)

    def out_shape(self):
        out = []
        for a, ax in zip(self.arrays, self.axes):
            shp = list(a.shape)
            if self.kind != "rs":
                shp[ax] *= NDEV
                out.append(SDS(tuple(shp), a.dtype))
            else:
                shp[ax] //= NDEV
                out.append(SDS((NDEV, *shp), a.dtype))
        return out

    def sems(self):
        return [pltpu.SemaphoreType.DMA((self.n, NDEV - 1)), pltpu.SemaphoreType.DMA((self.n, NDEV - 1)),
                pltpu.SemaphoreType.DMA((self.n,))]

    def _src(self, ins, t, to_idx):
        if self.kind == "ag":
            return ins[t]
        return _slab(ins[t], self.axes[t], ins[t].shape[self.axes[t]] // NDEV, to_idx)

    def _dst(self, ins, outs, t, from_idx):
        if self.kind != "rs":
            return _slab(outs[t], self.axes[t], ins[t].shape[self.axes[t]], from_idx)
        return outs[t].at[from_idx]

    def _copies(self, ins, outs, sems, sending):
        send_sems, recv_sems, loc_sems = sems
        me, peers = _me_and_peers()
        local, remote = [], []
        for t in range(self.n):
            local.append(pltpu.make_async_copy(self._src(ins, t, me), self._dst(ins, outs, t, me), loc_sems.at[t]))
            for r, (peer, pidx) in enumerate(peers):
                remote.append(pltpu.make_async_remote_copy(
                    src_ref=self._src(ins, t, pidx), dst_ref=self._dst(ins, outs, t, me if sending else pidx),
                    send_sem=send_sems.at[t, r], recv_sem=recv_sems.at[t, r], device_id=peer,
                    device_id_type=pl.DeviceIdType.MESH))
        return local, remote

    def _two_level(self, ins, outs, sems):
        send_sems, recv_sems, loc_sems = sems
        x, y, c = lax.axis_index("x"), lax.axis_index("y"), lax.axis_index("c")
        me, sibling = (x, y, c), (x, y, 1 - c)
        chips = [(1 - x, y), (x, 1 - y), (1 - x, 1 - y)]
        dev = lambda p: 4 * p[0] + 2 * p[1] + p[2]

        def copy(t, k, block, to, from_shard):
            dst = self._dst(ins, outs, t, dev(block))
            return pltpu.make_async_remote_copy(
                src_ref=ins[t] if from_shard else dst, dst_ref=dst, send_sem=send_sems.at[t, k], recv_sem=recv_sems.at[t, k],
                device_id=to, device_id_type=pl.DeviceIdType.MESH)

        ts = range(self.n)
        local = [pltpu.make_async_copy(ins[t], self._dst(ins, outs, t, dev(me)), loc_sems.at[t]) for t in ts]
        first = [copy(t, 0, me, sibling, True) for t in ts]
        first += [copy(t, 1 + j, me, (*chip, c), True) for t in ts for j, chip in enumerate(chips)]
        return local, first, copy, chips, me, sibling, c

    def start(self, ins, outs, sems):
        if self.kind == "ag2":
            local, first = self._two_level(ins, outs, sems)[:2]
            for cp in local + first:
                cp.start()
            return
        local, remote = self._copies(ins, outs, sems, True)
        for cp in local + remote:
            cp.start()

    def wait(self, ins, outs, sems):
        if self.kind == "ag2":
            local, first, copy, chips, me, sibling, c = self._two_level(ins, outs, sems)
            passed = []
            for t in range(self.n):
                for j, chip in enumerate(chips):
                    copy(t, 1 + j, (*chip, c), me, False).wait_recv()
                    fwd = copy(t, 4 + j, (*chip, c), sibling, False)
                    fwd.start()
                    passed.append(fwd)
            for t in range(self.n):
                copy(t, 0, sibling, me, False).wait_recv()
                for j, chip in enumerate(chips):
                    copy(t, 4 + j, (*chip, 1 - c), me, False).wait_recv()
            for cp in first + passed:
                cp.wait_send()
            for cp in local:
                cp.wait()
            return
        local, remote = self._copies(ins, outs, sems, False)
        for cp in remote:
            cp.wait_recv()
            cp.wait_send()
        for cp in local:
            cp.wait()


def _call(body, name, grid, in_specs, out_specs, out_shape, scratch=(), comm=None):
    params = pltpu.CompilerParams(vmem_limit_bytes=VMEM_LIMIT_BYTES)
    if comm is None:
        return pl.pallas_call(body, name=name, grid=grid, in_specs=in_specs, out_specs=out_specs, out_shape=out_shape,
                              scratch_shapes=list(scratch), compiler_params=params)
    single = not isinstance(out_shape, (tuple, list))
    c_out_specs = [out_specs] if single else list(out_specs)
    c_out_shape = [out_shape] if single else list(out_shape)
    n_in, n_out, n_scr, n = len(in_specs), len(c_out_shape), len(scratch), comm.n
    anyspec = pl.BlockSpec(memory_space=pl.ANY)

    def wrapped(*refs):
        ins, refs = refs[:n_in], refs[n_in:]
        cins, refs = refs[:n], refs[n:]
        outs, refs = refs[:n_out], refs[n_out:]
        couts, refs = refs[:n], refs[n:]
        scr, sems = refs[:n_scr], refs[n_scr:]
        first = functools.reduce(lambda a, b: a & b, [pl.program_id(d) == 0 for d in range(len(grid))], True)
        last = functools.reduce(lambda a, b: a & b, [pl.program_id(d) == grid[d] - 1 for d in range(len(grid))], True)
        if grid:
            pl.when(first)(lambda: comm.start(cins, couts, sems))
            body(*ins, *outs, *scr)
            pl.when(last)(lambda: comm.wait(cins, couts, sems))
        else:
            comm.start(cins, couts, sems)
            if body is not None:
                body(*ins, *outs, *scr)
            comm.wait(cins, couts, sems)

    call = pl.pallas_call(
        wrapped, name=name, grid=grid, in_specs=list(in_specs) + [anyspec] * n, out_specs=c_out_specs + [anyspec] * n,
        out_shape=c_out_shape + comm.out_shape(), scratch_shapes=list(scratch) + comm.sems(), compiler_params=params)

    def run(*args):
        res = call(*args, *comm.arrays)
        mine = res[:n_out]
        return (mine[0] if single else tuple(mine)), list(res[n_out:])

    return run


def _tile(n, want, align=8):
    if n <= want:
        return n
    for t in range(want // align * align, 0, -align):
        if n % t == 0:
            return t
    return n


def _dot(a, b, dims=NN, precision=None):
    return lax.dot_general(a, b, dims, preferred_element_type=f32, precision=precision)


def _logsig(x):
    return jnp.minimum(x, 0.0) - jnp.log(1.0 + jnp.exp(-jnp.abs(x)))


def _softplus(x):
    return jnp.maximum(x, 0.0) + jnp.log(1.0 + jnp.exp(-jnp.abs(x)))


def _rms(x, g):
    return x * lax.rsqrt(jnp.mean(x * x, axis=-1, keepdims=True) + EPS) * g


def _lane_pick(blk, lane_idx):
    lane = lax.broadcasted_iota(jnp.int32, (1, LANES), 1)
    return jnp.sum(jnp.where(lane == lane_idx, blk, 0.0), axis=1, keepdims=True)


def mm(a, b, mode, out_dtype, name, tm=512, tn=512, res=None, scale=1.0, comm=None):
    if mode == "nn":
        (M, K), (_, N) = a.shape, b.shape
    elif mode == "nt":
        (M, K), (N, _) = a.shape, b.shape
    else:
        (K, M), (_, N) = a.shape, b.shape
    tm, tn = _tile(M, tm, LANES), _tile(N, tn, LANES)
    a_spec = pl.BlockSpec((K, tm), lambda j, i: (0, i)) if mode == "tn" else pl.BlockSpec((tm, K), lambda j, i: (i, 0))
    b_spec = pl.BlockSpec((tn, K), lambda j, i: (j, 0)) if mode == "nt" else pl.BlockSpec((K, tn), lambda j, i: (0, j))
    dims = {"nn": NN, "nt": NT, "tn": TN}[mode]
    o_spec = pl.BlockSpec((tm, tn), lambda j, i: (i, j))

    def body(*refs):
        acc = _dot(refs[0][...].astype(bf16), refs[1][...].astype(bf16), dims)
        if scale != 1.0:
            acc = acc * scale
        if res is not None:
            acc = acc + refs[2][...]
        refs[-1][...] = acc.astype(out_dtype)

    ins, specs = [a, b], [a_spec, b_spec]
    if res is not None:
        ins.append(res)
        specs.append(o_spec)
    return _call(body, name, (N // tn, M // tm), specs, o_spec, SDS((M, N), out_dtype), comm=comm)(*ins)


def _rowspec(tm, w, cb=0):
    return pl.BlockSpec((tm, w), lambda i: (i, cb))


def _bspec(w):
    return pl.BlockSpec((1, w), lambda i: (0, 0))


def rms_fwd(x, g, name):
    S, D = x.shape
    tm = _tile(S, 512)

    def body(x_ref, g_ref, h_ref):
        h_ref[...] = _rms(x_ref[...], g_ref[...]).astype(bf16)

    return _call(body, name, (S // tm,), [_rowspec(tm, D), _bspec(D)], _rowspec(tm, D), SDS((S, D), bf16))(x, g)


def _swiglu(g, u):
    return g * jax.nn.sigmoid(g) * u


def ffn_gu_act(h, wgu, name, tm=1024, tn=768, comm=None):
    S, D = h.shape
    W = wgu.shape[1] // 2
    tm, tn = _tile(S, tm, LANES), _tile(W, tn, LANES)
    nb = W // tn

    def body(h_ref, wg_ref, wu_ref, gu_ref, a_ref):
        hb = h_ref[...]
        g = _dot(hb, wg_ref[...])
        u = _dot(hb, wu_ref[...])
        gu_ref[0] = g.astype(bf16)
        gu_ref[1] = u.astype(bf16)
        a_ref[...] = _swiglu(g, u).astype(bf16)

    return _call(body, name, (nb, S // tm),
                 [pl.BlockSpec((tm, D), lambda j, i: (i, 0)), pl.BlockSpec((D, tn), lambda j, i: (0, j)),
                  pl.BlockSpec((D, tn), lambda j, i: (0, nb + j))],
                 (pl.BlockSpec((2, tm, tn), lambda j, i: (0, i, j)), pl.BlockSpec((tm, tn), lambda j, i: (i, j))),
                 (SDS((2, S, W), bf16), SDS((S, W), bf16)), comm=comm)(h, wgu, wgu)


def ffn_bda_act(dy, wd, gu, scale, name, tm=512, tn=768, comm=None):
    S, D = dy.shape
    W = wd.shape[0]
    tm, tn = _tile(S, tm, LANES), _tile(W, tn, LANES)

    def body(dy_ref, wd_ref, gu_ref, dgu_ref):
        da = _dot(dy_ref[...].astype(bf16), wd_ref[...], NT) * scale
        _, vjp = jax.vjp(_swiglu, gu_ref[0].astype(f32), gu_ref[1].astype(f32))
        dg, du = vjp(da)
        dgu_ref[0] = dg.astype(bf16)
        dgu_ref[1] = du.astype(bf16)

    planes = pl.BlockSpec((2, tm, tn), lambda j, i: (0, i, j))
    return _call(body, name, (W // tn, S // tm),
                 [pl.BlockSpec((tm, D), lambda j, i: (i, 0)), pl.BlockSpec((tn, D), lambda j, i: (j, 0)), planes],
                 planes, SDS((2, S, W), bf16), comm=comm)(dy, wd, gu)


def ffn_bwgu(h, dgu, name, tm=512, tn=768, comm=None):
    S, D = h.shape
    W = dgu.shape[2]
    tm, tn = _tile(D, tm, LANES), _tile(W, tn, LANES)
    nb = W // tn

    def body(h_ref, d_ref, o_ref):
        o_ref[...] = _dot(h_ref[...], d_ref[...], TN).astype(bf16)

    return _call(body, name, (2 * nb, D // tm),
                 [pl.BlockSpec((S, tm), lambda j, i: (0, i)), pl.BlockSpec((None, S, tn), lambda j, i: (j // nb, 0, j % nb))],
                 pl.BlockSpec((tm, tn), lambda j, i: (i, j)), SDS((D, 2 * W), bf16), comm=comm)(h, dgu)


def nt_rms_bwd(pairs, x, g, dres, name, tm=256, comm=None):
    S, D = x.shape
    tm = _tile(S, tm)
    n = len(pairs)

    def body(*refs):
        x_ref, g_ref, dres_ref = refs[2 * n:2 * n + 3]
        dx_ref, dxb_ref, dg_ref = refs[2 * n + 3:]
        dh = sum(_dot(refs[2 * k][...].astype(bf16), refs[2 * k + 1][...], NT) for k in range(n))
        _, vjp = jax.vjp(_rms, x_ref[...], g_ref[...])
        dx, dg = vjp(dh)
        dx = dres_ref[...] + dx
        dx_ref[...] = dx
        dxb_ref[...] = dx.astype(bf16)

        @pl.when(pl.program_id(0) == 0)
        def _():
            dg_ref[...] = jnp.zeros_like(dg_ref)

        dg_ref[...] += dg

    arrays, specs = [], []
    for a, a_spec, b, b_spec in pairs:
        arrays += [a, b]
        specs += [a_spec, b_spec]
    (dx, dxb, dg), got = _carried(_call(body, name, (S // tm,), specs + [_rowspec(tm, D), _bspec(D), _rowspec(tm, D)],
                                        (_rowspec(tm, D), _rowspec(tm, D), _bspec(D)),
                                        (SDS((S, D), f32), SDS((S, D), bf16), SDS((1, D), f32)), comm=comm)(*arrays, x, g, dres),
                                  comm)
    return ((dx, dxb), dg) if comm is None else ((dx, dxb), dg, got)


def ffn_bdh_rms(dgu, wgu, x, g, dres, name, tm=256, comm=None):
    _, S, W = dgu.shape
    D = wgu.shape[0]
    tm = _tile(S, tm)
    pairs = [(dgu, pl.BlockSpec((None, tm, W), functools.partial(lambda p, i: (p, i, 0), p)),
              wgu, pl.BlockSpec((D, W), functools.partial(lambda p, i: (0, p), p))) for p in range(2)]
    return nt_rms_bwd(pairs, x, g, dres, name, tm, comm=comm)


def mm_bdh_rms(d, w, x, g, dres, name, tm=256, comm=None):
    S, K = d.shape
    tm = _tile(S, tm)
    return nt_rms_bwd([(d, pl.BlockSpec((tm, K), lambda i: (i, 0)), w, pl.BlockSpec(w.shape, lambda i: (0, 0)))], x, g, dres, name, tm,
                      comm=comm)


def loss_head(y, t, name):
    S, D = y.shape
    tm = _tile(S, 512)

    def body(y_ref, t_ref, dy_ref, dyb_ref, l_ref):
        e = y_ref[...] - t_ref[...]
        dy_ref[...] = e * (1.0 / D)
        dyb_ref[...] = (e * (1.0 / D)).astype(bf16)

        @pl.when(pl.program_id(0) == 0)
        def _():
            l_ref[...] = jnp.zeros_like(l_ref)

        l_ref[...] += jnp.sum(jnp.sum(e * e, axis=1, keepdims=True), axis=0, keepdims=True) * (0.5 / D)

    dy, dyb, loss = _call(body, name, (S // tm,), [_rowspec(tm, D), _rowspec(tm, D)],
                          (_rowspec(tm, D), _rowspec(tm, D), _bspec(LANES)),
                          (SDS((S, D), f32), SDS((S, D), bf16), SDS((1, LANES), f32)))(y, t)
    return (dy, dyb), loss


def ffn_fwd(x, g, wgu, wd, tag, comm_gu=None, comm_down=None):
    h = rms_fwd(x, g, f"{tag}_rms")
    (gu, a), got_gu = _carried(ffn_gu_act(h, wgu, f"{tag}_gu", comm=comm_gu), comm_gu)
    xo, got_down = _carried(mm(a, wd, "nn", f32, f"{tag}_down", tm=512, tn=512, res=x, scale=0.5, comm=comm_down), comm_down)
    return xo, (h, gu, a), got_gu, got_down


def ffn_bwd(x, g, wgu, wd, saved, dy, tag, comms=None, own_axes=None):
    h, gu, a = saved
    dy32, dy16 = dy
    cm = comms or {}
    dgu, got_bda = _carried(ffn_bda_act(dy16, wd, gu, 0.5, f"{tag}_bda", comm=cm.get("bda")), cm.get("bda"))
    dwd, got_bwd = _carried(mm(a, dy16, "tn", bf16, f"{tag}_bwd", tm=512, tn=512, scale=0.5, comm=cm.get("bwd")), cm.get("bwd"))
    c_wd = Comm("rs", [dwd], [own_axes[1]]) if own_axes else None
    dwgu, got_wd = _carried(ffn_bwgu(h, dgu, f"{tag}_bwgu", comm=c_wd), c_wd)
    c_wgu = Comm("rs", [dwgu], [own_axes[0]]) if own_axes else None
    res = ffn_bdh_rms(dgu, wgu, x, g, dy32, f"{tag}_bdh", comm=c_wgu)
    dx, dg = res[:2]
    if comms is None and own_axes is None:
        return dx, dg, dwgu, dwd
    return dx, dg, dwgu, dwd, dict(bda=got_bda, bwd=got_bwd, wd=got_wd, wgu=res[2] if c_wgu is not None else [])


def _split_bf16(x):
    hi = x.astype(bf16)
    lo = (x - hi.astype(f32)).astype(bf16)
    return hi, lo


SB_TQ, SB_TK, SB_NSUB = 512, 256, 4
FOX_TK = 256


def _sb_geometry(S, tk=SB_TK):
    tq = min(SB_TQ, S)
    tk = min(tk, tq)
    return tq, tk, tq // SB_NSUB, tq // tk


def _sb_consts(tk):
    r = lax.broadcasted_iota(jnp.int32, (tk, tk), 0)
    c = lax.broadcasted_iota(jnp.int32, (tk, tk), 1)
    return (r > c).astype(bf16), (r < c).astype(bf16)


def _sb_scores(qs, k, kb, tk, rows, masked):
    scale = HEAD_DIM ** -0.5
    z = [_dot(q, k, NT) * scale for q in qs]
    ls = [_logsig(z_) for z_ in z]
    lm = [l - z_ for l, z_ in zip(ls, z)]
    ok = None
    if masked:
        col = kb * tk + lax.broadcasted_iota(jnp.int32, z[0].shape, 1)
        ok = [col < row for row in rows]
        lm = [jnp.where(m, x, 0.0) for m, x in zip(ok, lm)]
    return ls, lm, ok


def _sb_weights(ls, lm, ok, tail, after):
    suf = [_dot(x.astype(bf16), after) for x in lm]
    a = [jnp.exp(l + s_ + t_) for l, s_, t_ in zip(ls, suf, tail)]
    if ok is not None:
        a = [jnp.where(m, x, 0.0) for m, x in zip(ok, a)]
    return a


def sb_fwd(qkv, nh, name, comm=None):
    S = qkv.shape[0]
    tq, tk, rs, nd = _sb_geometry(S)
    nsub = tq // rs

    def body(q_ref, k_ref, v_ref, o_ref, tails_ref):
        i = pl.program_id(1)
        after, _ = _sb_consts(tk)
        qs = [q_ref[pl.ds(s * rs, rs), :] for s in range(nsub)]
        rows = [i * tq + s * rs + lax.broadcasted_iota(jnp.int32, (rs, tk), 0) for s in range(nsub)]

        def tile(kb, carry, masked):
            tail, acc = carry
            off = pl.multiple_of(kb * tk, tk)
            k = k_ref[pl.ds(off, tk), :]
            v = v_ref[pl.ds(off, tk), :]
            ls, lm, ok = _sb_scores(qs, k, kb, tk, rows, masked)
            a = _sb_weights(ls, lm, ok, tail, after)
            tails_ref[pl.ds(kb, 1), :] += jnp.concatenate([t_.T for t_ in tail], axis=1)
            acc = [ac + _dot(a_.astype(bf16), v) for ac, a_ in zip(acc, a)]
            tail = [t_ + jnp.sum(x, axis=1, keepdims=True) for t_, x in zip(tail, lm)]
            return tail, acc

        tails_ref[...] = jnp.zeros_like(tails_ref)
        carry = ([jnp.zeros((rs, 1), f32)] * nsub, [jnp.zeros((rs, HEAD_DIM), f32)] * nsub)
        for d in reversed(range(nd)):
            carry = tile(i * nd + d, carry, True)
        carry = lax.fori_loop(0, i * nd, lambda t, c: tile(i * nd - 1 - t, c, False), carry)
        for s in range(nsub):
            o_ref[pl.ds(s * rs, rs), :] = carry[1][s].astype(o_ref.dtype)

    blk = lambda cb0: pl.BlockSpec((tq, HEAD_DIM), lambda h, i: (i, cb0 + h))
    full = lambda cb0: pl.BlockSpec((S, HEAD_DIM), lambda h, i: (0, cb0 + h))
    tails = pl.BlockSpec((S // tk, tq), lambda h, i: (h, i))
    return _call(body, name, (nh, S // tq), [blk(0), full(nh), full(2 * nh)], (blk(0), tails),
                 (SDS((S, nh * HEAD_DIM), bf16), SDS((nh * (S // tk), S), f32)), comm=comm)(qkv, qkv, qkv)


def sb_bwd(qkv, tails, do, nh, name, comm=None):
    S = qkv.shape[0]
    tq, tk, rs, nd = _sb_geometry(S)
    nsub = tq // rs
    scale = HEAD_DIM ** -0.5

    def body(q_ref, k_ref, v_ref, tails_ref, do_ref, dq_ref, dk_ref, dv_ref):
        i = pl.program_id(1)

        @pl.when(i == 0)
        def _():
            dk_ref[...] = jnp.zeros_like(dk_ref)
            dv_ref[...] = jnp.zeros_like(dv_ref)

        after, before = _sb_consts(tk)
        qs = [q_ref[pl.ds(s * rs, rs), :] for s in range(nsub)]
        dos = [do_ref[pl.ds(s * rs, rs), :].astype(bf16) for s in range(nsub)]
        rows = [i * tq + s * rs + lax.broadcasted_iota(jnp.int32, (rs, tk), 0) for s in range(nsub)]

        def sweep2(kb, carry, masked):
            pe, dq = carry
            off = pl.multiple_of(kb * tk, tk)
            k = k_ref[pl.ds(off, tk), :]
            v = v_ref[pl.ds(off, tk), :]
            ls, lm, ok = _sb_scores(qs, k, kb, tk, rows, masked)
            tail_row = tails_ref[pl.ds(kb, 1), :]
            tail = [tail_row[:, s * rs:(s + 1) * rs].T for s in range(nsub)]
            a = _sb_weights(ls, lm, ok, tail, after)
            e = [_dot(d_, v, NT) * a_ for d_, a_ in zip(dos, a)]
            cum_e = [p_ + _dot(e_.astype(bf16), before) for p_, e_ in zip(pe, e)]
            sig = [jnp.exp(l) for l in ls]
            dz = [e_ * (1.0 - sg) - c_ * sg for e_, sg, c_ in zip(e, sig, cum_e)]
            if ok is not None:
                dz = [jnp.where(m, x, 0.0) for m, x in zip(ok, dz)]
            dzb = [(x * scale).astype(bf16) for x in dz]
            dk_ref[pl.ds(off, tk), :] += sum(_dot(x, q, TN) for x, q in zip(dzb, qs))
            dv_ref[pl.ds(off, tk), :] += sum(_dot(a_.astype(bf16), d_, TN) for a_, d_ in zip(a, dos))
            pe = [p_ + jnp.sum(e_, axis=1, keepdims=True) for p_, e_ in zip(pe, e)]
            dq = [g + _dot(x, k) for g, x in zip(dq, dzb)]
            return pe, dq

        carry = ([jnp.zeros((rs, 1), f32)] * nsub, [jnp.zeros((rs, HEAD_DIM), f32)] * nsub)
        carry = lax.fori_loop(0, i * nd, lambda kb, c: sweep2(kb, c, False), carry)
        for d in range(nd):
            carry = sweep2(i * nd + d, carry, True)
        for s in range(nsub):
            dq_ref[pl.ds(s * rs, rs), :] = carry[1][s]

    blk = lambda cb0: pl.BlockSpec((tq, HEAD_DIM), lambda h, i: (i, cb0 + h))
    full = lambda cb0: pl.BlockSpec((S, HEAD_DIM), lambda h, i: (0, cb0 + h))
    sh = SDS((S, nh * HEAD_DIM), f32)
    tails_spec = pl.BlockSpec((S // tk, tq), lambda h, i: (h, i))
    return _call(body, name, (nh, S // tq), [blk(0), full(nh), full(2 * nh), tails_spec, blk(0)], (blk(0), full(0), full(0)),
                 (sh, sh, sh), comm=comm)(qkv, qkv, qkv, tails, do)


def fox_fwd(fq, fk, fv, c, cT, nh, lane0, name, comm=None):
    S = fq.shape[0]
    tq, tk, rs, nd = _sb_geometry(S, FOX_TK)
    nsub = tq // rs
    scale = HEAD_DIM ** -0.5

    def body(q_ref, k_ref, v_ref, c_ref, ct_ref, o_ref, lse_ref):
        h = pl.program_id(0)
        i = pl.program_id(1)
        subs = range(nsub)
        qs = [q_ref[pl.ds(s * rs, rs), :] for s in subs]
        ci = [_lane_pick(c_ref[pl.ds(s * rs, rs), :], lane0 + h) for s in subs]
        rows = [i * tq + s * rs + lax.broadcasted_iota(jnp.int32, (rs, tk), 0) for s in subs]

        def tile(kb, carry, masked):
            m, l, acc = carry
            off = pl.multiple_of(kb * tk, tk)
            k = k_ref[pl.ds(off, tk), :]
            v = v_ref[pl.ds(off, tk), :]
            cj = ct_ref[pl.ds(lane0 % 8 + h, 1), pl.ds(off, tk)]
            sc = [_dot(q, k, NT) * scale + (c_ - cj) for q, c_ in zip(qs, ci)]
            if masked:
                col = kb * tk + lax.broadcasted_iota(jnp.int32, (rs, tk), 1)
                sc = [jnp.where(col <= row, x, -jnp.inf) for x, row in zip(sc, rows)]
            m2 = [jnp.maximum(m_, jnp.max(x, axis=1, keepdims=True)) for m_, x in zip(m, sc)]
            p = [jnp.exp(x - m_) for x, m_ in zip(sc, m2)]
            al = [jnp.exp(a - b) for a, b in zip(m, m2)]
            l = [a * l_ + jnp.sum(p_, axis=1, keepdims=True) for a, l_, p_ in zip(al, l, p)]
            acc = [a * ac + _dot(p_.astype(bf16), v) for a, ac, p_ in zip(al, acc, p)]
            return m2, l, acc

        carry = ([jnp.full((rs, 1), -jnp.inf, f32)] * nsub, [jnp.zeros((rs, 1), f32)] * nsub,
                 [jnp.zeros((rs, HEAD_DIM), f32)] * nsub)
        for d in range(nd):
            carry = tile(i * nd + d, carry, True)
        m, l, acc = lax.fori_loop(0, i * nd, lambda kb, cr: tile(kb, cr, False), carry)
        for s in subs:
            o_ref[pl.ds(s * rs, rs), :] = acc[s] / l[s]
            lse_ref[pl.ds(s * rs, rs), :] = jnp.broadcast_to(m[s] + jnp.log(l[s]), (rs, HEAD_DIM))

    blk = pl.BlockSpec((tq, HEAD_DIM), lambda h, i: (i, h))
    full = pl.BlockSpec((S, HEAD_DIM), lambda h, i: (0, h))
    cspec = pl.BlockSpec((tq, LANES), lambda h, i: (i, 0))
    ctspec = pl.BlockSpec((8, S), lambda h, i: (lane0 // 8, 0))
    sh = SDS((S, nh * HEAD_DIM), f32)
    return _call(body, name, (nh, S // tq), [blk, full, full, cspec, ctspec], (blk, blk), (sh, sh), comm=comm)(fq, fk, fv, c, cT)


def fox_bwd(fq, fk, fv, c, cT, o, lse, do, nh, lane0, name, comm=None):
    S = fq.shape[0]
    tq, tk, rs, nd = _sb_geometry(S, FOX_TK)
    nsub = tq // rs
    scale = HEAD_DIM ** -0.5

    def body(q_ref, k_ref, v_ref, c_ref, ct_ref, o_ref, lse_ref, do_ref, dq_ref, dk_ref, dv_ref, dct_ref):
        h = pl.program_id(0)
        i = pl.program_id(1)

        @pl.when(i == 0)
        def _():
            dk_ref[...] = jnp.zeros_like(dk_ref)
            dv_ref[...] = jnp.zeros_like(dv_ref)

        @pl.when((i == 0) & (h == 0))
        def _():
            dct_ref[...] = jnp.zeros_like(dct_ref)

        subs = range(nsub)
        qs = [q_ref[pl.ds(s * rs, rs), :] for s in subs]
        dof = [do_ref[pl.ds(s * rs, rs), :] for s in subs]
        dos = [x.astype(bf16) for x in dof]
        ci = [_lane_pick(c_ref[pl.ds(s * rs, rs), :], lane0 + h) for s in subs]
        lses = [lse_ref[pl.ds(s * rs, rs), :][:, :1] for s in subs]
        dl = [jnp.sum(x * o_ref[pl.ds(s * rs, rs), :], axis=1, keepdims=True) for s, x in zip(subs, dof)]
        rows = [i * tq + s * rs + lax.broadcasted_iota(jnp.int32, (rs, tk), 0) for s in subs]

        def tile(kb, carry, masked):
            dq, rsum = carry
            off = pl.multiple_of(kb * tk, tk)
            k = k_ref[pl.ds(off, tk), :]
            v = v_ref[pl.ds(off, tk), :]
            cj = ct_ref[pl.ds(lane0 % 8 + h, 1), pl.ds(off, tk)]
            p = [jnp.exp(_dot(q, k, NT) * scale + (c_ - cj) - ls) for q, c_, ls in zip(qs, ci, lses)]
            if masked:
                col = kb * tk + lax.broadcasted_iota(jnp.int32, (rs, tk), 1)
                p = [jnp.where(col <= row, x, 0.0) for x, row in zip(p, rows)]
            ds = [p_ * (_dot(d_, v, NT) - dl_) for p_, d_, dl_ in zip(p, dos, dl)]
            dsb = [(x * scale).astype(bf16) for x in ds]
            dk_ref[pl.ds(off, tk), :] += sum(_dot(x, q, TN) for x, q in zip(dsb, qs))
            dv_ref[pl.ds(off, tk), :] += sum(_dot(p_.astype(bf16), d_, TN) for p_, d_ in zip(p, dos))
            dct_ref[pl.ds(lane0 + h, 1), pl.ds(off, tk)] -= sum(jnp.sum(x, axis=0, keepdims=True) for x in ds)
            return ([g + _dot(x, k) for g, x in zip(dq, dsb)],
                    [r_ + jnp.sum(x, axis=1, keepdims=True) for r_, x in zip(rsum, ds)])

        carry = ([jnp.zeros((rs, HEAD_DIM), f32)] * nsub, [jnp.zeros((rs, 1), f32)] * nsub)
        carry = lax.fori_loop(0, i * nd, lambda kb, cr: tile(kb, cr, False), carry)
        for d in range(nd):
            carry = tile(i * nd + d, carry, True)
        dq, rsum = carry
        for s in subs:
            dq_ref[pl.ds(s * rs, rs), :] = dq[s]
        dct_ref[pl.ds(lane0 + h, 1), pl.ds(pl.multiple_of(i * tq, tq), tq)] += jnp.concatenate([r_.T for r_ in rsum], axis=1)

    blk = pl.BlockSpec((tq, HEAD_DIM), lambda h, i: (i, h))
    full = pl.BlockSpec((S, HEAD_DIM), lambda h, i: (0, h))
    cspec = pl.BlockSpec((tq, LANES), lambda h, i: (i, 0))
    ctspec = pl.BlockSpec((8, S), lambda h, i: (lane0 // 8, 0))
    dctspec = pl.BlockSpec((LANES, S), lambda h, i: (0, 0))
    sh = SDS((S, nh * HEAD_DIM), f32)
    return _call(body, name, (nh, S // tq), [blk, full, full, cspec, ctspec, blk, blk, blk], (blk, full, full, dctspec),
                 (sh, sh, sh, SDS((LANES, S), f32)), comm=comm)(fq, fk, fv, c, cT, o, lse, do)


def gates_fwd(proj, small_cb, fbias_row, name, tm=256):
    S = proj.shape[0]
    tm = _tile(S, tm)

    def body(sm_ref, fb_ref, c_ref, ct_ref, carry_ref):
        @pl.when(pl.program_id(0) == 0)
        def _():
            carry_ref[...] = jnp.zeros_like(carry_ref)

        lf = _logsig(sm_ref[...] + fb_ref[...])
        r = lax.broadcasted_iota(jnp.int32, (tm, tm), 0)
        cc = lax.broadcasted_iota(jnp.int32, (tm, tm), 1)
        c = _dot((r >= cc).astype(f32), lf, NN, HI) + carry_ref[...]
        carry_ref[...] += jnp.sum(lf, axis=0, keepdims=True)
        c_ref[...] = c
        ct_ref[...] = c.T

    return _call(body, name, (S // tm,), [_rowspec(tm, LANES, small_cb), _bspec(LANES)],
                 (_rowspec(tm, LANES), pl.BlockSpec((LANES, tm), lambda i: (0, i))),
                 (SDS((S, LANES), f32), SDS((LANES, S), f32)), scratch=[pltpu.VMEM((1, LANES), f32)])(proj, fbias_row)


def gates_bwd(proj, small_cb, fbias_row, dcT, dsm_dn, lane0, nh, name, tm=256):
    S = proj.shape[0]
    tm = _tile(S, tm)
    nt = S // tm

    def body(sm_ref, fb_ref, dct_ref, dsm_ref, o_ref, dfb_ref, carry_ref):
        @pl.when(pl.program_id(0) == 0)
        def _():
            carry_ref[...] = jnp.zeros_like(carry_ref)
            dfb_ref[...] = jnp.zeros_like(dfb_ref)

        dc = dct_ref[...].T
        r = lax.broadcasted_iota(jnp.int32, (tm, tm), 0)
        cc = lax.broadcasted_iota(jnp.int32, (tm, tm), 1)
        dlf = _dot((r <= cc).astype(f32), dc, NN, HI) + carry_ref[...]
        carry_ref[...] += jnp.sum(dc, axis=0, keepdims=True)
        lane = lax.broadcasted_iota(jnp.int32, (1, LANES), 1)
        dpre = jnp.where((lane >= lane0) & (lane < lane0 + nh), dlf * jax.nn.sigmoid(-(sm_ref[...] + fb_ref[...])), 0.0)
        o_ref[...] = dsm_ref[...] + dpre
        dfb_ref[...] += jnp.sum(dpre, axis=0, keepdims=True)

    rev = lambda i: nt - 1 - i
    return _call(body, name, (nt,),
                 [pl.BlockSpec((tm, LANES), lambda i: (rev(i), small_cb)), _bspec(LANES),
                  pl.BlockSpec((LANES, tm), lambda i: (0, rev(i))), pl.BlockSpec((tm, LANES), lambda i: (rev(i), 0))],
                 (pl.BlockSpec((tm, LANES), lambda i: (rev(i), 0)), _bspec(LANES)),
                 (SDS((S, LANES), f32), SDS((1, LANES), f32)), scratch=[pltpu.VMEM((1, LANES), f32)])(proj, fbias_row, dcT, dsm_dn)


PAD_ROWS = 8


def conv_fwd(proj, w, width, name):
    S = proj.shape[0]
    cw = 256 if width % 256 == 0 else 128

    def body(x_ref, w_ref, y_ref, pad_ref):
        pad_ref[pl.ds(0, PAD_ROWS), :] = jnp.zeros((PAD_ROWS, cw), f32)
        pad_ref[pl.ds(PAD_ROWS, S), :] = x_ref[...]
        y = jnp.zeros((S, cw), f32)
        for i in range(CONV_WIDTH):
            y = y + pad_ref[pl.ds(PAD_ROWS - (CONV_WIDTH - 1) + i, S), :] * w_ref[pl.ds(i, 1), :]
        y_ref[...] = y * jax.nn.sigmoid(y)

    spec = pl.BlockSpec((S, cw), lambda j: (0, j))
    return _call(body, name, (width // cw,), [spec, pl.BlockSpec((CONV_WIDTH, cw), lambda j: (0, j))], spec,
                 SDS((S, width), f32), scratch=[pltpu.VMEM((S + PAD_ROWS, cw), f32)])(proj, w)


def conv_bwd(proj, w, dy, name):
    S, width = dy.shape
    cw = 256 if width % 256 == 0 else 128

    def body(x_ref, w_ref, dy_ref, dx_ref, dw_ref, xpad_ref, dpad_ref):
        xpad_ref[pl.ds(0, PAD_ROWS), :] = jnp.zeros((PAD_ROWS, cw), f32)
        xpad_ref[pl.ds(PAD_ROWS, S), :] = x_ref[...]
        y = jnp.zeros((S, cw), f32)
        for i in range(CONV_WIDTH):
            y = y + xpad_ref[pl.ds(PAD_ROWS - (CONV_WIDTH - 1) + i, S), :] * w_ref[pl.ds(i, 1), :]
        sg = jax.nn.sigmoid(y)
        dpre = dy_ref[...] * (sg * (1.0 + y * (1.0 - sg)))
        dpad_ref[pl.ds(S, PAD_ROWS), :] = jnp.zeros((PAD_ROWS, cw), f32)
        dpad_ref[pl.ds(0, S), :] = dpre
        dx = jnp.zeros((S, cw), f32)
        for i in range(CONV_WIDTH):
            dx = dx + dpad_ref[pl.ds(CONV_WIDTH - 1 - i, S), :] * w_ref[pl.ds(i, 1), :]
            dw_ref[pl.ds(i, 1), :] = jnp.sum(dpre * xpad_ref[pl.ds(PAD_ROWS - (CONV_WIDTH - 1) + i, S), :], axis=0, keepdims=True)
        dx_ref[...] = dx

    spec = pl.BlockSpec((S, cw), lambda j: (0, j))
    wspec = pl.BlockSpec((CONV_WIDTH, cw), lambda j: (0, j))
    return _call(body, name, (width // cw,), [spec, wspec, spec], (spec, wspec),
                 (SDS((S, width), f32), SDS((CONV_WIDTH, width), f32)),
                 scratch=[pltpu.VMEM((S + PAD_ROWS, cw), f32), pltpu.VMEM((S + PAD_ROWS, cw), f32)])(proj, w, dy)


def _pdot_raw(a, b, dims, passes):
    if passes == 1:
        return _dot(a.astype(bf16), b.astype(bf16), dims)
    ah, al = _split_bf16(a)
    bh, bl = _split_bf16(b)
    return _dot(ah, bh, dims) + (_dot(ah, bl, dims) + _dot(al, bh, dims))


def _make_pdot(passes):
    @functools.partial(jax.custom_vjp, nondiff_argnums=(2,))
    def pdot(a, b, mode):
        return _pdot_raw(a, b, {"nn": NN, "nt": NT, "tn": TN}[mode], passes)

    def fwd(a, b, mode):
        return pdot(a, b, mode), (a, b)

    def bwd(mode, res, g):
        a, b = res
        if mode == "nn":
            return _pdot_raw(g, b, NT, passes), _pdot_raw(a, g, TN, passes)
        if mode == "nt":
            return _pdot_raw(g, b, NN, passes), _pdot_raw(g, a, TN, passes)
        return _pdot_raw(b, g, NT, passes), _pdot_raw(a, g, NN, passes)

    pdot.defvjp(fwd, bwd)
    return pdot


_dot1 = _make_pdot(1)
_dot3 = _make_pdot(3)


def _each(f, *lists):
    return [f(*xs) for xs in zip(*lists)]


def _dn_chunk(ndn, cq, ck, cv, small, alog, dtb, s0):
    C = cq[0].shape[0]
    hs = list(range(ndn))
    b = [_lane_pick(small, h) for h in hs]
    d = [_lane_pick(small, ndn + h) for h in hs]
    al = [_lane_pick(alog, h) for h in hs]
    dt = [_lane_pick(dtb, h) for h in hs]
    q = _each(lambda x: x * lax.rsqrt(jnp.sum(x * x, axis=1, keepdims=True) + EPS) * (HEAD_DIM ** -0.5), cq)
    k = _each(lambda x: x * lax.rsqrt(jnp.sum(x * x, axis=1, keepdims=True) + EPS), ck)
    beta = _each(jax.nn.sigmoid, b)
    g = _each(lambda al_, d_, dt_: -jnp.exp(al_) * _softplus(d_ + dt_), al, d, dt)
    r = lax.broadcasted_iota(jnp.int32, (C, C), 0)
    c = lax.broadcasted_iota(jnp.int32, (C, C), 1)
    incl, strict = r >= c, r > c
    inclf = incl.astype(f32)
    gc = _each(lambda g_: _dot3(inclf, g_, "nn"), g)
    decay = _each(lambda gc_: jnp.where(incl, jnp.exp(jnp.where(incl, gc_ - gc_.T, 0.0)), 0.0), gc)
    kb = _each(lambda k_, b_: k_ * b_, k, beta)
    a = _each(lambda kb_, k_, dec: jnp.where(strict, _dot3(kb_, k_, "nt") * dec, 0.0), kb, k, decay)
    eye = (r == c).astype(f32)
    t = _each(lambda a_: eye - a_, a)
    p = a
    for _ in range(int(math.log2(C)) - 1):
        p = _each(lambda p_: _dot3(p_, p_, "nn"), p)
        t = _each(lambda t_, p_: t_ + _dot3(t_, p_, "nn"), t, p)
    eg = _each(jnp.exp, gc)
    u = _each(lambda t_, v_, b_: _dot3(t_, v_ * b_, "nn"), t, cv, beta)
    w = _each(lambda t_, kb_, eg_: _dot3(t_, kb_ * eg_, "nn"), t, kb, eg)
    attn = _each(lambda q_, k_, dec: _dot1(q_, k_, "nt") * dec, q, k, decay)
    g_last = _each(lambda g_: jnp.sum(g_, axis=0, keepdims=True), g)
    v_new = _each(lambda u_, w_, s_: u_ - _dot1(w_, s_, "nn"), u, w, s0)
    o = _each(lambda q_, eg_, s_, at, vn: _dot1(q_ * eg_, s_, "nn") + _dot1(at, vn, "nn"), q, eg, s0, attn, v_new)
    s1 = _each(lambda s_, gl, k_, gc_, vn: s_ * jnp.exp(gl) + _dot1(k_ * jnp.exp(gl - gc_), vn, "tn"), s0, g_last, k, gc, v_new)
    return o, s1


def dn_fwd(cqkv, proj, small_cb, alog_row, dt_row, ndn, name, comm=None):
    S = cqkv.shape[0]
    C = DN_CHUNK
    nc = S // C
    half = ndn * HEAD_DIM

    def body(q_ref, k_ref, v_ref, sm_ref, al_ref, dt_ref, o_ref, ss_ref, s_ref):
        @pl.when(pl.program_id(0) == 0)
        def _():
            s_ref[...] = jnp.zeros_like(s_ref)

        sls = [slice(h * HEAD_DIM, (h + 1) * HEAD_DIM) for h in range(ndn)]
        s0 = [s_ref[h] for h in range(ndn)]
        for h in range(ndn):
            ss_ref[h, 0] = s0[h]
        o, s1 = _dn_chunk(ndn, [q_ref[:, sl] for sl in sls], [k_ref[:, sl] for sl in sls], [v_ref[:, sl] for sl in sls],
                          sm_ref[...], al_ref[...], dt_ref[...], s0)
        for h in range(ndn):
            o_ref[:, sls[h]] = o[h]
            s_ref[h] = s1[h]

    blk = lambda cb: pl.BlockSpec((C, half), lambda n: (n, cb))
    row = pl.BlockSpec((1, LANES), lambda n: (0, 0))
    return _call(body, name, (nc,),
                 [blk(0), blk(1), blk(2), pl.BlockSpec((C, LANES), lambda n: (n, small_cb)), row, row],
                 (blk(0), pl.BlockSpec((ndn, 1, HEAD_DIM, HEAD_DIM), lambda n: (0, n, 0, 0))),
                 (SDS((S, half), f32), SDS((ndn, nc, HEAD_DIM, HEAD_DIM), f32)),
                 scratch=[pltpu.VMEM((ndn, HEAD_DIM, HEAD_DIM), f32)], comm=comm)(cqkv, cqkv, cqkv, proj, alog_row, dt_row)


def dn_bwd(cqkv, proj, small_cb, alog_row, dt_row, ssave, do, ndn, name, comm=None):
    S = cqkv.shape[0]
    C = DN_CHUNK
    nc = S // C
    half = ndn * HEAD_DIM

    def body(q_ref, k_ref, v_ref, sm_ref, al_ref, dt_ref, ss_ref, do_ref, dqkv_ref, dsm_ref, dal_ref, ddt_ref, ds_ref):
        @pl.when(pl.program_id(0) == 0)
        def _():
            ds_ref[...] = jnp.zeros_like(ds_ref)
            dal_ref[...] = jnp.zeros_like(dal_ref)
            ddt_ref[...] = jnp.zeros_like(ddt_ref)

        sls = [slice(h * HEAD_DIM, (h + 1) * HEAD_DIM) for h in range(ndn)]
        _, vjp = jax.vjp(functools.partial(_dn_chunk, ndn), [q_ref[:, sl] for sl in sls], [k_ref[:, sl] for sl in sls],
                         [v_ref[:, sl] for sl in sls], sm_ref[...], al_ref[...], dt_ref[...], [ss_ref[h, 0] for h in range(ndn)])
        dq, dk, dv, dsm, dal, ddt, ds0 = vjp(([do_ref[:, sl] for sl in sls], [ds_ref[h] for h in range(ndn)]))
        for h in range(ndn):
            dqkv_ref[:, sls[h]] = dq[h]
            dqkv_ref[:, half + h * HEAD_DIM:half + (h + 1) * HEAD_DIM] = dk[h]
            dqkv_ref[:, 2 * half + h * HEAD_DIM:2 * half + (h + 1) * HEAD_DIM] = dv[h]
            ds_ref[h] = ds0[h]
        dsm_ref[...] = dsm
        dal_ref[...] += dal
        ddt_ref[...] += ddt

    rev = lambda n: nc - 1 - n
    blk = lambda cb: pl.BlockSpec((C, half), lambda n: (rev(n), cb))
    row = pl.BlockSpec((1, LANES), lambda n: (0, 0))
    return _call(body, name, (nc,),
                 [blk(0), blk(1), blk(2), pl.BlockSpec((C, LANES), lambda n: (rev(n), small_cb)), row, row,
                  pl.BlockSpec((ndn, 1, HEAD_DIM, HEAD_DIM), lambda n: (0, rev(n), 0, 0)), blk(0)],
                 (pl.BlockSpec((C, 3 * half), lambda n: (rev(n), 0)), pl.BlockSpec((C, LANES), lambda n: (rev(n), 0)), row, row),
                 (SDS((S, 3 * half), f32), SDS((S, LANES), f32), SDS((1, LANES), f32), SDS((1, LANES), f32)),
                 scratch=[pltpu.VMEM((ndn, HEAD_DIM, HEAD_DIM), f32)], comm=comm)(cqkv, cqkv, cqkv, proj, alog_row, dt_row, ssave, do)


def even_pre(proj, gq, gk, dfx, cb0, name):
    S = proj.shape[0]
    tm = _tile(S, 256)
    nh = dfx // HEAD_DIM

    def body(q_ref, k_ref, v_ref, gq_ref, gk_ref, fq_ref, fk_ref, fv_ref):
        for h in range(nh):
            sl = slice(h * HEAD_DIM, (h + 1) * HEAD_DIM)
            fq_ref[:, sl] = _rms(q_ref[:, sl], gq_ref[...]).astype(bf16)
            fk_ref[:, sl] = _rms(k_ref[:, sl], gk_ref[...]).astype(bf16)
        fv_ref[...] = v_ref[...].astype(bf16)

    sh = SDS((S, dfx), bf16)
    o = _rowspec(tm, dfx)
    return _call(body, name, (S // tm,),
                 [_rowspec(tm, dfx, cb0), _rowspec(tm, dfx, cb0 + 1), _rowspec(tm, dfx, cb0 + 2), _bspec(HEAD_DIM), _bspec(HEAD_DIM)],
                 (o, o, o), (sh, sh, sh))(proj, proj, proj, gq, gk)


def even_pre_bwd(proj, gq, gk, dfq, dfk, dfx, cb0, name):
    S = proj.shape[0]
    tm = _tile(S, 256)
    nh = dfx // HEAD_DIM

    def body(q_ref, k_ref, gq_ref, gk_ref, dfq_ref, dfk_ref, dq_ref, dk_ref, dgq_ref, dgk_ref):
        @pl.when(pl.program_id(0) == 0)
        def _():
            dgq_ref[...] = jnp.zeros_like(dgq_ref)
            dgk_ref[...] = jnp.zeros_like(dgk_ref)

        for h in range(nh):
            sl = slice(h * HEAD_DIM, (h + 1) * HEAD_DIM)
            _, vq = jax.vjp(_rms, q_ref[:, sl], gq_ref[...])
            dq, dgq = vq(dfq_ref[:, sl])
            _, vk = jax.vjp(_rms, k_ref[:, sl], gk_ref[...])
            dk, dgk = vk(dfk_ref[:, sl])
            dq_ref[:, sl] = dq
            dk_ref[:, sl] = dk
            dgq_ref[...] += dgq
            dgk_ref[...] += dgk

    sh = SDS((S, dfx), f32)
    o = _rowspec(tm, dfx)
    g = SDS((1, HEAD_DIM), f32)
    return _call(body, name, (S // tm,),
                 [_rowspec(tm, dfx, cb0), _rowspec(tm, dfx, cb0 + 1), _bspec(HEAD_DIM), _bspec(HEAD_DIM), o, o],
                 (o, o, _bspec(HEAD_DIM), _bspec(HEAD_DIM)), (sh, sh, g, g))(proj, proj, gq, gk, dfq, dfk)


def _dn_out(o, gate, gn):
    return _rms(o, gn) * (gate * jax.nn.sigmoid(gate))


def _fox_out(o, gate):
    return o * jax.nn.sigmoid(gate)


def even_post(o_dn, o_fox, proj, gn, half, cb_dn_gate, cb_fox_gate, name):
    S = proj.shape[0]
    tm = _tile(S, 256)
    nh = half // HEAD_DIM

    def body(od_ref, of_ref, gd_ref, gf_ref, gn_ref, o_ref):
        for h in range(nh):
            sl = slice(h * HEAD_DIM, (h + 1) * HEAD_DIM)
            o_ref[:, sl] = _dn_out(od_ref[:, sl], gd_ref[:, sl], gn_ref[...]).astype(bf16)
        o_ref[:, half:] = _fox_out(of_ref[...], gf_ref[...]).astype(bf16)

    return _call(body, name, (S // tm,),
                 [_rowspec(tm, half), _rowspec(tm, half), _rowspec(tm, half, cb_dn_gate), _rowspec(tm, half, cb_fox_gate), _bspec(HEAD_DIM)],
                 _rowspec(tm, 2 * half), SDS((S, 2 * half), bf16))(o_dn, o_fox, proj, proj, gn)


def even_post_bwd(o_dn, o_fox, proj, gn, dom, half, cb_dn_gate, cb_fox_gate, name):
    S = proj.shape[0]
    tm = _tile(S, 256)
    nh = half // HEAD_DIM

    def body(od_ref, of_ref, gd_ref, gf_ref, gn_ref, dod_ref, dof_ref, dd_ref, df_ref, dgd_ref, dgf_ref, dgn_ref):
        @pl.when(pl.program_id(0) == 0)
        def _():
            dgn_ref[...] = jnp.zeros_like(dgn_ref)

        for h in range(nh):
            sl = slice(h * HEAD_DIM, (h + 1) * HEAD_DIM)
            _, vjp = jax.vjp(_dn_out, od_ref[:, sl], gd_ref[:, sl], gn_ref[...])
            d_o, d_gate, d_gn = vjp(dod_ref[:, sl])
            dd_ref[:, sl] = d_o
            dgd_ref[:, sl] = d_gate
            dgn_ref[...] += d_gn
        _, vjp = jax.vjp(_fox_out, of_ref[...], gf_ref[...])
        d_o, d_gate = vjp(dof_ref[...])
        df_ref[...] = d_o
        dgf_ref[...] = d_gate

    sh = SDS((S, half), f32)
    o = _rowspec(tm, half)
    return _call(body, name, (S // tm,),
                 [o, o, _rowspec(tm, half, cb_dn_gate), _rowspec(tm, half, cb_fox_gate), _bspec(HEAD_DIM),
                  _rowspec(tm, half, 0), _rowspec(tm, half, 1)],
                 (o, o, o, o, _bspec(HEAD_DIM)), (sh, sh, sh, sh, SDS((1, HEAD_DIM), f32)))(o_dn, o_fox, proj, proj, gn, dom, dom)


def _pad_row(v, lane0=0):
    return jnp.pad(v, (lane0, LANES - lane0 - v.shape[0]))[None]


def _carried(res, comm):
    return res if comm is not None else (res, [])


def even_mixer_fwd(x, gmix, w_in, w_out, conv_w, a_log, dt_bias, gn, gq, gk, f_bias, tag, comm_fox=None, comm_dn=None,
                   comm_in=None):
    S, D = x.shape
    half = D // 2
    ndn = half // HEAD_DIM
    small_cb = 4 * D // LANES
    alog_row, dt_row, fb_row = _pad_row(a_log), _pad_row(dt_bias), _pad_row(f_bias, 2 * ndn)
    h = rms_fwd(x, gmix, f"{tag}_rms")
    proj, got_in = _carried(mm(h, w_in, "nn", f32, f"{tag}_in", tm=512, tn=1408, comm=comm_in), comm_in)
    cqkv = conv_fwd(proj, conv_w, 3 * half, f"{tag}_conv")
    (o_dn, ssave), got_dn = _carried(dn_fwd(cqkv, proj, small_cb, alog_row, dt_row, ndn, f"{tag}_dn", comm=comm_dn), comm_dn)
    c, cT = gates_fwd(proj, small_cb, fb_row, f"{tag}_gates")
    fq, fk, fv = even_pre(proj, gq, gk, half, 4, f"{tag}_pre")
    (o_fox, lse), got_fox = _carried(fox_fwd(fq, fk, fv, c, cT, ndn, 2 * ndn, f"{tag}_fox", comm=comm_fox), comm_fox)
    om = even_post(o_dn, o_fox, proj, gn, half, 3, 7, f"{tag}_post")
    xo = mm(om, w_out, "nn", f32, f"{tag}_out", res=x)
    return xo, (h, proj, cqkv, o_dn, ssave, c, cT, fq, fk, fv, o_fox, lse, om, alog_row, dt_row, fb_row), got_fox, got_dn, got_in


def even_mixer_bwd(x, gmix, w_in, w_out, conv_w, gn, gq, gk, saved, dy, tag, comm_fox=None, comm_dn=None, comm_bdh=None):
    S, D = x.shape
    half = D // 2
    ndn = half // HEAD_DIM
    small_cb = 4 * D // LANES
    h, proj, cqkv, o_dn, ssave, c, cT, fq, fk, fv, o_fox, lse, om, alog_row, dt_row, fb_row = saved
    dy32, dy16 = dy
    dom = mm(dy16, w_out, "nt", f32, f"{tag}_bdom")
    dw_out = mm(om, dy16, "tn", bf16, f"{tag}_bwout")
    d_odn, d_ofox, d_dgate, d_fgate, d_gn = even_post_bwd(o_dn, o_fox, proj, gn, dom, half, 3, 7, f"{tag}_bpost")
    (dfq, dfk, dfv, dcT), got_fox = _carried(
        fox_bwd(fq, fk, fv, c, cT, o_fox, lse, d_ofox, ndn, 2 * ndn, f"{tag}_bfox", comm=comm_fox), comm_fox)
    dq_pre, dk_pre, d_gq, d_gk = even_pre_bwd(proj, gq, gk, dfq, dfk, half, 4, f"{tag}_bpre")
    (dcqkv, dsm_dn, d_alog, d_dt), got_dn = _carried(
        dn_bwd(cqkv, proj, small_cb, alog_row, dt_row, ssave, d_odn, ndn, f"{tag}_bdn", comm=comm_dn), comm_dn)
    d_conv_in, d_conv_w = conv_bwd(proj, conv_w, dcqkv, f"{tag}_bconv")
    d_small, d_fb = gates_bwd(proj, small_cb, fb_row, dcT, dsm_dn, 2 * ndn, ndn, f"{tag}_bgates")
    dproj = jnp.concatenate([d_conv_in, d_dgate, dq_pre, dk_pre, dfv, d_fgate, d_small], axis=1).astype(bf16)
    dw_in = mm(h, dproj, "tn", bf16, f"{tag}_bwin", tn=1408)
    res = mm_bdh_rms(dproj, w_in, x, gmix, dy32, f"{tag}_bdh", comm=comm_bdh)
    dx, d_gmix = res[:2]
    got_bdh = res[2] if comm_bdh is not None else []
    small = dict(norm_mix=d_gmix[0], dn_conv_w=d_conv_w, dn_a_log=d_alog[0, :ndn], dn_dt_bias=d_dt[0, :ndn],
                 dn_norm_g=d_gn[0], fox_q_norm_g=d_gq[0], fox_k_norm_g=d_gk[0], fox_f_bias=d_fb[0, 2 * ndn:3 * ndn])
    return dx, dw_in, dw_out, small, got_fox, got_dn, got_bdh


def odd_mixer_fwd(x, gmix, w_in, w_out, tag, comm=None):
    S, D = x.shape
    nh = D // HEAD_DIM
    h = rms_fwd(x, gmix, f"{tag}_rms")
    qkv = mm(h, w_in, "nn", bf16, f"{tag}_in", tn=768)
    (o, tails), got = _carried(sb_fwd(qkv, nh, f"{tag}_sb", comm=comm), comm)
    xo = mm(o, w_out, "nn", f32, f"{tag}_out", res=x)
    return xo, (h, qkv, o, tails), got


def odd_mixer_bwd(x, gmix, w_in, w_out, saved, dy, tag, comm=None):
    S, D = x.shape
    nh = D // HEAD_DIM
    h, qkv, o, tails = saved
    dy32, dy16 = dy
    do = mm(dy16, w_out, "nt", f32, f"{tag}_bdo")
    dw_out = mm(o, dy16, "tn", bf16, f"{tag}_bwout")
    (dq, dk, dv), got = _carried(sb_bwd(qkv, tails, do, nh, f"{tag}_bsb", comm=comm), comm)
    dqkv = jnp.concatenate([dq, dk, dv], axis=1).astype(bf16)
    dw_in = mm(h, dqkv, "tn", bf16, f"{tag}_bwin", tn=1536)
    dx, d_gmix = mm_bdh_rms(dqkv, w_in, x, gmix, dy32, f"{tag}_bdh")
    return dx, dw_in, dw_out, dict(norm_mix=d_gmix[0]), got


def all_gather(shards, axes, name, kind="ag"):
    return _call(None, name, (), [], [], [], comm=Comm(kind, shards, axes))()[1]


def sum_parts(parts, name):
    _, R, C = parts.shape
    tm = _tile(R, 256)

    def body(p_ref, o_ref):
        acc = p_ref[0].astype(f32)
        for k in range(1, NDEV):
            acc = acc + p_ref[k].astype(f32)
        o_ref[...] = acc

    return _call(body, name, (R // tm,), [pl.BlockSpec((NDEV, tm, C), lambda i: (0, i, 0))], _rowspec(tm, C), SDS((R, C), f32))(parts)


def adamw(w, g, m, v, name):
    L, R, C = w.shape
    tm = _tile(R, max(8, min(512, (ADAMW_TILE_ELEMS // C) // 8 * 8)))
    c1 = 1.0 - ADAM_B1 ** ADAM_STEP
    c2 = 1.0 - ADAM_B2 ** ADAM_STEP

    def body(w_ref, g_ref, m_ref, v_ref, d_ref, nm_ref, nv_ref):
        g_ = g_ref[...]
        nm = ADAM_B1 * m_ref[...] + (1.0 - ADAM_B1) * g_
        nv = ADAM_B2 * v_ref[...] + (1.0 - ADAM_B2) * (g_ * g_)
        d_ref[...] = -ADAM_LR * ((nm / c1) / (jnp.sqrt(nv / c2) + ADAM_EPS) + ADAM_WD * w_ref[...])
        nm_ref[...] = nm
        nv_ref[...] = nv

    spec = pl.BlockSpec((None, tm, C), lambda l, i: (l, i, 0))
    sh = SDS((L, R, C), f32)
    return _call(body, name, (L, R // tm), [spec] * 4, (spec, spec, spec), (sh, sh, sh))(w, g, m, v)


def _pad_to(n, mult):
    return -(-n // mult) * mult


def _even_perm(D):
    half = D // 2
    ndn = half // HEAD_DIM
    o_gate = 3 * half
    o_b = o_gate + half
    o_a = o_b + ndn
    o_fq = o_a + ndn
    o_fpre = o_fq + 4 * half
    return half, ndn, o_b, o_a, o_fq, o_fpre


def _even_to_mine(w):
    D = (w.shape[-1] // 4) // LANES * LANES
    half, ndn, o_b, o_a, o_fq, o_fpre = _even_perm(D)
    small = jnp.concatenate([w[..., o_b:o_b + ndn], w[..., o_a:o_a + ndn], w[..., o_fpre:o_fpre + ndn]], axis=-1)
    small = jnp.pad(small, [(0, 0)] * (w.ndim - 1) + [(0, LANES - 3 * ndn)])
    return jnp.concatenate([w[..., :o_b], w[..., o_fq:o_fpre], small], axis=-1)


def _even_from_mine(w, D):
    half, ndn, o_b, o_a, o_fq, o_fpre = _even_perm(D)
    sm = w[..., 4 * D:]
    return jnp.concatenate([w[..., :o_b], sm[..., :ndn], sm[..., ndn:2 * ndn], w[..., o_b:4 * D], sm[..., 2 * ndn:3 * ndn]], axis=-1)


def _pack_small(parts):
    flat = jnp.concatenate([p.reshape(-1).astype(f32) for p in parts])
    n = flat.shape[0]
    return jnp.pad(flat, (0, _pad_to(n, 8 * LANES) - n)).reshape(-1, LANES)


def _unpack_small(packed, like):
    flat = packed.reshape(-1)
    out, off = [], 0
    for p in like:
        out.append(flat[off:off + p.size].reshape(p.shape))
        off += p.size
    return out


SMALL_NAMES = ("norm_ffn1", "norm_mix", "dn_a_log", "dn_dt_bias", "dn_norm_g", "fox_q_norm_g", "fox_k_norm_g", "fox_f_bias", "norm_ffn2")
BIG_NAMES = ("ffn1_w_gu", "ffn1_w_down", "w_in_even", "dn_conv_w", "w_out_even", "w_in_odd", "w_out_odd", "ffn2_w_gu", "ffn2_w_down")
WEIGHT_NAMES = ("norm_ffn1", "ffn1_w_gu", "ffn1_w_down", "norm_mix", "w_in_even", "dn_conv_w", "dn_a_log", "dn_dt_bias", "dn_norm_g",
                "fox_q_norm_g", "fox_k_norm_g", "fox_f_bias", "w_out_even", "w_in_odd", "w_out_odd", "norm_ffn2", "ffn2_w_gu", "ffn2_w_down")


def kernel(x, norm_ffn1, ffn1_w_gu, ffn1_w_down, norm_mix, w_in_even, dn_conv_w, dn_a_log, dn_dt_bias, dn_norm_g, fox_q_norm_g, fox_k_norm_g, fox_f_bias, w_out_even, w_in_odd, w_out_odd, norm_ffn2, ffn2_w_gu, ffn2_w_down, loss_target, m_norm_ffn1, m_ffn1_w_gu, m_ffn1_w_down, m_norm_mix, m_w_in_even, m_dn_conv_w, m_dn_a_log, m_dn_dt_bias, m_dn_norm_g, m_fox_q_norm_g, m_fox_k_norm_g, m_fox_f_bias, m_w_out_even, m_w_in_odd, m_w_out_odd, m_norm_ffn2, m_ffn2_w_gu, m_ffn2_w_down, v_norm_ffn1, v_ffn1_w_gu, v_ffn1_w_down, v_norm_mix, v_w_in_even, v_dn_conv_w, v_dn_a_log, v_dn_dt_bias, v_dn_norm_g, v_fox_q_norm_g, v_fox_k_norm_g, v_fox_f_bias, v_w_out_even, v_w_in_odd, v_w_out_odd, v_norm_ffn2, v_ffn2_w_gu, v_ffn2_w_down):
    args = dict(locals())
    W = {n: args[n] for n in WEIGHT_NAMES}
    M = {n: args["m_" + n] for n in WEIGHT_NAMES}
    V = {n: args["v_" + n] for n in WEIGHT_NAMES}
    xs = x[0]
    tgt = loss_target[0]
    S, D = xs.shape
    L = norm_ffn1.shape[0]
    n_even = w_in_even.shape[0]
    hs = ffn1_w_down.shape[1]
    hp = _pad_to(hs, LANES)
    me = 4 * lax.axis_index("x") + 2 * lax.axis_index("y") + lax.axis_index("c")

    def prep_gu(w):
        w = w.reshape(L, D, 2, hs)
        return jnp.pad(w, ((0, 0), (0, 0), (0, 0), (0, hp - hs))).reshape(L, D, 2 * hp).astype(bf16)

    def prep_down(w):
        return jnp.pad(w, ((0, 0), (0, hp - hs), (0, 0))).astype(bf16)

    sh_gu1, sh_d1, sh_gu2, sh_d2 = prep_gu(ffn1_w_gu), prep_down(ffn1_w_down), prep_gu(ffn2_w_gu), prep_down(ffn2_w_down)
    sh_ie, sh_oe = _even_to_mine(w_in_even).astype(bf16), w_out_even.astype(bf16)
    sh_io, sh_oo = w_in_odd.astype(bf16), w_out_odd.astype(bf16)

    def layer_shards(l):
        j = l // 2
        mix = [sh_ie[j], sh_oe[j]] if l % 2 == 0 else [sh_io[j], sh_oo[j]]
        return [sh_gu1[l], sh_d1[l], *mix, sh_gu2[l], sh_d2[l]]

    def layer_axes(l):
        return [1, 0, 0 if l % 2 == 0 else 1, 0, 1, 0]

    def layer_names(l):
        return ["ffn1_w_gu", "ffn1_w_down", *(["w_in_even", "w_out_even"] if l % 2 == 0 else ["w_in_odd", "w_out_odd"]),
                "ffn2_w_gu", "ffn2_w_down"]

    FOX_CARRIES, DN_CARRIES = (0, 4, 3), (1, 2, 5)

    def split_comm(kind, arrays, axes):
        return (Comm(kind, [arrays[i] for i in FOX_CARRIES], [axes[i] for i in FOX_CARRIES]),
                Comm(kind, [arrays[i] for i in DN_CARRIES], [axes[i] for i in DN_CARRIES]))

    def join_split(got_fox, got_dn):
        out = [None] * 6
        for i, a in zip(FOX_CARRIES, got_fox):
            out[i] = a
        for i, a in zip(DN_CARRIES, got_dn):
            out[i] = a
        return out

    sh0, ax0 = layer_shards(0), layer_axes(0)
    first = all_gather(sh0[:2] + [dn_conv_w[None]], ax0[:2] + [0], "ag_layer0", kind="ag2")
    weights = {0: first[:2] + [None] * 4}
    convw = jnp.moveaxis(first[2], 0, 2).reshape(n_even, CONV_WIDTH, -1)
    LAYER0_CARRIES = dict(f1_gu=(2,), f1_down=(3,), mx_in=(5,))

    EVEN_FWD_CARRIES = dict(f1_gu=(), mx_in=(), dn=(1, 2, 5), fox=(0, 4, 3), f2_gu=(), f2_down=())
    saved = []
    cur = xs
    for l in range(L):
        j = l // 2
        wgu1, wd1, win, wout, wgu2, wd2 = weights[l]
        nxt = l + 1 < L
        x0 = cur
        if l % 2 == 0:
            sh_n, ax_n = (layer_shards(l + 1), layer_axes(l + 1)) if nxt else (None, None)
            cm = {k: (Comm("ag2", [sh_n[i] for i in idx], [ax_n[i] for i in idx]) if nxt and idx else None)
                  for k, idx in EVEN_FWD_CARRIES.items()}
            if l == 0:
                cm.update({k: Comm("ag2", [sh0[i] for i in idx], [ax0[i] for i in idx]) for k, idx in LAYER0_CARRIES.items()})
                cm["fox"] = Comm("ag2", cm["fox"].arrays + [sh0[4]], cm["fox"].axes + [ax0[4]])
            x1, s1, got_f1gu, got_f1down = ffn_fwd(x0, norm_ffn1[l:l + 1], wgu1, wd1, f"l{l}f1", comm_gu=cm["f1_gu"],
                                                   comm_down=cm["f1_down"] if l == 0 else None)
            if l == 0:
                win, wout = got_f1gu[0], got_f1down[0]
            x2, sm, got_f, got_d, got_in = even_mixer_fwd(
                x1, norm_mix[l:l + 1], win, wout, convw[j], dn_a_log[j], dn_dt_bias[j], dn_norm_g[j:j + 1],
                fox_q_norm_g[j:j + 1], fox_k_norm_g[j:j + 1], fox_f_bias[j], f"l{l}mx", comm_fox=cm["fox"], comm_dn=cm["dn"],
                comm_in=cm["mx_in"])
            if l == 0:
                wd2, wgu2 = got_in[0], got_f[-1]
                weights[0] = [wgu1, wd1, win, wout, wgu2, wd2]
                got_f1gu, got_in, got_f = [], [], got_f[:-1]
            x3, s2, got_f2gu, got_f2down = ffn_fwd(x2, norm_ffn2[l:l + 1], wgu2, wd2, f"l{l}f2", comm_gu=cm["f2_gu"],
                                                   comm_down=cm["f2_down"])
            if nxt:
                got = dict(f1_gu=got_f1gu, mx_in=got_in, dn=got_d, fox=got_f, f2_gu=got_f2gu, f2_down=got_f2down)
                weights[l + 1] = [None] * 6
                for k, idx in EVEN_FWD_CARRIES.items():
                    for i, a in zip(idx, got[k]):
                        weights[l + 1][i] = a
        else:
            x1, s1, _, _ = ffn_fwd(x0, norm_ffn1[l:l + 1], wgu1, wd1, f"l{l}f1")
            cm = Comm("ag2", layer_shards(l + 1), layer_axes(l + 1)) if nxt else None
            x2, sm, got = odd_mixer_fwd(x1, norm_mix[l:l + 1], win, wout, f"l{l}mx", comm=cm)
            if nxt:
                weights[l + 1] = got
            x3, s2, _, _ = ffn_fwd(x2, norm_ffn2[l:l + 1], wgu2, wd2, f"l{l}f2")
        saved.append((x0, x1, x2, s1, sm, s2))
        cur = x3
    dy, loss_row = loss_head(cur, tgt, "loss")

    gsmall = {n: [None] * W[n].shape[0] for n in SMALL_NAMES}
    gconv = [None] * n_even
    parts = {n: [None] * W[n].shape[0] for n in BIG_NAMES if n != "dn_conv_w"}

    def take(l, recv):
        for nm, p in zip(layer_names(l), recv):
            idx = l if nm.startswith("ffn") else l // 2
            parts[nm][idx] = sum_parts(p.reshape(NDEV, -1, p.shape[-1]), f"sum_{nm}_{idx}").reshape(p.shape[1:])

    pending = None
    for l in reversed(range(L)):
        j = l // 2
        wgu1, wd1, win, wout, wgu2, wd2 = weights[l]
        x0, x1, x2, s1, sm, s2 = saved[l]
        dy, dg, dwgu2, dwd2 = ffn_bwd(x2, norm_ffn2[l:l + 1], wgu2, wd2, s2, dy, f"l{l}f2")
        gsmall["norm_ffn2"][l] = dg[0]
        if l % 2 == 0:
            cf, cd = split_comm("rs", pending, layer_axes(l + 1)) if pending is not None else (None, None)
            cb = None
            if l == 0:
                cd = Comm("rs", cd.arrays + [dwd2], cd.axes + [ax0[5]])
                cb = Comm("rs", [dwgu2], [ax0[4]])
            dy, dwin, dwout, sg, got_f, got_d, got_b = even_mixer_bwd(
                x1, norm_mix[l:l + 1], win, wout, convw[j], dn_norm_g[j:j + 1], fox_q_norm_g[j:j + 1],
                fox_k_norm_g[j:j + 1], sm, dy, f"l{l}mx", comm_fox=cf, comm_dn=cd, comm_bdh=cb)
            if l == 0:
                got_wd2, got_wgu2, got_d = got_d[-1], got_b[0], got_d[:-1]
            if pending is not None:
                take(l + 1, join_split(got_f, got_d))
            gconv[j] = sg.pop("dn_conv_w")
            for k_, v_ in sg.items():
                gsmall[k_][l if k_ == "norm_mix" else j] = v_
        else:
            cm = Comm("rs", pending, layer_axes(l + 1)) if pending is not None else None
            dy, dwin, dwout, sg, got = odd_mixer_bwd(x1, norm_mix[l:l + 1], win, wout, sm, dy, f"l{l}mx", comm=cm)
            if pending is not None:
                take(l + 1, got)
            gsmall["norm_mix"][l] = sg["norm_mix"]
        if l > 0:
            dy, dg, dwgu1, dwd1 = ffn_bwd(x0, norm_ffn1[l:l + 1], wgu1, wd1, s1, dy, f"l{l}f1")
        else:
            rs = lambda a, ax: Comm("rs", [a], [ax])
            dy, dg, dwgu1, dwd1, got0 = ffn_bwd(x0, norm_ffn1[l:l + 1], wgu1, wd1, s1, dy, f"l{l}f1",
                                                comms=dict(bda=rs(dwin, ax0[2]), bwd=rs(dwout, ax0[3])), own_axes=(ax0[0], ax0[1]))
        gsmall["norm_ffn1"][l] = dg[0]
        pending = [dwgu1, dwd1, dwin, dwout, dwgu2, dwd2]
    take(0, [got0["wgu"][0], got0["wd"][0], got0["bda"][0], got0["bwd"][0], got_wgu2, got_wd2])
    grad_x = dy[0][None]

    small_list = [jnp.stack(gsmall[n]) for n in SMALL_NAMES] + [jnp.stack(gconv), loss_row[0, :1]]
    packed = _pack_small(small_list)
    gathered = all_gather([packed], [0], "ag_small")[0]
    summed = sum_parts(gathered.reshape(NDEV, packed.shape[0], LANES), "sum_small")
    small_sum = _unpack_small(summed, small_list)
    G = dict(zip(SMALL_NAMES, small_sum[:len(SMALL_NAMES)]))
    conv_full = small_sum[len(SMALL_NAMES)]
    cwid = dn_conv_w.shape[2]
    G["dn_conv_w"] = lax.dynamic_slice_in_dim(conv_full, me * cwid, cwid, axis=2)
    loss = small_sum[-1][0]

    def unprep_gu(g):
        return g.reshape(L, D, 2, hp)[..., :hs].reshape(L, D, 2 * hs)

    G["ffn1_w_gu"] = unprep_gu(jnp.stack(parts["ffn1_w_gu"]))
    G["ffn2_w_gu"] = unprep_gu(jnp.stack(parts["ffn2_w_gu"]))
    G["ffn1_w_down"] = jnp.stack(parts["ffn1_w_down"])[:, :hs]
    G["ffn2_w_down"] = jnp.stack(parts["ffn2_w_down"])[:, :hs]
    G["w_in_even"] = _even_from_mine(jnp.stack(parts["w_in_even"]), D)
    G["w_out_even"] = jnp.stack(parts["w_out_even"])
    G["w_in_odd"] = jnp.stack(parts["w_in_odd"])
    G["w_out_odd"] = jnp.stack(parts["w_out_odd"])

    delta, new_m, new_v = {}, {}, {}
    for n in BIG_NAMES:
        delta[n], new_m[n], new_v[n] = adamw(W[n], G[n], M[n], V[n], f"adamw_{n}")
    sw = [W[n] for n in SMALL_NAMES]
    d_, m_, v_ = adamw(_pack_small(sw)[None], _pack_small([G[n] for n in SMALL_NAMES])[None],
                       _pack_small([M[n] for n in SMALL_NAMES])[None], _pack_small([V[n] for n in SMALL_NAMES])[None], "adamw_small")
    for n, a, b, c_ in zip(SMALL_NAMES, _unpack_small(d_, sw), _unpack_small(m_, sw), _unpack_small(v_, sw)):
        delta[n], new_m[n], new_v[n] = a, b, c_

    return (loss, grad_x, *[G[n] for n in WEIGHT_NAMES], *[delta[n] for n in WEIGHT_NAMES],
            *[new_m[n] for n in WEIGHT_NAMES], *[new_v[n] for n in WEIGHT_NAMES])
```

```python
import functools
import math

import jax
import jax.numpy as jnp
from jax import lax
from jax.experimental import pallas as pl
from jax.experimental.pallas import tpu as pltpu

f32 = jnp.float32
bf16 = jnp.bfloat16
SDS = jax.ShapeDtypeStruct

HEAD_DIM = 128
LANES = 128
CONV_WIDTH = 4
DN_CHUNK = 128
EPS = 1e-6
NDEV = 8
VMEM_LIMIT_BYTES = 56 * 1024 * 1024
HI = lax.Precision.HIGHEST
ADAMW_TILE_ELEMS = 384 * 1024

ADAM_LR = 0.001
ADAM_B1 = 0.9
ADAM_B2 = 0.999
ADAM_EPS = 1e-08
ADAM_WD = 0.01
ADAM_STEP = 10

NN = (((1,), (0,)), ((), ()))
NT = (((1,), (1,)), ((), ()))
TN = (((0,), (0,)), ((), ()))


def _me_and_peers():
    x, y, c = lax.axis_index("x"), lax.axis_index("y"), lax.axis_index("c")
    me = 4 * x + 2 * y + c
    peers = []
    for r in range(1, NDEV):
        px = 1 - x if (r >> 2) & 1 else x
        py = 1 - y if (r >> 1) & 1 else y
        pc = 1 - c if r & 1 else c
        peers.append(((px, py, pc), 4 * px + 2 * py + pc))
    return me, peers


def _slab(ref, axis, width, k):
    idx = [slice(None)] * len(ref.shape)
    idx[axis] = pl.ds(k * width, width)
    return ref.at[tuple(idx)]


class Comm:
    def __init__(self, kind, arrays, axes):
        self.kind, self.arrays, self.axes, self.n = kind, list(arrays), list(axes), len(arrays)

    def out_shape(self):
        out = []
        for a, ax in zip(self.arrays, self.axes):
            shp = list(a.shape)
            if self.kind != "rs":
                shp[ax] *= NDEV
                out.append(SDS(tuple(shp), a.dtype))
            else:
                shp[ax] //= NDEV
                out.append(SDS((NDEV, *shp), a.dtype))
        return out

    def sems(self):
        return [pltpu.SemaphoreType.DMA((self.n, NDEV - 1)), pltpu.SemaphoreType.DMA((self.n, NDEV - 1)),
                pltpu.SemaphoreType.DMA((self.n,))]

    def _src(self, ins, t, to_idx):
        if self.kind == "ag":
            return ins[t]
        return _slab(ins[t], self.axes[t], ins[t].shape[self.axes[t]] // NDEV, to_idx)

    def _dst(self, ins, outs, t, from_idx):
        if self.kind != "rs":
            return _slab(outs[t], self.axes[t], ins[t].shape[self.axes[t]], from_idx)
        return outs[t].at[from_idx]

    def _copies(self, ins, outs, sems, sending):
        send_sems, recv_sems, loc_sems = sems
        me, peers = _me_and_peers()
        local, remote = [], []
        for t in range(self.n):
            local.append(pltpu.make_async_copy(self._src(ins, t, me), self._dst(ins, outs, t, me), loc_sems.at[t]))
            for r, (peer, pidx) in enumerate(peers):
                remote.append(pltpu.make_async_remote_copy(
                    src_ref=self._src(ins, t, pidx), dst_ref=self._dst(ins, outs, t, me if sending else pidx),
                    send_sem=send_sems.at[t, r], recv_sem=recv_sems.at[t, r], device_id=peer,
                    device_id_type=pl.DeviceIdType.MESH))
        return local, remote

    def _two_level(self, ins, outs, sems):
        send_sems, recv_sems, loc_sems = sems
        x, y, c = lax.axis_index("x"), lax.axis_index("y"), lax.axis_index("c")
        me, sibling = (x, y, c), (x, y, 1 - c)
        chips = [(1 - x, y), (x, 1 - y), (1 - x, 1 - y)]
        dev = lambda p: 4 * p[0] + 2 * p[1] + p[2]

        def copy(t, k, block, to, from_shard):
            dst = self._dst(ins, outs, t, dev(block))
            return pltpu.make_async_remote_copy(
                src_ref=ins[t] if from_shard else dst, dst_ref=dst, send_sem=send_sems.at[t, k], recv_sem=recv_sems.at[t, k],
                device_id=to, device_id_type=pl.DeviceIdType.MESH)

        ts = range(self.n)
        local = [pltpu.make_async_copy(ins[t], self._dst(ins, outs, t, dev(me)), loc_sems.at[t]) for t in ts]
        first = [copy(t, 0, me, sibling, True) for t in ts]
        first += [copy(t, 1 + j, me, (*chip, c), True) for t in ts for j, chip in enumerate(chips)]
        return local, first, copy, chips, me, sibling, c

    def start(self, ins, outs, sems):
        if self.kind == "ag2":
            local, first = self._two_level(ins, outs, sems)[:2]
            for cp in local + first:
                cp.start()
            return
        local, remote = self._copies(ins, outs, sems, True)
        for cp in local + remote:
            cp.start()

    def wait(self, ins, outs, sems):
        if self.kind == "ag2":
            local, first, copy, chips, me, sibling, c = self._two_level(ins, outs, sems)
            passed = []
            for t in range(self.n):
                for j, chip in enumerate(chips):
                    copy(t, 1 + j, (*chip, c), me, False).wait_recv()
                    fwd = copy(t, 4 + j, (*chip, c), sibling, False)
                    fwd.start()
                    passed.append(fwd)
            for t in range(self.n):
                copy(t, 0, sibling, me, False).wait_recv()
                for j, chip in enumerate(chips):
                    copy(t, 4 + j, (*chip, 1 - c), me, False).wait_recv()
            for cp in first + passed:
                cp.wait_send()
            for cp in local:
                cp.wait()
            return
        local, remote = self._copies(ins, outs, sems, False)
        for cp in remote:
            cp.wait_recv()
            cp.wait_send()
        for cp in local:
            cp.wait()


def _call(body, name, grid, in_specs, out_specs, out_shape, scratch=(), comm=None):
    params = pltpu.CompilerParams(vmem_limit_bytes=VMEM_LIMIT_BYTES)
    if comm is None:
        return pl.pallas_call(body, name=name, grid=grid, in_specs=in_specs, out_specs=out_specs, out_shape=out_shape,
                              scratch_shapes=list(scratch), compiler_params=params)
    single = not isinstance(out_shape, (tuple, list))
    c_out_specs = [out_specs] if single else list(out_specs)
    c_out_shape = [out_shape] if single else list(out_shape)
    n_in, n_out, n_scr, n = len(in_specs), len(c_out_shape), len(scratch), comm.n
    anyspec = pl.BlockSpec(memory_space=pl.ANY)

    def wrapped(*refs):
        ins, refs = refs[:n_in], refs[n_in:]
        cins, refs = refs[:n], refs[n:]
        outs, refs = refs[:n_out], refs[n_out:]
        couts, refs = refs[:n], refs[n:]
        scr, sems = refs[:n_scr], refs[n_scr:]
        first = functools.reduce(lambda a, b: a & b, [pl.program_id(d) == 0 for d in range(len(grid))], True)
        last = functools.reduce(lambda a, b: a & b, [pl.program_id(d) == grid[d] - 1 for d in range(len(grid))], True)
        if grid:
            pl.when(first)(lambda: comm.start(cins, couts, sems))
            body(*ins, *outs, *scr)
            pl.when(last)(lambda: comm.wait(cins, couts, sems))
        else:
            comm.start(cins, couts, sems)
            if body is not None:
                body(*ins, *outs, *scr)
            comm.wait(cins, couts, sems)

    call = pl.pallas_call(
        wrapped, name=name, grid=grid, in_specs=list(in_specs) + [anyspec] * n, out_specs=c_out_specs + [anyspec] * n,
        out_shape=c_out_shape + comm.out_shape(), scratch_shapes=list(scratch) + comm.sems(), compiler_params=params)

    def run(*args):
        res = call(*args, *comm.arrays)
        mine = res[:n_out]
        return (mine[0] if single else tuple(mine)), list(res[n_out:])

    return run


def _tile(n, want, align=8):
    if n <= want:
        return n
    for t in range(want // align * align, 0, -align):
        if n % t == 0:
            return t
    return n


def _dot(a, b, dims=NN, precision=None):
    return lax.dot_general(a, b, dims, preferred_element_type=f32, precision=precision)


def _logsig(x):
    return jnp.minimum(x, 0.0) - jnp.log(1.0 + jnp.exp(-jnp.abs(x)))


def _softplus(x):
    return jnp.maximum(x, 0.0) + jnp.log(1.0 + jnp.exp(-jnp.abs(x)))


def _rms(x, g):
    return x * lax.rsqrt(jnp.mean(x * x, axis=-1, keepdims=True) + EPS) * g


def _lane_pick(blk, lane_idx):
    lane = lax.broadcasted_iota(jnp.int32, (1, LANES), 1)
    return jnp.sum(jnp.where(lane == lane_idx, blk, 0.0), axis=1, keepdims=True)


def mm(a, b, mode, out_dtype, name, tm=512, tn=512, res=None, scale=1.0, comm=None):
    if mode == "nn":
        (M, K), (_, N) = a.shape, b.shape
    elif mode == "nt":
        (M, K), (N, _) = a.shape, b.shape
    else:
        (K, M), (_, N) = a.shape, b.shape
    tm, tn = _tile(M, tm, LANES), _tile(N, tn, LANES)
    a_spec = pl.BlockSpec((K, tm), lambda j, i: (0, i)) if mode == "tn" else pl.BlockSpec((tm, K), lambda j, i: (i, 0))
    b_spec = pl.BlockSpec((tn, K), lambda j, i: (j, 0)) if mode == "nt" else pl.BlockSpec((K, tn), lambda j, i: (0, j))
    dims = {"nn": NN, "nt": NT, "tn": TN}[mode]
    o_spec = pl.BlockSpec((tm, tn), lambda j, i: (i, j))

    def body(*refs):
        acc = _dot(refs[0][...].astype(bf16), refs[1][...].astype(bf16), dims)
        if scale != 1.0:
            acc = acc * scale
        if res is not None:
            acc = acc + refs[2][...]
        refs[-1][...] = acc.astype(out_dtype)

    ins, specs = [a, b], [a_spec, b_spec]
    if res is not None:
        ins.append(res)
        specs.append(o_spec)
    return _call(body, name, (N // tn, M // tm), specs, o_spec, SDS((M, N), out_dtype), comm=comm)(*ins)


def _rowspec(tm, w, cb=0):
    return pl.BlockSpec((tm, w), lambda i: (i, cb))


def _bspec(w):
    return pl.BlockSpec((1, w), lambda i: (0, 0))


def rms_fwd(x, g, name):
    S, D = x.shape
    tm = _tile(S, 512)

    def body(x_ref, g_ref, h_ref):
        h_ref[...] = _rms(x_ref[...], g_ref[...]).astype(bf16)

    return _call(body, name, (S // tm,), [_rowspec(tm, D), _bspec(D)], _rowspec(tm, D), SDS((S, D), bf16))(x, g)


def _swiglu(g, u):
    return g * jax.nn.sigmoid(g) * u


def ffn_gu_act(h, wgu, name, tm=1024, tn=768, comm=None):
    S, D = h.shape
    W = wgu.shape[1] // 2
    tm, tn = _tile(S, tm, LANES), _tile(W, tn, LANES)
    nb = W // tn

    def body(h_ref, wg_ref, wu_ref, gu_ref, a_ref):
        hb = h_ref[...]
        g = _dot(hb, wg_ref[...])
        u = _dot(hb, wu_ref[...])
        gu_ref[0] = g.astype(bf16)
        gu_ref[1] = u.astype(bf16)
        a_ref[...] = _swiglu(g, u).astype(bf16)

    return _call(body, name, (nb, S // tm),
                 [pl.BlockSpec((tm, D), lambda j, i: (i, 0)), pl.BlockSpec((D, tn), lambda j, i: (0, j)),
                  pl.BlockSpec((D, tn), lambda j, i: (0, nb + j))],
                 (pl.BlockSpec((2, tm, tn), lambda j, i: (0, i, j)), pl.BlockSpec((tm, tn), lambda j, i: (i, j))),
                 (SDS((2, S, W), bf16), SDS((S, W), bf16)), comm=comm)(h, wgu, wgu)


def ffn_bda_act(dy, wd, gu, scale, name, tm=512, tn=768, comm=None):
    S, D = dy.shape
    W = wd.shape[0]
    tm, tn = _tile(S, tm, LANES), _tile(W, tn, LANES)

    def body(dy_ref, wd_ref, gu_ref, dgu_ref):
        da = _dot(dy_ref[...].astype(bf16), wd_ref[...], NT) * scale
        _, vjp = jax.vjp(_swiglu, gu_ref[0].astype(f32), gu_ref[1].astype(f32))
        dg, du = vjp(da)
        dgu_ref[0] = dg.astype(bf16)
        dgu_ref[1] = du.astype(bf16)

    planes = pl.BlockSpec((2, tm, tn), lambda j, i: (0, i, j))
    return _call(body, name, (W // tn, S // tm),
                 [pl.BlockSpec((tm, D), lambda j, i: (i, 0)), pl.BlockSpec((tn, D), lambda j, i: (j, 0)), planes],
                 planes, SDS((2, S, W), bf16), comm=comm)(dy, wd, gu)


def ffn_bwgu(h, dgu, name, tm=512, tn=768, comm=None):
    S, D = h.shape
    W = dgu.shape[2]
    tm, tn = _tile(D, tm, LANES), _tile(W, tn, LANES)
    nb = W // tn

    def body(h_ref, d_ref, o_ref):
        o_ref[...] = _dot(h_ref[...], d_ref[...], TN).astype(bf16)

    return _call(body, name, (2 * nb, D // tm),
                 [pl.BlockSpec((S, tm), lambda j, i: (0, i)), pl.BlockSpec((None, S, tn), lambda j, i: (j // nb, 0, j % nb))],
                 pl.BlockSpec((tm, tn), lambda j, i: (i, j)), SDS((D, 2 * W), bf16), comm=comm)(h, dgu)


def nt_rms_bwd(pairs, x, g, dres, name, tm=256, comm=None):
    S, D = x.shape
    tm = _tile(S, tm)
    n = len(pairs)

    def body(*refs):
        x_ref, g_ref, dres_ref = refs[2 * n:2 * n + 3]
        dx_ref, dxb_ref, dg_ref = refs[2 * n + 3:]
        dh = sum(_dot(refs[2 * k][...].astype(bf16), refs[2 * k + 1][...], NT) for k in range(n))
        _, vjp = jax.vjp(_rms, x_ref[...], g_ref[...])
        dx, dg = vjp(dh)
        dx = dres_ref[...] + dx
        dx_ref[...] = dx
        dxb_ref[...] = dx.astype(bf16)

        @pl.when(pl.program_id(0) == 0)
        def _():
            dg_ref[...] = jnp.zeros_like(dg_ref)

        dg_ref[...] += dg

    arrays, specs = [], []
    for a, a_spec, b, b_spec in pairs:
        arrays += [a, b]
        specs += [a_spec, b_spec]
    (dx, dxb, dg), got = _carried(_call(body, name, (S // tm,), specs + [_rowspec(tm, D), _bspec(D), _rowspec(tm, D)],
                                        (_rowspec(tm, D), _rowspec(tm, D), _bspec(D)),
                                        (SDS((S, D), f32), SDS((S, D), bf16), SDS((1, D), f32)), comm=comm)(*arrays, x, g, dres),
                                  comm)
    return ((dx, dxb), dg) if comm is None else ((dx, dxb), dg, got)


def ffn_bdh_rms(dgu, wgu, x, g, dres, name, tm=256, comm=None):
    _, S, W = dgu.shape
    D = wgu.shape[0]
    tm = _tile(S, tm)
    pairs = [(dgu, pl.BlockSpec((None, tm, W), functools.partial(lambda p, i: (p, i, 0), p)),
              wgu, pl.BlockSpec((D, W), functools.partial(lambda p, i: (0, p), p))) for p in range(2)]
    return nt_rms_bwd(pairs, x, g, dres, name, tm, comm=comm)


def mm_bdh_rms(d, w, x, g, dres, name, tm=256, comm=None):
    S, K = d.shape
    tm = _tile(S, tm)
    return nt_rms_bwd([(d, pl.BlockSpec((tm, K), lambda i: (i, 0)), w, pl.BlockSpec(w.shape, lambda i: (0, 0)))], x, g, dres, name, tm,
                      comm=comm)


def loss_head(y, t, name):
    S, D = y.shape
    tm = _tile(S, 512)

    def body(y_ref, t_ref, dy_ref, dyb_ref, l_ref):
        e = y_ref[...] - t_ref[...]
        dy_ref[...] = e * (1.0 / D)
        dyb_ref[...] = (e * (1.0 / D)).astype(bf16)

        @pl.when(pl.program_id(0) == 0)
        def _():
            l_ref[...] = jnp.zeros_like(l_ref)

        l_ref[...] += jnp.sum(jnp.sum(e * e, axis=1, keepdims=True), axis=0, keepdims=True) * (0.5 / D)

    dy, dyb, loss = _call(body, name, (S // tm,), [_rowspec(tm, D), _rowspec(tm, D)],
                          (_rowspec(tm, D), _rowspec(tm, D), _bspec(LANES)),
                          (SDS((S, D), f32), SDS((S, D), bf16), SDS((1, LANES), f32)))(y, t)
    return (dy, dyb), loss


def ffn_fwd(x, g, wgu, wd, tag, comm_gu=None, comm_down=None):
    h = rms_fwd(x, g, f"{tag}_rms")
    (gu, a), got_gu = _carried(ffn_gu_act(h, wgu, f"{tag}_gu", comm=comm_gu), comm_gu)
    xo, got_down = _carried(mm(a, wd, "nn", f32, f"{tag}_down", tm=512, tn=512, res=x, scale=0.5, comm=comm_down), comm_down)
    return xo, (h, gu, a), got_gu, got_down


def ffn_bwd(x, g, wgu, wd, saved, dy, tag, comms=None, own_axes=None):
    h, gu, a = saved
    dy32, dy16 = dy
    cm = comms or {}
    dgu, got_bda = _carried(ffn_bda_act(dy16, wd, gu, 0.5, f"{tag}_bda", comm=cm.get("bda")), cm.get("bda"))
    dwd, got_bwd = _carried(mm(a, dy16, "tn", bf16, f"{tag}_bwd", tm=512, tn=512, scale=0.5, comm=cm.get("bwd")), cm.get("bwd"))
    c_wd = Comm("rs", [dwd], [own_axes[1]]) if own_axes else None
    dwgu, got_wd = _carried(ffn_bwgu(h, dgu, f"{tag}_bwgu", comm=c_wd), c_wd)
    c_wgu = Comm("rs", [dwgu], [own_axes[0]]) if own_axes else None
    res = ffn_bdh_rms(dgu, wgu, x, g, dy32, f"{tag}_bdh", comm=c_wgu)
    dx, dg = res[:2]
    if comms is None and own_axes is None:
        return dx, dg, dwgu, dwd
    return dx, dg, dwgu, dwd, dict(bda=got_bda, bwd=got_bwd, wd=got_wd, wgu=res[2] if c_wgu is not None else [])


def _split_bf16(x):
    hi = x.astype(bf16)
    lo = (x - hi.astype(f32)).astype(bf16)
    return hi, lo


SB_TQ, SB_TK, SB_NSUB = 512, 256, 4
FOX_TK = 256


def _sb_geometry(S, tk=SB_TK):
    tq = min(SB_TQ, S)
    tk = min(tk, tq)
    return tq, tk, tq // SB_NSUB, tq // tk


def _sb_consts(tk):
    r = lax.broadcasted_iota(jnp.int32, (tk, tk), 0)
    c = lax.broadcasted_iota(jnp.int32, (tk, tk), 1)
    return (r > c).astype(bf16), (r < c).astype(bf16)


def _sb_scores(qs, k, kb, tk, rows, masked):
    scale = HEAD_DIM ** -0.5
    z = [_dot(q, k, NT) * scale for q in qs]
    ls = [_logsig(z_) for z_ in z]
    lm = [l - z_ for l, z_ in zip(ls, z)]
    ok = None
    if masked:
        col = kb * tk + lax.broadcasted_iota(jnp.int32, z[0].shape, 1)
        ok = [col < row for row in rows]
        lm = [jnp.where(m, x, 0.0) for m, x in zip(ok, lm)]
    return ls, lm, ok


def _sb_weights(ls, lm, ok, tail, after):
    suf = [_dot(x.astype(bf16), after) for x in lm]
    a = [jnp.exp(l + s_ + t_) for l, s_, t_ in zip(ls, suf, tail)]
    if ok is not None:
        a = [jnp.where(m, x, 0.0) for m, x in zip(ok, a)]
    return a


def sb_fwd(qkv, nh, name, comm=None):
    S = qkv.shape[0]
    tq, tk, rs, nd = _sb_geometry(S)
    nsub = tq // rs

    def body(q_ref, k_ref, v_ref, o_ref, tails_ref):
        i = pl.program_id(1)
        after, _ = _sb_consts(tk)
        qs = [q_ref[pl.ds(s * rs, rs), :] for s in range(nsub)]
        rows = [i * tq + s * rs + lax.broadcasted_iota(jnp.int32, (rs, tk), 0) for s in range(nsub)]

        def tile(kb, carry, masked):
            tail, acc = carry
            off = pl.multiple_of(kb * tk, tk)
            k = k_ref[pl.ds(off, tk), :]
            v = v_ref[pl.ds(off, tk), :]
            ls, lm, ok = _sb_scores(qs, k, kb, tk, rows, masked)
            a = _sb_weights(ls, lm, ok, tail, after)
            tails_ref[pl.ds(kb, 1), :] += jnp.concatenate([t_.T for t_ in tail], axis=1)
            acc = [ac + _dot(a_.astype(bf16), v) for ac, a_ in zip(acc, a)]
            tail = [t_ + jnp.sum(x, axis=1, keepdims=True) for t_, x in zip(tail, lm)]
            return tail, acc

        tails_ref[...] = jnp.zeros_like(tails_ref)
        carry = ([jnp.zeros((rs, 1), f32)] * nsub, [jnp.zeros((rs, HEAD_DIM), f32)] * nsub)
        for d in reversed(range(nd)):
            carry = tile(i * nd + d, carry, True)
        carry = lax.fori_loop(0, i * nd, lambda t, c: tile(i * nd - 1 - t, c, False), carry)
        for s in range(nsub):
            o_ref[pl.ds(s * rs, rs), :] = carry[1][s].astype(o_ref.dtype)

    blk = lambda cb0: pl.BlockSpec((tq, HEAD_DIM), lambda h, i: (i, cb0 + h))
    full = lambda cb0: pl.BlockSpec((S, HEAD_DIM), lambda h, i: (0, cb0 + h))
    tails = pl.BlockSpec((S // tk, tq), lambda h, i: (h, i))
    return _call(body, name, (nh, S // tq), [blk(0), full(nh), full(2 * nh)], (blk(0), tails),
                 (SDS((S, nh * HEAD_DIM), bf16), SDS((nh * (S // tk), S), f32)), comm=comm)(qkv, qkv, qkv)


def sb_bwd(qkv, tails, do, nh, name, comm=None):
    S = qkv.shape[0]
    tq, tk, rs, nd = _sb_geometry(S)
    nsub = tq // rs
    scale = HEAD_DIM ** -0.5

    def body(q_ref, k_ref, v_ref, tails_ref, do_ref, dq_ref, dk_ref, dv_ref):
        i = pl.program_id(1)

        @pl.when(i == 0)
        def _():
            dk_ref[...] = jnp.zeros_like(dk_ref)
            dv_ref[...] = jnp.zeros_like(dv_ref)

        after, before = _sb_consts(tk)
        qs = [q_ref[pl.ds(s * rs, rs), :] for s in range(nsub)]
        dos = [do_ref[pl.ds(s * rs, rs), :].astype(bf16) for s in range(nsub)]
        rows = [i * tq + s * rs + lax.broadcasted_iota(jnp.int32, (rs, tk), 0) for s in range(nsub)]

        def sweep2(kb, carry, masked):
            pe, dq = carry
            off = pl.multiple_of(kb * tk, tk)
            k = k_ref[pl.ds(off, tk), :]
            v = v_ref[pl.ds(off, tk), :]
            ls, lm, ok = _sb_scores(qs, k, kb, tk, rows, masked)
            tail_row = tails_ref[pl.ds(kb, 1), :]
            tail = [tail_row[:, s * rs:(s + 1) * rs].T for s in range(nsub)]
            a = _sb_weights(ls, lm, ok, tail, after)
            e = [_dot(d_, v, NT) * a_ for d_, a_ in zip(dos, a)]
            cum_e = [p_ + _dot(e_.astype(bf16), before) for p_, e_ in zip(pe, e)]
            sig = [jnp.exp(l) for l in ls]
            dz = [e_ * (1.0 - sg) - c_ * sg for e_, sg, c_ in zip(e, sig, cum_e)]
            if ok is not None:
                dz = [jnp.where(m, x, 0.0) for m, x in zip(ok, dz)]
            dzb = [(x * scale).astype(bf16) for x in dz]
            dk_ref[pl.ds(off, tk), :] += sum(_dot(x, q, TN) for x, q in zip(dzb, qs))
            dv_ref[pl.ds(off, tk), :] += sum(_dot(a_.astype(bf16), d_, TN) for a_, d_ in zip(a, dos))
            pe = [p_ + jnp.sum(e_, axis=1, keepdims=True) for p_, e_ in zip(pe, e)]
            dq = [g + _dot(x, k) for g, x in zip(dq, dzb)]
            return pe, dq

        carry = ([jnp.zeros((rs, 1), f32)] * nsub, [jnp.zeros((rs, HEAD_DIM), f32)] * nsub)
        carry = lax.fori_loop(0, i * nd, lambda kb, c: sweep2(kb, c, False), carry)
        for d in range(nd):
            carry = sweep2(i * nd + d, carry, True)
        for s in range(nsub):
            dq_ref[pl.ds(s * rs, rs), :] = carry[1][s]

    blk = lambda cb0: pl.BlockSpec((tq, HEAD_DIM), lambda h, i: (i, cb0 + h))
    full = lambda cb0: pl.BlockSpec((S, HEAD_DIM), lambda h, i: (0, cb0 + h))
    sh = SDS((S, nh * HEAD_DIM), f32)
    tails_spec = pl.BlockSpec((S // tk, tq), lambda h, i: (h, i))
    return _call(body, name, (nh, S // tq), [blk(0), full(nh), full(2 * nh), tails_spec, blk(0)], (blk(0), full(0), full(0)),
                 (sh, sh, sh), comm=comm)(qkv, qkv, qkv, tails, do)


def fox_fwd(fq, fk, fv, c, cT, nh, lane0, name, comm=None):
    S = fq.shape[0]
    tq, tk, rs, nd = _sb_geometry(S, FOX_TK)
    nsub = tq // rs
    scale = HEAD_DIM ** -0.5

    def body(q_ref, k_ref, v_ref, c_ref, ct_ref, o_ref, lse_ref):
        h = pl.program_id(0)
        i = pl.program_id(1)
        subs = range(nsub)
        qs = [q_ref[pl.ds(s * rs, rs), :] for s in subs]
        ci = [_lane_pick(c_ref[pl.ds(s * rs, rs), :], lane0 + h) for s in subs]
        rows = [i * tq + s * rs + lax.broadcasted_iota(jnp.int32, (rs, tk), 0) for s in subs]

        def tile(kb, carry, masked):
            m, l, acc = carry
            off = pl.multiple_of(kb * tk, tk)
            k = k_ref[pl.ds(off, tk), :]
            v = v_ref[pl.ds(off, tk), :]
            cj = ct_ref[pl.ds(lane0 % 8 + h, 1), pl.ds(off, tk)]
            sc = [_dot(q, k, NT) * scale + (c_ - cj) for q, c_ in zip(qs, ci)]
            if masked:
                col = kb * tk + lax.broadcasted_iota(jnp.int32, (rs, tk), 1)
                sc = [jnp.where(col <= row, x, -jnp.inf) for x, row in zip(sc, rows)]
            m2 = [jnp.maximum(m_, jnp.max(x, axis=1, keepdims=True)) for m_, x in zip(m, sc)]
            p = [jnp.exp(x - m_) for x, m_ in zip(sc, m2)]
            al = [jnp.exp(a - b) for a, b in zip(m, m2)]
            l = [a * l_ + jnp.sum(p_, axis=1, keepdims=True) for a, l_, p_ in zip(al, l, p)]
            acc = [a * ac + _dot(p_.astype(bf16), v) for a, ac, p_ in zip(al, acc, p)]
            return m2, l, acc

        carry = ([jnp.full((rs, 1), -jnp.inf, f32)] * nsub, [jnp.zeros((rs, 1), f32)] * nsub,
                 [jnp.zeros((rs, HEAD_DIM), f32)] * nsub)
        for d in range(nd):
            carry = tile(i * nd + d, carry, True)
        m, l, acc = lax.fori_loop(0, i * nd, lambda kb, cr: tile(kb, cr, False), carry)
        for s in subs:
            o_ref[pl.ds(s * rs, rs), :] = acc[s] / l[s]
            lse_ref[pl.ds(s * rs, rs), :] = jnp.broadcast_to(m[s] + jnp.log(l[s]), (rs, HEAD_DIM))

    blk = pl.BlockSpec((tq, HEAD_DIM), lambda h, i: (i, h))
    full = pl.BlockSpec((S, HEAD_DIM), lambda h, i: (0, h))
    cspec = pl.BlockSpec((tq, LANES), lambda h, i: (i, 0))
    ctspec = pl.BlockSpec((8, S), lambda h, i: (lane0 // 8, 0))
    sh = SDS((S, nh * HEAD_DIM), f32)
    return _call(body, name, (nh, S // tq), [blk, full, full, cspec, ctspec], (blk, blk), (sh, sh), comm=comm)(fq, fk, fv, c, cT)


def fox_bwd(fq, fk, fv, c, cT, o, lse, do, nh, lane0, name, comm=None):
    S = fq.shape[0]
    tq, tk, rs, nd = _sb_geometry(S, FOX_TK)
    nsub = tq // rs
    scale = HEAD_DIM ** -0.5

    def body(q_ref, k_ref, v_ref, c_ref, ct_ref, o_ref, lse_ref, do_ref, dq_ref, dk_ref, dv_ref, dct_ref):
        h = pl.program_id(0)
        i = pl.program_id(1)

        @pl.when(i == 0)
        def _():
            dk_ref[...] = jnp.zeros_like(dk_ref)
            dv_ref[...] = jnp.zeros_like(dv_ref)

        @pl.when((i == 0) & (h == 0))
        def _():
            dct_ref[...] = jnp.zeros_like(dct_ref)

        subs = range(nsub)
        qs = [q_ref[pl.ds(s * rs, rs), :] for s in subs]
        dof = [do_ref[pl.ds(s * rs, rs), :] for s in subs]
        dos = [x.astype(bf16) for x in dof]
        ci = [_lane_pick(c_ref[pl.ds(s * rs, rs), :], lane0 + h) for s in subs]
        lses = [lse_ref[pl.ds(s * rs, rs), :][:, :1] for s in subs]
        dl = [jnp.sum(x * o_ref[pl.ds(s * rs, rs), :], axis=1, keepdims=True) for s, x in zip(subs, dof)]
        rows = [i * tq + s * rs + lax.broadcasted_iota(jnp.int32, (rs, tk), 0) for s in subs]

        def tile(kb, carry, masked):
            dq, rsum = carry
            off = pl.multiple_of(kb * tk, tk)
            k = k_ref[pl.ds(off, tk), :]
            v = v_ref[pl.ds(off, tk), :]
            cj = ct_ref[pl.ds(lane0 % 8 + h, 1), pl.ds(off, tk)]
            p = [jnp.exp(_dot(q, k, NT) * scale + (c_ - cj) - ls) for q, c_, ls in zip(qs, ci, lses)]
            if masked:
                col = kb * tk + lax.broadcasted_iota(jnp.int32, (rs, tk), 1)
                p = [jnp.where(col <= row, x, 0.0) for x, row in zip(p, rows)]
            ds = [p_ * (_dot(d_, v, NT) - dl_) for p_, d_, dl_ in zip(p, dos, dl)]
            dsb = [(x * scale).astype(bf16) for x in ds]
            dk_ref[pl.ds(off, tk), :] += sum(_dot(x, q, TN) for x, q in zip(dsb, qs))
            dv_ref[pl.ds(off, tk), :] += sum(_dot(p_.astype(bf16), d_, TN) for p_, d_ in zip(p, dos))
            dct_ref[pl.ds(lane0 + h, 1), pl.ds(off, tk)] -= sum(jnp.sum(x, axis=0, keepdims=True) for x in ds)
            return ([g + _dot(x, k) for g, x in zip(dq, dsb)],
                    [r_ + jnp.sum(x, axis=1, keepdims=True) for r_, x in zip(rsum, ds)])

        carry = ([jnp.zeros((rs, HEAD_DIM), f32)] * nsub, [jnp.zeros((rs, 1), f32)] * nsub)
        carry = lax.fori_loop(0, i * nd, lambda kb, cr: tile(kb, cr, False), carry)
        for d in range(nd):
            carry = tile(i * nd + d, carry, True)
        dq, rsum = carry
        for s in subs:
            dq_ref[pl.ds(s * rs, rs), :] = dq[s]
        dct_ref[pl.ds(lane0 + h, 1), pl.ds(pl.multiple_of(i * tq, tq), tq)] += jnp.concatenate([r_.T for r_ in rsum], axis=1)

    blk = pl.BlockSpec((tq, HEAD_DIM), lambda h, i: (i, h))
    full = pl.BlockSpec((S, HEAD_DIM), lambda h, i: (0, h))
    cspec = pl.BlockSpec((tq, LANES), lambda h, i: (i, 0))
    ctspec = pl.BlockSpec((8, S), lambda h, i: (lane0 // 8, 0))
    dctspec = pl.BlockSpec((LANES, S), lambda h, i: (0, 0))
    sh = SDS((S, nh * HEAD_DIM), f32)
    return _call(body, name, (nh, S // tq), [blk, full, full, cspec, ctspec, blk, blk, blk], (blk, full, full, dctspec),
                 (sh, sh, sh, SDS((LANES, S), f32)), comm=comm)(fq, fk, fv, c, cT, o, lse, do)


def gates_fwd(proj, small_cb, fbias_row, name, tm=256):
    S = proj.shape[0]
    tm = _tile(S, tm)

    def body(sm_ref, fb_ref, c_ref, ct_ref, carry_ref):
        @pl.when(pl.program_id(0) == 0)
        def _():
            carry_ref[...] = jnp.zeros_like(carry_ref)

        lf = _logsig(sm_ref[...] + fb_ref[...])
        r = lax.broadcasted_iota(jnp.int32, (tm, tm), 0)
        cc = lax.broadcasted_iota(jnp.int32, (tm, tm), 1)
        c = _dot((r >= cc).astype(f32), lf, NN, HI) + carry_ref[...]
        carry_ref[...] += jnp.sum(lf, axis=0, keepdims=True)
        c_ref[...] = c
        ct_ref[...] = c.T

    return _call(body, name, (S // tm,), [_rowspec(tm, LANES, small_cb), _bspec(LANES)],
                 (_rowspec(tm, LANES), pl.BlockSpec((LANES, tm), lambda i: (0, i))),
                 (SDS((S, LANES), f32), SDS((LANES, S), f32)), scratch=[pltpu.VMEM((1, LANES), f32)])(proj, fbias_row)


def gates_bwd(proj, small_cb, fbias_row, dcT, dsm_dn, lane0, nh, name, tm=256):
    S = proj.shape[0]
    tm = _tile(S, tm)
    nt = S // tm

    def body(sm_ref, fb_ref, dct_ref, dsm_ref, o_ref, dfb_ref, carry_ref):
        @pl.when(pl.program_id(0) == 0)
        def _():
            carry_ref[...] = jnp.zeros_like(carry_ref)
            dfb_ref[...] = jnp.zeros_like(dfb_ref)

        dc = dct_ref[...].T
        r = lax.broadcasted_iota(jnp.int32, (tm, tm), 0)
        cc = lax.broadcasted_iota(jnp.int32, (tm, tm), 1)
        dlf = _dot((r <= cc).astype(f32), dc, NN, HI) + carry_ref[...]
        carry_ref[...] += jnp.sum(dc, axis=0, keepdims=True)
        lane = lax.broadcasted_iota(jnp.int32, (1, LANES), 1)
        dpre = jnp.where((lane >= lane0) & (lane < lane0 + nh), dlf * jax.nn.sigmoid(-(sm_ref[...] + fb_ref[...])), 0.0)
        o_ref[...] = dsm_ref[...] + dpre
        dfb_ref[...] += jnp.sum(dpre, axis=0, keepdims=True)

    rev = lambda i: nt - 1 - i
    return _call(body, name, (nt,),
                 [pl.BlockSpec((tm, LANES), lambda i: (rev(i), small_cb)), _bspec(LANES),
                  pl.BlockSpec((LANES, tm), lambda i: (0, rev(i))), pl.BlockSpec((tm, LANES), lambda i: (rev(i), 0))],
                 (pl.BlockSpec((tm, LANES), lambda i: (rev(i), 0)), _bspec(LANES)),
                 (SDS((S, LANES), f32), SDS((1, LANES), f32)), scratch=[pltpu.VMEM((1, LANES), f32)])(proj, fbias_row, dcT, dsm_dn)


PAD_ROWS = 8


def conv_fwd(proj, w, width, name):
    S = proj.shape[0]
    cw = 256 if width % 256 == 0 else 128

    def body(x_ref, w_ref, y_ref, pad_ref):
        pad_ref[pl.ds(0, PAD_ROWS), :] = jnp.zeros((PAD_ROWS, cw), f32)
        pad_ref[pl.ds(PAD_ROWS, S), :] = x_ref[...]
        y = jnp.zeros((S, cw), f32)
        for i in range(CONV_WIDTH):
            y = y + pad_ref[pl.ds(PAD_ROWS - (CONV_WIDTH - 1) + i, S), :] * w_ref[pl.ds(i, 1), :]
        y_ref[...] = y * jax.nn.sigmoid(y)

    spec = pl.BlockSpec((S, cw), lambda j: (0, j))
    return _call(body, name, (width // cw,), [spec, pl.BlockSpec((CONV_WIDTH, cw), lambda j: (0, j))], spec,
                 SDS((S, width), f32), scratch=[pltpu.VMEM((S + PAD_ROWS, cw), f32)])(proj, w)


def conv_bwd(proj, w, dy, name):
    S, width = dy.shape
    cw = 256 if width % 256 == 0 else 128

    def body(x_ref, w_ref, dy_ref, dx_ref, dw_ref, xpad_ref, dpad_ref):
        xpad_ref[pl.ds(0, PAD_ROWS), :] = jnp.zeros((PAD_ROWS, cw), f32)
        xpad_ref[pl.ds(PAD_ROWS, S), :] = x_ref[...]
        y = jnp.zeros((S, cw), f32)
        for i in range(CONV_WIDTH):
            y = y + xpad_ref[pl.ds(PAD_ROWS - (CONV_WIDTH - 1) + i, S), :] * w_ref[pl.ds(i, 1), :]
        sg = jax.nn.sigmoid(y)
        dpre = dy_ref[...] * (sg * (1.0 + y * (1.0 - sg)))
        dpad_ref[pl.ds(S, PAD_ROWS), :] = jnp.zeros((PAD_ROWS, cw), f32)
        dpad_ref[pl.ds(0, S), :] = dpre
        dx = jnp.zeros((S, cw), f32)
        for i in range(CONV_WIDTH):
            dx = dx + dpad_ref[pl.ds(CONV_WIDTH - 1 - i, S), :] * w_ref[pl.ds(i, 1), :]
            dw_ref[pl.ds(i, 1), :] = jnp.sum(dpre * xpad_ref[pl.ds(PAD_ROWS - (CONV_WIDTH - 1) + i, S), :], axis=0, keepdims=True)
        dx_ref[...] = dx

    spec = pl.BlockSpec((S, cw), lambda j: (0, j))
    wspec = pl.BlockSpec((CONV_WIDTH, cw), lambda j: (0, j))
    return _call(body, name, (width // cw,), [spec, wspec, spec], (spec, wspec),
                 (SDS((S, width), f32), SDS((CONV_WIDTH, width), f32)),
                 scratch=[pltpu.VMEM((S + PAD_ROWS, cw), f32), pltpu.VMEM((S + PAD_ROWS, cw), f32)])(proj, w, dy)


def _pdot_raw(a, b, dims, passes):
    if passes == 1:
        return _dot(a.astype(bf16), b.astype(bf16), dims)
    ah, al = _split_bf16(a)
    bh, bl = _split_bf16(b)
    return _dot(ah, bh, dims) + (_dot(ah, bl, dims) + _dot(al, bh, dims))


def _make_pdot(passes):
    @functools.partial(jax.custom_vjp, nondiff_argnums=(2,))
    def pdot(a, b, mode):
        return _pdot_raw(a, b, {"nn": NN, "nt": NT, "tn": TN}[mode], passes)

    def fwd(a, b, mode):
        return pdot(a, b, mode), (a, b)

    def bwd(mode, res, g):
        a, b = res
        if mode == "nn":
            return _pdot_raw(g, b, NT, passes), _pdot_raw(a, g, TN, passes)
        if mode == "nt":
            return _pdot_raw(g, b, NN, passes), _pdot_raw(g, a, TN, passes)
        return _pdot_raw(b, g, NT, passes), _pdot_raw(a, g, NN, passes)

    pdot.defvjp(fwd, bwd)
    return pdot


_dot1 = _make_pdot(1)
_dot3 = _make_pdot(3)


def _each(f, *lists):
    return [f(*xs) for xs in zip(*lists)]


def _dn_chunk(ndn, cq, ck, cv, small, alog, dtb, s0):
    C = cq[0].shape[0]
    hs = list(range(ndn))
    b = [_lane_pick(small, h) for h in hs]
    d = [_lane_pick(small, ndn + h) for h in hs]
    al = [_lane_pick(alog, h) for h in hs]
    dt = [_lane_pick(dtb, h) for h in hs]
    q = _each(lambda x: x * lax.rsqrt(jnp.sum(x * x, axis=1, keepdims=True) + EPS) * (HEAD_DIM ** -0.5), cq)
    k = _each(lambda x: x * lax.rsqrt(jnp.sum(x * x, axis=1, keepdims=True) + EPS), ck)
    beta = _each(jax.nn.sigmoid, b)
    g = _each(lambda al_, d_, dt_: -jnp.exp(al_) * _softplus(d_ + dt_), al, d, dt)
    r = lax.broadcasted_iota(jnp.int32, (C, C), 0)
    c = lax.broadcasted_iota(jnp.int32, (C, C), 1)
    incl, strict = r >= c, r > c
    inclf = incl.astype(f32)
    gc = _each(lambda g_: _dot3(inclf, g_, "nn"), g)
    decay = _each(lambda gc_: jnp.where(incl, jnp.exp(jnp.where(incl, gc_ - gc_.T, 0.0)), 0.0), gc)
    kb = _each(lambda k_, b_: k_ * b_, k, beta)
    a = _each(lambda kb_, k_, dec: jnp.where(strict, _dot3(kb_, k_, "nt") * dec, 0.0), kb, k, decay)
    eye = (r == c).astype(f32)
    t = _each(lambda a_: eye - a_, a)
    p = a
    for _ in range(int(math.log2(C)) - 1):
        p = _each(lambda p_: _dot3(p_, p_, "nn"), p)
        t = _each(lambda t_, p_: t_ + _dot3(t_, p_, "nn"), t, p)
    eg = _each(jnp.exp, gc)
    u = _each(lambda t_, v_, b_: _dot3(t_, v_ * b_, "nn"), t, cv, beta)
    w = _each(lambda t_, kb_, eg_: _dot3(t_, kb_ * eg_, "nn"), t, kb, eg)
    attn = _each(lambda q_, k_, dec: _dot1(q_, k_, "nt") * dec, q, k, decay)
    g_last = _each(lambda g_: jnp.sum(g_, axis=0, keepdims=True), g)
    v_new = _each(lambda u_, w_, s_: u_ - _dot1(w_, s_, "nn"), u, w, s0)
    o = _each(lambda q_, eg_, s_, at, vn: _dot1(q_ * eg_, s_, "nn") + _dot1(at, vn, "nn"), q, eg, s0, attn, v_new)
    s1 = _each(lambda s_, gl, k_, gc_, vn: s_ * jnp.exp(gl) + _dot1(k_ * jnp.exp(gl - gc_), vn, "tn"), s0, g_last, k, gc, v_new)
    return o, s1


def dn_fwd(cqkv, proj, small_cb, alog_row, dt_row, ndn, name, comm=None):
    S = cqkv.shape[0]
    C = DN_CHUNK
    nc = S // C
    half = ndn * HEAD_DIM

    def body(q_ref, k_ref, v_ref, sm_ref, al_ref, dt_ref, o_ref, ss_ref, s_ref):
        @pl.when(pl.program_id(0) == 0)
        def _():
            s_ref[...] = jnp.zeros_like(s_ref)

        sls = [slice(h * HEAD_DIM, (h + 1) * HEAD_DIM) for h in range(ndn)]
        s0 = [s_ref[h] for h in range(ndn)]
        for h in range(ndn):
            ss_ref[h, 0] = s0[h]
        o, s1 = _dn_chunk(ndn, [q_ref[:, sl] for sl in sls], [k_ref[:, sl] for sl in sls], [v_ref[:, sl] for sl in sls],
                          sm_ref[...], al_ref[...], dt_ref[...], s0)
        for h in range(ndn):
            o_ref[:, sls[h]] = o[h]
            s_ref[h] = s1[h]

    blk = lambda cb: pl.BlockSpec((C, half), lambda n: (n, cb))
    row = pl.BlockSpec((1, LANES), lambda n: (0, 0))
    return _call(body, name, (nc,),
                 [blk(0), blk(1), blk(2), pl.BlockSpec((C, LANES), lambda n: (n, small_cb)), row, row],
                 (blk(0), pl.BlockSpec((ndn, 1, HEAD_DIM, HEAD_DIM), lambda n: (0, n, 0, 0))),
                 (SDS((S, half), f32), SDS((ndn, nc, HEAD_DIM, HEAD_DIM), f32)),
                 scratch=[pltpu.VMEM((ndn, HEAD_DIM, HEAD_DIM), f32)], comm=comm)(cqkv, cqkv, cqkv, proj, alog_row, dt_row)


def dn_bwd(cqkv, proj, small_cb, alog_row, dt_row, ssave, do, ndn, name, comm=None):
    S = cqkv.shape[0]
    C = DN_CHUNK
    nc = S // C
    half = ndn * HEAD_DIM

    def body(q_ref, k_ref, v_ref, sm_ref, al_ref, dt_ref, ss_ref, do_ref, dqkv_ref, dsm_ref, dal_ref, ddt_ref, ds_ref):
        @pl.when(pl.program_id(0) == 0)
        def _():
            ds_ref[...] = jnp.zeros_like(ds_ref)
            dal_ref[...] = jnp.zeros_like(dal_ref)
            ddt_ref[...] = jnp.zeros_like(ddt_ref)

        sls = [slice(h * HEAD_DIM, (h + 1) * HEAD_DIM) for h in range(ndn)]
        _, vjp = jax.vjp(functools.partial(_dn_chunk, ndn), [q_ref[:, sl] for sl in sls], [k_ref[:, sl] for sl in sls],
                         [v_ref[:, sl] for sl in sls], sm_ref[...], al_ref[...], dt_ref[...], [ss_ref[h, 0] for h in range(ndn)])
        dq, dk, dv, dsm, dal, ddt, ds0 = vjp(([do_ref[:, sl] for sl in sls], [ds_ref[h] for h in range(ndn)]))
        for h in range(ndn):
            dqkv_ref[:, sls[h]] = dq[h]
            dqkv_ref[:, half + h * HEAD_DIM:half + (h + 1) * HEAD_DIM] = dk[h]
            dqkv_ref[:, 2 * half + h * HEAD_DIM:2 * half + (h + 1) * HEAD_DIM] = dv[h]
            ds_ref[h] = ds0[h]
        dsm_ref[...] = dsm
        dal_ref[...] += dal
        ddt_ref[...] += ddt

    rev = lambda n: nc - 1 - n
    blk = lambda cb: pl.BlockSpec((C, half), lambda n: (rev(n), cb))
    row = pl.BlockSpec((1, LANES), lambda n: (0, 0))
    return _call(body, name, (nc,),
                 [blk(0), blk(1), blk(2), pl.BlockSpec((C, LANES), lambda n: (rev(n), small_cb)), row, row,
                  pl.BlockSpec((ndn, 1, HEAD_DIM, HEAD_DIM), lambda n: (0, rev(n), 0, 0)), blk(0)],
                 (pl.BlockSpec((C, 3 * half), lambda n: (rev(n), 0)), pl.BlockSpec((C, LANES), lambda n: (rev(n), 0)), row, row),
                 (SDS((S, 3 * half), f32), SDS((S, LANES), f32), SDS((1, LANES), f32), SDS((1, LANES), f32)),
                 scratch=[pltpu.VMEM((ndn, HEAD_DIM, HEAD_DIM), f32)], comm=comm)(cqkv, cqkv, cqkv, proj, alog_row, dt_row, ssave, do)


def even_pre(proj, gq, gk, dfx, cb0, name):
    S = proj.shape[0]
    tm = _tile(S, 256)
    nh = dfx // HEAD_DIM

    def body(q_ref, k_ref, v_ref, gq_ref, gk_ref, fq_ref, fk_ref, fv_ref):
        for h in range(nh):
            sl = slice(h * HEAD_DIM, (h + 1) * HEAD_DIM)
            fq_ref[:, sl] = _rms(q_ref[:, sl], gq_ref[...]).astype(bf16)
            fk_ref[:, sl] = _rms(k_ref[:, sl], gk_ref[...]).astype(bf16)
        fv_ref[...] = v_ref[...].astype(bf16)

    sh = SDS((S, dfx), bf16)
    o = _rowspec(tm, dfx)
    return _call(body, name, (S // tm,),
                 [_rowspec(tm, dfx, cb0), _rowspec(tm, dfx, cb0 + 1), _rowspec(tm, dfx, cb0 + 2), _bspec(HEAD_DIM), _bspec(HEAD_DIM)],
                 (o, o, o), (sh, sh, sh))(proj, proj, proj, gq, gk)


def even_pre_bwd(proj, gq, gk, dfq, dfk, dfx, cb0, name):
    S = proj.shape[0]
    tm = _tile(S, 256)
    nh = dfx // HEAD_DIM

    def body(q_ref, k_ref, gq_ref, gk_ref, dfq_ref, dfk_ref, dq_ref, dk_ref, dgq_ref, dgk_ref):
        @pl.when(pl.program_id(0) == 0)
        def _():
            dgq_ref[...] = jnp.zeros_like(dgq_ref)
            dgk_ref[...] = jnp.zeros_like(dgk_ref)

        for h in range(nh):
            sl = slice(h * HEAD_DIM, (h + 1) * HEAD_DIM)
            _, vq = jax.vjp(_rms, q_ref[:, sl], gq_ref[...])
            dq, dgq = vq(dfq_ref[:, sl])
            _, vk = jax.vjp(_rms, k_ref[:, sl], gk_ref[...])
            dk, dgk = vk(dfk_ref[:, sl])
            dq_ref[:, sl] = dq
            dk_ref[:, sl] = dk
            dgq_ref[...] += dgq
            dgk_ref[...] += dgk

    sh = SDS((S, dfx), f32)
    o = _rowspec(tm, dfx)
    g = SDS((1, HEAD_DIM), f32)
    return _call(body, name, (S // tm,),
                 [_rowspec(tm, dfx, cb0), _rowspec(tm, dfx, cb0 + 1), _bspec(HEAD_DIM), _bspec(HEAD_DIM), o, o],
                 (o, o, _bspec(HEAD_DIM), _bspec(HEAD_DIM)), (sh, sh, g, g))(proj, proj, gq, gk, dfq, dfk)


def _dn_out(o, gate, gn):
    return _rms(o, gn) * (gate * jax.nn.sigmoid(gate))


def _fox_out(o, gate):
    return o * jax.nn.sigmoid(gate)


def even_post(o_dn, o_fox, proj, gn, half, cb_dn_gate, cb_fox_gate, name):
    S = proj.shape[0]
    tm = _tile(S, 256)
    nh = half // HEAD_DIM

    def body(od_ref, of_ref, gd_ref, gf_ref, gn_ref, o_ref):
        for h in range(nh):
            sl = slice(h * HEAD_DIM, (h + 1) * HEAD_DIM)
            o_ref[:, sl] = _dn_out(od_ref[:, sl], gd_ref[:, sl], gn_ref[...]).astype(bf16)
        o_ref[:, half:] = _fox_out(of_ref[...], gf_ref[...]).astype(bf16)

    return _call(body, name, (S // tm,),
                 [_rowspec(tm, half), _rowspec(tm, half), _rowspec(tm, half, cb_dn_gate), _rowspec(tm, half, cb_fox_gate), _bspec(HEAD_DIM)],
                 _rowspec(tm, 2 * half), SDS((S, 2 * half), bf16))(o_dn, o_fox, proj, proj, gn)


def even_post_bwd(o_dn, o_fox, proj, gn, dom, half, cb_dn_gate, cb_fox_gate, name):
    S = proj.shape[0]
    tm = _tile(S, 256)
    nh = half // HEAD_DIM

    def body(od_ref, of_ref, gd_ref, gf_ref, gn_ref, dod_ref, dof_ref, dd_ref, df_ref, dgd_ref, dgf_ref, dgn_ref):
        @pl.when(pl.program_id(0) == 0)
        def _():
            dgn_ref[...] = jnp.zeros_like(dgn_ref)

        for h in range(nh):
            sl = slice(h * HEAD_DIM, (h + 1) * HEAD_DIM)
            _, vjp = jax.vjp(_dn_out, od_ref[:, sl], gd_ref[:, sl], gn_ref[...])
            d_o, d_gate, d_gn = vjp(dod_ref[:, sl])
            dd_ref[:, sl] = d_o
            dgd_ref[:, sl] = d_gate
            dgn_ref[...] += d_gn
        _, vjp = jax.vjp(_fox_out, of_ref[...], gf_ref[...])
        d_o, d_gate = vjp(dof_ref[...])
        df_ref[...] = d_o
        dgf_ref[...] = d_gate

    sh = SDS((S, half), f32)
    o = _rowspec(tm, half)
    return _call(body, name, (S // tm,),
                 [o, o, _rowspec(tm, half, cb_dn_gate), _rowspec(tm, half, cb_fox_gate), _bspec(HEAD_DIM),
                  _rowspec(tm, half, 0), _rowspec(tm, half, 1)],
                 (o, o, o, o, _bspec(HEAD_DIM)), (sh, sh, sh, sh, SDS((1, HEAD_DIM), f32)))(o_dn, o_fox, proj, proj, gn, dom, dom)


def _pad_row(v, lane0=0):
    return jnp.pad(v, (lane0, LANES - lane0 - v.shape[0]))[None]


def _carried(res, comm):
    return res if comm is not None else (res, [])


def even_mixer_fwd(x, gmix, w_in, w_out, conv_w, a_log, dt_bias, gn, gq, gk, f_bias, tag, comm_fox=None, comm_dn=None,
                   comm_in=None):
    S, D = x.shape
    half = D // 2
    ndn = half // HEAD_DIM
    small_cb = 4 * D // LANES
    alog_row, dt_row, fb_row = _pad_row(a_log), _pad_row(dt_bias), _pad_row(f_bias, 2 * ndn)
    h = rms_fwd(x, gmix, f"{tag}_rms")
    proj, got_in = _carried(mm(h, w_in, "nn", f32, f"{tag}_in", tm=512, tn=1408, comm=comm_in), comm_in)
    cqkv = conv_fwd(proj, conv_w, 3 * half, f"{tag}_conv")
    (o_dn, ssave), got_dn = _carried(dn_fwd(cqkv, proj, small_cb, alog_row, dt_row, ndn, f"{tag}_dn", comm=comm_dn), comm_dn)
    c, cT = gates_fwd(proj, small_cb, fb_row, f"{tag}_gates")
    fq, fk, fv = even_pre(proj, gq, gk, half, 4, f"{tag}_pre")
    (o_fox, lse), got_fox = _carried(fox_fwd(fq, fk, fv, c, cT, ndn, 2 * ndn, f"{tag}_fox", comm=comm_fox), comm_fox)
    om = even_post(o_dn, o_fox, proj, gn, half, 3, 7, f"{tag}_post")
    xo = mm(om, w_out, "nn", f32, f"{tag}_out", res=x)
    return xo, (h, proj, cqkv, o_dn, ssave, c, cT, fq, fk, fv, o_fox, lse, om, alog_row, dt_row, fb_row), got_fox, got_dn, got_in


def even_mixer_bwd(x, gmix, w_in, w_out, conv_w, gn, gq, gk, saved, dy, tag, comm_fox=None, comm_dn=None, comm_bdh=None):
    S, D = x.shape
    half = D // 2
    ndn = half // HEAD_DIM
    small_cb = 4 * D // LANES
    h, proj, cqkv, o_dn, ssave, c, cT, fq, fk, fv, o_fox, lse, om, alog_row, dt_row, fb_row = saved
    dy32, dy16 = dy
    dom = mm(dy16, w_out, "nt", f32, f"{tag}_bdom")
    dw_out = mm(om, dy16, "tn", bf16, f"{tag}_bwout")
    d_odn, d_ofox, d_dgate, d_fgate, d_gn = even_post_bwd(o_dn, o_fox, proj, gn, dom, half, 3, 7, f"{tag}_bpost")
    (dfq, dfk, dfv, dcT), got_fox = _carried(
        fox_bwd(fq, fk, fv, c, cT, o_fox, lse, d_ofox, ndn, 2 * ndn, f"{tag}_bfox", comm=comm_fox), comm_fox)
    dq_pre, dk_pre, d_gq, d_gk = even_pre_bwd(proj, gq, gk, dfq, dfk, half, 4, f"{tag}_bpre")
    (dcqkv, dsm_dn, d_alog, d_dt), got_dn = _carried(
        dn_bwd(cqkv, proj, small_cb, alog_row, dt_row, ssave, d_odn, ndn, f"{tag}_bdn", comm=comm_dn), comm_dn)
    d_conv_in, d_conv_w = conv_bwd(proj, conv_w, dcqkv, f"{tag}_bconv")
    d_small, d_fb = gates_bwd(proj, small_cb, fb_row, dcT, dsm_dn, 2 * ndn, ndn, f"{tag}_bgates")
    dproj = jnp.concatenate([d_conv_in, d_dgate, dq_pre, dk_pre, dfv, d_fgate, d_small], axis=1).astype(bf16)
    dw_in = mm(h, dproj, "tn", bf16, f"{tag}_bwin", tn=1408)
    res = mm_bdh_rms(dproj, w_in, x, gmix, dy32, f"{tag}_bdh", comm=comm_bdh)
    dx, d_gmix = res[:2]
    got_bdh = res[2] if comm_bdh is not None else []
    small = dict(norm_mix=d_gmix[0], dn_conv_w=d_conv_w, dn_a_log=d_alog[0, :ndn], dn_dt_bias=d_dt[0, :ndn],
                 dn_norm_g=d_gn[0], fox_q_norm_g=d_gq[0], fox_k_norm_g=d_gk[0], fox_f_bias=d_fb[0, 2 * ndn:3 * ndn])
    return dx, dw_in, dw_out, small, got_fox, got_dn, got_bdh


def odd_mixer_fwd(x, gmix, w_in, w_out, tag, comm=None):
    S, D = x.shape
    nh = D // HEAD_DIM
    h = rms_fwd(x, gmix, f"{tag}_rms")
    qkv = mm(h, w_in, "nn", bf16, f"{tag}_in", tn=768)
    (o, tails), got = _carried(sb_fwd(qkv, nh, f"{tag}_sb", comm=comm), comm)
    xo = mm(o, w_out, "nn", f32, f"{tag}_out", res=x)
    return xo, (h, qkv, o, tails), got


def odd_mixer_bwd(x, gmix, w_in, w_out, saved, dy, tag, comm=None):
    S, D = x.shape
    nh = D // HEAD_DIM
    h, qkv, o, tails = saved
    dy32, dy16 = dy
    do = mm(dy16, w_out, "nt", f32, f"{tag}_bdo")
    dw_out = mm(o, dy16, "tn", bf16, f"{tag}_bwout")
    (dq, dk, dv), got = _carried(sb_bwd(qkv, tails, do, nh, f"{tag}_bsb", comm=comm), comm)
    dqkv = jnp.concatenate([dq, dk, dv], axis=1).astype(bf16)
    dw_in = mm(h, dqkv, "tn", bf16, f"{tag}_bwin", tn=1536)
    dx, d_gmix = mm_bdh_rms(dqkv, w_in, x, gmix, dy32, f"{tag}_bdh")
    return dx, dw_in, dw_out, dict(norm_mix=d_gmix[0]), got


def all_gather(shards, axes, name, kind="ag"):
    return _call(None, name, (), [], [], [], comm=Comm(kind, shards, axes))()[1]


def sum_parts(parts, name):
    _, R, C = parts.shape
    tm = _tile(R, 256)

    def body(p_ref, o_ref):
        acc = p_ref[0].astype(f32)
        for k in range(1, NDEV):
            acc = acc + p_ref[k].astype(f32)
        o_ref[...] = acc

    return _call(body, name, (R // tm,), [pl.BlockSpec((NDEV, tm, C), lambda i: (0, i, 0))], _rowspec(tm, C), SDS((R, C), f32))(parts)


def adamw(w, g, m, v, name):
    L, R, C = w.shape
    tm = _tile(R, max(8, min(512, (ADAMW_TILE_ELEMS // C) // 8 * 8)))
    c1 = 1.0 - ADAM_B1 ** ADAM_STEP
    c2 = 1.0 - ADAM_B2 ** ADAM_STEP

    def body(w_ref, g_ref, m_ref, v_ref, d_ref, nm_ref, nv_ref):
        g_ = g_ref[...]
        nm = ADAM_B1 * m_ref[...] + (1.0 - ADAM_B1) * g_
        nv = ADAM_B2 * v_ref[...] + (1.0 - ADAM_B2) * (g_ * g_)
        d_ref[...] = -ADAM_LR * ((nm / c1) / (jnp.sqrt(nv / c2) + ADAM_EPS) + ADAM_WD * w_ref[...])
        nm_ref[...] = nm
        nv_ref[...] = nv

    spec = pl.BlockSpec((None, tm, C), lambda l, i: (l, i, 0))
    sh = SDS((L, R, C), f32)
    return _call(body, name, (L, R // tm), [spec] * 4, (spec, spec, spec), (sh, sh, sh))(w, g, m, v)


def _pad_to(n, mult):
    return -(-n // mult) * mult


def _even_perm(D):
    half = D // 2
    ndn = half // HEAD_DIM
    o_gate = 3 * half
    o_b = o_gate + half
    o_a = o_b + ndn
    o_fq = o_a + ndn
    o_fpre = o_fq + 4 * half
    return half, ndn, o_b, o_a, o_fq, o_fpre


def _even_to_mine(w):
    D = (w.shape[-1] // 4) // LANES * LANES
    half, ndn, o_b, o_a, o_fq, o_fpre = _even_perm(D)
    small = jnp.concatenate([w[..., o_b:o_b + ndn], w[..., o_a:o_a + ndn], w[..., o_fpre:o_fpre + ndn]], axis=-1)
    small = jnp.pad(small, [(0, 0)] * (w.ndim - 1) + [(0, LANES - 3 * ndn)])
    return jnp.concatenate([w[..., :o_b], w[..., o_fq:o_fpre], small], axis=-1)


def _even_from_mine(w, D):
    half, ndn, o_b, o_a, o_fq, o_fpre = _even_perm(D)
    sm = w[..., 4 * D:]
    return jnp.concatenate([w[..., :o_b], sm[..., :ndn], sm[..., ndn:2 * ndn], w[..., o_b:4 * D], sm[..., 2 * ndn:3 * ndn]], axis=-1)


def _pack_small(parts):
    flat = jnp.concatenate([p.reshape(-1).astype(f32) for p in parts])
    n = flat.shape[0]
    return jnp.pad(flat, (0, _pad_to(n, 8 * LANES) - n)).reshape(-1, LANES)


def _unpack_small(packed, like):
    flat = packed.reshape(-1)
    out, off = [], 0
    for p in like:
        out.append(flat[off:off + p.size].reshape(p.shape))
        off += p.size
    return out


SMALL_NAMES = ("norm_ffn1", "norm_mix", "dn_a_log", "dn_dt_bias", "dn_norm_g", "fox_q_norm_g", "fox_k_norm_g", "fox_f_bias", "norm_ffn2")
BIG_NAMES = ("ffn1_w_gu", "ffn1_w_down", "w_in_even", "dn_conv_w", "w_out_even", "w_in_odd", "w_out_odd", "ffn2_w_gu", "ffn2_w_down")
WEIGHT_NAMES = ("norm_ffn1", "ffn1_w_gu", "ffn1_w_down", "norm_mix", "w_in_even", "dn_conv_w", "dn_a_log", "dn_dt_bias", "dn_norm_g",
                "fox_q_norm_g", "fox_k_norm_g", "fox_f_bias", "w_out_even", "w_in_odd", "w_out_odd", "norm_ffn2", "ffn2_w_gu", "ffn2_w_down")


def kernel(x, norm_ffn1, ffn1_w_gu, ffn1_w_down, norm_mix, w_in_even, dn_conv_w, dn_a_log, dn_dt_bias, dn_norm_g, fox_q_norm_g, fox_k_norm_g, fox_f_bias, w_out_even, w_in_odd, w_out_odd, norm_ffn2, ffn2_w_gu, ffn2_w_down, loss_target, m_norm_ffn1, m_ffn1_w_gu, m_ffn1_w_down, m_norm_mix, m_w_in_even, m_dn_conv_w, m_dn_a_log, m_dn_dt_bias, m_dn_norm_g, m_fox_q_norm_g, m_fox_k_norm_g, m_fox_f_bias, m_w_out_even, m_w_in_odd, m_w_out_odd, m_norm_ffn2, m_ffn2_w_gu, m_ffn2_w_down, v_norm_ffn1, v_ffn1_w_gu, v_ffn1_w_down, v_norm_mix, v_w_in_even, v_dn_conv_w, v_dn_a_log, v_dn_dt_bias, v_dn_norm_g, v_fox_q_norm_g, v_fox_k_norm_g, v_fox_f_bias, v_w_out_even, v_w_in_odd, v_w_out_odd, v_norm_ffn2, v_ffn2_w_gu, v_ffn2_w_down):
    args = dict(locals())
    W = {n: args[n] for n in WEIGHT_NAMES}
    M = {n: args["m_" + n] for n in WEIGHT_NAMES}
    V = {n: args["v_" + n] for n in WEIGHT_NAMES}
    xs = x[0]
    tgt = loss_target[0]
    S, D = xs.shape
    L = norm_ffn1.shape[0]
    n_even = w_in_even.shape[0]
    hs = ffn1_w_down.shape[1]
    hp = _pad_to(hs, LANES)
    me = 4 * lax.axis_index("x") + 2 * lax.axis_index("y") + lax.axis_index("c")

    def prep_gu(w):
        w = w.reshape(L, D, 2, hs)
        return jnp.pad(w, ((0, 0), (0, 0), (0, 0), (0, hp - hs))).reshape(L, D, 2 * hp).astype(bf16)

    def prep_down(w):
        return jnp.pad(w, ((0, 0), (0, hp - hs), (0, 0))).astype(bf16)

    sh_gu1, sh_d1, sh_gu2, sh_d2 = prep_gu(ffn1_w_gu), prep_down(ffn1_w_down), prep_gu(ffn2_w_gu), prep_down(ffn2_w_down)
    sh_ie, sh_oe = _even_to_mine(w_in_even).astype(bf16), w_out_even.astype(bf16)
    sh_io, sh_oo = w_in_odd.astype(bf16), w_out_odd.astype(bf16)

    def layer_shards(l):
        j = l // 2
        mix = [sh_ie[j], sh_oe[j]] if l % 2 == 0 else [sh_io[j], sh_oo[j]]
        return [sh_gu1[l], sh_d1[l], *mix, sh_gu2[l], sh_d2[l]]

    def layer_axes(l):
        return [1, 0, 0 if l % 2 == 0 else 1, 0, 1, 0]

    def layer_names(l):
        return ["ffn1_w_gu", "ffn1_w_down", *(["w_in_even", "w_out_even"] if l % 2 == 0 else ["w_in_odd", "w_out_odd"]),
                "ffn2_w_gu", "ffn2_w_down"]

    FOX_CARRIES, DN_CARRIES = (0, 4, 3), (1, 2, 5)

    def split_comm(kind, arrays, axes):
        return (Comm(kind, [arrays[i] for i in FOX_CARRIES], [axes[i] for i in FOX_CARRIES]),
                Comm(kind, [arrays[i] for i in DN_CARRIES], [axes[i] for i in DN_CARRIES]))

    def join_split(got_fox, got_dn):
        out = [None] * 6
        for i, a in zip(FOX_CARRIES, got_fox):
            out[i] = a
        for i, a in zip(DN_CARRIES, got_dn):
            out[i] = a
        return out

    sh0, ax0 = layer_shards(0), layer_axes(0)
    first = all_gather(sh0[:2] + [dn_conv_w[None]], ax0[:2] + [0], "ag_layer0", kind="ag2")
    weights = {0: first[:2] + [None] * 4}
    convw = jnp.moveaxis(first[2], 0, 2).reshape(n_even, CONV_WIDTH, -1)
    LAYER0_CARRIES = dict(f1_gu=(2,), f1_down=(3,), mx_in=(5,))

    EVEN_FWD_CARRIES = dict(f1_gu=(), mx_in=(), dn=(1, 2, 5), fox=(0, 4, 3), f2_gu=(), f2_down=())
    saved = []
    cur = xs
    for l in range(L):
        j = l // 2
        wgu1, wd1, win, wout, wgu2, wd2 = weights[l]
        nxt = l + 1 < L
        x0 = cur
        if l % 2 == 0:
            sh_n, ax_n = (layer_shards(l + 1), layer_axes(l + 1)) if nxt else (None, None)
            cm = {k: (Comm("ag2", [sh_n[i] for i in idx], [ax_n[i] for i in idx]) if nxt and idx else None)
                  for k, idx in EVEN_FWD_CARRIES.items()}
            if l == 0:
                cm.update({k: Comm("ag2", [sh0[i] for i in idx], [ax0[i] for i in idx]) for k, idx in LAYER0_CARRIES.items()})
                cm["fox"] = Comm("ag2", cm["fox"].arrays + [sh0[4]], cm["fox"].axes + [ax0[4]])
            x1, s1, got_f1gu, got_f1down = ffn_fwd(x0, norm_ffn1[l:l + 1], wgu1, wd1, f"l{l}f1", comm_gu=cm["f1_gu"],
                                                   comm_down=cm["f1_down"] if l == 0 else None)
            if l == 0:
                win, wout = got_f1gu[0], got_f1down[0]
            x2, sm, got_f, got_d, got_in = even_mixer_fwd(
                x1, norm_mix[l:l + 1], win, wout, convw[j], dn_a_log[j], dn_dt_bias[j], dn_norm_g[j:j + 1],
                fox_q_norm_g[j:j + 1], fox_k_norm_g[j:j + 1], fox_f_bias[j], f"l{l}mx", comm_fox=cm["fox"], comm_dn=cm["dn"],
                comm_in=cm["mx_in"])
            if l == 0:
                wd2, wgu2 = got_in[0], got_f[-1]
                weights[0] = [wgu1, wd1, win, wout, wgu2, wd2]
                got_f1gu, got_in, got_f = [], [], got_f[:-1]
            x3, s2, got_f2gu, got_f2down = ffn_fwd(x2, norm_ffn2[l:l + 1], wgu2, wd2, f"l{l}f2", comm_gu=cm["f2_gu"],
                                                   comm_down=cm["f2_down"])
            if nxt:
                got = dict(f1_gu=got_f1gu, mx_in=got_in, dn=got_d, fox=got_f, f2_gu=got_f2gu, f2_down=got_f2down)
                weights[l + 1] = [None] * 6
                for k, idx in EVEN_FWD_CARRIES.items():
                    for i, a in zip(idx, got[k]):
                        weights[l + 1][i] = a
        else:
            x1, s1, _, _ = ffn_fwd(x0, norm_ffn1[l:l + 1], wgu1, wd1, f"l{l}f1")
            cm = Comm("ag2", layer_shards(l + 1), layer_axes(l + 1)) if nxt else None
            x2, sm, got = odd_mixer_fwd(x1, norm_mix[l:l + 1], win, wout, f"l{l}mx", comm=cm)
            if nxt:
                weights[l + 1] = got
            x3, s2, _, _ = ffn_fwd(x2, norm_ffn2[l:l + 1], wgu2, wd2, f"l{l}f2")
        saved.append((x0, x1, x2, s1, sm, s2))
        cur = x3
    dy, loss_row = loss_head(cur, tgt, "loss")

    gsmall = {n: [None] * W[n].shape[0] for n in SMALL_NAMES}
    gconv = [None] * n_even
    parts = {n: [None] * W[n].shape[0] for n in BIG_NAMES if n != "dn_conv_w"}

    def take(l, recv):
        for nm, p in zip(layer_names(l), recv):
            idx = l if nm.startswith("ffn") else l // 2
            parts[nm][idx] = sum_parts(p.reshape(NDEV, -1, p.shape[-1]), f"sum_{nm}_{idx}").reshape(p.shape[1:])

    pending = None
    for l in reversed(range(L)):
        j = l // 2
        wgu1, wd1, win, wout, wgu2, wd2 = weights[l]
        x0, x1, x2, s1, sm, s2 = saved[l]
        dy, dg, dwgu2, dwd2 = ffn_bwd(x2, norm_ffn2[l:l + 1], wgu2, wd2, s2, dy, f"l{l}f2")
        gsmall["norm_ffn2"][l] = dg[0]
        if l % 2 == 0:
            cf, cd = split_comm("rs", pending, layer_axes(l + 1)) if pending is not None else (None, None)
            cb = None
            if l == 0:
                cd = Comm("rs", cd.arrays + [dwd2], cd.axes + [ax0[5]])
                cb = Comm("rs", [dwgu2], [ax0[4]])
            dy, dwin, dwout, sg, got_f, got_d, got_b = even_mixer_bwd(
                x1, norm_mix[l:l + 1], win, wout, convw[j], dn_norm_g[j:j + 1], fox_q_norm_g[j:j + 1],
                fox_k_norm_g[j:j + 1], sm, dy, f"l{l}mx", comm_fox=cf, comm_dn=cd, comm_bdh=cb)
            if l == 0:
                got_wd2, got_wgu2, got_d = got_d[-1], got_b[0], got_d[:-1]
            if pending is not None:
                take(l + 1, join_split(got_f, got_d))
            gconv[j] = sg.pop("dn_conv_w")
            for k_, v_ in sg.items():
                gsmall[k_][l if k_ == "norm_mix" else j] = v_
        else:
            cm = Comm("rs", pending, layer_axes(l + 1)) if pending is not None else None
            dy, dwin, dwout, sg, got = odd_mixer_bwd(x1, norm_mix[l:l + 1], win, wout, sm, dy, f"l{l}mx", comm=cm)
            if pending is not None:
                take(l + 1, got)
            gsmall["norm_mix"][l] = sg["norm_mix"]
        if l > 0:
            dy, dg, dwgu1, dwd1 = ffn_bwd(x0, norm_ffn1[l:l + 1], wgu1, wd1, s1, dy, f"l{l}f1")
        else:
            rs = lambda a, ax: Comm("rs", [a], [ax])
            dy, dg, dwgu1, dwd1, got0 = ffn_bwd(x0, norm_ffn1[l:l + 1], wgu1, wd1, s1, dy, f"l{l}f1",
                                                comms=dict(bda=rs(dwin, ax0[2]), bwd=rs(dwout, ax0[3])), own_axes=(ax0[0], ax0[1]))
        gsmall["norm_ffn1"][l] = dg[0]
        pending = [dwgu1, dwd1, dwin, dwout, dwgu2, dwd2]
    take(0, [got0["wgu"][0], got0["wd"][0], got0["bda"][0], got0["bwd"][0], got_wgu2, got_wd2])
    grad_x = dy[0][None]

    small_list = [jnp.stack(gsmall[n]) for n in SMALL_NAMES] + [jnp.stack(gconv), loss_row[0, :1]]
    packed = _pack_small(small_list)
    gathered = all_gather([packed], [0], "ag_small")[0]
    summed = sum_parts(gathered.reshape(NDEV, packed.shape[0], LANES), "sum_small")
    small_sum = _unpack_small(summed, small_list)
    G = dict(zip(SMALL_NAMES, small_sum[:len(SMALL_NAMES)]))
    conv_full = small_sum[len(SMALL_NAMES)]
    cwid = dn_conv_w.shape[2]
    G["dn_conv_w"] = lax.dynamic_slice_in_dim(conv_full, me * cwid, cwid, axis=2)
    loss = small_sum[-1][0]

    def unprep_gu(g):
        return g.reshape(L, D, 2, hp)[..., :hs].reshape(L, D, 2 * hs)

    G["ffn1_w_gu"] = unprep_gu(jnp.stack(parts["ffn1_w_gu"]))
    G["ffn2_w_gu"] = unprep_gu(jnp.stack(parts["ffn2_w_gu"]))
    G["ffn1_w_down"] = jnp.stack(parts["ffn1_w_down"])[:, :hs]
    G["ffn2_w_down"] = jnp.stack(parts["ffn2_w_down"])[:, :hs]
    G["w_in_even"] = _even_from_mine(jnp.stack(parts["w_in_even"]), D)
    G["w_out_even"] = jnp.stack(parts["w_out_even"])
    G["w_in_odd"] = jnp.stack(parts["w_in_odd"])
    G["w_out_odd"] = jnp.stack(parts["w_out_odd"])

    delta, new_m, new_v = {}, {}, {}
    for n in BIG_NAMES:
        t = (lambda a: jnp.swapaxes(a, 1, 2)) if n.endswith("w_gu") else (lambda a: a)
        delta[n], new_m[n], new_v[n] = map(t, adamw(t(W[n]), t(G[n]), t(M[n]), t(V[n]), f"adamw_{n}"))
    sw = [W[n] for n in SMALL_NAMES]
    d_, m_, v_ = adamw(_pack_small(sw)[None], _pack_small([G[n] for n in SMALL_NAMES])[None],
                       _pack_small([M[n] for n in SMALL_NAMES])[None], _pack_small([V[n] for n in SMALL_NAMES])[None], "adamw_small")
    for n, a, b, c_ in zip(SMALL_NAMES, _unpack_small(d_, sw), _unpack_small(m_, sw), _unpack_small(v_, sw)):
        delta[n], new_m[n], new_v[n] = a, b, c_

    return (loss, grad_x, *[G[n] for n in WEIGHT_NAMES], *[delta[n] for n in WEIGHT_NAMES],
            *[new_m[n] for n in WEIGHT_NAMES], *[new_v[n] for n in WEIGHT_NAMES])
```

```python
import functools
import math

import jax
import jax.numpy as jnp
from jax import lax
from jax.experimental import pallas as pl
from jax.experimental.pallas import tpu as pltpu

f32 = jnp.float32
bf16 = jnp.bfloat16
SDS = jax.ShapeDtypeStruct

HEAD_DIM = 128
LANES = 128
CONV_WIDTH = 4
DN_CHUNK = 128
EPS = 1e-6
NDEV = 8
VMEM_LIMIT_BYTES = 56 * 1024 * 1024
HI = lax.Precision.HIGHEST
ADAMW_TILE_ELEMS = 384 * 1024

ADAM_LR = 0.001
ADAM_B1 = 0.9
ADAM_B2 = 0.999
ADAM_EPS = 1e-08
ADAM_WD = 0.01
ADAM_STEP = 10

NN = (((1,), (0,)), ((), ()))
NT = (((1,), (1,)), ((), ()))
TN = (((0,), (0,)), ((), ()))


def _me_and_peers():
    x, y, c = lax.axis_index("x"), lax.axis_index("y"), lax.axis_index("c")
    me = 4 * x + 2 * y + c
    peers = []
    for r in range(1, NDEV):
        px = 1 - x if (r >> 2) & 1 else x
        py = 1 - y if (r >> 1) & 1 else y
        pc = 1 - c if r & 1 else c
        peers.append(((px, py, pc), 4 * px + 2 * py + pc))
    return me, peers


def _slab(ref, axis, width, k):
    idx = [slice(None)] * len(ref.shape)
    idx[axis] = pl.ds(k * width, width)
    return ref.at[tuple(idx)]


class Comm:
    def __init__(self, kind, arrays, axes):
        self.kind, self.arrays, self.axes, self.n = kind, list(arrays), list(axes), len(arrays)

    def out_shape(self):
        out = []
        for a, ax in zip(self.arrays, self.axes):
            shp = list(a.shape)
            if self.kind != "rs":
                shp[ax] *= NDEV
                out.append(SDS(tuple(shp), a.dtype))
            else:
                shp[ax] //= NDEV
                out.append(SDS((NDEV, *shp), a.dtype))
        return out

    def sems(self):
        return [pltpu.SemaphoreType.DMA((self.n, NDEV - 1)), pltpu.SemaphoreType.DMA((self.n, NDEV - 1)),
                pltpu.SemaphoreType.DMA((self.n,))]

    def _src(self, ins, t, to_idx):
        if self.kind == "ag":
            return ins[t]
        return _slab(ins[t], self.axes[t], ins[t].shape[self.axes[t]] // NDEV, to_idx)

    def _dst(self, ins, outs, t, from_idx):
        if self.kind != "rs":
            return _slab(outs[t], self.axes[t], ins[t].shape[self.axes[t]], from_idx)
        return outs[t].at[from_idx]

    def _copies(self, ins, outs, sems, sending):
        send_sems, recv_sems, loc_sems = sems
        me, peers = _me_and_peers()
        local, remote = [], []
        for t in range(self.n):
            local.append(pltpu.make_async_copy(self._src(ins, t, me), self._dst(ins, outs, t, me), loc_sems.at[t]))
            for r, (peer, pidx) in enumerate(peers):
                remote.append(pltpu.make_async_remote_copy(
                    src_ref=self._src(ins, t, pidx), dst_ref=self._dst(ins, outs, t, me if sending else pidx),
                    send_sem=send_sems.at[t, r], recv_sem=recv_sems.at[t, r], device_id=peer,
                    device_id_type=pl.DeviceIdType.MESH))
        return local, remote

    def _two_level(self, ins, outs, sems, own=True):
        send_sems, recv_sems, loc_sems = sems
        x, y, c = lax.axis_index("x"), lax.axis_index("y"), lax.axis_index("c")
        me, sibling = (x, y, c), (x, y, 1 - c)
        chips = [(1 - x, y), (x, 1 - y), (1 - x, 1 - y)]
        dev = lambda p: 4 * p[0] + 2 * p[1] + p[2]

        def copy(t, k, block, to, from_shard):
            dst = self._dst(ins, outs, t, dev(block))
            return pltpu.make_async_remote_copy(
                src_ref=ins[t] if from_shard else dst, dst_ref=dst, send_sem=send_sems.at[t, k], recv_sem=recv_sems.at[t, k],
                device_id=to, device_id_type=pl.DeviceIdType.MESH)

        ts = range(self.n) if own else ()
        local = [pltpu.make_async_copy(ins[t], self._dst(ins, outs, t, dev(me)), loc_sems.at[t]) for t in ts]
        first = [copy(t, 0, me, sibling, True) for t in ts]
        first += [copy(t, 1 + j, me, (*chip, c), True) for t in ts for j, chip in enumerate(chips)]
        return local, first, copy, chips, me, sibling, c

    def start(self, ins, outs, sems):
        if self.kind == "ag2":
            local, first = self._two_level(ins, outs, sems)[:2]
            for cp in local + first:
                cp.start()
            return
        local, remote = self._copies(ins, outs, sems, True)
        for cp in local + remote:
            cp.start()

    def relay(self, ins, outs, sems):
        _, _, copy, chips, me, sibling, c = self._two_level(ins, outs, sems, own=False)
        for t in range(self.n):
            for j, chip in enumerate(chips):
                copy(t, 1 + j, (*chip, c), me, False).wait_recv()
                copy(t, 4 + j, (*chip, c), sibling, False).start()

    def wait(self, ins, outs, sems, relayed=False):
        if self.kind == "ag2":
            if not relayed:
                self.relay(ins, outs, sems)
            local, first, copy, chips, me, sibling, c = self._two_level(ins, outs, sems)
            passed = [copy(t, 4 + j, (*chip, c), sibling, False) for t in range(self.n) for j, chip in enumerate(chips)]
            for t in range(self.n):
                copy(t, 0, sibling, me, False).wait_recv()
                for j, chip in enumerate(chips):
                    copy(t, 4 + j, (*chip, 1 - c), me, False).wait_recv()
            for cp in first + passed:
                cp.wait_send()
            for cp in local:
                cp.wait()
            return
        local, remote = self._copies(ins, outs, sems, False)
        for cp in remote:
            cp.wait_recv()
            cp.wait_send()
        for cp in local:
            cp.wait()


def _call(body, name, grid, in_specs, out_specs, out_shape, scratch=(), comm=None):
    params = pltpu.CompilerParams(vmem_limit_bytes=VMEM_LIMIT_BYTES)
    if comm is None:
        return pl.pallas_call(body, name=name, grid=grid, in_specs=in_specs, out_specs=out_specs, out_shape=out_shape,
                              scratch_shapes=list(scratch), compiler_params=params)
    single = not isinstance(out_shape, (tuple, list))
    c_out_specs = [out_specs] if single else list(out_specs)
    c_out_shape = [out_shape] if single else list(out_shape)
    n_in, n_out, n_scr, n = len(in_specs), len(c_out_shape), len(scratch), comm.n
    anyspec = pl.BlockSpec(memory_space=pl.ANY)

    def wrapped(*refs):
        ins, refs = refs[:n_in], refs[n_in:]
        cins, refs = refs[:n], refs[n:]
        outs, refs = refs[:n_out], refs[n_out:]
        couts, refs = refs[:n], refs[n:]
        scr, sems = refs[:n_scr], refs[n_scr:]
        first = functools.reduce(lambda a, b: a & b, [pl.program_id(d) == 0 for d in range(len(grid))], True)
        last = functools.reduce(lambda a, b: a & b, [pl.program_id(d) == grid[d] - 1 for d in range(len(grid))], True)
        if grid:
            steps = math.prod(grid)
            step = functools.reduce(lambda a, d: a * grid[d] + pl.program_id(d), range(len(grid)), 0)
            relay_early = comm.kind == "ag2" and steps >= 4
            pl.when(first)(lambda: comm.start(cins, couts, sems))
            body(*ins, *outs, *scr)
            if relay_early:
                pl.when(step == (3 * steps) // 4)(lambda: comm.relay(cins, couts, sems))
            pl.when(last)(lambda: comm.wait(cins, couts, sems, relayed=relay_early))
        else:
            comm.start(cins, couts, sems)
            if body is not None:
                body(*ins, *outs, *scr)
            comm.wait(cins, couts, sems)

    call = pl.pallas_call(
        wrapped, name=name, grid=grid, in_specs=list(in_specs) + [anyspec] * n, out_specs=c_out_specs + [anyspec] * n,
        out_shape=c_out_shape + comm.out_shape(), scratch_shapes=list(scratch) + comm.sems(), compiler_params=params)

    def run(*args):
        res = call(*args, *comm.arrays)
        mine = res[:n_out]
        return (mine[0] if single else tuple(mine)), list(res[n_out:])

    return run


def _tile(n, want, align=8):
    if n <= want:
        return n
    for t in range(want // align * align, 0, -align):
        if n % t == 0:
            return t
    return n


def _dot(a, b, dims=NN, precision=None):
    return lax.dot_general(a, b, dims, preferred_element_type=f32, precision=precision)


def _logsig(x):
    return jnp.minimum(x, 0.0) - jnp.log(1.0 + jnp.exp(-jnp.abs(x)))


def _softplus(x):
    return jnp.maximum(x, 0.0) + jnp.log(1.0 + jnp.exp(-jnp.abs(x)))


def _rms(x, g):
    return x * lax.rsqrt(jnp.mean(x * x, axis=-1, keepdims=True) + EPS) * g


def _lane_pick(blk, lane_idx):
    lane = lax.broadcasted_iota(jnp.int32, (1, LANES), 1)
    return jnp.sum(jnp.where(lane == lane_idx, blk, 0.0), axis=1, keepdims=True)


def mm(a, b, mode, out_dtype, name, tm=512, tn=512, res=None, scale=1.0, comm=None):
    if mode == "nn":
        (M, K), (_, N) = a.shape, b.shape
    elif mode == "nt":
        (M, K), (N, _) = a.shape, b.shape
    else:
        (K, M), (_, N) = a.shape, b.shape
    tm, tn = _tile(M, tm, LANES), _tile(N, tn, LANES)
    a_spec = pl.BlockSpec((K, tm), lambda j, i: (0, i)) if mode == "tn" else pl.BlockSpec((tm, K), lambda j, i: (i, 0))
    b_spec = pl.BlockSpec((tn, K), lambda j, i: (j, 0)) if mode == "nt" else pl.BlockSpec((K, tn), lambda j, i: (0, j))
    dims = {"nn": NN, "nt": NT, "tn": TN}[mode]
    o_spec = pl.BlockSpec((tm, tn), lambda j, i: (i, j))

    def body(*refs):
        acc = _dot(refs[0][...].astype(bf16), refs[1][...].astype(bf16), dims)
        if scale != 1.0:
            acc = acc * scale
        if res is not None:
            acc = acc + refs[2][...]
        refs[-1][...] = acc.astype(out_dtype)

    ins, specs = [a, b], [a_spec, b_spec]
    if res is not None:
        ins.append(res)
        specs.append(o_spec)
    return _call(body, name, (N // tn, M // tm), specs, o_spec, SDS((M, N), out_dtype), comm=comm)(*ins)


def _rowspec(tm, w, cb=0):
    return pl.BlockSpec((tm, w), lambda i: (i, cb))


def _bspec(w):
    return pl.BlockSpec((1, w), lambda i: (0, 0))


def rms_fwd(x, g, name):
    S, D = x.shape
    tm = _tile(S, 512)

    def body(x_ref, g_ref, h_ref):
        h_ref[...] = _rms(x_ref[...], g_ref[...]).astype(bf16)

    return _call(body, name, (S // tm,), [_rowspec(tm, D), _bspec(D)], _rowspec(tm, D), SDS((S, D), bf16))(x, g)


def _swiglu(g, u):
    return g * jax.nn.sigmoid(g) * u


def ffn_gu_act(h, wgu, name, tm=1024, tn=768, comm=None):
    S, D = h.shape
    W = wgu.shape[1] // 2
    tm, tn = _tile(S, tm, LANES), _tile(W, tn, LANES)
    nb = W // tn

    def body(h_ref, wg_ref, wu_ref, gu_ref, a_ref):
        hb = h_ref[...]
        g = _dot(hb, wg_ref[...])
        u = _dot(hb, wu_ref[...])
        gu_ref[0] = g.astype(bf16)
        gu_ref[1] = u.astype(bf16)
        a_ref[...] = _swiglu(g, u).astype(bf16)

    return _call(body, name, (nb, S // tm),
                 [pl.BlockSpec((tm, D), lambda j, i: (i, 0)), pl.BlockSpec((D, tn), lambda j, i: (0, j)),
                  pl.BlockSpec((D, tn), lambda j, i: (0, nb + j))],
                 (pl.BlockSpec((2, tm, tn), lambda j, i: (0, i, j)), pl.BlockSpec((tm, tn), lambda j, i: (i, j))),
                 (SDS((2, S, W), bf16), SDS((S, W), bf16)), comm=comm)(h, wgu, wgu)


def ffn_bda_act(dy, wd, gu, scale, name, tm=512, tn=768, comm=None):
    S, D = dy.shape
    W = wd.shape[0]
    tm, tn = _tile(S, tm, LANES), _tile(W, tn, LANES)

    def body(dy_ref, wd_ref, gu_ref, dgu_ref):
        da = _dot(dy_ref[...].astype(bf16), wd_ref[...], NT) * scale
        _, vjp = jax.vjp(_swiglu, gu_ref[0].astype(f32), gu_ref[1].astype(f32))
        dg, du = vjp(da)
        dgu_ref[0] = dg.astype(bf16)
        dgu_ref[1] = du.astype(bf16)

    planes = pl.BlockSpec((2, tm, tn), lambda j, i: (0, i, j))
    return _call(body, name, (W // tn, S // tm),
                 [pl.BlockSpec((tm, D), lambda j, i: (i, 0)), pl.BlockSpec((tn, D), lambda j, i: (j, 0)), planes],
                 planes, SDS((2, S, W), bf16), comm=comm)(dy, wd, gu)


def ffn_bwgu(h, dgu, name, tm=512, tn=768, comm=None):
    S, D = h.shape
    W = dgu.shape[2]
    tm, tn = _tile(D, tm, LANES), _tile(W, tn, LANES)
    nb = W // tn

    def body(h_ref, d_ref, o_ref):
        o_ref[...] = _dot(h_ref[...], d_ref[...], TN).astype(bf16)

    return _call(body, name, (2 * nb, D // tm),
                 [pl.BlockSpec((S, tm), lambda j, i: (0, i)), pl.BlockSpec((None, S, tn), lambda j, i: (j // nb, 0, j % nb))],
                 pl.BlockSpec((tm, tn), lambda j, i: (i, j)), SDS((D, 2 * W), bf16), comm=comm)(h, dgu)


def nt_rms_bwd(pairs, x, g, dres, name, tm=256, comm=None):
    S, D = x.shape
    tm = _tile(S, tm)
    n = len(pairs)

    def body(*refs):
        x_ref, g_ref, dres_ref = refs[2 * n:2 * n + 3]
        dx_ref, dxb_ref, dg_ref = refs[2 * n + 3:]
        dh = sum(_dot(refs[2 * k][...].astype(bf16), refs[2 * k + 1][...], NT) for k in range(n))
        _, vjp = jax.vjp(_rms, x_ref[...], g_ref[...])
        dx, dg = vjp(dh)
        dx = dres_ref[...] + dx
        dx_ref[...] = dx
        dxb_ref[...] = dx.astype(bf16)

        @pl.when(pl.program_id(0) == 0)
        def _():
            dg_ref[...] = jnp.zeros_like(dg_ref)

        dg_ref[...] += dg

    arrays, specs = [], []
    for a, a_spec, b, b_spec in pairs:
        arrays += [a, b]
        specs += [a_spec, b_spec]
    (dx, dxb, dg), got = _carried(_call(body, name, (S // tm,), specs + [_rowspec(tm, D), _bspec(D), _rowspec(tm, D)],
                                        (_rowspec(tm, D), _rowspec(tm, D), _bspec(D)),
                                        (SDS((S, D), f32), SDS((S, D), bf16), SDS((1, D), f32)), comm=comm)(*arrays, x, g, dres),
                                  comm)
    return ((dx, dxb), dg) if comm is None else ((dx, dxb), dg, got)


def ffn_bdh_rms(dgu, wgu, x, g, dres, name, tm=256, comm=None):
    _, S, W = dgu.shape
    D = wgu.shape[0]
    tm = _tile(S, tm)
    pairs = [(dgu, pl.BlockSpec((None, tm, W), functools.partial(lambda p, i: (p, i, 0), p)),
              wgu, pl.BlockSpec((D, W), functools.partial(lambda p, i: (0, p), p))) for p in range(2)]
    return nt_rms_bwd(pairs, x, g, dres, name, tm, comm=comm)


def mm_bdh_rms(d, w, x, g, dres, name, tm=256, comm=None):
    S, K = d.shape
    tm = _tile(S, tm)
    return nt_rms_bwd([(d, pl.BlockSpec((tm, K), lambda i: (i, 0)), w, pl.BlockSpec(w.shape, lambda i: (0, 0)))], x, g, dres, name, tm,
                      comm=comm)


def loss_head(y, t, name):
    S, D = y.shape
    tm = _tile(S, 512)

    def body(y_ref, t_ref, dy_ref, dyb_ref, l_ref):
        e = y_ref[...] - t_ref[...]
        dy_ref[...] = e * (1.0 / D)
        dyb_ref[...] = (e * (1.0 / D)).astype(bf16)

        @pl.when(pl.program_id(0) == 0)
        def _():
            l_ref[...] = jnp.zeros_like(l_ref)

        l_ref[...] += jnp.sum(jnp.sum(e * e, axis=1, keepdims=True), axis=0, keepdims=True) * (0.5 / D)

    dy, dyb, loss = _call(body, name, (S // tm,), [_rowspec(tm, D), _rowspec(tm, D)],
                          (_rowspec(tm, D), _rowspec(tm, D), _bspec(LANES)),
                          (SDS((S, D), f32), SDS((S, D), bf16), SDS((1, LANES), f32)))(y, t)
    return (dy, dyb), loss


def ffn_fwd(x, g, wgu, wd, tag, comm_gu=None, comm_down=None):
    h = rms_fwd(x, g, f"{tag}_rms")
    (gu, a), got_gu = _carried(ffn_gu_act(h, wgu, f"{tag}_gu", comm=comm_gu), comm_gu)
    xo, got_down = _carried(mm(a, wd, "nn", f32, f"{tag}_down", tm=512, tn=512, res=x, scale=0.5, comm=comm_down), comm_down)
    return xo, (h, gu, a), got_gu, got_down


def ffn_bwd(x, g, wgu, wd, saved, dy, tag, comms=None, own_axes=None):
    h, gu, a = saved
    dy32, dy16 = dy
    cm = comms or {}
    dgu, got_bda = _carried(ffn_bda_act(dy16, wd, gu, 0.5, f"{tag}_bda", comm=cm.get("bda")), cm.get("bda"))
    dwd, got_bwd = _carried(mm(a, dy16, "tn", bf16, f"{tag}_bwd", tm=512, tn=512, scale=0.5, comm=cm.get("bwd")), cm.get("bwd"))
    c_wd = Comm("rs", [dwd], [own_axes[1]]) if own_axes else None
    dwgu, got_wd = _carried(ffn_bwgu(h, dgu, f"{tag}_bwgu", comm=c_wd), c_wd)
    c_wgu = Comm("rs", [dwgu], [own_axes[0]]) if own_axes else None
    res = ffn_bdh_rms(dgu, wgu, x, g, dy32, f"{tag}_bdh", comm=c_wgu)
    dx, dg = res[:2]
    if comms is None and own_axes is None:
        return dx, dg, dwgu, dwd
    return dx, dg, dwgu, dwd, dict(bda=got_bda, bwd=got_bwd, wd=got_wd, wgu=res[2] if c_wgu is not None else [])


def _split_bf16(x):
    hi = x.astype(bf16)
    lo = (x - hi.astype(f32)).astype(bf16)
    return hi, lo


SB_TQ, SB_TK, SB_NSUB = 512, 256, 4
FOX_TK = 256


def _sb_geometry(S, tk=SB_TK):
    tq = min(SB_TQ, S)
    tk = min(tk, tq)
    return tq, tk, tq // SB_NSUB, tq // tk


def _sb_consts(tk):
    r = lax.broadcasted_iota(jnp.int32, (tk, tk), 0)
    c = lax.broadcasted_iota(jnp.int32, (tk, tk), 1)
    return (r > c).astype(bf16), (r < c).astype(bf16)


def _sb_scores(qs, k, kb, tk, rows, masked):
    scale = HEAD_DIM ** -0.5
    z = [_dot(q, k, NT) * scale for q in qs]
    ls = [_logsig(z_) for z_ in z]
    lm = [l - z_ for l, z_ in zip(ls, z)]
    ok = None
    if masked:
        col = kb * tk + lax.broadcasted_iota(jnp.int32, z[0].shape, 1)
        ok = [col < row for row in rows]
        lm = [jnp.where(m, x, 0.0) for m, x in zip(ok, lm)]
    return ls, lm, ok


def _sb_weights(ls, lm, ok, tail, after):
    suf = [_dot(x.astype(bf16), after) for x in lm]
    a = [jnp.exp(l + s_ + t_) for l, s_, t_ in zip(ls, suf, tail)]
    if ok is not None:
        a = [jnp.where(m, x, 0.0) for m, x in zip(ok, a)]
    return a


def sb_fwd(qkv, nh, name, comm=None):
    S = qkv.shape[0]
    tq, tk, rs, nd = _sb_geometry(S)
    nsub = tq // rs

    def body(q_ref, k_ref, v_ref, o_ref, tails_ref):
        i = pl.program_id(1)
        after, _ = _sb_consts(tk)
        qs = [q_ref[pl.ds(s * rs, rs), :] for s in range(nsub)]
        rows = [i * tq + s * rs + lax.broadcasted_iota(jnp.int32, (rs, tk), 0) for s in range(nsub)]

        def tile(kb, carry, masked):
            tail, acc = carry
            off = pl.multiple_of(kb * tk, tk)
            k = k_ref[pl.ds(off, tk), :]
            v = v_ref[pl.ds(off, tk), :]
            ls, lm, ok = _sb_scores(qs, k, kb, tk, rows, masked)
            a = _sb_weights(ls, lm, ok, tail, after)
            tails_ref[pl.ds(kb, 1), :] += jnp.concatenate([t_.T for t_ in tail], axis=1)
            acc = [ac + _dot(a_.astype(bf16), v) for ac, a_ in zip(acc, a)]
            tail = [t_ + jnp.sum(x, axis=1, keepdims=True) for t_, x in zip(tail, lm)]
            return tail, acc

        tails_ref[...] = jnp.zeros_like(tails_ref)
        carry = ([jnp.zeros((rs, 1), f32)] * nsub, [jnp.zeros((rs, HEAD_DIM), f32)] * nsub)
        for d in reversed(range(nd)):
            carry = tile(i * nd + d, carry, True)
        carry = lax.fori_loop(0, i * nd, lambda t, c: tile(i * nd - 1 - t, c, False), carry)
        for s in range(nsub):
            o_ref[pl.ds(s * rs, rs), :] = carry[1][s].astype(o_ref.dtype)

    blk = lambda cb0: pl.BlockSpec((tq, HEAD_DIM), lambda h, i: (i, cb0 + h))
    full = lambda cb0: pl.BlockSpec((S, HEAD_DIM), lambda h, i: (0, cb0 + h))
    tails = pl.BlockSpec((S // tk, tq), lambda h, i: (h, i))
    return _call(body, name, (nh, S // tq), [blk(0), full(nh), full(2 * nh)], (blk(0), tails),
                 (SDS((S, nh * HEAD_DIM), bf16), SDS((nh * (S // tk), S), f32)), comm=comm)(qkv, qkv, qkv)


def sb_bwd(qkv, tails, do, nh, name, comm=None):
    S = qkv.shape[0]
    tq, tk, rs, nd = _sb_geometry(S)
    nsub = tq // rs
    scale = HEAD_DIM ** -0.5

    def body(q_ref, k_ref, v_ref, tails_ref, do_ref, dq_ref, dk_ref, dv_ref):
        i = pl.program_id(1)

        @pl.when(i == 0)
        def _():
            dk_ref[...] = jnp.zeros_like(dk_ref)
            dv_ref[...] = jnp.zeros_like(dv_ref)

        after, before = _sb_consts(tk)
        qs = [q_ref[pl.ds(s * rs, rs), :] for s in range(nsub)]
        dos = [do_ref[pl.ds(s * rs, rs), :].astype(bf16) for s in range(nsub)]
        rows = [i * tq + s * rs + lax.broadcasted_iota(jnp.int32, (rs, tk), 0) for s in range(nsub)]

        def sweep2(kb, carry, masked):
            pe, dq = carry
            off = pl.multiple_of(kb * tk, tk)
            k = k_ref[pl.ds(off, tk), :]
            v = v_ref[pl.ds(off, tk), :]
            ls, lm, ok = _sb_scores(qs, k, kb, tk, rows, masked)
            tail_row = tails_ref[pl.ds(kb, 1), :]
            tail = [tail_row[:, s * rs:(s + 1) * rs].T for s in range(nsub)]
            a = _sb_weights(ls, lm, ok, tail, after)
            e = [_dot(d_, v, NT) * a_ for d_, a_ in zip(dos, a)]
            cum_e = [p_ + _dot(e_.astype(bf16), before) for p_, e_ in zip(pe, e)]
            sig = [jnp.exp(l) for l in ls]
            dz = [e_ * (1.0 - sg) - c_ * sg for e_, sg, c_ in zip(e, sig, cum_e)]
            if ok is not None:
                dz = [jnp.where(m, x, 0.0) for m, x in zip(ok, dz)]
            dzb = [(x * scale).astype(bf16) for x in dz]
            dk_ref[pl.ds(off, tk), :] += sum(_dot(x, q, TN) for x, q in zip(dzb, qs))
            dv_ref[pl.ds(off, tk), :] += sum(_dot(a_.astype(bf16), d_, TN) for a_, d_ in zip(a, dos))
            pe = [p_ + jnp.sum(e_, axis=1, keepdims=True) for p_, e_ in zip(pe, e)]
            dq = [g + _dot(x, k) for g, x in zip(dq, dzb)]
            return pe, dq

        carry = ([jnp.zeros((rs, 1), f32)] * nsub, [jnp.zeros((rs, HEAD_DIM), f32)] * nsub)
        carry = lax.fori_loop(0, i * nd, lambda kb, c: sweep2(kb, c, False), carry)
        for d in range(nd):
            carry = sweep2(i * nd + d, carry, True)
        for s in range(nsub):
            dq_ref[pl.ds(s * rs, rs), :] = carry[1][s]

    blk = lambda cb0: pl.BlockSpec((tq, HEAD_DIM), lambda h, i: (i, cb0 + h))
    full = lambda cb0: pl.BlockSpec((S, HEAD_DIM), lambda h, i: (0, cb0 + h))
    sh = SDS((S, nh * HEAD_DIM), f32)
    tails_spec = pl.BlockSpec((S // tk, tq), lambda h, i: (h, i))
    return _call(body, name, (nh, S // tq), [blk(0), full(nh), full(2 * nh), tails_spec, blk(0)], (blk(0), full(0), full(0)),
                 (sh, sh, sh), comm=comm)(qkv, qkv, qkv, tails, do)


def fox_fwd(fq, fk, fv, c, cT, nh, lane0, name, comm=None):
    S = fq.shape[0]
    tq, tk, rs, nd = _sb_geometry(S, FOX_TK)
    nsub = tq // rs
    scale = HEAD_DIM ** -0.5

    def body(q_ref, k_ref, v_ref, c_ref, ct_ref, o_ref, lse_ref):
        h = pl.program_id(0)
        i = pl.program_id(1)
        subs = range(nsub)
        qs = [q_ref[pl.ds(s * rs, rs), :] for s in subs]
        ci = [_lane_pick(c_ref[pl.ds(s * rs, rs), :], lane0 + h) for s in subs]
        rows = [i * tq + s * rs + lax.broadcasted_iota(jnp.int32, (rs, tk), 0) for s in subs]

        def tile(kb, carry, masked):
            m, l, acc = carry
            off = pl.multiple_of(kb * tk, tk)
            k = k_ref[pl.ds(off, tk), :]
            v = v_ref[pl.ds(off, tk), :]
            cj = ct_ref[pl.ds(lane0 % 8 + h, 1), pl.ds(off, tk)]
            sc = [_dot(q, k, NT) * scale + (c_ - cj) for q, c_ in zip(qs, ci)]
            if masked:
                col = kb * tk + lax.broadcasted_iota(jnp.int32, (rs, tk), 1)
                sc = [jnp.where(col <= row, x, -jnp.inf) for x, row in zip(sc, rows)]
            m2 = [jnp.maximum(m_, jnp.max(x, axis=1, keepdims=True)) for m_, x in zip(m, sc)]
            p = [jnp.exp(x - m_) for x, m_ in zip(sc, m2)]
            al = [jnp.exp(a - b) for a, b in zip(m, m2)]
            l = [a * l_ + jnp.sum(p_, axis=1, keepdims=True) for a, l_, p_ in zip(al, l, p)]
            acc = [a * ac + _dot(p_.astype(bf16), v) for a, ac, p_ in zip(al, acc, p)]
            return m2, l, acc

        carry = ([jnp.full((rs, 1), -jnp.inf, f32)] * nsub, [jnp.zeros((rs, 1), f32)] * nsub,
                 [jnp.zeros((rs, HEAD_DIM), f32)] * nsub)
        for d in range(nd):
            carry = tile(i * nd + d, carry, True)
        m, l, acc = lax.fori_loop(0, i * nd, lambda kb, cr: tile(kb, cr, False), carry)
        for s in subs:
            o_ref[pl.ds(s * rs, rs), :] = acc[s] / l[s]
            lse_ref[pl.ds(s * rs, rs), :] = jnp.broadcast_to(m[s] + jnp.log(l[s]), (rs, HEAD_DIM))

    blk = pl.BlockSpec((tq, HEAD_DIM), lambda h, i: (i, h))
    full = pl.BlockSpec((S, HEAD_DIM), lambda h, i: (0, h))
    cspec = pl.BlockSpec((tq, LANES), lambda h, i: (i, 0))
    ctspec = pl.BlockSpec((8, S), lambda h, i: (lane0 // 8, 0))
    sh = SDS((S, nh * HEAD_DIM), f32)
    return _call(body, name, (nh, S // tq), [blk, full, full, cspec, ctspec], (blk, blk), (sh, sh), comm=comm)(fq, fk, fv, c, cT)


def fox_bwd(fq, fk, fv, c, cT, o, lse, do, nh, lane0, name, comm=None):
    S = fq.shape[0]
    tq, tk, rs, nd = _sb_geometry(S, FOX_TK)
    nsub = tq // rs
    scale = HEAD_DIM ** -0.5

    def body(q_ref, k_ref, v_ref, c_ref, ct_ref, o_ref, lse_ref, do_ref, dq_ref, dk_ref, dv_ref, dct_ref):
        h = pl.program_id(0)
        i = pl.program_id(1)

        @pl.when(i == 0)
        def _():
            dk_ref[...] = jnp.zeros_like(dk_ref)
            dv_ref[...] = jnp.zeros_like(dv_ref)

        @pl.when((i == 0) & (h == 0))
        def _():
            dct_ref[...] = jnp.zeros_like(dct_ref)

        subs = range(nsub)
        qs = [q_ref[pl.ds(s * rs, rs), :] for s in subs]
        dof = [do_ref[pl.ds(s * rs, rs), :] for s in subs]
        dos = [x.astype(bf16) for x in dof]
        ci = [_lane_pick(c_ref[pl.ds(s * rs, rs), :], lane0 + h) for s in subs]
        lses = [lse_ref[pl.ds(s * rs, rs), :][:, :1] for s in subs]
        dl = [jnp.sum(x * o_ref[pl.ds(s * rs, rs), :], axis=1, keepdims=True) for s, x in zip(subs, dof)]
        rows = [i * tq + s * rs + lax.broadcasted_iota(jnp.int32, (rs, tk), 0) for s in subs]

        def tile(kb, carry, masked):
            dq, rsum = carry
            off = pl.multiple_of(kb * tk, tk)
            k = k_ref[pl.ds(off, tk), :]
            v = v_ref[pl.ds(off, tk), :]
            cj = ct_ref[pl.ds(lane0 % 8 + h, 1), pl.ds(off, tk)]
            p = [jnp.exp(_dot(q, k, NT) * scale + (c_ - cj) - ls) for q, c_, ls in zip(qs, ci, lses)]
            if masked:
                col = kb * tk + lax.broadcasted_iota(jnp.int32, (rs, tk), 1)
                p = [jnp.where(col <= row, x, 0.0) for x, row in zip(p, rows)]
            ds = [p_ * (_dot(d_, v, NT) - dl_) for p_, d_, dl_ in zip(p, dos, dl)]
            dsb = [(x * scale).astype(bf16) for x in ds]
            dk_ref[pl.ds(off, tk), :] += sum(_dot(x, q, TN) for x, q in zip(dsb, qs))
            dv_ref[pl.ds(off, tk), :] += sum(_dot(p_.astype(bf16), d_, TN) for p_, d_ in zip(p, dos))
            dct_ref[pl.ds(lane0 + h, 1), pl.ds(off, tk)] -= sum(jnp.sum(x, axis=0, keepdims=True) for x in ds)
            return ([g + _dot(x, k) for g, x in zip(dq, dsb)],
                    [r_ + jnp.sum(x, axis=1, keepdims=True) for r_, x in zip(rsum, ds)])

        carry = ([jnp.zeros((rs, HEAD_DIM), f32)] * nsub, [jnp.zeros((rs, 1), f32)] * nsub)
        carry = lax.fori_loop(0, i * nd, lambda kb, cr: tile(kb, cr, False), carry)
        for d in range(nd):
            carry = tile(i * nd + d, carry, True)
        dq, rsum = carry
        for s in subs:
            dq_ref[pl.ds(s * rs, rs), :] = dq[s]
        dct_ref[pl.ds(lane0 + h, 1), pl.ds(pl.multiple_of(i * tq, tq), tq)] += jnp.concatenate([r_.T for r_ in rsum], axis=1)

    blk = pl.BlockSpec((tq, HEAD_DIM), lambda h, i: (i, h))
    full = pl.BlockSpec((S, HEAD_DIM), lambda h, i: (0, h))
    cspec = pl.BlockSpec((tq, LANES), lambda h, i: (i, 0))
    ctspec = pl.BlockSpec((8, S), lambda h, i: (lane0 // 8, 0))
    dctspec = pl.BlockSpec((LANES, S), lambda h, i: (0, 0))
    sh = SDS((S, nh * HEAD_DIM), f32)
    return _call(body, name, (nh, S // tq), [blk, full, full, cspec, ctspec, blk, blk, blk], (blk, full, full, dctspec),
                 (sh, sh, sh, SDS((LANES, S), f32)), comm=comm)(fq, fk, fv, c, cT, o, lse, do)


def gates_fwd(proj, small_cb, fbias_row, name, tm=256):
    S = proj.shape[0]
    tm = _tile(S, tm)

    def body(sm_ref, fb_ref, c_ref, ct_ref, carry_ref):
        @pl.when(pl.program_id(0) == 0)
        def _():
            carry_ref[...] = jnp.zeros_like(carry_ref)

        lf = _logsig(sm_ref[...] + fb_ref[...])
        r = lax.broadcasted_iota(jnp.int32, (tm, tm), 0)
        cc = lax.broadcasted_iota(jnp.int32, (tm, tm), 1)
        c = _dot((r >= cc).astype(f32), lf, NN, HI) + carry_ref[...]
        carry_ref[...] += jnp.sum(lf, axis=0, keepdims=True)
        c_ref[...] = c
        ct_ref[...] = c.T

    return _call(body, name, (S // tm,), [_rowspec(tm, LANES, small_cb), _bspec(LANES)],
                 (_rowspec(tm, LANES), pl.BlockSpec((LANES, tm), lambda i: (0, i))),
                 (SDS((S, LANES), f32), SDS((LANES, S), f32)), scratch=[pltpu.VMEM((1, LANES), f32)])(proj, fbias_row)


def gates_bwd(proj, small_cb, fbias_row, dcT, dsm_dn, lane0, nh, name, tm=256):
    S = proj.shape[0]
    tm = _tile(S, tm)
    nt = S // tm

    def body(sm_ref, fb_ref, dct_ref, dsm_ref, o_ref, dfb_ref, carry_ref):
        @pl.when(pl.program_id(0) == 0)
        def _():
            carry_ref[...] = jnp.zeros_like(carry_ref)
            dfb_ref[...] = jnp.zeros_like(dfb_ref)

        dc = dct_ref[...].T
        r = lax.broadcasted_iota(jnp.int32, (tm, tm), 0)
        cc = lax.broadcasted_iota(jnp.int32, (tm, tm), 1)
        dlf = _dot((r <= cc).astype(f32), dc, NN, HI) + carry_ref[...]
        carry_ref[...] += jnp.sum(dc, axis=0, keepdims=True)
        lane = lax.broadcasted_iota(jnp.int32, (1, LANES), 1)
        dpre = jnp.where((lane >= lane0) & (lane < lane0 + nh), dlf * jax.nn.sigmoid(-(sm_ref[...] + fb_ref[...])), 0.0)
        o_ref[...] = dsm_ref[...] + dpre
        dfb_ref[...] += jnp.sum(dpre, axis=0, keepdims=True)

    rev = lambda i: nt - 1 - i
    return _call(body, name, (nt,),
                 [pl.BlockSpec((tm, LANES), lambda i: (rev(i), small_cb)), _bspec(LANES),
                  pl.BlockSpec((LANES, tm), lambda i: (0, rev(i))), pl.BlockSpec((tm, LANES), lambda i: (rev(i), 0))],
                 (pl.BlockSpec((tm, LANES), lambda i: (rev(i), 0)), _bspec(LANES)),
                 (SDS((S, LANES), f32), SDS((1, LANES), f32)), scratch=[pltpu.VMEM((1, LANES), f32)])(proj, fbias_row, dcT, dsm_dn)


PAD_ROWS = 8


def conv_fwd(proj, w, width, name):
    S = proj.shape[0]
    cw = 256 if width % 256 == 0 else 128

    def body(x_ref, w_ref, y_ref, pad_ref):
        pad_ref[pl.ds(0, PAD_ROWS), :] = jnp.zeros((PAD_ROWS, cw), f32)
        pad_ref[pl.ds(PAD_ROWS, S), :] = x_ref[...]
        y = jnp.zeros((S, cw), f32)
        for i in range(CONV_WIDTH):
            y = y + pad_ref[pl.ds(PAD_ROWS - (CONV_WIDTH - 1) + i, S), :] * w_ref[pl.ds(i, 1), :]
        y_ref[...] = y * jax.nn.sigmoid(y)

    spec = pl.BlockSpec((S, cw), lambda j: (0, j))
    return _call(body, name, (width // cw,), [spec, pl.BlockSpec((CONV_WIDTH, cw), lambda j: (0, j))], spec,
                 SDS((S, width), f32), scratch=[pltpu.VMEM((S + PAD_ROWS, cw), f32)])(proj, w)


def conv_bwd(proj, w, dy, name):
    S, width = dy.shape
    cw = 256 if width % 256 == 0 else 128

    def body(x_ref, w_ref, dy_ref, dx_ref, dw_ref, xpad_ref, dpad_ref):
        xpad_ref[pl.ds(0, PAD_ROWS), :] = jnp.zeros((PAD_ROWS, cw), f32)
        xpad_ref[pl.ds(PAD_ROWS, S), :] = x_ref[...]
        y = jnp.zeros((S, cw), f32)
        for i in range(CONV_WIDTH):
            y = y + xpad_ref[pl.ds(PAD_ROWS - (CONV_WIDTH - 1) + i, S), :] * w_ref[pl.ds(i, 1), :]
        sg = jax.nn.sigmoid(y)
        dpre = dy_ref[...] * (sg * (1.0 + y * (1.0 - sg)))
        dpad_ref[pl.ds(S, PAD_ROWS), :] = jnp.zeros((PAD_ROWS, cw), f32)
        dpad_ref[pl.ds(0, S), :] = dpre
        dx = jnp.zeros((S, cw), f32)
        for i in range(CONV_WIDTH):
            dx = dx + dpad_ref[pl.ds(CONV_WIDTH - 1 - i, S), :] * w_ref[pl.ds(i, 1), :]
            dw_ref[pl.ds(i, 1), :] = jnp.sum(dpre * xpad_ref[pl.ds(PAD_ROWS - (CONV_WIDTH - 1) + i, S), :], axis=0, keepdims=True)
        dx_ref[...] = dx

    spec = pl.BlockSpec((S, cw), lambda j: (0, j))
    wspec = pl.BlockSpec((CONV_WIDTH, cw), lambda j: (0, j))
    return _call(body, name, (width // cw,), [spec, wspec, spec], (spec, wspec),
                 (SDS((S, width), f32), SDS((CONV_WIDTH, width), f32)),
                 scratch=[pltpu.VMEM((S + PAD_ROWS, cw), f32), pltpu.VMEM((S + PAD_ROWS, cw), f32)])(proj, w, dy)


def _pdot_raw(a, b, dims, passes):
    if passes == 1:
        return _dot(a.astype(bf16), b.astype(bf16), dims)
    ah, al = _split_bf16(a)
    bh, bl = _split_bf16(b)
    return _dot(ah, bh, dims) + (_dot(ah, bl, dims) + _dot(al, bh, dims))


def _make_pdot(passes):
    @functools.partial(jax.custom_vjp, nondiff_argnums=(2,))
    def pdot(a, b, mode):
        return _pdot_raw(a, b, {"nn": NN, "nt": NT, "tn": TN}[mode], passes)

    def fwd(a, b, mode):
        return pdot(a, b, mode), (a, b)

    def bwd(mode, res, g):
        a, b = res
        if mode == "nn":
            return _pdot_raw(g, b, NT, passes), _pdot_raw(a, g, TN, passes)
        if mode == "nt":
            return _pdot_raw(g, b, NN, passes), _pdot_raw(g, a, TN, passes)
        return _pdot_raw(b, g, NT, passes), _pdot_raw(a, g, NN, passes)

    pdot.defvjp(fwd, bwd)
    return pdot


_dot1 = _make_pdot(1)
_dot3 = _make_pdot(3)


def _each(f, *lists):
    return [f(*xs) for xs in zip(*lists)]


def _dn_chunk(ndn, cq, ck, cv, small, alog, dtb, s0):
    C = cq[0].shape[0]
    hs = list(range(ndn))
    b = [_lane_pick(small, h) for h in hs]
    d = [_lane_pick(small, ndn + h) for h in hs]
    al = [_lane_pick(alog, h) for h in hs]
    dt = [_lane_pick(dtb, h) for h in hs]
    q = _each(lambda x: x * lax.rsqrt(jnp.sum(x * x, axis=1, keepdims=True) + EPS) * (HEAD_DIM ** -0.5), cq)
    k = _each(lambda x: x * lax.rsqrt(jnp.sum(x * x, axis=1, keepdims=True) + EPS), ck)
    beta = _each(jax.nn.sigmoid, b)
    g = _each(lambda al_, d_, dt_: -jnp.exp(al_) * _softplus(d_ + dt_), al, d, dt)
    r = lax.broadcasted_iota(jnp.int32, (C, C), 0)
    c = lax.broadcasted_iota(jnp.int32, (C, C), 1)
    incl, strict = r >= c, r > c
    inclf = incl.astype(f32)
    gc = _each(lambda g_: _dot3(inclf, g_, "nn"), g)
    decay = _each(lambda gc_: jnp.where(incl, jnp.exp(jnp.where(incl, gc_ - gc_.T, 0.0)), 0.0), gc)
    kb = _each(lambda k_, b_: k_ * b_, k, beta)
    a = _each(lambda kb_, k_, dec: jnp.where(strict, _dot3(kb_, k_, "nt") * dec, 0.0), kb, k, decay)
    eye = (r == c).astype(f32)
    t = _each(lambda a_: eye - a_, a)
    p = a
    for _ in range(int(math.log2(C)) - 1):
        p = _each(lambda p_: _dot3(p_, p_, "nn"), p)
        t = _each(lambda t_, p_: t_ + _dot3(t_, p_, "nn"), t, p)
    eg = _each(jnp.exp, gc)
    u = _each(lambda t_, v_, b_: _dot3(t_, v_ * b_, "nn"), t, cv, beta)
    w = _each(lambda t_, kb_, eg_: _dot3(t_, kb_ * eg_, "nn"), t, kb, eg)
    attn = _each(lambda q_, k_, dec: _dot1(q_, k_, "nt") * dec, q, k, decay)
    g_last = _each(lambda g_: jnp.sum(g_, axis=0, keepdims=True), g)
    v_new = _each(lambda u_, w_, s_: u_ - _dot1(w_, s_, "nn"), u, w, s0)
    o = _each(lambda q_, eg_, s_, at, vn: _dot1(q_ * eg_, s_, "nn") + _dot1(at, vn, "nn"), q, eg, s0, attn, v_new)
    s1 = _each(lambda s_, gl, k_, gc_, vn: s_ * jnp.exp(gl) + _dot1(k_ * jnp.exp(gl - gc_), vn, "tn"), s0, g_last, k, gc, v_new)
    return o, s1


def dn_fwd(cqkv, proj, small_cb, alog_row, dt_row, ndn, name, comm=None):
    S = cqkv.shape[0]
    C = DN_CHUNK
    nc = S // C
    half = ndn * HEAD_DIM

    def body(q_ref, k_ref, v_ref, sm_ref, al_ref, dt_ref, o_ref, ss_ref, s_ref):
        @pl.when(pl.program_id(0) == 0)
        def _():
            s_ref[...] = jnp.zeros_like(s_ref)

        sls = [slice(h * HEAD_DIM, (h + 1) * HEAD_DIM) for h in range(ndn)]
        s0 = [s_ref[h] for h in range(ndn)]
        for h in range(ndn):
            ss_ref[h, 0] = s0[h]
        o, s1 = _dn_chunk(ndn, [q_ref[:, sl] for sl in sls], [k_ref[:, sl] for sl in sls], [v_ref[:, sl] for sl in sls],
                          sm_ref[...], al_ref[...], dt_ref[...], s0)
        for h in range(ndn):
            o_ref[:, sls[h]] = o[h]
            s_ref[h] = s1[h]

    blk = lambda cb: pl.BlockSpec((C, half), lambda n: (n, cb))
    row = pl.BlockSpec((1, LANES), lambda n: (0, 0))
    return _call(body, name, (nc,),
                 [blk(0), blk(1), blk(2), pl.BlockSpec((C, LANES), lambda n: (n, small_cb)), row, row],
                 (blk(0), pl.BlockSpec((ndn, 1, HEAD_DIM, HEAD_DIM), lambda n: (0, n, 0, 0))),
                 (SDS((S, half), f32), SDS((ndn, nc, HEAD_DIM, HEAD_DIM), f32)),
                 scratch=[pltpu.VMEM((ndn, HEAD_DIM, HEAD_DIM), f32)], comm=comm)(cqkv, cqkv, cqkv, proj, alog_row, dt_row)


def dn_bwd(cqkv, proj, small_cb, alog_row, dt_row, ssave, do, ndn, name, comm=None):
    S = cqkv.shape[0]
    C = DN_CHUNK
    nc = S // C
    half = ndn * HEAD_DIM

    def body(q_ref, k_ref, v_ref, sm_ref, al_ref, dt_ref, ss_ref, do_ref, dqkv_ref, dsm_ref, dal_ref, ddt_ref, ds_ref):
        @pl.when(pl.program_id(0) == 0)
        def _():
            ds_ref[...] = jnp.zeros_like(ds_ref)
            dal_ref[...] = jnp.zeros_like(dal_ref)
            ddt_ref[...] = jnp.zeros_like(ddt_ref)

        sls = [slice(h * HEAD_DIM, (h + 1) * HEAD_DIM) for h in range(ndn)]
        _, vjp = jax.vjp(functools.partial(_dn_chunk, ndn), [q_ref[:, sl] for sl in sls], [k_ref[:, sl] for sl in sls],
                         [v_ref[:, sl] for sl in sls], sm_ref[...], al_ref[...], dt_ref[...], [ss_ref[h, 0] for h in range(ndn)])
        dq, dk, dv, dsm, dal, ddt, ds0 = vjp(([do_ref[:, sl] for sl in sls], [ds_ref[h] for h in range(ndn)]))
        for h in range(ndn):
            dqkv_ref[:, sls[h]] = dq[h]
            dqkv_ref[:, half + h * HEAD_DIM:half + (h + 1) * HEAD_DIM] = dk[h]
            dqkv_ref[:, 2 * half + h * HEAD_DIM:2 * half + (h + 1) * HEAD_DIM] = dv[h]
            ds_ref[h] = ds0[h]
        dsm_ref[...] = dsm
        dal_ref[...] += dal
        ddt_ref[...] += ddt

    rev = lambda n: nc - 1 - n
    blk = lambda cb: pl.BlockSpec((C, half), lambda n: (rev(n), cb))
    row = pl.BlockSpec((1, LANES), lambda n: (0, 0))
    return _call(body, name, (nc,),
                 [blk(0), blk(1), blk(2), pl.BlockSpec((C, LANES), lambda n: (rev(n), small_cb)), row, row,
                  pl.BlockSpec((ndn, 1, HEAD_DIM, HEAD_DIM), lambda n: (0, rev(n), 0, 0)), blk(0)],
                 (pl.BlockSpec((C, 3 * half), lambda n: (rev(n), 0)), pl.BlockSpec((C, LANES), lambda n: (rev(n), 0)), row, row),
                 (SDS((S, 3 * half), f32), SDS((S, LANES), f32), SDS((1, LANES), f32), SDS((1, LANES), f32)),
                 scratch=[pltpu.VMEM((ndn, HEAD_DIM, HEAD_DIM), f32)], comm=comm)(cqkv, cqkv, cqkv, proj, alog_row, dt_row, ssave, do)


def even_pre(proj, gq, gk, dfx, cb0, name):
    S = proj.shape[0]
    tm = _tile(S, 256)
    nh = dfx // HEAD_DIM

    def body(q_ref, k_ref, v_ref, gq_ref, gk_ref, fq_ref, fk_ref, fv_ref):
        for h in range(nh):
            sl = slice(h * HEAD_DIM, (h + 1) * HEAD_DIM)
            fq_ref[:, sl] = _rms(q_ref[:, sl], gq_ref[...]).astype(bf16)
            fk_ref[:, sl] = _rms(k_ref[:, sl], gk_ref[...]).astype(bf16)
        fv_ref[...] = v_ref[...].astype(bf16)

    sh = SDS((S, dfx), bf16)
    o = _rowspec(tm, dfx)
    return _call(body, name, (S // tm,),
                 [_rowspec(tm, dfx, cb0), _rowspec(tm, dfx, cb0 + 1), _rowspec(tm, dfx, cb0 + 2), _bspec(HEAD_DIM), _bspec(HEAD_DIM)],
                 (o, o, o), (sh, sh, sh))(proj, proj, proj, gq, gk)


def even_pre_bwd(proj, gq, gk, dfq, dfk, dfx, cb0, name):
    S = proj.shape[0]
    tm = _tile(S, 256)
    nh = dfx // HEAD_DIM

    def body(q_ref, k_ref, gq_ref, gk_ref, dfq_ref, dfk_ref, dq_ref, dk_ref, dgq_ref, dgk_ref):
        @pl.when(pl.program_id(0) == 0)
        def _():
            dgq_ref[...] = jnp.zeros_like(dgq_ref)
            dgk_ref[...] = jnp.zeros_like(dgk_ref)

        for h in range(nh):
            sl = slice(h * HEAD_DIM, (h + 1) * HEAD_DIM)
            _, vq = jax.vjp(_rms, q_ref[:, sl], gq_ref[...])
            dq, dgq = vq(dfq_ref[:, sl])
            _, vk = jax.vjp(_rms, k_ref[:, sl], gk_ref[...])
            dk, dgk = vk(dfk_ref[:, sl])
            dq_ref[:, sl] = dq
            dk_ref[:, sl] = dk
            dgq_ref[...] += dgq
            dgk_ref[...] += dgk

    sh = SDS((S, dfx), f32)
    o = _rowspec(tm, dfx)
    g = SDS((1, HEAD_DIM), f32)
    return _call(body, name, (S // tm,),
                 [_rowspec(tm, dfx, cb0), _rowspec(tm, dfx, cb0 + 1), _bspec(HEAD_DIM), _bspec(HEAD_DIM), o, o],
                 (o, o, _bspec(HEAD_DIM), _bspec(HEAD_DIM)), (sh, sh, g, g))(proj, proj, gq, gk, dfq, dfk)


def _dn_out(o, gate, gn):
    return _rms(o, gn) * (gate * jax.nn.sigmoid(gate))


def _fox_out(o, gate):
    return o * jax.nn.sigmoid(gate)


def even_post(o_dn, o_fox, proj, gn, half, cb_dn_gate, cb_fox_gate, name):
    S = proj.shape[0]
    tm = _tile(S, 256)
    nh = half // HEAD_DIM

    def body(od_ref, of_ref, gd_ref, gf_ref, gn_ref, o_ref):
        for h in range(nh):
            sl = slice(h * HEAD_DIM, (h + 1) * HEAD_DIM)
            o_ref[:, sl] = _dn_out(od_ref[:, sl], gd_ref[:, sl], gn_ref[...]).astype(bf16)
        o_ref[:, half:] = _fox_out(of_ref[...], gf_ref[...]).astype(bf16)

    return _call(body, name, (S // tm,),
                 [_rowspec(tm, half), _rowspec(tm, half), _rowspec(tm, half, cb_dn_gate), _rowspec(tm, half, cb_fox_gate), _bspec(HEAD_DIM)],
                 _rowspec(tm, 2 * half), SDS((S, 2 * half), bf16))(o_dn, o_fox, proj, proj, gn)


def even_post_bwd(o_dn, o_fox, proj, gn, dom, half, cb_dn_gate, cb_fox_gate, name):
    S = proj.shape[0]
    tm = _tile(S, 256)
    nh = half // HEAD_DIM

    def body(od_ref, of_ref, gd_ref, gf_ref, gn_ref, dod_ref, dof_ref, dd_ref, df_ref, dgd_ref, dgf_ref, dgn_ref):
        @pl.when(pl.program_id(0) == 0)
        def _():
            dgn_ref[...] = jnp.zeros_like(dgn_ref)

        for h in range(nh):
            sl = slice(h * HEAD_DIM, (h + 1) * HEAD_DIM)
            _, vjp = jax.vjp(_dn_out, od_ref[:, sl], gd_ref[:, sl], gn_ref[...])
            d_o, d_gate, d_gn = vjp(dod_ref[:, sl])
            dd_ref[:, sl] = d_o
            dgd_ref[:, sl] = d_gate
            dgn_ref[...] += d_gn
        _, vjp = jax.vjp(_fox_out, of_ref[...], gf_ref[...])
        d_o, d_gate = vjp(dof_ref[...])
        df_ref[...] = d_o
        dgf_ref[...] = d_gate

    sh = SDS((S, half), f32)
    o = _rowspec(tm, half)
    return _call(body, name, (S // tm,),
                 [o, o, _rowspec(tm, half, cb_dn_gate), _rowspec(tm, half, cb_fox_gate), _bspec(HEAD_DIM),
                  _rowspec(tm, half, 0), _rowspec(tm, half, 1)],
                 (o, o, o, o, _bspec(HEAD_DIM)), (sh, sh, sh, sh, SDS((1, HEAD_DIM), f32)))(o_dn, o_fox, proj, proj, gn, dom, dom)


def _pad_row(v, lane0=0):
    return jnp.pad(v, (lane0, LANES - lane0 - v.shape[0]))[None]


def _carried(res, comm):
    return res if comm is not None else (res, [])


def even_mixer_fwd(x, gmix, w_in, w_out, conv_w, a_log, dt_bias, gn, gq, gk, f_bias, tag, comm_fox=None, comm_dn=None,
                   comm_in=None):
    S, D = x.shape
    half = D // 2
    ndn = half // HEAD_DIM
    small_cb = 4 * D // LANES
    alog_row, dt_row, fb_row = _pad_row(a_log), _pad_row(dt_bias), _pad_row(f_bias, 2 * ndn)
    h = rms_fwd(x, gmix, f"{tag}_rms")
    proj, got_in = _carried(mm(h, w_in, "nn", f32, f"{tag}_in", tm=512, tn=1408, comm=comm_in), comm_in)
    cqkv = conv_fwd(proj, conv_w, 3 * half, f"{tag}_conv")
    (o_dn, ssave), got_dn = _carried(dn_fwd(cqkv, proj, small_cb, alog_row, dt_row, ndn, f"{tag}_dn", comm=comm_dn), comm_dn)
    c, cT = gates_fwd(proj, small_cb, fb_row, f"{tag}_gates")
    fq, fk, fv = even_pre(proj, gq, gk, half, 4, f"{tag}_pre")
    (o_fox, lse), got_fox = _carried(fox_fwd(fq, fk, fv, c, cT, ndn, 2 * ndn, f"{tag}_fox", comm=comm_fox), comm_fox)
    om = even_post(o_dn, o_fox, proj, gn, half, 3, 7, f"{tag}_post")
    xo = mm(om, w_out, "nn", f32, f"{tag}_out", res=x)
    return xo, (h, proj, cqkv, o_dn, ssave, c, cT, fq, fk, fv, o_fox, lse, om, alog_row, dt_row, fb_row), got_fox, got_dn, got_in


def even_mixer_bwd(x, gmix, w_in, w_out, conv_w, gn, gq, gk, saved, dy, tag, comm_fox=None, comm_dn=None, comm_bdh=None):
    S, D = x.shape
    half = D // 2
    ndn = half // HEAD_DIM
    small_cb = 4 * D // LANES
    h, proj, cqkv, o_dn, ssave, c, cT, fq, fk, fv, o_fox, lse, om, alog_row, dt_row, fb_row = saved
    dy32, dy16 = dy
    dom = mm(dy16, w_out, "nt", f32, f"{tag}_bdom")
    dw_out = mm(om, dy16, "tn", bf16, f"{tag}_bwout")
    d_odn, d_ofox, d_dgate, d_fgate, d_gn = even_post_bwd(o_dn, o_fox, proj, gn, dom, half, 3, 7, f"{tag}_bpost")
    (dfq, dfk, dfv, dcT), got_fox = _carried(
        fox_bwd(fq, fk, fv, c, cT, o_fox, lse, d_ofox, ndn, 2 * ndn, f"{tag}_bfox", comm=comm_fox), comm_fox)
    dq_pre, dk_pre, d_gq, d_gk = even_pre_bwd(proj, gq, gk, dfq, dfk, half, 4, f"{tag}_bpre")
    (dcqkv, dsm_dn, d_alog, d_dt), got_dn = _carried(
        dn_bwd(cqkv, proj, small_cb, alog_row, dt_row, ssave, d_odn, ndn, f"{tag}_bdn", comm=comm_dn), comm_dn)
    d_conv_in, d_conv_w = conv_bwd(proj, conv_w, dcqkv, f"{tag}_bconv")
    d_small, d_fb = gates_bwd(proj, small_cb, fb_row, dcT, dsm_dn, 2 * ndn, ndn, f"{tag}_bgates")
    dproj = jnp.concatenate([d_conv_in, d_dgate, dq_pre, dk_pre, dfv, d_fgate, d_small], axis=1).astype(bf16)
    dw_in = mm(h, dproj, "tn", bf16, f"{tag}_bwin", tn=1408)
    res = mm_bdh_rms(dproj, w_in, x, gmix, dy32, f"{tag}_bdh", comm=comm_bdh)
    dx, d_gmix = res[:2]
    got_bdh = res[2] if comm_bdh is not None else []
    small = dict(norm_mix=d_gmix[0], dn_conv_w=d_conv_w, dn_a_log=d_alog[0, :ndn], dn_dt_bias=d_dt[0, :ndn],
                 dn_norm_g=d_gn[0], fox_q_norm_g=d_gq[0], fox_k_norm_g=d_gk[0], fox_f_bias=d_fb[0, 2 * ndn:3 * ndn])
    return dx, dw_in, dw_out, small, got_fox, got_dn, got_bdh


def odd_mixer_fwd(x, gmix, w_in, w_out, tag, comm=None):
    S, D = x.shape
    nh = D // HEAD_DIM
    h = rms_fwd(x, gmix, f"{tag}_rms")
    qkv = mm(h, w_in, "nn", bf16, f"{tag}_in", tn=768)
    (o, tails), got = _carried(sb_fwd(qkv, nh, f"{tag}_sb", comm=comm), comm)
    xo = mm(o, w_out, "nn", f32, f"{tag}_out", res=x)
    return xo, (h, qkv, o, tails), got


def odd_mixer_bwd(x, gmix, w_in, w_out, saved, dy, tag, comm=None):
    S, D = x.shape
    nh = D // HEAD_DIM
    h, qkv, o, tails = saved
    dy32, dy16 = dy
    do = mm(dy16, w_out, "nt", f32, f"{tag}_bdo")
    dw_out = mm(o, dy16, "tn", bf16, f"{tag}_bwout")
    (dq, dk, dv), got = _carried(sb_bwd(qkv, tails, do, nh, f"{tag}_bsb", comm=comm), comm)
    dqkv = jnp.concatenate([dq, dk, dv], axis=1).astype(bf16)
    dw_in = mm(h, dqkv, "tn", bf16, f"{tag}_bwin", tn=1536)
    dx, d_gmix = mm_bdh_rms(dqkv, w_in, x, gmix, dy32, f"{tag}_bdh")
    return dx, dw_in, dw_out, dict(norm_mix=d_gmix[0]), got


def all_gather(shards, axes, name, kind="ag"):
    return _call(None, name, (), [], [], [], comm=Comm(kind, shards, axes))()[1]


def sum_parts(parts, name):
    _, R, C = parts.shape
    tm = _tile(R, 256)

    def body(p_ref, o_ref):
        acc = p_ref[0].astype(f32)
        for k in range(1, NDEV):
            acc = acc + p_ref[k].astype(f32)
        o_ref[...] = acc

    return _call(body, name, (R // tm,), [pl.BlockSpec((NDEV, tm, C), lambda i: (0, i, 0))], _rowspec(tm, C), SDS((R, C), f32))(parts)


def adamw(w, g, m, v, name):
    L, R, C = w.shape
    tm = _tile(R, max(8, min(512, (ADAMW_TILE_ELEMS // C) // 8 * 8)))
    c1 = 1.0 - ADAM_B1 ** ADAM_STEP
    c2 = 1.0 - ADAM_B2 ** ADAM_STEP

    def body(w_ref, g_ref, m_ref, v_ref, d_ref, nm_ref, nv_ref):
        g_ = g_ref[...]
        nm = ADAM_B1 * m_ref[...] + (1.0 - ADAM_B1) * g_
        nv = ADAM_B2 * v_ref[...] + (1.0 - ADAM_B2) * (g_ * g_)
        d_ref[...] = -ADAM_LR * ((nm / c1) / (jnp.sqrt(nv / c2) + ADAM_EPS) + ADAM_WD * w_ref[...])
        nm_ref[...] = nm
        nv_ref[...] = nv

    spec = pl.BlockSpec((None, tm, C), lambda l, i: (l, i, 0))
    sh = SDS((L, R, C), f32)
    return _call(body, name, (L, R // tm), [spec] * 4, (spec, spec, spec), (sh, sh, sh))(w, g, m, v)


def _pad_to(n, mult):
    return -(-n // mult) * mult


def _even_perm(D):
    half = D // 2
    ndn = half // HEAD_DIM
    o_gate = 3 * half
    o_b = o_gate + half
    o_a = o_b + ndn
    o_fq = o_a + ndn
    o_fpre = o_fq + 4 * half
    return half, ndn, o_b, o_a, o_fq, o_fpre


def _even_to_mine(w):
    D = (w.shape[-1] // 4) // LANES * LANES
    half, ndn, o_b, o_a, o_fq, o_fpre = _even_perm(D)
    small = jnp.concatenate([w[..., o_b:o_b + ndn], w[..., o_a:o_a + ndn], w[..., o_fpre:o_fpre + ndn]], axis=-1)
    small = jnp.pad(small, [(0, 0)] * (w.ndim - 1) + [(0, LANES - 3 * ndn)])
    return jnp.concatenate([w[..., :o_b], w[..., o_fq:o_fpre], small], axis=-1)


def _even_from_mine(w, D):
    half, ndn, o_b, o_a, o_fq, o_fpre = _even_perm(D)
    sm = w[..., 4 * D:]
    return jnp.concatenate([w[..., :o_b], sm[..., :ndn], sm[..., ndn:2 * ndn], w[..., o_b:4 * D], sm[..., 2 * ndn:3 * ndn]], axis=-1)


def _pack_small(parts):
    flat = jnp.concatenate([p.reshape(-1).astype(f32) for p in parts])
    n = flat.shape[0]
    return jnp.pad(flat, (0, _pad_to(n, 8 * LANES) - n)).reshape(-1, LANES)


def _unpack_small(packed, like):
    flat = packed.reshape(-1)
    out, off = [], 0
    for p in like:
        out.append(flat[off:off + p.size].reshape(p.shape))
        off += p.size
    return out


SMALL_NAMES = ("norm_ffn1", "norm_mix", "dn_a_log", "dn_dt_bias", "dn_norm_g", "fox_q_norm_g", "fox_k_norm_g", "fox_f_bias", "norm_ffn2")
BIG_NAMES = ("ffn1_w_gu", "ffn1_w_down", "w_in_even", "dn_conv_w", "w_out_even", "w_in_odd", "w_out_odd", "ffn2_w_gu", "ffn2_w_down")
WEIGHT_NAMES = ("norm_ffn1", "ffn1_w_gu", "ffn1_w_down", "norm_mix", "w_in_even", "dn_conv_w", "dn_a_log", "dn_dt_bias", "dn_norm_g",
                "fox_q_norm_g", "fox_k_norm_g", "fox_f_bias", "w_out_even", "w_in_odd", "w_out_odd", "norm_ffn2", "ffn2_w_gu", "ffn2_w_down")


def kernel(x, norm_ffn1, ffn1_w_gu, ffn1_w_down, norm_mix, w_in_even, dn_conv_w, dn_a_log, dn_dt_bias, dn_norm_g, fox_q_norm_g, fox_k_norm_g, fox_f_bias, w_out_even, w_in_odd, w_out_odd, norm_ffn2, ffn2_w_gu, ffn2_w_down, loss_target, m_norm_ffn1, m_ffn1_w_gu, m_ffn1_w_down, m_norm_mix, m_w_in_even, m_dn_conv_w, m_dn_a_log, m_dn_dt_bias, m_dn_norm_g, m_fox_q_norm_g, m_fox_k_norm_g, m_fox_f_bias, m_w_out_even, m_w_in_odd, m_w_out_odd, m_norm_ffn2, m_ffn2_w_gu, m_ffn2_w_down, v_norm_ffn1, v_ffn1_w_gu, v_ffn1_w_down, v_norm_mix, v_w_in_even, v_dn_conv_w, v_dn_a_log, v_dn_dt_bias, v_dn_norm_g, v_fox_q_norm_g, v_fox_k_norm_g, v_fox_f_bias, v_w_out_even, v_w_in_odd, v_w_out_odd, v_norm_ffn2, v_ffn2_w_gu, v_ffn2_w_down):
    args = dict(locals())
    W = {n: args[n] for n in WEIGHT_NAMES}
    M = {n: args["m_" + n] for n in WEIGHT_NAMES}
    V = {n: args["v_" + n] for n in WEIGHT_NAMES}
    xs = x[0]
    tgt = loss_target[0]
    S, D = xs.shape
    L = norm_ffn1.shape[0]
    n_even = w_in_even.shape[0]
    hs = ffn1_w_down.shape[1]
    hp = _pad_to(hs, LANES)
    me = 4 * lax.axis_index("x") + 2 * lax.axis_index("y") + lax.axis_index("c")

    def prep_gu(w):
        w = w.reshape(L, D, 2, hs)
        return jnp.pad(w, ((0, 0), (0, 0), (0, 0), (0, hp - hs))).reshape(L, D, 2 * hp).astype(bf16)

    def prep_down(w):
        return jnp.pad(w, ((0, 0), (0, hp - hs), (0, 0))).astype(bf16)

    sh_gu1, sh_d1, sh_gu2, sh_d2 = prep_gu(ffn1_w_gu), prep_down(ffn1_w_down), prep_gu(ffn2_w_gu), prep_down(ffn2_w_down)
    sh_ie, sh_oe = _even_to_mine(w_in_even).astype(bf16), w_out_even.astype(bf16)
    sh_io, sh_oo = w_in_odd.astype(bf16), w_out_odd.astype(bf16)

    def layer_shards(l):
        j = l // 2
        mix = [sh_ie[j], sh_oe[j]] if l % 2 == 0 else [sh_io[j], sh_oo[j]]
        return [sh_gu1[l], sh_d1[l], *mix, sh_gu2[l], sh_d2[l]]

    def layer_axes(l):
        return [1, 0, 0 if l % 2 == 0 else 1, 0, 1, 0]

    def layer_names(l):
        return ["ffn1_w_gu", "ffn1_w_down", *(["w_in_even", "w_out_even"] if l % 2 == 0 else ["w_in_odd", "w_out_odd"]),
                "ffn2_w_gu", "ffn2_w_down"]

    FOX_CARRIES, DN_CARRIES = (0, 4, 3), (1, 2, 5)

    def split_comm(kind, arrays, axes):
        return (Comm(kind, [arrays[i] for i in FOX_CARRIES], [axes[i] for i in FOX_CARRIES]),
                Comm(kind, [arrays[i] for i in DN_CARRIES], [axes[i] for i in DN_CARRIES]))

    def join_split(got_fox, got_dn):
        out = [None] * 6
        for i, a in zip(FOX_CARRIES, got_fox):
            out[i] = a
        for i, a in zip(DN_CARRIES, got_dn):
            out[i] = a
        return out

    sh0, ax0 = layer_shards(0), layer_axes(0)
    first = all_gather(sh0[:2] + [dn_conv_w[None]], ax0[:2] + [0], "ag_layer0", kind="ag2")
    weights = {0: first[:2] + [None] * 4}
    convw = jnp.moveaxis(first[2], 0, 2).reshape(n_even, CONV_WIDTH, -1)
    LAYER0_CARRIES = dict(f1_gu=(2,), f1_down=(3,), mx_in=(5,))

    EVEN_FWD_CARRIES = dict(f1_gu=(), mx_in=(), dn=(1, 2, 5), fox=(0, 4, 3), f2_gu=(), f2_down=())
    saved = []
    cur = xs
    for l in range(L):
        j = l // 2
        wgu1, wd1, win, wout, wgu2, wd2 = weights[l]
        nxt = l + 1 < L
        x0 = cur
        if l % 2 == 0:
            sh_n, ax_n = (layer_shards(l + 1), layer_axes(l + 1)) if nxt else (None, None)
            cm = {k: (Comm("ag2", [sh_n[i] for i in idx], [ax_n[i] for i in idx]) if nxt and idx else None)
                  for k, idx in EVEN_FWD_CARRIES.items()}
            if l == 0:
                cm.update({k: Comm("ag2", [sh0[i] for i in idx], [ax0[i] for i in idx]) for k, idx in LAYER0_CARRIES.items()})
                cm["fox"] = Comm("ag2", cm["fox"].arrays + [sh0[4]], cm["fox"].axes + [ax0[4]])
            x1, s1, got_f1gu, got_f1down = ffn_fwd(x0, norm_ffn1[l:l + 1], wgu1, wd1, f"l{l}f1", comm_gu=cm["f1_gu"],
                                                   comm_down=cm["f1_down"] if l == 0 else None)
            if l == 0:
                win, wout = got_f1gu[0], got_f1down[0]
            x2, sm, got_f, got_d, got_in = even_mixer_fwd(
                x1, norm_mix[l:l + 1], win, wout, convw[j], dn_a_log[j], dn_dt_bias[j], dn_norm_g[j:j + 1],
                fox_q_norm_g[j:j + 1], fox_k_norm_g[j:j + 1], fox_f_bias[j], f"l{l}mx", comm_fox=cm["fox"], comm_dn=cm["dn"],
                comm_in=cm["mx_in"])
            if l == 0:
                wd2, wgu2 = got_in[0], got_f[-1]
                weights[0] = [wgu1, wd1, win, wout, wgu2, wd2]
                got_f1gu, got_in, got_f = [], [], got_f[:-1]
            x3, s2, got_f2gu, got_f2down = ffn_fwd(x2, norm_ffn2[l:l + 1], wgu2, wd2, f"l{l}f2", comm_gu=cm["f2_gu"],
                                                   comm_down=cm["f2_down"])
            if nxt:
                got = dict(f1_gu=got_f1gu, mx_in=got_in, dn=got_d, fox=got_f, f2_gu=got_f2gu, f2_down=got_f2down)
                weights[l + 1] = [None] * 6
                for k, idx in EVEN_FWD_CARRIES.items():
                    for i, a in zip(idx, got[k]):
                        weights[l + 1][i] = a
        else:
            x1, s1, _, _ = ffn_fwd(x0, norm_ffn1[l:l + 1], wgu1, wd1, f"l{l}f1")
            cm = Comm("ag2", layer_shards(l + 1), layer_axes(l + 1)) if nxt else None
            x2, sm, got = odd_mixer_fwd(x1, norm_mix[l:l + 1], win, wout, f"l{l}mx", comm=cm)
            if nxt:
                weights[l + 1] = got
            x3, s2, _, _ = ffn_fwd(x2, norm_ffn2[l:l + 1], wgu2, wd2, f"l{l}f2")
        saved.append((x0, x1, x2, s1, sm, s2))
        cur = x3
    dy, loss_row = loss_head(cur, tgt, "loss")

    gsmall = {n: [None] * W[n].shape[0] for n in SMALL_NAMES}
    gconv = [None] * n_even
    parts = {n: [None] * W[n].shape[0] for n in BIG_NAMES if n != "dn_conv_w"}

    def take(l, recv):
        for nm, p in zip(layer_names(l), recv):
            idx = l if nm.startswith("ffn") else l // 2
            parts[nm][idx] = sum_parts(p.reshape(NDEV, -1, p.shape[-1]), f"sum_{nm}_{idx}").reshape(p.shape[1:])

    pending = None
    for l in reversed(range(L)):
        j = l // 2
        wgu1, wd1, win, wout, wgu2, wd2 = weights[l]
        x0, x1, x2, s1, sm, s2 = saved[l]
        dy, dg, dwgu2, dwd2 = ffn_bwd(x2, norm_ffn2[l:l + 1], wgu2, wd2, s2, dy, f"l{l}f2")
        gsmall["norm_ffn2"][l] = dg[0]
        if l % 2 == 0:
            cf, cd = split_comm("rs", pending, layer_axes(l + 1)) if pending is not None else (None, None)
            cb = None
            if l == 0:
                cd = Comm("rs", cd.arrays + [dwd2], cd.axes + [ax0[5]])
                cb = Comm("rs", [dwgu2], [ax0[4]])
            dy, dwin, dwout, sg, got_f, got_d, got_b = even_mixer_bwd(
                x1, norm_mix[l:l + 1], win, wout, convw[j], dn_norm_g[j:j + 1], fox_q_norm_g[j:j + 1],
                fox_k_norm_g[j:j + 1], sm, dy, f"l{l}mx", comm_fox=cf, comm_dn=cd, comm_bdh=cb)
            if l == 0:
                got_wd2, got_wgu2, got_d = got_d[-1], got_b[0], got_d[:-1]
            if pending is not None:
                take(l + 1, join_split(got_f, got_d))
            gconv[j] = sg.pop("dn_conv_w")
            for k_, v_ in sg.items():
                gsmall[k_][l if k_ == "norm_mix" else j] = v_
        else:
            cm = Comm("rs", pending, layer_axes(l + 1)) if pending is not None else None
            dy, dwin, dwout, sg, got = odd_mixer_bwd(x1, norm_mix[l:l + 1], win, wout, sm, dy, f"l{l}mx", comm=cm)
            if pending is not None:
                take(l + 1, got)
            gsmall["norm_mix"][l] = sg["norm_mix"]
        if l > 0:
            dy, dg, dwgu1, dwd1 = ffn_bwd(x0, norm_ffn1[l:l + 1], wgu1, wd1, s1, dy, f"l{l}f1")
        else:
            rs = lambda a, ax: Comm("rs", [a], [ax])
            dy, dg, dwgu1, dwd1, got0 = ffn_bwd(x0, norm_ffn1[l:l + 1], wgu1, wd1, s1, dy, f"l{l}f1",
                                                comms=dict(bda=rs(dwin, ax0[2]), bwd=rs(dwout, ax0[3])), own_axes=(ax0[0], ax0[1]))
        gsmall["norm_ffn1"][l] = dg[0]
        pending = [dwgu1, dwd1, dwin, dwout, dwgu2, dwd2]
    take(0, [got0["wgu"][0], got0["wd"][0], got0["bda"][0], got0["bwd"][0], got_wgu2, got_wd2])
    grad_x = dy[0][None]

    small_list = [jnp.stack(gsmall[n]) for n in SMALL_NAMES] + [jnp.stack(gconv), loss_row[0, :1]]
    packed = _pack_small(small_list)
    gathered = all_gather([packed], [0], "ag_small")[0]
    summed = sum_parts(gathered.reshape(NDEV, packed.shape[0], LANES), "sum_small")
    small_sum = _unpack_small(summed, small_list)
    G = dict(zip(SMALL_NAMES, small_sum[:len(SMALL_NAMES)]))
    conv_full = small_sum[len(SMALL_NAMES)]
    cwid = dn_conv_w.shape[2]
    G["dn_conv_w"] = lax.dynamic_slice_in_dim(conv_full, me * cwid, cwid, axis=2)
    loss = small_sum[-1][0]

    def unprep_gu(g):
        return g.reshape(L, D, 2, hp)[..., :hs].reshape(L, D, 2 * hs)

    G["ffn1_w_gu"] = unprep_gu(jnp.stack(parts["ffn1_w_gu"]))
    G["ffn2_w_gu"] = unprep_gu(jnp.stack(parts["ffn2_w_gu"]))
    G["ffn1_w_down"] = jnp.stack(parts["ffn1_w_down"])[:, :hs]
    G["ffn2_w_down"] = jnp.stack(parts["ffn2_w_down"])[:, :hs]
    G["w_in_even"] = _even_from_mine(jnp.stack(parts["w_in_even"]), D)
    G["w_out_even"] = jnp.stack(parts["w_out_even"])
    G["w_in_odd"] = jnp.stack(parts["w_in_odd"])
    G["w_out_odd"] = jnp.stack(parts["w_out_odd"])

    delta, new_m, new_v = {}, {}, {}
    for n in BIG_NAMES:
        t = (lambda a: jnp.swapaxes(a, 1, 2)) if n.endswith("w_gu") else (lambda a: a)
        delta[n], new_m[n], new_v[n] = map(t, adamw(t(W[n]), t(G[n]), t(M[n]), t(V[n]), f"adamw_{n}"))
    sw = [W[n] for n in SMALL_NAMES]
    d_, m_, v_ = adamw(_pack_small(sw)[None], _pack_small([G[n] for n in SMALL_NAMES])[None],
                       _pack_small([M[n] for n in SMALL_NAMES])[None], _pack_small([V[n] for n in SMALL_NAMES])[None], "adamw_small")
    for n, a, b, c_ in zip(SMALL_NAMES, _unpack_small(d_, sw), _unpack_small(m_, sw), _unpack_small(v_, sw)):
        delta[n], new_m[n], new_v[n] = a, b, c_

    return (loss, grad_x, *[G[n] for n in WEIGHT_NAMES], *[delta[n] for n in WEIGHT_NAMES],
            *[new_m[n] for n in WEIGHT_NAMES], *[new_v[n] for n in WEIGHT_NAMES])
```

```python
import functools
import math

import jax
import jax.numpy as jnp
from jax import lax
from jax.experimental import pallas as pl
from jax.experimental.pallas import tpu as pltpu

f32 = jnp.float32
bf16 = jnp.bfloat16
SDS = jax.ShapeDtypeStruct

HEAD_DIM = 128
LANES = 128
CONV_WIDTH = 4
DN_CHUNK = 128
EPS = 1e-6
NDEV = 8
VMEM_LIMIT_BYTES = 56 * 1024 * 1024
HI = lax.Precision.HIGHEST
ADAMW_TILE_ELEMS = 384 * 1024

ADAM_LR = 0.001
ADAM_B1 = 0.9
ADAM_B2 = 0.999
ADAM_EPS = 1e-08
ADAM_WD = 0.01
ADAM_STEP = 10

NN = (((1,), (0,)), ((), ()))
NT = (((1,), (1,)), ((), ()))
TN = (((0,), (0,)), ((), ()))


def _me_and_peers():
    x, y, c = lax.axis_index("x"), lax.axis_index("y"), lax.axis_index("c")
    me = 4 * x + 2 * y + c
    peers = []
    for r in range(1, NDEV):
        px = 1 - x if (r >> 2) & 1 else x
        py = 1 - y if (r >> 1) & 1 else y
        pc = 1 - c if r & 1 else c
        peers.append(((px, py, pc), 4 * px + 2 * py + pc))
    return me, peers


def _slab(ref, axis, width, k):
    idx = [slice(None)] * len(ref.shape)
    idx[axis] = pl.ds(k * width, width)
    return ref.at[tuple(idx)]


class Comm:
    def __init__(self, kind, arrays, axes):
        self.kind, self.arrays, self.axes, self.n = kind, list(arrays), list(axes), len(arrays)

    def out_shape(self):
        out = []
        for a, ax in zip(self.arrays, self.axes):
            shp = list(a.shape)
            if self.kind != "rs":
                shp[ax] *= NDEV
                out.append(SDS(tuple(shp), a.dtype))
            else:
                shp[ax] //= NDEV
                out.append(SDS((NDEV, *shp), a.dtype))
        return out

    def sems(self):
        return [pltpu.SemaphoreType.DMA((self.n, NDEV - 1)), pltpu.SemaphoreType.DMA((self.n, NDEV - 1)),
                pltpu.SemaphoreType.DMA((self.n,))]

    def _src(self, ins, t, to_idx):
        if self.kind == "ag":
            return ins[t]
        return _slab(ins[t], self.axes[t], ins[t].shape[self.axes[t]] // NDEV, to_idx)

    def _dst(self, ins, outs, t, from_idx):
        if self.kind != "rs":
            return _slab(outs[t], self.axes[t], ins[t].shape[self.axes[t]], from_idx)
        return outs[t].at[from_idx]

    def _copies(self, ins, outs, sems, sending):
        send_sems, recv_sems, loc_sems = sems
        me, peers = _me_and_peers()
        local, remote = [], []
        for t in range(self.n):
            local.append(pltpu.make_async_copy(self._src(ins, t, me), self._dst(ins, outs, t, me), loc_sems.at[t]))
            for r, (peer, pidx) in enumerate(peers):
                remote.append(pltpu.make_async_remote_copy(
                    src_ref=self._src(ins, t, pidx), dst_ref=self._dst(ins, outs, t, me if sending else pidx),
                    send_sem=send_sems.at[t, r], recv_sem=recv_sems.at[t, r], device_id=peer,
                    device_id_type=pl.DeviceIdType.MESH))
        return local, remote

    def _two_level(self, ins, outs, sems, own=True):
        send_sems, recv_sems, loc_sems = sems
        x, y, c = lax.axis_index("x"), lax.axis_index("y"), lax.axis_index("c")
        me, sibling = (x, y, c), (x, y, 1 - c)
        chips = [(1 - x, y), (x, 1 - y), (1 - x, 1 - y)]
        dev = lambda p: 4 * p[0] + 2 * p[1] + p[2]

        def copy(t, k, block, to, from_shard):
            dst = self._dst(ins, outs, t, dev(block))
            return pltpu.make_async_remote_copy(
                src_ref=ins[t] if from_shard else dst, dst_ref=dst, send_sem=send_sems.at[t, k], recv_sem=recv_sems.at[t, k],
                device_id=to, device_id_type=pl.DeviceIdType.MESH)

        ts = range(self.n) if own else ()
        local = [pltpu.make_async_copy(ins[t], self._dst(ins, outs, t, dev(me)), loc_sems.at[t]) for t in ts]
        first = [copy(t, 0, me, sibling, True) for t in ts]
        first += [copy(t, 1 + j, me, (*chip, c), True) for t in ts for j, chip in enumerate(chips)]
        return local, first, copy, chips, me, sibling, c

    def start(self, ins, outs, sems):
        if self.kind == "ag2":
            local, first = self._two_level(ins, outs, sems)[:2]
            for cp in local + first:
                cp.start()
            return
        local, remote = self._copies(ins, outs, sems, True)
        for cp in local + remote:
            cp.start()

    def relay(self, ins, outs, sems):
        _, _, copy, chips, me, sibling, c = self._two_level(ins, outs, sems, own=False)
        for t in range(self.n):
            for j, chip in enumerate(chips):
                copy(t, 1 + j, (*chip, c), me, False).wait_recv()
                copy(t, 4 + j, (*chip, c), sibling, False).start()

    def wait(self, ins, outs, sems, relayed=False):
        if self.kind == "ag2":
            if not relayed:
                self.relay(ins, outs, sems)
            local, first, copy, chips, me, sibling, c = self._two_level(ins, outs, sems)
            passed = [copy(t, 4 + j, (*chip, c), sibling, False) for t in range(self.n) for j, chip in enumerate(chips)]
            for t in range(self.n):
                copy(t, 0, sibling, me, False).wait_recv()
                for j, chip in enumerate(chips):
                    copy(t, 4 + j, (*chip, 1 - c), me, False).wait_recv()
            for cp in first + passed:
                cp.wait_send()
            for cp in local:
                cp.wait()
            return
        local, remote = self._copies(ins, outs, sems, False)
        for cp in remote:
            cp.wait_recv()
            cp.wait_send()
        for cp in local:
            cp.wait()


def _call(body, name, grid, in_specs, out_specs, out_shape, scratch=(), comm=None):
    params = pltpu.CompilerParams(vmem_limit_bytes=VMEM_LIMIT_BYTES)
    if comm is None:
        return pl.pallas_call(body, name=name, grid=grid, in_specs=in_specs, out_specs=out_specs, out_shape=out_shape,
                              scratch_shapes=list(scratch), compiler_params=params)
    single = not isinstance(out_shape, (tuple, list))
    c_out_specs = [out_specs] if single else list(out_specs)
    c_out_shape = [out_shape] if single else list(out_shape)
    n_in, n_out, n_scr, n = len(in_specs), len(c_out_shape), len(scratch), comm.n
    anyspec = pl.BlockSpec(memory_space=pl.ANY)

    def wrapped(*refs):
        ins, refs = refs[:n_in], refs[n_in:]
        cins, refs = refs[:n], refs[n:]
        outs, refs = refs[:n_out], refs[n_out:]
        couts, refs = refs[:n], refs[n:]
        scr, sems = refs[:n_scr], refs[n_scr:]
        first = functools.reduce(lambda a, b: a & b, [pl.program_id(d) == 0 for d in range(len(grid))], True)
        last = functools.reduce(lambda a, b: a & b, [pl.program_id(d) == grid[d] - 1 for d in range(len(grid))], True)
        if grid:
            steps = math.prod(grid)
            step = functools.reduce(lambda a, d: a * grid[d] + pl.program_id(d), range(len(grid)), 0)
            relay_early = comm.kind == "ag2" and steps >= 4
            pl.when(first)(lambda: comm.start(cins, couts, sems))
            body(*ins, *outs, *scr)
            if relay_early:
                pl.when(step == (3 * steps) // 4)(lambda: comm.relay(cins, couts, sems))
            pl.when(last)(lambda: comm.wait(cins, couts, sems, relayed=relay_early))
        else:
            comm.start(cins, couts, sems)
            if body is not None:
                body(*ins, *outs, *scr)
            comm.wait(cins, couts, sems)

    call = pl.pallas_call(
        wrapped, name=name, grid=grid, in_specs=list(in_specs) + [anyspec] * n, out_specs=c_out_specs + [anyspec] * n,
        out_shape=c_out_shape + comm.out_shape(), scratch_shapes=list(scratch) + comm.sems(), compiler_params=params)

    def run(*args):
        res = call(*args, *comm.arrays)
        mine = res[:n_out]
        return (mine[0] if single else tuple(mine)), list(res[n_out:])

    return run


def _tile(n, want, align=8):
    if n <= want:
        return n
    for t in range(want // align * align, 0, -align):
        if n % t == 0:
            return t
    return n


def _dot(a, b, dims=NN, precision=None):
    return lax.dot_general(a, b, dims, preferred_element_type=f32, precision=precision)


def _logsig(x):
    return jnp.minimum(x, 0.0) - jnp.log(1.0 + jnp.exp(-jnp.abs(x)))


def _softplus(x):
    return jnp.maximum(x, 0.0) + jnp.log(1.0 + jnp.exp(-jnp.abs(x)))


def _rms(x, g):
    return x * lax.rsqrt(jnp.mean(x * x, axis=-1, keepdims=True) + EPS) * g


def _lane_pick(blk, lane_idx):
    lane = lax.broadcasted_iota(jnp.int32, (1, LANES), 1)
    return jnp.sum(jnp.where(lane == lane_idx, blk, 0.0), axis=1, keepdims=True)


def mm(a, b, mode, out_dtype, name, tm=512, tn=512, res=None, scale=1.0, comm=None):
    if mode == "nn":
        (M, K), (_, N) = a.shape, b.shape
    elif mode == "nt":
        (M, K), (N, _) = a.shape, b.shape
    else:
        (K, M), (_, N) = a.shape, b.shape
    tm, tn = _tile(M, tm, LANES), _tile(N, tn, LANES)
    a_spec = pl.BlockSpec((K, tm), lambda j, i: (0, i)) if mode == "tn" else pl.BlockSpec((tm, K), lambda j, i: (i, 0))
    b_spec = pl.BlockSpec((tn, K), lambda j, i: (j, 0)) if mode == "nt" else pl.BlockSpec((K, tn), lambda j, i: (0, j))
    dims = {"nn": NN, "nt": NT, "tn": TN}[mode]
    o_spec = pl.BlockSpec((tm, tn), lambda j, i: (i, j))

    def body(*refs):
        acc = _dot(refs[0][...].astype(bf16), refs[1][...].astype(bf16), dims)
        if scale != 1.0:
            acc = acc * scale
        if res is not None:
            acc = acc + refs[2][...]
        refs[-1][...] = acc.astype(out_dtype)

    ins, specs = [a, b], [a_spec, b_spec]
    if res is not None:
        ins.append(res)
        specs.append(o_spec)
    return _call(body, name, (N // tn, M // tm), specs, o_spec, SDS((M, N), out_dtype), comm=comm)(*ins)


def _rowspec(tm, w, cb=0):
    return pl.BlockSpec((tm, w), lambda i: (i, cb))


def _bspec(w):
    return pl.BlockSpec((1, w), lambda i: (0, 0))


def rms_fwd(x, g, name):
    S, D = x.shape
    tm = _tile(S, 512)

    def body(x_ref, g_ref, h_ref):
        h_ref[...] = _rms(x_ref[...], g_ref[...]).astype(bf16)

    return _call(body, name, (S // tm,), [_rowspec(tm, D), _bspec(D)], _rowspec(tm, D), SDS((S, D), bf16))(x, g)


def _swiglu(g, u):
    return g * jax.nn.sigmoid(g) * u


def ffn_gu_act(h, wgu, name, tm=1024, tn=768, comm=None):
    S, D = h.shape
    W = wgu.shape[1] // 2
    tm, tn = _tile(S, tm, LANES), _tile(W, tn, LANES)
    nb = W // tn

    def body(h_ref, wg_ref, wu_ref, gu_ref, a_ref):
        hb = h_ref[...]
        g = _dot(hb, wg_ref[...])
        u = _dot(hb, wu_ref[...])
        gu_ref[0] = g.astype(bf16)
        gu_ref[1] = u.astype(bf16)
        a_ref[...] = _swiglu(g, u).astype(bf16)

    return _call(body, name, (nb, S // tm),
                 [pl.BlockSpec((tm, D), lambda j, i: (i, 0)), pl.BlockSpec((D, tn), lambda j, i: (0, j)),
                  pl.BlockSpec((D, tn), lambda j, i: (0, nb + j))],
                 (pl.BlockSpec((2, tm, tn), lambda j, i: (0, i, j)), pl.BlockSpec((tm, tn), lambda j, i: (i, j))),
                 (SDS((2, S, W), bf16), SDS((S, W), bf16)), comm=comm)(h, wgu, wgu)


def ffn_bda_act(dy, wd, gu, scale, name, tm=1024, tn=768, comm=None):
    S, D = dy.shape
    W = wd.shape[0]
    tm, tn = _tile(S, tm, LANES), _tile(W, tn, LANES)

    def body(dy_ref, wd_ref, gu_ref, dgu_ref):
        da = _dot(dy_ref[...].astype(bf16), wd_ref[...], NT) * scale
        _, vjp = jax.vjp(_swiglu, gu_ref[0].astype(f32), gu_ref[1].astype(f32))
        dg, du = vjp(da)
        dgu_ref[0] = dg.astype(bf16)
        dgu_ref[1] = du.astype(bf16)

    planes = pl.BlockSpec((2, tm, tn), lambda j, i: (0, i, j))
    return _call(body, name, (W // tn, S // tm),
                 [pl.BlockSpec((tm, D), lambda j, i: (i, 0)), pl.BlockSpec((tn, D), lambda j, i: (j, 0)), planes],
                 planes, SDS((2, S, W), bf16), comm=comm)(dy, wd, gu)


def ffn_bwgu(h, dgu, name, tm=1024, tn=768, comm=None):
    S, D = h.shape
    W = dgu.shape[2]
    tm, tn = _tile(D, tm, LANES), _tile(W, tn, LANES)
    nb = W // tn

    def body(h_ref, d_ref, o_ref):
        o_ref[...] = _dot(h_ref[...], d_ref[...], TN).astype(bf16)

    return _call(body, name, (2 * nb, D // tm),
                 [pl.BlockSpec((S, tm), lambda j, i: (0, i)), pl.BlockSpec((None, S, tn), lambda j, i: (j // nb, 0, j % nb))],
                 pl.BlockSpec((tm, tn), lambda j, i: (i, j)), SDS((D, 2 * W), bf16), comm=comm)(h, dgu)


def nt_rms_bwd(pairs, x, g, dres, name, tm=512, comm=None):
    S, D = x.shape
    tm = _tile(S, tm)
    n = len(pairs)

    def body(*refs):
        x_ref, g_ref, dres_ref = refs[2 * n:2 * n + 3]
        dx_ref, dxb_ref, dg_ref = refs[2 * n + 3:]
        dh = sum(_dot(refs[2 * k][...].astype(bf16), refs[2 * k + 1][...], NT) for k in range(n))
        _, vjp = jax.vjp(_rms, x_ref[...], g_ref[...])
        dx, dg = vjp(dh)
        dx = dres_ref[...] + dx
        dx_ref[...] = dx
        dxb_ref[...] = dx.astype(bf16)

        @pl.when(pl.program_id(0) == 0)
        def _():
            dg_ref[...] = jnp.zeros_like(dg_ref)

        dg_ref[...] += dg

    arrays, specs = [], []
    for a, a_spec, b, b_spec in pairs:
        arrays += [a, b]
        specs += [a_spec, b_spec]
    (dx, dxb, dg), got = _carried(_call(body, name, (S // tm,), specs + [_rowspec(tm, D), _bspec(D), _rowspec(tm, D)],
                                        (_rowspec(tm, D), _rowspec(tm, D), _bspec(D)),
                                        (SDS((S, D), f32), SDS((S, D), bf16), SDS((1, D), f32)), comm=comm)(*arrays, x, g, dres),
                                  comm)
    return ((dx, dxb), dg) if comm is None else ((dx, dxb), dg, got)


def ffn_bdh_rms(dgu, wgu, x, g, dres, name, tm=512, comm=None):
    _, S, W = dgu.shape
    D = wgu.shape[0]
    tm = _tile(S, tm)
    pairs = [(dgu, pl.BlockSpec((None, tm, W), functools.partial(lambda p, i: (p, i, 0), p)),
              wgu, pl.BlockSpec((D, W), functools.partial(lambda p, i: (0, p), p))) for p in range(2)]
    return nt_rms_bwd(pairs, x, g, dres, name, tm, comm=comm)


def mm_bdh_rms(d, w, x, g, dres, name, tm=512, comm=None):
    S, K = d.shape
    tm = _tile(S, tm)
    return nt_rms_bwd([(d, pl.BlockSpec((tm, K), lambda i: (i, 0)), w, pl.BlockSpec(w.shape, lambda i: (0, 0)))], x, g, dres, name, tm,
                      comm=comm)


def loss_head(y, t, name):
    S, D = y.shape
    tm = _tile(S, 512)

    def body(y_ref, t_ref, dy_ref, dyb_ref, l_ref):
        e = y_ref[...] - t_ref[...]
        dy_ref[...] = e * (1.0 / D)
        dyb_ref[...] = (e * (1.0 / D)).astype(bf16)

        @pl.when(pl.program_id(0) == 0)
        def _():
            l_ref[...] = jnp.zeros_like(l_ref)

        l_ref[...] += jnp.sum(jnp.sum(e * e, axis=1, keepdims=True), axis=0, keepdims=True) * (0.5 / D)

    dy, dyb, loss = _call(body, name, (S // tm,), [_rowspec(tm, D), _rowspec(tm, D)],
                          (_rowspec(tm, D), _rowspec(tm, D), _bspec(LANES)),
                          (SDS((S, D), f32), SDS((S, D), bf16), SDS((1, LANES), f32)))(y, t)
    return (dy, dyb), loss


def ffn_fwd(x, g, wgu, wd, tag, comm_gu=None, comm_down=None):
    h = rms_fwd(x, g, f"{tag}_rms")
    (gu, a), got_gu = _carried(ffn_gu_act(h, wgu, f"{tag}_gu", comm=comm_gu), comm_gu)
    xo, got_down = _carried(mm(a, wd, "nn", f32, f"{tag}_down", tm=512, tn=1024, res=x, scale=0.5, comm=comm_down), comm_down)
    return xo, (h, gu, a), got_gu, got_down


def ffn_bwd(x, g, wgu, wd, saved, dy, tag, comms=None, own_axes=None):
    h, gu, a = saved
    dy32, dy16 = dy
    cm = comms or {}
    dgu, got_bda = _carried(ffn_bda_act(dy16, wd, gu, 0.5, f"{tag}_bda", comm=cm.get("bda")), cm.get("bda"))
    dwd, got_bwd = _carried(mm(a, dy16, "tn", bf16, f"{tag}_bwd", tm=512, tn=1024, scale=0.5, comm=cm.get("bwd")), cm.get("bwd"))
    c_wd = Comm("rs", [dwd], [own_axes[1]]) if own_axes else None
    dwgu, got_wd = _carried(ffn_bwgu(h, dgu, f"{tag}_bwgu", comm=c_wd), c_wd)
    c_wgu = Comm("rs", [dwgu], [own_axes[0]]) if own_axes else None
    res = ffn_bdh_rms(dgu, wgu, x, g, dy32, f"{tag}_bdh", comm=c_wgu)
    dx, dg = res[:2]
    if comms is None and own_axes is None:
        return dx, dg, dwgu, dwd
    return dx, dg, dwgu, dwd, dict(bda=got_bda, bwd=got_bwd, wd=got_wd, wgu=res[2] if c_wgu is not None else [])


def _split_bf16(x):
    hi = x.astype(bf16)
    lo = (x - hi.astype(f32)).astype(bf16)
    return hi, lo


SB_TQ, SB_TK, SB_NSUB = 512, 256, 4
FOX_TK = 256


def _sb_geometry(S, tk=SB_TK):
    tq = min(SB_TQ, S)
    tk = min(tk, tq)
    return tq, tk, tq // SB_NSUB, tq // tk


def _sb_consts(tk):
    r = lax.broadcasted_iota(jnp.int32, (tk, tk), 0)
    c = lax.broadcasted_iota(jnp.int32, (tk, tk), 1)
    return (r > c).astype(bf16), (r < c).astype(bf16)


def _sb_scores(qs, k, kb, tk, rows, masked):
    scale = HEAD_DIM ** -0.5
    z = [_dot(q, k, NT) * scale for q in qs]
    ls = [_logsig(z_) for z_ in z]
    lm = [l - z_ for l, z_ in zip(ls, z)]
    ok = None
    if masked:
        col = kb * tk + lax.broadcasted_iota(jnp.int32, z[0].shape, 1)
        ok = [col < row for row in rows]
        lm = [jnp.where(m, x, 0.0) for m, x in zip(ok, lm)]
    return ls, lm, ok


def _sb_weights(ls, lm, ok, tail, after):
    suf = [_dot(x.astype(bf16), after) for x in lm]
    a = [jnp.exp(l + s_ + t_) for l, s_, t_ in zip(ls, suf, tail)]
    if ok is not None:
        a = [jnp.where(m, x, 0.0) for m, x in zip(ok, a)]
    return a


def sb_fwd(qkv, nh, name, comm=None):
    S = qkv.shape[0]
    tq, tk, rs, nd = _sb_geometry(S)
    nsub = tq // rs

    def body(q_ref, k_ref, v_ref, o_ref, tails_ref):
        i = pl.program_id(1)
        after, _ = _sb_consts(tk)
        qs = [q_ref[pl.ds(s * rs, rs), :] for s in range(nsub)]
        rows = [i * tq + s * rs + lax.broadcasted_iota(jnp.int32, (rs, tk), 0) for s in range(nsub)]

        def tile(kb, carry, masked):
            tail, acc = carry
            off = pl.multiple_of(kb * tk, tk)
            k = k_ref[pl.ds(off, tk), :]
            v = v_ref[pl.ds(off, tk), :]
            ls, lm, ok = _sb_scores(qs, k, kb, tk, rows, masked)
            a = _sb_weights(ls, lm, ok, tail, after)
            tails_ref[pl.ds(kb, 1), :] += jnp.concatenate([t_.T for t_ in tail], axis=1)
            acc = [ac + _dot(a_.astype(bf16), v) for ac, a_ in zip(acc, a)]
            tail = [t_ + jnp.sum(x, axis=1, keepdims=True) for t_, x in zip(tail, lm)]
            return tail, acc

        tails_ref[...] = jnp.zeros_like(tails_ref)
        carry = ([jnp.zeros((rs, 1), f32)] * nsub, [jnp.zeros((rs, HEAD_DIM), f32)] * nsub)
        for d in reversed(range(nd)):
            carry = tile(i * nd + d, carry, True)
        carry = lax.fori_loop(0, i * nd, lambda t, c: tile(i * nd - 1 - t, c, False), carry)
        for s in range(nsub):
            o_ref[pl.ds(s * rs, rs), :] = carry[1][s].astype(o_ref.dtype)

    blk = lambda cb0: pl.BlockSpec((tq, HEAD_DIM), lambda h, i: (i, cb0 + h))
    full = lambda cb0: pl.BlockSpec((S, HEAD_DIM), lambda h, i: (0, cb0 + h))
    tails = pl.BlockSpec((S // tk, tq), lambda h, i: (h, i))
    return _call(body, name, (nh, S // tq), [blk(0), full(nh), full(2 * nh)], (blk(0), tails),
                 (SDS((S, nh * HEAD_DIM), bf16), SDS((nh * (S // tk), S), f32)), comm=comm)(qkv, qkv, qkv)


def sb_bwd(qkv, tails, do, nh, name, comm=None):
    S = qkv.shape[0]
    tq, tk, rs, nd = _sb_geometry(S)
    nsub = tq // rs
    scale = HEAD_DIM ** -0.5

    def body(q_ref, k_ref, v_ref, tails_ref, do_ref, dq_ref, dk_ref, dv_ref):
        i = pl.program_id(1)

        @pl.when(i == 0)
        def _():
            dk_ref[...] = jnp.zeros_like(dk_ref)
            dv_ref[...] = jnp.zeros_like(dv_ref)

        after, before = _sb_consts(tk)
        qs = [q_ref[pl.ds(s * rs, rs), :] for s in range(nsub)]
        dos = [do_ref[pl.ds(s * rs, rs), :].astype(bf16) for s in range(nsub)]
        rows = [i * tq + s * rs + lax.broadcasted_iota(jnp.int32, (rs, tk), 0) for s in range(nsub)]

        def sweep2(kb, carry, masked):
            pe, dq = carry
            off = pl.multiple_of(kb * tk, tk)
            k = k_ref[pl.ds(off, tk), :]
            v = v_ref[pl.ds(off, tk), :]
            ls, lm, ok = _sb_scores(qs, k, kb, tk, rows, masked)
            tail_row = tails_ref[pl.ds(kb, 1), :]
            tail = [tail_row[:, s * rs:(s + 1) * rs].T for s in range(nsub)]
            a = _sb_weights(ls, lm, ok, tail, after)
            e = [_dot(d_, v, NT) * a_ for d_, a_ in zip(dos, a)]
            cum_e = [p_ + _dot(e_.astype(bf16), before) for p_, e_ in zip(pe, e)]
            sig = [jnp.exp(l) for l in ls]
            dz = [e_ * (1.0 - sg) - c_ * sg for e_, sg, c_ in zip(e, sig, cum_e)]
            if ok is not None:
                dz = [jnp.where(m, x, 0.0) for m, x in zip(ok, dz)]
            dzb = [(x * scale).astype(bf16) for x in dz]
            dk_ref[pl.ds(off, tk), :] += sum(_dot(x, q, TN) for x, q in zip(dzb, qs))
            dv_ref[pl.ds(off, tk), :] += sum(_dot(a_.astype(bf16), d_, TN) for a_, d_ in zip(a, dos))
            pe = [p_ + jnp.sum(e_, axis=1, keepdims=True) for p_, e_ in zip(pe, e)]
            dq = [g + _dot(x, k) for g, x in zip(dq, dzb)]
            return pe, dq

        carry = ([jnp.zeros((rs, 1), f32)] * nsub, [jnp.zeros((rs, HEAD_DIM), f32)] * nsub)
        carry = lax.fori_loop(0, i * nd, lambda kb, c: sweep2(kb, c, False), carry)
        for d in range(nd):
            carry = sweep2(i * nd + d, carry, True)
        for s in range(nsub):
            dq_ref[pl.ds(s * rs, rs), :] = carry[1][s]

    blk = lambda cb0: pl.BlockSpec((tq, HEAD_DIM), lambda h, i: (i, cb0 + h))
    full = lambda cb0: pl.BlockSpec((S, HEAD_DIM), lambda h, i: (0, cb0 + h))
    sh = SDS((S, nh * HEAD_DIM), f32)
    tails_spec = pl.BlockSpec((S // tk, tq), lambda h, i: (h, i))
    return _call(body, name, (nh, S // tq), [blk(0), full(nh), full(2 * nh), tails_spec, blk(0)], (blk(0), full(0), full(0)),
                 (sh, sh, sh), comm=comm)(qkv, qkv, qkv, tails, do)


def fox_fwd(fq, fk, fv, c, cT, nh, lane0, name, comm=None):
    S = fq.shape[0]
    tq, tk, rs, nd = _sb_geometry(S, FOX_TK)
    nsub = tq // rs
    scale = HEAD_DIM ** -0.5

    def body(q_ref, k_ref, v_ref, c_ref, ct_ref, o_ref, lse_ref):
        h = pl.program_id(0)
        i = pl.program_id(1)
        subs = range(nsub)
        qs = [q_ref[pl.ds(s * rs, rs), :] for s in subs]
        ci = [_lane_pick(c_ref[pl.ds(s * rs, rs), :], lane0 + h) for s in subs]
        rows = [i * tq + s * rs + lax.broadcasted_iota(jnp.int32, (rs, tk), 0) for s in subs]

        def tile(kb, carry, masked):
            m, l, acc = carry
            off = pl.multiple_of(kb * tk, tk)
            k = k_ref[pl.ds(off, tk), :]
            v = v_ref[pl.ds(off, tk), :]
            cj = ct_ref[pl.ds(lane0 % 8 + h, 1), pl.ds(off, tk)]
            sc = [_dot(q, k, NT) * scale + (c_ - cj) for q, c_ in zip(qs, ci)]
            if masked:
                col = kb * tk + lax.broadcasted_iota(jnp.int32, (rs, tk), 1)
                sc = [jnp.where(col <= row, x, -jnp.inf) for x, row in zip(sc, rows)]
            m2 = [jnp.maximum(m_, jnp.max(x, axis=1, keepdims=True)) for m_, x in zip(m, sc)]
            p = [jnp.exp(x - m_) for x, m_ in zip(sc, m2)]
            al = [jnp.exp(a - b) for a, b in zip(m, m2)]
            l = [a * l_ + jnp.sum(p_, axis=1, keepdims=True) for a, l_, p_ in zip(al, l, p)]
            acc = [a * ac + _dot(p_.astype(bf16), v) for a, ac, p_ in zip(al, acc, p)]
            return m2, l, acc

        carry = ([jnp.full((rs, 1), -jnp.inf, f32)] * nsub, [jnp.zeros((rs, 1), f32)] * nsub,
                 [jnp.zeros((rs, HEAD_DIM), f32)] * nsub)
        for d in range(nd):
            carry = tile(i * nd + d, carry, True)
        m, l, acc = lax.fori_loop(0, i * nd, lambda kb, cr: tile(kb, cr, False), carry)
        for s in subs:
            o_ref[pl.ds(s * rs, rs), :] = acc[s] / l[s]
            lse_ref[pl.ds(s * rs, rs), :] = jnp.broadcast_to(m[s] + jnp.log(l[s]), (rs, HEAD_DIM))

    blk = pl.BlockSpec((tq, HEAD_DIM), lambda h, i: (i, h))
    full = pl.BlockSpec((S, HEAD_DIM), lambda h, i: (0, h))
    cspec = pl.BlockSpec((tq, LANES), lambda h, i: (i, 0))
    ctspec = pl.BlockSpec((8, S), lambda h, i: (lane0 // 8, 0))
    sh = SDS((S, nh * HEAD_DIM), f32)
    return _call(body, name, (nh, S // tq), [blk, full, full, cspec, ctspec], (blk, blk), (sh, sh), comm=comm)(fq, fk, fv, c, cT)


def fox_bwd(fq, fk, fv, c, cT, o, lse, do, nh, lane0, name, comm=None):
    S = fq.shape[0]
    tq, tk, rs, nd = _sb_geometry(S, FOX_TK)
    nsub = tq // rs
    scale = HEAD_DIM ** -0.5

    def body(q_ref, k_ref, v_ref, c_ref, ct_ref, o_ref, lse_ref, do_ref, dq_ref, dk_ref, dv_ref, dct_ref):
        h = pl.program_id(0)
        i = pl.program_id(1)

        @pl.when(i == 0)
        def _():
            dk_ref[...] = jnp.zeros_like(dk_ref)
            dv_ref[...] = jnp.zeros_like(dv_ref)

        @pl.when((i == 0) & (h == 0))
        def _():
            dct_ref[...] = jnp.zeros_like(dct_ref)

        subs = range(nsub)
        qs = [q_ref[pl.ds(s * rs, rs), :] for s in subs]
        dof = [do_ref[pl.ds(s * rs, rs), :] for s in subs]
        dos = [x.astype(bf16) for x in dof]
        ci = [_lane_pick(c_ref[pl.ds(s * rs, rs), :], lane0 + h) for s in subs]
        lses = [lse_ref[pl.ds(s * rs, rs), :][:, :1] for s in subs]
        dl = [jnp.sum(x * o_ref[pl.ds(s * rs, rs), :], axis=1, keepdims=True) for s, x in zip(subs, dof)]
        rows = [i * tq + s * rs + lax.broadcasted_iota(jnp.int32, (rs, tk), 0) for s in subs]

        def tile(kb, carry, masked):
            dq, rsum = carry
            off = pl.multiple_of(kb * tk, tk)
            k = k_ref[pl.ds(off, tk), :]
            v = v_ref[pl.ds(off, tk), :]
            cj = ct_ref[pl.ds(lane0 % 8 + h, 1), pl.ds(off, tk)]
            p = [jnp.exp(_dot(q, k, NT) * scale + (c_ - cj) - ls) for q, c_, ls in zip(qs, ci, lses)]
            if masked:
                col = kb * tk + lax.broadcasted_iota(jnp.int32, (rs, tk), 1)
                p = [jnp.where(col <= row, x, 0.0) for x, row in zip(p, rows)]
            ds = [p_ * (_dot(d_, v, NT) - dl_) for p_, d_, dl_ in zip(p, dos, dl)]
            dsb = [(x * scale).astype(bf16) for x in ds]
            dk_ref[pl.ds(off, tk), :] += sum(_dot(x, q, TN) for x, q in zip(dsb, qs))
            dv_ref[pl.ds(off, tk), :] += sum(_dot(p_.astype(bf16), d_, TN) for p_, d_ in zip(p, dos))
            dct_ref[pl.ds(lane0 + h, 1), pl.ds(off, tk)] -= sum(jnp.sum(x, axis=0, keepdims=True) for x in ds)
            return ([g + _dot(x, k) for g, x in zip(dq, dsb)],
                    [r_ + jnp.sum(x, axis=1, keepdims=True) for r_, x in zip(rsum, ds)])

        carry = ([jnp.zeros((rs, HEAD_DIM), f32)] * nsub, [jnp.zeros((rs, 1), f32)] * nsub)
        carry = lax.fori_loop(0, i * nd, lambda kb, cr: tile(kb, cr, False), carry)
        for d in range(nd):
            carry = tile(i * nd + d, carry, True)
        dq, rsum = carry
        for s in subs:
            dq_ref[pl.ds(s * rs, rs), :] = dq[s]
        dct_ref[pl.ds(lane0 + h, 1), pl.ds(pl.multiple_of(i * tq, tq), tq)] += jnp.concatenate([r_.T for r_ in rsum], axis=1)

    blk = pl.BlockSpec((tq, HEAD_DIM), lambda h, i: (i, h))
    full = pl.BlockSpec((S, HEAD_DIM), lambda h, i: (0, h))
    cspec = pl.BlockSpec((tq, LANES), lambda h, i: (i, 0))
    ctspec = pl.BlockSpec((8, S), lambda h, i: (lane0 // 8, 0))
    dctspec = pl.BlockSpec((LANES, S), lambda h, i: (0, 0))
    sh = SDS((S, nh * HEAD_DIM), f32)
    return _call(body, name, (nh, S // tq), [blk, full, full, cspec, ctspec, blk, blk, blk], (blk, full, full, dctspec),
                 (sh, sh, sh, SDS((LANES, S), f32)), comm=comm)(fq, fk, fv, c, cT, o, lse, do)


def gates_fwd(proj, small_cb, fbias_row, name, tm=256):
    S = proj.shape[0]
    tm = _tile(S, tm)

    def body(sm_ref, fb_ref, c_ref, ct_ref, carry_ref):
        @pl.when(pl.program_id(0) == 0)
        def _():
            carry_ref[...] = jnp.zeros_like(carry_ref)

        lf = _logsig(sm_ref[...] + fb_ref[...])
        r = lax.broadcasted_iota(jnp.int32, (tm, tm), 0)
        cc = lax.broadcasted_iota(jnp.int32, (tm, tm), 1)
        c = _dot((r >= cc).astype(f32), lf, NN, HI) + carry_ref[...]
        carry_ref[...] += jnp.sum(lf, axis=0, keepdims=True)
        c_ref[...] = c
        ct_ref[...] = c.T

    return _call(body, name, (S // tm,), [_rowspec(tm, LANES, small_cb), _bspec(LANES)],
                 (_rowspec(tm, LANES), pl.BlockSpec((LANES, tm), lambda i: (0, i))),
                 (SDS((S, LANES), f32), SDS((LANES, S), f32)), scratch=[pltpu.VMEM((1, LANES), f32)])(proj, fbias_row)


def gates_bwd(proj, small_cb, fbias_row, dcT, dsm_dn, lane0, nh, name, tm=256):
    S = proj.shape[0]
    tm = _tile(S, tm)
    nt = S // tm

    def body(sm_ref, fb_ref, dct_ref, dsm_ref, o_ref, dfb_ref, carry_ref):
        @pl.when(pl.program_id(0) == 0)
        def _():
            carry_ref[...] = jnp.zeros_like(carry_ref)
            dfb_ref[...] = jnp.zeros_like(dfb_ref)

        dc = dct_ref[...].T
        r = lax.broadcasted_iota(jnp.int32, (tm, tm), 0)
        cc = lax.broadcasted_iota(jnp.int32, (tm, tm), 1)
        dlf = _dot((r <= cc).astype(f32), dc, NN, HI) + carry_ref[...]
        carry_ref[...] += jnp.sum(dc, axis=0, keepdims=True)
        lane = lax.broadcasted_iota(jnp.int32, (1, LANES), 1)
        dpre = jnp.where((lane >= lane0) & (lane < lane0 + nh), dlf * jax.nn.sigmoid(-(sm_ref[...] + fb_ref[...])), 0.0)
        o_ref[...] = dsm_ref[...] + dpre
        dfb_ref[...] += jnp.sum(dpre, axis=0, keepdims=True)

    rev = lambda i: nt - 1 - i
    return _call(body, name, (nt,),
                 [pl.BlockSpec((tm, LANES), lambda i: (rev(i), small_cb)), _bspec(LANES),
                  pl.BlockSpec((LANES, tm), lambda i: (0, rev(i))), pl.BlockSpec((tm, LANES), lambda i: (rev(i), 0))],
                 (pl.BlockSpec((tm, LANES), lambda i: (rev(i), 0)), _bspec(LANES)),
                 (SDS((S, LANES), f32), SDS((1, LANES), f32)), scratch=[pltpu.VMEM((1, LANES), f32)])(proj, fbias_row, dcT, dsm_dn)


PAD_ROWS = 8


def conv_fwd(proj, w, width, name):
    S = proj.shape[0]
    cw = 256 if width % 256 == 0 else 128

    def body(x_ref, w_ref, y_ref, pad_ref):
        pad_ref[pl.ds(0, PAD_ROWS), :] = jnp.zeros((PAD_ROWS, cw), f32)
        pad_ref[pl.ds(PAD_ROWS, S), :] = x_ref[...]
        y = jnp.zeros((S, cw), f32)
        for i in range(CONV_WIDTH):
            y = y + pad_ref[pl.ds(PAD_ROWS - (CONV_WIDTH - 1) + i, S), :] * w_ref[pl.ds(i, 1), :]
        y_ref[...] = y * jax.nn.sigmoid(y)

    spec = pl.BlockSpec((S, cw), lambda j: (0, j))
    return _call(body, name, (width // cw,), [spec, pl.BlockSpec((CONV_WIDTH, cw), lambda j: (0, j))], spec,
                 SDS((S, width), f32), scratch=[pltpu.VMEM((S + PAD_ROWS, cw), f32)])(proj, w)


def conv_bwd(proj, w, dy, name):
    S, width = dy.shape
    cw = 256 if width % 256 == 0 else 128

    def body(x_ref, w_ref, dy_ref, dx_ref, dw_ref, xpad_ref, dpad_ref):
        xpad_ref[pl.ds(0, PAD_ROWS), :] = jnp.zeros((PAD_ROWS, cw), f32)
        xpad_ref[pl.ds(PAD_ROWS, S), :] = x_ref[...]
        y = jnp.zeros((S, cw), f32)
        for i in range(CONV_WIDTH):
            y = y + xpad_ref[pl.ds(PAD_ROWS - (CONV_WIDTH - 1) + i, S), :] * w_ref[pl.ds(i, 1), :]
        sg = jax.nn.sigmoid(y)
        dpre = dy_ref[...] * (sg * (1.0 + y * (1.0 - sg)))
        dpad_ref[pl.ds(S, PAD_ROWS), :] = jnp.zeros((PAD_ROWS, cw), f32)
        dpad_ref[pl.ds(0, S), :] = dpre
        dx = jnp.zeros((S, cw), f32)
        for i in range(CONV_WIDTH):
            dx = dx + dpad_ref[pl.ds(CONV_WIDTH - 1 - i, S), :] * w_ref[pl.ds(i, 1), :]
            dw_ref[pl.ds(i, 1), :] = jnp.sum(dpre * xpad_ref[pl.ds(PAD_ROWS - (CONV_WIDTH - 1) + i, S), :], axis=0, keepdims=True)
        dx_ref[...] = dx

    spec = pl.BlockSpec((S, cw), lambda j: (0, j))
    wspec = pl.BlockSpec((CONV_WIDTH, cw), lambda j: (0, j))
    return _call(body, name, (width // cw,), [spec, wspec, spec], (spec, wspec),
                 (SDS((S, width), f32), SDS((CONV_WIDTH, width), f32)),
                 scratch=[pltpu.VMEM((S + PAD_ROWS, cw), f32), pltpu.VMEM((S + PAD_ROWS, cw), f32)])(proj, w, dy)


def _pdot_raw(a, b, dims, passes):
    if passes == 1:
        return _dot(a.astype(bf16), b.astype(bf16), dims)
    ah, al = _split_bf16(a)
    bh, bl = _split_bf16(b)
    return _dot(ah, bh, dims) + (_dot(ah, bl, dims) + _dot(al, bh, dims))


def _make_pdot(passes):
    @functools.partial(jax.custom_vjp, nondiff_argnums=(2,))
    def pdot(a, b, mode):
        return _pdot_raw(a, b, {"nn": NN, "nt": NT, "tn": TN}[mode], passes)

    def fwd(a, b, mode):
        return pdot(a, b, mode), (a, b)

    def bwd(mode, res, g):
        a, b = res
        if mode == "nn":
            return _pdot_raw(g, b, NT, passes), _pdot_raw(a, g, TN, passes)
        if mode == "nt":
            return _pdot_raw(g, b, NN, passes), _pdot_raw(g, a, TN, passes)
        return _pdot_raw(b, g, NT, passes), _pdot_raw(a, g, NN, passes)

    pdot.defvjp(fwd, bwd)
    return pdot


_dot1 = _make_pdot(1)
_dot3 = _make_pdot(3)


def _each(f, *lists):
    return [f(*xs) for xs in zip(*lists)]


def _dn_chunk(ndn, cq, ck, cv, small, alog, dtb, s0):
    C = cq[0].shape[0]
    hs = list(range(ndn))
    b = [_lane_pick(small, h) for h in hs]
    d = [_lane_pick(small, ndn + h) for h in hs]
    al = [_lane_pick(alog, h) for h in hs]
    dt = [_lane_pick(dtb, h) for h in hs]
    q = _each(lambda x: x * lax.rsqrt(jnp.sum(x * x, axis=1, keepdims=True) + EPS) * (HEAD_DIM ** -0.5), cq)
    k = _each(lambda x: x * lax.rsqrt(jnp.sum(x * x, axis=1, keepdims=True) + EPS), ck)
    beta = _each(jax.nn.sigmoid, b)
    g = _each(lambda al_, d_, dt_: -jnp.exp(al_) * _softplus(d_ + dt_), al, d, dt)
    r = lax.broadcasted_iota(jnp.int32, (C, C), 0)
    c = lax.broadcasted_iota(jnp.int32, (C, C), 1)
    incl, strict = r >= c, r > c
    inclf = incl.astype(f32)
    gc = _each(lambda g_: _dot3(inclf, g_, "nn"), g)
    decay = _each(lambda gc_: jnp.where(incl, jnp.exp(jnp.where(incl, gc_ - gc_.T, 0.0)), 0.0), gc)
    kb = _each(lambda k_, b_: k_ * b_, k, beta)
    a = _each(lambda kb_, k_, dec: jnp.where(strict, _dot3(kb_, k_, "nt") * dec, 0.0), kb, k, decay)
    eye = (r == c).astype(f32)
    t = _each(lambda a_: eye - a_, a)
    p = a
    for _ in range(int(math.log2(C)) - 1):
        p = _each(lambda p_: _dot3(p_, p_, "nn"), p)
        t = _each(lambda t_, p_: t_ + _dot3(t_, p_, "nn"), t, p)
    eg = _each(jnp.exp, gc)
    u = _each(lambda t_, v_, b_: _dot3(t_, v_ * b_, "nn"), t, cv, beta)
    w = _each(lambda t_, kb_, eg_: _dot3(t_, kb_ * eg_, "nn"), t, kb, eg)
    attn = _each(lambda q_, k_, dec: _dot1(q_, k_, "nt") * dec, q, k, decay)
    g_last = _each(lambda g_: jnp.sum(g_, axis=0, keepdims=True), g)
    v_new = _each(lambda u_, w_, s_: u_ - _dot1(w_, s_, "nn"), u, w, s0)
    o = _each(lambda q_, eg_, s_, at, vn: _dot1(q_ * eg_, s_, "nn") + _dot1(at, vn, "nn"), q, eg, s0, attn, v_new)
    s1 = _each(lambda s_, gl, k_, gc_, vn: s_ * jnp.exp(gl) + _dot1(k_ * jnp.exp(gl - gc_), vn, "tn"), s0, g_last, k, gc, v_new)
    return o, s1


def dn_fwd(cqkv, proj, small_cb, alog_row, dt_row, ndn, name, comm=None):
    S = cqkv.shape[0]
    C = DN_CHUNK
    nc = S // C
    half = ndn * HEAD_DIM

    def body(q_ref, k_ref, v_ref, sm_ref, al_ref, dt_ref, o_ref, ss_ref, s_ref):
        @pl.when(pl.program_id(0) == 0)
        def _():
            s_ref[...] = jnp.zeros_like(s_ref)

        sls = [slice(h * HEAD_DIM, (h + 1) * HEAD_DIM) for h in range(ndn)]
        s0 = [s_ref[h] for h in range(ndn)]
        for h in range(ndn):
            ss_ref[h, 0] = s0[h]
        o, s1 = _dn_chunk(ndn, [q_ref[:, sl] for sl in sls], [k_ref[:, sl] for sl in sls], [v_ref[:, sl] for sl in sls],
                          sm_ref[...], al_ref[...], dt_ref[...], s0)
        for h in range(ndn):
            o_ref[:, sls[h]] = o[h]
            s_ref[h] = s1[h]

    blk = lambda cb: pl.BlockSpec((C, half), lambda n: (n, cb))
    row = pl.BlockSpec((1, LANES), lambda n: (0, 0))
    return _call(body, name, (nc,),
                 [blk(0), blk(1), blk(2), pl.BlockSpec((C, LANES), lambda n: (n, small_cb)), row, row],
                 (blk(0), pl.BlockSpec((ndn, 1, HEAD_DIM, HEAD_DIM), lambda n: (0, n, 0, 0))),
                 (SDS((S, half), f32), SDS((ndn, nc, HEAD_DIM, HEAD_DIM), f32)),
                 scratch=[pltpu.VMEM((ndn, HEAD_DIM, HEAD_DIM), f32)], comm=comm)(cqkv, cqkv, cqkv, proj, alog_row, dt_row)


def dn_bwd(cqkv, proj, small_cb, alog_row, dt_row, ssave, do, ndn, name, comm=None):
    S = cqkv.shape[0]
    C = DN_CHUNK
    nc = S // C
    half = ndn * HEAD_DIM

    def body(q_ref, k_ref, v_ref, sm_ref, al_ref, dt_ref, ss_ref, do_ref, dqkv_ref, dsm_ref, dal_ref, ddt_ref, ds_ref):
        @pl.when(pl.program_id(0) == 0)
        def _():
            ds_ref[...] = jnp.zeros_like(ds_ref)
            dal_ref[...] = jnp.zeros_like(dal_ref)
            ddt_ref[...] = jnp.zeros_like(ddt_ref)

        sls = [slice(h * HEAD_DIM, (h + 1) * HEAD_DIM) for h in range(ndn)]
        _, vjp = jax.vjp(functools.partial(_dn_chunk, ndn), [q_ref[:, sl] for sl in sls], [k_ref[:, sl] for sl in sls],
                         [v_ref[:, sl] for sl in sls], sm_ref[...], al_ref[...], dt_ref[...], [ss_ref[h, 0] for h in range(ndn)])
        dq, dk, dv, dsm, dal, ddt, ds0 = vjp(([do_ref[:, sl] for sl in sls], [ds_ref[h] for h in range(ndn)]))
        for h in range(ndn):
            dqkv_ref[:, sls[h]] = dq[h]
            dqkv_ref[:, half + h * HEAD_DIM:half + (h + 1) * HEAD_DIM] = dk[h]
            dqkv_ref[:, 2 * half + h * HEAD_DIM:2 * half + (h + 1) * HEAD_DIM] = dv[h]
            ds_ref[h] = ds0[h]
        dsm_ref[...] = dsm
        dal_ref[...] += dal
        ddt_ref[...] += ddt

    rev = lambda n: nc - 1 - n
    blk = lambda cb: pl.BlockSpec((C, half), lambda n: (rev(n), cb))
    row = pl.BlockSpec((1, LANES), lambda n: (0, 0))
    return _call(body, name, (nc,),
                 [blk(0), blk(1), blk(2), pl.BlockSpec((C, LANES), lambda n: (rev(n), small_cb)), row, row,
                  pl.BlockSpec((ndn, 1, HEAD_DIM, HEAD_DIM), lambda n: (0, rev(n), 0, 0)), blk(0)],
                 (pl.BlockSpec((C, 3 * half), lambda n: (rev(n), 0)), pl.BlockSpec((C, LANES), lambda n: (rev(n), 0)), row, row),
                 (SDS((S, 3 * half), f32), SDS((S, LANES), f32), SDS((1, LANES), f32), SDS((1, LANES), f32)),
                 scratch=[pltpu.VMEM((ndn, HEAD_DIM, HEAD_DIM), f32)], comm=comm)(cqkv, cqkv, cqkv, proj, alog_row, dt_row, ssave, do)


def even_pre(proj, gq, gk, dfx, cb0, name):
    S = proj.shape[0]
    tm = _tile(S, 256)
    nh = dfx // HEAD_DIM

    def body(q_ref, k_ref, v_ref, gq_ref, gk_ref, fq_ref, fk_ref, fv_ref):
        for h in range(nh):
            sl = slice(h * HEAD_DIM, (h + 1) * HEAD_DIM)
            fq_ref[:, sl] = _rms(q_ref[:, sl], gq_ref[...]).astype(bf16)
            fk_ref[:, sl] = _rms(k_ref[:, sl], gk_ref[...]).astype(bf16)
        fv_ref[...] = v_ref[...].astype(bf16)

    sh = SDS((S, dfx), bf16)
    o = _rowspec(tm, dfx)
    return _call(body, name, (S // tm,),
                 [_rowspec(tm, dfx, cb0), _rowspec(tm, dfx, cb0 + 1), _rowspec(tm, dfx, cb0 + 2), _bspec(HEAD_DIM), _bspec(HEAD_DIM)],
                 (o, o, o), (sh, sh, sh))(proj, proj, proj, gq, gk)


def even_pre_bwd(proj, gq, gk, dfq, dfk, dfx, cb0, name):
    S = proj.shape[0]
    tm = _tile(S, 256)
    nh = dfx // HEAD_DIM

    def body(q_ref, k_ref, gq_ref, gk_ref, dfq_ref, dfk_ref, dq_ref, dk_ref, dgq_ref, dgk_ref):
        @pl.when(pl.program_id(0) == 0)
        def _():
            dgq_ref[...] = jnp.zeros_like(dgq_ref)
            dgk_ref[...] = jnp.zeros_like(dgk_ref)

        for h in range(nh):
            sl = slice(h * HEAD_DIM, (h + 1) * HEAD_DIM)
            _, vq = jax.vjp(_rms, q_ref[:, sl], gq_ref[...])
            dq, dgq = vq(dfq_ref[:, sl])
            _, vk = jax.vjp(_rms, k_ref[:, sl], gk_ref[...])
            dk, dgk = vk(dfk_ref[:, sl])
            dq_ref[:, sl] = dq
            dk_ref[:, sl] = dk
            dgq_ref[...] += dgq
            dgk_ref[...] += dgk

    sh = SDS((S, dfx), f32)
    o = _rowspec(tm, dfx)
    g = SDS((1, HEAD_DIM), f32)
    return _call(body, name, (S // tm,),
                 [_rowspec(tm, dfx, cb0), _rowspec(tm, dfx, cb0 + 1), _bspec(HEAD_DIM), _bspec(HEAD_DIM), o, o],
                 (o, o, _bspec(HEAD_DIM), _bspec(HEAD_DIM)), (sh, sh, g, g))(proj, proj, gq, gk, dfq, dfk)


def _dn_out(o, gate, gn):
    return _rms(o, gn) * (gate * jax.nn.sigmoid(gate))


def _fox_out(o, gate):
    return o * jax.nn.sigmoid(gate)


def even_post(o_dn, o_fox, proj, gn, half, cb_dn_gate, cb_fox_gate, name):
    S = proj.shape[0]
    tm = _tile(S, 256)
    nh = half // HEAD_DIM

    def body(od_ref, of_ref, gd_ref, gf_ref, gn_ref, o_ref):
        for h in range(nh):
            sl = slice(h * HEAD_DIM, (h + 1) * HEAD_DIM)
            o_ref[:, sl] = _dn_out(od_ref[:, sl], gd_ref[:, sl], gn_ref[...]).astype(bf16)
        o_ref[:, half:] = _fox_out(of_ref[...], gf_ref[...]).astype(bf16)

    return _call(body, name, (S // tm,),
                 [_rowspec(tm, half), _rowspec(tm, half), _rowspec(tm, half, cb_dn_gate), _rowspec(tm, half, cb_fox_gate), _bspec(HEAD_DIM)],
                 _rowspec(tm, 2 * half), SDS((S, 2 * half), bf16))(o_dn, o_fox, proj, proj, gn)


def even_post_bwd(o_dn, o_fox, proj, gn, dom, half, cb_dn_gate, cb_fox_gate, name):
    S = proj.shape[0]
    tm = _tile(S, 256)
    nh = half // HEAD_DIM

    def body(od_ref, of_ref, gd_ref, gf_ref, gn_ref, dod_ref, dof_ref, dd_ref, df_ref, dgd_ref, dgf_ref, dgn_ref):
        @pl.when(pl.program_id(0) == 0)
        def _():
            dgn_ref[...] = jnp.zeros_like(dgn_ref)

        for h in range(nh):
            sl = slice(h * HEAD_DIM, (h + 1) * HEAD_DIM)
            _, vjp = jax.vjp(_dn_out, od_ref[:, sl], gd_ref[:, sl], gn_ref[...])
            d_o, d_gate, d_gn = vjp(dod_ref[:, sl])
            dd_ref[:, sl] = d_o
            dgd_ref[:, sl] = d_gate
            dgn_ref[...] += d_gn
        _, vjp = jax.vjp(_fox_out, of_ref[...], gf_ref[...])
        d_o, d_gate = vjp(dof_ref[...])
        df_ref[...] = d_o
        dgf_ref[...] = d_gate

    sh = SDS((S, half), f32)
    o = _rowspec(tm, half)
    return _call(body, name, (S // tm,),
                 [o, o, _rowspec(tm, half, cb_dn_gate), _rowspec(tm, half, cb_fox_gate), _bspec(HEAD_DIM),
                  _rowspec(tm, half, 0), _rowspec(tm, half, 1)],
                 (o, o, o, o, _bspec(HEAD_DIM)), (sh, sh, sh, sh, SDS((1, HEAD_DIM), f32)))(o_dn, o_fox, proj, proj, gn, dom, dom)


def _pad_row(v, lane0=0):
    return jnp.pad(v, (lane0, LANES - lane0 - v.shape[0]))[None]


def _carried(res, comm):
    return res if comm is not None else (res, [])


def even_mixer_fwd(x, gmix, w_in, w_out, conv_w, a_log, dt_bias, gn, gq, gk, f_bias, tag, comm_fox=None, comm_dn=None,
                   comm_in=None):
    S, D = x.shape
    half = D // 2
    ndn = half // HEAD_DIM
    small_cb = 4 * D // LANES
    alog_row, dt_row, fb_row = _pad_row(a_log), _pad_row(dt_bias), _pad_row(f_bias, 2 * ndn)
    h = rms_fwd(x, gmix, f"{tag}_rms")
    proj, got_in = _carried(mm(h, w_in, "nn", f32, f"{tag}_in", tm=1024, tn=1408, comm=comm_in), comm_in)
    cqkv = conv_fwd(proj, conv_w, 3 * half, f"{tag}_conv")
    (o_dn, ssave), got_dn = _carried(dn_fwd(cqkv, proj, small_cb, alog_row, dt_row, ndn, f"{tag}_dn", comm=comm_dn), comm_dn)
    c, cT = gates_fwd(proj, small_cb, fb_row, f"{tag}_gates")
    fq, fk, fv = even_pre(proj, gq, gk, half, 4, f"{tag}_pre")
    (o_fox, lse), got_fox = _carried(fox_fwd(fq, fk, fv, c, cT, ndn, 2 * ndn, f"{tag}_fox", comm=comm_fox), comm_fox)
    om = even_post(o_dn, o_fox, proj, gn, half, 3, 7, f"{tag}_post")
    xo = mm(om, w_out, "nn", f32, f"{tag}_out", tn=1024, res=x)
    return xo, (h, proj, cqkv, o_dn, ssave, c, cT, fq, fk, fv, o_fox, lse, om, alog_row, dt_row, fb_row), got_fox, got_dn, got_in


def even_mixer_bwd(x, gmix, w_in, w_out, conv_w, gn, gq, gk, saved, dy, tag, comm_fox=None, comm_dn=None, comm_bdh=None):
    S, D = x.shape
    half = D // 2
    ndn = half // HEAD_DIM
    small_cb = 4 * D // LANES
    h, proj, cqkv, o_dn, ssave, c, cT, fq, fk, fv, o_fox, lse, om, alog_row, dt_row, fb_row = saved
    dy32, dy16 = dy
    dom = mm(dy16, w_out, "nt", f32, f"{tag}_bdom", tn=1024)
    dw_out = mm(om, dy16, "tn", bf16, f"{tag}_bwout", tn=1024)
    d_odn, d_ofox, d_dgate, d_fgate, d_gn = even_post_bwd(o_dn, o_fox, proj, gn, dom, half, 3, 7, f"{tag}_bpost")
    (dfq, dfk, dfv, dcT), got_fox = _carried(
        fox_bwd(fq, fk, fv, c, cT, o_fox, lse, d_ofox, ndn, 2 * ndn, f"{tag}_bfox", comm=comm_fox), comm_fox)
    dq_pre, dk_pre, d_gq, d_gk = even_pre_bwd(proj, gq, gk, dfq, dfk, half, 4, f"{tag}_bpre")
    (dcqkv, dsm_dn, d_alog, d_dt), got_dn = _carried(
        dn_bwd(cqkv, proj, small_cb, alog_row, dt_row, ssave, d_odn, ndn, f"{tag}_bdn", comm=comm_dn), comm_dn)
    d_conv_in, d_conv_w = conv_bwd(proj, conv_w, dcqkv, f"{tag}_bconv")
    d_small, d_fb = gates_bwd(proj, small_cb, fb_row, dcT, dsm_dn, 2 * ndn, ndn, f"{tag}_bgates")
    dproj = jnp.concatenate([d_conv_in, d_dgate, dq_pre, dk_pre, dfv, d_fgate, d_small], axis=1).astype(bf16)
    dw_in = mm(h, dproj, "tn", bf16, f"{tag}_bwin", tn=1408)
    res = mm_bdh_rms(dproj, w_in, x, gmix, dy32, f"{tag}_bdh", comm=comm_bdh)
    dx, d_gmix = res[:2]
    got_bdh = res[2] if comm_bdh is not None else []
    small = dict(norm_mix=d_gmix[0], dn_conv_w=d_conv_w, dn_a_log=d_alog[0, :ndn], dn_dt_bias=d_dt[0, :ndn],
                 dn_norm_g=d_gn[0], fox_q_norm_g=d_gq[0], fox_k_norm_g=d_gk[0], fox_f_bias=d_fb[0, 2 * ndn:3 * ndn])
    return dx, dw_in, dw_out, small, got_fox, got_dn, got_bdh


def odd_mixer_fwd(x, gmix, w_in, w_out, tag, comm=None):
    S, D = x.shape
    nh = D // HEAD_DIM
    h = rms_fwd(x, gmix, f"{tag}_rms")
    qkv = mm(h, w_in, "nn", bf16, f"{tag}_in", tm=1024, tn=768)
    (o, tails), got = _carried(sb_fwd(qkv, nh, f"{tag}_sb", comm=comm), comm)
    xo = mm(o, w_out, "nn", f32, f"{tag}_out", tn=1024, res=x)
    return xo, (h, qkv, o, tails), got


def odd_mixer_bwd(x, gmix, w_in, w_out, saved, dy, tag, comm=None):
    S, D = x.shape
    nh = D // HEAD_DIM
    h, qkv, o, tails = saved
    dy32, dy16 = dy
    do = mm(dy16, w_out, "nt", f32, f"{tag}_bdo", tn=1024)
    dw_out = mm(o, dy16, "tn", bf16, f"{tag}_bwout", tn=1024)
    (dq, dk, dv), got = _carried(sb_bwd(qkv, tails, do, nh, f"{tag}_bsb", comm=comm), comm)
    dqkv = jnp.concatenate([dq, dk, dv], axis=1).astype(bf16)
    dw_in = mm(h, dqkv, "tn", bf16, f"{tag}_bwin", tn=1536)
    dx, d_gmix = mm_bdh_rms(dqkv, w_in, x, gmix, dy32, f"{tag}_bdh")
    return dx, dw_in, dw_out, dict(norm_mix=d_gmix[0]), got


def all_gather(shards, axes, name, kind="ag"):
    return _call(None, name, (), [], [], [], comm=Comm(kind, shards, axes))()[1]


def sum_parts(parts, name):
    _, R, C = parts.shape
    tm = _tile(R, 256)

    def body(p_ref, o_ref):
        acc = p_ref[0].astype(f32)
        for k in range(1, NDEV):
            acc = acc + p_ref[k].astype(f32)
        o_ref[...] = acc

    return _call(body, name, (R // tm,), [pl.BlockSpec((NDEV, tm, C), lambda i: (0, i, 0))], _rowspec(tm, C), SDS((R, C), f32))(parts)


def adamw(w, g, m, v, name):
    L, R, C = w.shape
    tm = _tile(R, max(8, min(512, (ADAMW_TILE_ELEMS // C) // 8 * 8)))
    c1 = 1.0 - ADAM_B1 ** ADAM_STEP
    c2 = 1.0 - ADAM_B2 ** ADAM_STEP

    def body(w_ref, g_ref, m_ref, v_ref, d_ref, nm_ref, nv_ref):
        g_ = g_ref[...]
        nm = ADAM_B1 * m_ref[...] + (1.0 - ADAM_B1) * g_
        nv = ADAM_B2 * v_ref[...] + (1.0 - ADAM_B2) * (g_ * g_)
        d_ref[...] = -ADAM_LR * ((nm / c1) / (jnp.sqrt(nv / c2) + ADAM_EPS) + ADAM_WD * w_ref[...])
        nm_ref[...] = nm
        nv_ref[...] = nv

    spec = pl.BlockSpec((None, tm, C), lambda l, i: (l, i, 0))
    sh = SDS((L, R, C), f32)
    return _call(body, name, (L, R // tm), [spec] * 4, (spec, spec, spec), (sh, sh, sh))(w, g, m, v)


def _pad_to(n, mult):
    return -(-n // mult) * mult


def _even_perm(D):
    half = D // 2
    ndn = half // HEAD_DIM
    o_gate = 3 * half
    o_b = o_gate + half
    o_a = o_b + ndn
    o_fq = o_a + ndn
    o_fpre = o_fq + 4 * half
    return half, ndn, o_b, o_a, o_fq, o_fpre


def _even_to_mine(w):
    D = (w.shape[-1] // 4) // LANES * LANES
    half, ndn, o_b, o_a, o_fq, o_fpre = _even_perm(D)
    small = jnp.concatenate([w[..., o_b:o_b + ndn], w[..., o_a:o_a + ndn], w[..., o_fpre:o_fpre + ndn]], axis=-1)
    small = jnp.pad(small, [(0, 0)] * (w.ndim - 1) + [(0, LANES - 3 * ndn)])
    return jnp.concatenate([w[..., :o_b], w[..., o_fq:o_fpre], small], axis=-1)


def _even_from_mine(w, D):
    half, ndn, o_b, o_a, o_fq, o_fpre = _even_perm(D)
    sm = w[..., 4 * D:]
    return jnp.concatenate([w[..., :o_b], sm[..., :ndn], sm[..., ndn:2 * ndn], w[..., o_b:4 * D], sm[..., 2 * ndn:3 * ndn]], axis=-1)


def _pack_small(parts):
    flat = jnp.concatenate([p.reshape(-1).astype(f32) for p in parts])
    n = flat.shape[0]
    return jnp.pad(flat, (0, _pad_to(n, 8 * LANES) - n)).reshape(-1, LANES)


def _unpack_small(packed, like):
    flat = packed.reshape(-1)
    out, off = [], 0
    for p in like:
        out.append(flat[off:off + p.size].reshape(p.shape))
        off += p.size
    return out


SMALL_NAMES = ("norm_ffn1", "norm_mix", "dn_a_log", "dn_dt_bias", "dn_norm_g", "fox_q_norm_g", "fox_k_norm_g", "fox_f_bias", "norm_ffn2")
BIG_NAMES = ("ffn1_w_gu", "ffn1_w_down", "w_in_even", "dn_conv_w", "w_out_even", "w_in_odd", "w_out_odd", "ffn2_w_gu", "ffn2_w_down")
WEIGHT_NAMES = ("norm_ffn1", "ffn1_w_gu", "ffn1_w_down", "norm_mix", "w_in_even", "dn_conv_w", "dn_a_log", "dn_dt_bias", "dn_norm_g",
                "fox_q_norm_g", "fox_k_norm_g", "fox_f_bias", "w_out_even", "w_in_odd", "w_out_odd", "norm_ffn2", "ffn2_w_gu", "ffn2_w_down")


def kernel(x, norm_ffn1, ffn1_w_gu, ffn1_w_down, norm_mix, w_in_even, dn_conv_w, dn_a_log, dn_dt_bias, dn_norm_g, fox_q_norm_g, fox_k_norm_g, fox_f_bias, w_out_even, w_in_odd, w_out_odd, norm_ffn2, ffn2_w_gu, ffn2_w_down, loss_target, m_norm_ffn1, m_ffn1_w_gu, m_ffn1_w_down, m_norm_mix, m_w_in_even, m_dn_conv_w, m_dn_a_log, m_dn_dt_bias, m_dn_norm_g, m_fox_q_norm_g, m_fox_k_norm_g, m_fox_f_bias, m_w_out_even, m_w_in_odd, m_w_out_odd, m_norm_ffn2, m_ffn2_w_gu, m_ffn2_w_down, v_norm_ffn1, v_ffn1_w_gu, v_ffn1_w_down, v_norm_mix, v_w_in_even, v_dn_conv_w, v_dn_a_log, v_dn_dt_bias, v_dn_norm_g, v_fox_q_norm_g, v_fox_k_norm_g, v_fox_f_bias, v_w_out_even, v_w_in_odd, v_w_out_odd, v_norm_ffn2, v_ffn2_w_gu, v_ffn2_w_down):
    args = dict(locals())
    W = {n: args[n] for n in WEIGHT_NAMES}
    M = {n: args["m_" + n] for n in WEIGHT_NAMES}
    V = {n: args["v_" + n] for n in WEIGHT_NAMES}
    xs = x[0]
    tgt = loss_target[0]
    S, D = xs.shape
    L = norm_ffn1.shape[0]
    n_even = w_in_even.shape[0]
    hs = ffn1_w_down.shape[1]
    hp = _pad_to(hs, LANES)
    me = 4 * lax.axis_index("x") + 2 * lax.axis_index("y") + lax.axis_index("c")

    def prep_gu(w):
        w = w.reshape(L, D, 2, hs)
        return jnp.pad(w, ((0, 0), (0, 0), (0, 0), (0, hp - hs))).reshape(L, D, 2 * hp).astype(bf16)

    def prep_down(w):
        return jnp.pad(w, ((0, 0), (0, hp - hs), (0, 0))).astype(bf16)

    sh_gu1, sh_d1, sh_gu2, sh_d2 = prep_gu(ffn1_w_gu), prep_down(ffn1_w_down), prep_gu(ffn2_w_gu), prep_down(ffn2_w_down)
    sh_ie, sh_oe = _even_to_mine(w_in_even).astype(bf16), w_out_even.astype(bf16)
    sh_io, sh_oo = w_in_odd.astype(bf16), w_out_odd.astype(bf16)

    def layer_shards(l):
        j = l // 2
        mix = [sh_ie[j], sh_oe[j]] if l % 2 == 0 else [sh_io[j], sh_oo[j]]
        return [sh_gu1[l], sh_d1[l], *mix, sh_gu2[l], sh_d2[l]]

    def layer_axes(l):
        return [1, 0, 0 if l % 2 == 0 else 1, 0, 1, 0]

    def layer_names(l):
        return ["ffn1_w_gu", "ffn1_w_down", *(["w_in_even", "w_out_even"] if l % 2 == 0 else ["w_in_odd", "w_out_odd"]),
                "ffn2_w_gu", "ffn2_w_down"]

    FOX_CARRIES, DN_CARRIES = (0, 4, 3), (1, 2, 5)

    def split_comm(kind, arrays, axes):
        return (Comm(kind, [arrays[i] for i in FOX_CARRIES], [axes[i] for i in FOX_CARRIES]),
                Comm(kind, [arrays[i] for i in DN_CARRIES], [axes[i] for i in DN_CARRIES]))

    def join_split(got_fox, got_dn):
        out = [None] * 6
        for i, a in zip(FOX_CARRIES, got_fox):
            out[i] = a
        for i, a in zip(DN_CARRIES, got_dn):
            out[i] = a
        return out

    sh0, ax0 = layer_shards(0), layer_axes(0)
    first = all_gather(sh0[:2] + [dn_conv_w[None]], ax0[:2] + [0], "ag_layer0", kind="ag2")
    weights = {0: first[:2] + [None] * 4}
    convw = jnp.moveaxis(first[2], 0, 2).reshape(n_even, CONV_WIDTH, -1)
    LAYER0_CARRIES = dict(f1_gu=(2,), f1_down=(3,), mx_in=(5,))

    EVEN_FWD_CARRIES = dict(f1_gu=(), mx_in=(), dn=(1, 2, 5), fox=(0, 4, 3), f2_gu=(), f2_down=())
    saved = []
    cur = xs
    for l in range(L):
        j = l // 2
        wgu1, wd1, win, wout, wgu2, wd2 = weights[l]
        nxt = l + 1 < L
        x0 = cur
        if l % 2 == 0:
            sh_n, ax_n = (layer_shards(l + 1), layer_axes(l + 1)) if nxt else (None, None)
            cm = {k: (Comm("ag2", [sh_n[i] for i in idx], [ax_n[i] for i in idx]) if nxt and idx else None)
                  for k, idx in EVEN_FWD_CARRIES.items()}
            if l == 0:
                cm.update({k: Comm("ag2", [sh0[i] for i in idx], [ax0[i] for i in idx]) for k, idx in LAYER0_CARRIES.items()})
                cm["fox"] = Comm("ag2", cm["fox"].arrays + [sh0[4]], cm["fox"].axes + [ax0[4]])
            x1, s1, got_f1gu, got_f1down = ffn_fwd(x0, norm_ffn1[l:l + 1], wgu1, wd1, f"l{l}f1", comm_gu=cm["f1_gu"],
                                                   comm_down=cm["f1_down"] if l == 0 else None)
            if l == 0:
                win, wout = got_f1gu[0], got_f1down[0]
            x2, sm, got_f, got_d, got_in = even_mixer_fwd(
                x1, norm_mix[l:l + 1], win, wout, convw[j], dn_a_log[j], dn_dt_bias[j], dn_norm_g[j:j + 1],
                fox_q_norm_g[j:j + 1], fox_k_norm_g[j:j + 1], fox_f_bias[j], f"l{l}mx", comm_fox=cm["fox"], comm_dn=cm["dn"],
                comm_in=cm["mx_in"])
            if l == 0:
                wd2, wgu2 = got_in[0], got_f[-1]
                weights[0] = [wgu1, wd1, win, wout, wgu2, wd2]
                got_f1gu, got_in, got_f = [], [], got_f[:-1]
            x3, s2, got_f2gu, got_f2down = ffn_fwd(x2, norm_ffn2[l:l + 1], wgu2, wd2, f"l{l}f2", comm_gu=cm["f2_gu"],
                                                   comm_down=cm["f2_down"])
            if nxt:
                got = dict(f1_gu=got_f1gu, mx_in=got_in, dn=got_d, fox=got_f, f2_gu=got_f2gu, f2_down=got_f2down)
                weights[l + 1] = [None] * 6
                for k, idx in EVEN_FWD_CARRIES.items():
                    for i, a in zip(idx, got[k]):
                        weights[l + 1][i] = a
        else:
            x1, s1, _, _ = ffn_fwd(x0, norm_ffn1[l:l + 1], wgu1, wd1, f"l{l}f1")
            cm = Comm("ag2", layer_shards(l + 1), layer_axes(l + 1)) if nxt else None
            x2, sm, got = odd_mixer_fwd(x1, norm_mix[l:l + 1], win, wout, f"l{l}mx", comm=cm)
            if nxt:
                weights[l + 1] = got
            x3, s2, _, _ = ffn_fwd(x2, norm_ffn2[l:l + 1], wgu2, wd2, f"l{l}f2")
        saved.append((x0, x1, x2, s1, sm, s2))
        cur = x3
    dy, loss_row = loss_head(cur, tgt, "loss")

    gsmall = {n: [None] * W[n].shape[0] for n in SMALL_NAMES}
    gconv = [None] * n_even
    parts = {n: [None] * W[n].shape[0] for n in BIG_NAMES if n != "dn_conv_w"}

    def take(l, recv):
        for nm, p in zip(layer_names(l), recv):
            idx = l if nm.startswith("ffn") else l // 2
            parts[nm][idx] = sum_parts(p.reshape(NDEV, -1, p.shape[-1]), f"sum_{nm}_{idx}").reshape(p.shape[1:])

    pending = None
    for l in reversed(range(L)):
        j = l // 2
        wgu1, wd1, win, wout, wgu2, wd2 = weights[l]
        x0, x1, x2, s1, sm, s2 = saved[l]
        dy, dg, dwgu2, dwd2 = ffn_bwd(x2, norm_ffn2[l:l + 1], wgu2, wd2, s2, dy, f"l{l}f2")
        gsmall["norm_ffn2"][l] = dg[0]
        if l % 2 == 0:
            cf, cd = split_comm("rs", pending, layer_axes(l + 1)) if pending is not None else (None, None)
            cb = None
            if l == 0:
                cd = Comm("rs", cd.arrays + [dwd2], cd.axes + [ax0[5]])
                cb = Comm("rs", [dwgu2], [ax0[4]])
            dy, dwin, dwout, sg, got_f, got_d, got_b = even_mixer_bwd(
                x1, norm_mix[l:l + 1], win, wout, convw[j], dn_norm_g[j:j + 1], fox_q_norm_g[j:j + 1],
                fox_k_norm_g[j:j + 1], sm, dy, f"l{l}mx", comm_fox=cf, comm_dn=cd, comm_bdh=cb)
            if l == 0:
                got_wd2, got_wgu2, got_d = got_d[-1], got_b[0], got_d[:-1]
            if pending is not None:
                take(l + 1, join_split(got_f, got_d))
            gconv[j] = sg.pop("dn_conv_w")
            for k_, v_ in sg.items():
                gsmall[k_][l if k_ == "norm_mix" else j] = v_
        else:
            cm = Comm("rs", pending, layer_axes(l + 1)) if pending is not None else None
            dy, dwin, dwout, sg, got = odd_mixer_bwd(x1, norm_mix[l:l + 1], win, wout, sm, dy, f"l{l}mx", comm=cm)
            if pending is not None:
                take(l + 1, got)
            gsmall["norm_mix"][l] = sg["norm_mix"]
        if l > 0:
            dy, dg, dwgu1, dwd1 = ffn_bwd(x0, norm_ffn1[l:l + 1], wgu1, wd1, s1, dy, f"l{l}f1")
        else:
            rs = lambda a, ax: Comm("rs", [a], [ax])
            dy, dg, dwgu1, dwd1, got0 = ffn_bwd(x0, norm_ffn1[l:l + 1], wgu1, wd1, s1, dy, f"l{l}f1",
                                                comms=dict(bda=rs(dwin, ax0[2]), bwd=rs(dwout, ax0[3])), own_axes=(ax0[0], ax0[1]))
        gsmall["norm_ffn1"][l] = dg[0]
        pending = [dwgu1, dwd1, dwin, dwout, dwgu2, dwd2]
    take(0, [got0["wgu"][0], got0["wd"][0], got0["bda"][0], got0["bwd"][0], got_wgu2, got_wd2])
    grad_x = dy[0][None]

    small_list = [jnp.stack(gsmall[n]) for n in SMALL_NAMES] + [jnp.stack(gconv), loss_row[0, :1]]
    packed = _pack_small(small_list)
    gathered = all_gather([packed], [0], "ag_small")[0]
    summed = sum_parts(gathered.reshape(NDEV, packed.shape[0], LANES), "sum_small")
    small_sum = _unpack_small(summed, small_list)
    G = dict(zip(SMALL_NAMES, small_sum[:len(SMALL_NAMES)]))
    conv_full = small_sum[len(SMALL_NAMES)]
    cwid = dn_conv_w.shape[2]
    G["dn_conv_w"] = lax.dynamic_slice_in_dim(conv_full, me * cwid, cwid, axis=2)
    loss = small_sum[-1][0]

    def unprep_gu(g):
        return g.reshape(L, D, 2, hp)[..., :hs].reshape(L, D, 2 * hs)

    G["ffn1_w_gu"] = unprep_gu(jnp.stack(parts["ffn1_w_gu"]))
    G["ffn2_w_gu"] = unprep_gu(jnp.stack(parts["ffn2_w_gu"]))
    G["ffn1_w_down"] = jnp.stack(parts["ffn1_w_down"])[:, :hs]
    G["ffn2_w_down"] = jnp.stack(parts["ffn2_w_down"])[:, :hs]
    G["w_in_even"] = _even_from_mine(jnp.stack(parts["w_in_even"]), D)
    G["w_out_even"] = jnp.stack(parts["w_out_even"])
    G["w_in_odd"] = jnp.stack(parts["w_in_odd"])
    G["w_out_odd"] = jnp.stack(parts["w_out_odd"])

    delta, new_m, new_v = {}, {}, {}
    for n in BIG_NAMES:
        t = (lambda a: jnp.swapaxes(a, 1, 2)) if n.endswith("w_gu") else (lambda a: a)
        delta[n], new_m[n], new_v[n] = map(t, adamw(t(W[n]), t(G[n]), t(M[n]), t(V[n]), f"adamw_{n}"))
    sw = [W[n] for n in SMALL_NAMES]
    d_, m_, v_ = adamw(_pack_small(sw)[None], _pack_small([G[n] for n in SMALL_NAMES])[None],
                       _pack_small([M[n] for n in SMALL_NAMES])[None], _pack_small([V[n] for n in SMALL_NAMES])[None], "adamw_small")
    for n, a, b, c_ in zip(SMALL_NAMES, _unpack_small(d_, sw), _unpack_small(m_, sw), _unpack_small(v_, sw)):
        delta[n], new_m[n], new_v[n] = a, b, c_

    return (loss, grad_x, *[G[n] for n in WEIGHT_NAMES], *[delta[n] for n in WEIGHT_NAMES],
            *[new_m[n] for n in WEIGHT_NAMES], *[new_v[n] for n in WEIGHT_NAMES])
```

```python
import functools
import math

import jax
import jax.numpy as jnp
from jax import lax
from jax.experimental import pallas as pl
from jax.experimental.pallas import tpu as pltpu

f32 = jnp.float32
bf16 = jnp.bfloat16
SDS = jax.ShapeDtypeStruct

HEAD_DIM = 128
LANES = 128
CONV_WIDTH = 4
DN_CHUNK = 128
EPS = 1e-6
NDEV = 8
VMEM_LIMIT_BYTES = 56 * 1024 * 1024
HI = lax.Precision.HIGHEST
ADAMW_TILE_ELEMS = 384 * 1024

ADAM_LR = 0.001
ADAM_B1 = 0.9
ADAM_B2 = 0.999
ADAM_EPS = 1e-08
ADAM_WD = 0.01
ADAM_STEP = 10

NN = (((1,), (0,)), ((), ()))
NT = (((1,), (1,)), ((), ()))
TN = (((0,), (0,)), ((), ()))


def _me_and_peers():
    x, y, c = lax.axis_index("x"), lax.axis_index("y"), lax.axis_index("c")
    me = 4 * x + 2 * y + c
    peers = []
    for r in range(1, NDEV):
        px = 1 - x if (r >> 2) & 1 else x
        py = 1 - y if (r >> 1) & 1 else y
        pc = 1 - c if r & 1 else c
        peers.append(((px, py, pc), 4 * px + 2 * py + pc))
    return me, peers


def _slab(ref, axis, width, k):
    idx = [slice(None)] * len(ref.shape)
    idx[axis] = pl.ds(k * width, width)
    return ref.at[tuple(idx)]


class Comm:
    def __init__(self, kind, arrays, axes):
        self.kind, self.arrays, self.axes, self.n = kind, list(arrays), list(axes), len(arrays)

    def out_shape(self):
        out = []
        for a, ax in zip(self.arrays, self.axes):
            shp = list(a.shape)
            if self.kind != "rs":
                shp[ax] *= NDEV
                out.append(SDS(tuple(shp), a.dtype))
            else:
                shp[ax] //= NDEV
                out.append(SDS((NDEV, *shp), a.dtype))
        return out

    def sems(self):
        return [pltpu.SemaphoreType.DMA((self.n, NDEV - 1)), pltpu.SemaphoreType.DMA((self.n, NDEV - 1)),
                pltpu.SemaphoreType.DMA((self.n,))]

    def _src(self, ins, t, to_idx):
        if self.kind == "ag":
            return ins[t]
        return _slab(ins[t], self.axes[t], ins[t].shape[self.axes[t]] // NDEV, to_idx)

    def _dst(self, ins, outs, t, from_idx):
        if self.kind != "rs":
            return _slab(outs[t], self.axes[t], ins[t].shape[self.axes[t]], from_idx)
        return outs[t].at[from_idx]

    def _copies(self, ins, outs, sems, sending):
        send_sems, recv_sems, loc_sems = sems
        me, peers = _me_and_peers()
        local, remote = [], []
        for t in range(self.n):
            local.append(pltpu.make_async_copy(self._src(ins, t, me), self._dst(ins, outs, t, me), loc_sems.at[t]))
            for r, (peer, pidx) in enumerate(peers):
                remote.append(pltpu.make_async_remote_copy(
                    src_ref=self._src(ins, t, pidx), dst_ref=self._dst(ins, outs, t, me if sending else pidx),
                    send_sem=send_sems.at[t, r], recv_sem=recv_sems.at[t, r], device_id=peer,
                    device_id_type=pl.DeviceIdType.MESH))
        return local, remote

    def _two_level(self, ins, outs, sems, own=True):
        send_sems, recv_sems, loc_sems = sems
        x, y, c = lax.axis_index("x"), lax.axis_index("y"), lax.axis_index("c")
        me, sibling = (x, y, c), (x, y, 1 - c)
        chips = [(1 - x, y), (x, 1 - y), (1 - x, 1 - y)]
        dev = lambda p: 4 * p[0] + 2 * p[1] + p[2]

        def copy(t, k, block, to, from_shard):
            dst = self._dst(ins, outs, t, dev(block))
            return pltpu.make_async_remote_copy(
                src_ref=ins[t] if from_shard else dst, dst_ref=dst, send_sem=send_sems.at[t, k], recv_sem=recv_sems.at[t, k],
                device_id=to, device_id_type=pl.DeviceIdType.MESH)

        ts = range(self.n) if own else ()
        local = [pltpu.make_async_copy(ins[t], self._dst(ins, outs, t, dev(me)), loc_sems.at[t]) for t in ts]
        first = [copy(t, 0, me, sibling, True) for t in ts]
        first += [copy(t, 1 + j, me, (*chip, c), True) for t in ts for j, chip in enumerate(chips)]
        return local, first, copy, chips, me, sibling, c

    def start(self, ins, outs, sems):
        if self.kind == "ag2":
            local, first = self._two_level(ins, outs, sems)[:2]
            for cp in local + first:
                cp.start()
            return
        local, remote = self._copies(ins, outs, sems, True)
        for cp in local + remote:
            cp.start()

    def relay(self, ins, outs, sems):
        _, _, copy, chips, me, sibling, c = self._two_level(ins, outs, sems, own=False)
        for t in range(self.n):
            for j, chip in enumerate(chips):
                copy(t, 1 + j, (*chip, c), me, False).wait_recv()
                copy(t, 4 + j, (*chip, c), sibling, False).start()

    def wait(self, ins, outs, sems, relayed=False):
        if self.kind == "ag2":
            if not relayed:
                self.relay(ins, outs, sems)
            local, first, copy, chips, me, sibling, c = self._two_level(ins, outs, sems)
            passed = [copy(t, 4 + j, (*chip, c), sibling, False) for t in range(self.n) for j, chip in enumerate(chips)]
            for t in range(self.n):
                copy(t, 0, sibling, me, False).wait_recv()
                for j, chip in enumerate(chips):
                    copy(t, 4 + j, (*chip, 1 - c), me, False).wait_recv()
            for cp in first + passed:
                cp.wait_send()
            for cp in local:
                cp.wait()
            return
        local, remote = self._copies(ins, outs, sems, False)
        for cp in remote:
            cp.wait_recv()
            cp.wait_send()
        for cp in local:
            cp.wait()


def _call(body, name, grid, in_specs, out_specs, out_shape, scratch=(), comm=None):
    params = pltpu.CompilerParams(vmem_limit_bytes=VMEM_LIMIT_BYTES)
    if comm is None:
        return pl.pallas_call(body, name=name, grid=grid, in_specs=in_specs, out_specs=out_specs, out_shape=out_shape,
                              scratch_shapes=list(scratch), compiler_params=params)
    single = not isinstance(out_shape, (tuple, list))
    c_out_specs = [out_specs] if single else list(out_specs)
    c_out_shape = [out_shape] if single else list(out_shape)
    n_in, n_out, n_scr, n = len(in_specs), len(c_out_shape), len(scratch), comm.n
    anyspec = pl.BlockSpec(memory_space=pl.ANY)

    def wrapped(*refs):
        ins, refs = refs[:n_in], refs[n_in:]
        cins, refs = refs[:n], refs[n:]
        outs, refs = refs[:n_out], refs[n_out:]
        couts, refs = refs[:n], refs[n:]
        scr, sems = refs[:n_scr], refs[n_scr:]
        first = functools.reduce(lambda a, b: a & b, [pl.program_id(d) == 0 for d in range(len(grid))], True)
        last = functools.reduce(lambda a, b: a & b, [pl.program_id(d) == grid[d] - 1 for d in range(len(grid))], True)
        if grid:
            steps = math.prod(grid)
            step = functools.reduce(lambda a, d: a * grid[d] + pl.program_id(d), range(len(grid)), 0)
            relay_early = comm.kind == "ag2" and steps >= 4
            pl.when(first)(lambda: comm.start(cins, couts, sems))
            body(*ins, *outs, *scr)
            if relay_early:
                pl.when(step == (3 * steps) // 4)(lambda: comm.relay(cins, couts, sems))
            pl.when(last)(lambda: comm.wait(cins, couts, sems, relayed=relay_early))
        else:
            comm.start(cins, couts, sems)
            if body is not None:
                body(*ins, *outs, *scr)
            comm.wait(cins, couts, sems)

    call = pl.pallas_call(
        wrapped, name=name, grid=grid, in_specs=list(in_specs) + [anyspec] * n, out_specs=c_out_specs + [anyspec] * n,
        out_shape=c_out_shape + comm.out_shape(), scratch_shapes=list(scratch) + comm.sems(), compiler_params=params)

    def run(*args):
        res = call(*args, *comm.arrays)
        mine = res[:n_out]
        return (mine[0] if single else tuple(mine)), list(res[n_out:])

    return run


def _tile(n, want, align=8):
    if n <= want:
        return n
    for t in range(want // align * align, 0, -align):
        if n % t == 0:
            return t
    return n


def _dot(a, b, dims=NN, precision=None):
    return lax.dot_general(a, b, dims, preferred_element_type=f32, precision=precision)


def _logsig(x):
    return jnp.minimum(x, 0.0) - jnp.log(1.0 + jnp.exp(-jnp.abs(x)))


def _softplus(x):
    return jnp.maximum(x, 0.0) + jnp.log(1.0 + jnp.exp(-jnp.abs(x)))


def _rms(x, g):
    return x * lax.rsqrt(jnp.mean(x * x, axis=-1, keepdims=True) + EPS) * g


def _lane_pick(blk, lane_idx):
    lane = lax.broadcasted_iota(jnp.int32, (1, LANES), 1)
    return jnp.sum(jnp.where(lane == lane_idx, blk, 0.0), axis=1, keepdims=True)


def mm(a, b, mode, out_dtype, name, tm=512, tn=512, res=None, scale=1.0, comm=None):
    if mode == "nn":
        (M, K), (_, N) = a.shape, b.shape
    elif mode == "nt":
        (M, K), (N, _) = a.shape, b.shape
    else:
        (K, M), (_, N) = a.shape, b.shape
    tm, tn = _tile(M, tm, LANES), _tile(N, tn, LANES)
    a_spec = pl.BlockSpec((K, tm), lambda j, i: (0, i)) if mode == "tn" else pl.BlockSpec((tm, K), lambda j, i: (i, 0))
    b_spec = pl.BlockSpec((tn, K), lambda j, i: (j, 0)) if mode == "nt" else pl.BlockSpec((K, tn), lambda j, i: (0, j))
    dims = {"nn": NN, "nt": NT, "tn": TN}[mode]
    o_spec = pl.BlockSpec((tm, tn), lambda j, i: (i, j))

    def body(*refs):
        acc = _dot(refs[0][...].astype(bf16), refs[1][...].astype(bf16), dims)
        if scale != 1.0:
            acc = acc * scale
        if res is not None:
            acc = acc + refs[2][...]
        refs[-1][...] = acc.astype(out_dtype)

    ins, specs = [a, b], [a_spec, b_spec]
    if res is not None:
        ins.append(res)
        specs.append(o_spec)
    return _call(body, name, (N // tn, M // tm), specs, o_spec, SDS((M, N), out_dtype), comm=comm)(*ins)


def _rowspec(tm, w, cb=0):
    return pl.BlockSpec((tm, w), lambda i: (i, cb))


def _bspec(w):
    return pl.BlockSpec((1, w), lambda i: (0, 0))


def rms_fwd(x, g, name):
    S, D = x.shape
    tm = _tile(S, 512)

    def body(x_ref, g_ref, h_ref):
        h_ref[...] = _rms(x_ref[...], g_ref[...]).astype(bf16)

    return _call(body, name, (S // tm,), [_rowspec(tm, D), _bspec(D)], _rowspec(tm, D), SDS((S, D), bf16))(x, g)


def _swiglu(g, u):
    return g * jax.nn.sigmoid(g) * u


def ffn_gu_act(h, wgu, name, tm=1024, tn=768, comm=None):
    S, D = h.shape
    W = wgu.shape[1] // 2
    tm, tn = _tile(S, tm, LANES), _tile(W, tn, LANES)
    nb = W // tn

    def body(h_ref, wg_ref, wu_ref, gu_ref, a_ref):
        hb = h_ref[...]
        g = _dot(hb, wg_ref[...])
        u = _dot(hb, wu_ref[...])
        gu_ref[0] = g.astype(bf16)
        gu_ref[1] = u.astype(bf16)
        a_ref[...] = _swiglu(g, u).astype(bf16)

    return _call(body, name, (nb, S // tm),
                 [pl.BlockSpec((tm, D), lambda j, i: (i, 0)), pl.BlockSpec((D, tn), lambda j, i: (0, j)),
                  pl.BlockSpec((D, tn), lambda j, i: (0, nb + j))],
                 (pl.BlockSpec((2, tm, tn), lambda j, i: (0, i, j)), pl.BlockSpec((tm, tn), lambda j, i: (i, j))),
                 (SDS((2, S, W), bf16), SDS((S, W), bf16)), comm=comm)(h, wgu, wgu)


def ffn_bda_act(dy, wd, gu, scale, name, tm=1024, tn=768, comm=None):
    S, D = dy.shape
    W = wd.shape[0]
    tm, tn = _tile(S, tm, LANES), _tile(W, tn, LANES)

    def body(dy_ref, wd_ref, gu_ref, dgu_ref):
        da = _dot(dy_ref[...].astype(bf16), wd_ref[...], NT) * scale
        _, vjp = jax.vjp(_swiglu, gu_ref[0].astype(f32), gu_ref[1].astype(f32))
        dg, du = vjp(da)
        dgu_ref[0] = dg.astype(bf16)
        dgu_ref[1] = du.astype(bf16)

    planes = pl.BlockSpec((2, tm, tn), lambda j, i: (0, i, j))
    return _call(body, name, (W // tn, S // tm),
                 [pl.BlockSpec((tm, D), lambda j, i: (i, 0)), pl.BlockSpec((tn, D), lambda j, i: (j, 0)), planes],
                 planes, SDS((2, S, W), bf16), comm=comm)(dy, wd, gu)


def ffn_bwgu(h, dgu, name, tm=1024, tn=768, comm=None):
    S, D = h.shape
    W = dgu.shape[2]
    tm, tn = _tile(D, tm, LANES), _tile(W, tn, LANES)
    nb = W // tn

    def body(h_ref, d_ref, o_ref):
        o_ref[...] = _dot(h_ref[...], d_ref[...], TN).astype(bf16)

    return _call(body, name, (2 * nb, D // tm),
                 [pl.BlockSpec((S, tm), lambda j, i: (0, i)), pl.BlockSpec((None, S, tn), lambda j, i: (j // nb, 0, j % nb))],
                 pl.BlockSpec((tm, tn), lambda j, i: (i, j)), SDS((D, 2 * W), bf16), comm=comm)(h, dgu)


def nt_rms_bwd(pairs, x, g, dres, name, tm=512, comm=None):
    S, D = x.shape
    tm = _tile(S, tm)
    n = len(pairs)

    def body(*refs):
        x_ref, g_ref, dres_ref = refs[2 * n:2 * n + 3]
        dx_ref, dxb_ref, dg_ref = refs[2 * n + 3:]
        dh = sum(_dot(refs[2 * k][...].astype(bf16), refs[2 * k + 1][...], NT) for k in range(n))
        _, vjp = jax.vjp(_rms, x_ref[...], g_ref[...])
        dx, dg = vjp(dh)
        dx = dres_ref[...] + dx
        dx_ref[...] = dx
        dxb_ref[...] = dx.astype(bf16)

        @pl.when(pl.program_id(0) == 0)
        def _():
            dg_ref[...] = jnp.zeros_like(dg_ref)

        dg_ref[...] += dg

    arrays, specs = [], []
    for a, a_spec, b, b_spec in pairs:
        arrays += [a, b]
        specs += [a_spec, b_spec]
    (dx, dxb, dg), got = _carried(_call(body, name, (S // tm,), specs + [_rowspec(tm, D), _bspec(D), _rowspec(tm, D)],
                                        (_rowspec(tm, D), _rowspec(tm, D), _bspec(D)),
                                        (SDS((S, D), f32), SDS((S, D), bf16), SDS((1, D), f32)), comm=comm)(*arrays, x, g, dres),
                                  comm)
    return ((dx, dxb), dg) if comm is None else ((dx, dxb), dg, got)


def ffn_bdh_rms(dgu, wgu, x, g, dres, name, tm=512, comm=None):
    _, S, W = dgu.shape
    D = wgu.shape[0]
    tm = _tile(S, tm)
    pairs = [(dgu, pl.BlockSpec((None, tm, W), functools.partial(lambda p, i: (p, i, 0), p)),
              wgu, pl.BlockSpec((D, W), functools.partial(lambda p, i: (0, p), p))) for p in range(2)]
    return nt_rms_bwd(pairs, x, g, dres, name, tm, comm=comm)


def mm_bdh_rms(d, w, x, g, dres, name, tm=512, comm=None):
    S, K = d.shape
    tm = _tile(S, tm)
    return nt_rms_bwd([(d, pl.BlockSpec((tm, K), lambda i: (i, 0)), w, pl.BlockSpec(w.shape, lambda i: (0, 0)))], x, g, dres, name, tm,
                      comm=comm)


def loss_head(y, t, name):
    S, D = y.shape
    tm = _tile(S, 512)

    def body(y_ref, t_ref, dy_ref, dyb_ref, l_ref):
        e = y_ref[...] - t_ref[...]
        dy_ref[...] = e * (1.0 / D)
        dyb_ref[...] = (e * (1.0 / D)).astype(bf16)

        @pl.when(pl.program_id(0) == 0)
        def _():
            l_ref[...] = jnp.zeros_like(l_ref)

        l_ref[...] += jnp.sum(jnp.sum(e * e, axis=1, keepdims=True), axis=0, keepdims=True) * (0.5 / D)

    dy, dyb, loss = _call(body, name, (S // tm,), [_rowspec(tm, D), _rowspec(tm, D)],
                          (_rowspec(tm, D), _rowspec(tm, D), _bspec(LANES)),
                          (SDS((S, D), f32), SDS((S, D), bf16), SDS((1, LANES), f32)))(y, t)
    return (dy, dyb), loss


def ffn_fwd(x, g, wgu, wd, tag, comm_gu=None, comm_down=None):
    h = rms_fwd(x, g, f"{tag}_rms")
    (gu, a), got_gu = _carried(ffn_gu_act(h, wgu, f"{tag}_gu", comm=comm_gu), comm_gu)
    xo, got_down = _carried(mm(a, wd, "nn", f32, f"{tag}_down", tm=512, tn=1024, res=x, scale=0.5, comm=comm_down), comm_down)
    return xo, (h, gu, a), got_gu, got_down


def ffn_bwd(x, g, wgu, wd, saved, dy, tag, comms=None, own_axes=None):
    h, gu, a = saved
    dy32, dy16 = dy
    cm = comms or {}
    dgu, got_bda = _carried(ffn_bda_act(dy16, wd, gu, 0.5, f"{tag}_bda", comm=cm.get("bda")), cm.get("bda"))
    dwd, got_bwd = _carried(mm(a, dy16, "tn", bf16, f"{tag}_bwd", tm=512, tn=1024, scale=0.5, comm=cm.get("bwd")), cm.get("bwd"))
    c_wd = Comm("rs", [dwd], [own_axes[1]]) if own_axes else None
    dwgu, got_wd = _carried(ffn_bwgu(h, dgu, f"{tag}_bwgu", comm=c_wd), c_wd)
    c_wgu = Comm("rs", [dwgu], [own_axes[0]]) if own_axes else None
    res = ffn_bdh_rms(dgu, wgu, x, g, dy32, f"{tag}_bdh", comm=c_wgu)
    dx, dg = res[:2]
    if comms is None and own_axes is None:
        return dx, dg, dwgu, dwd
    return dx, dg, dwgu, dwd, dict(bda=got_bda, bwd=got_bwd, wd=got_wd, wgu=res[2] if c_wgu is not None else [])


def _split_bf16(x):
    hi = x.astype(bf16)
    lo = (x - hi.astype(f32)).astype(bf16)
    return hi, lo


SB_TQ, SB_TK, SB_NSUB = 512, 256, 4
FOX_TK = 256


def _sb_geometry(S, tk=SB_TK):
    tq = min(SB_TQ, S)
    tk = min(tk, tq)
    return tq, tk, tq // SB_NSUB, tq // tk


def _sb_consts(tk):
    r = lax.broadcasted_iota(jnp.int32, (tk, tk), 0)
    c = lax.broadcasted_iota(jnp.int32, (tk, tk), 1)
    return (r > c).astype(bf16), (r < c).astype(bf16)


def _active(d, nsub, rs, tk):
    return [s for s in range(nsub) if d is None or (s + 1) * rs > d * tk]


def _put(full, act, part):
    out = list(full)
    for s, x in zip(act, part):
        out[s] = x
    return out


def _sb_scores(qs, k, kb, tk, rows, masked):
    scale = HEAD_DIM ** -0.5
    z = [_dot(q, k, NT) * scale for q in qs]
    ls = [_logsig(z_) for z_ in z]
    lm = [l - z_ for l, z_ in zip(ls, z)]
    ok = None
    if masked:
        col = kb * tk + lax.broadcasted_iota(jnp.int32, z[0].shape, 1)
        ok = [col < row for row in rows]
        lm = [jnp.where(m, x, 0.0) for m, x in zip(ok, lm)]
    return ls, lm, ok


def _sb_weights(ls, lm, ok, tail, after):
    suf = [_dot(x.astype(bf16), after) for x in lm]
    a = [jnp.exp(l + s_ + t_) for l, s_, t_ in zip(ls, suf, tail)]
    if ok is not None:
        a = [jnp.where(m, x, 0.0) for m, x in zip(ok, a)]
    return a


def sb_fwd(qkv, nh, name, comm=None):
    S = qkv.shape[0]
    tq, tk, rs, nd = _sb_geometry(S)
    nsub = tq // rs

    def body(q_ref, k_ref, v_ref, o_ref, tails_ref):
        i = pl.program_id(1)
        after, _ = _sb_consts(tk)
        qs = [q_ref[pl.ds(s * rs, rs), :] for s in range(nsub)]
        rows = [i * tq + s * rs + lax.broadcasted_iota(jnp.int32, (rs, tk), 0) for s in range(nsub)]

        def tile(kb, carry, d):
            tail, acc = carry
            act = _active(d, nsub, rs, tk)
            pick = lambda lst: [lst[s] for s in act]
            off = pl.multiple_of(kb * tk, tk)
            k = k_ref[pl.ds(off, tk), :]
            v = v_ref[pl.ds(off, tk), :]
            ls, lm, ok = _sb_scores(pick(qs), k, kb, tk, pick(rows), d is not None)
            a = _sb_weights(ls, lm, ok, pick(tail), after)
            tails_ref[pl.ds(kb, 1), :] += jnp.concatenate([t_.T for t_ in tail], axis=1)
            acc = _put(acc, act, [ac + _dot(a_.astype(bf16), v) for ac, a_ in zip(pick(acc), a)])
            tail = _put(tail, act, [t_ + jnp.sum(x, axis=1, keepdims=True) for t_, x in zip(pick(tail), lm)])
            return tail, acc

        tails_ref[...] = jnp.zeros_like(tails_ref)
        carry = ([jnp.zeros((rs, 1), f32)] * nsub, [jnp.zeros((rs, HEAD_DIM), f32)] * nsub)
        for d in reversed(range(nd)):
            carry = tile(i * nd + d, carry, d)
        carry = lax.fori_loop(0, i * nd, lambda t, c: tile(i * nd - 1 - t, c, None), carry)
        for s in range(nsub):
            o_ref[pl.ds(s * rs, rs), :] = carry[1][s].astype(o_ref.dtype)

    blk = lambda cb0: pl.BlockSpec((tq, HEAD_DIM), lambda h, i: (i, cb0 + h))
    full = lambda cb0: pl.BlockSpec((S, HEAD_DIM), lambda h, i: (0, cb0 + h))
    tails = pl.BlockSpec((S // tk, tq), lambda h, i: (h, i))
    return _call(body, name, (nh, S // tq), [blk(0), full(nh), full(2 * nh)], (blk(0), tails),
                 (SDS((S, nh * HEAD_DIM), bf16), SDS((nh * (S // tk), S), f32)), comm=comm)(qkv, qkv, qkv)


def sb_bwd(qkv, tails, do, nh, name, comm=None):
    S = qkv.shape[0]
    tq, tk, rs, nd = _sb_geometry(S)
    nsub = tq // rs
    scale = HEAD_DIM ** -0.5

    def body(q_ref, k_ref, v_ref, tails_ref, do_ref, dq_ref, dk_ref, dv_ref):
        i = pl.program_id(1)

        @pl.when(i == 0)
        def _():
            dk_ref[...] = jnp.zeros_like(dk_ref)
            dv_ref[...] = jnp.zeros_like(dv_ref)

        after, before = _sb_consts(tk)
        qs = [q_ref[pl.ds(s * rs, rs), :] for s in range(nsub)]
        dos = [do_ref[pl.ds(s * rs, rs), :].astype(bf16) for s in range(nsub)]
        rows = [i * tq + s * rs + lax.broadcasted_iota(jnp.int32, (rs, tk), 0) for s in range(nsub)]

        def sweep2(kb, carry, d):
            pe, dq = carry
            act = _active(d, nsub, rs, tk)
            pick = lambda lst: [lst[s] for s in act]
            qa, da = pick(qs), pick(dos)
            off = pl.multiple_of(kb * tk, tk)
            k = k_ref[pl.ds(off, tk), :]
            v = v_ref[pl.ds(off, tk), :]
            ls, lm, ok = _sb_scores(qa, k, kb, tk, pick(rows), d is not None)
            tail_row = tails_ref[pl.ds(kb, 1), :]
            tail = [tail_row[:, s * rs:(s + 1) * rs].T for s in act]
            a = _sb_weights(ls, lm, ok, tail, after)
            e = [_dot(d_, v, NT) * a_ for d_, a_ in zip(da, a)]
            cum_e = [p_ + _dot(e_.astype(bf16), before) for p_, e_ in zip(pick(pe), e)]
            sig = [jnp.exp(l) for l in ls]
            dz = [e_ * (1.0 - sg) - c_ * sg for e_, sg, c_ in zip(e, sig, cum_e)]
            if ok is not None:
                dz = [jnp.where(m, x, 0.0) for m, x in zip(ok, dz)]
            dzb = [(x * scale).astype(bf16) for x in dz]
            dk_ref[pl.ds(off, tk), :] += sum(_dot(x, q, TN) for x, q in zip(dzb, qa))
            dv_ref[pl.ds(off, tk), :] += sum(_dot(a_.astype(bf16), d_, TN) for a_, d_ in zip(a, da))
            pe = _put(pe, act, [p_ + jnp.sum(e_, axis=1, keepdims=True) for p_, e_ in zip(pick(pe), e)])
            dq = _put(dq, act, [g + _dot(x, k) for g, x in zip(pick(dq), dzb)])
            return pe, dq

        carry = ([jnp.zeros((rs, 1), f32)] * nsub, [jnp.zeros((rs, HEAD_DIM), f32)] * nsub)
        carry = lax.fori_loop(0, i * nd, lambda kb, c: sweep2(kb, c, None), carry)
        for d in range(nd):
            carry = sweep2(i * nd + d, carry, d)
        for s in range(nsub):
            dq_ref[pl.ds(s * rs, rs), :] = carry[1][s]

    blk = lambda cb0: pl.BlockSpec((tq, HEAD_DIM), lambda h, i: (i, cb0 + h))
    full = lambda cb0: pl.BlockSpec((S, HEAD_DIM), lambda h, i: (0, cb0 + h))
    sh = SDS((S, nh * HEAD_DIM), f32)
    tails_spec = pl.BlockSpec((S // tk, tq), lambda h, i: (h, i))
    return _call(body, name, (nh, S // tq), [blk(0), full(nh), full(2 * nh), tails_spec, blk(0)], (blk(0), full(0), full(0)),
                 (sh, sh, sh), comm=comm)(qkv, qkv, qkv, tails, do)


def fox_fwd(fq, fk, fv, c, cT, nh, lane0, name, comm=None):
    S = fq.shape[0]
    tq, tk, rs, nd = _sb_geometry(S, FOX_TK)
    nsub = tq // rs
    scale = HEAD_DIM ** -0.5

    def body(q_ref, k_ref, v_ref, c_ref, ct_ref, o_ref, lse_ref):
        h = pl.program_id(0)
        i = pl.program_id(1)
        subs = range(nsub)
        qs = [q_ref[pl.ds(s * rs, rs), :] for s in subs]
        ci = [_lane_pick(c_ref[pl.ds(s * rs, rs), :], lane0 + h) for s in subs]
        rows = [i * tq + s * rs + lax.broadcasted_iota(jnp.int32, (rs, tk), 0) for s in subs]

        def tile(kb, carry, d):
            m, l, acc = carry
            act = _active(d, nsub, rs, tk)
            pick = lambda lst: [lst[s] for s in act]
            off = pl.multiple_of(kb * tk, tk)
            k = k_ref[pl.ds(off, tk), :]
            v = v_ref[pl.ds(off, tk), :]
            cj = ct_ref[pl.ds(lane0 % 8 + h, 1), pl.ds(off, tk)]
            sc = [_dot(q, k, NT) * scale + (c_ - cj) for q, c_ in zip(pick(qs), pick(ci))]
            if d is not None:
                col = kb * tk + lax.broadcasted_iota(jnp.int32, (rs, tk), 1)
                sc = [jnp.where(col <= row, x, -jnp.inf) for x, row in zip(sc, pick(rows))]
            m2 = [jnp.maximum(m_, jnp.max(x, axis=1, keepdims=True)) for m_, x in zip(pick(m), sc)]
            p = [jnp.exp(x - m_) for x, m_ in zip(sc, m2)]
            al = [jnp.exp(a - b) for a, b in zip(pick(m), m2)]
            l = _put(l, act, [a * l_ + jnp.sum(p_, axis=1, keepdims=True) for a, l_, p_ in zip(al, pick(l), p)])
            acc = _put(acc, act, [a * ac + _dot(p_.astype(bf16), v) for a, ac, p_ in zip(al, pick(acc), p)])
            return _put(m, act, m2), l, acc

        carry = ([jnp.full((rs, 1), -jnp.inf, f32)] * nsub, [jnp.zeros((rs, 1), f32)] * nsub,
                 [jnp.zeros((rs, HEAD_DIM), f32)] * nsub)
        for d in range(nd):
            carry = tile(i * nd + d, carry, d)
        m, l, acc = lax.fori_loop(0, i * nd, lambda kb, cr: tile(kb, cr, None), carry)
        for s in subs:
            o_ref[pl.ds(s * rs, rs), :] = acc[s] / l[s]
            lse_ref[pl.ds(s * rs, rs), :] = jnp.broadcast_to(m[s] + jnp.log(l[s]), (rs, HEAD_DIM))

    blk = pl.BlockSpec((tq, HEAD_DIM), lambda h, i: (i, h))
    full = pl.BlockSpec((S, HEAD_DIM), lambda h, i: (0, h))
    cspec = pl.BlockSpec((tq, LANES), lambda h, i: (i, 0))
    ctspec = pl.BlockSpec((8, S), lambda h, i: (lane0 // 8, 0))
    sh = SDS((S, nh * HEAD_DIM), f32)
    return _call(body, name, (nh, S // tq), [blk, full, full, cspec, ctspec], (blk, blk), (sh, sh), comm=comm)(fq, fk, fv, c, cT)


def fox_bwd(fq, fk, fv, c, cT, o, lse, do, nh, lane0, name, comm=None):
    S = fq.shape[0]
    tq, tk, rs, nd = _sb_geometry(S, FOX_TK)
    nsub = tq // rs
    scale = HEAD_DIM ** -0.5

    def body(q_ref, k_ref, v_ref, c_ref, ct_ref, o_ref, lse_ref, do_ref, dq_ref, dk_ref, dv_ref, dct_ref):
        h = pl.program_id(0)
        i = pl.program_id(1)

        @pl.when(i == 0)
        def _():
            dk_ref[...] = jnp.zeros_like(dk_ref)
            dv_ref[...] = jnp.zeros_like(dv_ref)

        @pl.when((i == 0) & (h == 0))
        def _():
            dct_ref[...] = jnp.zeros_like(dct_ref)

        subs = range(nsub)
        qs = [q_ref[pl.ds(s * rs, rs), :] for s in subs]
        dof = [do_ref[pl.ds(s * rs, rs), :] for s in subs]
        dos = [x.astype(bf16) for x in dof]
        ci = [_lane_pick(c_ref[pl.ds(s * rs, rs), :], lane0 + h) for s in subs]
        lses = [lse_ref[pl.ds(s * rs, rs), :][:, :1] for s in subs]
        dl = [jnp.sum(x * o_ref[pl.ds(s * rs, rs), :], axis=1, keepdims=True) for s, x in zip(subs, dof)]
        rows = [i * tq + s * rs + lax.broadcasted_iota(jnp.int32, (rs, tk), 0) for s in subs]

        def tile(kb, carry, d):
            dq, rsum = carry
            act = _active(d, nsub, rs, tk)
            pick = lambda lst: [lst[s] for s in act]
            qa, da = pick(qs), pick(dos)
            off = pl.multiple_of(kb * tk, tk)
            k = k_ref[pl.ds(off, tk), :]
            v = v_ref[pl.ds(off, tk), :]
            cj = ct_ref[pl.ds(lane0 % 8 + h, 1), pl.ds(off, tk)]
            p = [jnp.exp(_dot(q, k, NT) * scale + (c_ - cj) - ls) for q, c_, ls in zip(qa, pick(ci), pick(lses))]
            if d is not None:
                col = kb * tk + lax.broadcasted_iota(jnp.int32, (rs, tk), 1)
                p = [jnp.where(col <= row, x, 0.0) for x, row in zip(p, pick(rows))]
            ds = [p_ * (_dot(d_, v, NT) - dl_) for p_, d_, dl_ in zip(p, da, pick(dl))]
            dsb = [(x * scale).astype(bf16) for x in ds]
            dk_ref[pl.ds(off, tk), :] += sum(_dot(x, q, TN) for x, q in zip(dsb, qa))
            dv_ref[pl.ds(off, tk), :] += sum(_dot(p_.astype(bf16), d_, TN) for p_, d_ in zip(p, da))
            dct_ref[pl.ds(lane0 + h, 1), pl.ds(off, tk)] -= sum(jnp.sum(x, axis=0, keepdims=True) for x in ds)
            return (_put(dq, act, [g + _dot(x, k) for g, x in zip(pick(dq), dsb)]),
                    _put(rsum, act, [r_ + jnp.sum(x, axis=1, keepdims=True) for r_, x in zip(pick(rsum), ds)]))

        carry = ([jnp.zeros((rs, HEAD_DIM), f32)] * nsub, [jnp.zeros((rs, 1), f32)] * nsub)
        carry = lax.fori_loop(0, i * nd, lambda kb, cr: tile(kb, cr, None), carry)
        for d in range(nd):
            carry = tile(i * nd + d, carry, d)
        dq, rsum = carry
        for s in subs:
            dq_ref[pl.ds(s * rs, rs), :] = dq[s]
        dct_ref[pl.ds(lane0 + h, 1), pl.ds(pl.multiple_of(i * tq, tq), tq)] += jnp.concatenate([r_.T for r_ in rsum], axis=1)

    blk = pl.BlockSpec((tq, HEAD_DIM), lambda h, i: (i, h))
    full = pl.BlockSpec((S, HEAD_DIM), lambda h, i: (0, h))
    cspec = pl.BlockSpec((tq, LANES), lambda h, i: (i, 0))
    ctspec = pl.BlockSpec((8, S), lambda h, i: (lane0 // 8, 0))
    dctspec = pl.BlockSpec((LANES, S), lambda h, i: (0, 0))
    sh = SDS((S, nh * HEAD_DIM), f32)
    return _call(body, name, (nh, S // tq), [blk, full, full, cspec, ctspec, blk, blk, blk], (blk, full, full, dctspec),
                 (sh, sh, sh, SDS((LANES, S), f32)), comm=comm)(fq, fk, fv, c, cT, o, lse, do)


def gates_fwd(proj, small_cb, fbias_row, name, tm=256):
    S = proj.shape[0]
    tm = _tile(S, tm)

    def body(sm_ref, fb_ref, c_ref, ct_ref, carry_ref):
        @pl.when(pl.program_id(0) == 0)
        def _():
            carry_ref[...] = jnp.zeros_like(carry_ref)

        lf = _logsig(sm_ref[...] + fb_ref[...])
        r = lax.broadcasted_iota(jnp.int32, (tm, tm), 0)
        cc = lax.broadcasted_iota(jnp.int32, (tm, tm), 1)
        c = _dot((r >= cc).astype(f32), lf, NN, HI) + carry_ref[...]
        carry_ref[...] += jnp.sum(lf, axis=0, keepdims=True)
        c_ref[...] = c
        ct_ref[...] = c.T

    return _call(body, name, (S // tm,), [_rowspec(tm, LANES, small_cb), _bspec(LANES)],
                 (_rowspec(tm, LANES), pl.BlockSpec((LANES, tm), lambda i: (0, i))),
                 (SDS((S, LANES), f32), SDS((LANES, S), f32)), scratch=[pltpu.VMEM((1, LANES), f32)])(proj, fbias_row)


def gates_bwd(proj, small_cb, fbias_row, dcT, dsm_dn, lane0, nh, name, tm=256):
    S = proj.shape[0]
    tm = _tile(S, tm)
    nt = S // tm

    def body(sm_ref, fb_ref, dct_ref, dsm_ref, o_ref, dfb_ref, carry_ref):
        @pl.when(pl.program_id(0) == 0)
        def _():
            carry_ref[...] = jnp.zeros_like(carry_ref)
            dfb_ref[...] = jnp.zeros_like(dfb_ref)

        dc = dct_ref[...].T
        r = lax.broadcasted_iota(jnp.int32, (tm, tm), 0)
        cc = lax.broadcasted_iota(jnp.int32, (tm, tm), 1)
        dlf = _dot((r <= cc).astype(f32), dc, NN, HI) + carry_ref[...]
        carry_ref[...] += jnp.sum(dc, axis=0, keepdims=True)
        lane = lax.broadcasted_iota(jnp.int32, (1, LANES), 1)
        dpre = jnp.where((lane >= lane0) & (lane < lane0 + nh), dlf * jax.nn.sigmoid(-(sm_ref[...] + fb_ref[...])), 0.0)
        o_ref[...] = dsm_ref[...] + dpre
        dfb_ref[...] += jnp.sum(dpre, axis=0, keepdims=True)

    rev = lambda i: nt - 1 - i
    return _call(body, name, (nt,),
                 [pl.BlockSpec((tm, LANES), lambda i: (rev(i), small_cb)), _bspec(LANES),
                  pl.BlockSpec((LANES, tm), lambda i: (0, rev(i))), pl.BlockSpec((tm, LANES), lambda i: (rev(i), 0))],
                 (pl.BlockSpec((tm, LANES), lambda i: (rev(i), 0)), _bspec(LANES)),
                 (SDS((S, LANES), f32), SDS((1, LANES), f32)), scratch=[pltpu.VMEM((1, LANES), f32)])(proj, fbias_row, dcT, dsm_dn)


PAD_ROWS = 8


def conv_fwd(proj, w, width, name):
    S = proj.shape[0]
    cw = 256 if width % 256 == 0 else 128

    def body(x_ref, w_ref, y_ref, pad_ref):
        pad_ref[pl.ds(0, PAD_ROWS), :] = jnp.zeros((PAD_ROWS, cw), f32)
        pad_ref[pl.ds(PAD_ROWS, S), :] = x_ref[...]
        y = jnp.zeros((S, cw), f32)
        for i in range(CONV_WIDTH):
            y = y + pad_ref[pl.ds(PAD_ROWS - (CONV_WIDTH - 1) + i, S), :] * w_ref[pl.ds(i, 1), :]
        y_ref[...] = y * jax.nn.sigmoid(y)

    spec = pl.BlockSpec((S, cw), lambda j: (0, j))
    return _call(body, name, (width // cw,), [spec, pl.BlockSpec((CONV_WIDTH, cw), lambda j: (0, j))], spec,
                 SDS((S, width), f32), scratch=[pltpu.VMEM((S + PAD_ROWS, cw), f32)])(proj, w)


def conv_bwd(proj, w, dy, name):
    S, width = dy.shape
    cw = 256 if width % 256 == 0 else 128

    def body(x_ref, w_ref, dy_ref, dx_ref, dw_ref, xpad_ref, dpad_ref):
        xpad_ref[pl.ds(0, PAD_ROWS), :] = jnp.zeros((PAD_ROWS, cw), f32)
        xpad_ref[pl.ds(PAD_ROWS, S), :] = x_ref[...]
        y = jnp.zeros((S, cw), f32)
        for i in range(CONV_WIDTH):
            y = y + xpad_ref[pl.ds(PAD_ROWS - (CONV_WIDTH - 1) + i, S), :] * w_ref[pl.ds(i, 1), :]
        sg = jax.nn.sigmoid(y)
        dpre = dy_ref[...] * (sg * (1.0 + y * (1.0 - sg)))
        dpad_ref[pl.ds(S, PAD_ROWS), :] = jnp.zeros((PAD_ROWS, cw), f32)
        dpad_ref[pl.ds(0, S), :] = dpre
        dx = jnp.zeros((S, cw), f32)
        for i in range(CONV_WIDTH):
            dx = dx + dpad_ref[pl.ds(CONV_WIDTH - 1 - i, S), :] * w_ref[pl.ds(i, 1), :]
            dw_ref[pl.ds(i, 1), :] = jnp.sum(dpre * xpad_ref[pl.ds(PAD_ROWS - (CONV_WIDTH - 1) + i, S), :], axis=0, keepdims=True)
        dx_ref[...] = dx

    spec = pl.BlockSpec((S, cw), lambda j: (0, j))
    wspec = pl.BlockSpec((CONV_WIDTH, cw), lambda j: (0, j))
    return _call(body, name, (width // cw,), [spec, wspec, spec], (spec, wspec),
                 (SDS((S, width), f32), SDS((CONV_WIDTH, width), f32)),
                 scratch=[pltpu.VMEM((S + PAD_ROWS, cw), f32), pltpu.VMEM((S + PAD_ROWS, cw), f32)])(proj, w, dy)


def _pdot_raw(a, b, dims, passes):
    if passes == 1:
        return _dot(a.astype(bf16), b.astype(bf16), dims)
    ah, al = _split_bf16(a)
    bh, bl = _split_bf16(b)
    return _dot(ah, bh, dims) + (_dot(ah, bl, dims) + _dot(al, bh, dims))


def _make_pdot(passes):
    @functools.partial(jax.custom_vjp, nondiff_argnums=(2,))
    def pdot(a, b, mode):
        return _pdot_raw(a, b, {"nn": NN, "nt": NT, "tn": TN}[mode], passes)

    def fwd(a, b, mode):
        return pdot(a, b, mode), (a, b)

    def bwd(mode, res, g):
        a, b = res
        if mode == "nn":
            return _pdot_raw(g, b, NT, passes), _pdot_raw(a, g, TN, passes)
        if mode == "nt":
            return _pdot_raw(g, b, NN, passes), _pdot_raw(g, a, TN, passes)
        return _pdot_raw(b, g, NT, passes), _pdot_raw(a, g, NN, passes)

    pdot.defvjp(fwd, bwd)
    return pdot


_dot1 = _make_pdot(1)
_dot3 = _make_pdot(3)


def _each(f, *lists):
    return [f(*xs) for xs in zip(*lists)]


def _dn_chunk(ndn, cq, ck, cv, small, alog, dtb, s0):
    C = cq[0].shape[0]
    hs = list(range(ndn))
    b = [_lane_pick(small, h) for h in hs]
    d = [_lane_pick(small, ndn + h) for h in hs]
    al = [_lane_pick(alog, h) for h in hs]
    dt = [_lane_pick(dtb, h) for h in hs]
    q = _each(lambda x: x * lax.rsqrt(jnp.sum(x * x, axis=1, keepdims=True) + EPS) * (HEAD_DIM ** -0.5), cq)
    k = _each(lambda x: x * lax.rsqrt(jnp.sum(x * x, axis=1, keepdims=True) + EPS), ck)
    beta = _each(jax.nn.sigmoid, b)
    g = _each(lambda al_, d_, dt_: -jnp.exp(al_) * _softplus(d_ + dt_), al, d, dt)
    r = lax.broadcasted_iota(jnp.int32, (C, C), 0)
    c = lax.broadcasted_iota(jnp.int32, (C, C), 1)
    incl, strict = r >= c, r > c
    inclf = incl.astype(f32)
    gc = _each(lambda g_: _dot3(inclf, g_, "nn"), g)
    decay = _each(lambda gc_: jnp.where(incl, jnp.exp(jnp.where(incl, gc_ - gc_.T, 0.0)), 0.0), gc)
    kb = _each(lambda k_, b_: k_ * b_, k, beta)
    a = _each(lambda kb_, k_, dec: jnp.where(strict, _dot3(kb_, k_, "nt") * dec, 0.0), kb, k, decay)
    eye = (r == c).astype(f32)
    t = _each(lambda a_: eye - a_, a)
    p = a
    for _ in range(int(math.log2(C)) - 1):
        p = _each(lambda p_: _dot3(p_, p_, "nn"), p)
        t = _each(lambda t_, p_: t_ + _dot3(t_, p_, "nn"), t, p)
    eg = _each(jnp.exp, gc)
    u = _each(lambda t_, v_, b_: _dot3(t_, v_ * b_, "nn"), t, cv, beta)
    w = _each(lambda t_, kb_, eg_: _dot3(t_, kb_ * eg_, "nn"), t, kb, eg)
    attn = _each(lambda q_, k_, dec: _dot1(q_, k_, "nt") * dec, q, k, decay)
    g_last = _each(lambda g_: jnp.sum(g_, axis=0, keepdims=True), g)
    v_new = _each(lambda u_, w_, s_: u_ - _dot1(w_, s_, "nn"), u, w, s0)
    o = _each(lambda q_, eg_, s_, at, vn: _dot1(q_ * eg_, s_, "nn") + _dot1(at, vn, "nn"), q, eg, s0, attn, v_new)
    s1 = _each(lambda s_, gl, k_, gc_, vn: s_ * jnp.exp(gl) + _dot1(k_ * jnp.exp(gl - gc_), vn, "tn"), s0, g_last, k, gc, v_new)
    return o, s1


def dn_fwd(cqkv, proj, small_cb, alog_row, dt_row, ndn, name, comm=None):
    S = cqkv.shape[0]
    C = DN_CHUNK
    nc = S // C
    half = ndn * HEAD_DIM

    def body(q_ref, k_ref, v_ref, sm_ref, al_ref, dt_ref, o_ref, ss_ref, s_ref):
        @pl.when(pl.program_id(0) == 0)
        def _():
            s_ref[...] = jnp.zeros_like(s_ref)

        sls = [slice(h * HEAD_DIM, (h + 1) * HEAD_DIM) for h in range(ndn)]
        s0 = [s_ref[h] for h in range(ndn)]
        for h in range(ndn):
            ss_ref[h, 0] = s0[h]
        o, s1 = _dn_chunk(ndn, [q_ref[:, sl] for sl in sls], [k_ref[:, sl] for sl in sls], [v_ref[:, sl] for sl in sls],
                          sm_ref[...], al_ref[...], dt_ref[...], s0)
        for h in range(ndn):
            o_ref[:, sls[h]] = o[h]
            s_ref[h] = s1[h]

    blk = lambda cb: pl.BlockSpec((C, half), lambda n: (n, cb))
    row = pl.BlockSpec((1, LANES), lambda n: (0, 0))
    return _call(body, name, (nc,),
                 [blk(0), blk(1), blk(2), pl.BlockSpec((C, LANES), lambda n: (n, small_cb)), row, row],
                 (blk(0), pl.BlockSpec((ndn, 1, HEAD_DIM, HEAD_DIM), lambda n: (0, n, 0, 0))),
                 (SDS((S, half), f32), SDS((ndn, nc, HEAD_DIM, HEAD_DIM), f32)),
                 scratch=[pltpu.VMEM((ndn, HEAD_DIM, HEAD_DIM), f32)], comm=comm)(cqkv, cqkv, cqkv, proj, alog_row, dt_row)


def dn_bwd(cqkv, proj, small_cb, alog_row, dt_row, ssave, do, ndn, name, comm=None):
    S = cqkv.shape[0]
    C = DN_CHUNK
    nc = S // C
    half = ndn * HEAD_DIM

    def body(q_ref, k_ref, v_ref, sm_ref, al_ref, dt_ref, ss_ref, do_ref, dqkv_ref, dsm_ref, dal_ref, ddt_ref, ds_ref):
        @pl.when(pl.program_id(0) == 0)
        def _():
            ds_ref[...] = jnp.zeros_like(ds_ref)
            dal_ref[...] = jnp.zeros_like(dal_ref)
            ddt_ref[...] = jnp.zeros_like(ddt_ref)

        sls = [slice(h * HEAD_DIM, (h + 1) * HEAD_DIM) for h in range(ndn)]
        _, vjp = jax.vjp(functools.partial(_dn_chunk, ndn), [q_ref[:, sl] for sl in sls], [k_ref[:, sl] for sl in sls],
                         [v_ref[:, sl] for sl in sls], sm_ref[...], al_ref[...], dt_ref[...], [ss_ref[h, 0] for h in range(ndn)])
        dq, dk, dv, dsm, dal, ddt, ds0 = vjp(([do_ref[:, sl] for sl in sls], [ds_ref[h] for h in range(ndn)]))
        for h in range(ndn):
            dqkv_ref[:, sls[h]] = dq[h]
            dqkv_ref[:, half + h * HEAD_DIM:half + (h + 1) * HEAD_DIM] = dk[h]
            dqkv_ref[:, 2 * half + h * HEAD_DIM:2 * half + (h + 1) * HEAD_DIM] = dv[h]
            ds_ref[h] = ds0[h]
        dsm_ref[...] = dsm
        dal_ref[...] += dal
        ddt_ref[...] += ddt

    rev = lambda n: nc - 1 - n
    blk = lambda cb: pl.BlockSpec((C, half), lambda n: (rev(n), cb))
    row = pl.BlockSpec((1, LANES), lambda n: (0, 0))
    return _call(body, name, (nc,),
                 [blk(0), blk(1), blk(2), pl.BlockSpec((C, LANES), lambda n: (rev(n), small_cb)), row, row,
                  pl.BlockSpec((ndn, 1, HEAD_DIM, HEAD_DIM), lambda n: (0, rev(n), 0, 0)), blk(0)],
                 (pl.BlockSpec((C, 3 * half), lambda n: (rev(n), 0)), pl.BlockSpec((C, LANES), lambda n: (rev(n), 0)), row, row),
                 (SDS((S, 3 * half), f32), SDS((S, LANES), f32), SDS((1, LANES), f32), SDS((1, LANES), f32)),
                 scratch=[pltpu.VMEM((ndn, HEAD_DIM, HEAD_DIM), f32)], comm=comm)(cqkv, cqkv, cqkv, proj, alog_row, dt_row, ssave, do)


def even_pre(proj, gq, gk, dfx, cb0, name):
    S = proj.shape[0]
    tm = _tile(S, 512)
    nh = dfx // HEAD_DIM

    def body(q_ref, k_ref, v_ref, gq_ref, gk_ref, fq_ref, fk_ref, fv_ref):
        for h in range(nh):
            sl = slice(h * HEAD_DIM, (h + 1) * HEAD_DIM)
            fq_ref[:, sl] = _rms(q_ref[:, sl], gq_ref[...]).astype(bf16)
            fk_ref[:, sl] = _rms(k_ref[:, sl], gk_ref[...]).astype(bf16)
        fv_ref[...] = v_ref[...].astype(bf16)

    sh = SDS((S, dfx), bf16)
    o = _rowspec(tm, dfx)
    return _call(body, name, (S // tm,),
                 [_rowspec(tm, dfx, cb0), _rowspec(tm, dfx, cb0 + 1), _rowspec(tm, dfx, cb0 + 2), _bspec(HEAD_DIM), _bspec(HEAD_DIM)],
                 (o, o, o), (sh, sh, sh))(proj, proj, proj, gq, gk)


def even_pre_bwd(proj, gq, gk, dfq, dfk, dfx, cb0, name):
    S = proj.shape[0]
    tm = _tile(S, 512)
    nh = dfx // HEAD_DIM

    def body(q_ref, k_ref, gq_ref, gk_ref, dfq_ref, dfk_ref, dq_ref, dk_ref, dgq_ref, dgk_ref):
        @pl.when(pl.program_id(0) == 0)
        def _():
            dgq_ref[...] = jnp.zeros_like(dgq_ref)
            dgk_ref[...] = jnp.zeros_like(dgk_ref)

        for h in range(nh):
            sl = slice(h * HEAD_DIM, (h + 1) * HEAD_DIM)
            _, vq = jax.vjp(_rms, q_ref[:, sl], gq_ref[...])
            dq, dgq = vq(dfq_ref[:, sl])
            _, vk = jax.vjp(_rms, k_ref[:, sl], gk_ref[...])
            dk, dgk = vk(dfk_ref[:, sl])
            dq_ref[:, sl] = dq
            dk_ref[:, sl] = dk
            dgq_ref[...] += dgq
            dgk_ref[...] += dgk

    sh = SDS((S, dfx), f32)
    o = _rowspec(tm, dfx)
    g = SDS((1, HEAD_DIM), f32)
    return _call(body, name, (S // tm,),
                 [_rowspec(tm, dfx, cb0), _rowspec(tm, dfx, cb0 + 1), _bspec(HEAD_DIM), _bspec(HEAD_DIM), o, o],
                 (o, o, _bspec(HEAD_DIM), _bspec(HEAD_DIM)), (sh, sh, g, g))(proj, proj, gq, gk, dfq, dfk)


def _dn_out(o, gate, gn):
    return _rms(o, gn) * (gate * jax.nn.sigmoid(gate))


def _fox_out(o, gate):
    return o * jax.nn.sigmoid(gate)


def even_post(o_dn, o_fox, proj, gn, half, cb_dn_gate, cb_fox_gate, name):
    S = proj.shape[0]
    tm = _tile(S, 512)
    nh = half // HEAD_DIM

    def body(od_ref, of_ref, gd_ref, gf_ref, gn_ref, o_ref):
        for h in range(nh):
            sl = slice(h * HEAD_DIM, (h + 1) * HEAD_DIM)
            o_ref[:, sl] = _dn_out(od_ref[:, sl], gd_ref[:, sl], gn_ref[...]).astype(bf16)
        o_ref[:, half:] = _fox_out(of_ref[...], gf_ref[...]).astype(bf16)

    return _call(body, name, (S // tm,),
                 [_rowspec(tm, half), _rowspec(tm, half), _rowspec(tm, half, cb_dn_gate), _rowspec(tm, half, cb_fox_gate), _bspec(HEAD_DIM)],
                 _rowspec(tm, 2 * half), SDS((S, 2 * half), bf16))(o_dn, o_fox, proj, proj, gn)


def even_post_bwd(o_dn, o_fox, proj, gn, dom, half, cb_dn_gate, cb_fox_gate, name):
    S = proj.shape[0]
    tm = _tile(S, 512)
    nh = half // HEAD_DIM

    def body(od_ref, of_ref, gd_ref, gf_ref, gn_ref, dod_ref, dof_ref, dd_ref, df_ref, dgd_ref, dgf_ref, dgn_ref):
        @pl.when(pl.program_id(0) == 0)
        def _():
            dgn_ref[...] = jnp.zeros_like(dgn_ref)

        for h in range(nh):
            sl = slice(h * HEAD_DIM, (h + 1) * HEAD_DIM)
            _, vjp = jax.vjp(_dn_out, od_ref[:, sl], gd_ref[:, sl], gn_ref[...])
            d_o, d_gate, d_gn = vjp(dod_ref[:, sl])
            dd_ref[:, sl] = d_o
            dgd_ref[:, sl] = d_gate
            dgn_ref[...] += d_gn
        _, vjp = jax.vjp(_fox_out, of_ref[...], gf_ref[...])
        d_o, d_gate = vjp(dof_ref[...])
        df_ref[...] = d_o
        dgf_ref[...] = d_gate

    sh = SDS((S, half), f32)
    o = _rowspec(tm, half)
    return _call(body, name, (S // tm,),
                 [o, o, _rowspec(tm, half, cb_dn_gate), _rowspec(tm, half, cb_fox_gate), _bspec(HEAD_DIM),
                  _rowspec(tm, half, 0), _rowspec(tm, half, 1)],
                 (o, o, o, o, _bspec(HEAD_DIM)), (sh, sh, sh, sh, SDS((1, HEAD_DIM), f32)))(o_dn, o_fox, proj, proj, gn, dom, dom)


def _pad_row(v, lane0=0):
    return jnp.pad(v, (lane0, LANES - lane0 - v.shape[0]))[None]


def _carried(res, comm):
    return res if comm is not None else (res, [])


def even_mixer_fwd(x, gmix, w_in, w_out, conv_w, a_log, dt_bias, gn, gq, gk, f_bias, tag, comm_fox=None, comm_dn=None,
                   comm_in=None):
    S, D = x.shape
    half = D // 2
    ndn = half // HEAD_DIM
    small_cb = 4 * D // LANES
    alog_row, dt_row, fb_row = _pad_row(a_log), _pad_row(dt_bias), _pad_row(f_bias, 2 * ndn)
    h = rms_fwd(x, gmix, f"{tag}_rms")
    proj, got_in = _carried(mm(h, w_in, "nn", f32, f"{tag}_in", tm=1024, tn=1408, comm=comm_in), comm_in)
    cqkv = conv_fwd(proj, conv_w, 3 * half, f"{tag}_conv")
    (o_dn, ssave), got_dn = _carried(dn_fwd(cqkv, proj, small_cb, alog_row, dt_row, ndn, f"{tag}_dn", comm=comm_dn), comm_dn)
    c, cT = gates_fwd(proj, small_cb, fb_row, f"{tag}_gates")
    fq, fk, fv = even_pre(proj, gq, gk, half, 4, f"{tag}_pre")
    (o_fox, lse), got_fox = _carried(fox_fwd(fq, fk, fv, c, cT, ndn, 2 * ndn, f"{tag}_fox", comm=comm_fox), comm_fox)
    om = even_post(o_dn, o_fox, proj, gn, half, 3, 7, f"{tag}_post")
    xo = mm(om, w_out, "nn", f32, f"{tag}_out", tn=1024, res=x)
    return xo, (h, proj, cqkv, o_dn, ssave, c, cT, fq, fk, fv, o_fox, lse, om, alog_row, dt_row, fb_row), got_fox, got_dn, got_in


def even_mixer_bwd(x, gmix, w_in, w_out, conv_w, gn, gq, gk, saved, dy, tag, comm_fox=None, comm_dn=None, comm_bdh=None):
    S, D = x.shape
    half = D // 2
    ndn = half // HEAD_DIM
    small_cb = 4 * D // LANES
    h, proj, cqkv, o_dn, ssave, c, cT, fq, fk, fv, o_fox, lse, om, alog_row, dt_row, fb_row = saved
    dy32, dy16 = dy
    dom = mm(dy16, w_out, "nt", f32, f"{tag}_bdom", tn=1024)
    dw_out = mm(om, dy16, "tn", bf16, f"{tag}_bwout", tn=1024)
    d_odn, d_ofox, d_dgate, d_fgate, d_gn = even_post_bwd(o_dn, o_fox, proj, gn, dom, half, 3, 7, f"{tag}_bpost")
    (dfq, dfk, dfv, dcT), got_fox = _carried(
        fox_bwd(fq, fk, fv, c, cT, o_fox, lse, d_ofox, ndn, 2 * ndn, f"{tag}_bfox", comm=comm_fox), comm_fox)
    dq_pre, dk_pre, d_gq, d_gk = even_pre_bwd(proj, gq, gk, dfq, dfk, half, 4, f"{tag}_bpre")
    (dcqkv, dsm_dn, d_alog, d_dt), got_dn = _carried(
        dn_bwd(cqkv, proj, small_cb, alog_row, dt_row, ssave, d_odn, ndn, f"{tag}_bdn", comm=comm_dn), comm_dn)
    d_conv_in, d_conv_w = conv_bwd(proj, conv_w, dcqkv, f"{tag}_bconv")
    d_small, d_fb = gates_bwd(proj, small_cb, fb_row, dcT, dsm_dn, 2 * ndn, ndn, f"{tag}_bgates")
    dproj = jnp.concatenate([d_conv_in, d_dgate, dq_pre, dk_pre, dfv, d_fgate, d_small], axis=1).astype(bf16)
    dw_in = mm(h, dproj, "tn", bf16, f"{tag}_bwin", tn=1408)
    res = mm_bdh_rms(dproj, w_in, x, gmix, dy32, f"{tag}_bdh", comm=comm_bdh)
    dx, d_gmix = res[:2]
    got_bdh = res[2] if comm_bdh is not None else []
    small = dict(norm_mix=d_gmix[0], dn_conv_w=d_conv_w, dn_a_log=d_alog[0, :ndn], dn_dt_bias=d_dt[0, :ndn],
                 dn_norm_g=d_gn[0], fox_q_norm_g=d_gq[0], fox_k_norm_g=d_gk[0], fox_f_bias=d_fb[0, 2 * ndn:3 * ndn])
    return dx, dw_in, dw_out, small, got_fox, got_dn, got_bdh


def odd_mixer_fwd(x, gmix, w_in, w_out, tag, comm=None):
    S, D = x.shape
    nh = D // HEAD_DIM
    h = rms_fwd(x, gmix, f"{tag}_rms")
    qkv = mm(h, w_in, "nn", bf16, f"{tag}_in", tm=1024, tn=768)
    (o, tails), got = _carried(sb_fwd(qkv, nh, f"{tag}_sb", comm=comm), comm)
    xo = mm(o, w_out, "nn", f32, f"{tag}_out", tn=1024, res=x)
    return xo, (h, qkv, o, tails), got


def odd_mixer_bwd(x, gmix, w_in, w_out, saved, dy, tag, comm=None):
    S, D = x.shape
    nh = D // HEAD_DIM
    h, qkv, o, tails = saved
    dy32, dy16 = dy
    do = mm(dy16, w_out, "nt", f32, f"{tag}_bdo", tn=1024)
    dw_out = mm(o, dy16, "tn", bf16, f"{tag}_bwout", tn=1024)
    (dq, dk, dv), got = _carried(sb_bwd(qkv, tails, do, nh, f"{tag}_bsb", comm=comm), comm)
    dqkv = jnp.concatenate([dq, dk, dv], axis=1).astype(bf16)
    dw_in = mm(h, dqkv, "tn", bf16, f"{tag}_bwin", tn=1536)
    dx, d_gmix = mm_bdh_rms(dqkv, w_in, x, gmix, dy32, f"{tag}_bdh")
    return dx, dw_in, dw_out, dict(norm_mix=d_gmix[0]), got


def all_gather(shards, axes, name, kind="ag"):
    return _call(None, name, (), [], [], [], comm=Comm(kind, shards, axes))()[1]


def sum_parts(parts, name):
    _, R, C = parts.shape
    tm = _tile(R, 512)

    def body(p_ref, o_ref):
        acc = p_ref[0].astype(f32)
        for k in range(1, NDEV):
            acc = acc + p_ref[k].astype(f32)
        o_ref[...] = acc

    return _call(body, name, (R // tm,), [pl.BlockSpec((NDEV, tm, C), lambda i: (0, i, 0))], _rowspec(tm, C), SDS((R, C), f32))(parts)


def adamw(w, g, m, v, name):
    L, R, C = w.shape
    tm = _tile(R, max(8, min(512, (ADAMW_TILE_ELEMS // C) // 8 * 8)))
    c1 = 1.0 - ADAM_B1 ** ADAM_STEP
    c2 = 1.0 - ADAM_B2 ** ADAM_STEP

    def body(w_ref, g_ref, m_ref, v_ref, d_ref, nm_ref, nv_ref):
        g_ = g_ref[...]
        nm = ADAM_B1 * m_ref[...] + (1.0 - ADAM_B1) * g_
        nv = ADAM_B2 * v_ref[...] + (1.0 - ADAM_B2) * (g_ * g_)
        d_ref[...] = -ADAM_LR * ((nm / c1) / (jnp.sqrt(nv / c2) + ADAM_EPS) + ADAM_WD * w_ref[...])
        nm_ref[...] = nm
        nv_ref[...] = nv

    spec = pl.BlockSpec((None, tm, C), lambda l, i: (l, i, 0))
    sh = SDS((L, R, C), f32)
    return _call(body, name, (L, R // tm), [spec] * 4, (spec, spec, spec), (sh, sh, sh))(w, g, m, v)


def _pad_to(n, mult):
    return -(-n // mult) * mult


def _even_perm(D):
    half = D // 2
    ndn = half // HEAD_DIM
    o_gate = 3 * half
    o_b = o_gate + half
    o_a = o_b + ndn
    o_fq = o_a + ndn
    o_fpre = o_fq + 4 * half
    return half, ndn, o_b, o_a, o_fq, o_fpre


def _even_to_mine(w):
    D = (w.shape[-1] // 4) // LANES * LANES
    half, ndn, o_b, o_a, o_fq, o_fpre = _even_perm(D)
    small = jnp.concatenate([w[..., o_b:o_b + ndn], w[..., o_a:o_a + ndn], w[..., o_fpre:o_fpre + ndn]], axis=-1)
    small = jnp.pad(small, [(0, 0)] * (w.ndim - 1) + [(0, LANES - 3 * ndn)])
    return jnp.concatenate([w[..., :o_b], w[..., o_fq:o_fpre], small], axis=-1)


def _even_from_mine(w, D):
    half, ndn, o_b, o_a, o_fq, o_fpre = _even_perm(D)
    sm = w[..., 4 * D:]
    return jnp.concatenate([w[..., :o_b], sm[..., :ndn], sm[..., ndn:2 * ndn], w[..., o_b:4 * D], sm[..., 2 * ndn:3 * ndn]], axis=-1)


def _pack_small(parts):
    flat = jnp.concatenate([p.reshape(-1).astype(f32) for p in parts])
    n = flat.shape[0]
    return jnp.pad(flat, (0, _pad_to(n, 8 * LANES) - n)).reshape(-1, LANES)


def _unpack_small(packed, like):
    flat = packed.reshape(-1)
    out, off = [], 0
    for p in like:
        out.append(flat[off:off + p.size].reshape(p.shape))
        off += p.size
    return out


SMALL_NAMES = ("norm_ffn1", "norm_mix", "dn_a_log", "dn_dt_bias", "dn_norm_g", "fox_q_norm_g", "fox_k_norm_g", "fox_f_bias", "norm_ffn2")
BIG_NAMES = ("ffn1_w_gu", "ffn1_w_down", "w_in_even", "dn_conv_w", "w_out_even", "w_in_odd", "w_out_odd", "ffn2_w_gu", "ffn2_w_down")
WEIGHT_NAMES = ("norm_ffn1", "ffn1_w_gu", "ffn1_w_down", "norm_mix", "w_in_even", "dn_conv_w", "dn_a_log", "dn_dt_bias", "dn_norm_g",
                "fox_q_norm_g", "fox_k_norm_g", "fox_f_bias", "w_out_even", "w_in_odd", "w_out_odd", "norm_ffn2", "ffn2_w_gu", "ffn2_w_down")


def kernel(x, norm_ffn1, ffn1_w_gu, ffn1_w_down, norm_mix, w_in_even, dn_conv_w, dn_a_log, dn_dt_bias, dn_norm_g, fox_q_norm_g, fox_k_norm_g, fox_f_bias, w_out_even, w_in_odd, w_out_odd, norm_ffn2, ffn2_w_gu, ffn2_w_down, loss_target, m_norm_ffn1, m_ffn1_w_gu, m_ffn1_w_down, m_norm_mix, m_w_in_even, m_dn_conv_w, m_dn_a_log, m_dn_dt_bias, m_dn_norm_g, m_fox_q_norm_g, m_fox_k_norm_g, m_fox_f_bias, m_w_out_even, m_w_in_odd, m_w_out_odd, m_norm_ffn2, m_ffn2_w_gu, m_ffn2_w_down, v_norm_ffn1, v_ffn1_w_gu, v_ffn1_w_down, v_norm_mix, v_w_in_even, v_dn_conv_w, v_dn_a_log, v_dn_dt_bias, v_dn_norm_g, v_fox_q_norm_g, v_fox_k_norm_g, v_fox_f_bias, v_w_out_even, v_w_in_odd, v_w_out_odd, v_norm_ffn2, v_ffn2_w_gu, v_ffn2_w_down):
    args = dict(locals())
    W = {n: args[n] for n in WEIGHT_NAMES}
    M = {n: args["m_" + n] for n in WEIGHT_NAMES}
    V = {n: args["v_" + n] for n in WEIGHT_NAMES}
    xs = x[0]
    tgt = loss_target[0]
    S, D = xs.shape
    L = norm_ffn1.shape[0]
    n_even = w_in_even.shape[0]
    hs = ffn1_w_down.shape[1]
    hp = _pad_to(hs, LANES)
    me = 4 * lax.axis_index("x") + 2 * lax.axis_index("y") + lax.axis_index("c")

    def prep_gu(w):
        w = w.reshape(L, D, 2, hs)
        return jnp.pad(w, ((0, 0), (0, 0), (0, 0), (0, hp - hs))).reshape(L, D, 2 * hp).astype(bf16)

    def prep_down(w):
        return jnp.pad(w, ((0, 0), (0, hp - hs), (0, 0))).astype(bf16)

    sh_gu1, sh_d1, sh_gu2, sh_d2 = prep_gu(ffn1_w_gu), prep_down(ffn1_w_down), prep_gu(ffn2_w_gu), prep_down(ffn2_w_down)
    sh_ie, sh_oe = _even_to_mine(w_in_even).astype(bf16), w_out_even.astype(bf16)
    sh_io, sh_oo = w_in_odd.astype(bf16), w_out_odd.astype(bf16)

    def layer_shards(l):
        j = l // 2
        mix = [sh_ie[j], sh_oe[j]] if l % 2 == 0 else [sh_io[j], sh_oo[j]]
        return [sh_gu1[l], sh_d1[l], *mix, sh_gu2[l], sh_d2[l]]

    def layer_axes(l):
        return [1, 0, 0 if l % 2 == 0 else 1, 0, 1, 0]

    def layer_names(l):
        return ["ffn1_w_gu", "ffn1_w_down", *(["w_in_even", "w_out_even"] if l % 2 == 0 else ["w_in_odd", "w_out_odd"]),
                "ffn2_w_gu", "ffn2_w_down"]

    FOX_CARRIES, DN_CARRIES = (0, 4, 3), (1, 2, 5)

    def split_comm(kind, arrays, axes):
        return (Comm(kind, [arrays[i] for i in FOX_CARRIES], [axes[i] for i in FOX_CARRIES]),
                Comm(kind, [arrays[i] for i in DN_CARRIES], [axes[i] for i in DN_CARRIES]))

    def join_split(got_fox, got_dn):
        out = [None] * 6
        for i, a in zip(FOX_CARRIES, got_fox):
            out[i] = a
        for i, a in zip(DN_CARRIES, got_dn):
            out[i] = a
        return out

    sh0, ax0 = layer_shards(0), layer_axes(0)
    first = all_gather(sh0[:2] + [dn_conv_w[None]], ax0[:2] + [0], "ag_layer0", kind="ag2")
    weights = {0: first[:2] + [None] * 4}
    convw = jnp.moveaxis(first[2], 0, 2).reshape(n_even, CONV_WIDTH, -1)
    LAYER0_CARRIES = dict(f1_gu=(2,), f1_down=(3,), mx_in=(5,))

    EVEN_FWD_CARRIES = dict(f1_gu=(), mx_in=(), dn=(1, 2, 5), fox=(0, 4, 3), f2_gu=(), f2_down=())
    saved = []
    cur = xs
    for l in range(L):
        j = l // 2
        wgu1, wd1, win, wout, wgu2, wd2 = weights[l]
        nxt = l + 1 < L
        x0 = cur
        if l % 2 == 0:
            sh_n, ax_n = (layer_shards(l + 1), layer_axes(l + 1)) if nxt else (None, None)
            cm = {k: (Comm("ag2", [sh_n[i] for i in idx], [ax_n[i] for i in idx]) if nxt and idx else None)
                  for k, idx in EVEN_FWD_CARRIES.items()}
            if l == 0:
                cm.update({k: Comm("ag2", [sh0[i] for i in idx], [ax0[i] for i in idx]) for k, idx in LAYER0_CARRIES.items()})
                cm["fox"] = Comm("ag2", cm["fox"].arrays + [sh0[4]], cm["fox"].axes + [ax0[4]])
            x1, s1, got_f1gu, got_f1down = ffn_fwd(x0, norm_ffn1[l:l + 1], wgu1, wd1, f"l{l}f1", comm_gu=cm["f1_gu"],
                                                   comm_down=cm["f1_down"] if l == 0 else None)
            if l == 0:
                win, wout = got_f1gu[0], got_f1down[0]
            x2, sm, got_f, got_d, got_in = even_mixer_fwd(
                x1, norm_mix[l:l + 1], win, wout, convw[j], dn_a_log[j], dn_dt_bias[j], dn_norm_g[j:j + 1],
                fox_q_norm_g[j:j + 1], fox_k_norm_g[j:j + 1], fox_f_bias[j], f"l{l}mx", comm_fox=cm["fox"], comm_dn=cm["dn"],
                comm_in=cm["mx_in"])
            if l == 0:
                wd2, wgu2 = got_in[0], got_f[-1]
                weights[0] = [wgu1, wd1, win, wout, wgu2, wd2]
                got_f1gu, got_in, got_f = [], [], got_f[:-1]
            x3, s2, got_f2gu, got_f2down = ffn_fwd(x2, norm_ffn2[l:l + 1], wgu2, wd2, f"l{l}f2", comm_gu=cm["f2_gu"],
                                                   comm_down=cm["f2_down"])
            if nxt:
                got = dict(f1_gu=got_f1gu, mx_in=got_in, dn=got_d, fox=got_f, f2_gu=got_f2gu, f2_down=got_f2down)
                weights[l + 1] = [None] * 6
                for k, idx in EVEN_FWD_CARRIES.items():
                    for i, a in zip(idx, got[k]):
                        weights[l + 1][i] = a
        else:
            x1, s1, _, _ = ffn_fwd(x0, norm_ffn1[l:l + 1], wgu1, wd1, f"l{l}f1")
            cm = Comm("ag2", layer_shards(l + 1), layer_axes(l + 1)) if nxt else None
            x2, sm, got = odd_mixer_fwd(x1, norm_mix[l:l + 1], win, wout, f"l{l}mx", comm=cm)
            if nxt:
                weights[l + 1] = got
            x3, s2, _, _ = ffn_fwd(x2, norm_ffn2[l:l + 1], wgu2, wd2, f"l{l}f2")
        saved.append((x0, x1, x2, s1, sm, s2))
        cur = x3
    dy, loss_row = loss_head(cur, tgt, "loss")

    gsmall = {n: [None] * W[n].shape[0] for n in SMALL_NAMES}
    gconv = [None] * n_even
    parts = {n: [None] * W[n].shape[0] for n in BIG_NAMES if n != "dn_conv_w"}

    def take(l, recv):
        for nm, p in zip(layer_names(l), recv):
            idx = l if nm.startswith("ffn") else l // 2
            parts[nm][idx] = sum_parts(p.reshape(NDEV, -1, p.shape[-1]), f"sum_{nm}_{idx}").reshape(p.shape[1:])

    pending = None
    for l in reversed(range(L)):
        j = l // 2
        wgu1, wd1, win, wout, wgu2, wd2 = weights[l]
        x0, x1, x2, s1, sm, s2 = saved[l]
        dy, dg, dwgu2, dwd2 = ffn_bwd(x2, norm_ffn2[l:l + 1], wgu2, wd2, s2, dy, f"l{l}f2")
        gsmall["norm_ffn2"][l] = dg[0]
        if l % 2 == 0:
            cf, cd = split_comm("rs", pending, layer_axes(l + 1)) if pending is not None else (None, None)
            cb = None
            if l == 0:
                cd = Comm("rs", cd.arrays + [dwd2], cd.axes + [ax0[5]])
                cb = Comm("rs", [dwgu2], [ax0[4]])
            dy, dwin, dwout, sg, got_f, got_d, got_b = even_mixer_bwd(
                x1, norm_mix[l:l + 1], win, wout, convw[j], dn_norm_g[j:j + 1], fox_q_norm_g[j:j + 1],
                fox_k_norm_g[j:j + 1], sm, dy, f"l{l}mx", comm_fox=cf, comm_dn=cd, comm_bdh=cb)
            if l == 0:
                got_wd2, got_wgu2, got_d = got_d[-1], got_b[0], got_d[:-1]
            if pending is not None:
                take(l + 1, join_split(got_f, got_d))
            gconv[j] = sg.pop("dn_conv_w")
            for k_, v_ in sg.items():
                gsmall[k_][l if k_ == "norm_mix" else j] = v_
        else:
            cm = Comm("rs", pending, layer_axes(l + 1)) if pending is not None else None
            dy, dwin, dwout, sg, got = odd_mixer_bwd(x1, norm_mix[l:l + 1], win, wout, sm, dy, f"l{l}mx", comm=cm)
            if pending is not None:
                take(l + 1, got)
            gsmall["norm_mix"][l] = sg["norm_mix"]
        if l > 0:
            dy, dg, dwgu1, dwd1 = ffn_bwd(x0, norm_ffn1[l:l + 1], wgu1, wd1, s1, dy, f"l{l}f1")
        else:
            rs = lambda a, ax: Comm("rs", [a], [ax])
            dy, dg, dwgu1, dwd1, got0 = ffn_bwd(x0, norm_ffn1[l:l + 1], wgu1, wd1, s1, dy, f"l{l}f1",
                                                comms=dict(bda=rs(dwin, ax0[2]), bwd=rs(dwout, ax0[3])), own_axes=(ax0[0], ax0[1]))
        gsmall["norm_ffn1"][l] = dg[0]
        pending = [dwgu1, dwd1, dwin, dwout, dwgu2, dwd2]
    take(0, [got0["wgu"][0], got0["wd"][0], got0["bda"][0], got0["bwd"][0], got_wgu2, got_wd2])
    grad_x = dy[0][None]

    small_list = [jnp.stack(gsmall[n]) for n in SMALL_NAMES] + [jnp.stack(gconv), loss_row[0, :1]]
    packed = _pack_small(small_list)
    gathered = all_gather([packed], [0], "ag_small")[0]
    summed = sum_parts(gathered.reshape(NDEV, packed.shape[0], LANES), "sum_small")
    small_sum = _unpack_small(summed, small_list)
    G = dict(zip(SMALL_NAMES, small_sum[:len(SMALL_NAMES)]))
    conv_full = small_sum[len(SMALL_NAMES)]
    cwid = dn_conv_w.shape[2]
    G["dn_conv_w"] = lax.dynamic_slice_in_dim(conv_full, me * cwid, cwid, axis=2)
    loss = small_sum[-1][0]

    def unprep_gu(g):
        return g.reshape(L, D, 2, hp)[..., :hs].reshape(L, D, 2 * hs)

    G["ffn1_w_gu"] = unprep_gu(jnp.stack(parts["ffn1_w_gu"]))
    G["ffn2_w_gu"] = unprep_gu(jnp.stack(parts["ffn2_w_gu"]))
    G["ffn1_w_down"] = jnp.stack(parts["ffn1_w_down"])[:, :hs]
    G["ffn2_w_down"] = jnp.stack(parts["ffn2_w_down"])[:, :hs]
    G["w_in_even"] = _even_from_mine(jnp.stack(parts["w_in_even"]), D)
    G["w_out_even"] = jnp.stack(parts["w_out_even"])
    G["w_in_odd"] = jnp.stack(parts["w_in_odd"])
    G["w_out_odd"] = jnp.stack(parts["w_out_odd"])

    delta, new_m, new_v = {}, {}, {}
    for n in BIG_NAMES:
        t = (lambda a: jnp.swapaxes(a, 1, 2)) if n.endswith("w_gu") else (lambda a: a)
        delta[n], new_m[n], new_v[n] = map(t, adamw(t(W[n]), t(G[n]), t(M[n]), t(V[n]), f"adamw_{n}"))
    sw = [W[n] for n in SMALL_NAMES]
    d_, m_, v_ = adamw(_pack_small(sw)[None], _pack_small([G[n] for n in SMALL_NAMES])[None],
                       _pack_small([M[n] for n in SMALL_NAMES])[None], _pack_small([V[n] for n in SMALL_NAMES])[None], "adamw_small")
    for n, a, b, c_ in zip(SMALL_NAMES, _unpack_small(d_, sw), _unpack_small(m_, sw), _unpack_small(v_, sw)):
        delta[n], new_m[n], new_v[n] = a, b, c_

    return (loss, grad_x, *[G[n] for n in WEIGHT_NAMES], *[delta[n] for n in WEIGHT_NAMES],
            *[new_m[n] for n in WEIGHT_NAMES], *[new_v[n] for n in WEIGHT_NAMES])
```

```python
import functools
import math

import jax
import jax.numpy as jnp
from jax import lax
from jax.experimental import pallas as pl
from jax.experimental.pallas import tpu as pltpu

f32 = jnp.float32
bf16 = jnp.bfloat16
SDS = jax.ShapeDtypeStruct

HEAD_DIM = 128
LANES = 128
CONV_WIDTH = 4
DN_CHUNK = 128
EPS = 1e-6
NDEV = 8
VMEM_LIMIT_BYTES = 56 * 1024 * 1024
HI = lax.Precision.HIGHEST
ADAMW_TILE_ELEMS = 384 * 1024

ADAM_LR = 0.001
ADAM_B1 = 0.9
ADAM_B2 = 0.999
ADAM_EPS = 1e-08
ADAM_WD = 0.01
ADAM_STEP = 10

NN = (((1,), (0,)), ((), ()))
NT = (((1,), (1,)), ((), ()))
TN = (((0,), (0,)), ((), ()))


def _me_and_peers():
    x, y, c = lax.axis_index("x"), lax.axis_index("y"), lax.axis_index("c")
    me = 4 * x + 2 * y + c
    peers = []
    for r in range(1, NDEV):
        px = 1 - x if (r >> 2) & 1 else x
        py = 1 - y if (r >> 1) & 1 else y
        pc = 1 - c if r & 1 else c
        peers.append(((px, py, pc), 4 * px + 2 * py + pc))
    return me, peers


def _slab(ref, axis, width, k):
    idx = [slice(None)] * len(ref.shape)
    idx[axis] = pl.ds(k * width, width)
    return ref.at[tuple(idx)]


class Comm:
    def __init__(self, kind, arrays, axes):
        self.kind, self.arrays, self.axes, self.n = kind, list(arrays), list(axes), len(arrays)

    def out_shape(self):
        out = []
        for a, ax in zip(self.arrays, self.axes):
            shp = list(a.shape)
            if self.kind != "rs":
                shp[ax] *= NDEV
                out.append(SDS(tuple(shp), a.dtype))
            else:
                shp[ax] //= NDEV
                out.append(SDS((NDEV, *shp), a.dtype))
        return out

    def sems(self):
        return [pltpu.SemaphoreType.DMA((self.n, NDEV - 1)), pltpu.SemaphoreType.DMA((self.n, NDEV - 1)),
                pltpu.SemaphoreType.DMA((self.n,))]

    def _src(self, ins, t, to_idx):
        if self.kind == "ag":
            return ins[t]
        return _slab(ins[t], self.axes[t], ins[t].shape[self.axes[t]] // NDEV, to_idx)

    def _dst(self, ins, outs, t, from_idx):
        if self.kind != "rs":
            return _slab(outs[t], self.axes[t], ins[t].shape[self.axes[t]], from_idx)
        return outs[t].at[from_idx]

    def _copies(self, ins, outs, sems, sending):
        send_sems, recv_sems, loc_sems = sems
        me, peers = _me_and_peers()
        local, remote = [], []
        for t in range(self.n):
            local.append(pltpu.make_async_copy(self._src(ins, t, me), self._dst(ins, outs, t, me), loc_sems.at[t]))
            for r, (peer, pidx) in enumerate(peers):
                remote.append(pltpu.make_async_remote_copy(
                    src_ref=self._src(ins, t, pidx), dst_ref=self._dst(ins, outs, t, me if sending else pidx),
                    send_sem=send_sems.at[t, r], recv_sem=recv_sems.at[t, r], device_id=peer,
                    device_id_type=pl.DeviceIdType.MESH))
        return local, remote

    def _two_level(self, ins, outs, sems, own=True):
        send_sems, recv_sems, loc_sems = sems
        x, y, c = lax.axis_index("x"), lax.axis_index("y"), lax.axis_index("c")
        me, sibling = (x, y, c), (x, y, 1 - c)
        chips = [(1 - x, y), (x, 1 - y), (1 - x, 1 - y)]
        dev = lambda p: 4 * p[0] + 2 * p[1] + p[2]

        def copy(t, k, block, to, from_shard):
            dst = self._dst(ins, outs, t, dev(block))
            return pltpu.make_async_remote_copy(
                src_ref=ins[t] if from_shard else dst, dst_ref=dst, send_sem=send_sems.at[t, k], recv_sem=recv_sems.at[t, k],
                device_id=to, device_id_type=pl.DeviceIdType.MESH)

        ts = range(self.n) if own else ()
        local = [pltpu.make_async_copy(ins[t], self._dst(ins, outs, t, dev(me)), loc_sems.at[t]) for t in ts]
        first = [copy(t, 0, me, sibling, True) for t in ts]
        first += [copy(t, 1 + j, me, (*chip, c), True) for t in ts for j, chip in enumerate(chips)]
        return local, first, copy, chips, me, sibling, c

    def start(self, ins, outs, sems):
        if self.kind == "ag2":
            local, first = self._two_level(ins, outs, sems)[:2]
            for cp in local + first:
                cp.start()
            return
        local, remote = self._copies(ins, outs, sems, True)
        for cp in local + remote:
            cp.start()

    def relay(self, ins, outs, sems):
        _, _, copy, chips, me, sibling, c = self._two_level(ins, outs, sems, own=False)
        for t in range(self.n):
            for j, chip in enumerate(chips):
                copy(t, 1 + j, (*chip, c), me, False).wait_recv()
                copy(t, 4 + j, (*chip, c), sibling, False).start()

    def wait(self, ins, outs, sems, relayed=False):
        if self.kind == "ag2":
            if not relayed:
                self.relay(ins, outs, sems)
            local, first, copy, chips, me, sibling, c = self._two_level(ins, outs, sems)
            passed = [copy(t, 4 + j, (*chip, c), sibling, False) for t in range(self.n) for j, chip in enumerate(chips)]
            for t in range(self.n):
                copy(t, 0, sibling, me, False).wait_recv()
                for j, chip in enumerate(chips):
                    copy(t, 4 + j, (*chip, 1 - c), me, False).wait_recv()
            for cp in first + passed:
                cp.wait_send()
            for cp in local:
                cp.wait()
            return
        local, remote = self._copies(ins, outs, sems, False)
        for cp in remote:
            cp.wait_recv()
            cp.wait_send()
        for cp in local:
            cp.wait()


def _call(body, name, grid, in_specs, out_specs, out_shape, scratch=(), comm=None):
    params = pltpu.CompilerParams(vmem_limit_bytes=VMEM_LIMIT_BYTES)
    if comm is None:
        return pl.pallas_call(body, name=name, grid=grid, in_specs=in_specs, out_specs=out_specs, out_shape=out_shape,
                              scratch_shapes=list(scratch), compiler_params=params)
    single = not isinstance(out_shape, (tuple, list))
    c_out_specs = [out_specs] if single else list(out_specs)
    c_out_shape = [out_shape] if single else list(out_shape)
    n_in, n_out, n_scr, n = len(in_specs), len(c_out_shape), len(scratch), comm.n
    anyspec = pl.BlockSpec(memory_space=pl.ANY)

    def wrapped(*refs):
        ins, refs = refs[:n_in], refs[n_in:]
        cins, refs = refs[:n], refs[n:]
        outs, refs = refs[:n_out], refs[n_out:]
        couts, refs = refs[:n], refs[n:]
        scr, sems = refs[:n_scr], refs[n_scr:]
        first = functools.reduce(lambda a, b: a & b, [pl.program_id(d) == 0 for d in range(len(grid))], True)
        last = functools.reduce(lambda a, b: a & b, [pl.program_id(d) == grid[d] - 1 for d in range(len(grid))], True)
        if grid:
            steps = math.prod(grid)
            step = functools.reduce(lambda a, d: a * grid[d] + pl.program_id(d), range(len(grid)), 0)
            relay_early = comm.kind == "ag2" and steps >= 4
            pl.when(first)(lambda: comm.start(cins, couts, sems))
            body(*ins, *outs, *scr)
            if relay_early:
                pl.when(step == (3 * steps) // 4)(lambda: comm.relay(cins, couts, sems))
            pl.when(last)(lambda: comm.wait(cins, couts, sems, relayed=relay_early))
        else:
            comm.start(cins, couts, sems)
            if body is not None:
                body(*ins, *outs, *scr)
            comm.wait(cins, couts, sems)

    call = pl.pallas_call(
        wrapped, name=name, grid=grid, in_specs=list(in_specs) + [anyspec] * n, out_specs=c_out_specs + [anyspec] * n,
        out_shape=c_out_shape + comm.out_shape(), scratch_shapes=list(scratch) + comm.sems(), compiler_params=params)

    def run(*args):
        res = call(*args, *comm.arrays)
        mine = res[:n_out]
        return (mine[0] if single else tuple(mine)), list(res[n_out:])

    return run


def _tile(n, want, align=8):
    if n <= want:
        return n
    for t in range(want // align * align, 0, -align):
        if n % t == 0:
            return t
    return n


def _dot(a, b, dims=NN, precision=None):
    return lax.dot_general(a, b, dims, preferred_element_type=f32, precision=precision)


def _logsig(x):
    return jnp.minimum(x, 0.0) - jnp.log(1.0 + jnp.exp(-jnp.abs(x)))


def _softplus(x):
    return jnp.maximum(x, 0.0) + jnp.log(1.0 + jnp.exp(-jnp.abs(x)))


def _rms(x, g):
    return x * lax.rsqrt(jnp.mean(x * x, axis=-1, keepdims=True) + EPS) * g


def _lane_pick(blk, lane_idx):
    lane = lax.broadcasted_iota(jnp.int32, (1, LANES), 1)
    return jnp.sum(jnp.where(lane == lane_idx, blk, 0.0), axis=1, keepdims=True)


def mm(a, b, mode, out_dtype, name, tm=512, tn=512, res=None, scale=1.0, comm=None):
    if mode == "nn":
        (M, K), (_, N) = a.shape, b.shape
    elif mode == "nt":
        (M, K), (N, _) = a.shape, b.shape
    else:
        (K, M), (_, N) = a.shape, b.shape
    tm, tn = _tile(M, tm, LANES), _tile(N, tn, LANES)
    a_spec = pl.BlockSpec((K, tm), lambda j, i: (0, i)) if mode == "tn" else pl.BlockSpec((tm, K), lambda j, i: (i, 0))
    b_spec = pl.BlockSpec((tn, K), lambda j, i: (j, 0)) if mode == "nt" else pl.BlockSpec((K, tn), lambda j, i: (0, j))
    dims = {"nn": NN, "nt": NT, "tn": TN}[mode]
    o_spec = pl.BlockSpec((tm, tn), lambda j, i: (i, j))

    def body(*refs):
        acc = _dot(refs[0][...].astype(bf16), refs[1][...].astype(bf16), dims)
        if scale != 1.0:
            acc = acc * scale
        if res is not None:
            acc = acc + refs[2][...]
        refs[-1][...] = acc.astype(out_dtype)

    ins, specs = [a, b], [a_spec, b_spec]
    if res is not None:
        ins.append(res)
        specs.append(o_spec)
    return _call(body, name, (N // tn, M // tm), specs, o_spec, SDS((M, N), out_dtype), comm=comm)(*ins)


def _rowspec(tm, w, cb=0):
    return pl.BlockSpec((tm, w), lambda i: (i, cb))


def _bspec(w):
    return pl.BlockSpec((1, w), lambda i: (0, 0))


def rms_fwd(x, g, name):
    S, D = x.shape
    tm = _tile(S, 512)

    def body(x_ref, g_ref, h_ref):
        h_ref[...] = _rms(x_ref[...], g_ref[...]).astype(bf16)

    return _call(body, name, (S // tm,), [_rowspec(tm, D), _bspec(D)], _rowspec(tm, D), SDS((S, D), bf16))(x, g)


def _swiglu(g, u):
    return g * jax.nn.sigmoid(g) * u


def ffn_gu_act(h, wgu, name, tm=1024, tn=768, comm=None):
    S, D = h.shape
    W = wgu.shape[1] // 2
    tm, tn = _tile(S, tm, LANES), _tile(W, tn, LANES)
    nb = W // tn

    def body(h_ref, wg_ref, wu_ref, gu_ref, a_ref):
        hb = h_ref[...]
        g = _dot(hb, wg_ref[...])
        u = _dot(hb, wu_ref[...])
        gu_ref[0] = g.astype(bf16)
        gu_ref[1] = u.astype(bf16)
        a_ref[...] = _swiglu(g, u).astype(bf16)

    return _call(body, name, (nb, S // tm),
                 [pl.BlockSpec((tm, D), lambda j, i: (i, 0)), pl.BlockSpec((D, tn), lambda j, i: (0, j)),
                  pl.BlockSpec((D, tn), lambda j, i: (0, nb + j))],
                 (pl.BlockSpec((2, tm, tn), lambda j, i: (0, i, j)), pl.BlockSpec((tm, tn), lambda j, i: (i, j))),
                 (SDS((2, S, W), bf16), SDS((S, W), bf16)), comm=comm)(h, wgu, wgu)


def ffn_bda_act(dy, wd, gu, scale, name, tm=1024, tn=768, comm=None):
    S, D = dy.shape
    W = wd.shape[0]
    tm, tn = _tile(S, tm, LANES), _tile(W, tn, LANES)

    def body(dy_ref, wd_ref, gu_ref, dgu_ref):
        da = _dot(dy_ref[...].astype(bf16), wd_ref[...], NT) * scale
        _, vjp = jax.vjp(_swiglu, gu_ref[0].astype(f32), gu_ref[1].astype(f32))
        dg, du = vjp(da)
        dgu_ref[0] = dg.astype(bf16)
        dgu_ref[1] = du.astype(bf16)

    planes = pl.BlockSpec((2, tm, tn), lambda j, i: (0, i, j))
    return _call(body, name, (W // tn, S // tm),
                 [pl.BlockSpec((tm, D), lambda j, i: (i, 0)), pl.BlockSpec((tn, D), lambda j, i: (j, 0)), planes],
                 planes, SDS((2, S, W), bf16), comm=comm)(dy, wd, gu)


def ffn_bwgu(h, dgu, name, tm=1024, tn=768, comm=None):
    S, D = h.shape
    W = dgu.shape[2]
    tm, tn = _tile(D, tm, LANES), _tile(W, tn, LANES)
    nb = W // tn

    def body(h_ref, d_ref, o_ref):
        o_ref[...] = _dot(h_ref[...], d_ref[...], TN).astype(bf16)

    return _call(body, name, (2 * nb, D // tm),
                 [pl.BlockSpec((S, tm), lambda j, i: (0, i)), pl.BlockSpec((None, S, tn), lambda j, i: (j // nb, 0, j % nb))],
                 pl.BlockSpec((tm, tn), lambda j, i: (i, j)), SDS((D, 2 * W), bf16), comm=comm)(h, dgu)


def nt_rms_bwd(pairs, x, g, dres, name, tm=512, comm=None):
    S, D = x.shape
    tm = _tile(S, tm)
    n = len(pairs)

    def body(*refs):
        x_ref, g_ref, dres_ref = refs[2 * n:2 * n + 3]
        dx_ref, dxb_ref, dg_ref = refs[2 * n + 3:]
        dh = sum(_dot(refs[2 * k][...].astype(bf16), refs[2 * k + 1][...], NT) for k in range(n))
        _, vjp = jax.vjp(_rms, x_ref[...], g_ref[...])
        dx, dg = vjp(dh)
        dx = dres_ref[...] + dx
        dx_ref[...] = dx
        dxb_ref[...] = dx.astype(bf16)

        @pl.when(pl.program_id(0) == 0)
        def _():
            dg_ref[...] = jnp.zeros_like(dg_ref)

        dg_ref[...] += dg

    arrays, specs = [], []
    for a, a_spec, b, b_spec in pairs:
        arrays += [a, b]
        specs += [a_spec, b_spec]
    (dx, dxb, dg), got = _carried(_call(body, name, (S // tm,), specs + [_rowspec(tm, D), _bspec(D), _rowspec(tm, D)],
                                        (_rowspec(tm, D), _rowspec(tm, D), _bspec(D)),
                                        (SDS((S, D), f32), SDS((S, D), bf16), SDS((1, D), f32)), comm=comm)(*arrays, x, g, dres),
                                  comm)
    return ((dx, dxb), dg) if comm is None else ((dx, dxb), dg, got)


def ffn_bdh_rms(dgu, wgu, x, g, dres, name, tm=512, comm=None):
    _, S, W = dgu.shape
    D = wgu.shape[0]
    tm = _tile(S, tm)
    pairs = [(dgu, pl.BlockSpec((None, tm, W), functools.partial(lambda p, i: (p, i, 0), p)),
              wgu, pl.BlockSpec((D, W), functools.partial(lambda p, i: (0, p), p))) for p in range(2)]
    return nt_rms_bwd(pairs, x, g, dres, name, tm, comm=comm)


def mm_bdh_rms(d, w, x, g, dres, name, tm=512, comm=None):
    S, K = d.shape
    tm = _tile(S, tm)
    return nt_rms_bwd([(d, pl.BlockSpec((tm, K), lambda i: (i, 0)), w, pl.BlockSpec(w.shape, lambda i: (0, 0)))], x, g, dres, name, tm,
                      comm=comm)


def loss_head(y, t, name):
    S, D = y.shape
    tm = _tile(S, 512)

    def body(y_ref, t_ref, dy_ref, dyb_ref, l_ref):
        e = y_ref[...] - t_ref[...]
        dy_ref[...] = e * (1.0 / D)
        dyb_ref[...] = (e * (1.0 / D)).astype(bf16)

        @pl.when(pl.program_id(0) == 0)
        def _():
            l_ref[...] = jnp.zeros_like(l_ref)

        l_ref[...] += jnp.sum(jnp.sum(e * e, axis=1, keepdims=True), axis=0, keepdims=True) * (0.5 / D)

    dy, dyb, loss = _call(body, name, (S // tm,), [_rowspec(tm, D), _rowspec(tm, D)],
                          (_rowspec(tm, D), _rowspec(tm, D), _bspec(LANES)),
                          (SDS((S, D), f32), SDS((S, D), bf16), SDS((1, LANES), f32)))(y, t)
    return (dy, dyb), loss


def ffn_fwd(x, g, wgu, wd, tag, comm_gu=None, comm_down=None):
    h = rms_fwd(x, g, f"{tag}_rms")
    (gu, a), got_gu = _carried(ffn_gu_act(h, wgu, f"{tag}_gu", comm=comm_gu), comm_gu)
    xo, got_down = _carried(mm(a, wd, "nn", f32, f"{tag}_down", tm=512, tn=1024, res=x, scale=0.5, comm=comm_down), comm_down)
    return xo, (h, gu, a), got_gu, got_down


def ffn_bwd(x, g, wgu, wd, saved, dy, tag, comms=None, own_axes=None):
    h, gu, a = saved
    dy32, dy16 = dy
    cm = comms or {}
    dgu, got_bda = _carried(ffn_bda_act(dy16, wd, gu, 0.5, f"{tag}_bda", comm=cm.get("bda")), cm.get("bda"))
    dwd, got_bwd = _carried(mm(a, dy16, "tn", bf16, f"{tag}_bwd", tm=512, tn=1024, scale=0.5, comm=cm.get("bwd")), cm.get("bwd"))
    c_wd = Comm("rs", [dwd], [own_axes[1]]) if own_axes else None
    dwgu, got_wd = _carried(ffn_bwgu(h, dgu, f"{tag}_bwgu", comm=c_wd), c_wd)
    c_wgu = Comm("rs", [dwgu], [own_axes[0]]) if own_axes else None
    res = ffn_bdh_rms(dgu, wgu, x, g, dy32, f"{tag}_bdh", comm=c_wgu)
    dx, dg = res[:2]
    if comms is None and own_axes is None:
        return dx, dg, dwgu, dwd
    return dx, dg, dwgu, dwd, dict(bda=got_bda, bwd=got_bwd, wd=got_wd, wgu=res[2] if c_wgu is not None else [])


def _split_bf16(x):
    hi = x.astype(bf16)
    lo = (x - hi.astype(f32)).astype(bf16)
    return hi, lo


SB_TQ, SB_TK, SB_NSUB = 512, 256, 4
FOX_TK = 256


def _sb_geometry(S, tk=SB_TK):
    tq = min(SB_TQ, S)
    tk = min(tk, tq)
    return tq, tk, tq // SB_NSUB, tq // tk


def _sb_consts(tk):
    r = lax.broadcasted_iota(jnp.int32, (tk, tk), 0)
    c = lax.broadcasted_iota(jnp.int32, (tk, tk), 1)
    return (r > c).astype(bf16), (r < c).astype(bf16)


def _active(d, nsub, rs, tk):
    return [s for s in range(nsub) if d is None or (s + 1) * rs > d * tk]


def _put(full, act, part):
    out = list(full)
    for s, x in zip(act, part):
        out[s] = x
    return out


def _sb_scores(qs, k, kb, tk, rows, masked):
    scale = HEAD_DIM ** -0.5
    z = [_dot(q, k, NT) * scale for q in qs]
    ls = [_logsig(z_) for z_ in z]
    lm = [l - z_ for l, z_ in zip(ls, z)]
    ok = None
    if masked:
        col = kb * tk + lax.broadcasted_iota(jnp.int32, z[0].shape, 1)
        ok = [col < row for row in rows]
        lm = [jnp.where(m, x, 0.0) for m, x in zip(ok, lm)]
    return ls, lm, ok


def _sb_weights(ls, lm, ok, tail, after):
    suf = [_dot(x.astype(bf16), after) for x in lm]
    a = [jnp.exp(l + s_ + t_) for l, s_, t_ in zip(ls, suf, tail)]
    if ok is not None:
        a = [jnp.where(m, x, 0.0) for m, x in zip(ok, a)]
    return a


def sb_fwd(qkv, nh, name, comm=None):
    S = qkv.shape[0]
    tq, tk, rs, nd = _sb_geometry(S)
    nsub = tq // rs

    def body(q_ref, k_ref, v_ref, o_ref, tails_ref):
        i = pl.program_id(1)
        after, _ = _sb_consts(tk)
        qs = [q_ref[pl.ds(s * rs, rs), :] for s in range(nsub)]
        rows = [i * tq + s * rs + lax.broadcasted_iota(jnp.int32, (rs, tk), 0) for s in range(nsub)]

        def tile(kb, carry, d):
            tail, acc = carry
            act = _active(d, nsub, rs, tk)
            pick = lambda lst: [lst[s] for s in act]
            off = pl.multiple_of(kb * tk, tk)
            k = k_ref[pl.ds(off, tk), :]
            v = v_ref[pl.ds(off, tk), :]
            ls, lm, ok = _sb_scores(pick(qs), k, kb, tk, pick(rows), d is not None)
            a = _sb_weights(ls, lm, ok, pick(tail), after)
            tails_ref[pl.ds(kb, 1), :] += jnp.concatenate([t_.T for t_ in tail], axis=1)
            acc = _put(acc, act, [ac + _dot(a_.astype(bf16), v) for ac, a_ in zip(pick(acc), a)])
            tail = _put(tail, act, [t_ + jnp.sum(x, axis=1, keepdims=True) for t_, x in zip(pick(tail), lm)])
            return tail, acc

        tails_ref[...] = jnp.zeros_like(tails_ref)
        carry = ([jnp.zeros((rs, 1), f32)] * nsub, [jnp.zeros((rs, HEAD_DIM), f32)] * nsub)
        for d in reversed(range(nd)):
            carry = tile(i * nd + d, carry, d)
        carry = lax.fori_loop(0, i * nd, lambda t, c: tile(i * nd - 1 - t, c, None), carry)
        for s in range(nsub):
            o_ref[pl.ds(s * rs, rs), :] = carry[1][s].astype(o_ref.dtype)

    blk = lambda cb0: pl.BlockSpec((tq, HEAD_DIM), lambda h, i: (i, cb0 + h))
    full = lambda cb0: pl.BlockSpec((S, HEAD_DIM), lambda h, i: (0, cb0 + h))
    tails = pl.BlockSpec((S // tk, tq), lambda h, i: (h, i))
    return _call(body, name, (nh, S // tq), [blk(0), full(nh), full(2 * nh)], (blk(0), tails),
                 (SDS((S, nh * HEAD_DIM), bf16), SDS((nh * (S // tk), S), f32)), comm=comm)(qkv, qkv, qkv)


def sb_bwd(qkv, tails, do, nh, name, comm=None):
    S = qkv.shape[0]
    tq, tk, rs, nd = _sb_geometry(S)
    nsub = tq // rs
    scale = HEAD_DIM ** -0.5

    def body(q_ref, k_ref, v_ref, tails_ref, do_ref, dq_ref, dk_ref, dv_ref):
        i = pl.program_id(1)

        @pl.when(i == 0)
        def _():
            dk_ref[...] = jnp.zeros_like(dk_ref)
            dv_ref[...] = jnp.zeros_like(dv_ref)

        after, before = _sb_consts(tk)
        qs = [q_ref[pl.ds(s * rs, rs), :] for s in range(nsub)]
        dos = [do_ref[pl.ds(s * rs, rs), :].astype(bf16) for s in range(nsub)]
        rows = [i * tq + s * rs + lax.broadcasted_iota(jnp.int32, (rs, tk), 0) for s in range(nsub)]

        def sweep2(kb, carry, d):
            pe, dq = carry
            act = _active(d, nsub, rs, tk)
            pick = lambda lst: [lst[s] for s in act]
            qa, da = pick(qs), pick(dos)
            off = pl.multiple_of(kb * tk, tk)
            k = k_ref[pl.ds(off, tk), :]
            v = v_ref[pl.ds(off, tk), :]
            ls, lm, ok = _sb_scores(qa, k, kb, tk, pick(rows), d is not None)
            tail_row = tails_ref[pl.ds(kb, 1), :]
            tail = [tail_row[:, s * rs:(s + 1) * rs].T for s in act]
            a = _sb_weights(ls, lm, ok, tail, after)
            e = [_dot(d_, v, NT) * a_ for d_, a_ in zip(da, a)]
            cum_e = [p_ + _dot(e_.astype(bf16), before) for p_, e_ in zip(pick(pe), e)]
            sig = [jnp.exp(l) for l in ls]
            dz = [e_ * (1.0 - sg) - c_ * sg for e_, sg, c_ in zip(e, sig, cum_e)]
            if ok is not None:
                dz = [jnp.where(m, x, 0.0) for m, x in zip(ok, dz)]
            dzb = [(x * scale).astype(bf16) for x in dz]
            dk_ref[pl.ds(off, tk), :] += sum(_dot(x, q, TN) for x, q in zip(dzb, qa))
            dv_ref[pl.ds(off, tk), :] += sum(_dot(a_.astype(bf16), d_, TN) for a_, d_ in zip(a, da))
            pe = _put(pe, act, [p_ + jnp.sum(e_, axis=1, keepdims=True) for p_, e_ in zip(pick(pe), e)])
            dq = _put(dq, act, [g + _dot(x, k) for g, x in zip(pick(dq), dzb)])
            return pe, dq

        carry = ([jnp.zeros((rs, 1), f32)] * nsub, [jnp.zeros((rs, HEAD_DIM), f32)] * nsub)
        carry = lax.fori_loop(0, i * nd, lambda kb, c: sweep2(kb, c, None), carry)
        for d in range(nd):
            carry = sweep2(i * nd + d, carry, d)
        for s in range(nsub):
            dq_ref[pl.ds(s * rs, rs), :] = carry[1][s]

    blk = lambda cb0: pl.BlockSpec((tq, HEAD_DIM), lambda h, i: (i, cb0 + h))
    full = lambda cb0: pl.BlockSpec((S, HEAD_DIM), lambda h, i: (0, cb0 + h))
    sh = SDS((S, nh * HEAD_DIM), f32)
    tails_spec = pl.BlockSpec((S // tk, tq), lambda h, i: (h, i))
    return _call(body, name, (nh, S // tq), [blk(0), full(nh), full(2 * nh), tails_spec, blk(0)], (blk(0), full(0), full(0)),
                 (sh, sh, sh), comm=comm)(qkv, qkv, qkv, tails, do)


def fox_fwd(fq, fk, fv, c, cT, nh, lane0, name, comm=None):
    S = fq.shape[0]
    tq, tk, rs, nd = _sb_geometry(S, FOX_TK)
    nsub = tq // rs
    scale = HEAD_DIM ** -0.5

    def body(q_ref, k_ref, v_ref, c_ref, ct_ref, o_ref, lse_ref):
        h = pl.program_id(0)
        i = pl.program_id(1)
        subs = range(nsub)
        qs = [q_ref[pl.ds(s * rs, rs), :] for s in subs]
        ci = [_lane_pick(c_ref[pl.ds(s * rs, rs), :], lane0 + h) for s in subs]
        rows = [i * tq + s * rs + lax.broadcasted_iota(jnp.int32, (rs, tk), 0) for s in subs]

        def tile(kb, carry, d):
            m, l, acc = carry
            act = _active(d, nsub, rs, tk)
            pick = lambda lst: [lst[s] for s in act]
            off = pl.multiple_of(kb * tk, tk)
            k = k_ref[pl.ds(off, tk), :]
            v = v_ref[pl.ds(off, tk), :]
            cj = ct_ref[pl.ds(lane0 % 8 + h, 1), pl.ds(off, tk)]
            sc = [_dot(q, k, NT) * scale + (c_ - cj) for q, c_ in zip(pick(qs), pick(ci))]
            if d is not None:
                col = kb * tk + lax.broadcasted_iota(jnp.int32, (rs, tk), 1)
                sc = [jnp.where(col <= row, x, -jnp.inf) for x, row in zip(sc, pick(rows))]
            m2 = [jnp.maximum(m_, jnp.max(x, axis=1, keepdims=True)) for m_, x in zip(pick(m), sc)]
            p = [jnp.exp(x - m_) for x, m_ in zip(sc, m2)]
            al = [jnp.exp(a - b) for a, b in zip(pick(m), m2)]
            l = _put(l, act, [a * l_ + jnp.sum(p_, axis=1, keepdims=True) for a, l_, p_ in zip(al, pick(l), p)])
            acc = _put(acc, act, [a * ac + _dot(p_.astype(bf16), v) for a, ac, p_ in zip(al, pick(acc), p)])
            return _put(m, act, m2), l, acc

        carry = ([jnp.full((rs, 1), -jnp.inf, f32)] * nsub, [jnp.zeros((rs, 1), f32)] * nsub,
                 [jnp.zeros((rs, HEAD_DIM), f32)] * nsub)
        for d in range(nd):
            carry = tile(i * nd + d, carry, d)
        m, l, acc = lax.fori_loop(0, i * nd, lambda kb, cr: tile(kb, cr, None), carry)
        for s in subs:
            o_ref[pl.ds(s * rs, rs), :] = acc[s] / l[s]
            lse_ref[pl.ds(s * rs, rs), :] = jnp.broadcast_to(m[s] + jnp.log(l[s]), (rs, HEAD_DIM))

    blk = pl.BlockSpec((tq, HEAD_DIM), lambda h, i: (i, h))
    full = pl.BlockSpec((S, HEAD_DIM), lambda h, i: (0, h))
    cspec = pl.BlockSpec((tq, LANES), lambda h, i: (i, 0))
    ctspec = pl.BlockSpec((8, S), lambda h, i: (lane0 // 8, 0))
    sh = SDS((S, nh * HEAD_DIM), f32)
    return _call(body, name, (nh, S // tq), [blk, full, full, cspec, ctspec], (blk, blk), (sh, sh), comm=comm)(fq, fk, fv, c, cT)


def fox_bwd(fq, fk, fv, c, cT, o, lse, do, nh, lane0, name, comm=None):
    S = fq.shape[0]
    tq, tk, rs, nd = _sb_geometry(S, FOX_TK)
    nsub = tq // rs
    scale = HEAD_DIM ** -0.5

    def body(q_ref, k_ref, v_ref, c_ref, ct_ref, o_ref, lse_ref, do_ref, dq_ref, dk_ref, dv_ref, dct_ref):
        h = pl.program_id(0)
        i = pl.program_id(1)

        @pl.when(i == 0)
        def _():
            dk_ref[...] = jnp.zeros_like(dk_ref)
            dv_ref[...] = jnp.zeros_like(dv_ref)

        @pl.when((i == 0) & (h == 0))
        def _():
            dct_ref[...] = jnp.zeros_like(dct_ref)

        subs = range(nsub)
        qs = [q_ref[pl.ds(s * rs, rs), :] for s in subs]
        dof = [do_ref[pl.ds(s * rs, rs), :] for s in subs]
        dos = [x.astype(bf16) for x in dof]
        ci = [_lane_pick(c_ref[pl.ds(s * rs, rs), :], lane0 + h) for s in subs]
        lses = [lse_ref[pl.ds(s * rs, rs), :][:, :1] for s in subs]
        dl = [jnp.sum(x * o_ref[pl.ds(s * rs, rs), :], axis=1, keepdims=True) for s, x in zip(subs, dof)]
        rows = [i * tq + s * rs + lax.broadcasted_iota(jnp.int32, (rs, tk), 0) for s in subs]

        def tile(kb, carry, d):
            dq, rsum = carry
            act = _active(d, nsub, rs, tk)
            pick = lambda lst: [lst[s] for s in act]
            qa, da = pick(qs), pick(dos)
            off = pl.multiple_of(kb * tk, tk)
            k = k_ref[pl.ds(off, tk), :]
            v = v_ref[pl.ds(off, tk), :]
            cj = ct_ref[pl.ds(lane0 % 8 + h, 1), pl.ds(off, tk)]
            p = [jnp.exp(_dot(q, k, NT) * scale + (c_ - cj) - ls) for q, c_, ls in zip(qa, pick(ci), pick(lses))]
            if d is not None:
                col = kb * tk + lax.broadcasted_iota(jnp.int32, (rs, tk), 1)
                p = [jnp.where(col <= row, x, 0.0) for x, row in zip(p, pick(rows))]
            ds = [p_ * (_dot(d_, v, NT) - dl_) for p_, d_, dl_ in zip(p, da, pick(dl))]
            dsb = [(x * scale).astype(bf16) for x in ds]
            dk_ref[pl.ds(off, tk), :] += sum(_dot(x, q, TN) for x, q in zip(dsb, qa))
            dv_ref[pl.ds(off, tk), :] += sum(_dot(p_.astype(bf16), d_, TN) for p_, d_ in zip(p, da))
            dct_ref[pl.ds(lane0 + h, 1), pl.ds(off, tk)] -= sum(jnp.sum(x, axis=0, keepdims=True) for x in ds)
            return (_put(dq, act, [g + _dot(x, k) for g, x in zip(pick(dq), dsb)]),
                    _put(rsum, act, [r_ + jnp.sum(x, axis=1, keepdims=True) for r_, x in zip(pick(rsum), ds)]))

        carry = ([jnp.zeros((rs, HEAD_DIM), f32)] * nsub, [jnp.zeros((rs, 1), f32)] * nsub)
        carry = lax.fori_loop(0, i * nd, lambda kb, cr: tile(kb, cr, None), carry)
        for d in range(nd):
            carry = tile(i * nd + d, carry, d)
        dq, rsum = carry
        for s in subs:
            dq_ref[pl.ds(s * rs, rs), :] = dq[s]
        dct_ref[pl.ds(lane0 + h, 1), pl.ds(pl.multiple_of(i * tq, tq), tq)] += jnp.concatenate([r_.T for r_ in rsum], axis=1)

    blk = pl.BlockSpec((tq, HEAD_DIM), lambda h, i: (i, h))
    full = pl.BlockSpec((S, HEAD_DIM), lambda h, i: (0, h))
    cspec = pl.BlockSpec((tq, LANES), lambda h, i: (i, 0))
    ctspec = pl.BlockSpec((8, S), lambda h, i: (lane0 // 8, 0))
    dctspec = pl.BlockSpec((LANES, S), lambda h, i: (0, 0))
    sh = SDS((S, nh * HEAD_DIM), f32)
    return _call(body, name, (nh, S // tq), [blk, full, full, cspec, ctspec, blk, blk, blk], (blk, full, full, dctspec),
                 (sh, sh, sh, SDS((LANES, S), f32)), comm=comm)(fq, fk, fv, c, cT, o, lse, do)


def gates_fwd(proj, small_cb, fbias_row, name, tm=256):
    S = proj.shape[0]
    tm = _tile(S, tm)

    def body(sm_ref, fb_ref, c_ref, ct_ref, carry_ref):
        @pl.when(pl.program_id(0) == 0)
        def _():
            carry_ref[...] = jnp.zeros_like(carry_ref)

        lf = _logsig(sm_ref[...] + fb_ref[...])
        r = lax.broadcasted_iota(jnp.int32, (tm, tm), 0)
        cc = lax.broadcasted_iota(jnp.int32, (tm, tm), 1)
        c = _dot((r >= cc).astype(f32), lf, NN, HI) + carry_ref[...]
        carry_ref[...] += jnp.sum(lf, axis=0, keepdims=True)
        c_ref[...] = c
        ct_ref[...] = c.T

    return _call(body, name, (S // tm,), [_rowspec(tm, LANES, small_cb), _bspec(LANES)],
                 (_rowspec(tm, LANES), pl.BlockSpec((LANES, tm), lambda i: (0, i))),
                 (SDS((S, LANES), f32), SDS((LANES, S), f32)), scratch=[pltpu.VMEM((1, LANES), f32)])(proj, fbias_row)


def gates_bwd(proj, small_cb, fbias_row, dcT, dsm_dn, lane0, nh, name, tm=256):
    S = proj.shape[0]
    tm = _tile(S, tm)
    nt = S // tm

    def body(sm_ref, fb_ref, dct_ref, dsm_ref, o_ref, dfb_ref, carry_ref):
        @pl.when(pl.program_id(0) == 0)
        def _():
            carry_ref[...] = jnp.zeros_like(carry_ref)
            dfb_ref[...] = jnp.zeros_like(dfb_ref)

        dc = dct_ref[...].T
        r = lax.broadcasted_iota(jnp.int32, (tm, tm), 0)
        cc = lax.broadcasted_iota(jnp.int32, (tm, tm), 1)
        dlf = _dot((r <= cc).astype(f32), dc, NN, HI) + carry_ref[...]
        carry_ref[...] += jnp.sum(dc, axis=0, keepdims=True)
        lane = lax.broadcasted_iota(jnp.int32, (1, LANES), 1)
        dpre = jnp.where((lane >= lane0) & (lane < lane0 + nh), dlf * jax.nn.sigmoid(-(sm_ref[...] + fb_ref[...])), 0.0)
        o_ref[...] = dsm_ref[...] + dpre
        dfb_ref[...] += jnp.sum(dpre, axis=0, keepdims=True)

    rev = lambda i: nt - 1 - i
    return _call(body, name, (nt,),
                 [pl.BlockSpec((tm, LANES), lambda i: (rev(i), small_cb)), _bspec(LANES),
                  pl.BlockSpec((LANES, tm), lambda i: (0, rev(i))), pl.BlockSpec((tm, LANES), lambda i: (rev(i), 0))],
                 (pl.BlockSpec((tm, LANES), lambda i: (rev(i), 0)), _bspec(LANES)),
                 (SDS((S, LANES), f32), SDS((1, LANES), f32)), scratch=[pltpu.VMEM((1, LANES), f32)])(proj, fbias_row, dcT, dsm_dn)


PAD_ROWS = 8


def conv_fwd(proj, w, width, name):
    S = proj.shape[0]
    cw = 256 if width % 256 == 0 else 128

    def body(x_ref, w_ref, y_ref, pad_ref):
        pad_ref[pl.ds(0, PAD_ROWS), :] = jnp.zeros((PAD_ROWS, cw), f32)
        pad_ref[pl.ds(PAD_ROWS, S), :] = x_ref[...]
        y = jnp.zeros((S, cw), f32)
        for i in range(CONV_WIDTH):
            y = y + pad_ref[pl.ds(PAD_ROWS - (CONV_WIDTH - 1) + i, S), :] * w_ref[pl.ds(i, 1), :]
        y_ref[...] = y * jax.nn.sigmoid(y)

    spec = pl.BlockSpec((S, cw), lambda j: (0, j))
    return _call(body, name, (width // cw,), [spec, pl.BlockSpec((CONV_WIDTH, cw), lambda j: (0, j))], spec,
                 SDS((S, width), f32), scratch=[pltpu.VMEM((S + PAD_ROWS, cw), f32)])(proj, w)


def conv_bwd(proj, w, dy, name):
    S, width = dy.shape
    cw = 256 if width % 256 == 0 else 128

    def body(x_ref, w_ref, dy_ref, dx_ref, dw_ref, xpad_ref, dpad_ref):
        xpad_ref[pl.ds(0, PAD_ROWS), :] = jnp.zeros((PAD_ROWS, cw), f32)
        xpad_ref[pl.ds(PAD_ROWS, S), :] = x_ref[...]
        y = jnp.zeros((S, cw), f32)
        for i in range(CONV_WIDTH):
            y = y + xpad_ref[pl.ds(PAD_ROWS - (CONV_WIDTH - 1) + i, S), :] * w_ref[pl.ds(i, 1), :]
        sg = jax.nn.sigmoid(y)
        dpre = dy_ref[...] * (sg * (1.0 + y * (1.0 - sg)))
        dpad_ref[pl.ds(S, PAD_ROWS), :] = jnp.zeros((PAD_ROWS, cw), f32)
        dpad_ref[pl.ds(0, S), :] = dpre
        dx = jnp.zeros((S, cw), f32)
        for i in range(CONV_WIDTH):
            dx = dx + dpad_ref[pl.ds(CONV_WIDTH - 1 - i, S), :] * w_ref[pl.ds(i, 1), :]
            dw_ref[pl.ds(i, 1), :] = jnp.sum(dpre * xpad_ref[pl.ds(PAD_ROWS - (CONV_WIDTH - 1) + i, S), :], axis=0, keepdims=True)
        dx_ref[...] = dx

    spec = pl.BlockSpec((S, cw), lambda j: (0, j))
    wspec = pl.BlockSpec((CONV_WIDTH, cw), lambda j: (0, j))
    return _call(body, name, (width // cw,), [spec, wspec, spec], (spec, wspec),
                 (SDS((S, width), f32), SDS((CONV_WIDTH, width), f32)),
                 scratch=[pltpu.VMEM((S + PAD_ROWS, cw), f32), pltpu.VMEM((S + PAD_ROWS, cw), f32)])(proj, w, dy)


def _pdot_raw(a, b, dims, passes):
    if passes == 1:
        return _dot(a.astype(bf16), b.astype(bf16), dims)
    ah, al = _split_bf16(a)
    bh, bl = _split_bf16(b)
    return _dot(ah, bh, dims) + (_dot(ah, bl, dims) + _dot(al, bh, dims))


def _make_pdot(passes):
    @functools.partial(jax.custom_vjp, nondiff_argnums=(2,))
    def pdot(a, b, mode):
        return _pdot_raw(a, b, {"nn": NN, "nt": NT, "tn": TN}[mode], passes)

    def fwd(a, b, mode):
        return pdot(a, b, mode), (a, b)

    def bwd(mode, res, g):
        a, b = res
        if mode == "nn":
            return _pdot_raw(g, b, NT, passes), _pdot_raw(a, g, TN, passes)
        if mode == "nt":
            return _pdot_raw(g, b, NN, passes), _pdot_raw(g, a, TN, passes)
        return _pdot_raw(b, g, NT, passes), _pdot_raw(a, g, NN, passes)

    pdot.defvjp(fwd, bwd)
    return pdot


_dot1 = _make_pdot(1)
_dot3 = _make_pdot(3)


def _each(f, *lists):
    return [f(*xs) for xs in zip(*lists)]


def _dn_chunk(ndn, cq, ck, cv, small, alog, dtb, s0):
    C = cq[0].shape[0]
    hs = list(range(ndn))
    b = [_lane_pick(small, h) for h in hs]
    d = [_lane_pick(small, ndn + h) for h in hs]
    al = [_lane_pick(alog, h) for h in hs]
    dt = [_lane_pick(dtb, h) for h in hs]
    q = _each(lambda x: x * lax.rsqrt(jnp.sum(x * x, axis=1, keepdims=True) + EPS) * (HEAD_DIM ** -0.5), cq)
    k = _each(lambda x: x * lax.rsqrt(jnp.sum(x * x, axis=1, keepdims=True) + EPS), ck)
    beta = _each(jax.nn.sigmoid, b)
    g = _each(lambda al_, d_, dt_: -jnp.exp(al_) * _softplus(d_ + dt_), al, d, dt)
    r = lax.broadcasted_iota(jnp.int32, (C, C), 0)
    c = lax.broadcasted_iota(jnp.int32, (C, C), 1)
    incl, strict = r >= c, r > c
    inclf = incl.astype(f32)
    gc = _each(lambda g_: _dot3(inclf, g_, "nn"), g)
    decay = _each(lambda gc_: jnp.where(incl, jnp.exp(jnp.where(incl, gc_ - gc_.T, 0.0)), 0.0), gc)
    kb = _each(lambda k_, b_: k_ * b_, k, beta)
    a = _each(lambda kb_, k_, dec: jnp.where(strict, _dot3(kb_, k_, "nt") * dec, 0.0), kb, k, decay)
    eye = (r == c).astype(f32)
    t = _each(lambda a_: eye - a_, a)
    p = a
    for _ in range(int(math.log2(C)) - 1):
        p = _each(lambda p_: _dot3(p_, p_, "nn"), p)
        t = _each(lambda t_, p_: t_ + _dot3(t_, p_, "nn"), t, p)
    eg = _each(jnp.exp, gc)
    u = _each(lambda t_, v_, b_: _dot3(t_, v_ * b_, "nn"), t, cv, beta)
    w = _each(lambda t_, kb_, eg_: _dot3(t_, kb_ * eg_, "nn"), t, kb, eg)
    attn = _each(lambda q_, k_, dec: _dot1(q_, k_, "nt") * dec, q, k, decay)
    g_last = _each(lambda g_: jnp.sum(g_, axis=0, keepdims=True), g)
    v_new = _each(lambda u_, w_, s_: u_ - _dot1(w_, s_, "nn"), u, w, s0)
    o = _each(lambda q_, eg_, s_, at, vn: _dot1(q_ * eg_, s_, "nn") + _dot1(at, vn, "nn"), q, eg, s0, attn, v_new)
    s1 = _each(lambda s_, gl, k_, gc_, vn: s_ * jnp.exp(gl) + _dot1(k_ * jnp.exp(gl - gc_), vn, "tn"), s0, g_last, k, gc, v_new)
    return o, s1


def dn_fwd(cqkv, proj, small_cb, alog_row, dt_row, ndn, name, comm=None):
    S = cqkv.shape[0]
    C = DN_CHUNK
    nc = S // C
    half = ndn * HEAD_DIM

    def body(q_ref, k_ref, v_ref, sm_ref, al_ref, dt_ref, o_ref, ss_ref, s_ref):
        @pl.when(pl.program_id(0) == 0)
        def _():
            s_ref[...] = jnp.zeros_like(s_ref)

        sls = [slice(h * HEAD_DIM, (h + 1) * HEAD_DIM) for h in range(ndn)]
        s0 = [s_ref[h] for h in range(ndn)]
        for h in range(ndn):
            ss_ref[h, 0] = s0[h]
        o, s1 = _dn_chunk(ndn, [q_ref[:, sl] for sl in sls], [k_ref[:, sl] for sl in sls], [v_ref[:, sl] for sl in sls],
                          sm_ref[...], al_ref[...], dt_ref[...], s0)
        for h in range(ndn):
            o_ref[:, sls[h]] = o[h]
            s_ref[h] = s1[h]

    blk = lambda cb: pl.BlockSpec((C, half), lambda n: (n, cb))
    row = pl.BlockSpec((1, LANES), lambda n: (0, 0))
    return _call(body, name, (nc,),
                 [blk(0), blk(1), blk(2), pl.BlockSpec((C, LANES), lambda n: (n, small_cb)), row, row],
                 (blk(0), pl.BlockSpec((ndn, 1, HEAD_DIM, HEAD_DIM), lambda n: (0, n, 0, 0))),
                 (SDS((S, half), f32), SDS((ndn, nc, HEAD_DIM, HEAD_DIM), f32)),
                 scratch=[pltpu.VMEM((ndn, HEAD_DIM, HEAD_DIM), f32)], comm=comm)(cqkv, cqkv, cqkv, proj, alog_row, dt_row)


def dn_bwd(cqkv, proj, small_cb, alog_row, dt_row, ssave, do, ndn, name, comm=None):
    S = cqkv.shape[0]
    C = DN_CHUNK
    nc = S // C
    half = ndn * HEAD_DIM

    def body(q_ref, k_ref, v_ref, sm_ref, al_ref, dt_ref, ss_ref, do_ref, dqkv_ref, dsm_ref, dal_ref, ddt_ref, ds_ref):
        @pl.when(pl.program_id(0) == 0)
        def _():
            ds_ref[...] = jnp.zeros_like(ds_ref)
            dal_ref[...] = jnp.zeros_like(dal_ref)
            ddt_ref[...] = jnp.zeros_like(ddt_ref)

        sls = [slice(h * HEAD_DIM, (h + 1) * HEAD_DIM) for h in range(ndn)]
        _, vjp = jax.vjp(functools.partial(_dn_chunk, ndn), [q_ref[:, sl] for sl in sls], [k_ref[:, sl] for sl in sls],
                         [v_ref[:, sl] for sl in sls], sm_ref[...], al_ref[...], dt_ref[...], [ss_ref[h, 0] for h in range(ndn)])
        dq, dk, dv, dsm, dal, ddt, ds0 = vjp(([do_ref[:, sl] for sl in sls], [ds_ref[h] for h in range(ndn)]))
        for h in range(ndn):
            dqkv_ref[:, sls[h]] = dq[h]
            dqkv_ref[:, half + h * HEAD_DIM:half + (h + 1) * HEAD_DIM] = dk[h]
            dqkv_ref[:, 2 * half + h * HEAD_DIM:2 * half + (h + 1) * HEAD_DIM] = dv[h]
            ds_ref[h] = ds0[h]
        dsm_ref[...] = dsm
        dal_ref[...] += dal
        ddt_ref[...] += ddt

    rev = lambda n: nc - 1 - n
    blk = lambda cb: pl.BlockSpec((C, half), lambda n: (rev(n), cb))
    row = pl.BlockSpec((1, LANES), lambda n: (0, 0))
    return _call(body, name, (nc,),
                 [blk(0), blk(1), blk(2), pl.BlockSpec((C, LANES), lambda n: (rev(n), small_cb)), row, row,
                  pl.BlockSpec((ndn, 1, HEAD_DIM, HEAD_DIM), lambda n: (0, rev(n), 0, 0)), blk(0)],
                 (pl.BlockSpec((C, 3 * half), lambda n: (rev(n), 0)), pl.BlockSpec((C, LANES), lambda n: (rev(n), 0)), row, row),
                 (SDS((S, 3 * half), f32), SDS((S, LANES), f32), SDS((1, LANES), f32), SDS((1, LANES), f32)),
                 scratch=[pltpu.VMEM((ndn, HEAD_DIM, HEAD_DIM), f32)], comm=comm)(cqkv, cqkv, cqkv, proj, alog_row, dt_row, ssave, do)


def even_pre(proj, gq, gk, dfx, cb0, name):
    S = proj.shape[0]
    tm = _tile(S, 512)
    nh = dfx // HEAD_DIM

    def body(q_ref, k_ref, v_ref, gq_ref, gk_ref, fq_ref, fk_ref, fv_ref):
        for h in range(nh):
            sl = slice(h * HEAD_DIM, (h + 1) * HEAD_DIM)
            fq_ref[:, sl] = _rms(q_ref[:, sl], gq_ref[...]).astype(bf16)
            fk_ref[:, sl] = _rms(k_ref[:, sl], gk_ref[...]).astype(bf16)
        fv_ref[...] = v_ref[...].astype(bf16)

    sh = SDS((S, dfx), bf16)
    o = _rowspec(tm, dfx)
    return _call(body, name, (S // tm,),
                 [_rowspec(tm, dfx, cb0), _rowspec(tm, dfx, cb0 + 1), _rowspec(tm, dfx, cb0 + 2), _bspec(HEAD_DIM), _bspec(HEAD_DIM)],
                 (o, o, o), (sh, sh, sh))(proj, proj, proj, gq, gk)


def even_pre_bwd(proj, gq, gk, dfq, dfk, dfx, cb0, name):
    S = proj.shape[0]
    tm = _tile(S, 512)
    nh = dfx // HEAD_DIM

    def body(q_ref, k_ref, gq_ref, gk_ref, dfq_ref, dfk_ref, dq_ref, dk_ref, dgq_ref, dgk_ref):
        @pl.when(pl.program_id(0) == 0)
        def _():
            dgq_ref[...] = jnp.zeros_like(dgq_ref)
            dgk_ref[...] = jnp.zeros_like(dgk_ref)

        for h in range(nh):
            sl = slice(h * HEAD_DIM, (h + 1) * HEAD_DIM)
            _, vq = jax.vjp(_rms, q_ref[:, sl], gq_ref[...])
            dq, dgq = vq(dfq_ref[:, sl])
            _, vk = jax.vjp(_rms, k_ref[:, sl], gk_ref[...])
            dk, dgk = vk(dfk_ref[:, sl])
            dq_ref[:, sl] = dq
            dk_ref[:, sl] = dk
            dgq_ref[...] += dgq
            dgk_ref[...] += dgk

    sh = SDS((S, dfx), f32)
    o = _rowspec(tm, dfx)
    g = SDS((1, HEAD_DIM), f32)
    return _call(body, name, (S // tm,),
                 [_rowspec(tm, dfx, cb0), _rowspec(tm, dfx, cb0 + 1), _bspec(HEAD_DIM), _bspec(HEAD_DIM), o, o],
                 (o, o, _bspec(HEAD_DIM), _bspec(HEAD_DIM)), (sh, sh, g, g))(proj, proj, gq, gk, dfq, dfk)


def _dn_out(o, gate, gn):
    return _rms(o, gn) * (gate * jax.nn.sigmoid(gate))


def _fox_out(o, gate):
    return o * jax.nn.sigmoid(gate)


def even_post(o_dn, o_fox, proj, gn, half, cb_dn_gate, cb_fox_gate, name):
    S = proj.shape[0]
    tm = _tile(S, 512)
    nh = half // HEAD_DIM

    def body(od_ref, of_ref, gd_ref, gf_ref, gn_ref, o_ref):
        for h in range(nh):
            sl = slice(h * HEAD_DIM, (h + 1) * HEAD_DIM)
            o_ref[:, sl] = _dn_out(od_ref[:, sl], gd_ref[:, sl], gn_ref[...]).astype(bf16)
        o_ref[:, half:] = _fox_out(of_ref[...], gf_ref[...]).astype(bf16)

    return _call(body, name, (S // tm,),
                 [_rowspec(tm, half), _rowspec(tm, half), _rowspec(tm, half, cb_dn_gate), _rowspec(tm, half, cb_fox_gate), _bspec(HEAD_DIM)],
                 _rowspec(tm, 2 * half), SDS((S, 2 * half), bf16))(o_dn, o_fox, proj, proj, gn)


def even_post_bwd(o_dn, o_fox, proj, gn, dom, half, cb_dn_gate, cb_fox_gate, name):
    S = proj.shape[0]
    tm = _tile(S, 512)
    nh = half // HEAD_DIM

    def body(od_ref, of_ref, gd_ref, gf_ref, gn_ref, dod_ref, dof_ref, dd_ref, df_ref, dgd_ref, dgf_ref, dgn_ref):
        @pl.when(pl.program_id(0) == 0)
        def _():
            dgn_ref[...] = jnp.zeros_like(dgn_ref)

        for h in range(nh):
            sl = slice(h * HEAD_DIM, (h + 1) * HEAD_DIM)
            _, vjp = jax.vjp(_dn_out, od_ref[:, sl], gd_ref[:, sl], gn_ref[...])
            d_o, d_gate, d_gn = vjp(dod_ref[:, sl])
            dd_ref[:, sl] = d_o
            dgd_ref[:, sl] = d_gate
            dgn_ref[...] += d_gn
        _, vjp = jax.vjp(_fox_out, of_ref[...], gf_ref[...])
        d_o, d_gate = vjp(dof_ref[...])
        df_ref[...] = d_o
        dgf_ref[...] = d_gate

    sh = SDS((S, half), f32)
    o = _rowspec(tm, half)
    return _call(body, name, (S // tm,),
                 [o, o, _rowspec(tm, half, cb_dn_gate), _rowspec(tm, half, cb_fox_gate), _bspec(HEAD_DIM),
                  _rowspec(tm, half, 0), _rowspec(tm, half, 1)],
                 (o, o, o, o, _bspec(HEAD_DIM)), (sh, sh, sh, sh, SDS((1, HEAD_DIM), f32)))(o_dn, o_fox, proj, proj, gn, dom, dom)


def _pad_row(v, lane0=0):
    return jnp.pad(v, (lane0, LANES - lane0 - v.shape[0]))[None]


def _carried(res, comm):
    return res if comm is not None else (res, [])


def even_mixer_fwd(x, gmix, w_in, w_out, conv_w, a_log, dt_bias, gn, gq, gk, f_bias, tag, comm_fox=None, comm_dn=None,
                   comm_in=None):
    S, D = x.shape
    half = D // 2
    ndn = half // HEAD_DIM
    small_cb = 4 * D // LANES
    alog_row, dt_row, fb_row = _pad_row(a_log), _pad_row(dt_bias), _pad_row(f_bias, 2 * ndn)
    h = rms_fwd(x, gmix, f"{tag}_rms")
    proj, got_in = _carried(mm(h, w_in, "nn", f32, f"{tag}_in", tm=1024, tn=1408, comm=comm_in), comm_in)
    cqkv = conv_fwd(proj, conv_w, 3 * half, f"{tag}_conv")
    (o_dn, ssave), got_dn = _carried(dn_fwd(cqkv, proj, small_cb, alog_row, dt_row, ndn, f"{tag}_dn", comm=comm_dn), comm_dn)
    c, cT = gates_fwd(proj, small_cb, fb_row, f"{tag}_gates")
    fq, fk, fv = even_pre(proj, gq, gk, half, 4, f"{tag}_pre")
    (o_fox, lse), got_fox = _carried(fox_fwd(fq, fk, fv, c, cT, ndn, 2 * ndn, f"{tag}_fox", comm=comm_fox), comm_fox)
    om = even_post(o_dn, o_fox, proj, gn, half, 3, 7, f"{tag}_post")
    xo = mm(om, w_out, "nn", f32, f"{tag}_out", tn=1024, res=x)
    return xo, (h, proj, cqkv, o_dn, ssave, c, cT, fq, fk, fv, o_fox, lse, om, alog_row, dt_row, fb_row), got_fox, got_dn, got_in


def even_mixer_bwd(x, gmix, w_in, w_out, conv_w, gn, gq, gk, saved, dy, tag, comm_fox=None, comm_dn=None, comm_bdh=None):
    S, D = x.shape
    half = D // 2
    ndn = half // HEAD_DIM
    small_cb = 4 * D // LANES
    h, proj, cqkv, o_dn, ssave, c, cT, fq, fk, fv, o_fox, lse, om, alog_row, dt_row, fb_row = saved
    dy32, dy16 = dy
    dom = mm(dy16, w_out, "nt", f32, f"{tag}_bdom", tn=1024)
    dw_out = mm(om, dy16, "tn", bf16, f"{tag}_bwout", tn=1024)
    d_odn, d_ofox, d_dgate, d_fgate, d_gn = even_post_bwd(o_dn, o_fox, proj, gn, dom, half, 3, 7, f"{tag}_bpost")
    (dfq, dfk, dfv, dcT), got_fox = _carried(
        fox_bwd(fq, fk, fv, c, cT, o_fox, lse, d_ofox, ndn, 2 * ndn, f"{tag}_bfox", comm=comm_fox), comm_fox)
    dq_pre, dk_pre, d_gq, d_gk = even_pre_bwd(proj, gq, gk, dfq, dfk, half, 4, f"{tag}_bpre")
    (dcqkv, dsm_dn, d_alog, d_dt), got_dn = _carried(
        dn_bwd(cqkv, proj, small_cb, alog_row, dt_row, ssave, d_odn, ndn, f"{tag}_bdn", comm=comm_dn), comm_dn)
    d_conv_in, d_conv_w = conv_bwd(proj, conv_w, dcqkv, f"{tag}_bconv")
    d_small, d_fb = gates_bwd(proj, small_cb, fb_row, dcT, dsm_dn, 2 * ndn, ndn, f"{tag}_bgates")
    dproj = jnp.concatenate([d_conv_in, d_dgate, dq_pre, dk_pre, dfv, d_fgate, d_small], axis=1).astype(bf16)
    dw_in = mm(h, dproj, "tn", bf16, f"{tag}_bwin", tn=1408)
    res = mm_bdh_rms(dproj, w_in, x, gmix, dy32, f"{tag}_bdh", comm=comm_bdh)
    dx, d_gmix = res[:2]
    got_bdh = res[2] if comm_bdh is not None else []
    small = dict(norm_mix=d_gmix[0], dn_conv_w=d_conv_w, dn_a_log=d_alog[0, :ndn], dn_dt_bias=d_dt[0, :ndn],
                 dn_norm_g=d_gn[0], fox_q_norm_g=d_gq[0], fox_k_norm_g=d_gk[0], fox_f_bias=d_fb[0, 2 * ndn:3 * ndn])
    return dx, dw_in, dw_out, small, got_fox, got_dn, got_bdh


def odd_mixer_fwd(x, gmix, w_in, w_out, tag, comm=None):
    S, D = x.shape
    nh = D // HEAD_DIM
    h = rms_fwd(x, gmix, f"{tag}_rms")
    qkv = mm(h, w_in, "nn", bf16, f"{tag}_in", tm=1024, tn=768)
    (o, tails), got = _carried(sb_fwd(qkv, nh, f"{tag}_sb", comm=comm), comm)
    xo = mm(o, w_out, "nn", f32, f"{tag}_out", tn=1024, res=x)
    return xo, (h, qkv, o, tails), got


def odd_mixer_bwd(x, gmix, w_in, w_out, saved, dy, tag, comm=None):
    S, D = x.shape
    nh = D // HEAD_DIM
    h, qkv, o, tails = saved
    dy32, dy16 = dy
    do = mm(dy16, w_out, "nt", f32, f"{tag}_bdo", tn=1024)
    dw_out = mm(o, dy16, "tn", bf16, f"{tag}_bwout", tn=1024)
    (dq, dk, dv), got = _carried(sb_bwd(qkv, tails, do, nh, f"{tag}_bsb", comm=comm), comm)
    dqkv = jnp.concatenate([dq, dk, dv], axis=1).astype(bf16)
    dw_in = mm(h, dqkv, "tn", bf16, f"{tag}_bwin", tn=1536)
    dx, d_gmix = mm_bdh_rms(dqkv, w_in, x, gmix, dy32, f"{tag}_bdh")
    return dx, dw_in, dw_out, dict(norm_mix=d_gmix[0]), got


def all_gather(shards, axes, name, kind="ag"):
    return _call(None, name, (), [], [], [], comm=Comm(kind, shards, axes))()[1]


def sum_parts(parts, name):
    _, R, C = parts.shape
    tm = _tile(R, 512)

    def body(p_ref, o_ref):
        acc = p_ref[0].astype(f32)
        for k in range(1, NDEV):
            acc = acc + p_ref[k].astype(f32)
        o_ref[...] = acc

    return _call(body, name, (R // tm,), [pl.BlockSpec((NDEV, tm, C), lambda i: (0, i, 0))], _rowspec(tm, C), SDS((R, C), f32))(parts)


def adamw(w, g, m, v, name):
    L, R, C = w.shape
    tm = _tile(R, max(8, min(512, (ADAMW_TILE_ELEMS // C) // 8 * 8)))
    c1 = 1.0 - ADAM_B1 ** ADAM_STEP
    c2 = 1.0 - ADAM_B2 ** ADAM_STEP

    def body(w_ref, g_ref, m_ref, v_ref, d_ref, nm_ref, nv_ref):
        g_ = g_ref[...]
        nm = ADAM_B1 * m_ref[...] + (1.0 - ADAM_B1) * g_
        nv = ADAM_B2 * v_ref[...] + (1.0 - ADAM_B2) * (g_ * g_)
        d_ref[...] = -ADAM_LR * ((nm / c1) / (jnp.sqrt(nv / c2) + ADAM_EPS) + ADAM_WD * w_ref[...])
        nm_ref[...] = nm
        nv_ref[...] = nv

    spec = pl.BlockSpec((None, tm, C), lambda l, i: (l, i, 0))
    sh = SDS((L, R, C), f32)
    return _call(body, name, (L, R // tm), [spec] * 4, (spec, spec, spec), (sh, sh, sh))(w, g, m, v)


def _pad_to(n, mult):
    return -(-n // mult) * mult


def _even_perm(D):
    half = D // 2
    ndn = half // HEAD_DIM
    o_gate = 3 * half
    o_b = o_gate + half
    o_a = o_b + ndn
    o_fq = o_a + ndn
    o_fpre = o_fq + 4 * half
    return half, ndn, o_b, o_a, o_fq, o_fpre


def _even_to_mine(w):
    D = (w.shape[-1] // 4) // LANES * LANES
    half, ndn, o_b, o_a, o_fq, o_fpre = _even_perm(D)
    small = jnp.concatenate([w[..., o_b:o_b + ndn], w[..., o_a:o_a + ndn], w[..., o_fpre:o_fpre + ndn]], axis=-1)
    small = jnp.pad(small, [(0, 0)] * (w.ndim - 1) + [(0, LANES - 3 * ndn)])
    return jnp.concatenate([w[..., :o_b], w[..., o_fq:o_fpre], small], axis=-1)


def _even_from_mine(w, D):
    half, ndn, o_b, o_a, o_fq, o_fpre = _even_perm(D)
    sm = w[..., 4 * D:]
    return jnp.concatenate([w[..., :o_b], sm[..., :ndn], sm[..., ndn:2 * ndn], w[..., o_b:4 * D], sm[..., 2 * ndn:3 * ndn]], axis=-1)


def _pack_small(parts):
    flat = jnp.concatenate([p.reshape(-1).astype(f32) for p in parts])
    n = flat.shape[0]
    return jnp.pad(flat, (0, _pad_to(n, 8 * LANES) - n)).reshape(-1, LANES)


def _unpack_small(packed, like):
    flat = packed.reshape(-1)
    out, off = [], 0
    for p in like:
        out.append(flat[off:off + p.size].reshape(p.shape))
        off += p.size
    return out


SMALL_NAMES = ("norm_ffn1", "norm_mix", "dn_a_log", "dn_dt_bias", "dn_norm_g", "fox_q_norm_g", "fox_k_norm_g", "fox_f_bias", "norm_ffn2")
BIG_NAMES = ("ffn1_w_gu", "ffn1_w_down", "w_in_even", "dn_conv_w", "w_out_even", "w_in_odd", "w_out_odd", "ffn2_w_gu", "ffn2_w_down")
WEIGHT_NAMES = ("norm_ffn1", "ffn1_w_gu", "ffn1_w_down", "norm_mix", "w_in_even", "dn_conv_w", "dn_a_log", "dn_dt_bias", "dn_norm_g",
                "fox_q_norm_g", "fox_k_norm_g", "fox_f_bias", "w_out_even", "w_in_odd", "w_out_odd", "norm_ffn2", "ffn2_w_gu", "ffn2_w_down")


def kernel(x, norm_ffn1, ffn1_w_gu, ffn1_w_down, norm_mix, w_in_even, dn_conv_w, dn_a_log, dn_dt_bias, dn_norm_g, fox_q_norm_g, fox_k_norm_g, fox_f_bias, w_out_even, w_in_odd, w_out_odd, norm_ffn2, ffn2_w_gu, ffn2_w_down, loss_target, m_norm_ffn1, m_ffn1_w_gu, m_ffn1_w_down, m_norm_mix, m_w_in_even, m_dn_conv_w, m_dn_a_log, m_dn_dt_bias, m_dn_norm_g, m_fox_q_norm_g, m_fox_k_norm_g, m_fox_f_bias, m_w_out_even, m_w_in_odd, m_w_out_odd, m_norm_ffn2, m_ffn2_w_gu, m_ffn2_w_down, v_norm_ffn1, v_ffn1_w_gu, v_ffn1_w_down, v_norm_mix, v_w_in_even, v_dn_conv_w, v_dn_a_log, v_dn_dt_bias, v_dn_norm_g, v_fox_q_norm_g, v_fox_k_norm_g, v_fox_f_bias, v_w_out_even, v_w_in_odd, v_w_out_odd, v_norm_ffn2, v_ffn2_w_gu, v_ffn2_w_down):
    args = dict(locals())
    W = {n: args[n] for n in WEIGHT_NAMES}
    M = {n: args["m_" + n] for n in WEIGHT_NAMES}
    V = {n: args["v_" + n] for n in WEIGHT_NAMES}
    xs = x[0]
    tgt = loss_target[0]
    S, D = xs.shape
    L = norm_ffn1.shape[0]
    n_even = w_in_even.shape[0]
    hs = ffn1_w_down.shape[1]
    hp = _pad_to(hs, LANES)
    me = 4 * lax.axis_index("x") + 2 * lax.axis_index("y") + lax.axis_index("c")

    def prep_gu(w):
        w = w.reshape(L, D, 2, hs)
        return jnp.pad(w, ((0, 0), (0, 0), (0, 0), (0, hp - hs))).reshape(L, D, 2 * hp).astype(bf16)

    def prep_down(w):
        return jnp.pad(w, ((0, 0), (0, hp - hs), (0, 0))).astype(bf16)

    sh_gu1, sh_d1, sh_gu2, sh_d2 = prep_gu(ffn1_w_gu), prep_down(ffn1_w_down), prep_gu(ffn2_w_gu), prep_down(ffn2_w_down)
    sh_ie, sh_oe = _even_to_mine(w_in_even).astype(bf16), w_out_even.astype(bf16)
    sh_io, sh_oo = w_in_odd.astype(bf16), w_out_odd.astype(bf16)

    def layer_shards(l):
        j = l // 2
        mix = [sh_ie[j], sh_oe[j]] if l % 2 == 0 else [sh_io[j], sh_oo[j]]
        return [sh_gu1[l], sh_d1[l], *mix, sh_gu2[l], sh_d2[l]]

    def layer_axes(l):
        return [1, 0, 0 if l % 2 == 0 else 1, 0, 1, 0]

    def layer_names(l):
        return ["ffn1_w_gu", "ffn1_w_down", *(["w_in_even", "w_out_even"] if l % 2 == 0 else ["w_in_odd", "w_out_odd"]),
                "ffn2_w_gu", "ffn2_w_down"]

    FOX_CARRIES, DN_CARRIES = (0, 1, 3), (2, 4, 5)

    def split_comm(kind, arrays, axes):
        return (Comm(kind, [arrays[i] for i in FOX_CARRIES], [axes[i] for i in FOX_CARRIES]),
                Comm(kind, [arrays[i] for i in DN_CARRIES], [axes[i] for i in DN_CARRIES]))

    def join_split(got_fox, got_dn):
        out = [None] * 6
        for i, a in zip(FOX_CARRIES, got_fox):
            out[i] = a
        for i, a in zip(DN_CARRIES, got_dn):
            out[i] = a
        return out

    sh0, ax0 = layer_shards(0), layer_axes(0)
    first = all_gather(sh0[:2] + [dn_conv_w[None]], ax0[:2] + [0], "ag_layer0", kind="ag2")
    weights = {0: first[:2] + [None] * 4}
    convw = jnp.moveaxis(first[2], 0, 2).reshape(n_even, CONV_WIDTH, -1)
    LAYER0_CARRIES = dict(f1_gu=(2,), f1_down=(3,), mx_in=(5,))

    EVEN_FWD_CARRIES = dict(f1_gu=(), mx_in=(), dn=(1, 2, 5), fox=(0, 4, 3), f2_gu=(), f2_down=())
    saved = []
    cur = xs
    for l in range(L):
        j = l // 2
        wgu1, wd1, win, wout, wgu2, wd2 = weights[l]
        nxt = l + 1 < L
        x0 = cur
        if l % 2 == 0:
            sh_n, ax_n = (layer_shards(l + 1), layer_axes(l + 1)) if nxt else (None, None)
            cm = {k: (Comm("ag2", [sh_n[i] for i in idx], [ax_n[i] for i in idx]) if nxt and idx else None)
                  for k, idx in EVEN_FWD_CARRIES.items()}
            if l == 0:
                cm.update({k: Comm("ag2", [sh0[i] for i in idx], [ax0[i] for i in idx]) for k, idx in LAYER0_CARRIES.items()})
                cm["fox"] = Comm("ag2", cm["fox"].arrays + [sh0[4]], cm["fox"].axes + [ax0[4]])
            x1, s1, got_f1gu, got_f1down = ffn_fwd(x0, norm_ffn1[l:l + 1], wgu1, wd1, f"l{l}f1", comm_gu=cm["f1_gu"],
                                                   comm_down=cm["f1_down"] if l == 0 else None)
            if l == 0:
                win, wout = got_f1gu[0], got_f1down[0]
            x2, sm, got_f, got_d, got_in = even_mixer_fwd(
                x1, norm_mix[l:l + 1], win, wout, convw[j], dn_a_log[j], dn_dt_bias[j], dn_norm_g[j:j + 1],
                fox_q_norm_g[j:j + 1], fox_k_norm_g[j:j + 1], fox_f_bias[j], f"l{l}mx", comm_fox=cm["fox"], comm_dn=cm["dn"],
                comm_in=cm["mx_in"])
            if l == 0:
                wd2, wgu2 = got_in[0], got_f[-1]
                weights[0] = [wgu1, wd1, win, wout, wgu2, wd2]
                got_f1gu, got_in, got_f = [], [], got_f[:-1]
            x3, s2, got_f2gu, got_f2down = ffn_fwd(x2, norm_ffn2[l:l + 1], wgu2, wd2, f"l{l}f2", comm_gu=cm["f2_gu"],
                                                   comm_down=cm["f2_down"])
            if nxt:
                got = dict(f1_gu=got_f1gu, mx_in=got_in, dn=got_d, fox=got_f, f2_gu=got_f2gu, f2_down=got_f2down)
                weights[l + 1] = [None] * 6
                for k, idx in EVEN_FWD_CARRIES.items():
                    for i, a in zip(idx, got[k]):
                        weights[l + 1][i] = a
        else:
            x1, s1, _, _ = ffn_fwd(x0, norm_ffn1[l:l + 1], wgu1, wd1, f"l{l}f1")
            cm = Comm("ag2", layer_shards(l + 1), layer_axes(l + 1)) if nxt else None
            x2, sm, got = odd_mixer_fwd(x1, norm_mix[l:l + 1], win, wout, f"l{l}mx", comm=cm)
            if nxt:
                weights[l + 1] = got
            x3, s2, _, _ = ffn_fwd(x2, norm_ffn2[l:l + 1], wgu2, wd2, f"l{l}f2")
        saved.append((x0, x1, x2, s1, sm, s2))
        cur = x3
    dy, loss_row = loss_head(cur, tgt, "loss")

    gsmall = {n: [None] * W[n].shape[0] for n in SMALL_NAMES}
    gconv = [None] * n_even
    parts = {n: [None] * W[n].shape[0] for n in BIG_NAMES if n != "dn_conv_w"}

    def take(l, recv):
        for nm, p in zip(layer_names(l), recv):
            idx = l if nm.startswith("ffn") else l // 2
            parts[nm][idx] = sum_parts(p.reshape(NDEV, -1, p.shape[-1]), f"sum_{nm}_{idx}").reshape(p.shape[1:])

    pending = None
    for l in reversed(range(L)):
        j = l // 2
        wgu1, wd1, win, wout, wgu2, wd2 = weights[l]
        x0, x1, x2, s1, sm, s2 = saved[l]
        dy, dg, dwgu2, dwd2 = ffn_bwd(x2, norm_ffn2[l:l + 1], wgu2, wd2, s2, dy, f"l{l}f2")
        gsmall["norm_ffn2"][l] = dg[0]
        if l % 2 == 0:
            cf, cd = split_comm("rs", pending, layer_axes(l + 1)) if pending is not None else (None, None)
            cb = None
            if l == 0:
                cf = Comm("rs", cf.arrays + [dwd2], cf.axes + [ax0[5]])
                cb = Comm("rs", [dwgu2], [ax0[4]])
            dy, dwin, dwout, sg, got_f, got_d, got_b = even_mixer_bwd(
                x1, norm_mix[l:l + 1], win, wout, convw[j], dn_norm_g[j:j + 1], fox_q_norm_g[j:j + 1],
                fox_k_norm_g[j:j + 1], sm, dy, f"l{l}mx", comm_fox=cf, comm_dn=cd, comm_bdh=cb)
            if l == 0:
                got_wd2, got_wgu2, got_f = got_f[-1], got_b[0], got_f[:-1]
            if pending is not None:
                take(l + 1, join_split(got_f, got_d))
            gconv[j] = sg.pop("dn_conv_w")
            for k_, v_ in sg.items():
                gsmall[k_][l if k_ == "norm_mix" else j] = v_
        else:
            cm = Comm("rs", pending, layer_axes(l + 1)) if pending is not None else None
            dy, dwin, dwout, sg, got = odd_mixer_bwd(x1, norm_mix[l:l + 1], win, wout, sm, dy, f"l{l}mx", comm=cm)
            if pending is not None:
                take(l + 1, got)
            gsmall["norm_mix"][l] = sg["norm_mix"]
        if l > 0:
            dy, dg, dwgu1, dwd1 = ffn_bwd(x0, norm_ffn1[l:l + 1], wgu1, wd1, s1, dy, f"l{l}f1")
        else:
            rs = lambda a, ax: Comm("rs", [a], [ax])
            dy, dg, dwgu1, dwd1, got0 = ffn_bwd(x0, norm_ffn1[l:l + 1], wgu1, wd1, s1, dy, f"l{l}f1",
                                                comms=dict(bda=rs(dwin, ax0[2]), bwd=rs(dwout, ax0[3])), own_axes=(ax0[0], ax0[1]))
        gsmall["norm_ffn1"][l] = dg[0]
        pending = [dwgu1, dwd1, dwin, dwout, dwgu2, dwd2]
    take(0, [got0["wgu"][0], got0["wd"][0], got0["bda"][0], got0["bwd"][0], got_wgu2, got_wd2])
    grad_x = dy[0][None]

    small_list = [jnp.stack(gsmall[n]) for n in SMALL_NAMES] + [jnp.stack(gconv), loss_row[0, :1]]
    packed = _pack_small(small_list)
    gathered = all_gather([packed], [0], "ag_small")[0]
    summed = sum_parts(gathered.reshape(NDEV, packed.shape[0], LANES), "sum_small")
    small_sum = _unpack_small(summed, small_list)
    G = dict(zip(SMALL_NAMES, small_sum[:len(SMALL_NAMES)]))
    conv_full = small_sum[len(SMALL_NAMES)]
    cwid = dn_conv_w.shape[2]
    G["dn_conv_w"] = lax.dynamic_slice_in_dim(conv_full, me * cwid, cwid, axis=2)
    loss = small_sum[-1][0]

    def unprep_gu(g):
        return g.reshape(L, D, 2, hp)[..., :hs].reshape(L, D, 2 * hs)

    G["ffn1_w_gu"] = unprep_gu(jnp.stack(parts["ffn1_w_gu"]))
    G["ffn2_w_gu"] = unprep_gu(jnp.stack(parts["ffn2_w_gu"]))
    G["ffn1_w_down"] = jnp.stack(parts["ffn1_w_down"])[:, :hs]
    G["ffn2_w_down"] = jnp.stack(parts["ffn2_w_down"])[:, :hs]
    G["w_in_even"] = _even_from_mine(jnp.stack(parts["w_in_even"]), D)
    G["w_out_even"] = jnp.stack(parts["w_out_even"])
    G["w_in_odd"] = jnp.stack(parts["w_in_odd"])
    G["w_out_odd"] = jnp.stack(parts["w_out_odd"])

    delta, new_m, new_v = {}, {}, {}
    for n in BIG_NAMES:
        t = (lambda a: jnp.swapaxes(a, 1, 2)) if n.endswith("w_gu") else (lambda a: a)
        delta[n], new_m[n], new_v[n] = map(t, adamw(t(W[n]), t(G[n]), t(M[n]), t(V[n]), f"adamw_{n}"))
    sw = [W[n] for n in SMALL_NAMES]
    d_, m_, v_ = adamw(_pack_small(sw)[None], _pack_small([G[n] for n in SMALL_NAMES])[None],
                       _pack_small([M[n] for n in SMALL_NAMES])[None], _pack_small([V[n] for n in SMALL_NAMES])[None], "adamw_small")
    for n, a, b, c_ in zip(SMALL_NAMES, _unpack_small(d_, sw), _unpack_small(m_, sw), _unpack_small(v_, sw)):
        delta[n], new_m[n], new_v[n] = a, b, c_

    return (loss, grad_x, *[G[n] for n in WEIGHT_NAMES], *[delta[n] for n in WEIGHT_NAMES],
            *[new_m[n] for n in WEIGHT_NAMES], *[new_v[n] for n in WEIGHT_NAMES])
```

```python
import functools
import math

import jax
import jax.numpy as jnp
from jax import lax
from jax.experimental import pallas as pl
from jax.experimental.pallas import tpu as pltpu

f32 = jnp.float32
bf16 = jnp.bfloat16
SDS = jax.ShapeDtypeStruct

HEAD_DIM = 128
LANES = 128
CONV_WIDTH = 4
DN_CHUNK = 128
EPS = 1e-6
NDEV = 8
VMEM_LIMIT_BYTES = 56 * 1024 * 1024
HI = lax.Precision.HIGHEST
ADAMW_TILE_ELEMS = 384 * 1024

ADAM_LR = 0.001
ADAM_B1 = 0.9
ADAM_B2 = 0.999
ADAM_EPS = 1e-08
ADAM_WD = 0.01
ADAM_STEP = 10

NN = (((1,), (0,)), ((), ()))
NT = (((1,), (1,)), ((), ()))
TN = (((0,), (0,)), ((), ()))


def _me_and_peers():
    x, y, c = lax.axis_index("x"), lax.axis_index("y"), lax.axis_index("c")
    me = 4 * x + 2 * y + c
    peers = []
    for r in range(1, NDEV):
        px = 1 - x if (r >> 2) & 1 else x
        py = 1 - y if (r >> 1) & 1 else y
        pc = 1 - c if r & 1 else c
        peers.append(((px, py, pc), 4 * px + 2 * py + pc))
    return me, peers


def _slab(ref, axis, width, k):
    idx = [slice(None)] * len(ref.shape)
    idx[axis] = pl.ds(k * width, width)
    return ref.at[tuple(idx)]


class Comm:
    def __init__(self, kind, arrays, axes):
        self.kind, self.arrays, self.axes, self.n = kind, list(arrays), list(axes), len(arrays)

    def out_shape(self):
        out = []
        for a, ax in zip(self.arrays, self.axes):
            shp = list(a.shape)
            if self.kind != "rs":
                shp[ax] *= NDEV
                out.append(SDS(tuple(shp), a.dtype))
            else:
                shp[ax] //= NDEV
                out.append(SDS((NDEV, *shp), a.dtype))
        return out

    def sems(self):
        return [pltpu.SemaphoreType.DMA((self.n, NDEV - 1)), pltpu.SemaphoreType.DMA((self.n, NDEV - 1)),
                pltpu.SemaphoreType.DMA((self.n,))]

    def _src(self, ins, t, to_idx):
        if self.kind == "ag":
            return ins[t]
        return _slab(ins[t], self.axes[t], ins[t].shape[self.axes[t]] // NDEV, to_idx)

    def _dst(self, ins, outs, t, from_idx):
        if self.kind != "rs":
            return _slab(outs[t], self.axes[t], ins[t].shape[self.axes[t]], from_idx)
        return outs[t].at[from_idx]

    def _copies(self, ins, outs, sems, sending):
        send_sems, recv_sems, loc_sems = sems
        me, peers = _me_and_peers()
        local, remote = [], []
        for t in range(self.n):
            local.append(pltpu.make_async_copy(self._src(ins, t, me), self._dst(ins, outs, t, me), loc_sems.at[t]))
            for r, (peer, pidx) in enumerate(peers):
                remote.append(pltpu.make_async_remote_copy(
                    src_ref=self._src(ins, t, pidx), dst_ref=self._dst(ins, outs, t, me if sending else pidx),
                    send_sem=send_sems.at[t, r], recv_sem=recv_sems.at[t, r], device_id=peer,
                    device_id_type=pl.DeviceIdType.MESH))
        return local, remote

    def _two_level(self, ins, outs, sems, own=True):
        send_sems, recv_sems, loc_sems = sems
        x, y, c = lax.axis_index("x"), lax.axis_index("y"), lax.axis_index("c")
        me, sibling = (x, y, c), (x, y, 1 - c)
        chips = [(1 - x, y), (x, 1 - y), (1 - x, 1 - y)]
        dev = lambda p: 4 * p[0] + 2 * p[1] + p[2]

        def copy(t, k, block, to, from_shard):
            dst = self._dst(ins, outs, t, dev(block))
            return pltpu.make_async_remote_copy(
                src_ref=ins[t] if from_shard else dst, dst_ref=dst, send_sem=send_sems.at[t, k], recv_sem=recv_sems.at[t, k],
                device_id=to, device_id_type=pl.DeviceIdType.MESH)

        ts = range(self.n) if own else ()
        local = [pltpu.make_async_copy(ins[t], self._dst(ins, outs, t, dev(me)), loc_sems.at[t]) for t in ts]
        first = [copy(t, 0, me, sibling, True) for t in ts]
        first += [copy(t, 1 + j, me, (*chip, c), True) for t in ts for j, chip in enumerate(chips)]
        return local, first, copy, chips, me, sibling, c

    def start(self, ins, outs, sems):
        if self.kind == "ag2":
            local, first = self._two_level(ins, outs, sems)[:2]
            for cp in local + first:
                cp.start()
            return
        local, remote = self._copies(ins, outs, sems, True)
        for cp in local + remote:
            cp.start()

    def relay(self, ins, outs, sems):
        _, _, copy, chips, me, sibling, c = self._two_level(ins, outs, sems, own=False)
        for t in range(self.n):
            for j, chip in enumerate(chips):
                copy(t, 1 + j, (*chip, c), me, False).wait_recv()
                copy(t, 4 + j, (*chip, c), sibling, False).start()

    def wait(self, ins, outs, sems, relayed=False):
        if self.kind == "ag2":
            if not relayed:
                self.relay(ins, outs, sems)
            local, first, copy, chips, me, sibling, c = self._two_level(ins, outs, sems)
            passed = [copy(t, 4 + j, (*chip, c), sibling, False) for t in range(self.n) for j, chip in enumerate(chips)]
            for t in range(self.n):
                copy(t, 0, sibling, me, False).wait_recv()
                for j, chip in enumerate(chips):
                    copy(t, 4 + j, (*chip, 1 - c), me, False).wait_recv()
            for cp in first + passed:
                cp.wait_send()
            for cp in local:
                cp.wait()
            return
        local, remote = self._copies(ins, outs, sems, False)
        for cp in remote:
            cp.wait_recv()
            cp.wait_send()
        for cp in local:
            cp.wait()


def _call(body, name, grid, in_specs, out_specs, out_shape, scratch=(), comm=None):
    params = pltpu.CompilerParams(vmem_limit_bytes=VMEM_LIMIT_BYTES)
    if comm is None:
        return pl.pallas_call(body, name=name, grid=grid, in_specs=in_specs, out_specs=out_specs, out_shape=out_shape,
                              scratch_shapes=list(scratch), compiler_params=params)
    single = not isinstance(out_shape, (tuple, list))
    c_out_specs = [out_specs] if single else list(out_specs)
    c_out_shape = [out_shape] if single else list(out_shape)
    n_in, n_out, n_scr, n = len(in_specs), len(c_out_shape), len(scratch), comm.n
    anyspec = pl.BlockSpec(memory_space=pl.ANY)

    def wrapped(*refs):
        ins, refs = refs[:n_in], refs[n_in:]
        cins, refs = refs[:n], refs[n:]
        outs, refs = refs[:n_out], refs[n_out:]
        couts, refs = refs[:n], refs[n:]
        scr, sems = refs[:n_scr], refs[n_scr:]
        first = functools.reduce(lambda a, b: a & b, [pl.program_id(d) == 0 for d in range(len(grid))], True)
        last = functools.reduce(lambda a, b: a & b, [pl.program_id(d) == grid[d] - 1 for d in range(len(grid))], True)
        if grid:
            steps = math.prod(grid)
            step = functools.reduce(lambda a, d: a * grid[d] + pl.program_id(d), range(len(grid)), 0)
            relay_early = comm.kind == "ag2" and steps >= 4
            pl.when(first)(lambda: comm.start(cins, couts, sems))
            body(*ins, *outs, *scr)
            if relay_early:
                pl.when(step == (3 * steps) // 4)(lambda: comm.relay(cins, couts, sems))
            pl.when(last)(lambda: comm.wait(cins, couts, sems, relayed=relay_early))
        else:
            comm.start(cins, couts, sems)
            if body is not None:
                body(*ins, *outs, *scr)
            comm.wait(cins, couts, sems)

    call = pl.pallas_call(
        wrapped, name=name, grid=grid, in_specs=list(in_specs) + [anyspec] * n, out_specs=c_out_specs + [anyspec] * n,
        out_shape=c_out_shape + comm.out_shape(), scratch_shapes=list(scratch) + comm.sems(), compiler_params=params)

    def run(*args):
        res = call(*args, *comm.arrays)
        mine = res[:n_out]
        return (mine[0] if single else tuple(mine)), list(res[n_out:])

    return run


def _tile(n, want, align=8):
    if n <= want:
        return n
    for t in range(want // align * align, 0, -align):
        if n % t == 0:
            return t
    return n


def _dot(a, b, dims=NN, precision=None):
    return lax.dot_general(a, b, dims, preferred_element_type=f32, precision=precision)


def _logsig(x):
    return jnp.minimum(x, 0.0) - jnp.log(1.0 + jnp.exp(-jnp.abs(x)))


def _softplus(x):
    return jnp.maximum(x, 0.0) + jnp.log(1.0 + jnp.exp(-jnp.abs(x)))


def _rms(x, g):
    return x * lax.rsqrt(jnp.mean(x * x, axis=-1, keepdims=True) + EPS) * g


def _lane_pick(blk, lane_idx):
    lane = lax.broadcasted_iota(jnp.int32, (1, LANES), 1)
    return jnp.sum(jnp.where(lane == lane_idx, blk, 0.0), axis=1, keepdims=True)


def mm(a, b, mode, out_dtype, name, tm=512, tn=512, res=None, scale=1.0, comm=None):
    if mode == "nn":
        (M, K), (_, N) = a.shape, b.shape
    elif mode == "nt":
        (M, K), (N, _) = a.shape, b.shape
    else:
        (K, M), (_, N) = a.shape, b.shape
    tm, tn = _tile(M, tm, LANES), _tile(N, tn, LANES)
    a_spec = pl.BlockSpec((K, tm), lambda j, i: (0, i)) if mode == "tn" else pl.BlockSpec((tm, K), lambda j, i: (i, 0))
    b_spec = pl.BlockSpec((tn, K), lambda j, i: (j, 0)) if mode == "nt" else pl.BlockSpec((K, tn), lambda j, i: (0, j))
    dims = {"nn": NN, "nt": NT, "tn": TN}[mode]
    o_spec = pl.BlockSpec((tm, tn), lambda j, i: (i, j))

    def body(*refs):
        acc = _dot(refs[0][...].astype(bf16), refs[1][...].astype(bf16), dims)
        if scale != 1.0:
            acc = acc * scale
        if res is not None:
            acc = acc + refs[2][...]
        refs[-1][...] = acc.astype(out_dtype)

    ins, specs = [a, b], [a_spec, b_spec]
    if res is not None:
        ins.append(res)
        specs.append(o_spec)
    return _call(body, name, (N // tn, M // tm), specs, o_spec, SDS((M, N), out_dtype), comm=comm)(*ins)


def _rowspec(tm, w, cb=0):
    return pl.BlockSpec((tm, w), lambda i: (i, cb))


def _bspec(w):
    return pl.BlockSpec((1, w), lambda i: (0, 0))


def rms_fwd(x, g, name):
    S, D = x.shape
    tm = _tile(S, 512)

    def body(x_ref, g_ref, h_ref):
        h_ref[...] = _rms(x_ref[...], g_ref[...]).astype(bf16)

    return _call(body, name, (S // tm,), [_rowspec(tm, D), _bspec(D)], _rowspec(tm, D), SDS((S, D), bf16))(x, g)


def _swiglu(g, u):
    return g * jax.nn.sigmoid(g) * u


def ffn_gu_act(h, wgu, name, tm=1024, tn=768, comm=None):
    S, D = h.shape
    W = wgu.shape[1] // 2
    tm, tn = _tile(S, tm, LANES), _tile(W, tn, LANES)
    nb = W // tn

    def body(h_ref, wg_ref, wu_ref, gu_ref, a_ref):
        hb = h_ref[...]
        g = _dot(hb, wg_ref[...])
        u = _dot(hb, wu_ref[...])
        gu_ref[0] = g.astype(bf16)
        gu_ref[1] = u.astype(bf16)
        a_ref[...] = _swiglu(g, u).astype(bf16)

    return _call(body, name, (nb, S // tm),
                 [pl.BlockSpec((tm, D), lambda j, i: (i, 0)), pl.BlockSpec((D, tn), lambda j, i: (0, j)),
                  pl.BlockSpec((D, tn), lambda j, i: (0, nb + j))],
                 (pl.BlockSpec((2, tm, tn), lambda j, i: (0, i, j)), pl.BlockSpec((tm, tn), lambda j, i: (i, j))),
                 (SDS((2, S, W), bf16), SDS((S, W), bf16)), comm=comm)(h, wgu, wgu)


def ffn_bda_act(dy, wd, gu, scale, name, tm=1024, tn=768, comm=None):
    S, D = dy.shape
    W = wd.shape[0]
    tm, tn = _tile(S, tm, LANES), _tile(W, tn, LANES)

    def body(dy_ref, wd_ref, gu_ref, dgu_ref):
        da = _dot(dy_ref[...].astype(bf16), wd_ref[...], NT) * scale
        _, vjp = jax.vjp(_swiglu, gu_ref[0].astype(f32), gu_ref[1].astype(f32))
        dg, du = vjp(da)
        dgu_ref[0] = dg.astype(bf16)
        dgu_ref[1] = du.astype(bf16)

    planes = pl.BlockSpec((2, tm, tn), lambda j, i: (0, i, j))
    return _call(body, name, (W // tn, S // tm),
                 [pl.BlockSpec((tm, D), lambda j, i: (i, 0)), pl.BlockSpec((tn, D), lambda j, i: (j, 0)), planes],
                 planes, SDS((2, S, W), bf16), comm=comm)(dy, wd, gu)


def ffn_bwgu(h, dgu, name, tm=1024, tn=768, comm=None):
    S, D = h.shape
    W = dgu.shape[2]
    tm, tn = _tile(D, tm, LANES), _tile(W, tn, LANES)
    nb = W // tn

    def body(h_ref, d_ref, o_ref):
        o_ref[...] = _dot(h_ref[...], d_ref[...], TN).astype(bf16)

    return _call(body, name, (2 * nb, D // tm),
                 [pl.BlockSpec((S, tm), lambda j, i: (0, i)), pl.BlockSpec((None, S, tn), lambda j, i: (j // nb, 0, j % nb))],
                 pl.BlockSpec((tm, tn), lambda j, i: (i, j)), SDS((D, 2 * W), bf16), comm=comm)(h, dgu)


def nt_rms_bwd(pairs, x, g, dres, name, tm=512, comm=None):
    S, D = x.shape
    tm = _tile(S, tm)
    n = len(pairs)

    def body(*refs):
        x_ref, g_ref, dres_ref = refs[2 * n:2 * n + 3]
        dx_ref, dxb_ref, dg_ref = refs[2 * n + 3:]
        dh = sum(_dot(refs[2 * k][...].astype(bf16), refs[2 * k + 1][...], NT) for k in range(n))
        _, vjp = jax.vjp(_rms, x_ref[...], g_ref[...])
        dx, dg = vjp(dh)
        dx = dres_ref[...] + dx
        dx_ref[...] = dx
        dxb_ref[...] = dx.astype(bf16)

        @pl.when(pl.program_id(0) == 0)
        def _():
            dg_ref[...] = jnp.zeros_like(dg_ref)

        dg_ref[...] += dg

    arrays, specs = [], []
    for a, a_spec, b, b_spec in pairs:
        arrays += [a, b]
        specs += [a_spec, b_spec]
    (dx, dxb, dg), got = _carried(_call(body, name, (S // tm,), specs + [_rowspec(tm, D), _bspec(D), _rowspec(tm, D)],
                                        (_rowspec(tm, D), _rowspec(tm, D), _bspec(D)),
                                        (SDS((S, D), f32), SDS((S, D), bf16), SDS((1, D), f32)), comm=comm)(*arrays, x, g, dres),
                                  comm)
    return ((dx, dxb), dg) if comm is None else ((dx, dxb), dg, got)


def ffn_bdh_rms(dgu, wgu, x, g, dres, name, tm=512, comm=None):
    _, S, W = dgu.shape
    D = wgu.shape[0]
    tm = _tile(S, tm)
    pairs = [(dgu, pl.BlockSpec((None, tm, W), functools.partial(lambda p, i: (p, i, 0), p)),
              wgu, pl.BlockSpec((D, W), functools.partial(lambda p, i: (0, p), p))) for p in range(2)]
    return nt_rms_bwd(pairs, x, g, dres, name, tm, comm=comm)


def mm_bdh_rms(d, w, x, g, dres, name, tm=512, comm=None):
    S, K = d.shape
    tm = _tile(S, tm)
    return nt_rms_bwd([(d, pl.BlockSpec((tm, K), lambda i: (i, 0)), w, pl.BlockSpec(w.shape, lambda i: (0, 0)))], x, g, dres, name, tm,
                      comm=comm)


def loss_head(y, t, name):
    S, D = y.shape
    tm = _tile(S, 512)

    def body(y_ref, t_ref, dy_ref, dyb_ref, l_ref):
        e = y_ref[...] - t_ref[...]
        dy_ref[...] = e * (1.0 / D)
        dyb_ref[...] = (e * (1.0 / D)).astype(bf16)

        @pl.when(pl.program_id(0) == 0)
        def _():
            l_ref[...] = jnp.zeros_like(l_ref)

        l_ref[...] += jnp.sum(jnp.sum(e * e, axis=1, keepdims=True), axis=0, keepdims=True) * (0.5 / D)

    dy, dyb, loss = _call(body, name, (S // tm,), [_rowspec(tm, D), _rowspec(tm, D)],
                          (_rowspec(tm, D), _rowspec(tm, D), _bspec(LANES)),
                          (SDS((S, D), f32), SDS((S, D), bf16), SDS((1, LANES), f32)))(y, t)
    return (dy, dyb), loss


def ffn_fwd(x, g, wgu, wd, tag, comm_gu=None, comm_down=None):
    h = rms_fwd(x, g, f"{tag}_rms")
    (gu, a), got_gu = _carried(ffn_gu_act(h, wgu, f"{tag}_gu", comm=comm_gu), comm_gu)
    xo, got_down = _carried(mm(a, wd, "nn", f32, f"{tag}_down", tm=512, tn=1024, res=x, scale=0.5, comm=comm_down), comm_down)
    return xo, (h, gu, a), got_gu, got_down


def ffn_bwd(x, g, wgu, wd, saved, dy, tag, comms=None, own_axes=None):
    h, gu, a = saved
    dy32, dy16 = dy
    cm = comms or {}
    dgu, got_bda = _carried(ffn_bda_act(dy16, wd, gu, 0.5, f"{tag}_bda", comm=cm.get("bda")), cm.get("bda"))
    dwd, got_bwd = _carried(mm(a, dy16, "tn", bf16, f"{tag}_bwd", tm=512, tn=1024, scale=0.5, comm=cm.get("bwd")), cm.get("bwd"))
    c_wd = Comm("rs", [dwd], [own_axes[1]]) if own_axes else None
    dwgu, got_wd = _carried(ffn_bwgu(h, dgu, f"{tag}_bwgu", comm=c_wd), c_wd)
    c_wgu = Comm("rs", [dwgu], [own_axes[0]]) if own_axes else None
    res = ffn_bdh_rms(dgu, wgu, x, g, dy32, f"{tag}_bdh", comm=c_wgu)
    dx, dg = res[:2]
    if comms is None and own_axes is None:
        return dx, dg, dwgu, dwd
    return dx, dg, dwgu, dwd, dict(bda=got_bda, bwd=got_bwd, wd=got_wd, wgu=res[2] if c_wgu is not None else [])


def _split_bf16(x):
    hi = x.astype(bf16)
    lo = (x - hi.astype(f32)).astype(bf16)
    return hi, lo


SB_TQ, SB_TK, SB_NSUB = 512, 256, 4
FOX_TK = 256


def _sb_geometry(S, tk=SB_TK):
    tq = min(SB_TQ, S)
    tk = min(tk, tq)
    return tq, tk, tq // SB_NSUB, tq // tk


def _sb_consts(tk):
    r = lax.broadcasted_iota(jnp.int32, (tk, tk), 0)
    c = lax.broadcasted_iota(jnp.int32, (tk, tk), 1)
    return (r > c).astype(bf16), (r < c).astype(bf16)


def _active(d, nsub, rs, tk):
    return [s for s in range(nsub) if d is None or (s + 1) * rs > d * tk]


def _put(full, act, part):
    out = list(full)
    for s, x in zip(act, part):
        out[s] = x
    return out


def _sb_scores(qs, k, kb, tk, rows, masked):
    scale = HEAD_DIM ** -0.5
    z = [_dot(q, k, NT) * scale for q in qs]
    ls = [_logsig(z_) for z_ in z]
    lm = [l - z_ for l, z_ in zip(ls, z)]
    ok = None
    if masked:
        col = kb * tk + lax.broadcasted_iota(jnp.int32, z[0].shape, 1)
        ok = [col < row for row in rows]
        lm = [jnp.where(m, x, 0.0) for m, x in zip(ok, lm)]
    return ls, lm, ok


def _sb_weights(ls, lm, ok, tail, after):
    suf = [_dot(x.astype(bf16), after) for x in lm]
    a = [jnp.exp(l + s_ + t_) for l, s_, t_ in zip(ls, suf, tail)]
    if ok is not None:
        a = [jnp.where(m, x, 0.0) for m, x in zip(ok, a)]
    return a


def sb_fwd(qkv, nh, name, comm=None):
    S = qkv.shape[0]
    tq, tk, rs, nd = _sb_geometry(S)
    nsub = tq // rs

    def body(q_ref, k_ref, v_ref, o_ref, tails_ref):
        i = pl.program_id(1)
        after, _ = _sb_consts(tk)
        qs = [q_ref[pl.ds(s * rs, rs), :] for s in range(nsub)]
        rows = [i * tq + s * rs + lax.broadcasted_iota(jnp.int32, (rs, tk), 0) for s in range(nsub)]

        def tile(kb, carry, d):
            tail, acc = carry
            act = _active(d, nsub, rs, tk)
            pick = lambda lst: [lst[s] for s in act]
            off = pl.multiple_of(kb * tk, tk)
            k = k_ref[pl.ds(off, tk), :]
            v = v_ref[pl.ds(off, tk), :]
            ls, lm, ok = _sb_scores(pick(qs), k, kb, tk, pick(rows), d is not None)
            a = _sb_weights(ls, lm, ok, pick(tail), after)
            tails_ref[pl.ds(kb, 1), :] += jnp.concatenate([t_.T for t_ in tail], axis=1)
            acc = _put(acc, act, [ac + _dot(a_.astype(bf16), v) for ac, a_ in zip(pick(acc), a)])
            tail = _put(tail, act, [t_ + jnp.sum(x, axis=1, keepdims=True) for t_, x in zip(pick(tail), lm)])
            return tail, acc

        tails_ref[...] = jnp.zeros_like(tails_ref)
        carry = ([jnp.zeros((rs, 1), f32)] * nsub, [jnp.zeros((rs, HEAD_DIM), f32)] * nsub)
        for d in reversed(range(nd)):
            carry = tile(i * nd + d, carry, d)
        carry = lax.fori_loop(0, i * nd, lambda t, c: tile(i * nd - 1 - t, c, None), carry)
        for s in range(nsub):
            o_ref[pl.ds(s * rs, rs), :] = carry[1][s].astype(o_ref.dtype)

    blk = lambda cb0: pl.BlockSpec((tq, HEAD_DIM), lambda h, i: (i, cb0 + h))
    full = lambda cb0: pl.BlockSpec((S, HEAD_DIM), lambda h, i: (0, cb0 + h))
    tails = pl.BlockSpec((S // tk, tq), lambda h, i: (h, i))
    return _call(body, name, (nh, S // tq), [blk(0), full(nh), full(2 * nh)], (blk(0), tails),
                 (SDS((S, nh * HEAD_DIM), bf16), SDS((nh * (S // tk), S), f32)), comm=comm)(qkv, qkv, qkv)


def sb_bwd(qkv, tails, do, nh, name, comm=None):
    S = qkv.shape[0]
    tq, tk, rs, nd = _sb_geometry(S)
    nsub = tq // rs
    scale = HEAD_DIM ** -0.5

    def body(q_ref, k_ref, v_ref, tails_ref, do_ref, dq_ref, dk_ref, dv_ref):
        i = pl.program_id(1)

        @pl.when(i == 0)
        def _():
            dk_ref[...] = jnp.zeros_like(dk_ref)
            dv_ref[...] = jnp.zeros_like(dv_ref)

        after, before = _sb_consts(tk)
        qs = [q_ref[pl.ds(s * rs, rs), :] for s in range(nsub)]
        dos = [do_ref[pl.ds(s * rs, rs), :].astype(bf16) for s in range(nsub)]
        rows = [i * tq + s * rs + lax.broadcasted_iota(jnp.int32, (rs, tk), 0) for s in range(nsub)]

        def sweep2(kb, carry, d):
            pe, dq = carry
            act = _active(d, nsub, rs, tk)
            pick = lambda lst: [lst[s] for s in act]
            qa, da = pick(qs), pick(dos)
            off = pl.multiple_of(kb * tk, tk)
            k = k_ref[pl.ds(off, tk), :]
            v = v_ref[pl.ds(off, tk), :]
            ls, lm, ok = _sb_scores(qa, k, kb, tk, pick(rows), d is not None)
            tail_row = tails_ref[pl.ds(kb, 1), :]
            tail = [tail_row[:, s * rs:(s + 1) * rs].T for s in act]
            a = _sb_weights(ls, lm, ok, tail, after)
            e = [_dot(d_, v, NT) * a_ for d_, a_ in zip(da, a)]
            cum_e = [p_ + _dot(e_.astype(bf16), before) for p_, e_ in zip(pick(pe), e)]
            sig = [jnp.exp(l) for l in ls]
            dz = [e_ * (1.0 - sg) - c_ * sg for e_, sg, c_ in zip(e, sig, cum_e)]
            if ok is not None:
                dz = [jnp.where(m, x, 0.0) for m, x in zip(ok, dz)]
            dzb = [(x * scale).astype(bf16) for x in dz]
            dk_ref[pl.ds(off, tk), :] += sum(_dot(x, q, TN) for x, q in zip(dzb, qa))
            dv_ref[pl.ds(off, tk), :] += sum(_dot(a_.astype(bf16), d_, TN) for a_, d_ in zip(a, da))
            pe = _put(pe, act, [p_ + jnp.sum(e_, axis=1, keepdims=True) for p_, e_ in zip(pick(pe), e)])
            dq = _put(dq, act, [g + _dot(x, k) for g, x in zip(pick(dq), dzb)])
            return pe, dq

        carry = ([jnp.zeros((rs, 1), f32)] * nsub, [jnp.zeros((rs, HEAD_DIM), f32)] * nsub)
        carry = lax.fori_loop(0, i * nd, lambda kb, c: sweep2(kb, c, None), carry)
        for d in range(nd):
            carry = sweep2(i * nd + d, carry, d)
        for s in range(nsub):
            dq_ref[pl.ds(s * rs, rs), :] = carry[1][s]

    blk = lambda cb0: pl.BlockSpec((tq, HEAD_DIM), lambda h, i: (i, cb0 + h))
    full = lambda cb0: pl.BlockSpec((S, HEAD_DIM), lambda h, i: (0, cb0 + h))
    sh = SDS((S, nh * HEAD_DIM), f32)
    tails_spec = pl.BlockSpec((S // tk, tq), lambda h, i: (h, i))
    return _call(body, name, (nh, S // tq), [blk(0), full(nh), full(2 * nh), tails_spec, blk(0)], (blk(0), full(0), full(0)),
                 (sh, sh, sh), comm=comm)(qkv, qkv, qkv, tails, do)


def fox_fwd(fq, fk, fv, c, cT, nh, lane0, name, comm=None):
    S = fq.shape[0]
    tq, tk, rs, nd = _sb_geometry(S, FOX_TK)
    nsub = tq // rs
    scale = HEAD_DIM ** -0.5

    def body(q_ref, k_ref, v_ref, c_ref, ct_ref, o_ref, lse_ref):
        h = pl.program_id(0)
        i = pl.program_id(1)
        subs = range(nsub)
        qs = [q_ref[pl.ds(s * rs, rs), :] for s in subs]
        ci = [_lane_pick(c_ref[pl.ds(s * rs, rs), :], lane0 + h) for s in subs]
        rows = [i * tq + s * rs + lax.broadcasted_iota(jnp.int32, (rs, tk), 0) for s in subs]

        def tile(kb, carry, d):
            m, l, acc = carry
            act = _active(d, nsub, rs, tk)
            pick = lambda lst: [lst[s] for s in act]
            off = pl.multiple_of(kb * tk, tk)
            k = k_ref[pl.ds(off, tk), :]
            v = v_ref[pl.ds(off, tk), :]
            cj = ct_ref[pl.ds(lane0 % 8 + h, 1), pl.ds(off, tk)]
            sc = [_dot(q, k, NT) * scale + (c_ - cj) for q, c_ in zip(pick(qs), pick(ci))]
            if d is not None:
                col = kb * tk + lax.broadcasted_iota(jnp.int32, (rs, tk), 1)
                sc = [jnp.where(col <= row, x, -jnp.inf) for x, row in zip(sc, pick(rows))]
            m2 = [jnp.maximum(m_, jnp.max(x, axis=1, keepdims=True)) for m_, x in zip(pick(m), sc)]
            p = [jnp.exp(x - m_) for x, m_ in zip(sc, m2)]
            al = [jnp.exp(a - b) for a, b in zip(pick(m), m2)]
            l = _put(l, act, [a * l_ + jnp.sum(p_, axis=1, keepdims=True) for a, l_, p_ in zip(al, pick(l), p)])
            acc = _put(acc, act, [a * ac + _dot(p_.astype(bf16), v) for a, ac, p_ in zip(al, pick(acc), p)])
            return _put(m, act, m2), l, acc

        carry = ([jnp.full((rs, 1), -jnp.inf, f32)] * nsub, [jnp.zeros((rs, 1), f32)] * nsub,
                 [jnp.zeros((rs, HEAD_DIM), f32)] * nsub)
        for d in range(nd):
            carry = tile(i * nd + d, carry, d)
        m, l, acc = lax.fori_loop(0, i * nd, lambda kb, cr: tile(kb, cr, None), carry)
        for s in subs:
            o_ref[pl.ds(s * rs, rs), :] = acc[s] / l[s]
            lse_ref[pl.ds(s * rs, rs), :] = jnp.broadcast_to(m[s] + jnp.log(l[s]), (rs, HEAD_DIM))

    blk = pl.BlockSpec((tq, HEAD_DIM), lambda h, i: (i, h))
    full = pl.BlockSpec((S, HEAD_DIM), lambda h, i: (0, h))
    cspec = pl.BlockSpec((tq, LANES), lambda h, i: (i, 0))
    ctspec = pl.BlockSpec((8, S), lambda h, i: (lane0 // 8, 0))
    sh = SDS((S, nh * HEAD_DIM), f32)
    return _call(body, name, (nh, S // tq), [blk, full, full, cspec, ctspec], (blk, blk), (sh, sh), comm=comm)(fq, fk, fv, c, cT)


def fox_bwd(fq, fk, fv, c, cT, o, lse, do, nh, lane0, name, comm=None):
    S = fq.shape[0]
    tq, tk, rs, nd = _sb_geometry(S, FOX_TK)
    nsub = tq // rs
    scale = HEAD_DIM ** -0.5

    def body(q_ref, k_ref, v_ref, c_ref, ct_ref, o_ref, lse_ref, do_ref, dq_ref, dk_ref, dv_ref, dct_ref):
        h = pl.program_id(0)
        i = pl.program_id(1)

        @pl.when(i == 0)
        def _():
            dk_ref[...] = jnp.zeros_like(dk_ref)
            dv_ref[...] = jnp.zeros_like(dv_ref)

        @pl.when((i == 0) & (h == 0))
        def _():
            dct_ref[...] = jnp.zeros_like(dct_ref)

        subs = range(nsub)
        qs = [q_ref[pl.ds(s * rs, rs), :] for s in subs]
        dof = [do_ref[pl.ds(s * rs, rs), :] for s in subs]
        dos = [x.astype(bf16) for x in dof]
        ci = [_lane_pick(c_ref[pl.ds(s * rs, rs), :], lane0 + h) for s in subs]
        lses = [lse_ref[pl.ds(s * rs, rs), :][:, :1] for s in subs]
        dl = [jnp.sum(x * o_ref[pl.ds(s * rs, rs), :], axis=1, keepdims=True) for s, x in zip(subs, dof)]
        rows = [i * tq + s * rs + lax.broadcasted_iota(jnp.int32, (rs, tk), 0) for s in subs]

        def tile(kb, carry, d):
            dq, rsum = carry
            act = _active(d, nsub, rs, tk)
            pick = lambda lst: [lst[s] for s in act]
            qa, da = pick(qs), pick(dos)
            off = pl.multiple_of(kb * tk, tk)
            k = k_ref[pl.ds(off, tk), :]
            v = v_ref[pl.ds(off, tk), :]
            cj = ct_ref[pl.ds(lane0 % 8 + h, 1), pl.ds(off, tk)]
            p = [jnp.exp(_dot(q, k, NT) * scale + (c_ - cj) - ls) for q, c_, ls in zip(qa, pick(ci), pick(lses))]
            if d is not None:
                col = kb * tk + lax.broadcasted_iota(jnp.int32, (rs, tk), 1)
                p = [jnp.where(col <= row, x, 0.0) for x, row in zip(p, pick(rows))]
            ds = [p_ * (_dot(d_, v, NT) - dl_) for p_, d_, dl_ in zip(p, da, pick(dl))]
            dsb = [(x * scale).astype(bf16) for x in ds]
            dk_ref[pl.ds(off, tk), :] += sum(_dot(x, q, TN) for x, q in zip(dsb, qa))
            dv_ref[pl.ds(off, tk), :] += sum(_dot(p_.astype(bf16), d_, TN) for p_, d_ in zip(p, da))
            dct_ref[pl.ds(lane0 + h, 1), pl.ds(off, tk)] -= sum(jnp.sum(x, axis=0, keepdims=True) for x in ds)
            return (_put(dq, act, [g + _dot(x, k) for g, x in zip(pick(dq), dsb)]),
                    _put(rsum, act, [r_ + jnp.sum(x, axis=1, keepdims=True) for r_, x in zip(pick(rsum), ds)]))

        carry = ([jnp.zeros((rs, HEAD_DIM), f32)] * nsub, [jnp.zeros((rs, 1), f32)] * nsub)
        carry = lax.fori_loop(0, i * nd, lambda kb, cr: tile(kb, cr, None), carry)
        for d in range(nd):
            carry = tile(i * nd + d, carry, d)
        dq, rsum = carry
        for s in subs:
            dq_ref[pl.ds(s * rs, rs), :] = dq[s]
        dct_ref[pl.ds(lane0 + h, 1), pl.ds(pl.multiple_of(i * tq, tq), tq)] += jnp.concatenate([r_.T for r_ in rsum], axis=1)

    blk = pl.BlockSpec((tq, HEAD_DIM), lambda h, i: (i, h))
    full = pl.BlockSpec((S, HEAD_DIM), lambda h, i: (0, h))
    cspec = pl.BlockSpec((tq, LANES), lambda h, i: (i, 0))
    ctspec = pl.BlockSpec((8, S), lambda h, i: (lane0 // 8, 0))
    dctspec = pl.BlockSpec((LANES, S), lambda h, i: (0, 0))
    sh = SDS((S, nh * HEAD_DIM), f32)
    return _call(body, name, (nh, S // tq), [blk, full, full, cspec, ctspec, blk, blk, blk], (blk, full, full, dctspec),
                 (sh, sh, sh, SDS((LANES, S), f32)), comm=comm)(fq, fk, fv, c, cT, o, lse, do)


def gates_fwd(proj, small_cb, fbias_row, name, tm=256):
    S = proj.shape[0]
    tm = _tile(S, tm)

    def body(sm_ref, fb_ref, c_ref, ct_ref, carry_ref):
        @pl.when(pl.program_id(0) == 0)
        def _():
            carry_ref[...] = jnp.zeros_like(carry_ref)

        lf = _logsig(sm_ref[...] + fb_ref[...])
        r = lax.broadcasted_iota(jnp.int32, (tm, tm), 0)
        cc = lax.broadcasted_iota(jnp.int32, (tm, tm), 1)
        c = _dot((r >= cc).astype(f32), lf, NN, HI) + carry_ref[...]
        carry_ref[...] += jnp.sum(lf, axis=0, keepdims=True)
        c_ref[...] = c
        ct_ref[...] = c.T

    return _call(body, name, (S // tm,), [_rowspec(tm, LANES, small_cb), _bspec(LANES)],
                 (_rowspec(tm, LANES), pl.BlockSpec((LANES, tm), lambda i: (0, i))),
                 (SDS((S, LANES), f32), SDS((LANES, S), f32)), scratch=[pltpu.VMEM((1, LANES), f32)])(proj, fbias_row)


def gates_bwd(proj, small_cb, fbias_row, dcT, dsm_dn, lane0, nh, name, tm=256):
    S = proj.shape[0]
    tm = _tile(S, tm)
    nt = S // tm

    def body(sm_ref, fb_ref, dct_ref, dsm_ref, o_ref, dfb_ref, carry_ref):
        @pl.when(pl.program_id(0) == 0)
        def _():
            carry_ref[...] = jnp.zeros_like(carry_ref)
            dfb_ref[...] = jnp.zeros_like(dfb_ref)

        dc = dct_ref[...].T
        r = lax.broadcasted_iota(jnp.int32, (tm, tm), 0)
        cc = lax.broadcasted_iota(jnp.int32, (tm, tm), 1)
        dlf = _dot((r <= cc).astype(f32), dc, NN, HI) + carry_ref[...]
        carry_ref[...] += jnp.sum(dc, axis=0, keepdims=True)
        lane = lax.broadcasted_iota(jnp.int32, (1, LANES), 1)
        dpre = jnp.where((lane >= lane0) & (lane < lane0 + nh), dlf * jax.nn.sigmoid(-(sm_ref[...] + fb_ref[...])), 0.0)
        o_ref[...] = dsm_ref[...] + dpre
        dfb_ref[...] += jnp.sum(dpre, axis=0, keepdims=True)

    rev = lambda i: nt - 1 - i
    return _call(body, name, (nt,),
                 [pl.BlockSpec((tm, LANES), lambda i: (rev(i), small_cb)), _bspec(LANES),
                  pl.BlockSpec((LANES, tm), lambda i: (0, rev(i))), pl.BlockSpec((tm, LANES), lambda i: (rev(i), 0))],
                 (pl.BlockSpec((tm, LANES), lambda i: (rev(i), 0)), _bspec(LANES)),
                 (SDS((S, LANES), f32), SDS((1, LANES), f32)), scratch=[pltpu.VMEM((1, LANES), f32)])(proj, fbias_row, dcT, dsm_dn)


PAD_ROWS = 8


def conv_fwd(proj, w, width, name):
    S = proj.shape[0]
    cw = 256 if width % 256 == 0 else 128

    def body(x_ref, w_ref, y_ref, pad_ref):
        pad_ref[pl.ds(0, PAD_ROWS), :] = jnp.zeros((PAD_ROWS, cw), f32)
        pad_ref[pl.ds(PAD_ROWS, S), :] = x_ref[...]
        y = jnp.zeros((S, cw), f32)
        for i in range(CONV_WIDTH):
            y = y + pad_ref[pl.ds(PAD_ROWS - (CONV_WIDTH - 1) + i, S), :] * w_ref[pl.ds(i, 1), :]
        y_ref[...] = y * jax.nn.sigmoid(y)

    spec = pl.BlockSpec((S, cw), lambda j: (0, j))
    return _call(body, name, (width // cw,), [spec, pl.BlockSpec((CONV_WIDTH, cw), lambda j: (0, j))], spec,
                 SDS((S, width), f32), scratch=[pltpu.VMEM((S + PAD_ROWS, cw), f32)])(proj, w)


def conv_bwd(proj, w, dy, name):
    S, width = dy.shape
    cw = 256 if width % 256 == 0 else 128

    def body(x_ref, w_ref, dy_ref, dx_ref, dw_ref, xpad_ref, dpad_ref):
        xpad_ref[pl.ds(0, PAD_ROWS), :] = jnp.zeros((PAD_ROWS, cw), f32)
        xpad_ref[pl.ds(PAD_ROWS, S), :] = x_ref[...]
        y = jnp.zeros((S, cw), f32)
        for i in range(CONV_WIDTH):
            y = y + xpad_ref[pl.ds(PAD_ROWS - (CONV_WIDTH - 1) + i, S), :] * w_ref[pl.ds(i, 1), :]
        sg = jax.nn.sigmoid(y)
        dpre = dy_ref[...] * (sg * (1.0 + y * (1.0 - sg)))
        dpad_ref[pl.ds(S, PAD_ROWS), :] = jnp.zeros((PAD_ROWS, cw), f32)
        dpad_ref[pl.ds(0, S), :] = dpre
        dx = jnp.zeros((S, cw), f32)
        for i in range(CONV_WIDTH):
            dx = dx + dpad_ref[pl.ds(CONV_WIDTH - 1 - i, S), :] * w_ref[pl.ds(i, 1), :]
            dw_ref[pl.ds(i, 1), :] = jnp.sum(dpre * xpad_ref[pl.ds(PAD_ROWS - (CONV_WIDTH - 1) + i, S), :], axis=0, keepdims=True)
        dx_ref[...] = dx

    spec = pl.BlockSpec((S, cw), lambda j: (0, j))
    wspec = pl.BlockSpec((CONV_WIDTH, cw), lambda j: (0, j))
    return _call(body, name, (width // cw,), [spec, wspec, spec], (spec, wspec),
                 (SDS((S, width), f32), SDS((CONV_WIDTH, width), f32)),
                 scratch=[pltpu.VMEM((S + PAD_ROWS, cw), f32), pltpu.VMEM((S + PAD_ROWS, cw), f32)])(proj, w, dy)


def _pdot_raw(a, b, dims, passes):
    if passes == 1:
        return _dot(a.astype(bf16), b.astype(bf16), dims)
    ah, al = _split_bf16(a)
    bh, bl = _split_bf16(b)
    return _dot(ah, bh, dims) + (_dot(ah, bl, dims) + _dot(al, bh, dims))


def _make_pdot(passes):
    @functools.partial(jax.custom_vjp, nondiff_argnums=(2,))
    def pdot(a, b, mode):
        return _pdot_raw(a, b, {"nn": NN, "nt": NT, "tn": TN}[mode], passes)

    def fwd(a, b, mode):
        return pdot(a, b, mode), (a, b)

    def bwd(mode, res, g):
        a, b = res
        if mode == "nn":
            return _pdot_raw(g, b, NT, passes), _pdot_raw(a, g, TN, passes)
        if mode == "nt":
            return _pdot_raw(g, b, NN, passes), _pdot_raw(g, a, TN, passes)
        return _pdot_raw(b, g, NT, passes), _pdot_raw(a, g, NN, passes)

    pdot.defvjp(fwd, bwd)
    return pdot


_dot1 = _make_pdot(1)
_dot3 = _make_pdot(3)


def _each(f, *lists):
    return [f(*xs) for xs in zip(*lists)]


def _dn_chunk(ndn, cq, ck, cv, small, alog, dtb, s0):
    C = cq[0].shape[0]
    hs = list(range(ndn))
    b = [_lane_pick(small, h) for h in hs]
    d = [_lane_pick(small, ndn + h) for h in hs]
    al = [_lane_pick(alog, h) for h in hs]
    dt = [_lane_pick(dtb, h) for h in hs]
    q = _each(lambda x: x * lax.rsqrt(jnp.sum(x * x, axis=1, keepdims=True) + EPS) * (HEAD_DIM ** -0.5), cq)
    k = _each(lambda x: x * lax.rsqrt(jnp.sum(x * x, axis=1, keepdims=True) + EPS), ck)
    beta = _each(jax.nn.sigmoid, b)
    g = _each(lambda al_, d_, dt_: -jnp.exp(al_) * _softplus(d_ + dt_), al, d, dt)
    r = lax.broadcasted_iota(jnp.int32, (C, C), 0)
    c = lax.broadcasted_iota(jnp.int32, (C, C), 1)
    incl, strict = r >= c, r > c
    inclf = incl.astype(f32)
    gc = _each(lambda g_: _dot3(inclf, g_, "nn"), g)
    decay = _each(lambda gc_: jnp.where(incl, jnp.exp(jnp.where(incl, gc_ - gc_.T, 0.0)), 0.0), gc)
    kb = _each(lambda k_, b_: k_ * b_, k, beta)
    a = _each(lambda kb_, k_, dec: jnp.where(strict, _dot3(kb_, k_, "nt") * dec, 0.0), kb, k, decay)
    eye = (r == c).astype(f32)
    t = _each(lambda a_: eye - a_, a)
    p = a
    for _ in range(int(math.log2(C)) - 1):
        p = _each(lambda p_: _dot3(p_, p_, "nn"), p)
        t = _each(lambda t_, p_: t_ + _dot3(t_, p_, "nn"), t, p)
    eg = _each(jnp.exp, gc)
    u = _each(lambda t_, v_, b_: _dot3(t_, v_ * b_, "nn"), t, cv, beta)
    w = _each(lambda t_, kb_, eg_: _dot3(t_, kb_ * eg_, "nn"), t, kb, eg)
    attn = _each(lambda q_, k_, dec: _dot1(q_, k_, "nt") * dec, q, k, decay)
    g_last = _each(lambda g_: jnp.sum(g_, axis=0, keepdims=True), g)
    v_new = _each(lambda u_, w_, s_: u_ - _dot1(w_, s_, "nn"), u, w, s0)
    o = _each(lambda q_, eg_, s_, at, vn: _dot1(q_ * eg_, s_, "nn") + _dot1(at, vn, "nn"), q, eg, s0, attn, v_new)
    s1 = _each(lambda s_, gl, k_, gc_, vn: s_ * jnp.exp(gl) + _dot1(k_ * jnp.exp(gl - gc_), vn, "tn"), s0, g_last, k, gc, v_new)
    return o, s1


def dn_fwd(cqkv, proj, small_cb, alog_row, dt_row, ndn, name, comm=None):
    S = cqkv.shape[0]
    C = DN_CHUNK
    nc = S // C
    half = ndn * HEAD_DIM

    def body(q_ref, k_ref, v_ref, sm_ref, al_ref, dt_ref, o_ref, ss_ref, s_ref):
        @pl.when(pl.program_id(0) == 0)
        def _():
            s_ref[...] = jnp.zeros_like(s_ref)

        sls = [slice(h * HEAD_DIM, (h + 1) * HEAD_DIM) for h in range(ndn)]
        s0 = [s_ref[h] for h in range(ndn)]
        for h in range(ndn):
            ss_ref[h, 0] = s0[h]
        o, s1 = _dn_chunk(ndn, [q_ref[:, sl] for sl in sls], [k_ref[:, sl] for sl in sls], [v_ref[:, sl] for sl in sls],
                          sm_ref[...], al_ref[...], dt_ref[...], s0)
        for h in range(ndn):
            o_ref[:, sls[h]] = o[h]
            s_ref[h] = s1[h]

    blk = lambda cb: pl.BlockSpec((C, half), lambda n: (n, cb))
    row = pl.BlockSpec((1, LANES), lambda n: (0, 0))
    return _call(body, name, (nc,),
                 [blk(0), blk(1), blk(2), pl.BlockSpec((C, LANES), lambda n: (n, small_cb)), row, row],
                 (blk(0), pl.BlockSpec((ndn, 1, HEAD_DIM, HEAD_DIM), lambda n: (0, n, 0, 0))),
                 (SDS((S, half), f32), SDS((ndn, nc, HEAD_DIM, HEAD_DIM), f32)),
                 scratch=[pltpu.VMEM((ndn, HEAD_DIM, HEAD_DIM), f32)], comm=comm)(cqkv, cqkv, cqkv, proj, alog_row, dt_row)


def dn_bwd(cqkv, proj, small_cb, alog_row, dt_row, ssave, do, ndn, name, comm=None):
    S = cqkv.shape[0]
    C = DN_CHUNK
    nc = S // C
    half = ndn * HEAD_DIM

    def body(q_ref, k_ref, v_ref, sm_ref, al_ref, dt_ref, ss_ref, do_ref, dqkv_ref, dsm_ref, dal_ref, ddt_ref, ds_ref):
        @pl.when(pl.program_id(0) == 0)
        def _():
            ds_ref[...] = jnp.zeros_like(ds_ref)
            dal_ref[...] = jnp.zeros_like(dal_ref)
            ddt_ref[...] = jnp.zeros_like(ddt_ref)

        sls = [slice(h * HEAD_DIM, (h + 1) * HEAD_DIM) for h in range(ndn)]
        _, vjp = jax.vjp(functools.partial(_dn_chunk, ndn), [q_ref[:, sl] for sl in sls], [k_ref[:, sl] for sl in sls],
                         [v_ref[:, sl] for sl in sls], sm_ref[...], al_ref[...], dt_ref[...], [ss_ref[h, 0] for h in range(ndn)])
        dq, dk, dv, dsm, dal, ddt, ds0 = vjp(([do_ref[:, sl] for sl in sls], [ds_ref[h] for h in range(ndn)]))
        for h in range(ndn):
            dqkv_ref[:, sls[h]] = dq[h]
            dqkv_ref[:, half + h * HEAD_DIM:half + (h + 1) * HEAD_DIM] = dk[h]
            dqkv_ref[:, 2 * half + h * HEAD_DIM:2 * half + (h + 1) * HEAD_DIM] = dv[h]
            ds_ref[h] = ds0[h]
        dsm_ref[...] = dsm
        dal_ref[...] += dal
        ddt_ref[...] += ddt

    rev = lambda n: nc - 1 - n
    blk = lambda cb: pl.BlockSpec((C, half), lambda n: (rev(n), cb))
    row = pl.BlockSpec((1, LANES), lambda n: (0, 0))
    return _call(body, name, (nc,),
                 [blk(0), blk(1), blk(2), pl.BlockSpec((C, LANES), lambda n: (rev(n), small_cb)), row, row,
                  pl.BlockSpec((ndn, 1, HEAD_DIM, HEAD_DIM), lambda n: (0, rev(n), 0, 0)), blk(0)],
                 (pl.BlockSpec((C, 3 * half), lambda n: (rev(n), 0)), pl.BlockSpec((C, LANES), lambda n: (rev(n), 0)), row, row),
                 (SDS((S, 3 * half), f32), SDS((S, LANES), f32), SDS((1, LANES), f32), SDS((1, LANES), f32)),
                 scratch=[pltpu.VMEM((ndn, HEAD_DIM, HEAD_DIM), f32)], comm=comm)(cqkv, cqkv, cqkv, proj, alog_row, dt_row, ssave, do)


def even_pre(proj, gq, gk, dfx, cb0, name):
    S = proj.shape[0]
    tm = _tile(S, 512)
    nh = dfx // HEAD_DIM

    def body(q_ref, k_ref, v_ref, gq_ref, gk_ref, fq_ref, fk_ref, fv_ref):
        for h in range(nh):
            sl = slice(h * HEAD_DIM, (h + 1) * HEAD_DIM)
            fq_ref[:, sl] = _rms(q_ref[:, sl], gq_ref[...]).astype(bf16)
            fk_ref[:, sl] = _rms(k_ref[:, sl], gk_ref[...]).astype(bf16)
        fv_ref[...] = v_ref[...].astype(bf16)

    sh = SDS((S, dfx), bf16)
    o = _rowspec(tm, dfx)
    return _call(body, name, (S // tm,),
                 [_rowspec(tm, dfx, cb0), _rowspec(tm, dfx, cb0 + 1), _rowspec(tm, dfx, cb0 + 2), _bspec(HEAD_DIM), _bspec(HEAD_DIM)],
                 (o, o, o), (sh, sh, sh))(proj, proj, proj, gq, gk)


def even_pre_bwd(proj, gq, gk, dfq, dfk, dfx, cb0, name):
    S = proj.shape[0]
    tm = _tile(S, 512)
    nh = dfx // HEAD_DIM

    def body(q_ref, k_ref, gq_ref, gk_ref, dfq_ref, dfk_ref, dq_ref, dk_ref, dgq_ref, dgk_ref):
        @pl.when(pl.program_id(0) == 0)
        def _():
            dgq_ref[...] = jnp.zeros_like(dgq_ref)
            dgk_ref[...] = jnp.zeros_like(dgk_ref)

        for h in range(nh):
            sl = slice(h * HEAD_DIM, (h + 1) * HEAD_DIM)
            _, vq = jax.vjp(_rms, q_ref[:, sl], gq_ref[...])
            dq, dgq = vq(dfq_ref[:, sl])
            _, vk = jax.vjp(_rms, k_ref[:, sl], gk_ref[...])
            dk, dgk = vk(dfk_ref[:, sl])
            dq_ref[:, sl] = dq
            dk_ref[:, sl] = dk
            dgq_ref[...] += dgq
            dgk_ref[...] += dgk

    sh = SDS((S, dfx), f32)
    o = _rowspec(tm, dfx)
    g = SDS((1, HEAD_DIM), f32)
    return _call(body, name, (S // tm,),
                 [_rowspec(tm, dfx, cb0), _rowspec(tm, dfx, cb0 + 1), _bspec(HEAD_DIM), _bspec(HEAD_DIM), o, o],
                 (o, o, _bspec(HEAD_DIM), _bspec(HEAD_DIM)), (sh, sh, g, g))(proj, proj, gq, gk, dfq, dfk)


def _dn_out(o, gate, gn):
    return _rms(o, gn) * (gate * jax.nn.sigmoid(gate))


def _fox_out(o, gate):
    return o * jax.nn.sigmoid(gate)


def even_post(o_dn, o_fox, proj, gn, half, cb_dn_gate, cb_fox_gate, name):
    S = proj.shape[0]
    tm = _tile(S, 512)
    nh = half // HEAD_DIM

    def body(od_ref, of_ref, gd_ref, gf_ref, gn_ref, o_ref):
        for h in range(nh):
            sl = slice(h * HEAD_DIM, (h + 1) * HEAD_DIM)
            o_ref[:, sl] = _dn_out(od_ref[:, sl], gd_ref[:, sl], gn_ref[...]).astype(bf16)
        o_ref[:, half:] = _fox_out(of_ref[...], gf_ref[...]).astype(bf16)

    return _call(body, name, (S // tm,),
                 [_rowspec(tm, half), _rowspec(tm, half), _rowspec(tm, half, cb_dn_gate), _rowspec(tm, half, cb_fox_gate), _bspec(HEAD_DIM)],
                 _rowspec(tm, 2 * half), SDS((S, 2 * half), bf16))(o_dn, o_fox, proj, proj, gn)


def even_post_bwd(o_dn, o_fox, proj, gn, dom, half, cb_dn_gate, cb_fox_gate, name):
    S = proj.shape[0]
    tm = _tile(S, 512)
    nh = half // HEAD_DIM

    def body(od_ref, of_ref, gd_ref, gf_ref, gn_ref, dod_ref, dof_ref, dd_ref, df_ref, dgd_ref, dgf_ref, dgn_ref):
        @pl.when(pl.program_id(0) == 0)
        def _():
            dgn_ref[...] = jnp.zeros_like(dgn_ref)

        for h in range(nh):
            sl = slice(h * HEAD_DIM, (h + 1) * HEAD_DIM)
            _, vjp = jax.vjp(_dn_out, od_ref[:, sl], gd_ref[:, sl], gn_ref[...])
            d_o, d_gate, d_gn = vjp(dod_ref[:, sl])
            dd_ref[:, sl] = d_o
            dgd_ref[:, sl] = d_gate
            dgn_ref[...] += d_gn
        _, vjp = jax.vjp(_fox_out, of_ref[...], gf_ref[...])
        d_o, d_gate = vjp(dof_ref[...])
        df_ref[...] = d_o
        dgf_ref[...] = d_gate

    sh = SDS((S, half), f32)
    o = _rowspec(tm, half)
    return _call(body, name, (S // tm,),
                 [o, o, _rowspec(tm, half, cb_dn_gate), _rowspec(tm, half, cb_fox_gate), _bspec(HEAD_DIM),
                  _rowspec(tm, half, 0), _rowspec(tm, half, 1)],
                 (o, o, o, o, _bspec(HEAD_DIM)), (sh, sh, sh, sh, SDS((1, HEAD_DIM), f32)))(o_dn, o_fox, proj, proj, gn, dom, dom)


def _pad_row(v, lane0=0):
    return jnp.pad(v, (lane0, LANES - lane0 - v.shape[0]))[None]


def _carried(res, comm):
    return res if comm is not None else (res, [])


def even_mixer_fwd(x, gmix, w_in, w_out, conv_w, a_log, dt_bias, gn, gq, gk, f_bias, tag, comm_fox=None, comm_dn=None,
                   comm_in=None):
    S, D = x.shape
    half = D // 2
    ndn = half // HEAD_DIM
    small_cb = 4 * D // LANES
    alog_row, dt_row, fb_row = _pad_row(a_log), _pad_row(dt_bias), _pad_row(f_bias, 2 * ndn)
    h = rms_fwd(x, gmix, f"{tag}_rms")
    proj, got_in = _carried(mm(h, w_in, "nn", f32, f"{tag}_in", tm=1024, tn=1408, comm=comm_in), comm_in)
    cqkv = conv_fwd(proj, conv_w, 3 * half, f"{tag}_conv")
    (o_dn, ssave), got_dn = _carried(dn_fwd(cqkv, proj, small_cb, alog_row, dt_row, ndn, f"{tag}_dn", comm=comm_dn), comm_dn)
    c, cT = gates_fwd(proj, small_cb, fb_row, f"{tag}_gates")
    fq, fk, fv = even_pre(proj, gq, gk, half, 4, f"{tag}_pre")
    (o_fox, lse), got_fox = _carried(fox_fwd(fq, fk, fv, c, cT, ndn, 2 * ndn, f"{tag}_fox", comm=comm_fox), comm_fox)
    om = even_post(o_dn, o_fox, proj, gn, half, 3, 7, f"{tag}_post")
    xo = mm(om, w_out, "nn", f32, f"{tag}_out", tn=1024, res=x)
    return xo, (h, proj, cqkv, o_dn, ssave, c, cT, fq, fk, fv, o_fox, lse, om, alog_row, dt_row, fb_row), got_fox, got_dn, got_in


def even_mixer_bwd(x, gmix, w_in, w_out, conv_w, gn, gq, gk, saved, dy, tag, comm_fox=None, comm_dn=None, comm_bdh=None):
    S, D = x.shape
    half = D // 2
    ndn = half // HEAD_DIM
    small_cb = 4 * D // LANES
    h, proj, cqkv, o_dn, ssave, c, cT, fq, fk, fv, o_fox, lse, om, alog_row, dt_row, fb_row = saved
    dy32, dy16 = dy
    dom = mm(dy16, w_out, "nt", f32, f"{tag}_bdom", tn=1024)
    dw_out = mm(om, dy16, "tn", bf16, f"{tag}_bwout", tn=1024)
    d_odn, d_ofox, d_dgate, d_fgate, d_gn = even_post_bwd(o_dn, o_fox, proj, gn, dom, half, 3, 7, f"{tag}_bpost")
    (dfq, dfk, dfv, dcT), got_fox = _carried(
        fox_bwd(fq, fk, fv, c, cT, o_fox, lse, d_ofox, ndn, 2 * ndn, f"{tag}_bfox", comm=comm_fox), comm_fox)
    dq_pre, dk_pre, d_gq, d_gk = even_pre_bwd(proj, gq, gk, dfq, dfk, half, 4, f"{tag}_bpre")
    (dcqkv, dsm_dn, d_alog, d_dt), got_dn = _carried(
        dn_bwd(cqkv, proj, small_cb, alog_row, dt_row, ssave, d_odn, ndn, f"{tag}_bdn", comm=comm_dn), comm_dn)
    d_conv_in, d_conv_w = conv_bwd(proj, conv_w, dcqkv, f"{tag}_bconv")
    d_small, d_fb = gates_bwd(proj, small_cb, fb_row, dcT, dsm_dn, 2 * ndn, ndn, f"{tag}_bgates")
    dproj = jnp.concatenate([d_conv_in, d_dgate, dq_pre, dk_pre, dfv, d_fgate, d_small], axis=1).astype(bf16)
    dw_in = mm(h, dproj, "tn", bf16, f"{tag}_bwin", tn=1408)
    res = mm_bdh_rms(dproj, w_in, x, gmix, dy32, f"{tag}_bdh", comm=comm_bdh)
    dx, d_gmix = res[:2]
    got_bdh = res[2] if comm_bdh is not None else []
    small = dict(norm_mix=d_gmix[0], dn_conv_w=d_conv_w, dn_a_log=d_alog[0, :ndn], dn_dt_bias=d_dt[0, :ndn],
                 dn_norm_g=d_gn[0], fox_q_norm_g=d_gq[0], fox_k_norm_g=d_gk[0], fox_f_bias=d_fb[0, 2 * ndn:3 * ndn])
    return dx, dw_in, dw_out, small, got_fox, got_dn, got_bdh


def odd_mixer_fwd(x, gmix, w_in, w_out, tag, comm=None):
    S, D = x.shape
    nh = D // HEAD_DIM
    h = rms_fwd(x, gmix, f"{tag}_rms")
    qkv = mm(h, w_in, "nn", bf16, f"{tag}_in", tm=1024, tn=768)
    (o, tails), got = _carried(sb_fwd(qkv, nh, f"{tag}_sb", comm=comm), comm)
    xo = mm(o, w_out, "nn", f32, f"{tag}_out", tn=1024, res=x)
    return xo, (h, qkv, o, tails), got


def odd_mixer_bwd(x, gmix, w_in, w_out, saved, dy, tag, comm=None):
    S, D = x.shape
    nh = D // HEAD_DIM
    h, qkv, o, tails = saved
    dy32, dy16 = dy
    do = mm(dy16, w_out, "nt", f32, f"{tag}_bdo", tn=1024)
    dw_out = mm(o, dy16, "tn", bf16, f"{tag}_bwout", tn=1024)
    (dq, dk, dv), got = _carried(sb_bwd(qkv, tails, do, nh, f"{tag}_bsb", comm=comm), comm)
    dqkv = jnp.concatenate([dq, dk, dv], axis=1).astype(bf16)
    dw_in = mm(h, dqkv, "tn", bf16, f"{tag}_bwin", tn=1536)
    dx, d_gmix = mm_bdh_rms(dqkv, w_in, x, gmix, dy32, f"{tag}_bdh")
    return dx, dw_in, dw_out, dict(norm_mix=d_gmix[0]), got


def all_gather(shards, axes, name, kind="ag"):
    return _call(None, name, (), [], [], [], comm=Comm(kind, shards, axes))()[1]


def sum_parts(parts, name):
    _, R, C = parts.shape
    tm = _tile(R, 512)

    def body(p_ref, o_ref):
        acc = p_ref[0].astype(f32)
        for k in range(1, NDEV):
            acc = acc + p_ref[k].astype(f32)
        o_ref[...] = acc

    return _call(body, name, (R // tm,), [pl.BlockSpec((NDEV, tm, C), lambda i: (0, i, 0))], _rowspec(tm, C), SDS((R, C), f32))(parts)


def adamw(w, g, m, v, name):
    L, R, C = w.shape
    tm = _tile(R, max(8, min(512, (ADAMW_TILE_ELEMS // C) // 8 * 8)))
    c1 = 1.0 - ADAM_B1 ** ADAM_STEP
    c2 = 1.0 - ADAM_B2 ** ADAM_STEP

    def body(w_ref, g_ref, m_ref, v_ref, d_ref, nm_ref, nv_ref):
        g_ = g_ref[...]
        nm = ADAM_B1 * m_ref[...] + (1.0 - ADAM_B1) * g_
        nv = ADAM_B2 * v_ref[...] + (1.0 - ADAM_B2) * (g_ * g_)
        d_ref[...] = -ADAM_LR * ((nm / c1) / (jnp.sqrt(nv / c2) + ADAM_EPS) + ADAM_WD * w_ref[...])
        nm_ref[...] = nm
        nv_ref[...] = nv

    spec = pl.BlockSpec((None, tm, C), lambda l, i: (l, i, 0))
    sh = SDS((L, R, C), f32)
    return _call(body, name, (L, R // tm), [spec] * 4, (spec, spec, spec), (sh, sh, sh))(w, g, m, v)


def _pad_to(n, mult):
    return -(-n // mult) * mult


def _even_perm(D):
    half = D // 2
    ndn = half // HEAD_DIM
    o_gate = 3 * half
    o_b = o_gate + half
    o_a = o_b + ndn
    o_fq = o_a + ndn
    o_fpre = o_fq + 4 * half
    return half, ndn, o_b, o_a, o_fq, o_fpre


def _even_to_mine(w):
    D = (w.shape[-1] // 4) // LANES * LANES
    half, ndn, o_b, o_a, o_fq, o_fpre = _even_perm(D)
    small = jnp.concatenate([w[..., o_b:o_b + ndn], w[..., o_a:o_a + ndn], w[..., o_fpre:o_fpre + ndn]], axis=-1)
    small = jnp.pad(small, [(0, 0)] * (w.ndim - 1) + [(0, LANES - 3 * ndn)])
    return jnp.concatenate([w[..., :o_b], w[..., o_fq:o_fpre], small], axis=-1)


def _even_from_mine(w, D):
    half, ndn, o_b, o_a, o_fq, o_fpre = _even_perm(D)
    sm = w[..., 4 * D:]
    return jnp.concatenate([w[..., :o_b], sm[..., :ndn], sm[..., ndn:2 * ndn], w[..., o_b:4 * D], sm[..., 2 * ndn:3 * ndn]], axis=-1)


def _pack_small(parts):
    flat = jnp.concatenate([p.reshape(-1).astype(f32) for p in parts])
    n = flat.shape[0]
    return jnp.pad(flat, (0, _pad_to(n, 8 * LANES) - n)).reshape(-1, LANES)


def _unpack_small(packed, like):
    flat = packed.reshape(-1)
    out, off = [], 0
    for p in like:
        out.append(flat[off:off + p.size].reshape(p.shape))
        off += p.size
    return out


SMALL_NAMES = ("norm_ffn1", "norm_mix", "dn_a_log", "dn_dt_bias", "dn_norm_g", "fox_q_norm_g", "fox_k_norm_g", "fox_f_bias", "norm_ffn2")
BIG_NAMES = ("ffn1_w_gu", "ffn1_w_down", "w_in_even", "dn_conv_w", "w_out_even", "w_in_odd", "w_out_odd", "ffn2_w_gu", "ffn2_w_down")
WEIGHT_NAMES = ("norm_ffn1", "ffn1_w_gu", "ffn1_w_down", "norm_mix", "w_in_even", "dn_conv_w", "dn_a_log", "dn_dt_bias", "dn_norm_g",
                "fox_q_norm_g", "fox_k_norm_g", "fox_f_bias", "w_out_even", "w_in_odd", "w_out_odd", "norm_ffn2", "ffn2_w_gu", "ffn2_w_down")


def kernel(x, norm_ffn1, ffn1_w_gu, ffn1_w_down, norm_mix, w_in_even, dn_conv_w, dn_a_log, dn_dt_bias, dn_norm_g, fox_q_norm_g, fox_k_norm_g, fox_f_bias, w_out_even, w_in_odd, w_out_odd, norm_ffn2, ffn2_w_gu, ffn2_w_down, loss_target, m_norm_ffn1, m_ffn1_w_gu, m_ffn1_w_down, m_norm_mix, m_w_in_even, m_dn_conv_w, m_dn_a_log, m_dn_dt_bias, m_dn_norm_g, m_fox_q_norm_g, m_fox_k_norm_g, m_fox_f_bias, m_w_out_even, m_w_in_odd, m_w_out_odd, m_norm_ffn2, m_ffn2_w_gu, m_ffn2_w_down, v_norm_ffn1, v_ffn1_w_gu, v_ffn1_w_down, v_norm_mix, v_w_in_even, v_dn_conv_w, v_dn_a_log, v_dn_dt_bias, v_dn_norm_g, v_fox_q_norm_g, v_fox_k_norm_g, v_fox_f_bias, v_w_out_even, v_w_in_odd, v_w_out_odd, v_norm_ffn2, v_ffn2_w_gu, v_ffn2_w_down):
    args = dict(locals())
    W = {n: args[n] for n in WEIGHT_NAMES}
    M = {n: args["m_" + n] for n in WEIGHT_NAMES}
    V = {n: args["v_" + n] for n in WEIGHT_NAMES}
    xs = x[0]
    tgt = loss_target[0]
    S, D = xs.shape
    L = norm_ffn1.shape[0]
    n_even = w_in_even.shape[0]
    hs = ffn1_w_down.shape[1]
    hp = _pad_to(hs, LANES)
    me = 4 * lax.axis_index("x") + 2 * lax.axis_index("y") + lax.axis_index("c")

    def prep_gu(w):
        w = w.reshape(L, D, 2, hs)
        return jnp.pad(w, ((0, 0), (0, 0), (0, 0), (0, hp - hs))).reshape(L, D, 2 * hp).astype(bf16)

    def prep_down(w):
        return jnp.pad(w, ((0, 0), (0, hp - hs), (0, 0))).astype(bf16)

    sh_gu1, sh_d1, sh_gu2, sh_d2 = prep_gu(ffn1_w_gu), prep_down(ffn1_w_down), prep_gu(ffn2_w_gu), prep_down(ffn2_w_down)
    sh_ie, sh_oe = _even_to_mine(w_in_even).astype(bf16), w_out_even.astype(bf16)
    sh_io, sh_oo = w_in_odd.astype(bf16), w_out_odd.astype(bf16)

    def layer_shards(l):
        j = l // 2
        mix = [sh_ie[j], sh_oe[j]] if l % 2 == 0 else [sh_io[j], sh_oo[j]]
        return [sh_gu1[l], sh_d1[l], *mix, sh_gu2[l], sh_d2[l]]

    def layer_axes(l):
        return [1, 0, 0 if l % 2 == 0 else 1, 0, 1, 0]

    def layer_names(l):
        return ["ffn1_w_gu", "ffn1_w_down", *(["w_in_even", "w_out_even"] if l % 2 == 0 else ["w_in_odd", "w_out_odd"]),
                "ffn2_w_gu", "ffn2_w_down"]

    FOX_CARRIES, DN_CARRIES = (0, 1, 3), (2, 4, 5)

    def split_comm(kind, arrays, axes):
        return (Comm(kind, [arrays[i] for i in FOX_CARRIES], [axes[i] for i in FOX_CARRIES]),
                Comm(kind, [arrays[i] for i in DN_CARRIES], [axes[i] for i in DN_CARRIES]))

    def join_split(got_fox, got_dn):
        out = [None] * 6
        for i, a in zip(FOX_CARRIES, got_fox):
            out[i] = a
        for i, a in zip(DN_CARRIES, got_dn):
            out[i] = a
        return out

    sh0, ax0 = layer_shards(0), layer_axes(0)
    first = all_gather(sh0[:2] + [dn_conv_w[None]], ax0[:2] + [0], "ag_layer0", kind="ag2")
    weights = {0: first[:2] + [None] * 4}
    convw = jnp.moveaxis(first[2], 0, 2).reshape(n_even, CONV_WIDTH, -1)
    LAYER0_CARRIES = dict(f1_gu=(2,), f1_down=(3,), mx_in=(5,))

    EVEN_FWD_CARRIES = dict(f1_gu=(), mx_in=(), dn=(1, 2, 5), fox=(0, 4, 3), f2_gu=(), f2_down=())
    saved = []
    cur = xs
    for l in range(L):
        j = l // 2
        wgu1, wd1, win, wout, wgu2, wd2 = weights[l]
        nxt = l + 1 < L
        x0 = cur
        if l % 2 == 0:
            sh_n, ax_n = (layer_shards(l + 1), layer_axes(l + 1)) if nxt else (None, None)
            cm = {k: (Comm("ag2", [sh_n[i] for i in idx], [ax_n[i] for i in idx]) if nxt and idx else None)
                  for k, idx in EVEN_FWD_CARRIES.items()}
            if l == 0:
                cm.update({k: Comm("ag2", [sh0[i] for i in idx], [ax0[i] for i in idx]) for k, idx in LAYER0_CARRIES.items()})
                cm["dn"] = Comm("ag2", cm["dn"].arrays + [sh0[4]], cm["dn"].axes + [ax0[4]])
            x1, s1, got_f1gu, got_f1down = ffn_fwd(x0, norm_ffn1[l:l + 1], wgu1, wd1, f"l{l}f1", comm_gu=cm["f1_gu"],
                                                   comm_down=cm["f1_down"] if l == 0 else None)
            if l == 0:
                win, wout = got_f1gu[0], got_f1down[0]
            x2, sm, got_f, got_d, got_in = even_mixer_fwd(
                x1, norm_mix[l:l + 1], win, wout, convw[j], dn_a_log[j], dn_dt_bias[j], dn_norm_g[j:j + 1],
                fox_q_norm_g[j:j + 1], fox_k_norm_g[j:j + 1], fox_f_bias[j], f"l{l}mx", comm_fox=cm["fox"], comm_dn=cm["dn"],
                comm_in=cm["mx_in"])
            if l == 0:
                wd2, wgu2 = got_in[0], got_d[-1]
                weights[0] = [wgu1, wd1, win, wout, wgu2, wd2]
                got_f1gu, got_in, got_d = [], [], got_d[:-1]
            x3, s2, got_f2gu, got_f2down = ffn_fwd(x2, norm_ffn2[l:l + 1], wgu2, wd2, f"l{l}f2", comm_gu=cm["f2_gu"],
                                                   comm_down=cm["f2_down"])
            if nxt:
                got = dict(f1_gu=got_f1gu, mx_in=got_in, dn=got_d, fox=got_f, f2_gu=got_f2gu, f2_down=got_f2down)
                weights[l + 1] = [None] * 6
                for k, idx in EVEN_FWD_CARRIES.items():
                    for i, a in zip(idx, got[k]):
                        weights[l + 1][i] = a
        else:
            x1, s1, _, _ = ffn_fwd(x0, norm_ffn1[l:l + 1], wgu1, wd1, f"l{l}f1")
            cm = Comm("ag2", layer_shards(l + 1), layer_axes(l + 1)) if nxt else None
            x2, sm, got = odd_mixer_fwd(x1, norm_mix[l:l + 1], win, wout, f"l{l}mx", comm=cm)
            if nxt:
                weights[l + 1] = got
            x3, s2, _, _ = ffn_fwd(x2, norm_ffn2[l:l + 1], wgu2, wd2, f"l{l}f2")
        saved.append((x0, x1, x2, s1, sm, s2))
        cur = x3
    dy, loss_row = loss_head(cur, tgt, "loss")

    gsmall = {n: [None] * W[n].shape[0] for n in SMALL_NAMES}
    gconv = [None] * n_even
    parts = {n: [None] * W[n].shape[0] for n in BIG_NAMES if n != "dn_conv_w"}

    def take(l, recv):
        for nm, p in zip(layer_names(l), recv):
            idx = l if nm.startswith("ffn") else l // 2
            parts[nm][idx] = sum_parts(p.reshape(NDEV, -1, p.shape[-1]), f"sum_{nm}_{idx}").reshape(p.shape[1:])

    pending = None
    for l in reversed(range(L)):
        j = l // 2
        wgu1, wd1, win, wout, wgu2, wd2 = weights[l]
        x0, x1, x2, s1, sm, s2 = saved[l]
        dy, dg, dwgu2, dwd2 = ffn_bwd(x2, norm_ffn2[l:l + 1], wgu2, wd2, s2, dy, f"l{l}f2")
        gsmall["norm_ffn2"][l] = dg[0]
        if l % 2 == 0:
            cf, cd = split_comm("rs", pending, layer_axes(l + 1)) if pending is not None else (None, None)
            cb = None
            if l == 0:
                cd = Comm("rs", cd.arrays + [dwd2], cd.axes + [ax0[5]])
                cb = Comm("rs", [dwgu2], [ax0[4]])
            dy, dwin, dwout, sg, got_f, got_d, got_b = even_mixer_bwd(
                x1, norm_mix[l:l + 1], win, wout, convw[j], dn_norm_g[j:j + 1], fox_q_norm_g[j:j + 1],
                fox_k_norm_g[j:j + 1], sm, dy, f"l{l}mx", comm_fox=cf, comm_dn=cd, comm_bdh=cb)
            if l == 0:
                got_wd2, got_wgu2, got_d = got_d[-1], got_b[0], got_d[:-1]
            if pending is not None:
                take(l + 1, join_split(got_f, got_d))
            gconv[j] = sg.pop("dn_conv_w")
            for k_, v_ in sg.items():
                gsmall[k_][l if k_ == "norm_mix" else j] = v_
        else:
            cm = Comm("rs", pending, layer_axes(l + 1)) if pending is not None else None
            dy, dwin, dwout, sg, got = odd_mixer_bwd(x1, norm_mix[l:l + 1], win, wout, sm, dy, f"l{l}mx", comm=cm)
            if pending is not None:
                take(l + 1, got)
            gsmall["norm_mix"][l] = sg["norm_mix"]
        if l > 0:
            dy, dg, dwgu1, dwd1 = ffn_bwd(x0, norm_ffn1[l:l + 1], wgu1, wd1, s1, dy, f"l{l}f1")
        else:
            rs = lambda a, ax: Comm("rs", [a], [ax])
            dy, dg, dwgu1, dwd1, got0 = ffn_bwd(x0, norm_ffn1[l:l + 1], wgu1, wd1, s1, dy, f"l{l}f1",
                                                comms=dict(bda=rs(dwin, ax0[2]), bwd=rs(dwout, ax0[3])), own_axes=(ax0[0], ax0[1]))
        gsmall["norm_ffn1"][l] = dg[0]
        pending = [dwgu1, dwd1, dwin, dwout, dwgu2, dwd2]
    take(0, [got0["wgu"][0], got0["wd"][0], got0["bda"][0], got0["bwd"][0], got_wgu2, got_wd2])
    grad_x = dy[0][None]

    small_list = [jnp.stack(gsmall[n]) for n in SMALL_NAMES] + [jnp.stack(gconv), loss_row[0, :1]]
    packed = _pack_small(small_list)
    gathered = all_gather([packed], [0], "ag_small")[0]
    summed = sum_parts(gathered.reshape(NDEV, packed.shape[0], LANES), "sum_small")
    small_sum = _unpack_small(summed, small_list)
    G = dict(zip(SMALL_NAMES, small_sum[:len(SMALL_NAMES)]))
    conv_full = small_sum[len(SMALL_NAMES)]
    cwid = dn_conv_w.shape[2]
    G["dn_conv_w"] = lax.dynamic_slice_in_dim(conv_full, me * cwid, cwid, axis=2)
    loss = small_sum[-1][0]

    def unprep_gu(g):
        return g.reshape(L, D, 2, hp)[..., :hs].reshape(L, D, 2 * hs)

    G["ffn1_w_gu"] = unprep_gu(jnp.stack(parts["ffn1_w_gu"]))
    G["ffn2_w_gu"] = unprep_gu(jnp.stack(parts["ffn2_w_gu"]))
    G["ffn1_w_down"] = jnp.stack(parts["ffn1_w_down"])[:, :hs]
    G["ffn2_w_down"] = jnp.stack(parts["ffn2_w_down"])[:, :hs]
    G["w_in_even"] = _even_from_mine(jnp.stack(parts["w_in_even"]), D)
    G["w_out_even"] = jnp.stack(parts["w_out_even"])
    G["w_in_odd"] = jnp.stack(parts["w_in_odd"])
    G["w_out_odd"] = jnp.stack(parts["w_out_odd"])

    delta, new_m, new_v = {}, {}, {}
    for n in BIG_NAMES:
        t = (lambda a: jnp.swapaxes(a, 1, 2)) if n.endswith("w_gu") else (lambda a: a)
        delta[n], new_m[n], new_v[n] = map(t, adamw(t(W[n]), t(G[n]), t(M[n]), t(V[n]), f"adamw_{n}"))
    sw = [W[n] for n in SMALL_NAMES]
    d_, m_, v_ = adamw(_pack_small(sw)[None], _pack_small([G[n] for n in SMALL_NAMES])[None],
                       _pack_small([M[n] for n in SMALL_NAMES])[None], _pack_small([V[n] for n in SMALL_NAMES])[None], "adamw_small")
    for n, a, b, c_ in zip(SMALL_NAMES, _unpack_small(d_, sw), _unpack_small(m_, sw), _unpack_small(v_, sw)):
        delta[n], new_m[n], new_v[n] = a, b, c_

    return (loss, grad_x, *[G[n] for n in WEIGHT_NAMES], *[delta[n] for n in WEIGHT_NAMES],
            *[new_m[n] for n in WEIGHT_NAMES], *[new_v[n] for n in WEIGHT_NAMES])
```

```python
import functools
import math

import jax
import jax.numpy as jnp
from jax import lax
from jax.experimental import pallas as pl
from jax.experimental.pallas import tpu as pltpu

f32 = jnp.float32
bf16 = jnp.bfloat16
SDS = jax.ShapeDtypeStruct

HEAD_DIM = 128
LANES = 128
CONV_WIDTH = 4
DN_CHUNK = 128
EPS = 1e-6
NDEV = 8
VMEM_LIMIT_BYTES = 56 * 1024 * 1024
HI = lax.Precision.HIGHEST
ADAMW_TILE_ELEMS = 384 * 1024

ADAM_LR = 0.001
ADAM_B1 = 0.9
ADAM_B2 = 0.999
ADAM_EPS = 1e-08
ADAM_WD = 0.01
ADAM_STEP = 10

NN = (((1,), (0,)), ((), ()))
NT = (((1,), (1,)), ((), ()))
TN = (((0,), (0,)), ((), ()))


def _me_and_peers():
    x, y, c = lax.axis_index("x"), lax.axis_index("y"), lax.axis_index("c")
    me = 4 * x + 2 * y + c
    peers = []
    for r in range(1, NDEV):
        px = 1 - x if (r >> 2) & 1 else x
        py = 1 - y if (r >> 1) & 1 else y
        pc = 1 - c if r & 1 else c
        peers.append(((px, py, pc), 4 * px + 2 * py + pc))
    return me, peers


def _slab(ref, axis, width, k):
    idx = [slice(None)] * len(ref.shape)
    idx[axis] = pl.ds(k * width, width)
    return ref.at[tuple(idx)]


class Comm:
    def __init__(self, kind, arrays, axes):
        self.kind, self.arrays, self.axes, self.n = kind, list(arrays), list(axes), len(arrays)

    def out_shape(self):
        out = []
        for a, ax in zip(self.arrays, self.axes):
            shp = list(a.shape)
            if self.kind != "rs":
                shp[ax] *= NDEV
                out.append(SDS(tuple(shp), a.dtype))
            else:
                shp[ax] //= NDEV
                out.append(SDS((NDEV, *shp), a.dtype))
        return out

    def sems(self):
        return [pltpu.SemaphoreType.DMA((self.n, NDEV - 1)), pltpu.SemaphoreType.DMA((self.n, NDEV - 1)),
                pltpu.SemaphoreType.DMA((self.n,))]

    def _src(self, ins, t, to_idx):
        if self.kind == "ag":
            return ins[t]
        return _slab(ins[t], self.axes[t], ins[t].shape[self.axes[t]] // NDEV, to_idx)

    def _dst(self, ins, outs, t, from_idx):
        if self.kind != "rs":
            return _slab(outs[t], self.axes[t], ins[t].shape[self.axes[t]], from_idx)
        return outs[t].at[from_idx]

    def _copies(self, ins, outs, sems, sending):
        send_sems, recv_sems, loc_sems = sems
        me, peers = _me_and_peers()
        local, remote = [], []
        for t in range(self.n):
            local.append(pltpu.make_async_copy(self._src(ins, t, me), self._dst(ins, outs, t, me), loc_sems.at[t]))
            for r, (peer, pidx) in enumerate(peers):
                remote.append(pltpu.make_async_remote_copy(
                    src_ref=self._src(ins, t, pidx), dst_ref=self._dst(ins, outs, t, me if sending else pidx),
                    send_sem=send_sems.at[t, r], recv_sem=recv_sems.at[t, r], device_id=peer,
                    device_id_type=pl.DeviceIdType.MESH))
        return local, remote

    def _two_level(self, ins, outs, sems, own=True):
        send_sems, recv_sems, loc_sems = sems
        x, y, c = lax.axis_index("x"), lax.axis_index("y"), lax.axis_index("c")
        me, sibling = (x, y, c), (x, y, 1 - c)
        chips = [(1 - x, y), (x, 1 - y), (1 - x, 1 - y)]
        dev = lambda p: 4 * p[0] + 2 * p[1] + p[2]

        def copy(t, k, block, to, from_shard):
            dst = self._dst(ins, outs, t, dev(block))
            return pltpu.make_async_remote_copy(
                src_ref=ins[t] if from_shard else dst, dst_ref=dst, send_sem=send_sems.at[t, k], recv_sem=recv_sems.at[t, k],
                device_id=to, device_id_type=pl.DeviceIdType.MESH)

        ts = range(self.n) if own else ()
        local = [pltpu.make_async_copy(ins[t], self._dst(ins, outs, t, dev(me)), loc_sems.at[t]) for t in ts]
        first = [copy(t, 0, me, sibling, True) for t in ts]
        first += [copy(t, 1 + j, me, (*chip, c), True) for t in ts for j, chip in enumerate(chips)]
        return local, first, copy, chips, me, sibling, c

    def start(self, ins, outs, sems):
        if self.kind == "ag2":
            local, first = self._two_level(ins, outs, sems)[:2]
            for cp in local + first:
                cp.start()
            return
        local, remote = self._copies(ins, outs, sems, True)
        for cp in local + remote:
            cp.start()

    def relay(self, ins, outs, sems):
        _, _, copy, chips, me, sibling, c = self._two_level(ins, outs, sems, own=False)
        for t in range(self.n):
            for j, chip in enumerate(chips):
                copy(t, 1 + j, (*chip, c), me, False).wait_recv()
                copy(t, 4 + j, (*chip, c), sibling, False).start()

    def wait(self, ins, outs, sems, relayed=False):
        if self.kind == "ag2":
            if not relayed:
                self.relay(ins, outs, sems)
            local, first, copy, chips, me, sibling, c = self._two_level(ins, outs, sems)
            passed = [copy(t, 4 + j, (*chip, c), sibling, False) for t in range(self.n) for j, chip in enumerate(chips)]
            for t in range(self.n):
                copy(t, 0, sibling, me, False).wait_recv()
                for j, chip in enumerate(chips):
                    copy(t, 4 + j, (*chip, 1 - c), me, False).wait_recv()
            for cp in first + passed:
                cp.wait_send()
            for cp in local:
                cp.wait()
            return
        local, remote = self._copies(ins, outs, sems, False)
        for cp in remote:
            cp.wait_recv()
            cp.wait_send()
        for cp in local:
            cp.wait()


def _call(body, name, grid, in_specs, out_specs, out_shape, scratch=(), comm=None):
    params = pltpu.CompilerParams(vmem_limit_bytes=VMEM_LIMIT_BYTES)
    if comm is None:
        return pl.pallas_call(body, name=name, grid=grid, in_specs=in_specs, out_specs=out_specs, out_shape=out_shape,
                              scratch_shapes=list(scratch), compiler_params=params)
    single = not isinstance(out_shape, (tuple, list))
    c_out_specs = [out_specs] if single else list(out_specs)
    c_out_shape = [out_shape] if single else list(out_shape)
    n_in, n_out, n_scr, n = len(in_specs), len(c_out_shape), len(scratch), comm.n
    anyspec = pl.BlockSpec(memory_space=pl.ANY)

    def wrapped(*refs):
        ins, refs = refs[:n_in], refs[n_in:]
        cins, refs = refs[:n], refs[n:]
        outs, refs = refs[:n_out], refs[n_out:]
        couts, refs = refs[:n], refs[n:]
        scr, sems = refs[:n_scr], refs[n_scr:]
        first = functools.reduce(lambda a, b: a & b, [pl.program_id(d) == 0 for d in range(len(grid))], True)
        last = functools.reduce(lambda a, b: a & b, [pl.program_id(d) == grid[d] - 1 for d in range(len(grid))], True)
        if grid:
            steps = math.prod(grid)
            step = functools.reduce(lambda a, d: a * grid[d] + pl.program_id(d), range(len(grid)), 0)
            relay_early = comm.kind == "ag2" and steps >= 4
            pl.when(first)(lambda: comm.start(cins, couts, sems))
            body(*ins, *outs, *scr)
            if relay_early:
                pl.when(step == (3 * steps) // 4)(lambda: comm.relay(cins, couts, sems))
            pl.when(last)(lambda: comm.wait(cins, couts, sems, relayed=relay_early))
        else:
            comm.start(cins, couts, sems)
            if body is not None:
                body(*ins, *outs, *scr)
            comm.wait(cins, couts, sems)

    call = pl.pallas_call(
        wrapped, name=name, grid=grid, in_specs=list(in_specs) + [anyspec] * n, out_specs=c_out_specs + [anyspec] * n,
        out_shape=c_out_shape + comm.out_shape(), scratch_shapes=list(scratch) + comm.sems(), compiler_params=params)

    def run(*args):
        res = call(*args, *comm.arrays)
        mine = res[:n_out]
        return (mine[0] if single else tuple(mine)), list(res[n_out:])

    return run


def _tile(n, want, align=8):
    if n <= want:
        return n
    for t in range(want // align * align, 0, -align):
        if n % t == 0:
            return t
    return n


def _dot(a, b, dims=NN, precision=None):
    return lax.dot_general(a, b, dims, preferred_element_type=f32, precision=precision)


def _logsig(x):
    return jnp.minimum(x, 0.0) - jnp.log(1.0 + jnp.exp(-jnp.abs(x)))


def _softplus(x):
    return jnp.maximum(x, 0.0) + jnp.log(1.0 + jnp.exp(-jnp.abs(x)))


def _rms(x, g):
    return x * lax.rsqrt(jnp.mean(x * x, axis=-1, keepdims=True) + EPS) * g


def _lane_pick(blk, lane_idx):
    lane = lax.broadcasted_iota(jnp.int32, (1, LANES), 1)
    return jnp.sum(jnp.where(lane == lane_idx, blk, 0.0), axis=1, keepdims=True)


def mm(a, b, mode, out_dtype, name, tm=512, tn=512, res=None, scale=1.0, comm=None):
    if mode == "nn":
        (M, K), (_, N) = a.shape, b.shape
    elif mode == "nt":
        (M, K), (N, _) = a.shape, b.shape
    else:
        (K, M), (_, N) = a.shape, b.shape
    tm, tn = _tile(M, tm, LANES), _tile(N, tn, LANES)
    a_spec = pl.BlockSpec((K, tm), lambda j, i: (0, i)) if mode == "tn" else pl.BlockSpec((tm, K), lambda j, i: (i, 0))
    b_spec = pl.BlockSpec((tn, K), lambda j, i: (j, 0)) if mode == "nt" else pl.BlockSpec((K, tn), lambda j, i: (0, j))
    dims = {"nn": NN, "nt": NT, "tn": TN}[mode]
    o_spec = pl.BlockSpec((tm, tn), lambda j, i: (i, j))

    def body(*refs):
        acc = _dot(refs[0][...].astype(bf16), refs[1][...].astype(bf16), dims)
        if scale != 1.0:
            acc = acc * scale
        if res is not None:
            acc = acc + refs[2][...]
        refs[-1][...] = acc.astype(out_dtype)

    ins, specs = [a, b], [a_spec, b_spec]
    if res is not None:
        ins.append(res)
        specs.append(o_spec)
    return _call(body, name, (N // tn, M // tm), specs, o_spec, SDS((M, N), out_dtype), comm=comm)(*ins)


def _rowspec(tm, w, cb=0):
    return pl.BlockSpec((tm, w), lambda i: (i, cb))


def _bspec(w):
    return pl.BlockSpec((1, w), lambda i: (0, 0))


def rms_fwd(x, g, name):
    S, D = x.shape
    tm = _tile(S, 512)

    def body(x_ref, g_ref, h_ref):
        h_ref[...] = _rms(x_ref[...], g_ref[...]).astype(bf16)

    return _call(body, name, (S // tm,), [_rowspec(tm, D), _bspec(D)], _rowspec(tm, D), SDS((S, D), bf16))(x, g)


def _swiglu(g, u):
    return g * jax.nn.sigmoid(g) * u


def ffn_gu_act(h, wgu, name, tm=1024, tn=768, comm=None):
    S, D = h.shape
    W = wgu.shape[1] // 2
    tm, tn = _tile(S, tm, LANES), _tile(W, tn, LANES)
    nb = W // tn

    def body(h_ref, wg_ref, wu_ref, gu_ref, a_ref):
        hb = h_ref[...]
        g = _dot(hb, wg_ref[...])
        u = _dot(hb, wu_ref[...])
        gu_ref[0] = g.astype(bf16)
        gu_ref[1] = u.astype(bf16)
        a_ref[...] = _swiglu(g, u).astype(bf16)

    return _call(body, name, (nb, S // tm),
                 [pl.BlockSpec((tm, D), lambda j, i: (i, 0)), pl.BlockSpec((D, tn), lambda j, i: (0, j)),
                  pl.BlockSpec((D, tn), lambda j, i: (0, nb + j))],
                 (pl.BlockSpec((2, tm, tn), lambda j, i: (0, i, j)), pl.BlockSpec((tm, tn), lambda j, i: (i, j))),
                 (SDS((2, S, W), bf16), SDS((S, W), bf16)), comm=comm)(h, wgu, wgu)


def ffn_bda_act(dy, wd, gu, scale, name, tm=1024, tn=768, comm=None):
    S, D = dy.shape
    W = wd.shape[0]
    tm, tn = _tile(S, tm, LANES), _tile(W, tn, LANES)

    def body(dy_ref, wd_ref, gu_ref, dgu_ref):
        da = _dot(dy_ref[...].astype(bf16), wd_ref[...], NT) * scale
        _, vjp = jax.vjp(_swiglu, gu_ref[0].astype(f32), gu_ref[1].astype(f32))
        dg, du = vjp(da)
        dgu_ref[0] = dg.astype(bf16)
        dgu_ref[1] = du.astype(bf16)

    planes = pl.BlockSpec((2, tm, tn), lambda j, i: (0, i, j))
    return _call(body, name, (W // tn, S // tm),
                 [pl.BlockSpec((tm, D), lambda j, i: (i, 0)), pl.BlockSpec((tn, D), lambda j, i: (j, 0)), planes],
                 planes, SDS((2, S, W), bf16), comm=comm)(dy, wd, gu)


def ffn_bwgu(h, dgu, name, tm=1024, tn=768, comm=None):
    S, D = h.shape
    W = dgu.shape[2]
    tm, tn = _tile(D, tm, LANES), _tile(W, tn, LANES)
    nb = W // tn

    def body(h_ref, d_ref, o_ref):
        o_ref[...] = _dot(h_ref[...], d_ref[...], TN).astype(bf16)

    return _call(body, name, (2 * nb, D // tm),
                 [pl.BlockSpec((S, tm), lambda j, i: (0, i)), pl.BlockSpec((None, S, tn), lambda j, i: (j // nb, 0, j % nb))],
                 pl.BlockSpec((tm, tn), lambda j, i: (i, j)), SDS((D, 2 * W), bf16), comm=comm)(h, dgu)


def nt_rms_bwd(pairs, x, g, dres, name, tm=512, comm=None):
    S, D = x.shape
    tm = _tile(S, tm)
    n = len(pairs)

    def body(*refs):
        x_ref, g_ref, dres_ref = refs[2 * n:2 * n + 3]
        dx_ref, dxb_ref, dg_ref = refs[2 * n + 3:]
        dh = sum(_dot(refs[2 * k][...].astype(bf16), refs[2 * k + 1][...], NT) for k in range(n))
        _, vjp = jax.vjp(_rms, x_ref[...], g_ref[...])
        dx, dg = vjp(dh)
        dx = dres_ref[...] + dx
        dx_ref[...] = dx
        dxb_ref[...] = dx.astype(bf16)

        @pl.when(pl.program_id(0) == 0)
        def _():
            dg_ref[...] = jnp.zeros_like(dg_ref)

        dg_ref[...] += dg

    arrays, specs = [], []
    for a, a_spec, b, b_spec in pairs:
        arrays += [a, b]
        specs += [a_spec, b_spec]
    (dx, dxb, dg), got = _carried(_call(body, name, (S // tm,), specs + [_rowspec(tm, D), _bspec(D), _rowspec(tm, D)],
                                        (_rowspec(tm, D), _rowspec(tm, D), _bspec(D)),
                                        (SDS((S, D), f32), SDS((S, D), bf16), SDS((1, D), f32)), comm=comm)(*arrays, x, g, dres),
                                  comm)
    return ((dx, dxb), dg) if comm is None else ((dx, dxb), dg, got)


def ffn_bdh_rms(dgu, wgu, x, g, dres, name, tm=512, comm=None):
    _, S, W = dgu.shape
    D = wgu.shape[0]
    tm = _tile(S, tm)
    pairs = [(dgu, pl.BlockSpec((None, tm, W), functools.partial(lambda p, i: (p, i, 0), p)),
              wgu, pl.BlockSpec((D, W), functools.partial(lambda p, i: (0, p), p))) for p in range(2)]
    return nt_rms_bwd(pairs, x, g, dres, name, tm, comm=comm)


def mm_bdh_rms(d, w, x, g, dres, name, tm=512, comm=None):
    S, K = d.shape
    tm = _tile(S, tm)
    return nt_rms_bwd([(d, pl.BlockSpec((tm, K), lambda i: (i, 0)), w, pl.BlockSpec(w.shape, lambda i: (0, 0)))], x, g, dres, name, tm,
                      comm=comm)


def loss_head(y, t, name):
    S, D = y.shape
    tm = _tile(S, 512)

    def body(y_ref, t_ref, dy_ref, dyb_ref, l_ref):
        e = y_ref[...] - t_ref[...]
        dy_ref[...] = e * (1.0 / D)
        dyb_ref[...] = (e * (1.0 / D)).astype(bf16)

        @pl.when(pl.program_id(0) == 0)
        def _():
            l_ref[...] = jnp.zeros_like(l_ref)

        l_ref[...] += jnp.sum(jnp.sum(e * e, axis=1, keepdims=True), axis=0, keepdims=True) * (0.5 / D)

    dy, dyb, loss = _call(body, name, (S // tm,), [_rowspec(tm, D), _rowspec(tm, D)],
                          (_rowspec(tm, D), _rowspec(tm, D), _bspec(LANES)),
                          (SDS((S, D), f32), SDS((S, D), bf16), SDS((1, LANES), f32)))(y, t)
    return (dy, dyb), loss


def ffn_fwd(x, g, wgu, wd, tag, comm_gu=None, comm_down=None):
    h = rms_fwd(x, g, f"{tag}_rms")
    (gu, a), got_gu = _carried(ffn_gu_act(h, wgu, f"{tag}_gu", comm=comm_gu), comm_gu)
    xo, got_down = _carried(mm(a, wd, "nn", f32, f"{tag}_down", tm=512, tn=1024, res=x, scale=0.5, comm=comm_down), comm_down)
    return xo, (h, gu, a), got_gu, got_down


def ffn_bwd(x, g, wgu, wd, saved, dy, tag, comms=None, own_axes=None):
    h, gu, a = saved
    dy32, dy16 = dy
    cm = comms or {}
    dgu, got_bda = _carried(ffn_bda_act(dy16, wd, gu, 0.5, f"{tag}_bda", comm=cm.get("bda")), cm.get("bda"))
    dwd, got_bwd = _carried(mm(a, dy16, "tn", bf16, f"{tag}_bwd", tm=512, tn=1024, scale=0.5, comm=cm.get("bwd")), cm.get("bwd"))
    c_wd = Comm("rs", [dwd], [own_axes[1]]) if own_axes else None
    dwgu, got_wd = _carried(ffn_bwgu(h, dgu, f"{tag}_bwgu", comm=c_wd), c_wd)
    c_wgu = Comm("rs", [dwgu], [own_axes[0]]) if own_axes else None
    res = ffn_bdh_rms(dgu, wgu, x, g, dy32, f"{tag}_bdh", comm=c_wgu)
    dx, dg = res[:2]
    if comms is None and own_axes is None:
        return dx, dg, dwgu, dwd
    return dx, dg, dwgu, dwd, dict(bda=got_bda, bwd=got_bwd, wd=got_wd, wgu=res[2] if c_wgu is not None else [])


def _split_bf16(x):
    hi = x.astype(bf16)
    lo = (x - hi.astype(f32)).astype(bf16)
    return hi, lo


SB_TQ, SB_TK, SB_NSUB = 512, 256, 4
FOX_TK = 256


def _sb_geometry(S, tk=SB_TK):
    tq = min(SB_TQ, S)
    tk = min(tk, tq)
    return tq, tk, tq // SB_NSUB, tq // tk


def _sb_consts(tk):
    r = lax.broadcasted_iota(jnp.int32, (tk, tk), 0)
    c = lax.broadcasted_iota(jnp.int32, (tk, tk), 1)
    return (r > c).astype(bf16), (r < c).astype(bf16)


def _active(d, nsub, rs, tk):
    return [s for s in range(nsub) if d is None or (s + 1) * rs > d * tk]


def _put(full, act, part):
    out = list(full)
    for s, x in zip(act, part):
        out[s] = x
    return out


def _sb_scores(qs, k, kb, tk, rows, masked):
    scale = HEAD_DIM ** -0.5
    z = [_dot(q, k, NT) * scale for q in qs]
    ls = [_logsig(z_) for z_ in z]
    lm = [l - z_ for l, z_ in zip(ls, z)]
    ok = None
    if masked:
        col = kb * tk + lax.broadcasted_iota(jnp.int32, z[0].shape, 1)
        ok = [col < row for row in rows]
        lm = [jnp.where(m, x, 0.0) for m, x in zip(ok, lm)]
    return ls, lm, ok


def _sb_weights(ls, lm, ok, tail, after):
    suf = [_dot(x.astype(bf16), after) for x in lm]
    a = [jnp.exp(l + s_ + t_) for l, s_, t_ in zip(ls, suf, tail)]
    if ok is not None:
        a = [jnp.where(m, x, 0.0) for m, x in zip(ok, a)]
    return a


def sb_fwd(qkv, nh, name, comm=None):
    S = qkv.shape[0]
    tq, tk, rs, nd = _sb_geometry(S)
    nsub = tq // rs

    def body(q_ref, k_ref, v_ref, o_ref, tails_ref):
        i = pl.program_id(1)
        after, _ = _sb_consts(tk)
        qs = [q_ref[pl.ds(s * rs, rs), :] for s in range(nsub)]
        rows = [i * tq + s * rs + lax.broadcasted_iota(jnp.int32, (rs, tk), 0) for s in range(nsub)]

        def tile(kb, carry, d):
            tail, acc = carry
            act = _active(d, nsub, rs, tk)
            pick = lambda lst: [lst[s] for s in act]
            off = pl.multiple_of(kb * tk, tk)
            k = k_ref[pl.ds(off, tk), :]
            v = v_ref[pl.ds(off, tk), :]
            ls, lm, ok = _sb_scores(pick(qs), k, kb, tk, pick(rows), d is not None)
            a = _sb_weights(ls, lm, ok, pick(tail), after)
            tails_ref[pl.ds(kb, 1), :] += jnp.concatenate([t_.T for t_ in tail], axis=1)
            acc = _put(acc, act, [ac + _dot(a_.astype(bf16), v) for ac, a_ in zip(pick(acc), a)])
            tail = _put(tail, act, [t_ + jnp.sum(x, axis=1, keepdims=True) for t_, x in zip(pick(tail), lm)])
            return tail, acc

        tails_ref[...] = jnp.zeros_like(tails_ref)
        carry = ([jnp.zeros((rs, 1), f32)] * nsub, [jnp.zeros((rs, HEAD_DIM), f32)] * nsub)
        for d in reversed(range(nd)):
            carry = tile(i * nd + d, carry, d)
        carry = lax.fori_loop(0, i * nd, lambda t, c: tile(i * nd - 1 - t, c, None), carry)
        for s in range(nsub):
            o_ref[pl.ds(s * rs, rs), :] = carry[1][s].astype(o_ref.dtype)

    blk = lambda cb0: pl.BlockSpec((tq, HEAD_DIM), lambda h, i: (i, cb0 + h))
    full = lambda cb0: pl.BlockSpec((S, HEAD_DIM), lambda h, i: (0, cb0 + h))
    tails = pl.BlockSpec((S // tk, tq), lambda h, i: (h, i))
    return _call(body, name, (nh, S // tq), [blk(0), full(nh), full(2 * nh)], (blk(0), tails),
                 (SDS((S, nh * HEAD_DIM), bf16), SDS((nh * (S // tk), S), f32)), comm=comm)(qkv, qkv, qkv)


def sb_bwd(qkv, tails, do, nh, name, comm=None):
    S = qkv.shape[0]
    tq, tk, rs, nd = _sb_geometry(S)
    nsub = tq // rs
    scale = HEAD_DIM ** -0.5

    def body(q_ref, k_ref, v_ref, tails_ref, do_ref, dq_ref, dk_ref, dv_ref):
        i = pl.program_id(1)

        @pl.when(i == 0)
        def _():
            dk_ref[...] = jnp.zeros_like(dk_ref)
            dv_ref[...] = jnp.zeros_like(dv_ref)

        after, before = _sb_consts(tk)
        qs = [q_ref[pl.ds(s * rs, rs), :] for s in range(nsub)]
        dos = [do_ref[pl.ds(s * rs, rs), :].astype(bf16) for s in range(nsub)]
        rows = [i * tq + s * rs + lax.broadcasted_iota(jnp.int32, (rs, tk), 0) for s in range(nsub)]

        def sweep2(kb, carry, d):
            pe, dq = carry
            act = _active(d, nsub, rs, tk)
            pick = lambda lst: [lst[s] for s in act]
            qa, da = pick(qs), pick(dos)
            off = pl.multiple_of(kb * tk, tk)
            k = k_ref[pl.ds(off, tk), :]
            v = v_ref[pl.ds(off, tk), :]
            ls, lm, ok = _sb_scores(qa, k, kb, tk, pick(rows), d is not None)
            tail_row = tails_ref[pl.ds(kb, 1), :]
            tail = [tail_row[:, s * rs:(s + 1) * rs].T for s in act]
            a = _sb_weights(ls, lm, ok, tail, after)
            e = [_dot(d_, v, NT) * a_ for d_, a_ in zip(da, a)]
            cum_e = [p_ + _dot(e_.astype(bf16), before) for p_, e_ in zip(pick(pe), e)]
            sig = [jnp.exp(l) for l in ls]
            dz = [e_ * (1.0 - sg) - c_ * sg for e_, sg, c_ in zip(e, sig, cum_e)]
            if ok is not None:
                dz = [jnp.where(m, x, 0.0) for m, x in zip(ok, dz)]
            dzb = [(x * scale).astype(bf16) for x in dz]
            dk_ref[pl.ds(off, tk), :] += sum(_dot(x, q, TN) for x, q in zip(dzb, qa))
            dv_ref[pl.ds(off, tk), :] += sum(_dot(a_.astype(bf16), d_, TN) for a_, d_ in zip(a, da))
            pe = _put(pe, act, [p_ + jnp.sum(e_, axis=1, keepdims=True) for p_, e_ in zip(pick(pe), e)])
            dq = _put(dq, act, [g + _dot(x, k) for g, x in zip(pick(dq), dzb)])
            return pe, dq

        carry = ([jnp.zeros((rs, 1), f32)] * nsub, [jnp.zeros((rs, HEAD_DIM), f32)] * nsub)
        carry = lax.fori_loop(0, i * nd, lambda kb, c: sweep2(kb, c, None), carry)
        for d in range(nd):
            carry = sweep2(i * nd + d, carry, d)
        for s in range(nsub):
            dq_ref[pl.ds(s * rs, rs), :] = carry[1][s]

    blk = lambda cb0: pl.BlockSpec((tq, HEAD_DIM), lambda h, i: (i, cb0 + h))
    full = lambda cb0: pl.BlockSpec((S, HEAD_DIM), lambda h, i: (0, cb0 + h))
    sh = SDS((S, nh * HEAD_DIM), f32)
    tails_spec = pl.BlockSpec((S // tk, tq), lambda h, i: (h, i))
    return _call(body, name, (nh, S // tq), [blk(0), full(nh), full(2 * nh), tails_spec, blk(0)], (blk(0), full(0), full(0)),
                 (sh, sh, sh), comm=comm)(qkv, qkv, qkv, tails, do)


def fox_fwd(fq, fk, fv, c, cT, nh, lane0, name, comm=None):
    S = fq.shape[0]
    tq, tk, rs, nd = _sb_geometry(S, FOX_TK)
    nsub = tq // rs
    scale = HEAD_DIM ** -0.5

    def body(q_ref, k_ref, v_ref, c_ref, ct_ref, o_ref, lse_ref):
        h = pl.program_id(0)
        i = pl.program_id(1)
        subs = range(nsub)
        qs = [q_ref[pl.ds(s * rs, rs), :] for s in subs]
        ci = [_lane_pick(c_ref[pl.ds(s * rs, rs), :], lane0 + h) for s in subs]
        rows = [i * tq + s * rs + lax.broadcasted_iota(jnp.int32, (rs, tk), 0) for s in subs]

        def tile(kb, carry, d):
            m, l, acc = carry
            act = _active(d, nsub, rs, tk)
            pick = lambda lst: [lst[s] for s in act]
            off = pl.multiple_of(kb * tk, tk)
            k = k_ref[pl.ds(off, tk), :]
            v = v_ref[pl.ds(off, tk), :]
            cj = ct_ref[pl.ds(lane0 % 8 + h, 1), pl.ds(off, tk)]
            sc = [_dot(q, k, NT) * scale + (c_ - cj) for q, c_ in zip(pick(qs), pick(ci))]
            if d is not None:
                col = kb * tk + lax.broadcasted_iota(jnp.int32, (rs, tk), 1)
                sc = [jnp.where(col <= row, x, -jnp.inf) for x, row in zip(sc, pick(rows))]
            m2 = [jnp.maximum(m_, jnp.max(x, axis=1, keepdims=True)) for m_, x in zip(pick(m), sc)]
            p = [jnp.exp(x - m_) for x, m_ in zip(sc, m2)]
            al = [jnp.exp(a - b) for a, b in zip(pick(m), m2)]
            l = _put(l, act, [a * l_ + jnp.sum(p_, axis=1, keepdims=True) for a, l_, p_ in zip(al, pick(l), p)])
            acc = _put(acc, act, [a * ac + _dot(p_.astype(bf16), v) for a, ac, p_ in zip(al, pick(acc), p)])
            return _put(m, act, m2), l, acc

        carry = ([jnp.full((rs, 1), -jnp.inf, f32)] * nsub, [jnp.zeros((rs, 1), f32)] * nsub,
                 [jnp.zeros((rs, HEAD_DIM), f32)] * nsub)
        for d in range(nd):
            carry = tile(i * nd + d, carry, d)
        m, l, acc = lax.fori_loop(0, i * nd, lambda kb, cr: tile(kb, cr, None), carry)
        for s in subs:
            o_ref[pl.ds(s * rs, rs), :] = acc[s] / l[s]
            lse_ref[pl.ds(s * rs, rs), :] = jnp.broadcast_to(m[s] + jnp.log(l[s]), (rs, HEAD_DIM))

    blk = pl.BlockSpec((tq, HEAD_DIM), lambda h, i: (i, h))
    full = pl.BlockSpec((S, HEAD_DIM), lambda h, i: (0, h))
    cspec = pl.BlockSpec((tq, LANES), lambda h, i: (i, 0))
    ctspec = pl.BlockSpec((8, S), lambda h, i: (lane0 // 8, 0))
    sh = SDS((S, nh * HEAD_DIM), f32)
    return _call(body, name, (nh, S // tq), [blk, full, full, cspec, ctspec], (blk, blk), (sh, sh), comm=comm)(fq, fk, fv, c, cT)


def fox_bwd(fq, fk, fv, c, cT, o, lse, do, nh, lane0, name, comm=None):
    S = fq.shape[0]
    tq, tk, rs, nd = _sb_geometry(S, FOX_TK)
    nsub = tq // rs
    scale = HEAD_DIM ** -0.5

    def body(q_ref, k_ref, v_ref, c_ref, ct_ref, o_ref, lse_ref, do_ref, dq_ref, dk_ref, dv_ref, dct_ref):
        h = pl.program_id(0)
        i = pl.program_id(1)

        @pl.when(i == 0)
        def _():
            dk_ref[...] = jnp.zeros_like(dk_ref)
            dv_ref[...] = jnp.zeros_like(dv_ref)

        @pl.when((i == 0) & (h == 0))
        def _():
            dct_ref[...] = jnp.zeros_like(dct_ref)

        subs = range(nsub)
        qs = [q_ref[pl.ds(s * rs, rs), :] for s in subs]
        dof = [do_ref[pl.ds(s * rs, rs), :] for s in subs]
        dos = [x.astype(bf16) for x in dof]
        ci = [_lane_pick(c_ref[pl.ds(s * rs, rs), :], lane0 + h) for s in subs]
        lses = [lse_ref[pl.ds(s * rs, rs), :][:, :1] for s in subs]
        dl = [jnp.sum(x * o_ref[pl.ds(s * rs, rs), :], axis=1, keepdims=True) for s, x in zip(subs, dof)]
        rows = [i * tq + s * rs + lax.broadcasted_iota(jnp.int32, (rs, tk), 0) for s in subs]

        def tile(kb, carry, d):
            dq, rsum = carry
            act = _active(d, nsub, rs, tk)
            pick = lambda lst: [lst[s] for s in act]
            qa, da = pick(qs), pick(dos)
            off = pl.multiple_of(kb * tk, tk)
            k = k_ref[pl.ds(off, tk), :]
            v = v_ref[pl.ds(off, tk), :]
            cj = ct_ref[pl.ds(lane0 % 8 + h, 1), pl.ds(off, tk)]
            p = [jnp.exp(_dot(q, k, NT) * scale + (c_ - cj) - ls) for q, c_, ls in zip(qa, pick(ci), pick(lses))]
            if d is not None:
                col = kb * tk + lax.broadcasted_iota(jnp.int32, (rs, tk), 1)
                p = [jnp.where(col <= row, x, 0.0) for x, row in zip(p, pick(rows))]
            ds = [p_ * (_dot(d_, v, NT) - dl_) for p_, d_, dl_ in zip(p, da, pick(dl))]
            dsb = [(x * scale).astype(bf16) for x in ds]
            dk_ref[pl.ds(off, tk), :] += sum(_dot(x, q, TN) for x, q in zip(dsb, qa))
            dv_ref[pl.ds(off, tk), :] += sum(_dot(p_.astype(bf16), d_, TN) for p_, d_ in zip(p, da))
            dct_ref[pl.ds(lane0 + h, 1), pl.ds(off, tk)] -= sum(jnp.sum(x, axis=0, keepdims=True) for x in ds)
            return (_put(dq, act, [g + _dot(x, k) for g, x in zip(pick(dq), dsb)]),
                    _put(rsum, act, [r_ + jnp.sum(x, axis=1, keepdims=True) for r_, x in zip(pick(rsum), ds)]))

        carry = ([jnp.zeros((rs, HEAD_DIM), f32)] * nsub, [jnp.zeros((rs, 1), f32)] * nsub)
        carry = lax.fori_loop(0, i * nd, lambda kb, cr: tile(kb, cr, None), carry)
        for d in range(nd):
            carry = tile(i * nd + d, carry, d)
        dq, rsum = carry
        for s in subs:
            dq_ref[pl.ds(s * rs, rs), :] = dq[s]
        dct_ref[pl.ds(lane0 + h, 1), pl.ds(pl.multiple_of(i * tq, tq), tq)] += jnp.concatenate([r_.T for r_ in rsum], axis=1)

    blk = pl.BlockSpec((tq, HEAD_DIM), lambda h, i: (i, h))
    full = pl.BlockSpec((S, HEAD_DIM), lambda h, i: (0, h))
    cspec = pl.BlockSpec((tq, LANES), lambda h, i: (i, 0))
    ctspec = pl.BlockSpec((8, S), lambda h, i: (lane0 // 8, 0))
    dctspec = pl.BlockSpec((LANES, S), lambda h, i: (0, 0))
    sh = SDS((S, nh * HEAD_DIM), f32)
    return _call(body, name, (nh, S // tq), [blk, full, full, cspec, ctspec, blk, blk, blk], (blk, full, full, dctspec),
                 (sh, sh, sh, SDS((LANES, S), f32)), comm=comm)(fq, fk, fv, c, cT, o, lse, do)


def gates_fwd(proj, small_cb, fbias_row, name, tm=256):
    S = proj.shape[0]
    tm = _tile(S, tm)

    def body(sm_ref, fb_ref, c_ref, ct_ref, carry_ref):
        @pl.when(pl.program_id(0) == 0)
        def _():
            carry_ref[...] = jnp.zeros_like(carry_ref)

        lf = _logsig(sm_ref[...] + fb_ref[...])
        r = lax.broadcasted_iota(jnp.int32, (tm, tm), 0)
        cc = lax.broadcasted_iota(jnp.int32, (tm, tm), 1)
        c = _dot((r >= cc).astype(f32), lf, NN, HI) + carry_ref[...]
        carry_ref[...] += jnp.sum(lf, axis=0, keepdims=True)
        c_ref[...] = c
        ct_ref[...] = c.T

    return _call(body, name, (S // tm,), [_rowspec(tm, LANES, small_cb), _bspec(LANES)],
                 (_rowspec(tm, LANES), pl.BlockSpec((LANES, tm), lambda i: (0, i))),
                 (SDS((S, LANES), f32), SDS((LANES, S), f32)), scratch=[pltpu.VMEM((1, LANES), f32)])(proj, fbias_row)


def gates_bwd(proj, small_cb, fbias_row, dcT, dsm_dn, lane0, nh, name, tm=256):
    S = proj.shape[0]
    tm = _tile(S, tm)
    nt = S // tm

    def body(sm_ref, fb_ref, dct_ref, dsm_ref, o_ref, dfb_ref, carry_ref):
        @pl.when(pl.program_id(0) == 0)
        def _():
            carry_ref[...] = jnp.zeros_like(carry_ref)
            dfb_ref[...] = jnp.zeros_like(dfb_ref)

        dc = dct_ref[...].T
        r = lax.broadcasted_iota(jnp.int32, (tm, tm), 0)
        cc = lax.broadcasted_iota(jnp.int32, (tm, tm), 1)
        dlf = _dot((r <= cc).astype(f32), dc, NN, HI) + carry_ref[...]
        carry_ref[...] += jnp.sum(dc, axis=0, keepdims=True)
        lane = lax.broadcasted_iota(jnp.int32, (1, LANES), 1)
        dpre = jnp.where((lane >= lane0) & (lane < lane0 + nh), dlf * jax.nn.sigmoid(-(sm_ref[...] + fb_ref[...])), 0.0)
        o_ref[...] = dsm_ref[...] + dpre
        dfb_ref[...] += jnp.sum(dpre, axis=0, keepdims=True)

    rev = lambda i: nt - 1 - i
    return _call(body, name, (nt,),
                 [pl.BlockSpec((tm, LANES), lambda i: (rev(i), small_cb)), _bspec(LANES),
                  pl.BlockSpec((LANES, tm), lambda i: (0, rev(i))), pl.BlockSpec((tm, LANES), lambda i: (rev(i), 0))],
                 (pl.BlockSpec((tm, LANES), lambda i: (rev(i), 0)), _bspec(LANES)),
                 (SDS((S, LANES), f32), SDS((1, LANES), f32)), scratch=[pltpu.VMEM((1, LANES), f32)])(proj, fbias_row, dcT, dsm_dn)


PAD_ROWS = 8


def conv_fwd(proj, w, width, name):
    S = proj.shape[0]
    cw = 256 if width % 256 == 0 else 128

    def body(x_ref, w_ref, y_ref, pad_ref):
        pad_ref[pl.ds(0, PAD_ROWS), :] = jnp.zeros((PAD_ROWS, cw), f32)
        pad_ref[pl.ds(PAD_ROWS, S), :] = x_ref[...]
        y = jnp.zeros((S, cw), f32)
        for i in range(CONV_WIDTH):
            y = y + pad_ref[pl.ds(PAD_ROWS - (CONV_WIDTH - 1) + i, S), :] * w_ref[pl.ds(i, 1), :]
        y_ref[...] = y * jax.nn.sigmoid(y)

    spec = pl.BlockSpec((S, cw), lambda j: (0, j))
    return _call(body, name, (width // cw,), [spec, pl.BlockSpec((CONV_WIDTH, cw), lambda j: (0, j))], spec,
                 SDS((S, width), f32), scratch=[pltpu.VMEM((S + PAD_ROWS, cw), f32)])(proj, w)


def conv_bwd(proj, w, dy, name):
    S, width = dy.shape
    cw = 256 if width % 256 == 0 else 128

    def body(x_ref, w_ref, dy_ref, dx_ref, dw_ref, xpad_ref, dpad_ref):
        xpad_ref[pl.ds(0, PAD_ROWS), :] = jnp.zeros((PAD_ROWS, cw), f32)
        xpad_ref[pl.ds(PAD_ROWS, S), :] = x_ref[...]
        y = jnp.zeros((S, cw), f32)
        for i in range(CONV_WIDTH):
            y = y + xpad_ref[pl.ds(PAD_ROWS - (CONV_WIDTH - 1) + i, S), :] * w_ref[pl.ds(i, 1), :]
        sg = jax.nn.sigmoid(y)
        dpre = dy_ref[...] * (sg * (1.0 + y * (1.0 - sg)))
        dpad_ref[pl.ds(S, PAD_ROWS), :] = jnp.zeros((PAD_ROWS, cw), f32)
        dpad_ref[pl.ds(0, S), :] = dpre
        dx = jnp.zeros((S, cw), f32)
        for i in range(CONV_WIDTH):
            dx = dx + dpad_ref[pl.ds(CONV_WIDTH - 1 - i, S), :] * w_ref[pl.ds(i, 1), :]
            dw_ref[pl.ds(i, 1), :] = jnp.sum(dpre * xpad_ref[pl.ds(PAD_ROWS - (CONV_WIDTH - 1) + i, S), :], axis=0, keepdims=True)
        dx_ref[...] = dx

    spec = pl.BlockSpec((S, cw), lambda j: (0, j))
    wspec = pl.BlockSpec((CONV_WIDTH, cw), lambda j: (0, j))
    return _call(body, name, (width // cw,), [spec, wspec, spec], (spec, wspec),
                 (SDS((S, width), f32), SDS((CONV_WIDTH, width), f32)),
                 scratch=[pltpu.VMEM((S + PAD_ROWS, cw), f32), pltpu.VMEM((S + PAD_ROWS, cw), f32)])(proj, w, dy)


def _pdot_raw(a, b, dims, passes):
    if passes == 1:
        return _dot(a.astype(bf16), b.astype(bf16), dims)
    ah, al = _split_bf16(a)
    bh, bl = _split_bf16(b)
    return _dot(ah, bh, dims) + (_dot(ah, bl, dims) + _dot(al, bh, dims))


def _make_pdot(passes):
    @functools.partial(jax.custom_vjp, nondiff_argnums=(2,))
    def pdot(a, b, mode):
        return _pdot_raw(a, b, {"nn": NN, "nt": NT, "tn": TN}[mode], passes)

    def fwd(a, b, mode):
        return pdot(a, b, mode), (a, b)

    def bwd(mode, res, g):
        a, b = res
        if mode == "nn":
            return _pdot_raw(g, b, NT, passes), _pdot_raw(a, g, TN, passes)
        if mode == "nt":
            return _pdot_raw(g, b, NN, passes), _pdot_raw(g, a, TN, passes)
        return _pdot_raw(b, g, NT, passes), _pdot_raw(a, g, NN, passes)

    pdot.defvjp(fwd, bwd)
    return pdot


_dot1 = _make_pdot(1)
_dot3 = _make_pdot(3)


def _each(f, *lists):
    return [f(*xs) for xs in zip(*lists)]


def _dn_chunk(ndn, cq, ck, cv, small, alog, dtb, s0):
    C = cq[0].shape[0]
    hs = list(range(ndn))
    b = [_lane_pick(small, h) for h in hs]
    d = [_lane_pick(small, ndn + h) for h in hs]
    al = [_lane_pick(alog, h) for h in hs]
    dt = [_lane_pick(dtb, h) for h in hs]
    q = _each(lambda x: x * lax.rsqrt(jnp.sum(x * x, axis=1, keepdims=True) + EPS) * (HEAD_DIM ** -0.5), cq)
    k = _each(lambda x: x * lax.rsqrt(jnp.sum(x * x, axis=1, keepdims=True) + EPS), ck)
    beta = _each(jax.nn.sigmoid, b)
    g = _each(lambda al_, d_, dt_: -jnp.exp(al_) * _softplus(d_ + dt_), al, d, dt)
    r = lax.broadcasted_iota(jnp.int32, (C, C), 0)
    c = lax.broadcasted_iota(jnp.int32, (C, C), 1)
    incl, strict = r >= c, r > c
    inclf = incl.astype(f32)
    gc = _each(lambda g_: _dot3(inclf, g_, "nn"), g)
    decay = _each(lambda gc_: jnp.where(incl, jnp.exp(jnp.where(incl, gc_ - gc_.T, 0.0)), 0.0), gc)
    kb = _each(lambda k_, b_: k_ * b_, k, beta)
    a = _each(lambda kb_, k_, dec: jnp.where(strict, _dot3(kb_, k_, "nt") * dec, 0.0), kb, k, decay)
    eye = (r == c).astype(f32)
    t = _each(lambda a_: eye - a_, a)
    p = a
    for _ in range(int(math.log2(C)) - 1):
        p = _each(lambda p_: _dot3(p_, p_, "nn"), p)
        t = _each(lambda t_, p_: t_ + _dot3(t_, p_, "nn"), t, p)
    eg = _each(jnp.exp, gc)
    u = _each(lambda t_, v_, b_: _dot3(t_, v_ * b_, "nn"), t, cv, beta)
    w = _each(lambda t_, kb_, eg_: _dot3(t_, kb_ * eg_, "nn"), t, kb, eg)
    attn = _each(lambda q_, k_, dec: _dot1(q_, k_, "nt") * dec, q, k, decay)
    g_last = _each(lambda g_: jnp.sum(g_, axis=0, keepdims=True), g)
    v_new = _each(lambda u_, w_, s_: u_ - _dot1(w_, s_, "nn"), u, w, s0)
    o = _each(lambda q_, eg_, s_, at, vn: _dot1(q_ * eg_, s_, "nn") + _dot1(at, vn, "nn"), q, eg, s0, attn, v_new)
    s1 = _each(lambda s_, gl, k_, gc_, vn: s_ * jnp.exp(gl) + _dot1(k_ * jnp.exp(gl - gc_), vn, "tn"), s0, g_last, k, gc, v_new)
    return o, s1


def dn_fwd(cqkv, proj, small_cb, alog_row, dt_row, ndn, name, comm=None):
    S = cqkv.shape[0]
    C = DN_CHUNK
    nc = S // C
    half = ndn * HEAD_DIM

    def body(q_ref, k_ref, v_ref, sm_ref, al_ref, dt_ref, o_ref, ss_ref, s_ref):
        @pl.when(pl.program_id(0) == 0)
        def _():
            s_ref[...] = jnp.zeros_like(s_ref)

        sls = [slice(h * HEAD_DIM, (h + 1) * HEAD_DIM) for h in range(ndn)]
        s0 = [s_ref[h] for h in range(ndn)]
        for h in range(ndn):
            ss_ref[h, 0] = s0[h]
        o, s1 = _dn_chunk(ndn, [q_ref[:, sl] for sl in sls], [k_ref[:, sl] for sl in sls], [v_ref[:, sl] for sl in sls],
                          sm_ref[...], al_ref[...], dt_ref[...], s0)
        for h in range(ndn):
            o_ref[:, sls[h]] = o[h]
            s_ref[h] = s1[h]

    blk = lambda cb: pl.BlockSpec((C, half), lambda n: (n, cb))
    row = pl.BlockSpec((1, LANES), lambda n: (0, 0))
    return _call(body, name, (nc,),
                 [blk(0), blk(1), blk(2), pl.BlockSpec((C, LANES), lambda n: (n, small_cb)), row, row],
                 (blk(0), pl.BlockSpec((ndn, 1, HEAD_DIM, HEAD_DIM), lambda n: (0, n, 0, 0))),
                 (SDS((S, half), f32), SDS((ndn, nc, HEAD_DIM, HEAD_DIM), f32)),
                 scratch=[pltpu.VMEM((ndn, HEAD_DIM, HEAD_DIM), f32)], comm=comm)(cqkv, cqkv, cqkv, proj, alog_row, dt_row)


def dn_bwd(cqkv, proj, small_cb, alog_row, dt_row, ssave, do, ndn, name, comm=None):
    S = cqkv.shape[0]
    C = DN_CHUNK
    nc = S // C
    half = ndn * HEAD_DIM

    def body(q_ref, k_ref, v_ref, sm_ref, al_ref, dt_ref, ss_ref, do_ref, dqkv_ref, dsm_ref, dal_ref, ddt_ref, ds_ref):
        @pl.when(pl.program_id(0) == 0)
        def _():
            ds_ref[...] = jnp.zeros_like(ds_ref)
            dal_ref[...] = jnp.zeros_like(dal_ref)
            ddt_ref[...] = jnp.zeros_like(ddt_ref)

        sls = [slice(h * HEAD_DIM, (h + 1) * HEAD_DIM) for h in range(ndn)]
        _, vjp = jax.vjp(functools.partial(_dn_chunk, ndn), [q_ref[:, sl] for sl in sls], [k_ref[:, sl] for sl in sls],
                         [v_ref[:, sl] for sl in sls], sm_ref[...], al_ref[...], dt_ref[...], [ss_ref[h, 0] for h in range(ndn)])
        dq, dk, dv, dsm, dal, ddt, ds0 = vjp(([do_ref[:, sl] for sl in sls], [ds_ref[h] for h in range(ndn)]))
        for h in range(ndn):
            dqkv_ref[:, sls[h]] = dq[h]
            dqkv_ref[:, half + h * HEAD_DIM:half + (h + 1) * HEAD_DIM] = dk[h]
            dqkv_ref[:, 2 * half + h * HEAD_DIM:2 * half + (h + 1) * HEAD_DIM] = dv[h]
            ds_ref[h] = ds0[h]
        dsm_ref[...] = dsm
        dal_ref[...] += dal
        ddt_ref[...] += ddt

    rev = lambda n: nc - 1 - n
    blk = lambda cb: pl.BlockSpec((C, half), lambda n: (rev(n), cb))
    row = pl.BlockSpec((1, LANES), lambda n: (0, 0))
    return _call(body, name, (nc,),
                 [blk(0), blk(1), blk(2), pl.BlockSpec((C, LANES), lambda n: (rev(n), small_cb)), row, row,
                  pl.BlockSpec((ndn, 1, HEAD_DIM, HEAD_DIM), lambda n: (0, rev(n), 0, 0)), blk(0)],
                 (pl.BlockSpec((C, 3 * half), lambda n: (rev(n), 0)), pl.BlockSpec((C, LANES), lambda n: (rev(n), 0)), row, row),
                 (SDS((S, 3 * half), f32), SDS((S, LANES), f32), SDS((1, LANES), f32), SDS((1, LANES), f32)),
                 scratch=[pltpu.VMEM((ndn, HEAD_DIM, HEAD_DIM), f32)], comm=comm)(cqkv, cqkv, cqkv, proj, alog_row, dt_row, ssave, do)


def even_pre(proj, gq, gk, dfx, cb0, name):
    S = proj.shape[0]
    tm = _tile(S, 512)
    nh = dfx // HEAD_DIM

    def body(q_ref, k_ref, v_ref, gq_ref, gk_ref, fq_ref, fk_ref, fv_ref):
        for h in range(nh):
            sl = slice(h * HEAD_DIM, (h + 1) * HEAD_DIM)
            fq_ref[:, sl] = _rms(q_ref[:, sl], gq_ref[...]).astype(bf16)
            fk_ref[:, sl] = _rms(k_ref[:, sl], gk_ref[...]).astype(bf16)
        fv_ref[...] = v_ref[...].astype(bf16)

    sh = SDS((S, dfx), bf16)
    o = _rowspec(tm, dfx)
    return _call(body, name, (S // tm,),
                 [_rowspec(tm, dfx, cb0), _rowspec(tm, dfx, cb0 + 1), _rowspec(tm, dfx, cb0 + 2), _bspec(HEAD_DIM), _bspec(HEAD_DIM)],
                 (o, o, o), (sh, sh, sh))(proj, proj, proj, gq, gk)


def even_pre_bwd(proj, gq, gk, dfq, dfk, dfx, cb0, name):
    S = proj.shape[0]
    tm = _tile(S, 512)
    nh = dfx // HEAD_DIM

    def body(q_ref, k_ref, gq_ref, gk_ref, dfq_ref, dfk_ref, dq_ref, dk_ref, dgq_ref, dgk_ref):
        @pl.when(pl.program_id(0) == 0)
        def _():
            dgq_ref[...] = jnp.zeros_like(dgq_ref)
            dgk_ref[...] = jnp.zeros_like(dgk_ref)

        for h in range(nh):
            sl = slice(h * HEAD_DIM, (h + 1) * HEAD_DIM)
            _, vq = jax.vjp(_rms, q_ref[:, sl], gq_ref[...])
            dq, dgq = vq(dfq_ref[:, sl])
            _, vk = jax.vjp(_rms, k_ref[:, sl], gk_ref[...])
            dk, dgk = vk(dfk_ref[:, sl])
            dq_ref[:, sl] = dq
            dk_ref[:, sl] = dk
            dgq_ref[...] += dgq
            dgk_ref[...] += dgk

    sh = SDS((S, dfx), f32)
    o = _rowspec(tm, dfx)
    g = SDS((1, HEAD_DIM), f32)
    return _call(body, name, (S // tm,),
                 [_rowspec(tm, dfx, cb0), _rowspec(tm, dfx, cb0 + 1), _bspec(HEAD_DIM), _bspec(HEAD_DIM), o, o],
                 (o, o, _bspec(HEAD_DIM), _bspec(HEAD_DIM)), (sh, sh, g, g))(proj, proj, gq, gk, dfq, dfk)


def _dn_out(o, gate, gn):
    return _rms(o, gn) * (gate * jax.nn.sigmoid(gate))


def _fox_out(o, gate):
    return o * jax.nn.sigmoid(gate)


def even_post(o_dn, o_fox, proj, gn, half, cb_dn_gate, cb_fox_gate, name):
    S = proj.shape[0]
    tm = _tile(S, 512)
    nh = half // HEAD_DIM

    def body(od_ref, of_ref, gd_ref, gf_ref, gn_ref, o_ref):
        for h in range(nh):
            sl = slice(h * HEAD_DIM, (h + 1) * HEAD_DIM)
            o_ref[:, sl] = _dn_out(od_ref[:, sl], gd_ref[:, sl], gn_ref[...]).astype(bf16)
        o_ref[:, half:] = _fox_out(of_ref[...], gf_ref[...]).astype(bf16)

    return _call(body, name, (S // tm,),
                 [_rowspec(tm, half), _rowspec(tm, half), _rowspec(tm, half, cb_dn_gate), _rowspec(tm, half, cb_fox_gate), _bspec(HEAD_DIM)],
                 _rowspec(tm, 2 * half), SDS((S, 2 * half), bf16))(o_dn, o_fox, proj, proj, gn)


def even_post_bwd(o_dn, o_fox, proj, gn, dom, half, cb_dn_gate, cb_fox_gate, name):
    S = proj.shape[0]
    tm = _tile(S, 512)
    nh = half // HEAD_DIM

    def body(od_ref, of_ref, gd_ref, gf_ref, gn_ref, dod_ref, dof_ref, dd_ref, df_ref, dgd_ref, dgf_ref, dgn_ref):
        @pl.when(pl.program_id(0) == 0)
        def _():
            dgn_ref[...] = jnp.zeros_like(dgn_ref)

        for h in range(nh):
            sl = slice(h * HEAD_DIM, (h + 1) * HEAD_DIM)
            _, vjp = jax.vjp(_dn_out, od_ref[:, sl], gd_ref[:, sl], gn_ref[...])
            d_o, d_gate, d_gn = vjp(dod_ref[:, sl])
            dd_ref[:, sl] = d_o
            dgd_ref[:, sl] = d_gate
            dgn_ref[...] += d_gn
        _, vjp = jax.vjp(_fox_out, of_ref[...], gf_ref[...])
        d_o, d_gate = vjp(dof_ref[...])
        df_ref[...] = d_o
        dgf_ref[...] = d_gate

    sh = SDS((S, half), f32)
    o = _rowspec(tm, half)
    return _call(body, name, (S // tm,),
                 [o, o, _rowspec(tm, half, cb_dn_gate), _rowspec(tm, half, cb_fox_gate), _bspec(HEAD_DIM),
                  _rowspec(tm, half, 0), _rowspec(tm, half, 1)],
                 (o, o, o, o, _bspec(HEAD_DIM)), (sh, sh, sh, sh, SDS((1, HEAD_DIM), f32)))(o_dn, o_fox, proj, proj, gn, dom, dom)


def _pad_row(v, lane0=0):
    return jnp.pad(v, (lane0, LANES - lane0 - v.shape[0]))[None]


def _carried(res, comm):
    return res if comm is not None else (res, [])


def even_mixer_fwd(x, gmix, w_in, w_out, conv_w, a_log, dt_bias, gn, gq, gk, f_bias, tag, comm_fox=None, comm_dn=None,
                   comm_in=None):
    S, D = x.shape
    half = D // 2
    ndn = half // HEAD_DIM
    small_cb = 4 * D // LANES
    alog_row, dt_row, fb_row = _pad_row(a_log), _pad_row(dt_bias), _pad_row(f_bias, 2 * ndn)
    h = rms_fwd(x, gmix, f"{tag}_rms")
    proj, got_in = _carried(mm(h, w_in, "nn", f32, f"{tag}_in", tm=1024, tn=1408, comm=comm_in), comm_in)
    cqkv = conv_fwd(proj, conv_w, 3 * half, f"{tag}_conv")
    (o_dn, ssave), got_dn = _carried(dn_fwd(cqkv, proj, small_cb, alog_row, dt_row, ndn, f"{tag}_dn", comm=comm_dn), comm_dn)
    c, cT = gates_fwd(proj, small_cb, fb_row, f"{tag}_gates")
    fq, fk, fv = even_pre(proj, gq, gk, half, 4, f"{tag}_pre")
    (o_fox, lse), got_fox = _carried(fox_fwd(fq, fk, fv, c, cT, ndn, 2 * ndn, f"{tag}_fox", comm=comm_fox), comm_fox)
    om = even_post(o_dn, o_fox, proj, gn, half, 3, 7, f"{tag}_post")
    xo = mm(om, w_out, "nn", f32, f"{tag}_out", tn=1024, res=x)
    return xo, (h, proj, cqkv, o_dn, ssave, c, cT, fq, fk, fv, o_fox, lse, om, alog_row, dt_row, fb_row), got_fox, got_dn, got_in


def even_mixer_bwd(x, gmix, w_in, w_out, conv_w, gn, gq, gk, saved, dy, tag, comm_fox=None, comm_dn=None, comm_bdh=None):
    S, D = x.shape
    half = D // 2
    ndn = half // HEAD_DIM
    small_cb = 4 * D // LANES
    h, proj, cqkv, o_dn, ssave, c, cT, fq, fk, fv, o_fox, lse, om, alog_row, dt_row, fb_row = saved
    dy32, dy16 = dy
    dom = mm(dy16, w_out, "nt", f32, f"{tag}_bdom", tn=1024)
    dw_out = mm(om, dy16, "tn", bf16, f"{tag}_bwout", tn=1024)
    d_odn, d_ofox, d_dgate, d_fgate, d_gn = even_post_bwd(o_dn, o_fox, proj, gn, dom, half, 3, 7, f"{tag}_bpost")
    (dfq, dfk, dfv, dcT), got_fox = _carried(
        fox_bwd(fq, fk, fv, c, cT, o_fox, lse, d_ofox, ndn, 2 * ndn, f"{tag}_bfox", comm=comm_fox), comm_fox)
    dq_pre, dk_pre, d_gq, d_gk = even_pre_bwd(proj, gq, gk, dfq, dfk, half, 4, f"{tag}_bpre")
    (dcqkv, dsm_dn, d_alog, d_dt), got_dn = _carried(
        dn_bwd(cqkv, proj, small_cb, alog_row, dt_row, ssave, d_odn, ndn, f"{tag}_bdn", comm=comm_dn), comm_dn)
    d_conv_in, d_conv_w = conv_bwd(proj, conv_w, dcqkv, f"{tag}_bconv")
    d_small, d_fb = gates_bwd(proj, small_cb, fb_row, dcT, dsm_dn, 2 * ndn, ndn, f"{tag}_bgates")
    dproj = jnp.concatenate([d_conv_in, d_dgate, dq_pre, dk_pre, dfv, d_fgate, d_small], axis=1).astype(bf16)
    dw_in = mm(h, dproj, "tn", bf16, f"{tag}_bwin", tn=1408)
    res = mm_bdh_rms(dproj, w_in, x, gmix, dy32, f"{tag}_bdh", comm=comm_bdh)
    dx, d_gmix = res[:2]
    got_bdh = res[2] if comm_bdh is not None else []
    small = dict(norm_mix=d_gmix[0], dn_conv_w=d_conv_w, dn_a_log=d_alog[0, :ndn], dn_dt_bias=d_dt[0, :ndn],
                 dn_norm_g=d_gn[0], fox_q_norm_g=d_gq[0], fox_k_norm_g=d_gk[0], fox_f_bias=d_fb[0, 2 * ndn:3 * ndn])
    return dx, dw_in, dw_out, small, got_fox, got_dn, got_bdh


def odd_mixer_fwd(x, gmix, w_in, w_out, tag, comm=None):
    S, D = x.shape
    nh = D // HEAD_DIM
    h = rms_fwd(x, gmix, f"{tag}_rms")
    qkv = mm(h, w_in, "nn", bf16, f"{tag}_in", tm=1024, tn=768)
    (o, tails), got = _carried(sb_fwd(qkv, nh, f"{tag}_sb", comm=comm), comm)
    xo = mm(o, w_out, "nn", f32, f"{tag}_out", tn=1024, res=x)
    return xo, (h, qkv, o, tails), got


def odd_mixer_bwd(x, gmix, w_in, w_out, saved, dy, tag, comm=None):
    S, D = x.shape
    nh = D // HEAD_DIM
    h, qkv, o, tails = saved
    dy32, dy16 = dy
    do = mm(dy16, w_out, "nt", f32, f"{tag}_bdo", tn=1024)
    dw_out = mm(o, dy16, "tn", bf16, f"{tag}_bwout", tn=1024)
    (dq, dk, dv), got = _carried(sb_bwd(qkv, tails, do, nh, f"{tag}_bsb", comm=comm), comm)
    dqkv = jnp.concatenate([dq, dk, dv], axis=1).astype(bf16)
    dw_in = mm(h, dqkv, "tn", bf16, f"{tag}_bwin", tn=1536)
    dx, d_gmix = mm_bdh_rms(dqkv, w_in, x, gmix, dy32, f"{tag}_bdh")
    return dx, dw_in, dw_out, dict(norm_mix=d_gmix[0]), got


def all_gather(shards, axes, name, kind="ag"):
    return _call(None, name, (), [], [], [], comm=Comm(kind, shards, axes))()[1]


def sum_parts(parts, name):
    _, R, C = parts.shape
    tm = _tile(R, 512)

    def body(p_ref, o_ref):
        acc = p_ref[0].astype(f32)
        for k in range(1, NDEV):
            acc = acc + p_ref[k].astype(f32)
        o_ref[...] = acc

    return _call(body, name, (R // tm,), [pl.BlockSpec((NDEV, tm, C), lambda i: (0, i, 0))], _rowspec(tm, C), SDS((R, C), f32))(parts)


def adamw(w, g, m, v, name):
    L, R, C = w.shape
    tm = _tile(R, max(8, min(512, (ADAMW_TILE_ELEMS // C) // 8 * 8)))
    c1 = 1.0 - ADAM_B1 ** ADAM_STEP
    c2 = 1.0 - ADAM_B2 ** ADAM_STEP

    def body(w_ref, g_ref, m_ref, v_ref, d_ref, nm_ref, nv_ref):
        g_ = g_ref[...]
        nm = ADAM_B1 * m_ref[...] + (1.0 - ADAM_B1) * g_
        nv = ADAM_B2 * v_ref[...] + (1.0 - ADAM_B2) * (g_ * g_)
        d_ref[...] = -ADAM_LR * ((nm / c1) / (jnp.sqrt(nv / c2) + ADAM_EPS) + ADAM_WD * w_ref[...])
        nm_ref[...] = nm
        nv_ref[...] = nv

    spec = pl.BlockSpec((None, tm, C), lambda l, i: (l, i, 0))
    sh = SDS((L, R, C), f32)
    return _call(body, name, (L, R // tm), [spec] * 4, (spec, spec, spec), (sh, sh, sh))(w, g, m, v)


def _pad_to(n, mult):
    return -(-n // mult) * mult


def _even_perm(D):
    half = D // 2
    ndn = half // HEAD_DIM
    o_gate = 3 * half
    o_b = o_gate + half
    o_a = o_b + ndn
    o_fq = o_a + ndn
    o_fpre = o_fq + 4 * half
    return half, ndn, o_b, o_a, o_fq, o_fpre


def _even_to_mine(w):
    D = (w.shape[-1] // 4) // LANES * LANES
    half, ndn, o_b, o_a, o_fq, o_fpre = _even_perm(D)
    small = jnp.concatenate([w[..., o_b:o_b + ndn], w[..., o_a:o_a + ndn], w[..., o_fpre:o_fpre + ndn]], axis=-1)
    small = jnp.pad(small, [(0, 0)] * (w.ndim - 1) + [(0, LANES - 3 * ndn)])
    return jnp.concatenate([w[..., :o_b], w[..., o_fq:o_fpre], small], axis=-1)


def _even_from_mine(w, D):
    half, ndn, o_b, o_a, o_fq, o_fpre = _even_perm(D)
    sm = w[..., 4 * D:]
    return jnp.concatenate([w[..., :o_b], sm[..., :ndn], sm[..., ndn:2 * ndn], w[..., o_b:4 * D], sm[..., 2 * ndn:3 * ndn]], axis=-1)


def _pack_small(parts):
    flat = jnp.concatenate([p.reshape(-1).astype(f32) for p in parts])
    n = flat.shape[0]
    return jnp.pad(flat, (0, _pad_to(n, 8 * LANES) - n)).reshape(-1, LANES)


def _unpack_small(packed, like):
    flat = packed.reshape(-1)
    out, off = [], 0
    for p in like:
        out.append(flat[off:off + p.size].reshape(p.shape))
        off += p.size
    return out


SMALL_NAMES = ("norm_ffn1", "norm_mix", "dn_a_log", "dn_dt_bias", "dn_norm_g", "fox_q_norm_g", "fox_k_norm_g", "fox_f_bias", "norm_ffn2")
BIG_NAMES = ("ffn1_w_gu", "ffn1_w_down", "w_in_even", "dn_conv_w", "w_out_even", "w_in_odd", "w_out_odd", "ffn2_w_gu", "ffn2_w_down")
WEIGHT_NAMES = ("norm_ffn1", "ffn1_w_gu", "ffn1_w_down", "norm_mix", "w_in_even", "dn_conv_w", "dn_a_log", "dn_dt_bias", "dn_norm_g",
                "fox_q_norm_g", "fox_k_norm_g", "fox_f_bias", "w_out_even", "w_in_odd", "w_out_odd", "norm_ffn2", "ffn2_w_gu", "ffn2_w_down")


def kernel(x, norm_ffn1, ffn1_w_gu, ffn1_w_down, norm_mix, w_in_even, dn_conv_w, dn_a_log, dn_dt_bias, dn_norm_g, fox_q_norm_g, fox_k_norm_g, fox_f_bias, w_out_even, w_in_odd, w_out_odd, norm_ffn2, ffn2_w_gu, ffn2_w_down, loss_target, m_norm_ffn1, m_ffn1_w_gu, m_ffn1_w_down, m_norm_mix, m_w_in_even, m_dn_conv_w, m_dn_a_log, m_dn_dt_bias, m_dn_norm_g, m_fox_q_norm_g, m_fox_k_norm_g, m_fox_f_bias, m_w_out_even, m_w_in_odd, m_w_out_odd, m_norm_ffn2, m_ffn2_w_gu, m_ffn2_w_down, v_norm_ffn1, v_ffn1_w_gu, v_ffn1_w_down, v_norm_mix, v_w_in_even, v_dn_conv_w, v_dn_a_log, v_dn_dt_bias, v_dn_norm_g, v_fox_q_norm_g, v_fox_k_norm_g, v_fox_f_bias, v_w_out_even, v_w_in_odd, v_w_out_odd, v_norm_ffn2, v_ffn2_w_gu, v_ffn2_w_down):
    args = dict(locals())
    W = {n: args[n] for n in WEIGHT_NAMES}
    M = {n: args["m_" + n] for n in WEIGHT_NAMES}
    V = {n: args["v_" + n] for n in WEIGHT_NAMES}
    xs = x[0]
    tgt = loss_target[0]
    S, D = xs.shape
    L = norm_ffn1.shape[0]
    n_even = w_in_even.shape[0]
    hs = ffn1_w_down.shape[1]
    hp = _pad_to(hs, LANES)
    me = 4 * lax.axis_index("x") + 2 * lax.axis_index("y") + lax.axis_index("c")

    def prep_gu(w):
        w = w.reshape(L, D, 2, hs)
        return jnp.pad(w, ((0, 0), (0, 0), (0, 0), (0, hp - hs))).reshape(L, D, 2 * hp).astype(bf16)

    def prep_down(w):
        return jnp.pad(w, ((0, 0), (0, hp - hs), (0, 0))).astype(bf16)

    sh_gu1, sh_d1, sh_gu2, sh_d2 = prep_gu(ffn1_w_gu), prep_down(ffn1_w_down), prep_gu(ffn2_w_gu), prep_down(ffn2_w_down)
    sh_ie, sh_oe = _even_to_mine(w_in_even).astype(bf16), w_out_even.astype(bf16)
    sh_io, sh_oo = w_in_odd.astype(bf16), w_out_odd.astype(bf16)

    def layer_shards(l):
        j = l // 2
        mix = [sh_ie[j], sh_oe[j]] if l % 2 == 0 else [sh_io[j], sh_oo[j]]
        return [sh_gu1[l], sh_d1[l], *mix, sh_gu2[l], sh_d2[l]]

    def layer_axes(l):
        return [1, 0, 0 if l % 2 == 0 else 1, 0, 1, 0]

    def layer_names(l):
        return ["ffn1_w_gu", "ffn1_w_down", *(["w_in_even", "w_out_even"] if l % 2 == 0 else ["w_in_odd", "w_out_odd"]),
                "ffn2_w_gu", "ffn2_w_down"]

    FOX_CARRIES, DN_CARRIES = (0, 1, 3), (2, 4, 5)

    def split_comm(kind, arrays, axes):
        return (Comm(kind, [arrays[i] for i in FOX_CARRIES], [axes[i] for i in FOX_CARRIES]),
                Comm(kind, [arrays[i] for i in DN_CARRIES], [axes[i] for i in DN_CARRIES]))

    def join_split(got_fox, got_dn):
        out = [None] * 6
        for i, a in zip(FOX_CARRIES, got_fox):
            out[i] = a
        for i, a in zip(DN_CARRIES, got_dn):
            out[i] = a
        return out

    sh0, ax0 = layer_shards(0), layer_axes(0)
    first = all_gather(sh0[:2] + [dn_conv_w[None]], ax0[:2] + [0], "ag_layer0", kind="ag2")
    weights = {0: first[:2] + [None] * 4}
    convw = jnp.moveaxis(first[2], 0, 2).reshape(n_even, CONV_WIDTH, -1)
    LAYER0_CARRIES = dict(f1_gu=(2,), f1_down=(3,), mx_in=(5,))

    EVEN_FWD_CARRIES = dict(f1_gu=(), mx_in=(), dn=(1, 2, 5), fox=(0, 4, 3), f2_gu=(), f2_down=())
    saved = []
    cur = xs
    for l in range(L):
        j = l // 2
        wgu1, wd1, win, wout, wgu2, wd2 = weights[l]
        nxt = l + 1 < L
        x0 = cur
        if l % 2 == 0:
            sh_n, ax_n = (layer_shards(l + 1), layer_axes(l + 1)) if nxt else (None, None)
            cm = {k: (Comm("ag2", [sh_n[i] for i in idx], [ax_n[i] for i in idx]) if nxt and idx else None)
                  for k, idx in EVEN_FWD_CARRIES.items()}
            if l == 0:
                cm.update({k: Comm("ag2", [sh0[i] for i in idx], [ax0[i] for i in idx]) for k, idx in LAYER0_CARRIES.items()})
                cm["fox"] = Comm("ag2", cm["fox"].arrays + [sh0[4]], cm["fox"].axes + [ax0[4]])
            x1, s1, got_f1gu, got_f1down = ffn_fwd(x0, norm_ffn1[l:l + 1], wgu1, wd1, f"l{l}f1", comm_gu=cm["f1_gu"],
                                                   comm_down=cm["f1_down"] if l == 0 else None)
            if l == 0:
                win, wout = got_f1gu[0], got_f1down[0]
            x2, sm, got_f, got_d, got_in = even_mixer_fwd(
                x1, norm_mix[l:l + 1], win, wout, convw[j], dn_a_log[j], dn_dt_bias[j], dn_norm_g[j:j + 1],
                fox_q_norm_g[j:j + 1], fox_k_norm_g[j:j + 1], fox_f_bias[j], f"l{l}mx", comm_fox=cm["fox"], comm_dn=cm["dn"],
                comm_in=cm["mx_in"])
            if l == 0:
                wd2, wgu2 = got_in[0], got_f[-1]
                weights[0] = [wgu1, wd1, win, wout, wgu2, wd2]
                got_f1gu, got_in, got_f = [], [], got_f[:-1]
            x3, s2, got_f2gu, got_f2down = ffn_fwd(x2, norm_ffn2[l:l + 1], wgu2, wd2, f"l{l}f2", comm_gu=cm["f2_gu"],
                                                   comm_down=cm["f2_down"])
            if nxt:
                got = dict(f1_gu=got_f1gu, mx_in=got_in, dn=got_d, fox=got_f, f2_gu=got_f2gu, f2_down=got_f2down)
                weights[l + 1] = [None] * 6
                for k, idx in EVEN_FWD_CARRIES.items():
                    for i, a in zip(idx, got[k]):
                        weights[l + 1][i] = a
        else:
            x1, s1, _, _ = ffn_fwd(x0, norm_ffn1[l:l + 1], wgu1, wd1, f"l{l}f1")
            cm = Comm("ag2", layer_shards(l + 1), layer_axes(l + 1)) if nxt else None
            x2, sm, got = odd_mixer_fwd(x1, norm_mix[l:l + 1], win, wout, f"l{l}mx", comm=cm)
            if nxt:
                weights[l + 1] = got
            x3, s2, _, _ = ffn_fwd(x2, norm_ffn2[l:l + 1], wgu2, wd2, f"l{l}f2")
        saved.append((x0, x1, x2, s1, sm, s2))
        cur = x3
    dy, loss_row = loss_head(cur, tgt, "loss")

    gsmall = {n: [None] * W[n].shape[0] for n in SMALL_NAMES}
    gconv = [None] * n_even
    parts = {n: [None] * W[n].shape[0] for n in BIG_NAMES if n != "dn_conv_w"}

    def take(l, recv):
        for nm, p in zip(layer_names(l), recv):
            idx = l if nm.startswith("ffn") else l // 2
            parts[nm][idx] = sum_parts(p.reshape(NDEV, -1, p.shape[-1]), f"sum_{nm}_{idx}").reshape(p.shape[1:])

    pending = None
    for l in reversed(range(L)):
        j = l // 2
        wgu1, wd1, win, wout, wgu2, wd2 = weights[l]
        x0, x1, x2, s1, sm, s2 = saved[l]
        dy, dg, dwgu2, dwd2 = ffn_bwd(x2, norm_ffn2[l:l + 1], wgu2, wd2, s2, dy, f"l{l}f2")
        gsmall["norm_ffn2"][l] = dg[0]
        if l % 2 == 0:
            cf, cd = split_comm("rs", pending, layer_axes(l + 1)) if pending is not None else (None, None)
            cb = None
            if l == 0:
                cd = Comm("rs", cd.arrays + [dwd2], cd.axes + [ax0[5]])
                cb = Comm("rs", [dwgu2], [ax0[4]])
            dy, dwin, dwout, sg, got_f, got_d, got_b = even_mixer_bwd(
                x1, norm_mix[l:l + 1], win, wout, convw[j], dn_norm_g[j:j + 1], fox_q_norm_g[j:j + 1],
                fox_k_norm_g[j:j + 1], sm, dy, f"l{l}mx", comm_fox=cf, comm_dn=cd, comm_bdh=cb)
            if l == 0:
                got_wd2, got_wgu2, got_d = got_d[-1], got_b[0], got_d[:-1]
            if pending is not None:
                take(l + 1, join_split(got_f, got_d))
            gconv[j] = sg.pop("dn_conv_w")
            for k_, v_ in sg.items():
                gsmall[k_][l if k_ == "norm_mix" else j] = v_
        else:
            cm = Comm("rs", pending, layer_axes(l + 1)) if pending is not None else None
            dy, dwin, dwout, sg, got = odd_mixer_bwd(x1, norm_mix[l:l + 1], win, wout, sm, dy, f"l{l}mx", comm=cm)
            if pending is not None:
                take(l + 1, got)
            gsmall["norm_mix"][l] = sg["norm_mix"]
        if l > 0:
            dy, dg, dwgu1, dwd1 = ffn_bwd(x0, norm_ffn1[l:l + 1], wgu1, wd1, s1, dy, f"l{l}f1")
        else:
            rs = lambda a, ax: Comm("rs", [a], [ax])
            dy, dg, dwgu1, dwd1, got0 = ffn_bwd(x0, norm_ffn1[l:l + 1], wgu1, wd1, s1, dy, f"l{l}f1",
                                                comms=dict(bda=rs(dwin, ax0[2]), bwd=rs(dwout, ax0[3])), own_axes=(ax0[0], ax0[1]))
        gsmall["norm_ffn1"][l] = dg[0]
        pending = [dwgu1, dwd1, dwin, dwout, dwgu2, dwd2]
    take(0, [got0["wgu"][0], got0["wd"][0], got0["bda"][0], got0["bwd"][0], got_wgu2, got_wd2])
    grad_x = dy[0][None]

    small_list = [jnp.stack(gsmall[n]) for n in SMALL_NAMES] + [jnp.stack(gconv), loss_row[0, :1]]
    packed = _pack_small(small_list)
    gathered = all_gather([packed], [0], "ag_small")[0]
    summed = sum_parts(gathered.reshape(NDEV, packed.shape[0], LANES), "sum_small")
    small_sum = _unpack_small(summed, small_list)
    G = dict(zip(SMALL_NAMES, small_sum[:len(SMALL_NAMES)]))
    conv_full = small_sum[len(SMALL_NAMES)]
    cwid = dn_conv_w.shape[2]
    G["dn_conv_w"] = lax.dynamic_slice_in_dim(conv_full, me * cwid, cwid, axis=2)
    loss = small_sum[-1][0]

    def unprep_gu(g):
        return g.reshape(L, D, 2, hp)[..., :hs].reshape(L, D, 2 * hs)

    G["ffn1_w_gu"] = unprep_gu(jnp.stack(parts["ffn1_w_gu"]))
    G["ffn2_w_gu"] = unprep_gu(jnp.stack(parts["ffn2_w_gu"]))
    G["ffn1_w_down"] = jnp.stack(parts["ffn1_w_down"])[:, :hs]
    G["ffn2_w_down"] = jnp.stack(parts["ffn2_w_down"])[:, :hs]
    G["w_in_even"] = _even_from_mine(jnp.stack(parts["w_in_even"]), D)
    G["w_out_even"] = jnp.stack(parts["w_out_even"])
    G["w_in_odd"] = jnp.stack(parts["w_in_odd"])
    G["w_out_odd"] = jnp.stack(parts["w_out_odd"])

    delta, new_m, new_v = {}, {}, {}
    for n in BIG_NAMES:
        t = (lambda a: jnp.swapaxes(a, 1, 2)) if n.endswith("w_gu") else (lambda a: a)
        delta[n], new_m[n], new_v[n] = map(t, adamw(t(W[n]), t(G[n]), t(M[n]), t(V[n]), f"adamw_{n}"))
    sw = [W[n] for n in SMALL_NAMES]
    d_, m_, v_ = adamw(_pack_small(sw)[None], _pack_small([G[n] for n in SMALL_NAMES])[None],
                       _pack_small([M[n] for n in SMALL_NAMES])[None], _pack_small([V[n] for n in SMALL_NAMES])[None], "adamw_small")
    for n, a, b, c_ in zip(SMALL_NAMES, _unpack_small(d_, sw), _unpack_small(m_, sw), _unpack_small(v_, sw)):
        delta[n], new_m[n], new_v[n] = a, b, c_

    return (loss, grad_x, *[G[n] for n in WEIGHT_NAMES], *[delta[n] for n in WEIGHT_NAMES],
            *[new_m[n] for n in WEIGHT_NAMES], *[new_v[n] for n in WEIGHT_NAMES])
```

```python
import functools
import math

import jax
import jax.numpy as jnp
from jax import lax
from jax.experimental import pallas as pl
from jax.experimental.pallas import tpu as pltpu

f32 = jnp.float32
bf16 = jnp.bfloat16
SDS = jax.ShapeDtypeStruct

HEAD_DIM = 128
LANES = 128
CONV_WIDTH = 4
DN_CHUNK = 128
EPS = 1e-6
NDEV = 8
VMEM_LIMIT_BYTES = 56 * 1024 * 1024
HI = lax.Precision.HIGHEST
ADAMW_TILE_ELEMS = 384 * 1024

ADAM_LR = 0.001
ADAM_B1 = 0.9
ADAM_B2 = 0.999
ADAM_EPS = 1e-08
ADAM_WD = 0.01
ADAM_STEP = 10

NN = (((1,), (0,)), ((), ()))
NT = (((1,), (1,)), ((), ()))
TN = (((0,), (0,)), ((), ()))


def _me_and_peers():
    x, y, c = lax.axis_index("x"), lax.axis_index("y"), lax.axis_index("c")
    me = 4 * x + 2 * y + c
    peers = []
    for r in range(1, NDEV):
        px = 1 - x if (r >> 2) & 1 else x
        py = 1 - y if (r >> 1) & 1 else y
        pc = 1 - c if r & 1 else c
        peers.append(((px, py, pc), 4 * px + 2 * py + pc))
    return me, peers


def _slab(ref, axis, width, k):
    idx = [slice(None)] * len(ref.shape)
    idx[axis] = pl.ds(k * width, width)
    return ref.at[tuple(idx)]


class Comm:
    def __init__(self, kind, arrays, axes):
        self.kind, self.arrays, self.axes, self.n = kind, list(arrays), list(axes), len(arrays)

    def out_shape(self):
        out = []
        for a, ax in zip(self.arrays, self.axes):
            shp = list(a.shape)
            if self.kind != "rs":
                shp[ax] *= NDEV
                out.append(SDS(tuple(shp), a.dtype))
            else:
                shp[ax] //= NDEV
                out.append(SDS((NDEV, *shp), a.dtype))
        return out

    def sems(self):
        return [pltpu.SemaphoreType.DMA((self.n, NDEV - 1)), pltpu.SemaphoreType.DMA((self.n, NDEV - 1)),
                pltpu.SemaphoreType.DMA((self.n,))]

    def _src(self, ins, t, to_idx):
        if self.kind == "ag":
            return ins[t]
        return _slab(ins[t], self.axes[t], ins[t].shape[self.axes[t]] // NDEV, to_idx)

    def _dst(self, ins, outs, t, from_idx):
        if self.kind != "rs":
            return _slab(outs[t], self.axes[t], ins[t].shape[self.axes[t]], from_idx)
        return outs[t].at[from_idx]

    def _copies(self, ins, outs, sems, sending):
        send_sems, recv_sems, loc_sems = sems
        me, peers = _me_and_peers()
        local, remote = [], []
        for t in range(self.n):
            local.append(pltpu.make_async_copy(self._src(ins, t, me), self._dst(ins, outs, t, me), loc_sems.at[t]))
            for r, (peer, pidx) in enumerate(peers):
                remote.append(pltpu.make_async_remote_copy(
                    src_ref=self._src(ins, t, pidx), dst_ref=self._dst(ins, outs, t, me if sending else pidx),
                    send_sem=send_sems.at[t, r], recv_sem=recv_sems.at[t, r], device_id=peer,
                    device_id_type=pl.DeviceIdType.MESH))
        return local, remote

    def _two_level(self, ins, outs, sems, own=True):
        send_sems, recv_sems, loc_sems = sems
        x, y, c = lax.axis_index("x"), lax.axis_index("y"), lax.axis_index("c")
        me, sibling = (x, y, c), (x, y, 1 - c)
        chips = [(1 - x, y), (x, 1 - y), (1 - x, 1 - y)]
        dev = lambda p: 4 * p[0] + 2 * p[1] + p[2]

        def copy(t, k, block, to, from_shard):
            dst = self._dst(ins, outs, t, dev(block))
            return pltpu.make_async_remote_copy(
                src_ref=ins[t] if from_shard else dst, dst_ref=dst, send_sem=send_sems.at[t, k], recv_sem=recv_sems.at[t, k],
                device_id=to, device_id_type=pl.DeviceIdType.MESH)

        ts = range(self.n) if own else ()
        local = [pltpu.make_async_copy(ins[t], self._dst(ins, outs, t, dev(me)), loc_sems.at[t]) for t in ts]
        first = [copy(t, 0, me, sibling, True) for t in ts]
        first += [copy(t, 1 + j, me, (*chip, c), True) for t in ts for j, chip in enumerate(chips)]
        return local, first, copy, chips, me, sibling, c

    def start(self, ins, outs, sems):
        if self.kind == "ag2":
            local, first = self._two_level(ins, outs, sems)[:2]
            for cp in local + first:
                cp.start()
            return
        local, remote = self._copies(ins, outs, sems, True)
        for cp in local + remote:
            cp.start()

    def relay(self, ins, outs, sems):
        _, _, copy, chips, me, sibling, c = self._two_level(ins, outs, sems, own=False)
        for t in range(self.n):
            for j, chip in enumerate(chips):
                copy(t, 1 + j, (*chip, c), me, False).wait_recv()
                copy(t, 4 + j, (*chip, c), sibling, False).start()

    def wait(self, ins, outs, sems, relayed=False):
        if self.kind == "ag2":
            if not relayed:
                self.relay(ins, outs, sems)
            local, first, copy, chips, me, sibling, c = self._two_level(ins, outs, sems)
            passed = [copy(t, 4 + j, (*chip, c), sibling, False) for t in range(self.n) for j, chip in enumerate(chips)]
            for t in range(self.n):
                copy(t, 0, sibling, me, False).wait_recv()
                for j, chip in enumerate(chips):
                    copy(t, 4 + j, (*chip, 1 - c), me, False).wait_recv()
            for cp in first + passed:
                cp.wait_send()
            for cp in local:
                cp.wait()
            return
        local, remote = self._copies(ins, outs, sems, False)
        for cp in remote:
            cp.wait_recv()
            cp.wait_send()
        for cp in local:
            cp.wait()


def _call(body, name, grid, in_specs, out_specs, out_shape, scratch=(), comm=None):
    params = pltpu.CompilerParams(vmem_limit_bytes=VMEM_LIMIT_BYTES)
    if comm is None:
        return pl.pallas_call(body, name=name, grid=grid, in_specs=in_specs, out_specs=out_specs, out_shape=out_shape,
                              scratch_shapes=list(scratch), compiler_params=params)
    single = not isinstance(out_shape, (tuple, list))
    c_out_specs = [out_specs] if single else list(out_specs)
    c_out_shape = [out_shape] if single else list(out_shape)
    n_in, n_out, n_scr, n = len(in_specs), len(c_out_shape), len(scratch), comm.n
    anyspec = pl.BlockSpec(memory_space=pl.ANY)

    def wrapped(*refs):
        ins, refs = refs[:n_in], refs[n_in:]
        cins, refs = refs[:n], refs[n:]
        outs, refs = refs[:n_out], refs[n_out:]
        couts, refs = refs[:n], refs[n:]
        scr, sems = refs[:n_scr], refs[n_scr:]
        first = functools.reduce(lambda a, b: a & b, [pl.program_id(d) == 0 for d in range(len(grid))], True)
        last = functools.reduce(lambda a, b: a & b, [pl.program_id(d) == grid[d] - 1 for d in range(len(grid))], True)
        if grid:
            steps = math.prod(grid)
            step = functools.reduce(lambda a, d: a * grid[d] + pl.program_id(d), range(len(grid)), 0)
            relay_early = comm.kind == "ag2" and steps >= 4
            pl.when(first)(lambda: comm.start(cins, couts, sems))
            body(*ins, *outs, *scr)
            if relay_early:
                pl.when(step == (3 * steps) // 4)(lambda: comm.relay(cins, couts, sems))
            pl.when(last)(lambda: comm.wait(cins, couts, sems, relayed=relay_early))
        else:
            comm.start(cins, couts, sems)
            if body is not None:
                body(*ins, *outs, *scr)
            comm.wait(cins, couts, sems)

    call = pl.pallas_call(
        wrapped, name=name, grid=grid, in_specs=list(in_specs) + [anyspec] * n, out_specs=c_out_specs + [anyspec] * n,
        out_shape=c_out_shape + comm.out_shape(), scratch_shapes=list(scratch) + comm.sems(), compiler_params=params)

    def run(*args):
        res = call(*args, *comm.arrays)
        mine = res[:n_out]
        return (mine[0] if single else tuple(mine)), list(res[n_out:])

    return run


def _tile(n, want, align=8):
    if n <= want:
        return n
    for t in range(want // align * align, 0, -align):
        if n % t == 0:
            return t
    return n


def _dot(a, b, dims=NN, precision=None):
    return lax.dot_general(a, b, dims, preferred_element_type=f32, precision=precision)


def _logsig(x):
    return jnp.minimum(x, 0.0) - jnp.log(1.0 + jnp.exp(-jnp.abs(x)))


def _softplus(x):
    return jnp.maximum(x, 0.0) + jnp.log(1.0 + jnp.exp(-jnp.abs(x)))


def _rms(x, g):
    return x * lax.rsqrt(jnp.mean(x * x, axis=-1, keepdims=True) + EPS) * g


def _lane_pick(blk, lane_idx):
    lane = lax.broadcasted_iota(jnp.int32, (1, LANES), 1)
    return jnp.sum(jnp.where(lane == lane_idx, blk, 0.0), axis=1, keepdims=True)


def mm(a, b, mode, out_dtype, name, tm=512, tn=512, res=None, scale=1.0, comm=None):
    if mode == "nn":
        (M, K), (_, N) = a.shape, b.shape
    elif mode == "nt":
        (M, K), (N, _) = a.shape, b.shape
    else:
        (K, M), (_, N) = a.shape, b.shape
    tm, tn = _tile(M, tm, LANES), _tile(N, tn, LANES)
    a_spec = pl.BlockSpec((K, tm), lambda j, i: (0, i)) if mode == "tn" else pl.BlockSpec((tm, K), lambda j, i: (i, 0))
    b_spec = pl.BlockSpec((tn, K), lambda j, i: (j, 0)) if mode == "nt" else pl.BlockSpec((K, tn), lambda j, i: (0, j))
    dims = {"nn": NN, "nt": NT, "tn": TN}[mode]
    o_spec = pl.BlockSpec((tm, tn), lambda j, i: (i, j))

    def body(*refs):
        acc = _dot(refs[0][...].astype(bf16), refs[1][...].astype(bf16), dims)
        if scale != 1.0:
            acc = acc * scale
        if res is not None:
            acc = acc + refs[2][...]
        refs[-1][...] = acc.astype(out_dtype)

    ins, specs = [a, b], [a_spec, b_spec]
    if res is not None:
        ins.append(res)
        specs.append(o_spec)
    return _call(body, name, (N // tn, M // tm), specs, o_spec, SDS((M, N), out_dtype), comm=comm)(*ins)


def _rowspec(tm, w, cb=0):
    return pl.BlockSpec((tm, w), lambda i: (i, cb))


def _bspec(w):
    return pl.BlockSpec((1, w), lambda i: (0, 0))


def rms_fwd(x, g, name):
    S, D = x.shape
    tm = _tile(S, 512)

    def body(x_ref, g_ref, h_ref):
        h_ref[...] = _rms(x_ref[...], g_ref[...]).astype(bf16)

    return _call(body, name, (S // tm,), [_rowspec(tm, D), _bspec(D)], _rowspec(tm, D), SDS((S, D), bf16))(x, g)


def _swiglu(g, u):
    return g * jax.nn.sigmoid(g) * u


def ffn_gu_act(h, wgu, name, tm=1024, tn=768, comm=None):
    S, D = h.shape
    W = wgu.shape[1] // 2
    tm, tn = _tile(S, tm, LANES), _tile(W, tn, LANES)
    nb = W // tn

    def body(h_ref, wg_ref, wu_ref, gu_ref, a_ref):
        hb = h_ref[...]
        g = _dot(hb, wg_ref[...])
        u = _dot(hb, wu_ref[...])
        gu_ref[0] = g.astype(bf16)
        gu_ref[1] = u.astype(bf16)
        a_ref[...] = _swiglu(g, u).astype(bf16)

    return _call(body, name, (nb, S // tm),
                 [pl.BlockSpec((tm, D), lambda j, i: (i, 0)), pl.BlockSpec((D, tn), lambda j, i: (0, j)),
                  pl.BlockSpec((D, tn), lambda j, i: (0, nb + j))],
                 (pl.BlockSpec((2, tm, tn), lambda j, i: (0, i, j)), pl.BlockSpec((tm, tn), lambda j, i: (i, j))),
                 (SDS((2, S, W), bf16), SDS((S, W), bf16)), comm=comm)(h, wgu, wgu)


def ffn_bda_act(dy, wd, gu, scale, name, tm=1024, tn=768, comm=None):
    S, D = dy.shape
    W = wd.shape[0]
    tm, tn = _tile(S, tm, LANES), _tile(W, tn, LANES)

    def body(dy_ref, wd_ref, gu_ref, dgu_ref):
        da = _dot(dy_ref[...].astype(bf16), wd_ref[...], NT) * scale
        _, vjp = jax.vjp(_swiglu, gu_ref[0].astype(f32), gu_ref[1].astype(f32))
        dg, du = vjp(da)
        dgu_ref[0] = dg.astype(bf16)
        dgu_ref[1] = du.astype(bf16)

    planes = pl.BlockSpec((2, tm, tn), lambda j, i: (0, i, j))
    return _call(body, name, (W // tn, S // tm),
                 [pl.BlockSpec((tm, D), lambda j, i: (i, 0)), pl.BlockSpec((tn, D), lambda j, i: (j, 0)), planes],
                 planes, SDS((2, S, W), bf16), comm=comm)(dy, wd, gu)


def ffn_bwgu(h, dgu, name, tm=1024, tn=768, comm=None):
    S, D = h.shape
    W = dgu.shape[2]
    tm, tn = _tile(D, tm, LANES), _tile(W, tn, LANES)
    nb = W // tn

    def body(h_ref, d_ref, o_ref):
        o_ref[...] = _dot(h_ref[...], d_ref[...], TN).astype(bf16)

    return _call(body, name, (2 * nb, D // tm),
                 [pl.BlockSpec((S, tm), lambda j, i: (0, i)), pl.BlockSpec((None, S, tn), lambda j, i: (j // nb, 0, j % nb))],
                 pl.BlockSpec((tm, tn), lambda j, i: (i, j)), SDS((D, 2 * W), bf16), comm=comm)(h, dgu)


def nt_rms_bwd(pairs, x, g, dres, name, tm=512, comm=None):
    S, D = x.shape
    tm = _tile(S, tm)
    n = len(pairs)

    def body(*refs):
        x_ref, g_ref, dres_ref = refs[2 * n:2 * n + 3]
        dx_ref, dxb_ref, dg_ref = refs[2 * n + 3:]
        dh = sum(_dot(refs[2 * k][...].astype(bf16), refs[2 * k + 1][...], NT) for k in range(n))
        _, vjp = jax.vjp(_rms, x_ref[...], g_ref[...])
        dx, dg = vjp(dh)
        dx = dres_ref[...] + dx
        dx_ref[...] = dx
        dxb_ref[...] = dx.astype(bf16)

        @pl.when(pl.program_id(0) == 0)
        def _():
            dg_ref[...] = jnp.zeros_like(dg_ref)

        dg_ref[...] += dg

    arrays, specs = [], []
    for a, a_spec, b, b_spec in pairs:
        arrays += [a, b]
        specs += [a_spec, b_spec]
    (dx, dxb, dg), got = _carried(_call(body, name, (S // tm,), specs + [_rowspec(tm, D), _bspec(D), _rowspec(tm, D)],
                                        (_rowspec(tm, D), _rowspec(tm, D), _bspec(D)),
                                        (SDS((S, D), f32), SDS((S, D), bf16), SDS((1, D), f32)), comm=comm)(*arrays, x, g, dres),
                                  comm)
    return ((dx, dxb), dg) if comm is None else ((dx, dxb), dg, got)


def ffn_bdh_rms(dgu, wgu, x, g, dres, name, tm=512, comm=None):
    _, S, W = dgu.shape
    D = wgu.shape[0]
    tm = _tile(S, tm)
    pairs = [(dgu, pl.BlockSpec((None, tm, W), functools.partial(lambda p, i: (p, i, 0), p)),
              wgu, pl.BlockSpec((D, W), functools.partial(lambda p, i: (0, p), p))) for p in range(2)]
    return nt_rms_bwd(pairs, x, g, dres, name, tm, comm=comm)


def mm_bdh_rms(d, w, x, g, dres, name, tm=512, comm=None):
    S, K = d.shape
    tm = _tile(S, tm)
    return nt_rms_bwd([(d, pl.BlockSpec((tm, K), lambda i: (i, 0)), w, pl.BlockSpec(w.shape, lambda i: (0, 0)))], x, g, dres, name, tm,
                      comm=comm)


def loss_head(y, t, name):
    S, D = y.shape
    tm = _tile(S, 512)

    def body(y_ref, t_ref, dy_ref, dyb_ref, l_ref):
        e = y_ref[...] - t_ref[...]
        dy_ref[...] = e * (1.0 / D)
        dyb_ref[...] = (e * (1.0 / D)).astype(bf16)

        @pl.when(pl.program_id(0) == 0)
        def _():
            l_ref[...] = jnp.zeros_like(l_ref)

        l_ref[...] += jnp.sum(jnp.sum(e * e, axis=1, keepdims=True), axis=0, keepdims=True) * (0.5 / D)

    dy, dyb, loss = _call(body, name, (S // tm,), [_rowspec(tm, D), _rowspec(tm, D)],
                          (_rowspec(tm, D), _rowspec(tm, D), _bspec(LANES)),
                          (SDS((S, D), f32), SDS((S, D), bf16), SDS((1, LANES), f32)))(y, t)
    return (dy, dyb), loss


def ffn_fwd(x, g, wgu, wd, tag, comm_gu=None, comm_down=None):
    h = rms_fwd(x, g, f"{tag}_rms")
    (gu, a), got_gu = _carried(ffn_gu_act(h, wgu, f"{tag}_gu", comm=comm_gu), comm_gu)
    xo, got_down = _carried(mm(a, wd, "nn", f32, f"{tag}_down", tm=512, tn=1024, res=x, scale=0.5, comm=comm_down), comm_down)
    return xo, (h, gu, a), got_gu, got_down


def ffn_bwd(x, g, wgu, wd, saved, dy, tag, comms=None, own_axes=None):
    h, gu, a = saved
    dy32, dy16 = dy
    cm = comms or {}
    dgu, got_bda = _carried(ffn_bda_act(dy16, wd, gu, 0.5, f"{tag}_bda", comm=cm.get("bda")), cm.get("bda"))
    dwd, got_bwd = _carried(mm(a, dy16, "tn", bf16, f"{tag}_bwd", tm=512, tn=1024, scale=0.5, comm=cm.get("bwd")), cm.get("bwd"))
    c_wd = Comm("rs", [dwd], [own_axes[1]]) if own_axes else None
    dwgu, got_wd = _carried(ffn_bwgu(h, dgu, f"{tag}_bwgu", comm=c_wd), c_wd)
    c_wgu = Comm("rs", [dwgu], [own_axes[0]]) if own_axes else None
    res = ffn_bdh_rms(dgu, wgu, x, g, dy32, f"{tag}_bdh", comm=c_wgu)
    dx, dg = res[:2]
    if comms is None and own_axes is None:
        return dx, dg, dwgu, dwd
    return dx, dg, dwgu, dwd, dict(bda=got_bda, bwd=got_bwd, wd=got_wd, wgu=res[2] if c_wgu is not None else [])


def _split_bf16(x):
    hi = x.astype(bf16)
    lo = (x - hi.astype(f32)).astype(bf16)
    return hi, lo


SB_TQ, SB_TK, SB_NSUB = 512, 256, 4
FOX_TK = 256


def _sb_geometry(S, tk=SB_TK):
    tq = min(SB_TQ, S)
    tk = min(tk, tq)
    return tq, tk, tq // SB_NSUB, tq // tk


def _sb_consts(tk):
    r = lax.broadcasted_iota(jnp.int32, (tk, tk), 0)
    c = lax.broadcasted_iota(jnp.int32, (tk, tk), 1)
    return (r > c).astype(bf16), (r < c).astype(bf16)


def _active(d, nsub, rs, tk):
    return [s for s in range(nsub) if d is None or (s + 1) * rs > d * tk]


def _put(full, act, part):
    out = list(full)
    for s, x in zip(act, part):
        out[s] = x
    return out


def _scaled_q(q_ref, rs, nsub):
    scale = HEAD_DIM ** -0.5
    return [(q_ref[pl.ds(s * rs, rs), :].astype(f32) * scale).astype(bf16) for s in range(nsub)]


def _sb_scores(qs, k, kb, tk, rows, masked):
    z = [_dot(q, k, NT) for q in qs]
    ls = [_logsig(z_) for z_ in z]
    lm = [l - z_ for l, z_ in zip(ls, z)]
    ok = None
    if masked:
        col = kb * tk + lax.broadcasted_iota(jnp.int32, z[0].shape, 1)
        ok = [col < row for row in rows]
        lm = [jnp.where(m, x, 0.0) for m, x in zip(ok, lm)]
    return ls, lm, ok


def _sb_weights(ls, lm, ok, tail, after):
    suf = [_dot(x.astype(bf16), after) for x in lm]
    a = [jnp.exp(l + s_ + t_) for l, s_, t_ in zip(ls, suf, tail)]
    if ok is not None:
        a = [jnp.where(m, x, 0.0) for m, x in zip(ok, a)]
    return a


def sb_fwd(qkv, nh, name, comm=None):
    S = qkv.shape[0]
    tq, tk, rs, nd = _sb_geometry(S)
    nsub = tq // rs

    def body(q_ref, k_ref, v_ref, o_ref, tails_ref):
        i = pl.program_id(1)
        after, _ = _sb_consts(tk)
        qs = _scaled_q(q_ref, rs, nsub)
        rows = [i * tq + s * rs + lax.broadcasted_iota(jnp.int32, (rs, tk), 0) for s in range(nsub)]

        def tile(kb, carry, d):
            tail, acc = carry
            act = _active(d, nsub, rs, tk)
            pick = lambda lst: [lst[s] for s in act]
            off = pl.multiple_of(kb * tk, tk)
            k = k_ref[pl.ds(off, tk), :]
            v = v_ref[pl.ds(off, tk), :]
            ls, lm, ok = _sb_scores(pick(qs), k, kb, tk, pick(rows), d is not None)
            a = _sb_weights(ls, lm, ok, pick(tail), after)
            tails_ref[pl.ds(kb, 1), :] += jnp.concatenate([t_.T for t_ in tail], axis=1)
            acc = _put(acc, act, [ac + _dot(a_.astype(bf16), v) for ac, a_ in zip(pick(acc), a)])
            tail = _put(tail, act, [t_ + jnp.sum(x, axis=1, keepdims=True) for t_, x in zip(pick(tail), lm)])
            return tail, acc

        tails_ref[...] = jnp.zeros_like(tails_ref)
        carry = ([jnp.zeros((rs, 1), f32)] * nsub, [jnp.zeros((rs, HEAD_DIM), f32)] * nsub)
        for d in reversed(range(nd)):
            carry = tile(i * nd + d, carry, d)
        carry = lax.fori_loop(0, i * nd, lambda t, c: tile(i * nd - 1 - t, c, None), carry)
        for s in range(nsub):
            o_ref[pl.ds(s * rs, rs), :] = carry[1][s].astype(o_ref.dtype)

    blk = lambda cb0: pl.BlockSpec((tq, HEAD_DIM), lambda h, i: (i, cb0 + h))
    full = lambda cb0: pl.BlockSpec((S, HEAD_DIM), lambda h, i: (0, cb0 + h))
    tails = pl.BlockSpec((S // tk, tq), lambda h, i: (h, i))
    return _call(body, name, (nh, S // tq), [blk(0), full(nh), full(2 * nh)], (blk(0), tails),
                 (SDS((S, nh * HEAD_DIM), bf16), SDS((nh * (S // tk), S), f32)), comm=comm)(qkv, qkv, qkv)


def sb_bwd(qkv, tails, do, nh, name, comm=None):
    S = qkv.shape[0]
    tq, tk, rs, nd = _sb_geometry(S)
    nsub = tq // rs
    scale = HEAD_DIM ** -0.5

    def body(q_ref, k_ref, v_ref, tails_ref, do_ref, dq_ref, dk_ref, dv_ref):
        i = pl.program_id(1)

        @pl.when(i == 0)
        def _():
            dk_ref[...] = jnp.zeros_like(dk_ref)
            dv_ref[...] = jnp.zeros_like(dv_ref)

        after, before = _sb_consts(tk)
        qs = _scaled_q(q_ref, rs, nsub)
        dos = [do_ref[pl.ds(s * rs, rs), :].astype(bf16) for s in range(nsub)]
        rows = [i * tq + s * rs + lax.broadcasted_iota(jnp.int32, (rs, tk), 0) for s in range(nsub)]

        def sweep2(kb, carry, d):
            pe, dq = carry
            act = _active(d, nsub, rs, tk)
            pick = lambda lst: [lst[s] for s in act]
            qa, da = pick(qs), pick(dos)
            off = pl.multiple_of(kb * tk, tk)
            k = k_ref[pl.ds(off, tk), :]
            v = v_ref[pl.ds(off, tk), :]
            ls, lm, ok = _sb_scores(qa, k, kb, tk, pick(rows), d is not None)
            tail_row = tails_ref[pl.ds(kb, 1), :]
            tail = [tail_row[:, s * rs:(s + 1) * rs].T for s in act]
            a = _sb_weights(ls, lm, ok, tail, after)
            e = [_dot(d_, v, NT) * a_ for d_, a_ in zip(da, a)]
            cum_e = [p_ + _dot(e_.astype(bf16), before) for p_, e_ in zip(pick(pe), e)]
            sig = [jnp.exp(l) for l in ls]
            dz = [e_ * (1.0 - sg) - c_ * sg for e_, sg, c_ in zip(e, sig, cum_e)]
            if ok is not None:
                dz = [jnp.where(m, x, 0.0) for m, x in zip(ok, dz)]
            dzb = [x.astype(bf16) for x in dz]
            dk_ref[pl.ds(off, tk), :] += sum(_dot(x, q, TN) for x, q in zip(dzb, qa))
            dv_ref[pl.ds(off, tk), :] += sum(_dot(a_.astype(bf16), d_, TN) for a_, d_ in zip(a, da))
            pe = _put(pe, act, [p_ + jnp.sum(e_, axis=1, keepdims=True) for p_, e_ in zip(pick(pe), e)])
            dq = _put(dq, act, [g + _dot(x, k) for g, x in zip(pick(dq), dzb)])
            return pe, dq

        carry = ([jnp.zeros((rs, 1), f32)] * nsub, [jnp.zeros((rs, HEAD_DIM), f32)] * nsub)
        carry = lax.fori_loop(0, i * nd, lambda kb, c: sweep2(kb, c, None), carry)
        for d in range(nd):
            carry = sweep2(i * nd + d, carry, d)
        for s in range(nsub):
            dq_ref[pl.ds(s * rs, rs), :] = carry[1][s] * scale

    blk = lambda cb0: pl.BlockSpec((tq, HEAD_DIM), lambda h, i: (i, cb0 + h))
    full = lambda cb0: pl.BlockSpec((S, HEAD_DIM), lambda h, i: (0, cb0 + h))
    sh = SDS((S, nh * HEAD_DIM), f32)
    tails_spec = pl.BlockSpec((S // tk, tq), lambda h, i: (h, i))
    return _call(body, name, (nh, S // tq), [blk(0), full(nh), full(2 * nh), tails_spec, blk(0)], (blk(0), full(0), full(0)),
                 (sh, sh, sh), comm=comm)(qkv, qkv, qkv, tails, do)


def fox_fwd(fq, fk, fv, c, cT, nh, lane0, name, comm=None):
    S = fq.shape[0]
    tq, tk, rs, nd = _sb_geometry(S, FOX_TK)
    nsub = tq // rs
    scale = HEAD_DIM ** -0.5

    def body(q_ref, k_ref, v_ref, c_ref, ct_ref, o_ref, lse_ref):
        h = pl.program_id(0)
        i = pl.program_id(1)
        subs = range(nsub)
        qs = _scaled_q(q_ref, rs, nsub)
        ci = [_lane_pick(c_ref[pl.ds(s * rs, rs), :], lane0 + h) for s in subs]
        rows = [i * tq + s * rs + lax.broadcasted_iota(jnp.int32, (rs, tk), 0) for s in subs]

        def tile(kb, carry, d):
            m, l, acc = carry
            act = _active(d, nsub, rs, tk)
            pick = lambda lst: [lst[s] for s in act]
            off = pl.multiple_of(kb * tk, tk)
            k = k_ref[pl.ds(off, tk), :]
            v = v_ref[pl.ds(off, tk), :]
            cj = ct_ref[pl.ds(lane0 % 8 + h, 1), pl.ds(off, tk)]
            sc = [_dot(q, k, NT) + (c_ - cj) for q, c_ in zip(pick(qs), pick(ci))]
            if d is not None:
                col = kb * tk + lax.broadcasted_iota(jnp.int32, (rs, tk), 1)
                sc = [jnp.where(col <= row, x, -jnp.inf) for x, row in zip(sc, pick(rows))]
            m2 = [jnp.maximum(m_, jnp.max(x, axis=1, keepdims=True)) for m_, x in zip(pick(m), sc)]
            p = [jnp.exp(x - m_) for x, m_ in zip(sc, m2)]
            al = [jnp.exp(a - b) for a, b in zip(pick(m), m2)]
            l = _put(l, act, [a * l_ + jnp.sum(p_, axis=1, keepdims=True) for a, l_, p_ in zip(al, pick(l), p)])
            acc = _put(acc, act, [a * ac + _dot(p_.astype(bf16), v) for a, ac, p_ in zip(al, pick(acc), p)])
            return _put(m, act, m2), l, acc

        carry = ([jnp.full((rs, 1), -jnp.inf, f32)] * nsub, [jnp.zeros((rs, 1), f32)] * nsub,
                 [jnp.zeros((rs, HEAD_DIM), f32)] * nsub)
        for d in range(nd):
            carry = tile(i * nd + d, carry, d)
        m, l, acc = lax.fori_loop(0, i * nd, lambda kb, cr: tile(kb, cr, None), carry)
        for s in subs:
            o_ref[pl.ds(s * rs, rs), :] = acc[s] / l[s]
            lse_ref[pl.ds(s * rs, rs), :] = jnp.broadcast_to(m[s] + jnp.log(l[s]), (rs, HEAD_DIM))

    blk = pl.BlockSpec((tq, HEAD_DIM), lambda h, i: (i, h))
    full = pl.BlockSpec((S, HEAD_DIM), lambda h, i: (0, h))
    cspec = pl.BlockSpec((tq, LANES), lambda h, i: (i, 0))
    ctspec = pl.BlockSpec((8, S), lambda h, i: (lane0 // 8, 0))
    sh = SDS((S, nh * HEAD_DIM), f32)
    return _call(body, name, (nh, S // tq), [blk, full, full, cspec, ctspec], (blk, blk), (sh, sh), comm=comm)(fq, fk, fv, c, cT)


def fox_bwd(fq, fk, fv, c, cT, o, lse, do, nh, lane0, name, comm=None):
    S = fq.shape[0]
    tq, tk, rs, nd = _sb_geometry(S, FOX_TK)
    nsub = tq // rs
    scale = HEAD_DIM ** -0.5

    def body(q_ref, k_ref, v_ref, c_ref, ct_ref, o_ref, lse_ref, do_ref, dq_ref, dk_ref, dv_ref, dct_ref):
        h = pl.program_id(0)
        i = pl.program_id(1)

        @pl.when(i == 0)
        def _():
            dk_ref[...] = jnp.zeros_like(dk_ref)
            dv_ref[...] = jnp.zeros_like(dv_ref)

        @pl.when((i == 0) & (h == 0))
        def _():
            dct_ref[...] = jnp.zeros_like(dct_ref)

        subs = range(nsub)
        qs = _scaled_q(q_ref, rs, nsub)
        dof = [do_ref[pl.ds(s * rs, rs), :] for s in subs]
        dos = [x.astype(bf16) for x in dof]
        ci = [_lane_pick(c_ref[pl.ds(s * rs, rs), :], lane0 + h) for s in subs]
        lses = [lse_ref[pl.ds(s * rs, rs), :][:, :1] for s in subs]
        dl = [jnp.sum(x * o_ref[pl.ds(s * rs, rs), :], axis=1, keepdims=True) for s, x in zip(subs, dof)]
        rows = [i * tq + s * rs + lax.broadcasted_iota(jnp.int32, (rs, tk), 0) for s in subs]

        def tile(kb, carry, d):
            dq, rsum = carry
            act = _active(d, nsub, rs, tk)
            pick = lambda lst: [lst[s] for s in act]
            qa, da = pick(qs), pick(dos)
            off = pl.multiple_of(kb * tk, tk)
            k = k_ref[pl.ds(off, tk), :]
            v = v_ref[pl.ds(off, tk), :]
            cj = ct_ref[pl.ds(lane0 % 8 + h, 1), pl.ds(off, tk)]
            p = [jnp.exp(_dot(q, k, NT) + (c_ - cj) - ls) for q, c_, ls in zip(qa, pick(ci), pick(lses))]
            if d is not None:
                col = kb * tk + lax.broadcasted_iota(jnp.int32, (rs, tk), 1)
                p = [jnp.where(col <= row, x, 0.0) for x, row in zip(p, pick(rows))]
            ds = [p_ * (_dot(d_, v, NT) - dl_) for p_, d_, dl_ in zip(p, da, pick(dl))]
            dsb = [x.astype(bf16) for x in ds]
            dk_ref[pl.ds(off, tk), :] += sum(_dot(x, q, TN) for x, q in zip(dsb, qa))
            dv_ref[pl.ds(off, tk), :] += sum(_dot(p_.astype(bf16), d_, TN) for p_, d_ in zip(p, da))
            dct_ref[pl.ds(lane0 + h, 1), pl.ds(off, tk)] -= sum(jnp.sum(x, axis=0, keepdims=True) for x in ds)
            return (_put(dq, act, [g + _dot(x, k) for g, x in zip(pick(dq), dsb)]),
                    _put(rsum, act, [r_ + jnp.sum(x, axis=1, keepdims=True) for r_, x in zip(pick(rsum), ds)]))

        carry = ([jnp.zeros((rs, HEAD_DIM), f32)] * nsub, [jnp.zeros((rs, 1), f32)] * nsub)
        carry = lax.fori_loop(0, i * nd, lambda kb, cr: tile(kb, cr, None), carry)
        for d in range(nd):
            carry = tile(i * nd + d, carry, d)
        dq, rsum = carry
        for s in subs:
            dq_ref[pl.ds(s * rs, rs), :] = dq[s] * scale
        dct_ref[pl.ds(lane0 + h, 1), pl.ds(pl.multiple_of(i * tq, tq), tq)] += jnp.concatenate([r_.T for r_ in rsum], axis=1)

    blk = pl.BlockSpec((tq, HEAD_DIM), lambda h, i: (i, h))
    full = pl.BlockSpec((S, HEAD_DIM), lambda h, i: (0, h))
    cspec = pl.BlockSpec((tq, LANES), lambda h, i: (i, 0))
    ctspec = pl.BlockSpec((8, S), lambda h, i: (lane0 // 8, 0))
    dctspec = pl.BlockSpec((LANES, S), lambda h, i: (0, 0))
    sh = SDS((S, nh * HEAD_DIM), f32)
    return _call(body, name, (nh, S // tq), [blk, full, full, cspec, ctspec, blk, blk, blk], (blk, full, full, dctspec),
                 (sh, sh, sh, SDS((LANES, S), f32)), comm=comm)(fq, fk, fv, c, cT, o, lse, do)


def gates_fwd(proj, small_cb, fbias_row, name, tm=256):
    S = proj.shape[0]
    tm = _tile(S, tm)

    def body(sm_ref, fb_ref, c_ref, ct_ref, carry_ref):
        @pl.when(pl.program_id(0) == 0)
        def _():
            carry_ref[...] = jnp.zeros_like(carry_ref)

        lf = _logsig(sm_ref[...] + fb_ref[...])
        r = lax.broadcasted_iota(jnp.int32, (tm, tm), 0)
        cc = lax.broadcasted_iota(jnp.int32, (tm, tm), 1)
        c = _dot((r >= cc).astype(f32), lf, NN, HI) + carry_ref[...]
        carry_ref[...] += jnp.sum(lf, axis=0, keepdims=True)
        c_ref[...] = c
        ct_ref[...] = c.T

    return _call(body, name, (S // tm,), [_rowspec(tm, LANES, small_cb), _bspec(LANES)],
                 (_rowspec(tm, LANES), pl.BlockSpec((LANES, tm), lambda i: (0, i))),
                 (SDS((S, LANES), f32), SDS((LANES, S), f32)), scratch=[pltpu.VMEM((1, LANES), f32)])(proj, fbias_row)


def gates_bwd(proj, small_cb, fbias_row, dcT, dsm_dn, lane0, nh, name, tm=256):
    S = proj.shape[0]
    tm = _tile(S, tm)
    nt = S // tm

    def body(sm_ref, fb_ref, dct_ref, dsm_ref, o_ref, dfb_ref, carry_ref):
        @pl.when(pl.program_id(0) == 0)
        def _():
            carry_ref[...] = jnp.zeros_like(carry_ref)
            dfb_ref[...] = jnp.zeros_like(dfb_ref)

        dc = dct_ref[...].T
        r = lax.broadcasted_iota(jnp.int32, (tm, tm), 0)
        cc = lax.broadcasted_iota(jnp.int32, (tm, tm), 1)
        dlf = _dot((r <= cc).astype(f32), dc, NN, HI) + carry_ref[...]
        carry_ref[...] += jnp.sum(dc, axis=0, keepdims=True)
        lane = lax.broadcasted_iota(jnp.int32, (1, LANES), 1)
        dpre = jnp.where((lane >= lane0) & (lane < lane0 + nh), dlf * jax.nn.sigmoid(-(sm_ref[...] + fb_ref[...])), 0.0)
        o_ref[...] = dsm_ref[...] + dpre
        dfb_ref[...] += jnp.sum(dpre, axis=0, keepdims=True)

    rev = lambda i: nt - 1 - i
    return _call(body, name, (nt,),
                 [pl.BlockSpec((tm, LANES), lambda i: (rev(i), small_cb)), _bspec(LANES),
                  pl.BlockSpec((LANES, tm), lambda i: (0, rev(i))), pl.BlockSpec((tm, LANES), lambda i: (rev(i), 0))],
                 (pl.BlockSpec((tm, LANES), lambda i: (rev(i), 0)), _bspec(LANES)),
                 (SDS((S, LANES), f32), SDS((1, LANES), f32)), scratch=[pltpu.VMEM((1, LANES), f32)])(proj, fbias_row, dcT, dsm_dn)


PAD_ROWS = 8


def conv_fwd(proj, w, width, name):
    S = proj.shape[0]
    cw = 256 if width % 256 == 0 else 128

    def body(x_ref, w_ref, y_ref, pad_ref):
        pad_ref[pl.ds(0, PAD_ROWS), :] = jnp.zeros((PAD_ROWS, cw), f32)
        pad_ref[pl.ds(PAD_ROWS, S), :] = x_ref[...]
        y = jnp.zeros((S, cw), f32)
        for i in range(CONV_WIDTH):
            y = y + pad_ref[pl.ds(PAD_ROWS - (CONV_WIDTH - 1) + i, S), :] * w_ref[pl.ds(i, 1), :]
        y_ref[...] = y * jax.nn.sigmoid(y)

    spec = pl.BlockSpec((S, cw), lambda j: (0, j))
    return _call(body, name, (width // cw,), [spec, pl.BlockSpec((CONV_WIDTH, cw), lambda j: (0, j))], spec,
                 SDS((S, width), f32), scratch=[pltpu.VMEM((S + PAD_ROWS, cw), f32)])(proj, w)


def conv_bwd(proj, w, dy, name):
    S, width = dy.shape
    cw = 256 if width % 256 == 0 else 128

    def body(x_ref, w_ref, dy_ref, dx_ref, dw_ref, xpad_ref, dpad_ref):
        xpad_ref[pl.ds(0, PAD_ROWS), :] = jnp.zeros((PAD_ROWS, cw), f32)
        xpad_ref[pl.ds(PAD_ROWS, S), :] = x_ref[...]
        y = jnp.zeros((S, cw), f32)
        for i in range(CONV_WIDTH):
            y = y + xpad_ref[pl.ds(PAD_ROWS - (CONV_WIDTH - 1) + i, S), :] * w_ref[pl.ds(i, 1), :]
        sg = jax.nn.sigmoid(y)
        dpre = dy_ref[...] * (sg * (1.0 + y * (1.0 - sg)))
        dpad_ref[pl.ds(S, PAD_ROWS), :] = jnp.zeros((PAD_ROWS, cw), f32)
        dpad_ref[pl.ds(0, S), :] = dpre
        dx = jnp.zeros((S, cw), f32)
        for i in range(CONV_WIDTH):
            dx = dx + dpad_ref[pl.ds(CONV_WIDTH - 1 - i, S), :] * w_ref[pl.ds(i, 1), :]
            dw_ref[pl.ds(i, 1), :] = jnp.sum(dpre * xpad_ref[pl.ds(PAD_ROWS - (CONV_WIDTH - 1) + i, S), :], axis=0, keepdims=True)
        dx_ref[...] = dx

    spec = pl.BlockSpec((S, cw), lambda j: (0, j))
    wspec = pl.BlockSpec((CONV_WIDTH, cw), lambda j: (0, j))
    return _call(body, name, (width // cw,), [spec, wspec, spec], (spec, wspec),
                 (SDS((S, width), f32), SDS((CONV_WIDTH, width), f32)),
                 scratch=[pltpu.VMEM((S + PAD_ROWS, cw), f32), pltpu.VMEM((S + PAD_ROWS, cw), f32)])(proj, w, dy)


def _pdot_raw(a, b, dims, passes):
    if passes == 1:
        return _dot(a.astype(bf16), b.astype(bf16), dims)
    ah, al = _split_bf16(a)
    bh, bl = _split_bf16(b)
    return _dot(ah, bh, dims) + (_dot(ah, bl, dims) + _dot(al, bh, dims))


def _make_pdot(passes):
    @functools.partial(jax.custom_vjp, nondiff_argnums=(2,))
    def pdot(a, b, mode):
        return _pdot_raw(a, b, {"nn": NN, "nt": NT, "tn": TN}[mode], passes)

    def fwd(a, b, mode):
        return pdot(a, b, mode), (a, b)

    def bwd(mode, res, g):
        a, b = res
        if mode == "nn":
            return _pdot_raw(g, b, NT, passes), _pdot_raw(a, g, TN, passes)
        if mode == "nt":
            return _pdot_raw(g, b, NN, passes), _pdot_raw(g, a, TN, passes)
        return _pdot_raw(b, g, NT, passes), _pdot_raw(a, g, NN, passes)

    pdot.defvjp(fwd, bwd)
    return pdot


_dot1 = _make_pdot(1)
_dot3 = _make_pdot(3)


def _each(f, *lists):
    return [f(*xs) for xs in zip(*lists)]


def _dn_chunk(ndn, cq, ck, cv, small, alog, dtb, s0):
    C = cq[0].shape[0]
    hs = list(range(ndn))
    b = [_lane_pick(small, h) for h in hs]
    d = [_lane_pick(small, ndn + h) for h in hs]
    al = [_lane_pick(alog, h) for h in hs]
    dt = [_lane_pick(dtb, h) for h in hs]
    q = _each(lambda x: x * lax.rsqrt(jnp.sum(x * x, axis=1, keepdims=True) + EPS) * (HEAD_DIM ** -0.5), cq)
    k = _each(lambda x: x * lax.rsqrt(jnp.sum(x * x, axis=1, keepdims=True) + EPS), ck)
    beta = _each(jax.nn.sigmoid, b)
    g = _each(lambda al_, d_, dt_: -jnp.exp(al_) * _softplus(d_ + dt_), al, d, dt)
    r = lax.broadcasted_iota(jnp.int32, (C, C), 0)
    c = lax.broadcasted_iota(jnp.int32, (C, C), 1)
    incl, strict = r >= c, r > c
    inclf = incl.astype(f32)
    gc = _each(lambda g_: _dot3(inclf, g_, "nn"), g)
    decay = _each(lambda gc_: jnp.where(incl, jnp.exp(jnp.where(incl, gc_ - gc_.T, 0.0)), 0.0), gc)
    kb = _each(lambda k_, b_: k_ * b_, k, beta)
    a = _each(lambda kb_, k_, dec: jnp.where(strict, _dot3(kb_, k_, "nt") * dec, 0.0), kb, k, decay)
    eye = (r == c).astype(f32)
    t = _each(lambda a_: eye - a_, a)
    p = a
    for _ in range(int(math.log2(C)) - 1):
        p = _each(lambda p_: _dot3(p_, p_, "nn"), p)
        t = _each(lambda t_, p_: t_ + _dot3(t_, p_, "nn"), t, p)
    eg = _each(jnp.exp, gc)
    u = _each(lambda t_, v_, b_: _dot3(t_, v_ * b_, "nn"), t, cv, beta)
    w = _each(lambda t_, kb_, eg_: _dot3(t_, kb_ * eg_, "nn"), t, kb, eg)
    attn = _each(lambda q_, k_, dec: _dot1(q_, k_, "nt") * dec, q, k, decay)
    g_last = _each(lambda g_: jnp.sum(g_, axis=0, keepdims=True), g)
    v_new = _each(lambda u_, w_, s_: u_ - _dot1(w_, s_, "nn"), u, w, s0)
    o = _each(lambda q_, eg_, s_, at, vn: _dot1(q_ * eg_, s_, "nn") + _dot1(at, vn, "nn"), q, eg, s0, attn, v_new)
    s1 = _each(lambda s_, gl, k_, gc_, vn: s_ * jnp.exp(gl) + _dot1(k_ * jnp.exp(gl - gc_), vn, "tn"), s0, g_last, k, gc, v_new)
    return o, s1


def dn_fwd(cqkv, proj, small_cb, alog_row, dt_row, ndn, name, comm=None):
    S = cqkv.shape[0]
    C = DN_CHUNK
    nc = S // C
    half = ndn * HEAD_DIM

    def body(q_ref, k_ref, v_ref, sm_ref, al_ref, dt_ref, o_ref, ss_ref, s_ref):
        @pl.when(pl.program_id(0) == 0)
        def _():
            s_ref[...] = jnp.zeros_like(s_ref)

        sls = [slice(h * HEAD_DIM, (h + 1) * HEAD_DIM) for h in range(ndn)]
        s0 = [s_ref[h] for h in range(ndn)]
        for h in range(ndn):
            ss_ref[h, 0] = s0[h]
        o, s1 = _dn_chunk(ndn, [q_ref[:, sl] for sl in sls], [k_ref[:, sl] for sl in sls], [v_ref[:, sl] for sl in sls],
                          sm_ref[...], al_ref[...], dt_ref[...], s0)
        for h in range(ndn):
            o_ref[:, sls[h]] = o[h]
            s_ref[h] = s1[h]

    blk = lambda cb: pl.BlockSpec((C, half), lambda n: (n, cb))
    row = pl.BlockSpec((1, LANES), lambda n: (0, 0))
    return _call(body, name, (nc,),
                 [blk(0), blk(1), blk(2), pl.BlockSpec((C, LANES), lambda n: (n, small_cb)), row, row],
                 (blk(0), pl.BlockSpec((ndn, 1, HEAD_DIM, HEAD_DIM), lambda n: (0, n, 0, 0))),
                 (SDS((S, half), f32), SDS((ndn, nc, HEAD_DIM, HEAD_DIM), f32)),
                 scratch=[pltpu.VMEM((ndn, HEAD_DIM, HEAD_DIM), f32)], comm=comm)(cqkv, cqkv, cqkv, proj, alog_row, dt_row)


def dn_bwd(cqkv, proj, small_cb, alog_row, dt_row, ssave, do, ndn, name, comm=None):
    S = cqkv.shape[0]
    C = DN_CHUNK
    nc = S // C
    half = ndn * HEAD_DIM

    def body(q_ref, k_ref, v_ref, sm_ref, al_ref, dt_ref, ss_ref, do_ref, dqkv_ref, dsm_ref, dal_ref, ddt_ref, ds_ref):
        @pl.when(pl.program_id(0) == 0)
        def _():
            ds_ref[...] = jnp.zeros_like(ds_ref)
            dal_ref[...] = jnp.zeros_like(dal_ref)
            ddt_ref[...] = jnp.zeros_like(ddt_ref)

        sls = [slice(h * HEAD_DIM, (h + 1) * HEAD_DIM) for h in range(ndn)]
        _, vjp = jax.vjp(functools.partial(_dn_chunk, ndn), [q_ref[:, sl] for sl in sls], [k_ref[:, sl] for sl in sls],
                         [v_ref[:, sl] for sl in sls], sm_ref[...], al_ref[...], dt_ref[...], [ss_ref[h, 0] for h in range(ndn)])
        dq, dk, dv, dsm, dal, ddt, ds0 = vjp(([do_ref[:, sl] for sl in sls], [ds_ref[h] for h in range(ndn)]))
        for h in range(ndn):
            dqkv_ref[:, sls[h]] = dq[h]
            dqkv_ref[:, half + h * HEAD_DIM:half + (h + 1) * HEAD_DIM] = dk[h]
            dqkv_ref[:, 2 * half + h * HEAD_DIM:2 * half + (h + 1) * HEAD_DIM] = dv[h]
            ds_ref[h] = ds0[h]
        dsm_ref[...] = dsm
        dal_ref[...] += dal
        ddt_ref[...] += ddt

    rev = lambda n: nc - 1 - n
    blk = lambda cb: pl.BlockSpec((C, half), lambda n: (rev(n), cb))
    row = pl.BlockSpec((1, LANES), lambda n: (0, 0))
    return _call(body, name, (nc,),
                 [blk(0), blk(1), blk(2), pl.BlockSpec((C, LANES), lambda n: (rev(n), small_cb)), row, row,
                  pl.BlockSpec((ndn, 1, HEAD_DIM, HEAD_DIM), lambda n: (0, rev(n), 0, 0)), blk(0)],
                 (pl.BlockSpec((C, 3 * half), lambda n: (rev(n), 0)), pl.BlockSpec((C, LANES), lambda n: (rev(n), 0)), row, row),
                 (SDS((S, 3 * half), f32), SDS((S, LANES), f32), SDS((1, LANES), f32), SDS((1, LANES), f32)),
                 scratch=[pltpu.VMEM((ndn, HEAD_DIM, HEAD_DIM), f32)], comm=comm)(cqkv, cqkv, cqkv, proj, alog_row, dt_row, ssave, do)


def even_pre(proj, gq, gk, dfx, cb0, name):
    S = proj.shape[0]
    tm = _tile(S, 512)
    nh = dfx // HEAD_DIM

    def body(q_ref, k_ref, v_ref, gq_ref, gk_ref, fq_ref, fk_ref, fv_ref):
        for h in range(nh):
            sl = slice(h * HEAD_DIM, (h + 1) * HEAD_DIM)
            fq_ref[:, sl] = _rms(q_ref[:, sl], gq_ref[...]).astype(bf16)
            fk_ref[:, sl] = _rms(k_ref[:, sl], gk_ref[...]).astype(bf16)
        fv_ref[...] = v_ref[...].astype(bf16)

    sh = SDS((S, dfx), bf16)
    o = _rowspec(tm, dfx)
    return _call(body, name, (S // tm,),
                 [_rowspec(tm, dfx, cb0), _rowspec(tm, dfx, cb0 + 1), _rowspec(tm, dfx, cb0 + 2), _bspec(HEAD_DIM), _bspec(HEAD_DIM)],
                 (o, o, o), (sh, sh, sh))(proj, proj, proj, gq, gk)


def even_pre_bwd(proj, gq, gk, dfq, dfk, dfx, cb0, name):
    S = proj.shape[0]
    tm = _tile(S, 512)
    nh = dfx // HEAD_DIM

    def body(q_ref, k_ref, gq_ref, gk_ref, dfq_ref, dfk_ref, dq_ref, dk_ref, dgq_ref, dgk_ref):
        @pl.when(pl.program_id(0) == 0)
        def _():
            dgq_ref[...] = jnp.zeros_like(dgq_ref)
            dgk_ref[...] = jnp.zeros_like(dgk_ref)

        for h in range(nh):
            sl = slice(h * HEAD_DIM, (h + 1) * HEAD_DIM)
            _, vq = jax.vjp(_rms, q_ref[:, sl], gq_ref[...])
            dq, dgq = vq(dfq_ref[:, sl])
            _, vk = jax.vjp(_rms, k_ref[:, sl], gk_ref[...])
            dk, dgk = vk(dfk_ref[:, sl])
            dq_ref[:, sl] = dq
            dk_ref[:, sl] = dk
            dgq_ref[...] += dgq
            dgk_ref[...] += dgk

    sh = SDS((S, dfx), f32)
    o = _rowspec(tm, dfx)
    g = SDS((1, HEAD_DIM), f32)
    return _call(body, name, (S // tm,),
                 [_rowspec(tm, dfx, cb0), _rowspec(tm, dfx, cb0 + 1), _bspec(HEAD_DIM), _bspec(HEAD_DIM), o, o],
                 (o, o, _bspec(HEAD_DIM), _bspec(HEAD_DIM)), (sh, sh, g, g))(proj, proj, gq, gk, dfq, dfk)


def _dn_out(o, gate, gn):
    return _rms(o, gn) * (gate * jax.nn.sigmoid(gate))


def _fox_out(o, gate):
    return o * jax.nn.sigmoid(gate)


def even_post(o_dn, o_fox, proj, gn, half, cb_dn_gate, cb_fox_gate, name):
    S = proj.shape[0]
    tm = _tile(S, 512)
    nh = half // HEAD_DIM

    def body(od_ref, of_ref, gd_ref, gf_ref, gn_ref, o_ref):
        for h in range(nh):
            sl = slice(h * HEAD_DIM, (h + 1) * HEAD_DIM)
            o_ref[:, sl] = _dn_out(od_ref[:, sl], gd_ref[:, sl], gn_ref[...]).astype(bf16)
        o_ref[:, half:] = _fox_out(of_ref[...], gf_ref[...]).astype(bf16)

    return _call(body, name, (S // tm,),
                 [_rowspec(tm, half), _rowspec(tm, half), _rowspec(tm, half, cb_dn_gate), _rowspec(tm, half, cb_fox_gate), _bspec(HEAD_DIM)],
                 _rowspec(tm, 2 * half), SDS((S, 2 * half), bf16))(o_dn, o_fox, proj, proj, gn)


def even_post_bwd(o_dn, o_fox, proj, gn, dom, half, cb_dn_gate, cb_fox_gate, name):
    S = proj.shape[0]
    tm = _tile(S, 512)
    nh = half // HEAD_DIM

    def body(od_ref, of_ref, gd_ref, gf_ref, gn_ref, dod_ref, dof_ref, dd_ref, df_ref, dgd_ref, dgf_ref, dgn_ref):
        @pl.when(pl.program_id(0) == 0)
        def _():
            dgn_ref[...] = jnp.zeros_like(dgn_ref)

        for h in range(nh):
            sl = slice(h * HEAD_DIM, (h + 1) * HEAD_DIM)
            _, vjp = jax.vjp(_dn_out, od_ref[:, sl], gd_ref[:, sl], gn_ref[...])
            d_o, d_gate, d_gn = vjp(dod_ref[:, sl])
            dd_ref[:, sl] = d_o
            dgd_ref[:, sl] = d_gate
            dgn_ref[...] += d_gn
        _, vjp = jax.vjp(_fox_out, of_ref[...], gf_ref[...])
        d_o, d_gate = vjp(dof_ref[...])
        df_ref[...] = d_o
        dgf_ref[...] = d_gate

    sh = SDS((S, half), f32)
    o = _rowspec(tm, half)
    return _call(body, name, (S // tm,),
                 [o, o, _rowspec(tm, half, cb_dn_gate), _rowspec(tm, half, cb_fox_gate), _bspec(HEAD_DIM),
                  _rowspec(tm, half, 0), _rowspec(tm, half, 1)],
                 (o, o, o, o, _bspec(HEAD_DIM)), (sh, sh, sh, sh, SDS((1, HEAD_DIM), f32)))(o_dn, o_fox, proj, proj, gn, dom, dom)


def _pad_row(v, lane0=0):
    return jnp.pad(v, (lane0, LANES - lane0 - v.shape[0]))[None]


def _carried(res, comm):
    return res if comm is not None else (res, [])


def even_mixer_fwd(x, gmix, w_in, w_out, conv_w, a_log, dt_bias, gn, gq, gk, f_bias, tag, comm_fox=None, comm_dn=None,
                   comm_in=None):
    S, D = x.shape
    half = D // 2
    ndn = half // HEAD_DIM
    small_cb = 4 * D // LANES
    alog_row, dt_row, fb_row = _pad_row(a_log), _pad_row(dt_bias), _pad_row(f_bias, 2 * ndn)
    h = rms_fwd(x, gmix, f"{tag}_rms")
    proj, got_in = _carried(mm(h, w_in, "nn", f32, f"{tag}_in", tm=1024, tn=1408, comm=comm_in), comm_in)
    cqkv = conv_fwd(proj, conv_w, 3 * half, f"{tag}_conv")
    (o_dn, ssave), got_dn = _carried(dn_fwd(cqkv, proj, small_cb, alog_row, dt_row, ndn, f"{tag}_dn", comm=comm_dn), comm_dn)
    c, cT = gates_fwd(proj, small_cb, fb_row, f"{tag}_gates")
    fq, fk, fv = even_pre(proj, gq, gk, half, 4, f"{tag}_pre")
    (o_fox, lse), got_fox = _carried(fox_fwd(fq, fk, fv, c, cT, ndn, 2 * ndn, f"{tag}_fox", comm=comm_fox), comm_fox)
    om = even_post(o_dn, o_fox, proj, gn, half, 3, 7, f"{tag}_post")
    xo = mm(om, w_out, "nn", f32, f"{tag}_out", tn=1024, res=x)
    return xo, (h, proj, cqkv, o_dn, ssave, c, cT, fq, fk, fv, o_fox, lse, om, alog_row, dt_row, fb_row), got_fox, got_dn, got_in


def even_mixer_bwd(x, gmix, w_in, w_out, conv_w, gn, gq, gk, saved, dy, tag, comm_fox=None, comm_dn=None, comm_bdh=None):
    S, D = x.shape
    half = D // 2
    ndn = half // HEAD_DIM
    small_cb = 4 * D // LANES
    h, proj, cqkv, o_dn, ssave, c, cT, fq, fk, fv, o_fox, lse, om, alog_row, dt_row, fb_row = saved
    dy32, dy16 = dy
    dom = mm(dy16, w_out, "nt", f32, f"{tag}_bdom", tn=1024)
    dw_out = mm(om, dy16, "tn", bf16, f"{tag}_bwout", tn=1024)
    d_odn, d_ofox, d_dgate, d_fgate, d_gn = even_post_bwd(o_dn, o_fox, proj, gn, dom, half, 3, 7, f"{tag}_bpost")
    (dfq, dfk, dfv, dcT), got_fox = _carried(
        fox_bwd(fq, fk, fv, c, cT, o_fox, lse, d_ofox, ndn, 2 * ndn, f"{tag}_bfox", comm=comm_fox), comm_fox)
    dq_pre, dk_pre, d_gq, d_gk = even_pre_bwd(proj, gq, gk, dfq, dfk, half, 4, f"{tag}_bpre")
    (dcqkv, dsm_dn, d_alog, d_dt), got_dn = _carried(
        dn_bwd(cqkv, proj, small_cb, alog_row, dt_row, ssave, d_odn, ndn, f"{tag}_bdn", comm=comm_dn), comm_dn)
    d_conv_in, d_conv_w = conv_bwd(proj, conv_w, dcqkv, f"{tag}_bconv")
    d_small, d_fb = gates_bwd(proj, small_cb, fb_row, dcT, dsm_dn, 2 * ndn, ndn, f"{tag}_bgates")
    dproj = jnp.concatenate([d_conv_in, d_dgate, dq_pre, dk_pre, dfv, d_fgate, d_small], axis=1).astype(bf16)
    dw_in = mm(h, dproj, "tn", bf16, f"{tag}_bwin", tn=1408)
    res = mm_bdh_rms(dproj, w_in, x, gmix, dy32, f"{tag}_bdh", comm=comm_bdh)
    dx, d_gmix = res[:2]
    got_bdh = res[2] if comm_bdh is not None else []
    small = dict(norm_mix=d_gmix[0], dn_conv_w=d_conv_w, dn_a_log=d_alog[0, :ndn], dn_dt_bias=d_dt[0, :ndn],
                 dn_norm_g=d_gn[0], fox_q_norm_g=d_gq[0], fox_k_norm_g=d_gk[0], fox_f_bias=d_fb[0, 2 * ndn:3 * ndn])
    return dx, dw_in, dw_out, small, got_fox, got_dn, got_bdh


def odd_mixer_fwd(x, gmix, w_in, w_out, tag, comm=None):
    S, D = x.shape
    nh = D // HEAD_DIM
    h = rms_fwd(x, gmix, f"{tag}_rms")
    qkv = mm(h, w_in, "nn", bf16, f"{tag}_in", tm=1024, tn=768)
    (o, tails), got = _carried(sb_fwd(qkv, nh, f"{tag}_sb", comm=comm), comm)
    xo = mm(o, w_out, "nn", f32, f"{tag}_out", tn=1024, res=x)
    return xo, (h, qkv, o, tails), got


def odd_mixer_bwd(x, gmix, w_in, w_out, saved, dy, tag, comm=None):
    S, D = x.shape
    nh = D // HEAD_DIM
    h, qkv, o, tails = saved
    dy32, dy16 = dy
    do = mm(dy16, w_out, "nt", f32, f"{tag}_bdo", tn=1024)
    dw_out = mm(o, dy16, "tn", bf16, f"{tag}_bwout", tn=1024)
    (dq, dk, dv), got = _carried(sb_bwd(qkv, tails, do, nh, f"{tag}_bsb", comm=comm), comm)
    dqkv = jnp.concatenate([dq, dk, dv], axis=1).astype(bf16)
    dw_in = mm(h, dqkv, "tn", bf16, f"{tag}_bwin", tn=1536)
    dx, d_gmix = mm_bdh_rms(dqkv, w_in, x, gmix, dy32, f"{tag}_bdh")
    return dx, dw_in, dw_out, dict(norm_mix=d_gmix[0]), got


def all_gather(shards, axes, name, kind="ag"):
    return _call(None, name, (), [], [], [], comm=Comm(kind, shards, axes))()[1]


def sum_parts(parts, name):
    _, R, C = parts.shape
    tm = _tile(R, 512)

    def body(p_ref, o_ref):
        acc = p_ref[0].astype(f32)
        for k in range(1, NDEV):
            acc = acc + p_ref[k].astype(f32)
        o_ref[...] = acc

    return _call(body, name, (R // tm,), [pl.BlockSpec((NDEV, tm, C), lambda i: (0, i, 0))], _rowspec(tm, C), SDS((R, C), f32))(parts)


def adamw(w, g, m, v, name):
    L, R, C = w.shape
    tm = _tile(R, max(8, min(512, (ADAMW_TILE_ELEMS // C) // 8 * 8)))
    c1 = 1.0 - ADAM_B1 ** ADAM_STEP
    c2 = 1.0 - ADAM_B2 ** ADAM_STEP

    def body(w_ref, g_ref, m_ref, v_ref, d_ref, nm_ref, nv_ref):
        g_ = g_ref[...]
        nm = ADAM_B1 * m_ref[...] + (1.0 - ADAM_B1) * g_
        nv = ADAM_B2 * v_ref[...] + (1.0 - ADAM_B2) * (g_ * g_)
        d_ref[...] = -ADAM_LR * ((nm / c1) / (jnp.sqrt(nv / c2) + ADAM_EPS) + ADAM_WD * w_ref[...])
        nm_ref[...] = nm
        nv_ref[...] = nv

    spec = pl.BlockSpec((None, tm, C), lambda l, i: (l, i, 0))
    sh = SDS((L, R, C), f32)
    return _call(body, name, (L, R // tm), [spec] * 4, (spec, spec, spec), (sh, sh, sh))(w, g, m, v)


def _pad_to(n, mult):
    return -(-n // mult) * mult


def _even_perm(D):
    half = D // 2
    ndn = half // HEAD_DIM
    o_gate = 3 * half
    o_b = o_gate + half
    o_a = o_b + ndn
    o_fq = o_a + ndn
    o_fpre = o_fq + 4 * half
    return half, ndn, o_b, o_a, o_fq, o_fpre


def _even_to_mine(w):
    D = (w.shape[-1] // 4) // LANES * LANES
    half, ndn, o_b, o_a, o_fq, o_fpre = _even_perm(D)
    small = jnp.concatenate([w[..., o_b:o_b + ndn], w[..., o_a:o_a + ndn], w[..., o_fpre:o_fpre + ndn]], axis=-1)
    small = jnp.pad(small, [(0, 0)] * (w.ndim - 1) + [(0, LANES - 3 * ndn)])
    return jnp.concatenate([w[..., :o_b], w[..., o_fq:o_fpre], small], axis=-1)


def _even_from_mine(w, D):
    half, ndn, o_b, o_a, o_fq, o_fpre = _even_perm(D)
    sm = w[..., 4 * D:]
    return jnp.concatenate([w[..., :o_b], sm[..., :ndn], sm[..., ndn:2 * ndn], w[..., o_b:4 * D], sm[..., 2 * ndn:3 * ndn]], axis=-1)


def _pack_small(parts):
    flat = jnp.concatenate([p.reshape(-1).astype(f32) for p in parts])
    n = flat.shape[0]
    return jnp.pad(flat, (0, _pad_to(n, 8 * LANES) - n)).reshape(-1, LANES)


def _unpack_small(packed, like):
    flat = packed.reshape(-1)
    out, off = [], 0
    for p in like:
        out.append(flat[off:off + p.size].reshape(p.shape))
        off += p.size
    return out


SMALL_NAMES = ("norm_ffn1", "norm_mix", "dn_a_log", "dn_dt_bias", "dn_norm_g", "fox_q_norm_g", "fox_k_norm_g", "fox_f_bias", "norm_ffn2")
BIG_NAMES = ("ffn1_w_gu", "ffn1_w_down", "w_in_even", "dn_conv_w", "w_out_even", "w_in_odd", "w_out_odd", "ffn2_w_gu", "ffn2_w_down")
WEIGHT_NAMES = ("norm_ffn1", "ffn1_w_gu", "ffn1_w_down", "norm_mix", "w_in_even", "dn_conv_w", "dn_a_log", "dn_dt_bias", "dn_norm_g",
                "fox_q_norm_g", "fox_k_norm_g", "fox_f_bias", "w_out_even", "w_in_odd", "w_out_odd", "norm_ffn2", "ffn2_w_gu", "ffn2_w_down")


def kernel(x, norm_ffn1, ffn1_w_gu, ffn1_w_down, norm_mix, w_in_even, dn_conv_w, dn_a_log, dn_dt_bias, dn_norm_g, fox_q_norm_g, fox_k_norm_g, fox_f_bias, w_out_even, w_in_odd, w_out_odd, norm_ffn2, ffn2_w_gu, ffn2_w_down, loss_target, m_norm_ffn1, m_ffn1_w_gu, m_ffn1_w_down, m_norm_mix, m_w_in_even, m_dn_conv_w, m_dn_a_log, m_dn_dt_bias, m_dn_norm_g, m_fox_q_norm_g, m_fox_k_norm_g, m_fox_f_bias, m_w_out_even, m_w_in_odd, m_w_out_odd, m_norm_ffn2, m_ffn2_w_gu, m_ffn2_w_down, v_norm_ffn1, v_ffn1_w_gu, v_ffn1_w_down, v_norm_mix, v_w_in_even, v_dn_conv_w, v_dn_a_log, v_dn_dt_bias, v_dn_norm_g, v_fox_q_norm_g, v_fox_k_norm_g, v_fox_f_bias, v_w_out_even, v_w_in_odd, v_w_out_odd, v_norm_ffn2, v_ffn2_w_gu, v_ffn2_w_down):
    args = dict(locals())
    W = {n: args[n] for n in WEIGHT_NAMES}
    M = {n: args["m_" + n] for n in WEIGHT_NAMES}
    V = {n: args["v_" + n] for n in WEIGHT_NAMES}
    xs = x[0]
    tgt = loss_target[0]
    S, D = xs.shape
    L = norm_ffn1.shape[0]
    n_even = w_in_even.shape[0]
    hs = ffn1_w_down.shape[1]
    hp = _pad_to(hs, LANES)
    me = 4 * lax.axis_index("x") + 2 * lax.axis_index("y") + lax.axis_index("c")

    def prep_gu(w):
        w = w.reshape(L, D, 2, hs)
        return jnp.pad(w, ((0, 0), (0, 0), (0, 0), (0, hp - hs))).reshape(L, D, 2 * hp).astype(bf16)

    def prep_down(w):
        return jnp.pad(w, ((0, 0), (0, hp - hs), (0, 0))).astype(bf16)

    sh_gu1, sh_d1, sh_gu2, sh_d2 = prep_gu(ffn1_w_gu), prep_down(ffn1_w_down), prep_gu(ffn2_w_gu), prep_down(ffn2_w_down)
    sh_ie, sh_oe = _even_to_mine(w_in_even).astype(bf16), w_out_even.astype(bf16)
    sh_io, sh_oo = w_in_odd.astype(bf16), w_out_odd.astype(bf16)

    def layer_shards(l):
        j = l // 2
        mix = [sh_ie[j], sh_oe[j]] if l % 2 == 0 else [sh_io[j], sh_oo[j]]
        return [sh_gu1[l], sh_d1[l], *mix, sh_gu2[l], sh_d2[l]]

    def layer_axes(l):
        return [1, 0, 0 if l % 2 == 0 else 1, 0, 1, 0]

    def layer_names(l):
        return ["ffn1_w_gu", "ffn1_w_down", *(["w_in_even", "w_out_even"] if l % 2 == 0 else ["w_in_odd", "w_out_odd"]),
                "ffn2_w_gu", "ffn2_w_down"]

    FOX_CARRIES, DN_CARRIES = (0, 1, 3), (2, 4, 5)

    def split_comm(kind, arrays, axes):
        return (Comm(kind, [arrays[i] for i in FOX_CARRIES], [axes[i] for i in FOX_CARRIES]),
                Comm(kind, [arrays[i] for i in DN_CARRIES], [axes[i] for i in DN_CARRIES]))

    def join_split(got_fox, got_dn):
        out = [None] * 6
        for i, a in zip(FOX_CARRIES, got_fox):
            out[i] = a
        for i, a in zip(DN_CARRIES, got_dn):
            out[i] = a
        return out

    sh0, ax0 = layer_shards(0), layer_axes(0)
    first = all_gather(sh0[:2] + [dn_conv_w[None]], ax0[:2] + [0], "ag_layer0", kind="ag2")
    weights = {0: first[:2] + [None] * 4}
    convw = jnp.moveaxis(first[2], 0, 2).reshape(n_even, CONV_WIDTH, -1)
    LAYER0_CARRIES = dict(f1_gu=(2,), f1_down=(3,), mx_in=(5,))

    EVEN_FWD_CARRIES = dict(f1_gu=(), mx_in=(), dn=(1, 2, 5), fox=(0, 4, 3), f2_gu=(), f2_down=())
    saved = []
    cur = xs
    for l in range(L):
        j = l // 2
        wgu1, wd1, win, wout, wgu2, wd2 = weights[l]
        nxt = l + 1 < L
        x0 = cur
        if l % 2 == 0:
            sh_n, ax_n = (layer_shards(l + 1), layer_axes(l + 1)) if nxt else (None, None)
            cm = {k: (Comm("ag2", [sh_n[i] for i in idx], [ax_n[i] for i in idx]) if nxt and idx else None)
                  for k, idx in EVEN_FWD_CARRIES.items()}
            if l == 0:
                cm.update({k: Comm("ag2", [sh0[i] for i in idx], [ax0[i] for i in idx]) for k, idx in LAYER0_CARRIES.items()})
                cm["fox"] = Comm("ag2", cm["fox"].arrays + [sh0[4]], cm["fox"].axes + [ax0[4]])
            x1, s1, got_f1gu, got_f1down = ffn_fwd(x0, norm_ffn1[l:l + 1], wgu1, wd1, f"l{l}f1", comm_gu=cm["f1_gu"],
                                                   comm_down=cm["f1_down"] if l == 0 else None)
            if l == 0:
                win, wout = got_f1gu[0], got_f1down[0]
            x2, sm, got_f, got_d, got_in = even_mixer_fwd(
                x1, norm_mix[l:l + 1], win, wout, convw[j], dn_a_log[j], dn_dt_bias[j], dn_norm_g[j:j + 1],
                fox_q_norm_g[j:j + 1], fox_k_norm_g[j:j + 1], fox_f_bias[j], f"l{l}mx", comm_fox=cm["fox"], comm_dn=cm["dn"],
                comm_in=cm["mx_in"])
            if l == 0:
                wd2, wgu2 = got_in[0], got_f[-1]
                weights[0] = [wgu1, wd1, win, wout, wgu2, wd2]
                got_f1gu, got_in, got_f = [], [], got_f[:-1]
            x3, s2, got_f2gu, got_f2down = ffn_fwd(x2, norm_ffn2[l:l + 1], wgu2, wd2, f"l{l}f2", comm_gu=cm["f2_gu"],
                                                   comm_down=cm["f2_down"])
            if nxt:
                got = dict(f1_gu=got_f1gu, mx_in=got_in, dn=got_d, fox=got_f, f2_gu=got_f2gu, f2_down=got_f2down)
                weights[l + 1] = [None] * 6
                for k, idx in EVEN_FWD_CARRIES.items():
                    for i, a in zip(idx, got[k]):
                        weights[l + 1][i] = a
        else:
            x1, s1, _, _ = ffn_fwd(x0, norm_ffn1[l:l + 1], wgu1, wd1, f"l{l}f1")
            cm = Comm("ag2", layer_shards(l + 1), layer_axes(l + 1)) if nxt else None
            x2, sm, got = odd_mixer_fwd(x1, norm_mix[l:l + 1], win, wout, f"l{l}mx", comm=cm)
            if nxt:
                weights[l + 1] = got
            x3, s2, _, _ = ffn_fwd(x2, norm_ffn2[l:l + 1], wgu2, wd2, f"l{l}f2")
        saved.append((x0, x1, x2, s1, sm, s2))
        cur = x3
    dy, loss_row = loss_head(cur, tgt, "loss")

    gsmall = {n: [None] * W[n].shape[0] for n in SMALL_NAMES}
    gconv = [None] * n_even
    parts = {n: [None] * W[n].shape[0] for n in BIG_NAMES if n != "dn_conv_w"}

    def take(l, recv):
        for nm, p in zip(layer_names(l), recv):
            idx = l if nm.startswith("ffn") else l // 2
            parts[nm][idx] = sum_parts(p.reshape(NDEV, -1, p.shape[-1]), f"sum_{nm}_{idx}").reshape(p.shape[1:])

    pending = None
    for l in reversed(range(L)):
        j = l // 2
        wgu1, wd1, win, wout, wgu2, wd2 = weights[l]
        x0, x1, x2, s1, sm, s2 = saved[l]
        dy, dg, dwgu2, dwd2 = ffn_bwd(x2, norm_ffn2[l:l + 1], wgu2, wd2, s2, dy, f"l{l}f2")
        gsmall["norm_ffn2"][l] = dg[0]
        if l % 2 == 0:
            cf, cd = split_comm("rs", pending, layer_axes(l + 1)) if pending is not None else (None, None)
            cb = None
            if l == 0:
                cf = Comm("rs", cf.arrays + [dwd2], cf.axes + [ax0[5]])
                cb = Comm("rs", [dwgu2], [ax0[4]])
            dy, dwin, dwout, sg, got_f, got_d, got_b = even_mixer_bwd(
                x1, norm_mix[l:l + 1], win, wout, convw[j], dn_norm_g[j:j + 1], fox_q_norm_g[j:j + 1],
                fox_k_norm_g[j:j + 1], sm, dy, f"l{l}mx", comm_fox=cf, comm_dn=cd, comm_bdh=cb)
            if l == 0:
                got_wd2, got_wgu2, got_f = got_f[-1], got_b[0], got_f[:-1]
            if pending is not None:
                take(l + 1, join_split(got_f, got_d))
            gconv[j] = sg.pop("dn_conv_w")
            for k_, v_ in sg.items():
                gsmall[k_][l if k_ == "norm_mix" else j] = v_
        else:
            cm = Comm("rs", pending, layer_axes(l + 1)) if pending is not None else None
            dy, dwin, dwout, sg, got = odd_mixer_bwd(x1, norm_mix[l:l + 1], win, wout, sm, dy, f"l{l}mx", comm=cm)
            if pending is not None:
                take(l + 1, got)
            gsmall["norm_mix"][l] = sg["norm_mix"]
        if l > 0:
            dy, dg, dwgu1, dwd1 = ffn_bwd(x0, norm_ffn1[l:l + 1], wgu1, wd1, s1, dy, f"l{l}f1")
        else:
            rs = lambda a, ax: Comm("rs", [a], [ax])
            dy, dg, dwgu1, dwd1, got0 = ffn_bwd(x0, norm_ffn1[l:l + 1], wgu1, wd1, s1, dy, f"l{l}f1",
                                                comms=dict(bda=rs(dwin, ax0[2]), bwd=rs(dwout, ax0[3])), own_axes=(ax0[0], ax0[1]))
        gsmall["norm_ffn1"][l] = dg[0]
        pending = [dwgu1, dwd1, dwin, dwout, dwgu2, dwd2]
    take(0, [got0["wgu"][0], got0["wd"][0], got0["bda"][0], got0["bwd"][0], got_wgu2, got_wd2])
    grad_x = dy[0][None]

    small_list = [jnp.stack(gsmall[n]) for n in SMALL_NAMES] + [jnp.stack(gconv), loss_row[0, :1]]
    packed = _pack_small(small_list)
    gathered = all_gather([packed], [0], "ag_small")[0]
    summed = sum_parts(gathered.reshape(NDEV, packed.shape[0], LANES), "sum_small")
    small_sum = _unpack_small(summed, small_list)
    G = dict(zip(SMALL_NAMES, small_sum[:len(SMALL_NAMES)]))
    conv_full = small_sum[len(SMALL_NAMES)]
    cwid = dn_conv_w.shape[2]
    G["dn_conv_w"] = lax.dynamic_slice_in_dim(conv_full, me * cwid, cwid, axis=2)
    loss = small_sum[-1][0]

    def unprep_gu(g):
        return g.reshape(L, D, 2, hp)[..., :hs].reshape(L, D, 2 * hs)

    G["ffn1_w_gu"] = unprep_gu(jnp.stack(parts["ffn1_w_gu"]))
    G["ffn2_w_gu"] = unprep_gu(jnp.stack(parts["ffn2_w_gu"]))
    G["ffn1_w_down"] = jnp.stack(parts["ffn1_w_down"])[:, :hs]
    G["ffn2_w_down"] = jnp.stack(parts["ffn2_w_down"])[:, :hs]
    G["w_in_even"] = _even_from_mine(jnp.stack(parts["w_in_even"]), D)
    G["w_out_even"] = jnp.stack(parts["w_out_even"])
    G["w_in_odd"] = jnp.stack(parts["w_in_odd"])
    G["w_out_odd"] = jnp.stack(parts["w_out_odd"])

    delta, new_m, new_v = {}, {}, {}
    for n in BIG_NAMES:
        t = (lambda a: jnp.swapaxes(a, 1, 2)) if n.endswith("w_gu") else (lambda a: a)
        delta[n], new_m[n], new_v[n] = map(t, adamw(t(W[n]), t(G[n]), t(M[n]), t(V[n]), f"adamw_{n}"))
    sw = [W[n] for n in SMALL_NAMES]
    d_, m_, v_ = adamw(_pack_small(sw)[None], _pack_small([G[n] for n in SMALL_NAMES])[None],
                       _pack_small([M[n] for n in SMALL_NAMES])[None], _pack_small([V[n] for n in SMALL_NAMES])[None], "adamw_small")
    for n, a, b, c_ in zip(SMALL_NAMES, _unpack_small(d_, sw), _unpack_small(m_, sw), _unpack_small(v_, sw)):
        delta[n], new_m[n], new_v[n] = a, b, c_

    return (loss, grad_x, *[G[n] for n in WEIGHT_NAMES], *[delta[n] for n in WEIGHT_NAMES],
            *[new_m[n] for n in WEIGHT_NAMES], *[new_v[n] for n in WEIGHT_NAMES])
```
